```python
import math
import jax, jax.numpy as jnp
from jax import lax
import numpy as np

D_MODEL = 2048
BATCH = 8
SEQ = 2048
DEPTH = 4

MEM_LEN = 256
A_HEAD_DIM = 128
A_HEADS = D_MODEL // 256
A_KV_HEADS = A_HEADS // 4
A_GROUP = A_HEADS // A_KV_HEADS
WINDOW = 128
BLOCK = 128
N_BUCKETS = 32
MAX_DISTANCE = 128
B_HEADS = D_MODEL // 512
B_KEY_DIM = 64
B_VAL_DIM = 128
GATE_RANK = 16
GATE_TAU = 16.0
GLA_CHUNK = 16
C_WIDTH = D_MODEL // 4
C_BLOCKS = 4
C_BLOCK_DIM = C_WIDTH // C_BLOCKS
CONV_WIDTH = 4
CONV_LEFT = 2
LRU_C = 8.0
X_HEADS = 4
X_HEAD_DIM = D_MODEL // X_HEADS
D_FF = 4 * D_MODEL
EPS = 1e-6
NEG_INF = -1e30

A_Q = A_HEADS * A_HEAD_DIM
A_KV = A_KV_HEADS * A_HEAD_DIM
B_QK = B_HEADS * B_KEY_DIM
B_V = B_HEADS * B_VAL_DIM
SPLIT_SIZES = (A_Q, A_KV, A_KV, B_QK, B_QK, B_V, B_V, GATE_RANK, GATE_RANK, C_WIDTH, C_WIDTH)
D_IN = sum(SPLIT_SIZES)
D_MIX = A_Q + B_V + C_WIDTH

kernel_name = 'hymba_style_parallel_hybrid_encoder'


def rmsnorm(x, g):
    xf = x.astype(jnp.float32)
    y = xf * lax.rsqrt(jnp.mean(jnp.square(xf), axis=-1, keepdims=True) + EPS)
    return (y * g.astype(jnp.float32)).astype(x.dtype)


def t5_bucket(rel):
    nb = N_BUCKETS // 2
    max_exact = nb // 2
    ret = jnp.where(rel > 0, nb, 0)
    n = jnp.abs(rel)
    nf = jnp.maximum(n, 1).astype(jnp.float32)
    large = max_exact + (jnp.log(nf / max_exact) / math.log(MAX_DISTANCE / max_exact)
                         * (nb - max_exact)).astype(jnp.int32)
    large = jnp.minimum(large, nb - 1)
    return ret + jnp.where(n < max_exact, n, large)


def windowed_gqa(q, k, v, rel_table, sink):
    bsz, seq = q.shape[:2]
    nblk = seq // BLOCK
    qb = q.reshape(bsz, nblk, BLOCK, A_KV_HEADS, A_GROUP, A_HEAD_DIM)

    def band(t):
        tp = jnp.pad(t, ((0, 0), (BLOCK, BLOCK), (0, 0), (0, 0)))
        tp = tp.reshape(bsz, nblk + 2, BLOCK, A_KV_HEADS, A_HEAD_DIM)
        return jnp.concatenate([tp[:, :-2], tp[:, 1:-1], tp[:, 2:]], axis=2)

    kb, vb = band(k), band(v)
    s = jnp.einsum('bnqhgd,bnkhd->bnhgqk', qb, kb).astype(jnp.float32) * (A_HEAD_DIM ** -0.5)
    qi = jnp.arange(BLOCK)[:, None]
    kj = jnp.arange(3 * BLOCK)[None, :]
    rel = kj - BLOCK - qi
    bias = rel_table.astype(jnp.float32)[t5_bucket(rel)]
    bias = jnp.transpose(bias, (2, 0, 1)).reshape(A_KV_HEADS, A_GROUP, BLOCK, 3 * BLOCK)
    kpos = jnp.arange(nblk)[:, None] * BLOCK + jnp.arange(3 * BLOCK)[None, :] - BLOCK
    valid = (jnp.abs(rel) <= WINDOW)[None] & ((kpos >= 0) & (kpos < seq))[:, None, :]
    s = jnp.where(valid[None, :, None, None], s + bias, NEG_INF)
    sink_col = jnp.broadcast_to(sink.astype(jnp.float32).reshape(A_KV_HEADS, A_GROUP, 1, 1),
                                s.shape[:-1] + (1,))
    p = jax.nn.softmax(jnp.concatenate([s, sink_col], axis=-1), axis=-1)[..., :-1]
    o = jnp.einsum('bnhgqk,bnkhd->bnqhgd', p.astype(v.dtype), vb)
    return o.reshape(bsz, seq, A_Q)


def gla_chunked(q, k, v, log_a, inclusive):
    bsz, nh, seq, dk = q.shape
    dv = v.shape[-1]
    nc = seq // GLA_CHUNK
    q = q.reshape(bsz, nh, nc, GLA_CHUNK, dk)
    k = k.reshape(bsz, nh, nc, GLA_CHUNK, dk)
    log_a = log_a.reshape(bsz, nh, nc, GLA_CHUNK, dk)
    v = v.reshape(bsz, nh, nc, GLA_CHUNK, dv)
    b = jnp.cumsum(log_a, axis=3)
    idx = jnp.arange(GLA_CHUNK)
    mask = (idx[:, None] >= idx[None, :]) if inclusive else (idx[:, None] > idx[None, :])
    m3 = mask[:, :, None]
    diff = b[:, :, :, :, None, :] - b[:, :, :, None, :, :]
    decay = jnp.where(m3, jnp.exp(jnp.where(m3, diff, 0.0)), 0.0)
    attn = jnp.einsum('bhnid,bhnjd,bhnijd->bhnij', q, k, decay)
    o = jnp.einsum('bhnij,bhnjd->bhnid', attn, v)
    b_last = b[:, :, :, -1, :]
    u = jnp.einsum('bhncd,bhnce->bhnde', k * jnp.exp(b_last[:, :, :, None, :] - b), v)

    def step(state, xs):
        dec, un = xs
        return dec[..., None] * state + un, state

    s0 = jnp.zeros((bsz, nh, dk, dv), jnp.float32)
    _, s_prev = lax.scan(step, s0, (jnp.moveaxis(jnp.exp(b_last), 2, 0), jnp.moveaxis(u, 2, 0)))
    s_prev = jnp.moveaxis(s_prev, 0, 2)
    o = o + jnp.einsum('bhncd,bhnde->bhnce', q * jnp.exp(b), s_prev)
    return o.reshape(bsz, nh, seq, dv)


def gla_mixer(q, k, v, g, zf, zb, w2f, b2f, w2b, b2b, gn):
    bsz, seq = q.shape[:2]
    f32 = jnp.float32

    def heads(t, d):
        return t.astype(f32).reshape(bsz, seq, B_HEADS, d).transpose(0, 2, 1, 3)

    qh = heads(q, B_KEY_DIM) * (B_KEY_DIM ** -0.5)
    kh = heads(k, B_KEY_DIM)
    vh = heads(v, B_VAL_DIM)
    la_f = heads(jax.nn.log_sigmoid(zf.astype(f32) @ w2f.astype(f32) + b2f.astype(f32)) / GATE_TAU, B_KEY_DIM)
    la_b = heads(jax.nn.log_sigmoid(zb.astype(f32) @ w2b.astype(f32) + b2b.astype(f32)) / GATE_TAU, B_KEY_DIM)
    o_f = gla_chunked(qh, kh, vh, la_f, True)
    flip = lambda t: jnp.flip(t, axis=2)
    o_b = flip(gla_chunked(flip(qh), flip(kh), flip(vh), flip(la_b), False))
    o = o_f + o_b
    o = o * lax.rsqrt(jnp.mean(jnp.square(o), axis=-1, keepdims=True) + EPS)
    o = o.transpose(0, 2, 1, 3).reshape(bsz, seq, B_V) * gn.astype(f32)
    return (o * jax.nn.silu(g.astype(f32))).astype(q.dtype)


def linear_scan(a, u, reverse):
    def combine(l, r):
        return (l[0] * r[0], r[0] * l[1] + r[1])
    _, h = lax.associative_scan(combine, (a, u), reverse=reverse, axis=1)
    return h


def rglru_mixer(xc, y, conv_w, conv_b, w_a, b_a, w_x, b_x, lam):
    bsz, seq, _ = xc.shape
    f32 = jnp.float32
    xp = jnp.pad(xc, ((0, 0), (CONV_LEFT, CONV_WIDTH - 1 - CONV_LEFT), (0, 0)))
    xconv = sum(xp[:, j:j + seq] * conv_w[j] for j in range(CONV_WIDTH)) + conv_b
    xf = xconv.astype(f32)
    xblk = xf.reshape(bsz, seq, C_BLOCKS, C_BLOCK_DIM)
    r = jax.nn.sigmoid(jnp.einsum('btgi,sgij->sbtgj', xblk, w_a.astype(f32)).reshape(2, bsz, seq, C_WIDTH)
                       + b_a.astype(f32)[:, None, None])
    i = jax.nn.sigmoid(jnp.einsum('btgi,sgij->sbtgj', xblk, w_x.astype(f32)).reshape(2, bsz, seq, C_WIDTH)
                       + b_x.astype(f32)[:, None, None])
    log_a = -LRU_C * r * jax.nn.softplus(-lam.astype(f32))[:, None, None]
    a = jnp.exp(log_a)
    u = jnp.sqrt(-jnp.expm1(2.0 * log_a)) * (i * xf[None])
    h = linear_scan(a[0], u[0], False) + linear_scan(a[1], u[1], True)
    return (h * jax.nn.gelu(y.astype(f32))).astype(xc.dtype)


def cross_attention(xn, memn, wq, wk, wv, wo):
    bsz, seq, _ = xn.shape
    mlen = memn.shape[1]
    q = (xn @ wq).reshape(bsz, seq, X_HEADS, X_HEAD_DIM)
    k = (memn @ wk).reshape(bsz, mlen, X_HEADS, X_HEAD_DIM)
    v = (memn @ wv).reshape(bsz, mlen, X_HEADS, X_HEAD_DIM)
    s = jnp.einsum('bthd,bmhd->bhtm', q, k).astype(jnp.float32) * (X_HEAD_DIM ** -0.5)
    p = jax.nn.softmax(s, axis=-1)
    o = jnp.einsum('bhtm,bmhd->bthd', p.astype(v.dtype), v).reshape(bsz, seq, D_MODEL)
    return o @ wo


def _fwd_setup_inputs(seed: int = 0) -> dict:
    key = jax.random.key(seed)
    ks = jax.random.split(key, 32)
    f32 = jnp.float32
    L = DEPTH

    def nrm(k, shape, scale):
        return jax.random.normal(k, shape, f32) * scale

    def gain(k, shape):
        return 1.0 + 0.05 * jax.random.normal(k, shape, f32)

    a_init = jax.random.uniform(ks[17], (L, 2, C_WIDTH), f32, 0.9, 0.999) ** (1.0 / LRU_C)
    lru_lambda = jnp.log(a_init) - jnp.log1p(-a_init)
    return {
        'x': nrm(ks[0], (BATCH, SEQ, D_MODEL), 1.0),
        'mem': nrm(ks[1], (BATCH, MEM_LEN, D_MODEL), 1.0),
        'rel_bias': nrm(ks[2], (N_BUCKETS, A_HEADS), 0.5),
        'w_in': nrm(ks[3], (L, D_MODEL, D_IN), D_MODEL ** -0.5),
        'w_out': nrm(ks[4], (L, D_MIX, D_MODEL), D_MIX ** -0.5),
        'attn_sink': nrm(ks[5], (L, A_HEADS), 0.5),
        'gla_w2_f': nrm(ks[6], (L, GATE_RANK, B_QK), GATE_RANK ** -0.5),
        'gla_b2_f': nrm(ks[7], (L, B_QK), 0.1),
        'gla_w2_b': nrm(ks[8], (L, GATE_RANK, B_QK), GATE_RANK ** -0.5),
        'gla_b2_b': nrm(ks[9], (L, B_QK), 0.1),
        'gla_norm': gain(ks[10], (L, B_V)),
        'conv_w': nrm(ks[11], (L, CONV_WIDTH, C_WIDTH), CONV_WIDTH ** -0.5),
        'conv_b': nrm(ks[12], (L, C_WIDTH), 0.02),
        'lru_wa': nrm(ks[13], (L, 2, C_BLOCKS, C_BLOCK_DIM, C_BLOCK_DIM), C_BLOCK_DIM ** -0.5),
        'lru_ba': nrm(ks[14], (L, 2, C_WIDTH), 0.1),
        'lru_wx': nrm(ks[15], (L, 2, C_BLOCKS, C_BLOCK_DIM, C_BLOCK_DIM), C_BLOCK_DIM ** -0.5),
        'lru_bx': nrm(ks[16], (L, 2, C_WIDTH), 0.1),
        'lru_lambda': lru_lambda,
        'xq': nrm(ks[18], (L, D_MODEL, D_MODEL), D_MODEL ** -0.5),
        'xk': nrm(ks[19], (L, D_MODEL, D_MODEL), D_MODEL ** -0.5),
        'xv': nrm(ks[20], (L, D_MODEL, D_MODEL), D_MODEL ** -0.5),
        'xo': nrm(ks[21], (L, D_MODEL, D_MODEL), D_MODEL ** -0.5),
        'w_up': nrm(ks[22], (L, D_MODEL, D_FF), D_MODEL ** -0.5),
        'w_down': nrm(ks[23], (L, D_FF, D_MODEL), D_FF ** -0.5),
        'norm_mix_pre': gain(ks[24], (L, D_MODEL)),
        'norm_mix_post': gain(ks[25], (L, D_MODEL)),
        'norm_mem': gain(ks[26], (L, D_MODEL)),
        'norm_x_pre': gain(ks[27], (L, D_MODEL)),
        'norm_x_post': gain(ks[28], (L, D_MODEL)),
        'norm_ff_pre': gain(ks[29], (L, D_MODEL)),
        'norm_ff_post': gain(ks[30], (L, D_MODEL)),
    }


def _fwd_reference(x, mem, rel_bias, w_in, w_out, attn_sink, gla_w2_f, gla_b2_f, gla_w2_b, gla_b2_b,
              gla_norm, conv_w, conv_b, lru_wa, lru_ba, lru_wx, lru_bx, lru_lambda,
              xq, xk, xv, xo, w_up, w_down, norm_mix_pre, norm_mix_post, norm_mem,
              norm_x_pre, norm_x_post, norm_ff_pre, norm_ff_post):
    bsz, seq, _ = x.shape
    offsets = np.cumsum(SPLIT_SIZES)[:-1].tolist()
    for l in range(DEPTH):
        h = rmsnorm(x, norm_mix_pre[l])
        aq, ak, av, bq, bk, bv, bg, zf, zb, cx, cy = jnp.split(h @ w_in[l], offsets, axis=-1)
        oa = windowed_gqa(aq.reshape(bsz, seq, A_HEADS, A_HEAD_DIM),
                          ak.reshape(bsz, seq, A_KV_HEADS, A_HEAD_DIM),
                          av.reshape(bsz, seq, A_KV_HEADS, A_HEAD_DIM),
                          rel_bias, attn_sink[l])
        ob = gla_mixer(bq, bk, bv, bg, zf, zb, gla_w2_f[l], gla_b2_f[l], gla_w2_b[l], gla_b2_b[l], gla_norm[l])
        oc = rglru_mixer(cx, cy, conv_w[l], conv_b[l], lru_wa[l], lru_ba[l], lru_wx[l], lru_bx[l], lru_lambda[l])
        mixed = jnp.concatenate([oa, ob.astype(oa.dtype), oc.astype(oa.dtype)], axis=-1) @ w_out[l]
        x = x + rmsnorm(mixed, norm_mix_post[l])
        h = rmsnorm(x, norm_x_pre[l])
        memn = rmsnorm(mem, norm_mem[l])
        x = x + rmsnorm(cross_attention(h, memn, xq[l], xk[l], xv[l], xo[l]), norm_x_post[l])
        h = rmsnorm(x, norm_ff_pre[l])
        ff = jnp.square(jax.nn.relu(h @ w_up[l])) @ w_down[l]
        x = x + rmsnorm(ff, norm_ff_post[l])
    return x


import jax as _jax
import jax.numpy as _jnp

TWIN_FORMAT = 'train_step'
FWD_PARAMS = ['x', 'mem', 'rel_bias', 'w_in', 'w_out', 'attn_sink', 'gla_w2_f', 'gla_b2_f', 'gla_w2_b', 'gla_b2_b', 'gla_norm', 'conv_w', 'conv_b', 'lru_wa', 'lru_ba', 'lru_wx', 'lru_bx', 'lru_lambda', 'xq', 'xk', 'xv', 'xo', 'w_up', 'w_down', 'norm_mix_pre', 'norm_mix_post', 'norm_mem', 'norm_x_pre', 'norm_x_post', 'norm_ff_pre', 'norm_ff_post']
TWIN_WEIGHTS = ['rel_bias', 'w_in', 'w_out', 'attn_sink', 'gla_w2_f', 'gla_b2_f', 'gla_w2_b', 'gla_b2_b', 'gla_norm', 'conv_w', 'conv_b', 'lru_wa', 'lru_ba', 'lru_wx', 'lru_bx', 'lru_lambda', 'xq', 'xk', 'xv', 'xo', 'w_up', 'w_down', 'norm_mix_pre', 'norm_mix_post', 'norm_mem', 'norm_x_pre', 'norm_x_post', 'norm_ff_pre', 'norm_ff_post']
TWIN_DIFF_INPUT = 'x'
TWIN_INPUTS = ['x', 'mem', 'rel_bias', 'w_in', 'w_out', 'attn_sink', 'gla_w2_f', 'gla_b2_f', 'gla_w2_b', 'gla_b2_b', 'gla_norm', 'conv_w', 'conv_b', 'lru_wa', 'lru_ba', 'lru_wx', 'lru_bx', 'lru_lambda', 'xq', 'xk', 'xv', 'xo', 'w_up', 'w_down', 'norm_mix_pre', 'norm_mix_post', 'norm_mem', 'norm_x_pre', 'norm_x_post', 'norm_ff_pre', 'norm_ff_post', 'loss_target', 'm_rel_bias', 'm_w_in', 'm_w_out', 'm_attn_sink', 'm_gla_w2_f', 'm_gla_b2_f', 'm_gla_w2_b', 'm_gla_b2_b', 'm_gla_norm', 'm_conv_w', 'm_conv_b', 'm_lru_wa', 'm_lru_ba', 'm_lru_wx', 'm_lru_bx', 'm_lru_lambda', 'm_xq', 'm_xk', 'm_xv', 'm_xo', 'm_w_up', 'm_w_down', 'm_norm_mix_pre', 'm_norm_mix_post', 'm_norm_mem', 'm_norm_x_pre', 'm_norm_x_post', 'm_norm_ff_pre', 'm_norm_ff_post', 'v_rel_bias', 'v_w_in', 'v_w_out', 'v_attn_sink', 'v_gla_w2_f', 'v_gla_b2_f', 'v_gla_w2_b', 'v_gla_b2_b', 'v_gla_norm', 'v_conv_w', 'v_conv_b', 'v_lru_wa', 'v_lru_ba', 'v_lru_wx', 'v_lru_bx', 'v_lru_lambda', 'v_xq', 'v_xk', 'v_xv', 'v_xo', 'v_w_up', 'v_w_down', 'v_norm_mix_pre', 'v_norm_mix_post', 'v_norm_mem', 'v_norm_x_pre', 'v_norm_x_post', 'v_norm_ff_pre', 'v_norm_ff_post']
TWIN_OUTPUTS = ['loss', 'grad_x', 'grad_rel_bias', 'grad_w_in', 'grad_w_out', 'grad_attn_sink', 'grad_gla_w2_f', 'grad_gla_b2_f', 'grad_gla_w2_b', 'grad_gla_b2_b', 'grad_gla_norm', 'grad_conv_w', 'grad_conv_b', 'grad_lru_wa', 'grad_lru_ba', 'grad_lru_wx', 'grad_lru_bx', 'grad_lru_lambda', 'grad_xq', 'grad_xk', 'grad_xv', 'grad_xo', 'grad_w_up', 'grad_w_down', 'grad_norm_mix_pre', 'grad_norm_mix_post', 'grad_norm_mem', 'grad_norm_x_pre', 'grad_norm_x_post', 'grad_norm_ff_pre', 'grad_norm_ff_post', 'delta_rel_bias', 'delta_w_in', 'delta_w_out', 'delta_attn_sink', 'delta_gla_w2_f', 'delta_gla_b2_f', 'delta_gla_w2_b', 'delta_gla_b2_b', 'delta_gla_norm', 'delta_conv_w', 'delta_conv_b', 'delta_lru_wa', 'delta_lru_ba', 'delta_lru_wx', 'delta_lru_bx', 'delta_lru_lambda', 'delta_xq', 'delta_xk', 'delta_xv', 'delta_xo', 'delta_w_up', 'delta_w_down', 'delta_norm_mix_pre', 'delta_norm_mix_post', 'delta_norm_mem', 'delta_norm_x_pre', 'delta_norm_x_post', 'delta_norm_ff_pre', 'delta_norm_ff_post', 'new_m_rel_bias', 'new_m_w_in', 'new_m_w_out', 'new_m_attn_sink', 'new_m_gla_w2_f', 'new_m_gla_b2_f', 'new_m_gla_w2_b', 'new_m_gla_b2_b', 'new_m_gla_norm', 'new_m_conv_w', 'new_m_conv_b', 'new_m_lru_wa', 'new_m_lru_ba', 'new_m_lru_wx', 'new_m_lru_bx', 'new_m_lru_lambda', 'new_m_xq', 'new_m_xk', 'new_m_xv', 'new_m_xo', 'new_m_w_up', 'new_m_w_down', 'new_m_norm_mix_pre', 'new_m_norm_mix_post', 'new_m_norm_mem', 'new_m_norm_x_pre', 'new_m_norm_x_post', 'new_m_norm_ff_pre', 'new_m_norm_ff_post', 'new_v_rel_bias', 'new_v_w_in', 'new_v_w_out', 'new_v_attn_sink', 'new_v_gla_w2_f', 'new_v_gla_b2_f', 'new_v_gla_w2_b', 'new_v_gla_b2_b', 'new_v_gla_norm', 'new_v_conv_w', 'new_v_conv_b', 'new_v_lru_wa', 'new_v_lru_ba', 'new_v_lru_wx', 'new_v_lru_bx', 'new_v_lru_lambda', 'new_v_xq', 'new_v_xk', 'new_v_xv', 'new_v_xo', 'new_v_w_up', 'new_v_w_down', 'new_v_norm_mix_pre', 'new_v_norm_mix_post', 'new_v_norm_mem', 'new_v_norm_x_pre', 'new_v_norm_x_post', 'new_v_norm_ff_pre', 'new_v_norm_ff_post']
TWIN_LEAF_KINDS = {'loss': 'loss', 'grad_x': 'grad_x', 'grad_rel_bias': 'grad_w', 'grad_w_in': 'grad_w', 'grad_w_out': 'grad_w', 'grad_attn_sink': 'grad_w', 'grad_gla_w2_f': 'grad_w', 'grad_gla_b2_f': 'grad_w', 'grad_gla_w2_b': 'grad_w', 'grad_gla_b2_b': 'grad_w', 'grad_gla_norm': 'grad_w', 'grad_conv_w': 'grad_w', 'grad_conv_b': 'grad_w', 'grad_lru_wa': 'grad_w', 'grad_lru_ba': 'grad_w', 'grad_lru_wx': 'grad_w', 'grad_lru_bx': 'grad_w', 'grad_lru_lambda': 'grad_w', 'grad_xq': 'grad_w', 'grad_xk': 'grad_w', 'grad_xv': 'grad_w', 'grad_xo': 'grad_w', 'grad_w_up': 'grad_w', 'grad_w_down': 'grad_w', 'grad_norm_mix_pre': 'grad_w', 'grad_norm_mix_post': 'grad_w', 'grad_norm_mem': 'grad_w', 'grad_norm_x_pre': 'grad_w', 'grad_norm_x_post': 'grad_w', 'grad_norm_ff_pre': 'grad_w', 'grad_norm_ff_post': 'grad_w', 'delta_rel_bias': 'delta_w', 'delta_w_in': 'delta_w', 'delta_w_out': 'delta_w', 'delta_attn_sink': 'delta_w', 'delta_gla_w2_f': 'delta_w', 'delta_gla_b2_f': 'delta_w', 'delta_gla_w2_b': 'delta_w', 'delta_gla_b2_b': 'delta_w', 'delta_gla_norm': 'delta_w', 'delta_conv_w': 'delta_w', 'delta_conv_b': 'delta_w', 'delta_lru_wa': 'delta_w', 'delta_lru_ba': 'delta_w', 'delta_lru_wx': 'delta_w', 'delta_lru_bx': 'delta_w', 'delta_lru_lambda': 'delta_w', 'delta_xq': 'delta_w', 'delta_xk': 'delta_w', 'delta_xv': 'delta_w', 'delta_xo': 'delta_w', 'delta_w_up': 'delta_w', 'delta_w_down': 'delta_w', 'delta_norm_mix_pre': 'delta_w', 'delta_norm_mix_post': 'delta_w', 'delta_norm_mem': 'delta_w', 'delta_norm_x_pre': 'delta_w', 'delta_norm_x_post': 'delta_w', 'delta_norm_ff_pre': 'delta_w', 'delta_norm_ff_post': 'delta_w', 'new_m_rel_bias': 'new_m', 'new_m_w_in': 'new_m', 'new_m_w_out': 'new_m', 'new_m_attn_sink': 'new_m', 'new_m_gla_w2_f': 'new_m', 'new_m_gla_b2_f': 'new_m', 'new_m_gla_w2_b': 'new_m', 'new_m_gla_b2_b': 'new_m', 'new_m_gla_norm': 'new_m', 'new_m_conv_w': 'new_m', 'new_m_conv_b': 'new_m', 'new_m_lru_wa': 'new_m', 'new_m_lru_ba': 'new_m', 'new_m_lru_wx': 'new_m', 'new_m_lru_bx': 'new_m', 'new_m_lru_lambda': 'new_m', 'new_m_xq': 'new_m', 'new_m_xk': 'new_m', 'new_m_xv': 'new_m', 'new_m_xo': 'new_m', 'new_m_w_up': 'new_m', 'new_m_w_down': 'new_m', 'new_m_norm_mix_pre': 'new_m', 'new_m_norm_mix_post': 'new_m', 'new_m_norm_mem': 'new_m', 'new_m_norm_x_pre': 'new_m', 'new_m_norm_x_post': 'new_m', 'new_m_norm_ff_pre': 'new_m', 'new_m_norm_ff_post': 'new_m', 'new_v_rel_bias': 'new_v', 'new_v_w_in': 'new_v', 'new_v_w_out': 'new_v', 'new_v_attn_sink': 'new_v', 'new_v_gla_w2_f': 'new_v', 'new_v_gla_b2_f': 'new_v', 'new_v_gla_w2_b': 'new_v', 'new_v_gla_b2_b': 'new_v', 'new_v_gla_norm': 'new_v', 'new_v_conv_w': 'new_v', 'new_v_conv_b': 'new_v', 'new_v_lru_wa': 'new_v', 'new_v_lru_ba': 'new_v', 'new_v_lru_wx': 'new_v', 'new_v_lru_bx': 'new_v', 'new_v_lru_lambda': 'new_v', 'new_v_xq': 'new_v', 'new_v_xk': 'new_v', 'new_v_xv': 'new_v', 'new_v_xo': 'new_v', 'new_v_w_up': 'new_v', 'new_v_w_down': 'new_v', 'new_v_norm_mix_pre': 'new_v', 'new_v_norm_mix_post': 'new_v', 'new_v_norm_mem': 'new_v', 'new_v_norm_x_pre': 'new_v', 'new_v_norm_x_post': 'new_v', 'new_v_norm_ff_pre': 'new_v', 'new_v_norm_ff_post': 'new_v'}


def _forward(args):
    return _fwd_reference(*[args[k] for k in FWD_PARAMS])


def _output_shape():
    out = _jax.eval_shape(lambda: _forward(_fwd_setup_inputs(0)))
    return out.shape, out.dtype

N_MICROBATCH = 1
ADAM_LR = 0.001
ADAM_B1 = 0.9
ADAM_B2 = 0.999
ADAM_EPS = 1e-08
ADAM_WD = 0.01
ADAM_STEP = 10
PER_EXAMPLE_BATCH_AXIS = {'x': 0, 'mem': 0, 'loss_target': 0}
SHARED_INPUTS = []
_WEIGHT_DTYPES = {'rel_bias': _jnp.float32, 'w_in': _jnp.float32, 'w_out': _jnp.float32, 'attn_sink': _jnp.float32, 'gla_w2_f': _jnp.float32, 'gla_b2_f': _jnp.float32, 'gla_w2_b': _jnp.float32, 'gla_b2_b': _jnp.float32, 'gla_norm': _jnp.float32, 'conv_w': _jnp.float32, 'conv_b': _jnp.float32, 'lru_wa': _jnp.float32, 'lru_ba': _jnp.float32, 'lru_wx': _jnp.float32, 'lru_bx': _jnp.float32, 'lru_lambda': _jnp.float32, 'xq': _jnp.float32, 'xk': _jnp.float32, 'xv': _jnp.float32, 'xo': _jnp.float32, 'w_up': _jnp.float32, 'w_down': _jnp.float32, 'norm_mix_pre': _jnp.float32, 'norm_mix_post': _jnp.float32, 'norm_mem': _jnp.float32, 'norm_x_pre': _jnp.float32, 'norm_x_post': _jnp.float32, 'norm_ff_pre': _jnp.float32, 'norm_ff_post': _jnp.float32}
MOMENT_SCALE = {'rel_bias': 5.026137e-01, 'w_in': 3.993564e+00, 'w_out': 4.925449e+00, 'attn_sink': 6.537025e-02, 'gla_w2_f': 8.484293e-02, 'gla_b2_f': 2.835004e-01, 'gla_w2_b': 9.229518e-02, 'gla_b2_b': 2.736141e-01, 'gla_norm': 7.901109e-01, 'conv_w': 8.866060e+00, 'conv_b': 5.842444e+01, 'lru_wa': 8.564560e-01, 'lru_ba': 6.936605e-01, 'lru_wx': 1.700755e+00, 'lru_bx': 1.814531e+00, 'lru_lambda': 1.505689e+00, 'xq': 2.591369e+00, 'xk': 2.609140e+00, 'xv': 7.099002e+00, 'xo': 7.123697e+00, 'w_up': 1.814918e+00, 'w_down': 6.367508e+00, 'norm_mix_pre': 5.642744e+00, 'norm_mix_post': 9.555511e+00, 'norm_mem': 7.605283e+00, 'norm_x_pre': 2.545884e+00, 'norm_x_post': 1.120672e+01, 'norm_ff_pre': 3.530773e+00, 'norm_ff_post': 1.052213e+01}


def _to_microbatches(a, axis):
    t = _jnp.moveaxis(a, axis, 0)
    t = t.reshape((N_MICROBATCH, t.shape[0] // N_MICROBATCH) + t.shape[1:])
    return _jnp.moveaxis(t, 1, axis + 1)


def setup_inputs(seed: int = 0) -> dict:
    inp = _fwd_setup_inputs(seed)
    key = _jax.random.fold_in(_jax.random.key(seed), 7919)
    shape, _ = _output_shape()
    out = dict(inp)
    out["loss_target"] = _jax.random.normal(_jax.random.fold_in(key, 0), shape, _jnp.float32)
    for i, name in enumerate(TWIN_WEIGHTS):
        w = inp[name].astype(_jnp.float32)
        if MOMENT_SCALE is None:
            s = _jnp.sqrt(_jnp.mean(_jnp.square(w)) + 1e-30)
        else:
            s = MOMENT_SCALE[name]
        km, kv = _jax.random.split(_jax.random.fold_in(key, i + 1))
        out[name] = w
        out["m_" + name] = s * _jax.random.normal(km, w.shape, _jnp.float32)
        out["v_" + name] = (s * s) * _jax.random.uniform(kv, w.shape, _jnp.float32, 0.5, 1.5)
    if N_MICROBATCH > 1:
        for name, axis in PER_EXAMPLE_BATCH_AXIS.items():
            out[name] = _to_microbatches(out[name], axis)
    return {'x': out['x'], 'mem': out['mem'], 'rel_bias': out['rel_bias'], 'w_in': out['w_in'], 'w_out': out['w_out'], 'attn_sink': out['attn_sink'], 'gla_w2_f': out['gla_w2_f'], 'gla_b2_f': out['gla_b2_f'], 'gla_w2_b': out['gla_w2_b'], 'gla_b2_b': out['gla_b2_b'], 'gla_norm': out['gla_norm'], 'conv_w': out['conv_w'], 'conv_b': out['conv_b'], 'lru_wa': out['lru_wa'], 'lru_ba': out['lru_ba'], 'lru_wx': out['lru_wx'], 'lru_bx': out['lru_bx'], 'lru_lambda': out['lru_lambda'], 'xq': out['xq'], 'xk': out['xk'], 'xv': out['xv'], 'xo': out['xo'], 'w_up': out['w_up'], 'w_down': out['w_down'], 'norm_mix_pre': out['norm_mix_pre'], 'norm_mix_post': out['norm_mix_post'], 'norm_mem': out['norm_mem'], 'norm_x_pre': out['norm_x_pre'], 'norm_x_post': out['norm_x_post'], 'norm_ff_pre': out['norm_ff_pre'], 'norm_ff_post': out['norm_ff_post'], 'loss_target': out['loss_target'], 'm_rel_bias': out['m_rel_bias'], 'm_w_in': out['m_w_in'], 'm_w_out': out['m_w_out'], 'm_attn_sink': out['m_attn_sink'], 'm_gla_w2_f': out['m_gla_w2_f'], 'm_gla_b2_f': out['m_gla_b2_f'], 'm_gla_w2_b': out['m_gla_w2_b'], 'm_gla_b2_b': out['m_gla_b2_b'], 'm_gla_norm': out['m_gla_norm'], 'm_conv_w': out['m_conv_w'], 'm_conv_b': out['m_conv_b'], 'm_lru_wa': out['m_lru_wa'], 'm_lru_ba': out['m_lru_ba'], 'm_lru_wx': out['m_lru_wx'], 'm_lru_bx': out['m_lru_bx'], 'm_lru_lambda': out['m_lru_lambda'], 'm_xq': out['m_xq'], 'm_xk': out['m_xk'], 'm_xv': out['m_xv'], 'm_xo': out['m_xo'], 'm_w_up': out['m_w_up'], 'm_w_down': out['m_w_down'], 'm_norm_mix_pre': out['m_norm_mix_pre'], 'm_norm_mix_post': out['m_norm_mix_post'], 'm_norm_mem': out['m_norm_mem'], 'm_norm_x_pre': out['m_norm_x_pre'], 'm_norm_x_post': out['m_norm_x_post'], 'm_norm_ff_pre': out['m_norm_ff_pre'], 'm_norm_ff_post': out['m_norm_ff_post'], 'v_rel_bias': out['v_rel_bias'], 'v_w_in': out['v_w_in'], 'v_w_out': out['v_w_out'], 'v_attn_sink': out['v_attn_sink'], 'v_gla_w2_f': out['v_gla_w2_f'], 'v_gla_b2_f': out['v_gla_b2_f'], 'v_gla_w2_b': out['v_gla_w2_b'], 'v_gla_b2_b': out['v_gla_b2_b'], 'v_gla_norm': out['v_gla_norm'], 'v_conv_w': out['v_conv_w'], 'v_conv_b': out['v_conv_b'], 'v_lru_wa': out['v_lru_wa'], 'v_lru_ba': out['v_lru_ba'], 'v_lru_wx': out['v_lru_wx'], 'v_lru_bx': out['v_lru_bx'], 'v_lru_lambda': out['v_lru_lambda'], 'v_xq': out['v_xq'], 'v_xk': out['v_xk'], 'v_xv': out['v_xv'], 'v_xo': out['v_xo'], 'v_w_up': out['v_w_up'], 'v_w_down': out['v_w_down'], 'v_norm_mix_pre': out['v_norm_mix_pre'], 'v_norm_mix_post': out['v_norm_mix_post'], 'v_norm_mem': out['v_norm_mem'], 'v_norm_x_pre': out['v_norm_x_pre'], 'v_norm_x_post': out['v_norm_x_post'], 'v_norm_ff_pre': out['v_norm_ff_pre'], 'v_norm_ff_post': out['v_norm_ff_post']}


def _loss(weights, diff, rest, loss_target):
    with _jax.named_scope("forward"):
        args = {**rest, TWIN_DIFF_INPUT: diff, **{k: w.astype(_WEIGHT_DTYPES[k]) for k, w in weights.items()}}
        y = _forward(args)
    with _jax.named_scope("loss_head"):
        err = _jnp.square(y.astype(_jnp.float32) - loss_target)
        return 0.5 * _jnp.sum(_jnp.mean(err, axis=-1)) if err.ndim else 0.5 * err


def _adamw(w, g, m, v):
    m = ADAM_B1 * m + (1.0 - ADAM_B1) * g
    v = ADAM_B2 * v + (1.0 - ADAM_B2) * _jnp.square(g)
    m_hat = m / (1.0 - ADAM_B1 ** ADAM_STEP)
    v_hat = v / (1.0 - ADAM_B2 ** ADAM_STEP)
    delta = -ADAM_LR * (m_hat / (_jnp.sqrt(v_hat) + ADAM_EPS) + ADAM_WD * w)
    return delta, m, v


def reference(x, mem, rel_bias, w_in, w_out, attn_sink, gla_w2_f, gla_b2_f, gla_w2_b, gla_b2_b, gla_norm, conv_w, conv_b, lru_wa, lru_ba, lru_wx, lru_bx, lru_lambda, xq, xk, xv, xo, w_up, w_down, norm_mix_pre, norm_mix_post, norm_mem, norm_x_pre, norm_x_post, norm_ff_pre, norm_ff_post, loss_target, m_rel_bias, m_w_in, m_w_out, m_attn_sink, m_gla_w2_f, m_gla_b2_f, m_gla_w2_b, m_gla_b2_b, m_gla_norm, m_conv_w, m_conv_b, m_lru_wa, m_lru_ba, m_lru_wx, m_lru_bx, m_lru_lambda, m_xq, m_xk, m_xv, m_xo, m_w_up, m_w_down, m_norm_mix_pre, m_norm_mix_post, m_norm_mem, m_norm_x_pre, m_norm_x_post, m_norm_ff_pre, m_norm_ff_post, v_rel_bias, v_w_in, v_w_out, v_attn_sink, v_gla_w2_f, v_gla_b2_f, v_gla_w2_b, v_gla_b2_b, v_gla_norm, v_conv_w, v_conv_b, v_lru_wa, v_lru_ba, v_lru_wx, v_lru_bx, v_lru_lambda, v_xq, v_xk, v_xv, v_xo, v_w_up, v_w_down, v_norm_mix_pre, v_norm_mix_post, v_norm_mem, v_norm_x_pre, v_norm_x_post, v_norm_ff_pre, v_norm_ff_post):
    given = dict(x=x, mem=mem, rel_bias=rel_bias, w_in=w_in, w_out=w_out, attn_sink=attn_sink, gla_w2_f=gla_w2_f, gla_b2_f=gla_b2_f, gla_w2_b=gla_w2_b, gla_b2_b=gla_b2_b, gla_norm=gla_norm, conv_w=conv_w, conv_b=conv_b, lru_wa=lru_wa, lru_ba=lru_ba, lru_wx=lru_wx, lru_bx=lru_bx, lru_lambda=lru_lambda, xq=xq, xk=xk, xv=xv, xo=xo, w_up=w_up, w_down=w_down, norm_mix_pre=norm_mix_pre, norm_mix_post=norm_mix_post, norm_mem=norm_mem, norm_x_pre=norm_x_pre, norm_x_post=norm_x_post, norm_ff_pre=norm_ff_pre, norm_ff_post=norm_ff_post, loss_target=loss_target, m_rel_bias=m_rel_bias, m_w_in=m_w_in, m_w_out=m_w_out, m_attn_sink=m_attn_sink, m_gla_w2_f=m_gla_w2_f, m_gla_b2_f=m_gla_b2_f, m_gla_w2_b=m_gla_w2_b, m_gla_b2_b=m_gla_b2_b, m_gla_norm=m_gla_norm, m_conv_w=m_conv_w, m_conv_b=m_conv_b, m_lru_wa=m_lru_wa, m_lru_ba=m_lru_ba, m_lru_wx=m_lru_wx, m_lru_bx=m_lru_bx, m_lru_lambda=m_lru_lambda, m_xq=m_xq, m_xk=m_xk, m_xv=m_xv, m_xo=m_xo, m_w_up=m_w_up, m_w_down=m_w_down, m_norm_mix_pre=m_norm_mix_pre, m_norm_mix_post=m_norm_mix_post, m_norm_mem=m_norm_mem, m_norm_x_pre=m_norm_x_pre, m_norm_x_post=m_norm_x_post, m_norm_ff_pre=m_norm_ff_pre, m_norm_ff_post=m_norm_ff_post, v_rel_bias=v_rel_bias, v_w_in=v_w_in, v_w_out=v_w_out, v_attn_sink=v_attn_sink, v_gla_w2_f=v_gla_w2_f, v_gla_b2_f=v_gla_b2_f, v_gla_w2_b=v_gla_w2_b, v_gla_b2_b=v_gla_b2_b, v_gla_norm=v_gla_norm, v_conv_w=v_conv_w, v_conv_b=v_conv_b, v_lru_wa=v_lru_wa, v_lru_ba=v_lru_ba, v_lru_wx=v_lru_wx, v_lru_bx=v_lru_bx, v_lru_lambda=v_lru_lambda, v_xq=v_xq, v_xk=v_xk, v_xv=v_xv, v_xo=v_xo, v_w_up=v_w_up, v_w_down=v_w_down, v_norm_mix_pre=v_norm_mix_pre, v_norm_mix_post=v_norm_mix_post, v_norm_mem=v_norm_mem, v_norm_x_pre=v_norm_x_pre, v_norm_x_post=v_norm_x_post, v_norm_ff_pre=v_norm_ff_pre, v_norm_ff_post=v_norm_ff_post)
    weights = {n: given[n] for n in TWIN_WEIGHTS}
    shared = {n: given[n] for n in SHARED_INPUTS}
    per_example = {n: given[n] for n in ['x', 'mem']}
    grad_fn = _jax.value_and_grad(_loss, argnums=(0, 1))

    def one_microbatch(ex, loss_target):
        ex = dict(ex)
        diff = ex.pop(TWIN_DIFF_INPUT)
        return grad_fn(weights, diff, {**shared, **ex}, loss_target)

    if N_MICROBATCH == 1:
        loss, (grad_w, grad_x) = one_microbatch(per_example, given["loss_target"])
    else:
        def body(carry, xs):
            loss_sum, grad_sum = carry
            l_k, (gw_k, gx_k) = one_microbatch(xs[0], xs[1])
            with _jax.named_scope("update"):
                return (loss_sum + l_k, _jax.tree.map(_jnp.add, grad_sum, gw_k)), gx_k

        init = (_jnp.zeros((), _jnp.float32), _jax.tree.map(_jnp.zeros_like, weights))
        (loss, grad_w), grad_x = _jax.lax.scan(body, init, (per_example, given["loss_target"]))
    with _jax.named_scope("update"):
        delta_w, new_m, new_v = {}, {}, {}
        for n in TWIN_WEIGHTS:
            delta_w[n], new_m[n], new_v[n] = _adamw(weights[n], grad_w[n], given["m_" + n], given["v_" + n])
    return (loss, grad_x, *[grad_w[n] for n in TWIN_WEIGHTS], *[delta_w[n] for n in TWIN_WEIGHTS],
            *[new_m[n] for n in TWIN_WEIGHTS], *[new_v[n] for n in TWIN_WEIGHTS])
```

```python
import math

import jax
import jax.numpy as jnp
import numpy as np
from jax import lax
from jax.experimental import pallas as pl
from jax.experimental.pallas import tpu as pltpu

F32 = jnp.float32
BF16 = jnp.bfloat16
HI = lax.Precision.HIGHEST
NN = (((1,), (0,)), ((), ()))
NT = (((1,), (1,)), ((), ()))
MESH = pl.DeviceIdType.MESH
AXES = ("x", "y", "c")
N_DEV = 8

A_HEAD_DIM = 128
A_HEADS = 8
A_KV_HEADS = 2
A_GROUP = 4
WINDOW = 128
BLOCK = 128
N_BUCKETS = 32
MAX_DISTANCE = 128
B_HEADS = 4
B_KEY_DIM = 64
B_VAL_DIM = 128
GATE_RANK = 16
GATE_TAU = 16.0
GLA_CHUNK = 16
C_WIDTH = 512
C_BLOCKS = 4
C_BLOCK_DIM = 128
LRU_C = 8.0
X_HEADS = 4
EPS = 1e-6
NEG_INF = -1e30
A_Q = A_HEADS * A_HEAD_DIM
A_KV = A_KV_HEADS * A_HEAD_DIM
B_QK = B_HEADS * B_KEY_DIM
B_V = B_HEADS * B_VAL_DIM
SPLIT_SIZES = (A_Q, A_KV, A_KV, B_QK, B_QK, B_V, B_V, GATE_RANK, GATE_RANK, C_WIDTH, C_WIDTH)
D_IN = sum(SPLIT_SIZES)
D_MIX = A_Q + B_V + C_WIDTH
W_IN_SHARD = D_IN // N_DEV
W_IN_ROWS = 768
GLA_TILE = 128
CHUNKS_PER_TILE = GLA_TILE // GLA_CHUNK
EXP_CLAMP = 80.0

ADAM_LR = 0.001
ADAM_B1 = 0.9
ADAM_B2 = 0.999
ADAM_EPS = 1e-08
ADAM_WD = 0.01
ADAM_STEP = 10

VMEM_LIMIT_BYTES = 48 * 1024 * 1024


def _call(body, **kw):
    return pl.pallas_call(body, **kw)


def _cparams():
    return pltpu.CompilerParams(vmem_limit_bytes=VMEM_LIMIT_BYTES)


def _dot(a, b, dims=NN, hi=False):
    if hi:
        return lax.dot_general(a, b, dims, precision=HI, preferred_element_type=F32)
    return lax.dot_general(a.astype(BF16), b.astype(BF16), dims, preferred_element_type=F32)


def _sds(shape, dtype=F32):
    return jax.ShapeDtypeStruct(tuple(shape), dtype)


def _row_tile(rows, cols, target_elems=1 << 18):
    want = max(8, target_elems // max(cols, 1))
    if rows <= want:
        return rows
    t = (want // 8) * 8
    while t >= 8:
        if rows % t == 0:
            return t
        t -= 8
    return rows


def _expm1(x):
    poly = x * (1.0 + x * (1.0 / 2 + x * (1.0 / 6 + x * (1.0 / 24 + x * (1.0 / 120 + x * (
        1.0 / 720 + x * (1.0 / 5040 + x * (1.0 / 40320))))))))
    return jnp.where(jnp.abs(x) < 0.3, poly, jnp.exp(x) - 1.0)


def _log1p(e):
    w = 1.0 + e
    return jnp.where(w == 1.0, e, jnp.log(w) * e / (w - 1.0))


def _softplus(x):
    return jnp.maximum(x, 0.0) + _log1p(jnp.exp(-jnp.abs(x)))


def _sigmoid(x):
    return jax.nn.sigmoid(x)


GELU_K = math.sqrt(2.0 / math.pi)


def _gelu(y):
    t = jnp.tanh(GELU_K * (y + 0.044715 * y * y * y))
    return 0.5 * y * (1.0 + t)


def _gelu_grad(y):
    t = jnp.tanh(GELU_K * (y + 0.044715 * y * y * y))
    return 0.5 * (1.0 + t) + 0.5 * y * (1.0 - t * t) * GELU_K * (1.0 + 3 * 0.044715 * y * y)


def rms_fwd(x, g, name):
    m, d = x.shape
    tm = _row_tile(m, d)

    def body(x_ref, g_ref, o_ref):
        xv = x_ref[...]
        r = lax.rsqrt(jnp.mean(xv * xv, axis=1, keepdims=True) + EPS)
        o_ref[...] = xv * r * g_ref[...]

    return _call(body, name=name, grid=(m // tm,),
                 in_specs=[pl.BlockSpec((tm, d), lambda i: (i, 0)), pl.BlockSpec((1, d), lambda i: (0, 0))],
                 out_specs=pl.BlockSpec((tm, d), lambda i: (i, 0)),
                 out_shape=_sds((m, d)))(x, g.reshape(1, d))


def resid_rms(xres, mid, g_post, g_pre, name):
    m, d = xres.shape
    tm = _row_tile(m, d)
    with_pre = g_pre is not None

    def body(*refs):
        if with_pre:
            x_ref, m_ref, gp_ref, gn_ref, xo_ref, h_ref = refs
        else:
            x_ref, m_ref, gp_ref, xo_ref = refs
        mv = m_ref[...]
        r = lax.rsqrt(jnp.mean(mv * mv, axis=1, keepdims=True) + EPS)
        xn = x_ref[...] + mv * r * gp_ref[...]
        xo_ref[...] = xn
        if with_pre:
            r2 = lax.rsqrt(jnp.mean(xn * xn, axis=1, keepdims=True) + EPS)
            h_ref[...] = xn * r2 * gn_ref[...]

    row = pl.BlockSpec((tm, d), lambda i: (i, 0))
    vec = pl.BlockSpec((1, d), lambda i: (0, 0))
    ins = [xres, mid, g_post.reshape(1, d)] + ([g_pre.reshape(1, d)] if with_pre else [])
    in_specs = [row, row, vec] + ([vec] if with_pre else [])
    if with_pre:
        return _call(body, name=name, grid=(m // tm,), in_specs=in_specs, out_specs=(row, row),
                     out_shape=(_sds((m, d)), _sds((m, d))))(*ins)
    return _call(body, name=name, grid=(m // tm,), in_specs=in_specs, out_specs=row,
                 out_shape=_sds((m, d)))(*ins)


def rms_bwd(x, g, dy, name, dy2=None, add=None):
    m, d = x.shape
    tm = _row_tile(m, d)
    has2, hasadd = dy2 is not None, add is not None

    def body(*refs):
        it = iter(refs)
        x_ref, g_ref, dy_ref = next(it), next(it), next(it)
        dy2_ref = next(it) if has2 else None
        add_ref = next(it) if hasadd else None
        dx_ref, dg_ref = next(it), next(it)
        xv = x_ref[...]
        dyv = dy_ref[...]
        if has2:
            dyv = dyv + dy2_ref[...]
        r = lax.rsqrt(jnp.mean(xv * xv, axis=1, keepdims=True) + EPS)
        xh = xv * r
        dxh = dyv * g_ref[...]
        dx = r * (dxh - xh * jnp.mean(dxh * xh, axis=1, keepdims=True))
        if hasadd:
            dx = dx + add_ref[...]
        dx_ref[...] = dx
        part = jnp.sum(dyv * xh, axis=0, keepdims=True)

        @pl.when(pl.program_id(0) == 0)
        def _():
            dg_ref[...] = part

        @pl.when(pl.program_id(0) > 0)
        def _():
            dg_ref[...] += part

    row = pl.BlockSpec((tm, d), lambda i: (i, 0))
    vec = pl.BlockSpec((1, d), lambda i: (0, 0))
    ins = [x, g.reshape(1, d), dy] + ([dy2] if has2 else []) + ([add] if hasadd else [])
    in_specs = [row, vec, row] + ([row] if has2 else []) + ([row] if hasadd else [])
    return _call(body, name=name, grid=(m // tm,), in_specs=in_specs, out_specs=(row, vec),
                 out_shape=(_sds((m, d)), _sds((1, d))))(*ins)


def loss_and_grad(y, target, name):
    m, d = y.shape
    tm = _row_tile(m, d)

    def body(y_ref, t_ref, dy_ref, l_ref):
        e = y_ref[...] - t_ref[...]
        dy_ref[...] = e * (1.0 / d)
        s = jnp.sum(jnp.sum(e * e, axis=1, keepdims=True), axis=0, keepdims=True) * (0.5 / d)
        part = jnp.broadcast_to(s, (1, 128))

        @pl.when(pl.program_id(0) == 0)
        def _():
            l_ref[...] = part

        @pl.when(pl.program_id(0) > 0)
        def _():
            l_ref[...] += part

    row = pl.BlockSpec((tm, d), lambda i: (i, 0))
    dy, l = _call(body, name=name, grid=(m // tm,), in_specs=[row, row],
                  out_specs=(row, pl.BlockSpec((1, 128), lambda i: (0, 0))),
                  out_shape=(_sds((m, d)), _sds((1, 128))))(y, target)
    return dy, l[0, 0]


def adamw(g, w, m, v, name):
    shape = w.shape
    cols = shape[-1]
    rows = int(np.prod(shape[:-1]))
    tm = _row_tile(rows, cols)
    c1 = 1.0 - ADAM_B1 ** ADAM_STEP
    c2 = 1.0 - ADAM_B2 ** ADAM_STEP

    def body(g_ref, w_ref, m_ref, v_ref, d_ref, mo_ref, vo_ref):
        gv = g_ref[...]
        mn = ADAM_B1 * m_ref[...] + (1.0 - ADAM_B1) * gv
        vn = ADAM_B2 * v_ref[...] + (1.0 - ADAM_B2) * (gv * gv)
        m_hat = mn / c1
        v_hat = vn / c2
        d_ref[...] = -ADAM_LR * (m_hat / (jnp.sqrt(v_hat) + ADAM_EPS) + ADAM_WD * w_ref[...])
        mo_ref[...] = mn
        vo_ref[...] = vn

    row = pl.BlockSpec((tm, cols), lambda i: (i, 0))
    outs = _call(body, name=name, grid=(rows // tm,), in_specs=[row] * 4, out_specs=(row,) * 3,
                 out_shape=(_sds((rows, cols)),) * 3)(*[a.reshape(rows, cols) for a in (g, w, m, v)])
    return tuple(o.reshape(shape) for o in outs)


def sum_lead(x, order, out_dtype, name):
    n, rows, cols = x.shape
    tm = _row_tile(rows, cols)

    def body(x_ref, o_ref):
        acc = x_ref[order[0]].astype(F32)
        for i in order[1:]:
            acc = acc + x_ref[i].astype(F32)
        o_ref[...] = acc.astype(out_dtype)

    return _call(body, name=name, grid=(rows // tm,), in_specs=[pl.BlockSpec((n, tm, cols), lambda i: (0, i, 0))],
                 out_specs=pl.BlockSpec((tm, cols), lambda i: (i, 0)), out_shape=_sds((rows, cols), out_dtype))(x)


def add_n(xs, out_dtype, name):
    shape = xs[0].shape
    cols = shape[-1]
    rows = int(np.prod(shape[:-1]))
    tm = _row_tile(rows, cols)
    n = len(xs)

    def body(*refs):
        acc = refs[0][...].astype(F32)
        for r in refs[1:n]:
            acc = acc + r[...].astype(F32)
        refs[n][...] = acc.astype(out_dtype)

    row = pl.BlockSpec((tm, cols), lambda i: (i, 0))
    out = _call(body, name=name, grid=(rows // tm,), in_specs=[row] * n, out_specs=row,
                out_shape=_sds((rows, cols), out_dtype))(*[a.reshape(rows, cols) for a in xs])
    return out.reshape(shape)


def mm_plain(a, b, name, tb=False, out_dtype=F32, hi=False, tm=512, tn=512):
    m, k = a.shape
    n = b.shape[0] if tb else b.shape[1]
    tm, tn = min(tm, m), min(tn, n)

    def body(a_ref, b_ref, o_ref):
        o_ref[...] = _dot(a_ref[...], b_ref[...], NT if tb else NN, hi).astype(out_dtype)

    b_spec = pl.BlockSpec((tn, k), lambda i, j: (j, 0)) if tb else pl.BlockSpec((k, tn), lambda i, j: (0, j))
    return _call(body, name=name, grid=(m // tm, n // tn),
                 in_specs=[pl.BlockSpec((tm, k), lambda i, j: (i, 0)), b_spec],
                 out_specs=pl.BlockSpec((tm, tn), lambda i, j: (i, j)),
                 out_shape=_sds((m, n), out_dtype), compiler_params=_cparams())(a, b)


def mm_wk(a, gw, off, r, name, jb=N_DEV, relu2=False, tm=512, tn=512):
    m = a.shape[0]
    d = gw.shape[2]
    tm, tn = min(tm, m), min(tn, d)
    nk = N_DEV // jb
    ob = off // r
    assert off % r == 0 and a.shape[1] == N_DEV * r

    def body(a_ref, b_ref, o_ref, *acc):
        av = a_ref[...]
        if relu2:
            av = jnp.square(jnp.maximum(av, 0.0))
        av = av.astype(BF16)
        p = _dot(av[:, 0:r], b_ref[0])
        for q in range(1, jb):
            p = p + _dot(av[:, q * r:(q + 1) * r], b_ref[q])
        if nk == 1:
            o_ref[...] = p
        else:
            kk = pl.program_id(2)

            @pl.when(kk == 0)
            def _():
                acc[0][...] = p

            @pl.when(kk > 0)
            def _():
                acc[0][...] += p

            @pl.when(kk == nk - 1)
            def _():
                o_ref[...] = acc[0][...]

    return _call(body, name=name, grid=(m // tm, d // tn, nk),
                 in_specs=[pl.BlockSpec((tm, jb * r), lambda i, j, k: (i, k)),
                           pl.BlockSpec((jb, r, tn), lambda i, j, k: (k, ob, j))],
                 out_specs=pl.BlockSpec((tm, tn), lambda i, j, k: (i, j)),
                 out_shape=_sds((m, d)),
                 scratch_shapes=([pltpu.VMEM((tm, tn), F32)] if nk > 1 else []),
                 compiler_params=_cparams())(a, gw)


def mm_wn(a, gw, off, r, name, relu_grad_of=None, tm=512):
    m, d = a.shape
    tm = min(tm, m)
    ob = off // r
    assert off % r == 0 and gw.shape[2] == d
    epi = relu_grad_of is not None

    def body(*refs):
        if epi:
            a_ref, b_ref, e_ref, o_ref = refs
        else:
            a_ref, b_ref, o_ref = refs
        p = _dot(a_ref[...], b_ref[...], NT)
        if epi:
            p = p * (2.0 * jnp.maximum(e_ref[...], 0.0))
        o_ref[...] = p

    blk = pl.BlockSpec((tm, r), lambda i, j: (i, j))
    in_specs = [pl.BlockSpec((tm, d), lambda i, j: (i, 0)), pl.BlockSpec((None, r, d), lambda i, j: (j, ob, 0))]
    ins = [a, gw]
    if epi:
        in_specs.append(blk)
        ins.append(relu_grad_of)
    return _call(body, name=name, grid=(m // tm, N_DEV), in_specs=in_specs, out_specs=blk,
                 out_shape=_sds((m, N_DEV * r)), compiler_params=_cparams())(*ins)


def blockdiag_dw(xt, dz, name):
    t = xt.shape[1]

    def body(a_ref, b_ref, o_ref):
        o_ref[...] = _dot(a_ref[...], b_ref[...])

    return _call(body, name=name, grid=(C_BLOCKS,),
                 in_specs=[pl.BlockSpec((C_BLOCK_DIM, t), lambda g: (g, 0)),
                           pl.BlockSpec((t, C_BLOCK_DIM), lambda g: (0, g))],
                 out_specs=pl.BlockSpec((None, C_BLOCK_DIM, C_BLOCK_DIM), lambda g: (g, 0, 0)),
                 out_shape=_sds((C_BLOCKS, C_BLOCK_DIM, C_BLOCK_DIM)))(xt, dz)


def _band_mask(n, nblk, transposed):
    shape = (3 * BLOCK, BLOCK) if transposed else (BLOCK, 3 * BLOCK)
    qi = lax.broadcasted_iota(jnp.int32, shape, 1 if transposed else 0)
    kj = lax.broadcasted_iota(jnp.int32, shape, 0 if transposed else 1)
    lo = jnp.where(n > 0, 0, BLOCK)
    hi = jnp.where(n < nblk - 1, 3 * BLOCK, 2 * BLOCK)
    return (jnp.abs(kj - BLOCK - qi) <= WINDOW) & (kj >= lo) & (kj < hi)


def _band_rows(ref, n, nblk):
    starts = [jnp.maximum(n - 1, 0), n, jnp.minimum(n + 1, nblk - 1)]
    return jnp.concatenate([ref[pl.ds(pl.multiple_of(s * BLOCK, BLOCK), BLOCK), :] for s in starts], axis=0)


def attn_fwd(q, k, v, bias, sink_b, name):
    t = q.shape[0]
    nblk = t // BLOCK
    scale = A_HEAD_DIM ** -0.5

    def body(q_ref, k_ref, v_ref, b_ref, s_ref, o_ref):
        n = pl.program_id(1)
        kb = _band_rows(k_ref, n, nblk).astype(BF16)
        vb = _band_rows(v_ref, n, nblk).astype(BF16)
        mask = _band_mask(n, nblk, False)
        for j in range(A_GROUP):
            sl = slice(j * A_HEAD_DIM, (j + 1) * A_HEAD_DIM)
            s = _dot(q_ref[:, sl], kb, NT) * scale + b_ref[j]
            s = jnp.where(mask, s, NEG_INF)
            sk = s_ref[j:j + 1, 0:1]
            mx = jnp.maximum(jnp.max(s, axis=1, keepdims=True), sk)
            p = jnp.exp(s - mx)
            den = jnp.sum(p, axis=1, keepdims=True) + jnp.exp(sk - mx)
            o_ref[:, sl] = _dot(p / den, vb)

    gw = A_GROUP * A_HEAD_DIM
    return _call(body, name=name, grid=(A_KV_HEADS, nblk),
                 in_specs=[pl.BlockSpec((BLOCK, gw), lambda g, n: (n, g)),
                           pl.BlockSpec((t, A_HEAD_DIM), lambda g, n: (0, g)),
                           pl.BlockSpec((t, A_HEAD_DIM), lambda g, n: (0, g)),
                           pl.BlockSpec((A_GROUP, BLOCK, 3 * BLOCK), lambda g, n: (g, 0, 0)),
                           pl.BlockSpec((None, 8, 128), lambda g, n: (g, 0, 0))],
                 out_specs=pl.BlockSpec((BLOCK, gw), lambda g, n: (n, g)),
                 out_shape=_sds((t, A_Q)))(q, k, v, bias, sink_b)


def attn_bwd(q, k, v, bias, bias_t, sink_b, do, o, name):
    t = q.shape[0]
    nblk = t // BLOCK
    scale = A_HEAD_DIM ** -0.5

    def body(q_ref, k_ref, v_ref, b_ref, bt_ref, s_ref, do_ref, o_ref, dq_ref, dk_ref, dv_ref, db_ref, ds_ref):
        n = pl.program_id(1)

        @pl.when(n == 0)
        def _():
            dk_ref[...] = jnp.zeros_like(dk_ref)
            dv_ref[...] = jnp.zeros_like(dv_ref)
            db_ref[...] = jnp.zeros_like(db_ref)
            ds_ref[...] = jnp.zeros_like(ds_ref)

        kb = _band_rows(k_ref, n, nblk).astype(BF16)
        vb = _band_rows(v_ref, n, nblk).astype(BF16)
        mask = _band_mask(n, nblk, False)
        mask_t = _band_mask(n, nblk, True)
        ones8 = jnp.ones((8, A_HEAD_DIM), F32)
        dkb = jnp.zeros((3 * BLOCK, A_HEAD_DIM), F32)
        dvb = jnp.zeros((3 * BLOCK, A_HEAD_DIM), F32)
        for j in range(A_GROUP):
            sl = slice(j * A_HEAD_DIM, (j + 1) * A_HEAD_DIM)
            qj = q_ref[:, sl].astype(BF16)
            doj = do_ref[:, sl]
            doo = doj * o_ref[:, sl]
            doj = doj.astype(BF16)
            sk = s_ref[j:j + 1, 0:1]
            s = jnp.where(mask, _dot(qj, kb, NT) * scale + b_ref[j], NEG_INF)
            mx = jnp.maximum(jnp.max(s, axis=1, keepdims=True), sk)
            p = jnp.exp(s - mx)
            den = jnp.sum(p, axis=1, keepdims=True) + jnp.exp(sk - mx)
            p = p / den
            psink = jnp.exp(sk - mx) / den
            delta = jnp.sum(doo, axis=1, keepdims=True)
            dsc = p * (_dot(doj, vb, NT) - delta)
            db_ref[j] += dsc
            ds_ref[j:j + 1, :] += jnp.broadcast_to(-jnp.sum(psink * delta, axis=0, keepdims=True), (1, 128))
            dq_ref[:, sl] = _dot(dsc, kb) * scale
            st = jnp.where(mask_t, _dot(kb, qj, NT) * scale + bt_ref[j], NEG_INF)
            mxt = jnp.maximum(jnp.max(st, axis=0, keepdims=True), sk)
            pt = jnp.exp(st - mxt)
            dent = jnp.sum(pt, axis=0, keepdims=True) + jnp.exp(sk - mxt)
            pt = pt / dent
            delta_t = _dot(ones8, doo, NT, hi=True)[0:1, :]
            dst = pt * (_dot(vb, doj, NT) - delta_t)
            dkb = dkb + _dot(dst, qj) * scale
            dvb = dvb + _dot(pt, doj)
        starts = [jnp.maximum(n - 1, 0), n, jnp.minimum(n + 1, nblk - 1)]
        for c, st_ in enumerate(starts):
            rows = pl.ds(pl.multiple_of(st_ * BLOCK, BLOCK), BLOCK)
            dk_ref[rows, :] += dkb[c * BLOCK:(c + 1) * BLOCK, :]
            dv_ref[rows, :] += dvb[c * BLOCK:(c + 1) * BLOCK, :]

    gw = A_GROUP * A_HEAD_DIM
    qspec = pl.BlockSpec((BLOCK, gw), lambda g, n: (n, g))
    kspec = pl.BlockSpec((t, A_HEAD_DIM), lambda g, n: (0, g))
    sspec = pl.BlockSpec((None, 8, 128), lambda g, n: (g, 0, 0))
    bspec = pl.BlockSpec((A_GROUP, BLOCK, 3 * BLOCK), lambda g, n: (g, 0, 0))
    btspec = pl.BlockSpec((A_GROUP, 3 * BLOCK, BLOCK), lambda g, n: (g, 0, 0))
    return _call(body, name=name, grid=(A_KV_HEADS, nblk),
                 in_specs=[qspec, kspec, kspec, bspec, btspec, sspec, qspec, qspec],
                 out_specs=(qspec, kspec, kspec, bspec, sspec),
                 out_shape=(_sds((t, A_Q)), _sds((t, A_KV)), _sds((t, A_KV)),
                            _sds((A_HEADS, BLOCK, 3 * BLOCK)), _sds((A_KV_HEADS, 8, 128))),
                 compiler_params=_cparams())(q, k, v, bias, bias_t, sink_b, do, o)


def xattn_fwd(q, k, v, name):
    t, d = q.shape
    ml = k.shape[0]
    dh = d // X_HEADS
    tq = min(256, t)
    scale = dh ** -0.5

    def body(q_ref, k_ref, v_ref, o_ref):
        s = _dot(q_ref[...], k_ref[...], NT) * scale
        p = jnp.exp(s - jnp.max(s, axis=1, keepdims=True))
        p = p / jnp.sum(p, axis=1, keepdims=True)
        o_ref[...] = _dot(p, v_ref[...])

    qspec = pl.BlockSpec((tq, dh), lambda h, i: (i, h))
    kspec = pl.BlockSpec((ml, dh), lambda h, i: (0, h))
    return _call(body, name=name, grid=(X_HEADS, t // tq), in_specs=[qspec, kspec, kspec], out_specs=qspec,
                 out_shape=_sds((t, d)))(q, k, v)


def xattn_bwd(q, k, v, o, do, name):
    t, d = q.shape
    ml = k.shape[0]
    dh = d // X_HEADS
    tq = min(256, t)
    scale = dh ** -0.5

    def body(q_ref, k_ref, v_ref, o_ref, do_ref, dq_ref, dk_ref, dv_ref):
        i = pl.program_id(1)
        qv, kv, vv = q_ref[...].astype(BF16), k_ref[...].astype(BF16), v_ref[...].astype(BF16)
        dov = do_ref[...]
        doo = dov * o_ref[...]
        dov = dov.astype(BF16)
        s = _dot(qv, kv, NT) * scale
        p = jnp.exp(s - jnp.max(s, axis=1, keepdims=True))
        p = p / jnp.sum(p, axis=1, keepdims=True)
        ds = p * (_dot(dov, vv, NT) - jnp.sum(doo, axis=1, keepdims=True))
        dq_ref[...] = _dot(ds, kv) * scale
        st = _dot(kv, qv, NT) * scale
        pt = jnp.exp(st - jnp.max(st, axis=0, keepdims=True))
        pt = pt / jnp.sum(pt, axis=0, keepdims=True)
        delta_t = _dot(jnp.ones((8, dh), F32), doo, NT, hi=True)[0:1, :]
        dst = pt * (_dot(vv, dov, NT) - delta_t)
        dkp = _dot(dst, qv) * scale
        dvp = _dot(pt, dov)

        @pl.when(i == 0)
        def _():
            dk_ref[...] = dkp
            dv_ref[...] = dvp

        @pl.when(i > 0)
        def _():
            dk_ref[...] += dkp
            dv_ref[...] += dvp

    qspec = pl.BlockSpec((tq, dh), lambda h, i: (i, h))
    kspec = pl.BlockSpec((ml, dh), lambda h, i: (0, h))
    return _call(body, name=name, grid=(X_HEADS, t // tq), in_specs=[qspec, kspec, kspec, qspec, qspec],
                 out_specs=(qspec, kspec, kspec),
                 out_shape=(_sds((t, d)), _sds((ml, d)), _sds((ml, d))))(q, k, v, o, do)


def scan_lead(a, u, name, reverse, inclusive):
    n, r, c = a.shape
    blk = max(1, min(n, (1 << 18) // (max(r, 8) * c)))
    while n % blk:
        blk -= 1
    nb = n // blk

    def body(a_ref, u_ref, o_ref, carry):
        @pl.when(pl.program_id(0) == 0)
        def _():
            carry[...] = jnp.zeros_like(carry)

        def step(s, h):
            idx = (blk - 1 - s) if reverse else s
            hn = a_ref[idx] * h + u_ref[idx]
            o_ref[idx] = hn if inclusive else h
            return hn

        carry[...] = lax.fori_loop(0, blk, step, carry[...])

    spec = pl.BlockSpec((blk, r, c), (lambda i: (nb - 1 - i, 0, 0)) if reverse else (lambda i: (i, 0, 0)))
    return _call(body, name=name, grid=(nb,), in_specs=[spec, spec], out_specs=spec,
                 out_shape=_sds((n, r, c)), scratch_shapes=[pltpu.VMEM((r, c), F32)])(a, u)


def _chunk_mats(bwd_dir):
    i = lax.broadcasted_iota(jnp.int32, (GLA_TILE, GLA_TILE), 0)
    j = lax.broadcasted_iota(jnp.int32, (GLA_TILE, GLA_TILE), 1)
    same = lax.shift_right_logical(i, 4) == lax.shift_right_logical(j, 4)
    if bwd_dir:
        cm, cm_t = same & (j >= i), same & (i >= j)
        mk, mk_t = same & (j > i), same & (i > j)
    else:
        cm, cm_t = same & (j <= i), same & (i <= j)
        mk, mk_t = same & (j <= i), same & (i <= j)
    f = lambda b: jnp.where(b, 1.0, 0.0).astype(F32)
    return f(cm), f(cm_t), mk, mk_t, f(same)


def gla_gates_fwd(zf, zb, w2f, b2f, w2b, b2b, name):
    t = zf.shape[0]
    tm = min(256, t)

    def body(zf_ref, zb_ref, wf_ref, bf_ref, wb_ref, bb_ref, lf_ref, lb_ref):
        lf_ref[...] = -_softplus(-(_dot(zf_ref[...], wf_ref[...], hi=True) + bf_ref[...])) / GATE_TAU
        lb_ref[...] = -_softplus(-(_dot(zb_ref[...], wb_ref[...], hi=True) + bb_ref[...])) / GATE_TAU

    zs = pl.BlockSpec((tm, GATE_RANK), lambda i: (i, 0))
    ws = pl.BlockSpec((GATE_RANK, B_QK), lambda i: (0, 0))
    bs = pl.BlockSpec((1, B_QK), lambda i: (0, 0))
    os_ = pl.BlockSpec((tm, B_QK), lambda i: (i, 0))
    return _call(body, name=name, grid=(t // tm,), in_specs=[zs, zs, ws, bs, ws, bs], out_specs=(os_, os_),
                 out_shape=(_sds((t, B_QK)),) * 2)(zf, zb, w2f, b2f.reshape(1, B_QK), w2b, b2b.reshape(1, B_QK))


def gla_gates_bwd(zf, zb, w2f, b2f, w2b, b2b, dlf, dlb, name):
    t = zf.shape[0]
    tm = min(256, t)

    def body(zf_ref, zb_ref, wf_ref, bf_ref, wb_ref, bb_ref, dlf_ref, dlb_ref,
             dzf_ref, dzb_ref, dpf_ref, dpb_ref, dbf_ref, dbb_ref):
        first = pl.program_id(0) == 0
        for z_ref, w_ref, b_ref, dl_ref, dz_ref, dp_ref, db_ref in (
                (zf_ref, wf_ref, bf_ref, dlf_ref, dzf_ref, dpf_ref, dbf_ref),
                (zb_ref, wb_ref, bb_ref, dlb_ref, dzb_ref, dpb_ref, dbb_ref)):
            pre = _dot(z_ref[...], w_ref[...], hi=True) + b_ref[...]
            dpre = dl_ref[...] * (1.0 / GATE_TAU) * _sigmoid(-pre)
            dp_ref[...] = dpre
            dz_ref[...] = _dot(dpre, w_ref[...], NT, hi=True)
            part = jnp.sum(dpre, axis=0, keepdims=True)

            @pl.when(first)
            def _():
                db_ref[...] = part

            @pl.when(jnp.logical_not(first))
            def _():
                db_ref[...] += part

    zs = pl.BlockSpec((tm, GATE_RANK), lambda i: (i, 0))
    ws = pl.BlockSpec((GATE_RANK, B_QK), lambda i: (0, 0))
    bs = pl.BlockSpec((1, B_QK), lambda i: (0, 0))
    os_ = pl.BlockSpec((tm, B_QK), lambda i: (i, 0))
    return _call(body, name=name, grid=(t // tm,), in_specs=[zs, zs, ws, bs, ws, bs, os_, os_],
                 out_specs=(zs, zs, os_, os_, bs, bs),
                 out_shape=(_sds((t, GATE_RANK)),) * 2 + (_sds((t, B_QK)),) * 2 + (_sds((1, B_QK)),) * 2)(
        zf, zb, w2f, b2f.reshape(1, B_QK), w2b, b2b.reshape(1, B_QK), dlf, dlb)


def gla_outer(xt, lat, y, name, bwd_dir, mode):
    t = y.shape[0]
    nchunk = t // GLA_CHUNK
    khat = mode == "khat"
    scale = B_KEY_DIM ** -0.5

    def body(xt_ref, lat_ref, y_ref, *outs):
        _, cm_t, _, _, same = _chunk_mats(bwd_dir)
        lat_v = lat_ref[...]
        bt = _dot(lat_v, cm_t, hi=True)
        if khat:
            mult = jnp.exp(_dot(lat_v, same, hi=True) - bt)
        else:
            mult = jnp.exp(bt) * scale
        xm = xt_ref[...] * mult
        lane = lax.shift_right_logical(lax.broadcasted_iota(jnp.int32, (1, GLA_TILE), 1), 4)
        ones = jnp.ones((GLA_TILE, B_VAL_DIM), F32)
        yv = [y_ref[:, h * B_VAL_DIM:(h + 1) * B_VAL_DIM].astype(BF16) for h in range(B_HEADS)]
        for c in range(CHUNKS_PER_TILE):
            sel = lane == c
            xc = jnp.where(sel, xm, 0.0).astype(BF16)
            for h in range(B_HEADS):
                rows = slice(h * B_KEY_DIM, (h + 1) * B_KEY_DIM)
                outs[0][c, rows, :] = _dot(xc[rows, :], yv[h])
            if khat:
                outs[1][c] = jnp.exp(_dot(jnp.where(sel, lat_v, 0.0), ones, hi=True))

    tspec = pl.BlockSpec((B_QK, GLA_TILE), lambda i: (0, i))
    ospec = pl.BlockSpec((CHUNKS_PER_TILE, B_QK, B_VAL_DIM), lambda i: (i, 0, 0))
    oshape = _sds((nchunk, B_QK, B_VAL_DIM))
    return _call(body, name=name, grid=(t // GLA_TILE,),
                 in_specs=[tspec, tspec, pl.BlockSpec((GLA_TILE, B_V), lambda i: (i, 0))],
                 out_specs=(ospec, ospec) if khat else ospec,
                 out_shape=(oshape, oshape) if khat else oshape)(xt, lat, y)


def _head_lane_mask(h):
    lane = lax.broadcasted_iota(jnp.int32, (1, B_QK), 1)
    return lax.shift_right_logical(lane, 6) == h


def _chunk_rows(c):
    return slice(c * GLA_CHUNK, (c + 1) * GLA_CHUNK)


def gla_inner_fwd(q, k, v, la, sp, name, bwd_dir):
    t = q.shape[0]
    scale = B_KEY_DIM ** -0.5

    def body(q_ref, k_ref, v_ref, la_ref, sp_ref, o_ref):
        cm, _, mk, _, _ = _chunk_mats(bwd_dir)
        b = _dot(cm, la_ref[...], hi=True)
        qt = q_ref[...] * scale * jnp.exp(b)
        kt = k_ref[...] * jnp.exp(jnp.minimum(-b, EXP_CLAMP))
        spb = [sp_ref[c].astype(BF16) for c in range(CHUNKS_PER_TILE)]
        for h in range(B_HEADS):
            lm = _head_lane_mask(h)
            qm = jnp.where(lm, qt, 0.0).astype(BF16)
            km = jnp.where(lm, kt, 0.0).astype(BF16)
            vs = slice(h * B_VAL_DIM, (h + 1) * B_VAL_DIM)
            att = jnp.where(mk, _dot(qm, km, NT), 0.0)
            inter = jnp.concatenate([_dot(qm[_chunk_rows(c), :], spb[c]) for c in range(CHUNKS_PER_TILE)], axis=0)
            o_ref[:, vs] = _dot(att, v_ref[:, vs]) + inter

    qs = pl.BlockSpec((GLA_TILE, B_QK), lambda i: (i, 0))
    vs_ = pl.BlockSpec((GLA_TILE, B_V), lambda i: (i, 0))
    ss = pl.BlockSpec((CHUNKS_PER_TILE, B_QK, B_VAL_DIM), lambda i: (i, 0, 0))
    return _call(body, name=name, grid=(t // GLA_TILE,), in_specs=[qs, qs, vs_, qs, ss], out_specs=vs_,
                 out_shape=_sds((t, B_V)))(q, k, v, la, sp)


def gla_inner_bwd(q, k, v, la, do, sp, gs, dec, name, bwd_dir, add=None):
    t = q.shape[0]
    scale = B_KEY_DIM ** -0.5
    hasadd = add is not None

    def body(*refs):
        it = iter(refs)
        q_ref, k_ref, v_ref, la_ref, do_ref, sp_ref, gs_ref, dec_ref = [next(it) for _ in range(8)]
        adds = [next(it) for _ in range(3)] if hasadd else None
        dq_ref, dk_ref, dv_ref, dla_ref = [next(it) for _ in range(4)]
        cm, cm_t, mk, mk_t, same = _chunk_mats(bwd_dir)
        la_v = la_ref[...]
        b = _dot(cm, la_v, hi=True)
        btot = _dot(same, la_v, hi=True)
        eb = jnp.exp(b)
        ek = jnp.exp(jnp.minimum(-b, EXP_CLAMP))
        ekh = jnp.exp(btot - b)
        qt = q_ref[...] * scale * eb
        kt = k_ref[...] * ek
        kh = k_ref[...] * ekh
        spb = [sp_ref[c].astype(BF16) for c in range(CHUNKS_PER_TILE)]
        gsb = [gs_ref[c].astype(BF16) for c in range(CHUNKS_PER_TILE)]
        dqt = jnp.zeros((GLA_TILE, B_QK), F32)
        dkt = jnp.zeros((GLA_TILE, B_QK), F32)
        dkh = jnp.zeros((GLA_TILE, B_QK), F32)
        for h in range(B_HEADS):
            lm = _head_lane_mask(h)
            qm = jnp.where(lm, qt, 0.0).astype(BF16)
            km = jnp.where(lm, kt, 0.0).astype(BF16)
            khm = jnp.where(lm, kh, 0.0).astype(BF16)
            vs = slice(h * B_VAL_DIM, (h + 1) * B_VAL_DIM)
            vh = v_ref[:, vs].astype(BF16)
            doh = do_ref[:, vs].astype(BF16)
            da = jnp.where(mk, _dot(doh, vh, NT), 0.0)
            da_t = jnp.where(mk_t, _dot(vh, doh, NT), 0.0)
            att_t = jnp.where(mk_t, _dot(km, qm, NT), 0.0)
            dv_h = _dot(att_t, doh) + jnp.concatenate(
                [_dot(khm[_chunk_rows(c), :], gsb[c]) for c in range(CHUNKS_PER_TILE)], axis=0)
            if hasadd:
                dv_h = dv_h + adds[2][:, vs]
            dv_ref[:, vs] = dv_h
            dq_inter = jnp.concatenate(
                [_dot(doh[_chunk_rows(c), :], spb[c], NT) for c in range(CHUNKS_PER_TILE)], axis=0)
            dqt = dqt + _dot(da, km) + jnp.where(lm, dq_inter, 0.0)
            dkt = dkt + _dot(da_t, qm)
            dkh_inter = jnp.concatenate(
                [_dot(vh[_chunk_rows(c), :], gsb[c], NT) for c in range(CHUNKS_PER_TILE)], axis=0)
            dkh = dkh + jnp.where(lm, dkh_inter, 0.0)
        dq = dqt * scale * eb
        dk = dkt * ek + dkh * ekh
        if hasadd:
            dq = dq + adds[0][...]
            dk = dk + adds[1][...]
        dq_ref[...] = dq
        dk_ref[...] = dk
        db = dqt * qt - dkt * kt - dkh * kh
        ones16 = jnp.ones((GLA_CHUNK, B_VAL_DIM), F32)
        t2 = jnp.concatenate(
            [_dot(ones16, gs_ref[c] * dec_ref[c] * sp_ref[c], NT, hi=True) for c in range(CHUNKS_PER_TILE)], axis=0)
        dla_ref[...] = _dot(cm_t, db, hi=True) + _dot(same, dkh * kh, hi=True) + t2

    qs = pl.BlockSpec((GLA_TILE, B_QK), lambda i: (i, 0))
    vs_ = pl.BlockSpec((GLA_TILE, B_V), lambda i: (i, 0))
    ss = pl.BlockSpec((CHUNKS_PER_TILE, B_QK, B_VAL_DIM), lambda i: (i, 0, 0))
    ins = [q, k, v, la, do, sp, gs, dec] + (list(add) if hasadd else [])
    in_specs = [qs, qs, vs_, qs, vs_, ss, ss, ss] + ([qs, qs, vs_] if hasadd else [])
    return _call(body, name=name, grid=(t // GLA_TILE,), in_specs=in_specs, out_specs=(qs, qs, vs_, qs),
                 out_shape=(_sds((t, B_QK)), _sds((t, B_QK)), _sds((t, B_V)), _sds((t, B_QK))),
                 compiler_params=_cparams())(*ins)


def gla_out_fwd(of, ob, g, gn, name):
    t = of.shape[0]
    tm = min(256, t)

    def body(of_ref, ob_ref, g_ref, gn_ref, o_ref):
        for h in range(B_HEADS):
            vs = slice(h * B_VAL_DIM, (h + 1) * B_VAL_DIM)
            o = of_ref[:, vs] + ob_ref[:, vs]
            on = o * lax.rsqrt(jnp.mean(o * o, axis=1, keepdims=True) + EPS)
            gv = g_ref[:, vs]
            o_ref[:, vs] = on * gn_ref[:, vs] * (gv * _sigmoid(gv))

    row = pl.BlockSpec((tm, B_V), lambda i: (i, 0))
    vec = pl.BlockSpec((1, B_V), lambda i: (0, 0))
    return _call(body, name=name, grid=(t // tm,), in_specs=[row, row, row, vec], out_specs=row,
                 out_shape=_sds((t, B_V)))(of, ob, g, gn.reshape(1, B_V))


def gla_out_bwd(of, ob, g, gn, dout, name):
    t = of.shape[0]
    tm = min(256, t)

    def body(of_ref, ob_ref, g_ref, gn_ref, d_ref, do_ref, dg_ref, dgn_ref):
        first = pl.program_id(0) == 0
        for h in range(B_HEADS):
            vs = slice(h * B_VAL_DIM, (h + 1) * B_VAL_DIM)
            o = of_ref[:, vs] + ob_ref[:, vs]
            r = lax.rsqrt(jnp.mean(o * o, axis=1, keepdims=True) + EPS)
            on = o * r
            gv = g_ref[:, vs]
            sg = _sigmoid(gv)
            silu = gv * sg
            dv = d_ref[:, vs]
            gnv = gn_ref[:, vs]
            dg_ref[:, vs] = dv * on * gnv * (sg * (1.0 + gv * (1.0 - sg)))
            don = dv * silu * gnv
            do_ref[:, vs] = r * (don - on * jnp.mean(don * on, axis=1, keepdims=True))
            part = jnp.sum(dv * silu * on, axis=0, keepdims=True)

            @pl.when(first)
            def _():
                dgn_ref[:, vs] = part

            @pl.when(jnp.logical_not(first))
            def _():
                dgn_ref[:, vs] += part

    row = pl.BlockSpec((tm, B_V), lambda i: (i, 0))
    vec = pl.BlockSpec((1, B_V), lambda i: (0, 0))
    return _call(body, name=name, grid=(t // tm,), in_specs=[row, row, row, vec, row], out_specs=(row, row, vec),
                 out_shape=(_sds((t, B_V)), _sds((t, B_V)), _sds((1, B_V))))(of, ob, g, gn.reshape(1, B_V), dout)


def _shift(x, k):
    if k > 0:
        return jnp.concatenate([x[k:], jnp.zeros((k,) + x.shape[1:], x.dtype)], axis=0)
    return jnp.concatenate([jnp.zeros((-k,) + x.shape[1:], x.dtype), x[:k]], axis=0)


def _lru_gates(xc, s, wa_ref, ba_ref, wx_ref, bx_ref, lam_ref):
    cols = [slice(g * C_BLOCK_DIM, (g + 1) * C_BLOCK_DIM) for g in range(C_BLOCKS)]
    zr = jnp.concatenate([_dot(xc[:, cs], wa_ref[s, g]) for g, cs in enumerate(cols)], axis=1) + ba_ref[s:s + 1, :]
    zi = jnp.concatenate([_dot(xc[:, cs], wx_ref[s, g]) for g, cs in enumerate(cols)], axis=1) + bx_ref[s:s + 1, :]
    r = _sigmoid(zr)
    i = _sigmoid(zi)
    sp = _softplus(-lam_ref[s:s + 1, :])
    log_a = -LRU_C * r * sp
    return r, i, sp, log_a


def lru_gates_fwd(x0, xm2, xm1, xp1, cw, cb, wa, ba, wx, bx, lam, name):
    t = x0.shape[0]
    tm = min(256, t)

    def body(x0_ref, xm2_ref, xm1_ref, xp1_ref, cw_ref, cb_ref, wa_ref, ba_ref, wx_ref, bx_ref, lam_ref,
             xc_ref, a0_ref, u0_ref, a1_ref, u1_ref):
        xc = (xm2_ref[...] * cw_ref[0:1, :] + xm1_ref[...] * cw_ref[1:2, :] + x0_ref[...] * cw_ref[2:3, :]
              + xp1_ref[...] * cw_ref[3:4, :] + cb_ref[...])
        xc_ref[...] = xc
        for s, (a_ref, u_ref) in enumerate(((a0_ref, u0_ref), (a1_ref, u1_ref))):
            _, i, _, log_a = _lru_gates(xc, s, wa_ref, ba_ref, wx_ref, bx_ref, lam_ref)
            a_ref[...] = jnp.exp(log_a)
            u_ref[...] = jnp.sqrt(-_expm1(2.0 * log_a)) * (i * xc)

    row = pl.BlockSpec((tm, C_WIDTH), lambda i: (i, 0))
    full = lambda shape: pl.BlockSpec(shape, lambda i: (0,) * len(shape))
    wshape = (2, C_BLOCKS, C_BLOCK_DIM, C_BLOCK_DIM)
    return _call(body, name=name, grid=(t // tm,),
                 in_specs=[row] * 4 + [full((4, C_WIDTH)), full((1, C_WIDTH)), full(wshape), full((2, C_WIDTH)),
                                       full(wshape), full((2, C_WIDTH)), full((2, C_WIDTH))],
                 out_specs=(row,) * 5, out_shape=(_sds((t, C_WIDTH)),) * 5)(
        x0, xm2, xm1, xp1, cw, cb.reshape(1, C_WIDTH), wa, ba, wx, bx, lam)


def lru_gates_bwd(xc, g0, hs0, g1, hs1, wa, ba, wx, bx, lam, name):
    t = xc.shape[0]
    tm = min(256, t)

    def body(xc_ref, g0_ref, hs0_ref, g1_ref, hs1_ref, wa_ref, ba_ref, wx_ref, bx_ref, lam_ref,
             dxc_ref, dzr0_ref, dzi0_ref, dzr1_ref, dzi1_ref, dlam_ref, dba_ref, dbx_ref):
        first = pl.program_id(0) == 0

        @pl.when(first)
        def _():
            dlam_ref[...] = jnp.zeros_like(dlam_ref)
            dba_ref[...] = jnp.zeros_like(dba_ref)
            dbx_ref[...] = jnp.zeros_like(dbx_ref)

        xcv = xc_ref[...]
        dxc = jnp.zeros_like(xcv)
        cols = [slice(g * C_BLOCK_DIM, (g + 1) * C_BLOCK_DIM) for g in range(C_BLOCKS)]
        for s, (g_ref, hs_ref, dzr_ref, dzi_ref) in enumerate(
                ((g0_ref, hs0_ref, dzr0_ref, dzi0_ref), (g1_ref, hs1_ref, dzr1_ref, dzi1_ref))):
            r, i, sp, log_a = _lru_gates(xcv, s, wa_ref, ba_ref, wx_ref, bx_ref, lam_ref)
            du = g_ref[...]
            da = du * hs_ref[...]
            a = jnp.exp(log_a)
            e2 = jnp.exp(2.0 * log_a)
            c = jnp.sqrt(-_expm1(2.0 * log_a))
            ix = i * xcv
            dlog = da * a - du * ix * (e2 / c)
            dix = du * c
            dxc = dxc + dix * i
            dzi = dix * xcv * i * (1.0 - i)
            dzr = dlog * (-LRU_C * sp) * r * (1.0 - r)
            dzr_ref[...] = dzr
            dzi_ref[...] = dzi
            dxc = dxc + jnp.concatenate(
                [_dot(dzr[:, cs], wa_ref[s, g], NT) + _dot(dzi[:, cs], wx_ref[s, g], NT) for g, cs in enumerate(cols)],
                axis=1)
            dsp = jnp.sum(dlog * (-LRU_C * r), axis=0, keepdims=True)
            dlam_ref[s:s + 1, :] += dsp * (-_sigmoid(-lam_ref[s:s + 1, :]))
            dba_ref[s:s + 1, :] += jnp.sum(dzr, axis=0, keepdims=True)
            dbx_ref[s:s + 1, :] += jnp.sum(dzi, axis=0, keepdims=True)
        dxc_ref[...] = dxc

    row = pl.BlockSpec((tm, C_WIDTH), lambda i: (i, 0))
    full = lambda shape: pl.BlockSpec(shape, lambda i: (0,) * len(shape))
    wshape = (2, C_BLOCKS, C_BLOCK_DIM, C_BLOCK_DIM)
    vec2 = full((2, C_WIDTH))
    return _call(body, name=name, grid=(t // tm,),
                 in_specs=[row] * 5 + [full(wshape), vec2, full(wshape), vec2, vec2],
                 out_specs=(row,) * 5 + (vec2,) * 3,
                 out_shape=(_sds((t, C_WIDTH)),) * 5 + (_sds((2, C_WIDTH)),) * 3)(
        xc, g0, hs0, g1, hs1, wa, ba, wx, bx, lam)


def lru_out_fwd(h0, h1, y, name):
    t = y.shape[0]
    tm = min(256, t)

    def body(h0_ref, h1_ref, y_ref, o_ref):
        o_ref[...] = (h0_ref[...] + h1_ref[...]) * _gelu(y_ref[...])

    row = pl.BlockSpec((tm, C_WIDTH), lambda i: (i, 0))
    return _call(body, name=name, grid=(t // tm,), in_specs=[row] * 3, out_specs=row,
                 out_shape=_sds((t, C_WIDTH)))(h0, h1, y)


def lru_out_bwd(h0, h1, y, dout, name):
    t = y.shape[0]
    tm = min(256, t)

    def body(h0_ref, h1_ref, y_ref, d_ref, dh_ref, dy_ref):
        yv = y_ref[...]
        dv = d_ref[...]
        dh_ref[...] = dv * _gelu(yv)
        dy_ref[...] = dv * (h0_ref[...] + h1_ref[...]) * _gelu_grad(yv)

    row = pl.BlockSpec((tm, C_WIDTH), lambda i: (i, 0))
    return _call(body, name=name, grid=(t // tm,), in_specs=[row] * 4, out_specs=(row, row),
                 out_shape=(_sds((t, C_WIDTH)),) * 2)(h0, h1, y, dout)


def conv_bwd(dxc, dp2, dp1, dm1, x0, xm2, xm1, xp1, cw, name):
    t = x0.shape[0]
    tm = min(256, t)

    def body(d_ref, dp2_ref, dp1_ref, dm1_ref, x0_ref, xm2_ref, xm1_ref, xp1_ref, cw_ref, dx_ref, dcw_ref, dcb_ref):
        @pl.when(pl.program_id(0) == 0)
        def _():
            dcw_ref[...] = jnp.zeros_like(dcw_ref)
            dcb_ref[...] = jnp.zeros_like(dcb_ref)

        dv = d_ref[...]
        dx_ref[...] = (dp2_ref[...] * cw_ref[0:1, :] + dp1_ref[...] * cw_ref[1:2, :] + dv * cw_ref[2:3, :]
                       + dm1_ref[...] * cw_ref[3:4, :])
        for j, x_ref in enumerate((xm2_ref, xm1_ref, x0_ref, xp1_ref)):
            dcw_ref[j:j + 1, :] += jnp.sum(dv * x_ref[...], axis=0, keepdims=True)
        dcb_ref[...] += jnp.sum(dv, axis=0, keepdims=True)

    row = pl.BlockSpec((tm, C_WIDTH), lambda i: (i, 0))
    cws = pl.BlockSpec((4, C_WIDTH), lambda i: (0, 0))
    cbs = pl.BlockSpec((1, C_WIDTH), lambda i: (0, 0))
    return _call(body, name=name, grid=(t // tm,), in_specs=[row] * 8 + [cws], out_specs=(row, cws, cbs),
                 out_shape=(_sds((t, C_WIDTH)), _sds((4, C_WIDTH)), _sds((1, C_WIDTH))))(
        dxc, dp2, dp1, dm1, x0, xm2, xm1, xp1, cw)


def _my_place():
    return lax.axis_index("x"), lax.axis_index("y"), lax.axis_index("c")


def all_gather(xs, name):
    r, c = xs.shape

    def body(x_ref, out_ref, send_sems, recv_sems, local_sem):
        x, y, cc = _my_place()
        me, sibling = (x, y, cc), (x, y, 1 - cc)
        chips = [(1 - x, y), (x, 1 - y), (1 - x, 1 - y)]

        def slot(px, py, pc):
            return out_ref.at[4 * px + 2 * py + pc]

        def copy(k, block, to, src=None):
            return pltpu.make_async_remote_copy(
                src_ref=slot(*block) if src is None else src, dst_ref=slot(*block),
                send_sem=send_sems.at[k], recv_sem=recv_sems.at[k], device_id=to, device_id_type=MESH)

        mine = pltpu.make_async_copy(x_ref, slot(*me), local_sem)
        mine.start()
        first = [copy(0, me, sibling, src=x_ref)]
        first += [copy(1 + j, me, (*chip, cc), src=x_ref) for j, chip in enumerate(chips)]
        for cp in first:
            cp.start()
        passed = [copy(4 + j, (*chip, cc), sibling) for j, chip in enumerate(chips)]
        for j, chip in enumerate(chips):
            copy(1 + j, (*chip, cc), me).wait_recv()
            passed[j].start()
        copy(0, sibling, me).wait_recv()
        for j, chip in enumerate(chips):
            copy(4 + j, (*chip, 1 - cc), me).wait_recv()
        for cp in first + passed:
            cp.wait_send()
        mine.wait()

    return _call(body, name=name, in_specs=[pl.BlockSpec(memory_space=pl.ANY)],
                 out_specs=pl.BlockSpec(memory_space=pl.ANY), out_shape=_sds((N_DEV, r, c), xs.dtype),
                 scratch_shapes=[pltpu.SemaphoreType.DMA((7,)), pltpu.SemaphoreType.DMA((7,)),
                                 pltpu.SemaphoreType.DMA])(xs)


def exchange_sibling(gw, name):
    _, r, c = gw.shape
    g5 = gw.reshape(4, 2, r, c)

    def body(g_ref, out_ref, send_sem, recv_sem, local_sem):
        x, y, cc = _my_place()
        keep = pltpu.make_async_copy(g_ref.at[:, pl.ds(cc, 1)], out_ref.at[0], local_sem)
        keep.start()
        swap = pltpu.make_async_remote_copy(
            src_ref=g_ref.at[:, pl.ds(1 - cc, 1)], dst_ref=out_ref.at[1], send_sem=send_sem, recv_sem=recv_sem,
            device_id=(x, y, 1 - cc), device_id_type=MESH)
        swap.start()
        swap.wait()
        keep.wait()

    out = _call(body, name=name, in_specs=[pl.BlockSpec(memory_space=pl.ANY)],
                out_specs=pl.BlockSpec(memory_space=pl.ANY), out_shape=_sds((2, 4, 1, r, c), gw.dtype),
                scratch_shapes=[pltpu.SemaphoreType.DMA, pltpu.SemaphoreType.DMA, pltpu.SemaphoreType.DMA])(g5)
    return out.reshape(2, 4, r, c)


def exchange_chips(p, name):
    _, r, c = p.shape

    def body(p_ref, out_ref, send_sems, recv_sems, local_sem):
        x, y, cc = _my_place()
        chips = [(1 - x, y), (x, 1 - y), (1 - x, 1 - y)]
        keep = pltpu.make_async_copy(p_ref.at[2 * x + y], out_ref.at[3], local_sem)
        keep.start()
        copies = [pltpu.make_async_remote_copy(
            src_ref=p_ref.at[2 * px + py], dst_ref=out_ref.at[j], send_sem=send_sems.at[j], recv_sem=recv_sems.at[j],
            device_id=(px, py, cc), device_id_type=MESH) for j, (px, py) in enumerate(chips)]
        for cp in copies:
            cp.start()
        for cp in copies:
            cp.wait()
        keep.wait()

    return _call(body, name=name, in_specs=[pl.BlockSpec(memory_space=pl.ANY)],
                 out_specs=pl.BlockSpec(memory_space=pl.ANY), out_shape=_sds((4, r, c), p.dtype),
                 scratch_shapes=[pltpu.SemaphoreType.DMA((3,)), pltpu.SemaphoreType.DMA((3,)),
                                 pltpu.SemaphoreType.DMA])(p)


def reduce_scatter(gw, name):
    _, r, c = gw.shape
    halves = exchange_sibling(gw, name + "_sibling").reshape(2, 4 * r, c)
    chip_sum = sum_lead(halves, (0, 1), BF16, name + "_add2").reshape(4, r, c)
    parts = exchange_chips(chip_sum, name + "_chips")
    return sum_lead(parts, (3, 0, 1, 2), F32, name + "_add4")


def _pack(arrs):
    flat = jnp.concatenate([a.reshape(-1).astype(F32) for a in arrs])
    n = flat.shape[0]
    pad = (-n) % 1024
    return jnp.pad(flat, (0, pad)).reshape(-1, 128)


def _unpack(packed, shapes):
    flat = packed.reshape(-1)
    out, off = [], 0
    for s in shapes:
        n = int(np.prod(s))
        out.append(flat[off:off + n].reshape(s))
        off += n
    return out


def _t5_bucket(rel):
    nb = N_BUCKETS // 2
    max_exact = nb // 2
    ret = jnp.where(rel > 0, nb, 0)
    n = jnp.abs(rel)
    nf = jnp.maximum(n, 1).astype(jnp.float32)
    large = max_exact + (jnp.log(nf / max_exact) / math.log(MAX_DISTANCE / max_exact)
                         * (nb - max_exact)).astype(jnp.int32)
    large = jnp.minimum(large, nb - 1)
    return ret + jnp.where(n < max_exact, n, large)


SMALL_SHARDED = ("gla_w2_f", "gla_w2_b", "conv_w", "lru_ba", "lru_bx", "lru_lambda")
SMALL_REPL = ("rel_bias", "attn_sink", "gla_b2_f", "gla_b2_b", "gla_norm", "conv_b", "lru_wa", "lru_wx",
              "norm_mix_pre", "norm_mix_post", "norm_mem", "norm_x_pre", "norm_x_post", "norm_ff_pre", "norm_ff_post")
BIG = ("w_in", "w_out", "xq", "xk", "xv", "xo", "w_up", "w_down")
WEIGHTS = ['rel_bias', 'w_in', 'w_out', 'attn_sink', 'gla_w2_f', 'gla_b2_f', 'gla_w2_b', 'gla_b2_b', 'gla_norm',
           'conv_w', 'conv_b', 'lru_wa', 'lru_ba', 'lru_wx', 'lru_bx', 'lru_lambda', 'xq', 'xk', 'xv', 'xo', 'w_up',
           'w_down', 'norm_mix_pre', 'norm_mix_post', 'norm_mem', 'norm_x_pre', 'norm_x_post', 'norm_ff_pre',
           'norm_ff_post']


def _step(x, mem, loss_target, w, m, v):
    depth = w["w_in"].shape[0]
    t, d = x.shape[1], x.shape[2]
    ml = mem.shape[1]
    rx = d // N_DEV
    rf = w["w_up"].shape[2]
    r_out = D_MIX // N_DEV
    x = x.reshape(t, d)
    mem = mem.reshape(ml, d)
    loss_target = loss_target.reshape(t, d)
    my_idx = 4 * lax.axis_index("x") + 2 * lax.axis_index("y") + lax.axis_index("c")

    off_in, off_out = 0, W_IN_ROWS
    off_xq = off_out + r_out
    off_xk, off_xv, off_xo = off_xq + rx, off_xq + 2 * rx, off_xq + 3 * rx
    off_up = off_xq + 4 * rx
    off_down = off_up + rf
    r_tot = off_down + rf

    sh_shapes = [w[n].shape for n in SMALL_SHARDED]
    gathered = all_gather(_pack([w[n] for n in SMALL_SHARDED]), "ag_small")
    per_dev = [_unpack(gathered[j], sh_shapes) for j in range(N_DEV)]
    full = {n: jnp.concatenate([per_dev[j][i] for j in range(N_DEV)], axis=-1) for i, n in enumerate(SMALL_SHARDED)}
    for n in SMALL_REPL:
        full[n] = w[n]

    gws = []
    for l in range(depth):
        w_in_t = jnp.pad(w["w_in"][l].T, ((0, W_IN_ROWS - W_IN_SHARD), (0, 0)))
        blk = jnp.concatenate([w_in_t, w["w_out"][l], w["xq"][l], w["xk"][l], w["xv"][l], w["xo"][l],
                               w["w_up"][l].T, w["w_down"][l]], axis=0).astype(BF16)
        gws.append(all_gather(blk, "ag_weights"))

    qi = jnp.arange(BLOCK)[:, None]
    kj = jnp.arange(3 * BLOCK)[None, :]
    onehot_t = (jnp.arange(N_BUCKETS)[:, None] == _t5_bucket(kj - BLOCK - qi).reshape(1, -1)).astype(F32)
    bias = mm_plain(full["rel_bias"].T, onehot_t, "rel_bias_lookup", hi=True, tn=3 * BLOCK * 16)
    bias = bias.reshape(A_HEADS, BLOCK, 3 * BLOCK)
    bias_t = jnp.transpose(bias, (0, 2, 1))

    def sink_rows(sink):
        s = jnp.broadcast_to(sink.reshape(A_KV_HEADS, A_GROUP, 1), (A_KV_HEADS, A_GROUP, 128))
        return jnp.pad(s, ((0, 0), (0, 8 - A_GROUP), (0, 0)))

    def split_proj(p):
        outs, off = [], 0
        for s in SPLIT_SIZES:
            outs.append(p[:, off:off + s])
            off += s
        return outs

    def lead(a):
        return a.reshape(a.shape[0], C_WIDTH // 128, 128)

    saved = []
    h = rms_fwd(x, full["norm_mix_pre"][0], "rms_first")
    for l in range(depth):
        gw = gws[l]
        sv = {"x": x, "h_in": h}
        proj_pad = mm_wn(h, gw, off_in, W_IN_ROWS, "mm_w_in")
        proj = proj_pad.reshape(t, N_DEV, W_IN_ROWS)[:, :, :W_IN_SHARD].reshape(t, D_IN)
        aq, ak, av, bq, bk, bv, bg, zf, zb, cx, cy = split_proj(proj)
        sv.update(aq=aq, ak=ak, av=av, bq=bq, bk=bk, bv=bv, bg=bg, zf=zf, zb=zb, cx=cx, cy=cy)
        sink_b = sink_rows(full["attn_sink"][l])
        oa = attn_fwd(aq, ak, av, bias, sink_b, "attn_fwd")
        la_f, la_b = gla_gates_fwd(zf, zb, full["gla_w2_f"][l], full["gla_b2_f"][l], full["gla_w2_b"][l],
                                   full["gla_b2_b"][l], "gla_gates_fwd")
        bk_t = bk.T
        gla = {}
        for nm, la, bdir in (("f", la_f, False), ("b", la_b, True)):
            la_t = la.T
            u, dec = gla_outer(bk_t, la_t, bv, "gla_outer_k_" + nm, bdir, "khat")
            sp = scan_lead(dec, u, "gla_state_scan_" + nm, reverse=bdir, inclusive=False)
            o_dir = gla_inner_fwd(bq, bk, bv, la, sp, "gla_inner_fwd_" + nm, bdir)
            gla[nm] = dict(la=la, la_t=la_t, dec=dec, sp=sp, o=o_dir)
        ob = gla_out_fwd(gla["f"]["o"], gla["b"]["o"], bg, full["gla_norm"][l], "gla_out_fwd")
        sv["gla"] = gla
        xm2, xm1, xp1 = _shift(cx, -2), _shift(cx, -1), _shift(cx, 1)
        xc, a0, u0, a1, u1 = lru_gates_fwd(cx, xm2, xm1, xp1, full["conv_w"][l], full["conv_b"][l], full["lru_wa"][l],
                                           full["lru_ba"][l], full["lru_wx"][l], full["lru_bx"][l],
                                           full["lru_lambda"][l], "lru_gates_fwd")
        h0 = scan_lead(lead(a0), lead(u0), "lru_scan_fwd", reverse=False, inclusive=True).reshape(t, C_WIDTH)
        h1 = scan_lead(lead(a1), lead(u1), "lru_scan_rev", reverse=True, inclusive=True).reshape(t, C_WIDTH)
        oc = lru_out_fwd(h0, h1, cy, "lru_out_fwd")
        sv.update(xm2=xm2, xm1=xm1, xp1=xp1, xc=xc, a0=a0, a1=a1, h0=h0, h1=h1, oa=oa)
        cat = jnp.concatenate([oa, ob, oc], axis=1)
        mixed = mm_wk(cat, gw, off_out, r_out, "mm_w_out")
        x1, h2 = resid_rms(x, mixed, full["norm_mix_post"][l], full["norm_x_pre"][l], "resid_rms")
        sv.update(cat=cat, mixed=mixed, x1=x1, h2=h2)
        memn = rms_fwd(mem, full["norm_mem"][l], "rms_mem")
        q = mm_wk(h2, gw, off_xq, rx, "mm_xq")
        k = mm_wk(memn, gw, off_xk, rx, "mm_xkv")
        vv = mm_wk(memn, gw, off_xv, rx, "mm_xkv")
        ox = xattn_fwd(q, k, vv, "xattn_fwd")
        ca = mm_wk(ox, gw, off_xo, rx, "mm_xo")
        x2, h3 = resid_rms(x1, ca, full["norm_x_post"][l], full["norm_ff_pre"][l], "resid_rms")
        sv.update(memn=memn, q=q, k=k, v=vv, ox=ox, ca=ca, x2=x2, h3=h3)
        up = mm_wn(h3, gw, off_up, rf, "mm_w_up")
        ff = mm_wk(up, gw, off_down, rf, "mm_w_down", jb=max(1, min(N_DEV, 2048 // rf)), relu2=True)
        if l + 1 < depth:
            x, h = resid_rms(x2, ff, full["norm_ff_post"][l], full["norm_mix_pre"][l + 1], "resid_rms")
        else:
            x = resid_rms(x2, ff, full["norm_ff_post"][l], None, "resid_rms_last")
        sv.update(up=up, ff=ff)
        saved.append(sv)

    dx, loss_local = loss_and_grad(x, loss_target, "loss")
    loss = lax.psum(loss_local, AXES)

    grads = {n: [None] * depth for n in WEIGHTS if n != "rel_bias"}
    dbias_total = None
    big_grads = [None] * depth
    bf = lambda a: a.astype(BF16)
    for l in reversed(range(depth)):
        gw = gws[l]
        sv = saved[l]
        dff, grads["norm_ff_post"][l] = rms_bwd(sv["ff"], full["norm_ff_post"][l], dx, "rms_bwd")
        dup = mm_wn(dff, gw, off_down, rf, "mm_w_down_dx", relu_grad_of=sv["up"])
        up_t = sv["up"].T
        act_t = bf(jnp.square(jnp.maximum(up_t, 0.0)))
        g_down = mm_plain(act_t, dff, "mm_dw_f", out_dtype=BF16)
        g_up_t = mm_plain(bf(dup.T), sv["h3"], "mm_dw_f", out_dtype=BF16)
        dh3 = mm_wk(dup, gw, off_up, rf, "mm_w_up_dx", jb=max(1, min(N_DEV, 2048 // rf)))
        dx2, grads["norm_ff_pre"][l] = rms_bwd(sv["x2"], full["norm_ff_pre"][l], dh3, "rms_bwd_add", add=dx)
        dca, grads["norm_x_post"][l] = rms_bwd(sv["ca"], full["norm_x_post"][l], dx2, "rms_bwd")
        dox = mm_wn(dca, gw, off_xo, rx, "mm_x_dx")
        g_xo = mm_plain(bf(sv["ox"].T), dca, "mm_dw_d", out_dtype=BF16)
        dq, dk, dv = xattn_bwd(sv["q"], sv["k"], sv["v"], sv["ox"], dox, "xattn_bwd")
        g_xq = mm_plain(bf(sv["h2"].T), dq, "mm_dw_d", out_dtype=BF16)
        memn_t = bf(sv["memn"].T)
        g_xk = mm_plain(memn_t, dk, "mm_dw_mem", out_dtype=BF16)
        g_xv = mm_plain(memn_t, dv, "mm_dw_mem", out_dtype=BF16)
        dh2 = mm_wn(dq, gw, off_xq, rx, "mm_x_dx")
        dmem_k = mm_wn(dk, gw, off_xk, rx, "mm_x_dx_mem")
        dmem_v = mm_wn(dv, gw, off_xv, rx, "mm_x_dx_mem")
        _, grads["norm_mem"][l] = rms_bwd(mem, full["norm_mem"][l], dmem_k, "rms_bwd_mem", dy2=dmem_v)
        dx1, grads["norm_x_pre"][l] = rms_bwd(sv["x1"], full["norm_x_pre"][l], dh2, "rms_bwd_add", add=dx2)
        dmixed, grads["norm_mix_post"][l] = rms_bwd(sv["mixed"], full["norm_mix_post"][l], dx1, "rms_bwd")
        dcat = mm_wn(dmixed, gw, off_out, r_out, "mm_w_out_dx")
        g_out = mm_plain(bf(sv["cat"].T), dmixed, "mm_dw_d", out_dtype=BF16)
        doa, dob, doc = dcat[:, :A_Q], dcat[:, A_Q:A_Q + B_V], dcat[:, A_Q + B_V:]
        daq, dak, dav, dbias, dsink = attn_bwd(sv["aq"], sv["ak"], sv["av"], bias, bias_t,
                                               sink_rows(full["attn_sink"][l]), doa, sv["oa"], "attn_bwd")
        grads["attn_sink"][l] = dsink[:, :A_GROUP, 0].reshape(A_HEADS)
        dbias_total = dbias if dbias_total is None else add_n([dbias_total, dbias], F32, "add_dbias")
        gf, gb = sv["gla"]["f"], sv["gla"]["b"]
        do_gla, dbg, dgn = gla_out_bwd(gf["o"], gb["o"], sv["bg"], full["gla_norm"][l], dob, "gla_out_bwd")
        grads["gla_norm"][l] = dgn.reshape(B_V)
        bq_t = sv["bq"].T
        acc = None
        dlas = {}
        for nm, gd, bdir in (("f", gf, False), ("b", gb, True)):
            wq = gla_outer(bq_t, gd["la_t"], do_gla, "gla_outer_q_" + nm, bdir, "qtil")
            gs = scan_lead(gd["dec"], wq, "gla_adj_scan_" + nm, reverse=not bdir, inclusive=False)
            dbq, dbk, dbv, dlas[nm] = gla_inner_bwd(sv["bq"], sv["bk"], sv["bv"], gd["la"], do_gla, gd["sp"], gs,
                                                    gd["dec"], "gla_inner_bwd_" + nm, bdir, add=acc)
            acc = (dbq, dbk, dbv)
        dzf, dzb, dpre_f, dpre_b, db2f, db2b = gla_gates_bwd(
            sv["zf"], sv["zb"], full["gla_w2_f"][l], full["gla_b2_f"][l], full["gla_w2_b"][l], full["gla_b2_b"][l],
            dlas["f"], dlas["b"], "gla_gates_bwd")
        grads["gla_b2_f"][l] = db2f.reshape(B_QK)
        grads["gla_b2_b"][l] = db2b.reshape(B_QK)
        grads["gla_w2_f"][l] = mm_plain(sv["zf"].T, dpre_f, "mm_dw_gate", hi=True)
        grads["gla_w2_b"][l] = mm_plain(sv["zb"].T, dpre_b, "mm_dw_gate", hi=True)
        dh, dcy = lru_out_bwd(sv["h0"], sv["h1"], sv["cy"], doc, "lru_out_bwd")
        g0 = scan_lead(lead(_shift(sv["a0"], 1)), lead(dh), "lru_scan_rev", reverse=True,
                       inclusive=True).reshape(t, C_WIDTH)
        g1 = scan_lead(lead(_shift(sv["a1"], -1)), lead(dh), "lru_scan_fwd", reverse=False,
                       inclusive=True).reshape(t, C_WIDTH)
        dxc, dzr0, dzi0, dzr1, dzi1, dlam, dba, dbx = lru_gates_bwd(
            sv["xc"], g0, _shift(sv["h0"], -1), g1, _shift(sv["h1"], 1), full["lru_wa"][l], full["lru_ba"][l],
            full["lru_wx"][l], full["lru_bx"][l], full["lru_lambda"][l], "lru_gates_bwd")
        xc_t = bf(sv["xc"].T)
        grads["lru_wa"][l] = jnp.stack([blockdiag_dw(xc_t, dzr0, "lru_dw"), blockdiag_dw(xc_t, dzr1, "lru_dw")])
        grads["lru_wx"][l] = jnp.stack([blockdiag_dw(xc_t, dzi0, "lru_dw"), blockdiag_dw(xc_t, dzi1, "lru_dw")])
        grads["lru_lambda"][l], grads["lru_ba"][l], grads["lru_bx"][l] = dlam, dba, dbx
        dcx, dcw, dcb = conv_bwd(dxc, _shift(dxc, 2), _shift(dxc, 1), _shift(dxc, -1), sv["cx"], sv["xm2"],
                                 sv["xm1"], sv["xp1"], full["conv_w"][l], "conv_bwd")
        grads["conv_w"][l] = dcw
        grads["conv_b"][l] = dcb.reshape(C_WIDTH)
        dproj = jnp.concatenate([daq, dak, dav, dbq, dbk, dbv, dbg, dzf, dzb, dcx, dcy], axis=1)
        dproj_pad = jnp.pad(dproj.reshape(t, N_DEV, W_IN_SHARD),
                            ((0, 0), (0, 0), (0, W_IN_ROWS - W_IN_SHARD))).reshape(t, N_DEV * W_IN_ROWS)
        g_in_t = mm_plain(bf(dproj_pad.T), sv["h_in"], "mm_dw_in", out_dtype=BF16)
        dh1 = mm_wk(dproj_pad, gw, off_in, W_IN_ROWS, "mm_w_in_dx", jb=2)
        dx, grads["norm_mix_pre"][l] = rms_bwd(sv["x"], full["norm_mix_pre"][l], dh1, "rms_bwd_add", add=dx1)
        parts = [g_in_t, g_out, g_xq, g_xk, g_xv, g_xo, g_up_t, g_down]
        gpack = jnp.concatenate([p.reshape(N_DEV, p.shape[0] // N_DEV, d) for p in parts], axis=1)
        big_grads[l] = reduce_scatter(gpack, "rs_weights")

    grad_rel = mm_plain(dbias_total.reshape(A_HEADS, -1), onehot_t, "rel_bias_grad", tb=True, hi=True).T

    small_names = [n for n in WEIGHTS if n not in BIG]
    small_g = {"rel_bias": grad_rel}
    for n in small_names:
        if n != "rel_bias":
            small_g[n] = jnp.stack([g.reshape(full[n].shape[1:]) for g in grads[n]])
    shapes = [small_g[n].shape for n in small_names]
    packed = all_gather(_pack([small_g[n] for n in small_names]), "ag_small_grads")
    summed = sum_lead(packed, tuple(range(N_DEV)), F32, "add8_small")
    small_g = dict(zip(small_names, _unpack(summed, shapes)))
    for n in SMALL_SHARDED:
        wdt = w[n].shape[-1]
        small_g[n] = lax.dynamic_slice_in_dim(small_g[n], my_idx * wdt, wdt, axis=small_g[n].ndim - 1)

    grad_out, delta, new_m, new_v = {}, {}, {}, {}
    sshapes = [w[n].shape for n in small_names]
    ds, ms, vs = adamw(_pack([small_g[n] for n in small_names]), _pack([w[n] for n in small_names]),
                       _pack([m[n] for n in small_names]), _pack([v[n] for n in small_names]), "adamw_small")
    for n, g_, d_, m_, v_ in zip(small_names, [small_g[n] for n in small_names], _unpack(ds, sshapes),
                                 _unpack(ms, sshapes), _unpack(vs, sshapes)):
        grad_out[n], delta[n], new_m[n], new_v[n] = g_, d_, m_, v_

    def rows(l, off, r):
        return big_grads[l][off:off + r]

    big_g = {
        "w_in": jnp.stack([rows(l, off_in, W_IN_SHARD).T for l in range(depth)]),
        "w_out": jnp.stack([rows(l, off_out, r_out) for l in range(depth)]),
        "xq": jnp.stack([rows(l, off_xq, rx) for l in range(depth)]),
        "xk": jnp.stack([rows(l, off_xk, rx) for l in range(depth)]),
        "xv": jnp.stack([rows(l, off_xv, rx) for l in range(depth)]),
        "xo": jnp.stack([rows(l, off_xo, rx) for l in range(depth)]),
        "w_up": jnp.stack([rows(l, off_up, rf).T for l in range(depth)]),
        "w_down": jnp.stack([rows(l, off_down, rf) for l in range(depth)]),
    }
    for n in BIG:
        grad_out[n] = big_g[n]
        delta[n], new_m[n], new_v[n] = adamw(big_g[n], w[n], m[n], v[n], "adamw_" + n)

    return (loss, dx.reshape(1, t, d), *[grad_out[n] for n in WEIGHTS], *[delta[n] for n in WEIGHTS],
            *[new_m[n] for n in WEIGHTS], *[new_v[n] for n in WEIGHTS])


def kernel(x, mem, rel_bias, w_in, w_out, attn_sink, gla_w2_f, gla_b2_f, gla_w2_b, gla_b2_b, gla_norm, conv_w, conv_b, lru_wa, lru_ba, lru_wx, lru_bx, lru_lambda, xq, xk, xv, xo, w_up, w_down, norm_mix_pre, norm_mix_post, norm_mem, norm_x_pre, norm_x_post, norm_ff_pre, norm_ff_post, loss_target, m_rel_bias, m_w_in, m_w_out, m_attn_sink, m_gla_w2_f, m_gla_b2_f, m_gla_w2_b, m_gla_b2_b, m_gla_norm, m_conv_w, m_conv_b, m_lru_wa, m_lru_ba, m_lru_wx, m_lru_bx, m_lru_lambda, m_xq, m_xk, m_xv, m_xo, m_w_up, m_w_down, m_norm_mix_pre, m_norm_mix_post, m_norm_mem, m_norm_x_pre, m_norm_x_post, m_norm_ff_pre, m_norm_ff_post, v_rel_bias, v_w_in, v_w_out, v_attn_sink, v_gla_w2_f, v_gla_b2_f, v_gla_w2_b, v_gla_b2_b, v_gla_norm, v_conv_w, v_conv_b, v_lru_wa, v_lru_ba, v_lru_wx, v_lru_bx, v_lru_lambda, v_xq, v_xk, v_xv, v_xo, v_w_up, v_w_down, v_norm_mix_pre, v_norm_mix_post, v_norm_mem, v_norm_x_pre, v_norm_x_post, v_norm_ff_pre, v_norm_ff_post):
    given = dict(locals())
    w = {n: given[n] for n in WEIGHTS}
    m = {n: given["m_" + n] for n in WEIGHTS}
    v = {n: given["v_" + n] for n in WEIGHTS}
    return _step(x, mem, loss_target, w, m, v)
```

```python
import math

import jax
import jax.numpy as jnp
import numpy as np
from jax import lax
from jax.experimental import pallas as pl
from jax.experimental.pallas import tpu as pltpu

F32 = jnp.float32
BF16 = jnp.bfloat16
HI = lax.Precision.HIGHEST
NN = (((1,), (0,)), ((), ()))
NT = (((1,), (1,)), ((), ()))
MESH = pl.DeviceIdType.MESH
AXES = ("x", "y", "c")
N_DEV = 8

A_HEAD_DIM = 128
A_HEADS = 8
A_KV_HEADS = 2
A_GROUP = 4
WINDOW = 128
BLOCK = 128
N_BUCKETS = 32
MAX_DISTANCE = 128
B_HEADS = 4
B_KEY_DIM = 64
B_VAL_DIM = 128
GATE_RANK = 16
GATE_TAU = 16.0
GLA_CHUNK = 16
C_WIDTH = 512
C_BLOCKS = 4
C_BLOCK_DIM = 128
LRU_C = 8.0
X_HEADS = 4
EPS = 1e-6
NEG_INF = -1e30
A_Q = A_HEADS * A_HEAD_DIM
A_KV = A_KV_HEADS * A_HEAD_DIM
B_QK = B_HEADS * B_KEY_DIM
B_V = B_HEADS * B_VAL_DIM
SPLIT_SIZES = (A_Q, A_KV, A_KV, B_QK, B_QK, B_V, B_V, GATE_RANK, GATE_RANK, C_WIDTH, C_WIDTH)
D_IN = sum(SPLIT_SIZES)
D_MIX = A_Q + B_V + C_WIDTH
W_IN_SHARD = D_IN // N_DEV
W_IN_ROWS = 768
GLA_TILE = 128
CHUNKS_PER_TILE = GLA_TILE // GLA_CHUNK
EXP_CLAMP = 80.0

ADAM_LR = 0.001
ADAM_B1 = 0.9
ADAM_B2 = 0.999
ADAM_EPS = 1e-08
ADAM_WD = 0.01
ADAM_STEP = 10

VMEM_LIMIT_BYTES = 48 * 1024 * 1024
SIBLING_STREAMS = 16
PACK_ELEMS = 128 * 2048


def _call(body, **kw):
    return pl.pallas_call(body, **kw)


def _cparams():
    return pltpu.CompilerParams(vmem_limit_bytes=VMEM_LIMIT_BYTES)


def _dot(a, b, dims=NN, hi=False):
    if hi:
        return lax.dot_general(a, b, dims, precision=HI, preferred_element_type=F32)
    return lax.dot_general(a.astype(BF16), b.astype(BF16), dims, preferred_element_type=F32)


def _sds(shape, dtype=F32):
    return jax.ShapeDtypeStruct(tuple(shape), dtype)


def _row_tile(rows, cols, target_elems=1 << 18):
    want = max(8, target_elems // max(cols, 1))
    if rows <= want:
        return rows
    t = (want // 8) * 8
    while t >= 8:
        if rows % t == 0:
            return t
        t -= 8
    return rows


def _expm1(x):
    poly = x * (1.0 + x * (1.0 / 2 + x * (1.0 / 6 + x * (1.0 / 24 + x * (1.0 / 120 + x * (
        1.0 / 720 + x * (1.0 / 5040 + x * (1.0 / 40320))))))))
    return jnp.where(jnp.abs(x) < 0.3, poly, jnp.exp(x) - 1.0)


def _log1p(e):
    w = 1.0 + e
    return jnp.where(w == 1.0, e, jnp.log(w) * e / (w - 1.0))


def _softplus(x):
    return jnp.maximum(x, 0.0) + _log1p(jnp.exp(-jnp.abs(x)))


def _sigmoid(x):
    return jax.nn.sigmoid(x)


GELU_K = math.sqrt(2.0 / math.pi)


def _gelu(y):
    t = jnp.tanh(GELU_K * (y + 0.044715 * y * y * y))
    return 0.5 * y * (1.0 + t)


def _gelu_grad(y):
    t = jnp.tanh(GELU_K * (y + 0.044715 * y * y * y))
    return 0.5 * (1.0 + t) + 0.5 * y * (1.0 - t * t) * GELU_K * (1.0 + 3 * 0.044715 * y * y)


def rms_fwd(x, g, name):
    m, d = x.shape
    tm = _row_tile(m, d)

    def body(x_ref, g_ref, o_ref):
        xv = x_ref[...]
        r = lax.rsqrt(jnp.mean(xv * xv, axis=1, keepdims=True) + EPS)
        o_ref[...] = xv * r * g_ref[...]

    return _call(body, name=name, grid=(m // tm,),
                 in_specs=[pl.BlockSpec((tm, d), lambda i: (i, 0)), pl.BlockSpec((1, d), lambda i: (0, 0))],
                 out_specs=pl.BlockSpec((tm, d), lambda i: (i, 0)),
                 out_shape=_sds((m, d)))(x, g.reshape(1, d))


def resid_rms(xres, mid, g_post, g_pre, name):
    m, d = xres.shape
    tm = _row_tile(m, d)
    with_pre = g_pre is not None

    def body(*refs):
        if with_pre:
            x_ref, m_ref, gp_ref, gn_ref, xo_ref, h_ref = refs
        else:
            x_ref, m_ref, gp_ref, xo_ref = refs
        mv = m_ref[...]
        r = lax.rsqrt(jnp.mean(mv * mv, axis=1, keepdims=True) + EPS)
        xn = x_ref[...] + mv * r * gp_ref[...]
        xo_ref[...] = xn
        if with_pre:
            r2 = lax.rsqrt(jnp.mean(xn * xn, axis=1, keepdims=True) + EPS)
            h_ref[...] = xn * r2 * gn_ref[...]

    row = pl.BlockSpec((tm, d), lambda i: (i, 0))
    vec = pl.BlockSpec((1, d), lambda i: (0, 0))
    ins = [xres, mid, g_post.reshape(1, d)] + ([g_pre.reshape(1, d)] if with_pre else [])
    in_specs = [row, row, vec] + ([vec] if with_pre else [])
    if with_pre:
        return _call(body, name=name, grid=(m // tm,), in_specs=in_specs, out_specs=(row, row),
                     out_shape=(_sds((m, d)), _sds((m, d))))(*ins)
    return _call(body, name=name, grid=(m // tm,), in_specs=in_specs, out_specs=row,
                 out_shape=_sds((m, d)))(*ins)


def rms_bwd(x, g, dy, name, dy2=None, add=None):
    m, d = x.shape
    tm = _row_tile(m, d)
    has2, hasadd = dy2 is not None, add is not None

    def body(*refs):
        it = iter(refs)
        x_ref, g_ref, dy_ref = next(it), next(it), next(it)
        dy2_ref = next(it) if has2 else None
        add_ref = next(it) if hasadd else None
        dx_ref, dg_ref = next(it), next(it)
        xv = x_ref[...]
        dyv = dy_ref[...]
        if has2:
            dyv = dyv + dy2_ref[...]
        r = lax.rsqrt(jnp.mean(xv * xv, axis=1, keepdims=True) + EPS)
        xh = xv * r
        dxh = dyv * g_ref[...]
        dx = r * (dxh - xh * jnp.mean(dxh * xh, axis=1, keepdims=True))
        if hasadd:
            dx = dx + add_ref[...]
        dx_ref[...] = dx
        part = jnp.sum(dyv * xh, axis=0, keepdims=True)

        @pl.when(pl.program_id(0) == 0)
        def _():
            dg_ref[...] = part

        @pl.when(pl.program_id(0) > 0)
        def _():
            dg_ref[...] += part

    row = pl.BlockSpec((tm, d), lambda i: (i, 0))
    vec = pl.BlockSpec((1, d), lambda i: (0, 0))
    ins = [x, g.reshape(1, d), dy] + ([dy2] if has2 else []) + ([add] if hasadd else [])
    in_specs = [row, vec, row] + ([row] if has2 else []) + ([row] if hasadd else [])
    return _call(body, name=name, grid=(m // tm,), in_specs=in_specs, out_specs=(row, vec),
                 out_shape=(_sds((m, d)), _sds((1, d))))(*ins)


def loss_and_grad(y, target, name):
    m, d = y.shape
    tm = _row_tile(m, d)

    def body(y_ref, t_ref, dy_ref, l_ref):
        e = y_ref[...] - t_ref[...]
        dy_ref[...] = e * (1.0 / d)
        s = jnp.sum(jnp.sum(e * e, axis=1, keepdims=True), axis=0, keepdims=True) * (0.5 / d)
        part = jnp.broadcast_to(s, (1, 128))

        @pl.when(pl.program_id(0) == 0)
        def _():
            l_ref[...] = part

        @pl.when(pl.program_id(0) > 0)
        def _():
            l_ref[...] += part

    row = pl.BlockSpec((tm, d), lambda i: (i, 0))
    dy, l = _call(body, name=name, grid=(m // tm,), in_specs=[row, row],
                  out_specs=(row, pl.BlockSpec((1, 128), lambda i: (0, 0))),
                  out_shape=(_sds((m, d)), _sds((1, 128))))(y, target)
    return dy, l[0, 0]


def adamw(g, w, m, v, name):
    shape = w.shape
    cols = shape[-1]
    rows = int(np.prod(shape[:-1]))
    tm = _row_tile(rows, cols)
    c1 = 1.0 - ADAM_B1 ** ADAM_STEP
    c2 = 1.0 - ADAM_B2 ** ADAM_STEP

    def body(g_ref, w_ref, m_ref, v_ref, d_ref, mo_ref, vo_ref):
        gv = g_ref[...]
        mn = ADAM_B1 * m_ref[...] + (1.0 - ADAM_B1) * gv
        vn = ADAM_B2 * v_ref[...] + (1.0 - ADAM_B2) * (gv * gv)
        m_hat = mn / c1
        v_hat = vn / c2
        d_ref[...] = -ADAM_LR * (m_hat / (jnp.sqrt(v_hat) + ADAM_EPS) + ADAM_WD * w_ref[...])
        mo_ref[...] = mn
        vo_ref[...] = vn

    row = pl.BlockSpec((tm, cols), lambda i: (i, 0))
    outs = _call(body, name=name, grid=(rows // tm,), in_specs=[row] * 4, out_specs=(row,) * 3,
                 out_shape=(_sds((rows, cols)),) * 3)(*[a.reshape(rows, cols) for a in (g, w, m, v)])
    return tuple(o.reshape(shape) for o in outs)


def sum_lead(x, order, out_dtype, name):
    n, rows, cols = x.shape
    tm = _row_tile(rows, cols)

    def body(x_ref, o_ref):
        acc = x_ref[order[0]].astype(F32)
        for i in order[1:]:
            acc = acc + x_ref[i].astype(F32)
        o_ref[...] = acc.astype(out_dtype)

    return _call(body, name=name, grid=(rows // tm,), in_specs=[pl.BlockSpec((n, tm, cols), lambda i: (0, i, 0))],
                 out_specs=pl.BlockSpec((tm, cols), lambda i: (i, 0)), out_shape=_sds((rows, cols), out_dtype))(x)


def add_n(xs, out_dtype, name):
    shape = xs[0].shape
    cols = shape[-1]
    rows = int(np.prod(shape[:-1]))
    tm = _row_tile(rows, cols)
    n = len(xs)

    def body(*refs):
        acc = refs[0][...].astype(F32)
        for r in refs[1:n]:
            acc = acc + r[...].astype(F32)
        refs[n][...] = acc.astype(out_dtype)

    row = pl.BlockSpec((tm, cols), lambda i: (i, 0))
    out = _call(body, name=name, grid=(rows // tm,), in_specs=[row] * n, out_specs=row,
                out_shape=_sds((rows, cols), out_dtype))(*[a.reshape(rows, cols) for a in xs])
    return out.reshape(shape)


def mm_plain(a, b, name, tb=False, out_dtype=F32, hi=False, tm=512, tn=512):
    m, k = a.shape
    n = b.shape[0] if tb else b.shape[1]
    tm, tn = min(tm, m), min(tn, n)

    def body(a_ref, b_ref, o_ref):
        o_ref[...] = _dot(a_ref[...], b_ref[...], NT if tb else NN, hi).astype(out_dtype)

    b_spec = pl.BlockSpec((tn, k), lambda i, j: (j, 0)) if tb else pl.BlockSpec((k, tn), lambda i, j: (0, j))
    return _call(body, name=name, grid=(m // tm, n // tn),
                 in_specs=[pl.BlockSpec((tm, k), lambda i, j: (i, 0)), b_spec],
                 out_specs=pl.BlockSpec((tm, tn), lambda i, j: (i, j)),
                 out_shape=_sds((m, n), out_dtype), compiler_params=_cparams())(a, b)


def mm_wk(a, gw, off, r, name, jb=N_DEV, relu2=False, tm=512, tn=512):
    m = a.shape[0]
    d = gw.shape[2]
    tm, tn = min(tm, m), min(tn, d)
    nk = N_DEV // jb
    ob = off // r
    assert off % r == 0 and a.shape[1] == N_DEV * r

    def body(a_ref, b_ref, o_ref, *acc):
        av = a_ref[...]
        if relu2:
            av = jnp.square(jnp.maximum(av, 0.0))
        av = av.astype(BF16)
        p = _dot(av[:, 0:r], b_ref[0])
        for q in range(1, jb):
            p = p + _dot(av[:, q * r:(q + 1) * r], b_ref[q])
        if nk == 1:
            o_ref[...] = p
        else:
            kk = pl.program_id(2)

            @pl.when(kk == 0)
            def _():
                acc[0][...] = p

            @pl.when(kk > 0)
            def _():
                acc[0][...] += p

            @pl.when(kk == nk - 1)
            def _():
                o_ref[...] = acc[0][...]

    return _call(body, name=name, grid=(m // tm, d // tn, nk),
                 in_specs=[pl.BlockSpec((tm, jb * r), lambda i, j, k: (i, k)),
                           pl.BlockSpec((jb, r, tn), lambda i, j, k: (k, ob, j))],
                 out_specs=pl.BlockSpec((tm, tn), lambda i, j, k: (i, j)),
                 out_shape=_sds((m, d)),
                 scratch_shapes=([pltpu.VMEM((tm, tn), F32)] if nk > 1 else []),
                 compiler_params=_cparams())(a, gw)


def mm_wn(a, gw, off, r, name, relu_grad_of=None, tm=512):
    m, d = a.shape
    tm = min(tm, m)
    ob = off // r
    assert off % r == 0 and gw.shape[2] == d
    epi = relu_grad_of is not None

    def body(*refs):
        if epi:
            a_ref, b_ref, e_ref, o_ref = refs
        else:
            a_ref, b_ref, o_ref = refs
        p = _dot(a_ref[...], b_ref[...], NT)
        if epi:
            p = p * (2.0 * jnp.maximum(e_ref[...], 0.0))
        o_ref[...] = p

    blk = pl.BlockSpec((tm, r), lambda i, j: (i, j))
    in_specs = [pl.BlockSpec((tm, d), lambda i, j: (i, 0)), pl.BlockSpec((None, r, d), lambda i, j: (j, ob, 0))]
    ins = [a, gw]
    if epi:
        in_specs.append(blk)
        ins.append(relu_grad_of)
    return _call(body, name=name, grid=(m // tm, N_DEV), in_specs=in_specs, out_specs=blk,
                 out_shape=_sds((m, N_DEV * r)), compiler_params=_cparams())(*ins)


def blockdiag_dw(xt, dz, name):
    t = xt.shape[1]

    def body(a_ref, b_ref, o_ref):
        o_ref[...] = _dot(a_ref[...], b_ref[...])

    return _call(body, name=name, grid=(C_BLOCKS,),
                 in_specs=[pl.BlockSpec((C_BLOCK_DIM, t), lambda g: (g, 0)),
                           pl.BlockSpec((t, C_BLOCK_DIM), lambda g: (0, g))],
                 out_specs=pl.BlockSpec((None, C_BLOCK_DIM, C_BLOCK_DIM), lambda g: (g, 0, 0)),
                 out_shape=_sds((C_BLOCKS, C_BLOCK_DIM, C_BLOCK_DIM)))(xt, dz)


def _band_mask(n, nblk, transposed):
    shape = (3 * BLOCK, BLOCK) if transposed else (BLOCK, 3 * BLOCK)
    qi = lax.broadcasted_iota(jnp.int32, shape, 1 if transposed else 0)
    kj = lax.broadcasted_iota(jnp.int32, shape, 0 if transposed else 1)
    lo = jnp.where(n > 0, 0, BLOCK)
    hi = jnp.where(n < nblk - 1, 3 * BLOCK, 2 * BLOCK)
    return (jnp.abs(kj - BLOCK - qi) <= WINDOW) & (kj >= lo) & (kj < hi)


def _band_rows(ref, n, nblk):
    starts = [jnp.maximum(n - 1, 0), n, jnp.minimum(n + 1, nblk - 1)]
    return jnp.concatenate([ref[pl.ds(pl.multiple_of(s * BLOCK, BLOCK), BLOCK), :] for s in starts], axis=0)


def attn_fwd(q, k, v, bias, sink_b, name):
    t = q.shape[0]
    nblk = t // BLOCK
    scale = A_HEAD_DIM ** -0.5

    def body(q_ref, k_ref, v_ref, b_ref, s_ref, o_ref):
        n = pl.program_id(1)
        kb = _band_rows(k_ref, n, nblk).astype(BF16)
        vb = _band_rows(v_ref, n, nblk).astype(BF16)
        mask = _band_mask(n, nblk, False)
        for j in range(A_GROUP):
            sl = slice(j * A_HEAD_DIM, (j + 1) * A_HEAD_DIM)
            s = _dot(q_ref[:, sl], kb, NT) * scale + b_ref[j]
            s = jnp.where(mask, s, NEG_INF)
            sk = s_ref[j:j + 1, 0:1]
            mx = jnp.maximum(jnp.max(s, axis=1, keepdims=True), sk)
            p = jnp.exp(s - mx)
            den = jnp.sum(p, axis=1, keepdims=True) + jnp.exp(sk - mx)
            o_ref[:, sl] = _dot(p / den, vb)

    gw = A_GROUP * A_HEAD_DIM
    return _call(body, name=name, grid=(A_KV_HEADS, nblk),
                 in_specs=[pl.BlockSpec((BLOCK, gw), lambda g, n: (n, g)),
                           pl.BlockSpec((t, A_HEAD_DIM), lambda g, n: (0, g)),
                           pl.BlockSpec((t, A_HEAD_DIM), lambda g, n: (0, g)),
                           pl.BlockSpec((A_GROUP, BLOCK, 3 * BLOCK), lambda g, n: (g, 0, 0)),
                           pl.BlockSpec((None, 8, 128), lambda g, n: (g, 0, 0))],
                 out_specs=pl.BlockSpec((BLOCK, gw), lambda g, n: (n, g)),
                 out_shape=_sds((t, A_Q)))(q, k, v, bias, sink_b)


def attn_bwd(q, k, v, bias, bias_t, sink_b, do, o, name):
    t = q.shape[0]
    nblk = t // BLOCK
    scale = A_HEAD_DIM ** -0.5

    def body(q_ref, k_ref, v_ref, b_ref, bt_ref, s_ref, do_ref, o_ref, dq_ref, dk_ref, dv_ref, db_ref, ds_ref):
        n = pl.program_id(1)

        @pl.when(n == 0)
        def _():
            dk_ref[...] = jnp.zeros_like(dk_ref)
            dv_ref[...] = jnp.zeros_like(dv_ref)
            db_ref[...] = jnp.zeros_like(db_ref)
            ds_ref[...] = jnp.zeros_like(ds_ref)

        kb = _band_rows(k_ref, n, nblk).astype(BF16)
        vb = _band_rows(v_ref, n, nblk).astype(BF16)
        mask = _band_mask(n, nblk, False)
        mask_t = _band_mask(n, nblk, True)
        ones8 = jnp.ones((8, A_HEAD_DIM), F32)
        dkb = jnp.zeros((3 * BLOCK, A_HEAD_DIM), F32)
        dvb = jnp.zeros((3 * BLOCK, A_HEAD_DIM), F32)
        for j in range(A_GROUP):
            sl = slice(j * A_HEAD_DIM, (j + 1) * A_HEAD_DIM)
            qj = q_ref[:, sl].astype(BF16)
            doj = do_ref[:, sl]
            doo = doj * o_ref[:, sl]
            doj = doj.astype(BF16)
            sk = s_ref[j:j + 1, 0:1]
            s = jnp.where(mask, _dot(qj, kb, NT) * scale + b_ref[j], NEG_INF)
            mx = jnp.maximum(jnp.max(s, axis=1, keepdims=True), sk)
            p = jnp.exp(s - mx)
            den = jnp.sum(p, axis=1, keepdims=True) + jnp.exp(sk - mx)
            p = p / den
            psink = jnp.exp(sk - mx) / den
            delta = jnp.sum(doo, axis=1, keepdims=True)
            dsc = p * (_dot(doj, vb, NT) - delta)
            db_ref[j] += dsc
            ds_ref[j:j + 1, :] += jnp.broadcast_to(-jnp.sum(psink * delta, axis=0, keepdims=True), (1, 128))
            dq_ref[:, sl] = _dot(dsc, kb) * scale
            st = jnp.where(mask_t, _dot(kb, qj, NT) * scale + bt_ref[j], NEG_INF)
            mxt = jnp.maximum(jnp.max(st, axis=0, keepdims=True), sk)
            pt = jnp.exp(st - mxt)
            dent = jnp.sum(pt, axis=0, keepdims=True) + jnp.exp(sk - mxt)
            pt = pt / dent
            delta_t = _dot(ones8, doo, NT, hi=True)[0:1, :]
            dst = pt * (_dot(vb, doj, NT) - delta_t)
            dkb = dkb + _dot(dst, qj) * scale
            dvb = dvb + _dot(pt, doj)
        starts = [jnp.maximum(n - 1, 0), n, jnp.minimum(n + 1, nblk - 1)]
        for c, st_ in enumerate(starts):
            rows = pl.ds(pl.multiple_of(st_ * BLOCK, BLOCK), BLOCK)
            dk_ref[rows, :] += dkb[c * BLOCK:(c + 1) * BLOCK, :]
            dv_ref[rows, :] += dvb[c * BLOCK:(c + 1) * BLOCK, :]

    gw = A_GROUP * A_HEAD_DIM
    qspec = pl.BlockSpec((BLOCK, gw), lambda g, n: (n, g))
    kspec = pl.BlockSpec((t, A_HEAD_DIM), lambda g, n: (0, g))
    sspec = pl.BlockSpec((None, 8, 128), lambda g, n: (g, 0, 0))
    bspec = pl.BlockSpec((A_GROUP, BLOCK, 3 * BLOCK), lambda g, n: (g, 0, 0))
    btspec = pl.BlockSpec((A_GROUP, 3 * BLOCK, BLOCK), lambda g, n: (g, 0, 0))
    return _call(body, name=name, grid=(A_KV_HEADS, nblk),
                 in_specs=[qspec, kspec, kspec, bspec, btspec, sspec, qspec, qspec],
                 out_specs=(qspec, kspec, kspec, bspec, sspec),
                 out_shape=(_sds((t, A_Q)), _sds((t, A_KV)), _sds((t, A_KV)),
                            _sds((A_HEADS, BLOCK, 3 * BLOCK)), _sds((A_KV_HEADS, 8, 128))),
                 compiler_params=_cparams())(q, k, v, bias, bias_t, sink_b, do, o)


def xattn_fwd(q, k, v, name):
    t, d = q.shape
    ml = k.shape[0]
    dh = d // X_HEADS
    tq = min(256, t)
    scale = dh ** -0.5

    def body(q_ref, k_ref, v_ref, o_ref):
        s = _dot(q_ref[...], k_ref[...], NT) * scale
        p = jnp.exp(s - jnp.max(s, axis=1, keepdims=True))
        p = p / jnp.sum(p, axis=1, keepdims=True)
        o_ref[...] = _dot(p, v_ref[...])

    qspec = pl.BlockSpec((tq, dh), lambda h, i: (i, h))
    kspec = pl.BlockSpec((ml, dh), lambda h, i: (0, h))
    return _call(body, name=name, grid=(X_HEADS, t // tq), in_specs=[qspec, kspec, kspec], out_specs=qspec,
                 out_shape=_sds((t, d)))(q, k, v)


def xattn_bwd(q, k, v, o, do, name):
    t, d = q.shape
    ml = k.shape[0]
    dh = d // X_HEADS
    tq = min(256, t)
    scale = dh ** -0.5

    def body(q_ref, k_ref, v_ref, o_ref, do_ref, dq_ref, dk_ref, dv_ref):
        i = pl.program_id(1)
        qv, kv, vv = q_ref[...].astype(BF16), k_ref[...].astype(BF16), v_ref[...].astype(BF16)
        dov = do_ref[...]
        doo = dov * o_ref[...]
        dov = dov.astype(BF16)
        s = _dot(qv, kv, NT) * scale
        p = jnp.exp(s - jnp.max(s, axis=1, keepdims=True))
        p = p / jnp.sum(p, axis=1, keepdims=True)
        ds = p * (_dot(dov, vv, NT) - jnp.sum(doo, axis=1, keepdims=True))
        dq_ref[...] = _dot(ds, kv) * scale
        st = _dot(kv, qv, NT) * scale
        pt = jnp.exp(st - jnp.max(st, axis=0, keepdims=True))
        pt = pt / jnp.sum(pt, axis=0, keepdims=True)
        delta_t = _dot(jnp.ones((8, dh), F32), doo, NT, hi=True)[0:1, :]
        dst = pt * (_dot(vv, dov, NT) - delta_t)
        dkp = _dot(dst, qv) * scale
        dvp = _dot(pt, dov)

        @pl.when(i == 0)
        def _():
            dk_ref[...] = dkp
            dv_ref[...] = dvp

        @pl.when(i > 0)
        def _():
            dk_ref[...] += dkp
            dv_ref[...] += dvp

    qspec = pl.BlockSpec((tq, dh), lambda h, i: (i, h))
    kspec = pl.BlockSpec((ml, dh), lambda h, i: (0, h))
    return _call(body, name=name, grid=(X_HEADS, t // tq), in_specs=[qspec, kspec, kspec, qspec, qspec],
                 out_specs=(qspec, kspec, kspec),
                 out_shape=(_sds((t, d)), _sds((ml, d)), _sds((ml, d))))(q, k, v, o, do)


def scan_lead(a, u, name, reverse, inclusive):
    n, r, c = a.shape
    blk = max(1, min(n, (1 << 18) // (max(r, 8) * c)))
    while n % blk:
        blk -= 1
    nb = n // blk

    def body(a_ref, u_ref, o_ref, carry):
        @pl.when(pl.program_id(0) == 0)
        def _():
            carry[...] = jnp.zeros_like(carry)

        def step(s, h):
            idx = (blk - 1 - s) if reverse else s
            hn = a_ref[idx] * h + u_ref[idx]
            o_ref[idx] = hn if inclusive else h
            return hn

        carry[...] = lax.fori_loop(0, blk, step, carry[...])

    spec = pl.BlockSpec((blk, r, c), (lambda i: (nb - 1 - i, 0, 0)) if reverse else (lambda i: (i, 0, 0)))
    return _call(body, name=name, grid=(nb,), in_specs=[spec, spec], out_specs=spec,
                 out_shape=_sds((n, r, c)), scratch_shapes=[pltpu.VMEM((r, c), F32)])(a, u)


def _chunk_mats(bwd_dir):
    i = lax.broadcasted_iota(jnp.int32, (GLA_TILE, GLA_TILE), 0)
    j = lax.broadcasted_iota(jnp.int32, (GLA_TILE, GLA_TILE), 1)
    same = lax.shift_right_logical(i, 4) == lax.shift_right_logical(j, 4)
    if bwd_dir:
        cm, cm_t = same & (j >= i), same & (i >= j)
        mk, mk_t = same & (j > i), same & (i > j)
    else:
        cm, cm_t = same & (j <= i), same & (i <= j)
        mk, mk_t = same & (j <= i), same & (i <= j)
    f = lambda b: jnp.where(b, 1.0, 0.0).astype(F32)
    return f(cm), f(cm_t), mk, mk_t, f(same)


def gla_gates_fwd(zf, zb, w2f, b2f, w2b, b2b, name):
    t = zf.shape[0]
    tm = min(256, t)

    def body(zf_ref, zb_ref, wf_ref, bf_ref, wb_ref, bb_ref, lf_ref, lb_ref):
        lf_ref[...] = -_softplus(-(_dot(zf_ref[...], wf_ref[...], hi=True) + bf_ref[...])) / GATE_TAU
        lb_ref[...] = -_softplus(-(_dot(zb_ref[...], wb_ref[...], hi=True) + bb_ref[...])) / GATE_TAU

    zs = pl.BlockSpec((tm, GATE_RANK), lambda i: (i, 0))
    ws = pl.BlockSpec((GATE_RANK, B_QK), lambda i: (0, 0))
    bs = pl.BlockSpec((1, B_QK), lambda i: (0, 0))
    os_ = pl.BlockSpec((tm, B_QK), lambda i: (i, 0))
    return _call(body, name=name, grid=(t // tm,), in_specs=[zs, zs, ws, bs, ws, bs], out_specs=(os_, os_),
                 out_shape=(_sds((t, B_QK)),) * 2)(zf, zb, w2f, b2f.reshape(1, B_QK), w2b, b2b.reshape(1, B_QK))


def gla_gates_bwd(zf, zb, w2f, b2f, w2b, b2b, dlf, dlb, name):
    t = zf.shape[0]
    tm = min(256, t)

    def body(zf_ref, zb_ref, wf_ref, bf_ref, wb_ref, bb_ref, dlf_ref, dlb_ref,
             dzf_ref, dzb_ref, dpf_ref, dpb_ref, dbf_ref, dbb_ref):
        first = pl.program_id(0) == 0
        for z_ref, w_ref, b_ref, dl_ref, dz_ref, dp_ref, db_ref in (
                (zf_ref, wf_ref, bf_ref, dlf_ref, dzf_ref, dpf_ref, dbf_ref),
                (zb_ref, wb_ref, bb_ref, dlb_ref, dzb_ref, dpb_ref, dbb_ref)):
            pre = _dot(z_ref[...], w_ref[...], hi=True) + b_ref[...]
            dpre = dl_ref[...] * (1.0 / GATE_TAU) * _sigmoid(-pre)
            dp_ref[...] = dpre
            dz_ref[...] = _dot(dpre, w_ref[...], NT, hi=True)
            part = jnp.sum(dpre, axis=0, keepdims=True)

            @pl.when(first)
            def _():
                db_ref[...] = part

            @pl.when(jnp.logical_not(first))
            def _():
                db_ref[...] += part

    zs = pl.BlockSpec((tm, GATE_RANK), lambda i: (i, 0))
    ws = pl.BlockSpec((GATE_RANK, B_QK), lambda i: (0, 0))
    bs = pl.BlockSpec((1, B_QK), lambda i: (0, 0))
    os_ = pl.BlockSpec((tm, B_QK), lambda i: (i, 0))
    return _call(body, name=name, grid=(t // tm,), in_specs=[zs, zs, ws, bs, ws, bs, os_, os_],
                 out_specs=(zs, zs, os_, os_, bs, bs),
                 out_shape=(_sds((t, GATE_RANK)),) * 2 + (_sds((t, B_QK)),) * 2 + (_sds((1, B_QK)),) * 2)(
        zf, zb, w2f, b2f.reshape(1, B_QK), w2b, b2b.reshape(1, B_QK), dlf, dlb)


def gla_outer(xt, lat, y, name, bwd_dir, mode):
    t = y.shape[0]
    nchunk = t // GLA_CHUNK
    khat = mode == "khat"
    scale = B_KEY_DIM ** -0.5

    def body(xt_ref, lat_ref, y_ref, *outs):
        _, cm_t, _, _, same = _chunk_mats(bwd_dir)
        lat_v = lat_ref[...]
        bt = _dot(lat_v, cm_t, hi=True)
        if khat:
            mult = jnp.exp(_dot(lat_v, same, hi=True) - bt)
        else:
            mult = jnp.exp(bt) * scale
        xm = xt_ref[...] * mult
        lane = lax.shift_right_logical(lax.broadcasted_iota(jnp.int32, (1, GLA_TILE), 1), 4)
        ones = jnp.ones((GLA_TILE, B_VAL_DIM), F32)
        yv = [y_ref[:, h * B_VAL_DIM:(h + 1) * B_VAL_DIM].astype(BF16) for h in range(B_HEADS)]
        for c in range(CHUNKS_PER_TILE):
            sel = lane == c
            xc = jnp.where(sel, xm, 0.0).astype(BF16)
            for h in range(B_HEADS):
                rows = slice(h * B_KEY_DIM, (h + 1) * B_KEY_DIM)
                outs[0][c, rows, :] = _dot(xc[rows, :], yv[h])
            if khat:
                outs[1][c] = jnp.exp(_dot(jnp.where(sel, lat_v, 0.0), ones, hi=True))

    tspec = pl.BlockSpec((B_QK, GLA_TILE), lambda i: (0, i))
    ospec = pl.BlockSpec((CHUNKS_PER_TILE, B_QK, B_VAL_DIM), lambda i: (i, 0, 0))
    oshape = _sds((nchunk, B_QK, B_VAL_DIM))
    return _call(body, name=name, grid=(t // GLA_TILE,),
                 in_specs=[tspec, tspec, pl.BlockSpec((GLA_TILE, B_V), lambda i: (i, 0))],
                 out_specs=(ospec, ospec) if khat else ospec,
                 out_shape=(oshape, oshape) if khat else oshape)(xt, lat, y)


def _head_lane_mask(h):
    lane = lax.broadcasted_iota(jnp.int32, (1, B_QK), 1)
    return lax.shift_right_logical(lane, 6) == h


def _chunk_rows(c):
    return slice(c * GLA_CHUNK, (c + 1) * GLA_CHUNK)


def gla_inner_fwd(q, k, v, la, sp, name, bwd_dir):
    t = q.shape[0]
    scale = B_KEY_DIM ** -0.5

    def body(q_ref, k_ref, v_ref, la_ref, sp_ref, o_ref):
        cm, _, mk, _, _ = _chunk_mats(bwd_dir)
        b = _dot(cm, la_ref[...], hi=True)
        qt = q_ref[...] * scale * jnp.exp(b)
        kt = k_ref[...] * jnp.exp(jnp.minimum(-b, EXP_CLAMP))
        spb = [sp_ref[c].astype(BF16) for c in range(CHUNKS_PER_TILE)]
        for h in range(B_HEADS):
            lm = _head_lane_mask(h)
            qm = jnp.where(lm, qt, 0.0).astype(BF16)
            km = jnp.where(lm, kt, 0.0).astype(BF16)
            vs = slice(h * B_VAL_DIM, (h + 1) * B_VAL_DIM)
            att = jnp.where(mk, _dot(qm, km, NT), 0.0)
            inter = jnp.concatenate([_dot(qm[_chunk_rows(c), :], spb[c]) for c in range(CHUNKS_PER_TILE)], axis=0)
            o_ref[:, vs] = _dot(att, v_ref[:, vs]) + inter

    qs = pl.BlockSpec((GLA_TILE, B_QK), lambda i: (i, 0))
    vs_ = pl.BlockSpec((GLA_TILE, B_V), lambda i: (i, 0))
    ss = pl.BlockSpec((CHUNKS_PER_TILE, B_QK, B_VAL_DIM), lambda i: (i, 0, 0))
    return _call(body, name=name, grid=(t // GLA_TILE,), in_specs=[qs, qs, vs_, qs, ss], out_specs=vs_,
                 out_shape=_sds((t, B_V)))(q, k, v, la, sp)


def gla_inner_bwd(q, k, v, la, do, sp, gs, dec, name, bwd_dir, add=None):
    t = q.shape[0]
    scale = B_KEY_DIM ** -0.5
    hasadd = add is not None

    def body(*refs):
        it = iter(refs)
        q_ref, k_ref, v_ref, la_ref, do_ref, sp_ref, gs_ref, dec_ref = [next(it) for _ in range(8)]
        adds = [next(it) for _ in range(3)] if hasadd else None
        dq_ref, dk_ref, dv_ref, dla_ref = [next(it) for _ in range(4)]
        cm, cm_t, mk, mk_t, same = _chunk_mats(bwd_dir)
        la_v = la_ref[...]
        b = _dot(cm, la_v, hi=True)
        btot = _dot(same, la_v, hi=True)
        eb = jnp.exp(b)
        ek = jnp.exp(jnp.minimum(-b, EXP_CLAMP))
        ekh = jnp.exp(btot - b)
        qt = q_ref[...] * scale * eb
        kt = k_ref[...] * ek
        kh = k_ref[...] * ekh
        spb = [sp_ref[c].astype(BF16) for c in range(CHUNKS_PER_TILE)]
        gsb = [gs_ref[c].astype(BF16) for c in range(CHUNKS_PER_TILE)]
        dqt = jnp.zeros((GLA_TILE, B_QK), F32)
        dkt = jnp.zeros((GLA_TILE, B_QK), F32)
        dkh = jnp.zeros((GLA_TILE, B_QK), F32)
        for h in range(B_HEADS):
            lm = _head_lane_mask(h)
            qm = jnp.where(lm, qt, 0.0).astype(BF16)
            km = jnp.where(lm, kt, 0.0).astype(BF16)
            khm = jnp.where(lm, kh, 0.0).astype(BF16)
            vs = slice(h * B_VAL_DIM, (h + 1) * B_VAL_DIM)
            vh = v_ref[:, vs].astype(BF16)
            doh = do_ref[:, vs].astype(BF16)
            da = jnp.where(mk, _dot(doh, vh, NT), 0.0)
            da_t = jnp.where(mk_t, _dot(vh, doh, NT), 0.0)
            att_t = jnp.where(mk_t, _dot(km, qm, NT), 0.0)
            dv_h = _dot(att_t, doh) + jnp.concatenate(
                [_dot(khm[_chunk_rows(c), :], gsb[c]) for c in range(CHUNKS_PER_TILE)], axis=0)
            if hasadd:
                dv_h = dv_h + adds[2][:, vs]
            dv_ref[:, vs] = dv_h
            dq_inter = jnp.concatenate(
                [_dot(doh[_chunk_rows(c), :], spb[c], NT) for c in range(CHUNKS_PER_TILE)], axis=0)
            dqt = dqt + _dot(da, km) + jnp.where(lm, dq_inter, 0.0)
            dkt = dkt + _dot(da_t, qm)
            dkh_inter = jnp.concatenate(
                [_dot(vh[_chunk_rows(c), :], gsb[c], NT) for c in range(CHUNKS_PER_TILE)], axis=0)
            dkh = dkh + jnp.where(lm, dkh_inter, 0.0)
        dq = dqt * scale * eb
        dk = dkt * ek + dkh * ekh
        if hasadd:
            dq = dq + adds[0][...]
            dk = dk + adds[1][...]
        dq_ref[...] = dq
        dk_ref[...] = dk
        db = dqt * qt - dkt * kt - dkh * kh
        ones16 = jnp.ones((GLA_CHUNK, B_VAL_DIM), F32)
        t2 = jnp.concatenate(
            [_dot(ones16, gs_ref[c] * dec_ref[c] * sp_ref[c], NT, hi=True) for c in range(CHUNKS_PER_TILE)], axis=0)
        dla_ref[...] = _dot(cm_t, db, hi=True) + _dot(same, dkh * kh, hi=True) + t2

    qs = pl.BlockSpec((GLA_TILE, B_QK), lambda i: (i, 0))
    vs_ = pl.BlockSpec((GLA_TILE, B_V), lambda i: (i, 0))
    ss = pl.BlockSpec((CHUNKS_PER_TILE, B_QK, B_VAL_DIM), lambda i: (i, 0, 0))
    ins = [q, k, v, la, do, sp, gs, dec] + (list(add) if hasadd else [])
    in_specs = [qs, qs, vs_, qs, vs_, ss, ss, ss] + ([qs, qs, vs_] if hasadd else [])
    return _call(body, name=name, grid=(t // GLA_TILE,), in_specs=in_specs, out_specs=(qs, qs, vs_, qs),
                 out_shape=(_sds((t, B_QK)), _sds((t, B_QK)), _sds((t, B_V)), _sds((t, B_QK))),
                 compiler_params=_cparams())(*ins)


def gla_out_fwd(of, ob, g, gn, name):
    t = of.shape[0]
    tm = min(256, t)

    def body(of_ref, ob_ref, g_ref, gn_ref, o_ref):
        for h in range(B_HEADS):
            vs = slice(h * B_VAL_DIM, (h + 1) * B_VAL_DIM)
            o = of_ref[:, vs] + ob_ref[:, vs]
            on = o * lax.rsqrt(jnp.mean(o * o, axis=1, keepdims=True) + EPS)
            gv = g_ref[:, vs]
            o_ref[:, vs] = on * gn_ref[:, vs] * (gv * _sigmoid(gv))

    row = pl.BlockSpec((tm, B_V), lambda i: (i, 0))
    vec = pl.BlockSpec((1, B_V), lambda i: (0, 0))
    return _call(body, name=name, grid=(t // tm,), in_specs=[row, row, row, vec], out_specs=row,
                 out_shape=_sds((t, B_V)))(of, ob, g, gn.reshape(1, B_V))


def gla_out_bwd(of, ob, g, gn, dout, name):
    t = of.shape[0]
    tm = min(256, t)

    def body(of_ref, ob_ref, g_ref, gn_ref, d_ref, do_ref, dg_ref, dgn_ref):
        first = pl.program_id(0) == 0
        for h in range(B_HEADS):
            vs = slice(h * B_VAL_DIM, (h + 1) * B_VAL_DIM)
            o = of_ref[:, vs] + ob_ref[:, vs]
            r = lax.rsqrt(jnp.mean(o * o, axis=1, keepdims=True) + EPS)
            on = o * r
            gv = g_ref[:, vs]
            sg = _sigmoid(gv)
            silu = gv * sg
            dv = d_ref[:, vs]
            gnv = gn_ref[:, vs]
            dg_ref[:, vs] = dv * on * gnv * (sg * (1.0 + gv * (1.0 - sg)))
            don = dv * silu * gnv
            do_ref[:, vs] = r * (don - on * jnp.mean(don * on, axis=1, keepdims=True))
            part = jnp.sum(dv * silu * on, axis=0, keepdims=True)

            @pl.when(first)
            def _():
                dgn_ref[:, vs] = part

            @pl.when(jnp.logical_not(first))
            def _():
                dgn_ref[:, vs] += part

    row = pl.BlockSpec((tm, B_V), lambda i: (i, 0))
    vec = pl.BlockSpec((1, B_V), lambda i: (0, 0))
    return _call(body, name=name, grid=(t // tm,), in_specs=[row, row, row, vec, row], out_specs=(row, row, vec),
                 out_shape=(_sds((t, B_V)), _sds((t, B_V)), _sds((1, B_V))))(of, ob, g, gn.reshape(1, B_V), dout)


def _shift(x, k):
    if k > 0:
        return jnp.concatenate([x[k:], jnp.zeros((k,) + x.shape[1:], x.dtype)], axis=0)
    return jnp.concatenate([jnp.zeros((-k,) + x.shape[1:], x.dtype), x[:k]], axis=0)


def _lru_gates(xc, s, wa_ref, ba_ref, wx_ref, bx_ref, lam_ref):
    cols = [slice(g * C_BLOCK_DIM, (g + 1) * C_BLOCK_DIM) for g in range(C_BLOCKS)]
    zr = jnp.concatenate([_dot(xc[:, cs], wa_ref[s, g]) for g, cs in enumerate(cols)], axis=1) + ba_ref[s:s + 1, :]
    zi = jnp.concatenate([_dot(xc[:, cs], wx_ref[s, g]) for g, cs in enumerate(cols)], axis=1) + bx_ref[s:s + 1, :]
    r = _sigmoid(zr)
    i = _sigmoid(zi)
    sp = _softplus(-lam_ref[s:s + 1, :])
    log_a = -LRU_C * r * sp
    return r, i, sp, log_a


def lru_gates_fwd(x0, xm2, xm1, xp1, cw, cb, wa, ba, wx, bx, lam, name):
    t = x0.shape[0]
    tm = min(256, t)

    def body(x0_ref, xm2_ref, xm1_ref, xp1_ref, cw_ref, cb_ref, wa_ref, ba_ref, wx_ref, bx_ref, lam_ref,
             xc_ref, a0_ref, u0_ref, a1_ref, u1_ref):
        xc = (xm2_ref[...] * cw_ref[0:1, :] + xm1_ref[...] * cw_ref[1:2, :] + x0_ref[...] * cw_ref[2:3, :]
              + xp1_ref[...] * cw_ref[3:4, :] + cb_ref[...])
        xc_ref[...] = xc
        for s, (a_ref, u_ref) in enumerate(((a0_ref, u0_ref), (a1_ref, u1_ref))):
            _, i, _, log_a = _lru_gates(xc, s, wa_ref, ba_ref, wx_ref, bx_ref, lam_ref)
            a_ref[...] = jnp.exp(log_a)
            u_ref[...] = jnp.sqrt(-_expm1(2.0 * log_a)) * (i * xc)

    row = pl.BlockSpec((tm, C_WIDTH), lambda i: (i, 0))
    full = lambda shape: pl.BlockSpec(shape, lambda i: (0,) * len(shape))
    wshape = (2, C_BLOCKS, C_BLOCK_DIM, C_BLOCK_DIM)
    return _call(body, name=name, grid=(t // tm,),
                 in_specs=[row] * 4 + [full((4, C_WIDTH)), full((1, C_WIDTH)), full(wshape), full((2, C_WIDTH)),
                                       full(wshape), full((2, C_WIDTH)), full((2, C_WIDTH))],
                 out_specs=(row,) * 5, out_shape=(_sds((t, C_WIDTH)),) * 5)(
        x0, xm2, xm1, xp1, cw, cb.reshape(1, C_WIDTH), wa, ba, wx, bx, lam)


def lru_gates_bwd(xc, g0, hs0, g1, hs1, wa, ba, wx, bx, lam, name):
    t = xc.shape[0]
    tm = min(256, t)

    def body(xc_ref, g0_ref, hs0_ref, g1_ref, hs1_ref, wa_ref, ba_ref, wx_ref, bx_ref, lam_ref,
             dxc_ref, dzr0_ref, dzi0_ref, dzr1_ref, dzi1_ref, dlam_ref, dba_ref, dbx_ref):
        first = pl.program_id(0) == 0

        @pl.when(first)
        def _():
            dlam_ref[...] = jnp.zeros_like(dlam_ref)
            dba_ref[...] = jnp.zeros_like(dba_ref)
            dbx_ref[...] = jnp.zeros_like(dbx_ref)

        xcv = xc_ref[...]
        dxc = jnp.zeros_like(xcv)
        cols = [slice(g * C_BLOCK_DIM, (g + 1) * C_BLOCK_DIM) for g in range(C_BLOCKS)]
        for s, (g_ref, hs_ref, dzr_ref, dzi_ref) in enumerate(
                ((g0_ref, hs0_ref, dzr0_ref, dzi0_ref), (g1_ref, hs1_ref, dzr1_ref, dzi1_ref))):
            r, i, sp, log_a = _lru_gates(xcv, s, wa_ref, ba_ref, wx_ref, bx_ref, lam_ref)
            du = g_ref[...]
            da = du * hs_ref[...]
            a = jnp.exp(log_a)
            e2 = jnp.exp(2.0 * log_a)
            c = jnp.sqrt(-_expm1(2.0 * log_a))
            ix = i * xcv
            dlog = da * a - du * ix * (e2 / c)
            dix = du * c
            dxc = dxc + dix * i
            dzi = dix * xcv * i * (1.0 - i)
            dzr = dlog * (-LRU_C * sp) * r * (1.0 - r)
            dzr_ref[...] = dzr
            dzi_ref[...] = dzi
            dxc = dxc + jnp.concatenate(
                [_dot(dzr[:, cs], wa_ref[s, g], NT) + _dot(dzi[:, cs], wx_ref[s, g], NT) for g, cs in enumerate(cols)],
                axis=1)
            dsp = jnp.sum(dlog * (-LRU_C * r), axis=0, keepdims=True)
            dlam_ref[s:s + 1, :] += dsp * (-_sigmoid(-lam_ref[s:s + 1, :]))
            dba_ref[s:s + 1, :] += jnp.sum(dzr, axis=0, keepdims=True)
            dbx_ref[s:s + 1, :] += jnp.sum(dzi, axis=0, keepdims=True)
        dxc_ref[...] = dxc

    row = pl.BlockSpec((tm, C_WIDTH), lambda i: (i, 0))
    full = lambda shape: pl.BlockSpec(shape, lambda i: (0,) * len(shape))
    wshape = (2, C_BLOCKS, C_BLOCK_DIM, C_BLOCK_DIM)
    vec2 = full((2, C_WIDTH))
    return _call(body, name=name, grid=(t // tm,),
                 in_specs=[row] * 5 + [full(wshape), vec2, full(wshape), vec2, vec2],
                 out_specs=(row,) * 5 + (vec2,) * 3,
                 out_shape=(_sds((t, C_WIDTH)),) * 5 + (_sds((2, C_WIDTH)),) * 3)(
        xc, g0, hs0, g1, hs1, wa, ba, wx, bx, lam)


def lru_out_fwd(h0, h1, y, name):
    t = y.shape[0]
    tm = min(256, t)

    def body(h0_ref, h1_ref, y_ref, o_ref):
        o_ref[...] = (h0_ref[...] + h1_ref[...]) * _gelu(y_ref[...])

    row = pl.BlockSpec((tm, C_WIDTH), lambda i: (i, 0))
    return _call(body, name=name, grid=(t // tm,), in_specs=[row] * 3, out_specs=row,
                 out_shape=_sds((t, C_WIDTH)))(h0, h1, y)


def lru_out_bwd(h0, h1, y, dout, name):
    t = y.shape[0]
    tm = min(256, t)

    def body(h0_ref, h1_ref, y_ref, d_ref, dh_ref, dy_ref):
        yv = y_ref[...]
        dv = d_ref[...]
        dh_ref[...] = dv * _gelu(yv)
        dy_ref[...] = dv * (h0_ref[...] + h1_ref[...]) * _gelu_grad(yv)

    row = pl.BlockSpec((tm, C_WIDTH), lambda i: (i, 0))
    return _call(body, name=name, grid=(t // tm,), in_specs=[row] * 4, out_specs=(row, row),
                 out_shape=(_sds((t, C_WIDTH)),) * 2)(h0, h1, y, dout)


def conv_bwd(dxc, dp2, dp1, dm1, x0, xm2, xm1, xp1, cw, name):
    t = x0.shape[0]
    tm = min(256, t)

    def body(d_ref, dp2_ref, dp1_ref, dm1_ref, x0_ref, xm2_ref, xm1_ref, xp1_ref, cw_ref, dx_ref, dcw_ref, dcb_ref):
        @pl.when(pl.program_id(0) == 0)
        def _():
            dcw_ref[...] = jnp.zeros_like(dcw_ref)
            dcb_ref[...] = jnp.zeros_like(dcb_ref)

        dv = d_ref[...]
        dx_ref[...] = (dp2_ref[...] * cw_ref[0:1, :] + dp1_ref[...] * cw_ref[1:2, :] + dv * cw_ref[2:3, :]
                       + dm1_ref[...] * cw_ref[3:4, :])
        for j, x_ref in enumerate((xm2_ref, xm1_ref, x0_ref, xp1_ref)):
            dcw_ref[j:j + 1, :] += jnp.sum(dv * x_ref[...], axis=0, keepdims=True)
        dcb_ref[...] += jnp.sum(dv, axis=0, keepdims=True)

    row = pl.BlockSpec((tm, C_WIDTH), lambda i: (i, 0))
    cws = pl.BlockSpec((4, C_WIDTH), lambda i: (0, 0))
    cbs = pl.BlockSpec((1, C_WIDTH), lambda i: (0, 0))
    return _call(body, name=name, grid=(t // tm,), in_specs=[row] * 8 + [cws], out_specs=(row, cws, cbs),
                 out_shape=(_sds((t, C_WIDTH)), _sds((4, C_WIDTH)), _sds((1, C_WIDTH))))(
        dxc, dp2, dp1, dm1, x0, xm2, xm1, xp1, cw)


def _my_place():
    return lax.axis_index("x"), lax.axis_index("y"), lax.axis_index("c")


def all_gather(xs, name):
    r, c = xs.shape

    def body(x_ref, out_ref, send_sems, recv_sems, local_sem):
        x, y, cc = _my_place()
        me, sibling = (x, y, cc), (x, y, 1 - cc)
        chips = [(1 - x, y), (x, 1 - y), (1 - x, 1 - y)]

        def slot(px, py, pc):
            return out_ref.at[4 * px + 2 * py + pc]

        def copy(k, block, to, src=None):
            return pltpu.make_async_remote_copy(
                src_ref=slot(*block) if src is None else src, dst_ref=slot(*block),
                send_sem=send_sems.at[k], recv_sem=recv_sems.at[k], device_id=to, device_id_type=MESH)

        mine = pltpu.make_async_copy(x_ref, slot(*me), local_sem)
        mine.start()
        first = [copy(0, me, sibling, src=x_ref)]
        first += [copy(1 + j, me, (*chip, cc), src=x_ref) for j, chip in enumerate(chips)]
        for cp in first:
            cp.start()
        passed = [copy(4 + j, (*chip, cc), sibling) for j, chip in enumerate(chips)]
        for j, chip in enumerate(chips):
            copy(1 + j, (*chip, cc), me).wait_recv()
            passed[j].start()
        copy(0, sibling, me).wait_recv()
        for j, chip in enumerate(chips):
            copy(4 + j, (*chip, 1 - cc), me).wait_recv()
        for cp in first + passed:
            cp.wait_send()
        mine.wait()

    return _call(body, name=name, in_specs=[pl.BlockSpec(memory_space=pl.ANY)],
                 out_specs=pl.BlockSpec(memory_space=pl.ANY), out_shape=_sds((N_DEV, r, c), xs.dtype),
                 scratch_shapes=[pltpu.SemaphoreType.DMA((7,)), pltpu.SemaphoreType.DMA((7,)),
                                 pltpu.SemaphoreType.DMA])(xs)


def exchange_sibling(gw, name):
    _, r, c = gw.shape
    g5 = gw.reshape(4, 2, r, c)
    nch = SIBLING_STREAMS // 4 if r % (8 * (SIBLING_STREAMS // 4)) == 0 else 1
    rows = r // nch

    def body(g_ref, out_ref, send_sems, recv_sems, local_sems):
        x, y, cc = _my_place()
        keeps, swaps = [], []
        for q in range(4):
            for s in range(nch):
                k = q * nch + s
                win = pl.ds(s * rows, rows)
                keeps.append(pltpu.make_async_copy(g_ref.at[q, cc, win], out_ref.at[0, q, win], local_sems.at[k]))
                swaps.append(pltpu.make_async_remote_copy(
                    src_ref=g_ref.at[q, 1 - cc, win], dst_ref=out_ref.at[1, q, win], send_sem=send_sems.at[k],
                    recv_sem=recv_sems.at[k], device_id=(x, y, 1 - cc), device_id_type=MESH))
        for cp in swaps + keeps:
            cp.start()
        for cp in swaps + keeps:
            cp.wait()

    nsem = 4 * nch
    return _call(body, name=name, in_specs=[pl.BlockSpec(memory_space=pl.ANY)],
                 out_specs=pl.BlockSpec(memory_space=pl.ANY), out_shape=_sds((2, 4, r, c), gw.dtype),
                 scratch_shapes=[pltpu.SemaphoreType.DMA((nsem,)), pltpu.SemaphoreType.DMA((nsem,)),
                                 pltpu.SemaphoreType.DMA((nsem,))])(g5)


def exchange_chips(p, name):
    _, r, c = p.shape

    def body(p_ref, out_ref, send_sems, recv_sems, local_sem):
        x, y, cc = _my_place()
        chips = [(1 - x, y), (x, 1 - y), (1 - x, 1 - y)]
        keep = pltpu.make_async_copy(p_ref.at[2 * x + y], out_ref.at[3], local_sem)
        keep.start()
        copies = [pltpu.make_async_remote_copy(
            src_ref=p_ref.at[2 * px + py], dst_ref=out_ref.at[j], send_sem=send_sems.at[j], recv_sem=recv_sems.at[j],
            device_id=(px, py, cc), device_id_type=MESH) for j, (px, py) in enumerate(chips)]
        for cp in copies:
            cp.start()
        for cp in copies:
            cp.wait()
        keep.wait()

    return _call(body, name=name, in_specs=[pl.BlockSpec(memory_space=pl.ANY)],
                 out_specs=pl.BlockSpec(memory_space=pl.ANY), out_shape=_sds((4, r, c), p.dtype),
                 scratch_shapes=[pltpu.SemaphoreType.DMA((3,)), pltpu.SemaphoreType.DMA((3,)),
                                 pltpu.SemaphoreType.DMA])(p)


def reduce_scatter(gw, name):
    _, r, c = gw.shape
    halves = exchange_sibling(gw, name + "_sibling").reshape(2, 4 * r, c)
    chip_sum = sum_lead(halves, (0, 1), BF16, name + "_add2").reshape(4, r, c)
    parts = exchange_chips(chip_sum, name + "_chips")
    return sum_lead(parts, (3, 0, 1, 2), F32, name + "_add4")


def _pack(arrs):
    flat = jnp.concatenate([a.reshape(-1).astype(F32) for a in arrs])
    n = flat.shape[0]
    pad = (-n) % PACK_ELEMS
    return jnp.pad(flat, (0, pad)).reshape(-1, 128)


def _unpack(packed, shapes):
    flat = packed.reshape(-1)
    out, off = [], 0
    for s in shapes:
        n = int(np.prod(s))
        out.append(flat[off:off + n].reshape(s))
        off += n
    return out


def _t5_bucket(rel):
    nb = N_BUCKETS // 2
    max_exact = nb // 2
    ret = jnp.where(rel > 0, nb, 0)
    n = jnp.abs(rel)
    nf = jnp.maximum(n, 1).astype(jnp.float32)
    large = max_exact + (jnp.log(nf / max_exact) / math.log(MAX_DISTANCE / max_exact)
                         * (nb - max_exact)).astype(jnp.int32)
    large = jnp.minimum(large, nb - 1)
    return ret + jnp.where(n < max_exact, n, large)


SMALL_SHARDED = ("gla_w2_f", "gla_w2_b", "conv_w", "lru_ba", "lru_bx", "lru_lambda")
SMALL_REPL = ("rel_bias", "attn_sink", "gla_b2_f", "gla_b2_b", "gla_norm", "conv_b", "lru_wa", "lru_wx",
              "norm_mix_pre", "norm_mix_post", "norm_mem", "norm_x_pre", "norm_x_post", "norm_ff_pre", "norm_ff_post")
BIG = ("w_in", "w_out", "xq", "xk", "xv", "xo", "w_up", "w_down")
WEIGHTS = ['rel_bias', 'w_in', 'w_out', 'attn_sink', 'gla_w2_f', 'gla_b2_f', 'gla_w2_b', 'gla_b2_b', 'gla_norm',
           'conv_w', 'conv_b', 'lru_wa', 'lru_ba', 'lru_wx', 'lru_bx', 'lru_lambda', 'xq', 'xk', 'xv', 'xo', 'w_up',
           'w_down', 'norm_mix_pre', 'norm_mix_post', 'norm_mem', 'norm_x_pre', 'norm_x_post', 'norm_ff_pre',
           'norm_ff_post']


def _step(x, mem, loss_target, w, m, v):
    depth = w["w_in"].shape[0]
    t, d = x.shape[1], x.shape[2]
    ml = mem.shape[1]
    rx = d // N_DEV
    rf = w["w_up"].shape[2]
    r_out = D_MIX // N_DEV
    x = x.reshape(t, d)
    mem = mem.reshape(ml, d)
    loss_target = loss_target.reshape(t, d)
    my_idx = 4 * lax.axis_index("x") + 2 * lax.axis_index("y") + lax.axis_index("c")

    off_in, off_out = 0, W_IN_ROWS
    off_xq = off_out + r_out
    off_xk, off_xv, off_xo = off_xq + rx, off_xq + 2 * rx, off_xq + 3 * rx
    off_up = off_xq + 4 * rx
    off_down = off_up + rf
    r_tot = off_down + rf

    sh_shapes = [w[n].shape for n in SMALL_SHARDED]
    gathered = all_gather(_pack([w[n] for n in SMALL_SHARDED]), "ag_small")
    per_dev = [_unpack(gathered[j], sh_shapes) for j in range(N_DEV)]
    full = {n: jnp.concatenate([per_dev[j][i] for j in range(N_DEV)], axis=-1) for i, n in enumerate(SMALL_SHARDED)}
    for n in SMALL_REPL:
        full[n] = w[n]

    gws = []
    for l in range(depth):
        w_in_t = jnp.pad(w["w_in"][l].T, ((0, W_IN_ROWS - W_IN_SHARD), (0, 0)))
        blk = jnp.concatenate([w_in_t, w["w_out"][l], w["xq"][l], w["xk"][l], w["xv"][l], w["xo"][l],
                               w["w_up"][l].T, w["w_down"][l]], axis=0).astype(BF16)
        gws.append(all_gather(blk, "ag_weights"))

    qi = jnp.arange(BLOCK)[:, None]
    kj = jnp.arange(3 * BLOCK)[None, :]
    onehot_t = (jnp.arange(N_BUCKETS)[:, None] == _t5_bucket(kj - BLOCK - qi).reshape(1, -1)).astype(F32)
    bias = mm_plain(full["rel_bias"].T, onehot_t, "rel_bias_lookup", hi=True, tn=3 * BLOCK * 16)
    bias = bias.reshape(A_HEADS, BLOCK, 3 * BLOCK)
    bias_t = jnp.transpose(bias, (0, 2, 1))

    def sink_rows(sink):
        s = jnp.broadcast_to(sink.reshape(A_KV_HEADS, A_GROUP, 1), (A_KV_HEADS, A_GROUP, 128))
        return jnp.pad(s, ((0, 0), (0, 8 - A_GROUP), (0, 0)))

    def split_proj(p):
        outs, off = [], 0
        for s in SPLIT_SIZES:
            outs.append(p[:, off:off + s])
            off += s
        return outs

    def lead(a):
        return a.reshape(a.shape[0], C_WIDTH // 128, 128)

    saved = []
    h = rms_fwd(x, full["norm_mix_pre"][0], "rms_first")
    for l in range(depth):
        gw = gws[l]
        sv = {"x": x, "h_in": h}
        proj_pad = mm_wn(h, gw, off_in, W_IN_ROWS, "mm_w_in")
        proj = proj_pad.reshape(t, N_DEV, W_IN_ROWS)[:, :, :W_IN_SHARD].reshape(t, D_IN)
        aq, ak, av, bq, bk, bv, bg, zf, zb, cx, cy = split_proj(proj)
        sv.update(aq=aq, ak=ak, av=av, bq=bq, bk=bk, bv=bv, bg=bg, zf=zf, zb=zb, cx=cx, cy=cy)
        sink_b = sink_rows(full["attn_sink"][l])
        oa = attn_fwd(aq, ak, av, bias, sink_b, "attn_fwd")
        la_f, la_b = gla_gates_fwd(zf, zb, full["gla_w2_f"][l], full["gla_b2_f"][l], full["gla_w2_b"][l],
                                   full["gla_b2_b"][l], "gla_gates_fwd")
        bk_t = bk.T
        gla = {}
        for nm, la, bdir in (("f", la_f, False), ("b", la_b, True)):
            la_t = la.T
            u, dec = gla_outer(bk_t, la_t, bv, "gla_outer_k_" + nm, bdir, "khat")
            sp = scan_lead(dec, u, "gla_state_scan_" + nm, reverse=bdir, inclusive=False)
            o_dir = gla_inner_fwd(bq, bk, bv, la, sp, "gla_inner_fwd_" + nm, bdir)
            gla[nm] = dict(la=la, la_t=la_t, dec=dec, sp=sp, o=o_dir)
        ob = gla_out_fwd(gla["f"]["o"], gla["b"]["o"], bg, full["gla_norm"][l], "gla_out_fwd")
        sv["gla"] = gla
        xm2, xm1, xp1 = _shift(cx, -2), _shift(cx, -1), _shift(cx, 1)
        xc, a0, u0, a1, u1 = lru_gates_fwd(cx, xm2, xm1, xp1, full["conv_w"][l], full["conv_b"][l], full["lru_wa"][l],
                                           full["lru_ba"][l], full["lru_wx"][l], full["lru_bx"][l],
                                           full["lru_lambda"][l], "lru_gates_fwd")
        h0 = scan_lead(lead(a0), lead(u0), "lru_scan_fwd", reverse=False, inclusive=True).reshape(t, C_WIDTH)
        h1 = scan_lead(lead(a1), lead(u1), "lru_scan_rev", reverse=True, inclusive=True).reshape(t, C_WIDTH)
        oc = lru_out_fwd(h0, h1, cy, "lru_out_fwd")
        sv.update(xm2=xm2, xm1=xm1, xp1=xp1, xc=xc, a0=a0, a1=a1, h0=h0, h1=h1, oa=oa)
        cat = jnp.concatenate([oa, ob, oc], axis=1)
        mixed = mm_wk(cat, gw, off_out, r_out, "mm_w_out")
        x1, h2 = resid_rms(x, mixed, full["norm_mix_post"][l], full["norm_x_pre"][l], "resid_rms")
        sv.update(cat=cat, mixed=mixed, x1=x1, h2=h2)
        memn = rms_fwd(mem, full["norm_mem"][l], "rms_mem")
        q = mm_wk(h2, gw, off_xq, rx, "mm_xq")
        k = mm_wk(memn, gw, off_xk, rx, "mm_xkv")
        vv = mm_wk(memn, gw, off_xv, rx, "mm_xkv")
        ox = xattn_fwd(q, k, vv, "xattn_fwd")
        ca = mm_wk(ox, gw, off_xo, rx, "mm_xo")
        x2, h3 = resid_rms(x1, ca, full["norm_x_post"][l], full["norm_ff_pre"][l], "resid_rms")
        sv.update(memn=memn, q=q, k=k, v=vv, ox=ox, ca=ca, x2=x2, h3=h3)
        up = mm_wn(h3, gw, off_up, rf, "mm_w_up")
        ff = mm_wk(up, gw, off_down, rf, "mm_w_down", jb=max(1, min(N_DEV, 2048 // rf)), relu2=True)
        if l + 1 < depth:
            x, h = resid_rms(x2, ff, full["norm_ff_post"][l], full["norm_mix_pre"][l + 1], "resid_rms")
        else:
            x = resid_rms(x2, ff, full["norm_ff_post"][l], None, "resid_rms_last")
        sv.update(up=up, ff=ff)
        saved.append(sv)

    dx, loss_local = loss_and_grad(x, loss_target, "loss")
    loss = lax.psum(loss_local, AXES)

    grads = {n: [None] * depth for n in WEIGHTS if n != "rel_bias"}
    dbias_total = None
    big_grads = [None] * depth
    bf = lambda a: a.astype(BF16)
    for l in reversed(range(depth)):
        gw = gws[l]
        sv = saved[l]
        dff, grads["norm_ff_post"][l] = rms_bwd(sv["ff"], full["norm_ff_post"][l], dx, "rms_bwd")
        dup = mm_wn(dff, gw, off_down, rf, "mm_w_down_dx", relu_grad_of=sv["up"])
        up_t = sv["up"].T
        act_t = bf(jnp.square(jnp.maximum(up_t, 0.0)))
        g_down = mm_plain(act_t, dff, "mm_dw_f", out_dtype=BF16)
        g_up_t = mm_plain(bf(dup.T), sv["h3"], "mm_dw_f", out_dtype=BF16)
        dh3 = mm_wk(dup, gw, off_up, rf, "mm_w_up_dx", jb=max(1, min(N_DEV, 2048 // rf)))
        dx2, grads["norm_ff_pre"][l] = rms_bwd(sv["x2"], full["norm_ff_pre"][l], dh3, "rms_bwd_add", add=dx)
        dca, grads["norm_x_post"][l] = rms_bwd(sv["ca"], full["norm_x_post"][l], dx2, "rms_bwd")
        dox = mm_wn(dca, gw, off_xo, rx, "mm_x_dx")
        g_xo = mm_plain(bf(sv["ox"].T), dca, "mm_dw_d", out_dtype=BF16)
        dq, dk, dv = xattn_bwd(sv["q"], sv["k"], sv["v"], sv["ox"], dox, "xattn_bwd")
        g_xq = mm_plain(bf(sv["h2"].T), dq, "mm_dw_d", out_dtype=BF16)
        memn_t = bf(sv["memn"].T)
        g_xk = mm_plain(memn_t, dk, "mm_dw_mem", out_dtype=BF16)
        g_xv = mm_plain(memn_t, dv, "mm_dw_mem", out_dtype=BF16)
        dh2 = mm_wn(dq, gw, off_xq, rx, "mm_x_dx")
        dmem_k = mm_wn(dk, gw, off_xk, rx, "mm_x_dx_mem")
        dmem_v = mm_wn(dv, gw, off_xv, rx, "mm_x_dx_mem")
        _, grads["norm_mem"][l] = rms_bwd(mem, full["norm_mem"][l], dmem_k, "rms_bwd_mem", dy2=dmem_v)
        dx1, grads["norm_x_pre"][l] = rms_bwd(sv["x1"], full["norm_x_pre"][l], dh2, "rms_bwd_add", add=dx2)
        dmixed, grads["norm_mix_post"][l] = rms_bwd(sv["mixed"], full["norm_mix_post"][l], dx1, "rms_bwd")
        dcat = mm_wn(dmixed, gw, off_out, r_out, "mm_w_out_dx")
        g_out = mm_plain(bf(sv["cat"].T), dmixed, "mm_dw_d", out_dtype=BF16)
        doa, dob, doc = dcat[:, :A_Q], dcat[:, A_Q:A_Q + B_V], dcat[:, A_Q + B_V:]
        daq, dak, dav, dbias, dsink = attn_bwd(sv["aq"], sv["ak"], sv["av"], bias, bias_t,
                                               sink_rows(full["attn_sink"][l]), doa, sv["oa"], "attn_bwd")
        grads["attn_sink"][l] = dsink[:, :A_GROUP, 0].reshape(A_HEADS)
        dbias_total = dbias if dbias_total is None else add_n([dbias_total, dbias], F32, "add_dbias")
        gf, gb = sv["gla"]["f"], sv["gla"]["b"]
        do_gla, dbg, dgn = gla_out_bwd(gf["o"], gb["o"], sv["bg"], full["gla_norm"][l], dob, "gla_out_bwd")
        grads["gla_norm"][l] = dgn.reshape(B_V)
        bq_t = sv["bq"].T
        acc = None
        dlas = {}
        for nm, gd, bdir in (("f", gf, False), ("b", gb, True)):
            wq = gla_outer(bq_t, gd["la_t"], do_gla, "gla_outer_q_" + nm, bdir, "qtil")
            gs = scan_lead(gd["dec"], wq, "gla_adj_scan_" + nm, reverse=not bdir, inclusive=False)
            dbq, dbk, dbv, dlas[nm] = gla_inner_bwd(sv["bq"], sv["bk"], sv["bv"], gd["la"], do_gla, gd["sp"], gs,
                                                    gd["dec"], "gla_inner_bwd_" + nm, bdir, add=acc)
            acc = (dbq, dbk, dbv)
        dzf, dzb, dpre_f, dpre_b, db2f, db2b = gla_gates_bwd(
            sv["zf"], sv["zb"], full["gla_w2_f"][l], full["gla_b2_f"][l], full["gla_w2_b"][l], full["gla_b2_b"][l],
            dlas["f"], dlas["b"], "gla_gates_bwd")
        grads["gla_b2_f"][l] = db2f.reshape(B_QK)
        grads["gla_b2_b"][l] = db2b.reshape(B_QK)
        grads["gla_w2_f"][l] = mm_plain(sv["zf"].T, dpre_f, "mm_dw_gate", hi=True)
        grads["gla_w2_b"][l] = mm_plain(sv["zb"].T, dpre_b, "mm_dw_gate", hi=True)
        dh, dcy = lru_out_bwd(sv["h0"], sv["h1"], sv["cy"], doc, "lru_out_bwd")
        g0 = scan_lead(lead(_shift(sv["a0"], 1)), lead(dh), "lru_scan_rev", reverse=True,
                       inclusive=True).reshape(t, C_WIDTH)
        g1 = scan_lead(lead(_shift(sv["a1"], -1)), lead(dh), "lru_scan_fwd", reverse=False,
                       inclusive=True).reshape(t, C_WIDTH)
        dxc, dzr0, dzi0, dzr1, dzi1, dlam, dba, dbx = lru_gates_bwd(
            sv["xc"], g0, _shift(sv["h0"], -1), g1, _shift(sv["h1"], 1), full["lru_wa"][l], full["lru_ba"][l],
            full["lru_wx"][l], full["lru_bx"][l], full["lru_lambda"][l], "lru_gates_bwd")
        xc_t = bf(sv["xc"].T)
        grads["lru_wa"][l] = jnp.stack([blockdiag_dw(xc_t, dzr0, "lru_dw"), blockdiag_dw(xc_t, dzr1, "lru_dw")])
        grads["lru_wx"][l] = jnp.stack([blockdiag_dw(xc_t, dzi0, "lru_dw"), blockdiag_dw(xc_t, dzi1, "lru_dw")])
        grads["lru_lambda"][l], grads["lru_ba"][l], grads["lru_bx"][l] = dlam, dba, dbx
        dcx, dcw, dcb = conv_bwd(dxc, _shift(dxc, 2), _shift(dxc, 1), _shift(dxc, -1), sv["cx"], sv["xm2"],
                                 sv["xm1"], sv["xp1"], full["conv_w"][l], "conv_bwd")
        grads["conv_w"][l] = dcw
        grads["conv_b"][l] = dcb.reshape(C_WIDTH)
        dproj = jnp.concatenate([daq, dak, dav, dbq, dbk, dbv, dbg, dzf, dzb, dcx, dcy], axis=1)
        dproj_pad = jnp.pad(dproj.reshape(t, N_DEV, W_IN_SHARD),
                            ((0, 0), (0, 0), (0, W_IN_ROWS - W_IN_SHARD))).reshape(t, N_DEV * W_IN_ROWS)
        g_in_t = mm_plain(bf(dproj_pad.T), sv["h_in"], "mm_dw_in", out_dtype=BF16)
        dh1 = mm_wk(dproj_pad, gw, off_in, W_IN_ROWS, "mm_w_in_dx", jb=2)
        dx, grads["norm_mix_pre"][l] = rms_bwd(sv["x"], full["norm_mix_pre"][l], dh1, "rms_bwd_add", add=dx1)
        parts = [g_in_t, g_out, g_xq, g_xk, g_xv, g_xo, g_up_t, g_down]
        gpack = jnp.concatenate([p.reshape(N_DEV, p.shape[0] // N_DEV, d) for p in parts], axis=1)
        big_grads[l] = reduce_scatter(gpack, "rs_weights")

    grad_rel = mm_plain(dbias_total.reshape(A_HEADS, -1), onehot_t, "rel_bias_grad", tb=True, hi=True).T

    small_names = [n for n in WEIGHTS if n not in BIG]
    small_g = {"rel_bias": grad_rel}
    for n in small_names:
        if n != "rel_bias":
            small_g[n] = jnp.stack([g.reshape(full[n].shape[1:]) for g in grads[n]])
    shapes = [small_g[n].shape for n in small_names]
    packed = all_gather(_pack([small_g[n] for n in small_names]), "ag_small_grads")
    summed = sum_lead(packed, tuple(range(N_DEV)), F32, "add8_small")
    small_g = dict(zip(small_names, _unpack(summed, shapes)))
    for n in SMALL_SHARDED:
        wdt = w[n].shape[-1]
        small_g[n] = lax.dynamic_slice_in_dim(small_g[n], my_idx * wdt, wdt, axis=small_g[n].ndim - 1)

    grad_out, delta, new_m, new_v = {}, {}, {}, {}
    sshapes = [w[n].shape for n in small_names]
    ds, ms, vs = adamw(_pack([small_g[n] for n in small_names]), _pack([w[n] for n in small_names]),
                       _pack([m[n] for n in small_names]), _pack([v[n] for n in small_names]), "adamw_small")
    for n, g_, d_, m_, v_ in zip(small_names, [small_g[n] for n in small_names], _unpack(ds, sshapes),
                                 _unpack(ms, sshapes), _unpack(vs, sshapes)):
        grad_out[n], delta[n], new_m[n], new_v[n] = g_, d_, m_, v_

    def rows(l, off, r):
        return big_grads[l][off:off + r]

    big_g = {
        "w_in": jnp.stack([rows(l, off_in, W_IN_SHARD).T for l in range(depth)]),
        "w_out": jnp.stack([rows(l, off_out, r_out) for l in range(depth)]),
        "xq": jnp.stack([rows(l, off_xq, rx) for l in range(depth)]),
        "xk": jnp.stack([rows(l, off_xk, rx) for l in range(depth)]),
        "xv": jnp.stack([rows(l, off_xv, rx) for l in range(depth)]),
        "xo": jnp.stack([rows(l, off_xo, rx) for l in range(depth)]),
        "w_up": jnp.stack([rows(l, off_up, rf).T for l in range(depth)]),
        "w_down": jnp.stack([rows(l, off_down, rf) for l in range(depth)]),
    }
    for n in BIG:
        grad_out[n] = big_g[n]
        delta[n], new_m[n], new_v[n] = adamw(big_g[n], w[n], m[n], v[n], "adamw_" + n)

    return (loss, dx.reshape(1, t, d), *[grad_out[n] for n in WEIGHTS], *[delta[n] for n in WEIGHTS],
            *[new_m[n] for n in WEIGHTS], *[new_v[n] for n in WEIGHTS])


def kernel(x, mem, rel_bias, w_in, w_out, attn_sink, gla_w2_f, gla_b2_f, gla_w2_b, gla_b2_b, gla_norm, conv_w, conv_b, lru_wa, lru_ba, lru_wx, lru_bx, lru_lambda, xq, xk, xv, xo, w_up, w_down, norm_mix_pre, norm_mix_post, norm_mem, norm_x_pre, norm_x_post, norm_ff_pre, norm_ff_post, loss_target, m_rel_bias, m_w_in, m_w_out, m_attn_sink, m_gla_w2_f, m_gla_b2_f, m_gla_w2_b, m_gla_b2_b, m_gla_norm, m_conv_w, m_conv_b, m_lru_wa, m_lru_ba, m_lru_wx, m_lru_bx, m_lru_lambda, m_xq, m_xk, m_xv, m_xo, m_w_up, m_w_down, m_norm_mix_pre, m_norm_mix_post, m_norm_mem, m_norm_x_pre, m_norm_x_post, m_norm_ff_pre, m_norm_ff_post, v_rel_bias, v_w_in, v_w_out, v_attn_sink, v_gla_w2_f, v_gla_b2_f, v_gla_w2_b, v_gla_b2_b, v_gla_norm, v_conv_w, v_conv_b, v_lru_wa, v_lru_ba, v_lru_wx, v_lru_bx, v_lru_lambda, v_xq, v_xk, v_xv, v_xo, v_w_up, v_w_down, v_norm_mix_pre, v_norm_mix_post, v_norm_mem, v_norm_x_pre, v_norm_x_post, v_norm_ff_pre, v_norm_ff_post):
    given = dict(locals())
    w = {n: given[n] for n in WEIGHTS}
    m = {n: given["m_" + n] for n in WEIGHTS}
    v = {n: given["v_" + n] for n in WEIGHTS}
    return _step(x, mem, loss_target, w, m, v)
```

```python
import math

import jax
import jax.numpy as jnp
import numpy as np
from jax import lax
from jax.experimental import pallas as pl
from jax.experimental.pallas import tpu as pltpu

F32 = jnp.float32
BF16 = jnp.bfloat16
HI = lax.Precision.HIGHEST
NN = (((1,), (0,)), ((), ()))
NT = (((1,), (1,)), ((), ()))
MESH = pl.DeviceIdType.MESH
AXES = ("x", "y", "c")
N_DEV = 8

A_HEAD_DIM = 128
A_HEADS = 8
A_KV_HEADS = 2
A_GROUP = 4
WINDOW = 128
BLOCK = 128
N_BUCKETS = 32
MAX_DISTANCE = 128
B_HEADS = 4
B_KEY_DIM = 64
B_VAL_DIM = 128
GATE_RANK = 16
GATE_TAU = 16.0
GLA_CHUNK = 16
C_WIDTH = 512
C_BLOCKS = 4
C_BLOCK_DIM = 128
LRU_C = 8.0
X_HEADS = 4
EPS = 1e-6
NEG_INF = -1e30
A_Q = A_HEADS * A_HEAD_DIM
A_KV = A_KV_HEADS * A_HEAD_DIM
B_QK = B_HEADS * B_KEY_DIM
B_V = B_HEADS * B_VAL_DIM
SPLIT_SIZES = (A_Q, A_KV, A_KV, B_QK, B_QK, B_V, B_V, GATE_RANK, GATE_RANK, C_WIDTH, C_WIDTH)
D_IN = sum(SPLIT_SIZES)
D_MIX = A_Q + B_V + C_WIDTH
W_IN_SHARD = D_IN // N_DEV
W_IN_ROWS = 768
GLA_TILE = 128
CHUNKS_PER_TILE = GLA_TILE // GLA_CHUNK
EXP_CLAMP = 80.0

ADAM_LR = 0.001
ADAM_B1 = 0.9
ADAM_B2 = 0.999
ADAM_EPS = 1e-08
ADAM_WD = 0.01
ADAM_STEP = 10

VMEM_LIMIT_BYTES = 48 * 1024 * 1024
SIBLING_STREAMS = 16
PACK_ELEMS = 128 * 2048


def _call(body, **kw):
    return pl.pallas_call(body, **kw)


def _cparams():
    return pltpu.CompilerParams(vmem_limit_bytes=VMEM_LIMIT_BYTES)


def _dot(a, b, dims=NN, hi=False):
    if hi:
        return lax.dot_general(a, b, dims, precision=HI, preferred_element_type=F32)
    return lax.dot_general(a.astype(BF16), b.astype(BF16), dims, preferred_element_type=F32)


def _sds(shape, dtype=F32):
    return jax.ShapeDtypeStruct(tuple(shape), dtype)


def _row_tile(rows, cols, target_elems=1 << 18):
    want = max(8, target_elems // max(cols, 1))
    if rows <= want:
        return rows
    t = (want // 8) * 8
    while t >= 8:
        if rows % t == 0:
            return t
        t -= 8
    return rows


def _expm1(x):
    poly = x * (1.0 + x * (1.0 / 2 + x * (1.0 / 6 + x * (1.0 / 24 + x * (1.0 / 120 + x * (
        1.0 / 720 + x * (1.0 / 5040 + x * (1.0 / 40320))))))))
    return jnp.where(jnp.abs(x) < 0.3, poly, jnp.exp(x) - 1.0)


def _log1p(e):
    w = 1.0 + e
    return jnp.where(w == 1.0, e, jnp.log(w) * e / (w - 1.0))


def _softplus(x):
    return jnp.maximum(x, 0.0) + _log1p(jnp.exp(-jnp.abs(x)))


def _sigmoid(x):
    return jax.nn.sigmoid(x)


GELU_K = math.sqrt(2.0 / math.pi)


def _gelu(y):
    t = jnp.tanh(GELU_K * (y + 0.044715 * y * y * y))
    return 0.5 * y * (1.0 + t)


def _gelu_grad(y):
    t = jnp.tanh(GELU_K * (y + 0.044715 * y * y * y))
    return 0.5 * (1.0 + t) + 0.5 * y * (1.0 - t * t) * GELU_K * (1.0 + 3 * 0.044715 * y * y)


def rms_fwd(x, g, name):
    m, d = x.shape
    tm = _row_tile(m, d)

    def body(x_ref, g_ref, o_ref):
        xv = x_ref[...]
        r = lax.rsqrt(jnp.mean(xv * xv, axis=1, keepdims=True) + EPS)
        o_ref[...] = xv * r * g_ref[...]

    return _call(body, name=name, grid=(m // tm,),
                 in_specs=[pl.BlockSpec((tm, d), lambda i: (i, 0)), pl.BlockSpec((1, d), lambda i: (0, 0))],
                 out_specs=pl.BlockSpec((tm, d), lambda i: (i, 0)),
                 out_shape=_sds((m, d)))(x, g.reshape(1, d))


def resid_rms(xres, mid, g_post, g_pre, name):
    m, d = xres.shape
    tm = _row_tile(m, d)
    with_pre = g_pre is not None

    def body(*refs):
        if with_pre:
            x_ref, m_ref, gp_ref, gn_ref, xo_ref, h_ref = refs
        else:
            x_ref, m_ref, gp_ref, xo_ref = refs
        mv = m_ref[...]
        r = lax.rsqrt(jnp.mean(mv * mv, axis=1, keepdims=True) + EPS)
        xn = x_ref[...] + mv * r * gp_ref[...]
        xo_ref[...] = xn
        if with_pre:
            r2 = lax.rsqrt(jnp.mean(xn * xn, axis=1, keepdims=True) + EPS)
            h_ref[...] = xn * r2 * gn_ref[...]

    row = pl.BlockSpec((tm, d), lambda i: (i, 0))
    vec = pl.BlockSpec((1, d), lambda i: (0, 0))
    ins = [xres, mid, g_post.reshape(1, d)] + ([g_pre.reshape(1, d)] if with_pre else [])
    in_specs = [row, row, vec] + ([vec] if with_pre else [])
    if with_pre:
        return _call(body, name=name, grid=(m // tm,), in_specs=in_specs, out_specs=(row, row),
                     out_shape=(_sds((m, d)), _sds((m, d))))(*ins)
    return _call(body, name=name, grid=(m // tm,), in_specs=in_specs, out_specs=row,
                 out_shape=_sds((m, d)))(*ins)


def rms_bwd(x, g, dy, name, dy2=None, add=None):
    m, d = x.shape
    tm = _row_tile(m, d)
    has2, hasadd = dy2 is not None, add is not None

    def body(*refs):
        it = iter(refs)
        x_ref, g_ref, dy_ref = next(it), next(it), next(it)
        dy2_ref = next(it) if has2 else None
        add_ref = next(it) if hasadd else None
        dx_ref, dg_ref = next(it), next(it)
        xv = x_ref[...]
        dyv = dy_ref[...]
        if has2:
            dyv = dyv + dy2_ref[...]
        r = lax.rsqrt(jnp.mean(xv * xv, axis=1, keepdims=True) + EPS)
        xh = xv * r
        dxh = dyv * g_ref[...]
        dx = r * (dxh - xh * jnp.mean(dxh * xh, axis=1, keepdims=True))
        if hasadd:
            dx = dx + add_ref[...]
        dx_ref[...] = dx
        part = jnp.sum(dyv * xh, axis=0, keepdims=True)

        @pl.when(pl.program_id(0) == 0)
        def _():
            dg_ref[...] = part

        @pl.when(pl.program_id(0) > 0)
        def _():
            dg_ref[...] += part

    row = pl.BlockSpec((tm, d), lambda i: (i, 0))
    vec = pl.BlockSpec((1, d), lambda i: (0, 0))
    ins = [x, g.reshape(1, d), dy] + ([dy2] if has2 else []) + ([add] if hasadd else [])
    in_specs = [row, vec, row] + ([row] if has2 else []) + ([row] if hasadd else [])
    return _call(body, name=name, grid=(m // tm,), in_specs=in_specs, out_specs=(row, vec),
                 out_shape=(_sds((m, d)), _sds((1, d))))(*ins)


def loss_and_grad(y, target, name):
    m, d = y.shape
    tm = _row_tile(m, d)

    def body(y_ref, t_ref, dy_ref, l_ref):
        e = y_ref[...] - t_ref[...]
        dy_ref[...] = e * (1.0 / d)
        s = jnp.sum(jnp.sum(e * e, axis=1, keepdims=True), axis=0, keepdims=True) * (0.5 / d)
        part = jnp.broadcast_to(s, (1, 128))

        @pl.when(pl.program_id(0) == 0)
        def _():
            l_ref[...] = part

        @pl.when(pl.program_id(0) > 0)
        def _():
            l_ref[...] += part

    row = pl.BlockSpec((tm, d), lambda i: (i, 0))
    dy, l = _call(body, name=name, grid=(m // tm,), in_specs=[row, row],
                  out_specs=(row, pl.BlockSpec((1, 128), lambda i: (0, 0))),
                  out_shape=(_sds((m, d)), _sds((1, 128))))(y, target)
    return dy, l[0, 0]


def adamw(g, w, m, v, name):
    shape = w.shape
    cols = shape[-1]
    rows = int(np.prod(shape[:-1]))
    tm = _row_tile(rows, cols)
    c1 = 1.0 - ADAM_B1 ** ADAM_STEP
    c2 = 1.0 - ADAM_B2 ** ADAM_STEP

    def body(g_ref, w_ref, m_ref, v_ref, d_ref, mo_ref, vo_ref):
        gv = g_ref[...]
        mn = ADAM_B1 * m_ref[...] + (1.0 - ADAM_B1) * gv
        vn = ADAM_B2 * v_ref[...] + (1.0 - ADAM_B2) * (gv * gv)
        m_hat = mn / c1
        v_hat = vn / c2
        d_ref[...] = -ADAM_LR * (m_hat / (jnp.sqrt(v_hat) + ADAM_EPS) + ADAM_WD * w_ref[...])
        mo_ref[...] = mn
        vo_ref[...] = vn

    row = pl.BlockSpec((tm, cols), lambda i: (i, 0))
    outs = _call(body, name=name, grid=(rows // tm,), in_specs=[row] * 4, out_specs=(row,) * 3,
                 out_shape=(_sds((rows, cols)),) * 3)(*[a.reshape(rows, cols) for a in (g, w, m, v)])
    return tuple(o.reshape(shape) for o in outs)


def sum_lead(x, order, out_dtype, name):
    n, rows, cols = x.shape
    tm = _row_tile(rows, cols)

    def body(x_ref, o_ref):
        acc = x_ref[order[0]].astype(F32)
        for i in order[1:]:
            acc = acc + x_ref[i].astype(F32)
        o_ref[...] = acc.astype(out_dtype)

    return _call(body, name=name, grid=(rows // tm,), in_specs=[pl.BlockSpec((n, tm, cols), lambda i: (0, i, 0))],
                 out_specs=pl.BlockSpec((tm, cols), lambda i: (i, 0)), out_shape=_sds((rows, cols), out_dtype))(x)


def add_own_lead(own, parts, name):
    n, rows, cols = parts.shape
    tm = _row_tile(rows, cols)

    def body(o_ref, p_ref, out_ref):
        acc = o_ref[...].astype(F32)
        for i in range(n):
            acc = acc + p_ref[i].astype(F32)
        out_ref[...] = acc

    row = pl.BlockSpec((tm, cols), lambda i: (i, 0))
    return _call(body, name=name, grid=(rows // tm,),
                 in_specs=[row, pl.BlockSpec((n, tm, cols), lambda i: (0, i, 0))], out_specs=row,
                 out_shape=_sds((rows, cols)))(own, parts)


def add_n(xs, out_dtype, name):
    shape = xs[0].shape
    cols = shape[-1]
    rows = int(np.prod(shape[:-1]))
    tm = _row_tile(rows, cols)
    n = len(xs)

    def body(*refs):
        acc = refs[0][...].astype(F32)
        for r in refs[1:n]:
            acc = acc + r[...].astype(F32)
        refs[n][...] = acc.astype(out_dtype)

    row = pl.BlockSpec((tm, cols), lambda i: (i, 0))
    out = _call(body, name=name, grid=(rows // tm,), in_specs=[row] * n, out_specs=row,
                out_shape=_sds((rows, cols), out_dtype))(*[a.reshape(rows, cols) for a in xs])
    return out.reshape(shape)


def mm_plain(a, b, name, tb=False, out_dtype=F32, hi=False, tm=512, tn=512):
    m, k = a.shape
    n = b.shape[0] if tb else b.shape[1]
    tm, tn = min(tm, m), min(tn, n)

    def body(a_ref, b_ref, o_ref):
        o_ref[...] = _dot(a_ref[...], b_ref[...], NT if tb else NN, hi).astype(out_dtype)

    b_spec = pl.BlockSpec((tn, k), lambda i, j: (j, 0)) if tb else pl.BlockSpec((k, tn), lambda i, j: (0, j))
    return _call(body, name=name, grid=(m // tm, n // tn),
                 in_specs=[pl.BlockSpec((tm, k), lambda i, j: (i, 0)), b_spec],
                 out_specs=pl.BlockSpec((tm, tn), lambda i, j: (i, j)),
                 out_shape=_sds((m, n), out_dtype), compiler_params=_cparams())(a, b)


def mm_wk(a, gw, off, r, name, jb=N_DEV, relu2=False, tm=512, tn=512):
    m = a.shape[0]
    d = gw.shape[2]
    tm, tn = min(tm, m), min(tn, d)
    nk = N_DEV // jb
    ob = off // r
    assert off % r == 0 and a.shape[1] == N_DEV * r

    def body(a_ref, b_ref, o_ref, *acc):
        av = a_ref[...]
        if relu2:
            av = jnp.square(jnp.maximum(av, 0.0))
        av = av.astype(BF16)
        p = _dot(av[:, 0:r], b_ref[0])
        for q in range(1, jb):
            p = p + _dot(av[:, q * r:(q + 1) * r], b_ref[q])
        if nk == 1:
            o_ref[...] = p
        else:
            kk = pl.program_id(2)

            @pl.when(kk == 0)
            def _():
                acc[0][...] = p

            @pl.when(kk > 0)
            def _():
                acc[0][...] += p

            @pl.when(kk == nk - 1)
            def _():
                o_ref[...] = acc[0][...]

    return _call(body, name=name, grid=(m // tm, d // tn, nk),
                 in_specs=[pl.BlockSpec((tm, jb * r), lambda i, j, k: (i, k)),
                           pl.BlockSpec((jb, r, tn), lambda i, j, k: (k, ob, j))],
                 out_specs=pl.BlockSpec((tm, tn), lambda i, j, k: (i, j)),
                 out_shape=_sds((m, d)),
                 scratch_shapes=([pltpu.VMEM((tm, tn), F32)] if nk > 1 else []),
                 compiler_params=_cparams())(a, gw)


def mm_wn(a, gw, off, r, name, relu_grad_of=None, tm=512):
    m, d = a.shape
    tm = min(tm, m)
    ob = off // r
    assert off % r == 0 and gw.shape[2] == d
    epi = relu_grad_of is not None

    def body(*refs):
        if epi:
            a_ref, b_ref, e_ref, o_ref = refs
        else:
            a_ref, b_ref, o_ref = refs
        p = _dot(a_ref[...], b_ref[...], NT)
        if epi:
            p = p * (2.0 * jnp.maximum(e_ref[...], 0.0))
        o_ref[...] = p

    blk = pl.BlockSpec((tm, r), lambda i, j: (i, j))
    in_specs = [pl.BlockSpec((tm, d), lambda i, j: (i, 0)), pl.BlockSpec((None, r, d), lambda i, j: (j, ob, 0))]
    ins = [a, gw]
    if epi:
        in_specs.append(blk)
        ins.append(relu_grad_of)
    return _call(body, name=name, grid=(m // tm, N_DEV), in_specs=in_specs, out_specs=blk,
                 out_shape=_sds((m, N_DEV * r)), compiler_params=_cparams())(*ins)


def blockdiag_dw(xt, dz, name):
    t = xt.shape[1]

    def body(a_ref, b_ref, o_ref):
        o_ref[...] = _dot(a_ref[...], b_ref[...])

    return _call(body, name=name, grid=(C_BLOCKS,),
                 in_specs=[pl.BlockSpec((C_BLOCK_DIM, t), lambda g: (g, 0)),
                           pl.BlockSpec((t, C_BLOCK_DIM), lambda g: (0, g))],
                 out_specs=pl.BlockSpec((None, C_BLOCK_DIM, C_BLOCK_DIM), lambda g: (g, 0, 0)),
                 out_shape=_sds((C_BLOCKS, C_BLOCK_DIM, C_BLOCK_DIM)))(xt, dz)


def _band_mask(n, nblk, transposed):
    shape = (3 * BLOCK, BLOCK) if transposed else (BLOCK, 3 * BLOCK)
    qi = lax.broadcasted_iota(jnp.int32, shape, 1 if transposed else 0)
    kj = lax.broadcasted_iota(jnp.int32, shape, 0 if transposed else 1)
    lo = jnp.where(n > 0, 0, BLOCK)
    hi = jnp.where(n < nblk - 1, 3 * BLOCK, 2 * BLOCK)
    return (jnp.abs(kj - BLOCK - qi) <= WINDOW) & (kj >= lo) & (kj < hi)


def _band_rows(ref, n, nblk):
    starts = [jnp.maximum(n - 1, 0), n, jnp.minimum(n + 1, nblk - 1)]
    return jnp.concatenate([ref[pl.ds(pl.multiple_of(s * BLOCK, BLOCK), BLOCK), :] for s in starts], axis=0)


def attn_fwd(q, k, v, bias, sink_b, name):
    t = q.shape[0]
    nblk = t // BLOCK
    scale = A_HEAD_DIM ** -0.5

    def body(q_ref, k_ref, v_ref, b_ref, s_ref, o_ref):
        n = pl.program_id(1)
        kb = _band_rows(k_ref, n, nblk).astype(BF16)
        vb = _band_rows(v_ref, n, nblk).astype(BF16)
        mask = _band_mask(n, nblk, False)
        for j in range(A_GROUP):
            sl = slice(j * A_HEAD_DIM, (j + 1) * A_HEAD_DIM)
            s = _dot(q_ref[:, sl], kb, NT) * scale + b_ref[j]
            s = jnp.where(mask, s, NEG_INF)
            sk = s_ref[j:j + 1, 0:1]
            mx = jnp.maximum(jnp.max(s, axis=1, keepdims=True), sk)
            p = jnp.exp(s - mx)
            den = jnp.sum(p, axis=1, keepdims=True) + jnp.exp(sk - mx)
            o_ref[:, sl] = _dot(p / den, vb)

    gw = A_GROUP * A_HEAD_DIM
    return _call(body, name=name, grid=(A_KV_HEADS, nblk),
                 in_specs=[pl.BlockSpec((BLOCK, gw), lambda g, n: (n, g)),
                           pl.BlockSpec((t, A_HEAD_DIM), lambda g, n: (0, g)),
                           pl.BlockSpec((t, A_HEAD_DIM), lambda g, n: (0, g)),
                           pl.BlockSpec((A_GROUP, BLOCK, 3 * BLOCK), lambda g, n: (g, 0, 0)),
                           pl.BlockSpec((None, 8, 128), lambda g, n: (g, 0, 0))],
                 out_specs=pl.BlockSpec((BLOCK, gw), lambda g, n: (n, g)),
                 out_shape=_sds((t, A_Q)))(q, k, v, bias, sink_b)


def attn_bwd(q, k, v, bias, bias_t, sink_b, do, o, name):
    t = q.shape[0]
    nblk = t // BLOCK
    scale = A_HEAD_DIM ** -0.5

    def body(q_ref, k_ref, v_ref, b_ref, bt_ref, s_ref, do_ref, o_ref, dq_ref, dk_ref, dv_ref, db_ref, ds_ref):
        n = pl.program_id(1)

        @pl.when(n == 0)
        def _():
            dk_ref[...] = jnp.zeros_like(dk_ref)
            dv_ref[...] = jnp.zeros_like(dv_ref)
            db_ref[...] = jnp.zeros_like(db_ref)
            ds_ref[...] = jnp.zeros_like(ds_ref)

        kb = _band_rows(k_ref, n, nblk).astype(BF16)
        vb = _band_rows(v_ref, n, nblk).astype(BF16)
        mask = _band_mask(n, nblk, False)
        mask_t = _band_mask(n, nblk, True)
        ones8 = jnp.ones((8, A_HEAD_DIM), F32)
        dkb = jnp.zeros((3 * BLOCK, A_HEAD_DIM), F32)
        dvb = jnp.zeros((3 * BLOCK, A_HEAD_DIM), F32)
        for j in range(A_GROUP):
            sl = slice(j * A_HEAD_DIM, (j + 1) * A_HEAD_DIM)
            qj = q_ref[:, sl].astype(BF16)
            doj = do_ref[:, sl]
            doo = doj * o_ref[:, sl]
            doj = doj.astype(BF16)
            sk = s_ref[j:j + 1, 0:1]
            s = jnp.where(mask, _dot(qj, kb, NT) * scale + b_ref[j], NEG_INF)
            mx = jnp.maximum(jnp.max(s, axis=1, keepdims=True), sk)
            p = jnp.exp(s - mx)
            den = jnp.sum(p, axis=1, keepdims=True) + jnp.exp(sk - mx)
            p = p / den
            psink = jnp.exp(sk - mx) / den
            delta = jnp.sum(doo, axis=1, keepdims=True)
            dsc = p * (_dot(doj, vb, NT) - delta)
            db_ref[j] += dsc
            ds_ref[j:j + 1, :] += jnp.broadcast_to(-jnp.sum(psink * delta, axis=0, keepdims=True), (1, 128))
            dq_ref[:, sl] = _dot(dsc, kb) * scale
            st = jnp.where(mask_t, _dot(kb, qj, NT) * scale + bt_ref[j], NEG_INF)
            mxt = jnp.maximum(jnp.max(st, axis=0, keepdims=True), sk)
            pt = jnp.exp(st - mxt)
            dent = jnp.sum(pt, axis=0, keepdims=True) + jnp.exp(sk - mxt)
            pt = pt / dent
            delta_t = _dot(ones8, doo, NT, hi=True)[0:1, :]
            dst = pt * (_dot(vb, doj, NT) - delta_t)
            dkb = dkb + _dot(dst, qj) * scale
            dvb = dvb + _dot(pt, doj)
        starts = [jnp.maximum(n - 1, 0), n, jnp.minimum(n + 1, nblk - 1)]
        for c, st_ in enumerate(starts):
            rows = pl.ds(pl.multiple_of(st_ * BLOCK, BLOCK), BLOCK)
            dk_ref[rows, :] += dkb[c * BLOCK:(c + 1) * BLOCK, :]
            dv_ref[rows, :] += dvb[c * BLOCK:(c + 1) * BLOCK, :]

    gw = A_GROUP * A_HEAD_DIM
    qspec = pl.BlockSpec((BLOCK, gw), lambda g, n: (n, g))
    kspec = pl.BlockSpec((t, A_HEAD_DIM), lambda g, n: (0, g))
    sspec = pl.BlockSpec((None, 8, 128), lambda g, n: (g, 0, 0))
    bspec = pl.BlockSpec((A_GROUP, BLOCK, 3 * BLOCK), lambda g, n: (g, 0, 0))
    btspec = pl.BlockSpec((A_GROUP, 3 * BLOCK, BLOCK), lambda g, n: (g, 0, 0))
    return _call(body, name=name, grid=(A_KV_HEADS, nblk),
                 in_specs=[qspec, kspec, kspec, bspec, btspec, sspec, qspec, qspec],
                 out_specs=(qspec, kspec, kspec, bspec, sspec),
                 out_shape=(_sds((t, A_Q)), _sds((t, A_KV)), _sds((t, A_KV)),
                            _sds((A_HEADS, BLOCK, 3 * BLOCK)), _sds((A_KV_HEADS, 8, 128))),
                 compiler_params=_cparams())(q, k, v, bias, bias_t, sink_b, do, o)


def xattn_fwd(q, k, v, name):
    t, d = q.shape
    ml = k.shape[0]
    dh = d // X_HEADS
    tq = min(256, t)
    scale = dh ** -0.5

    def body(q_ref, k_ref, v_ref, o_ref):
        s = _dot(q_ref[...], k_ref[...], NT) * scale
        p = jnp.exp(s - jnp.max(s, axis=1, keepdims=True))
        p = p / jnp.sum(p, axis=1, keepdims=True)
        o_ref[...] = _dot(p, v_ref[...])

    qspec = pl.BlockSpec((tq, dh), lambda h, i: (i, h))
    kspec = pl.BlockSpec((ml, dh), lambda h, i: (0, h))
    return _call(body, name=name, grid=(X_HEADS, t // tq), in_specs=[qspec, kspec, kspec], out_specs=qspec,
                 out_shape=_sds((t, d)))(q, k, v)


def xattn_bwd(q, k, v, o, do, name):
    t, d = q.shape
    ml = k.shape[0]
    dh = d // X_HEADS
    tq = min(256, t)
    scale = dh ** -0.5

    def body(q_ref, k_ref, v_ref, o_ref, do_ref, dq_ref, dk_ref, dv_ref):
        i = pl.program_id(1)
        qv, kv, vv = q_ref[...].astype(BF16), k_ref[...].astype(BF16), v_ref[...].astype(BF16)
        dov = do_ref[...]
        doo = dov * o_ref[...]
        dov = dov.astype(BF16)
        s = _dot(qv, kv, NT) * scale
        p = jnp.exp(s - jnp.max(s, axis=1, keepdims=True))
        p = p / jnp.sum(p, axis=1, keepdims=True)
        ds = p * (_dot(dov, vv, NT) - jnp.sum(doo, axis=1, keepdims=True))
        dq_ref[...] = _dot(ds, kv) * scale
        st = _dot(kv, qv, NT) * scale
        pt = jnp.exp(st - jnp.max(st, axis=0, keepdims=True))
        pt = pt / jnp.sum(pt, axis=0, keepdims=True)
        delta_t = _dot(jnp.ones((8, dh), F32), doo, NT, hi=True)[0:1, :]
        dst = pt * (_dot(vv, dov, NT) - delta_t)
        dkp = _dot(dst, qv) * scale
        dvp = _dot(pt, dov)

        @pl.when(i == 0)
        def _():
            dk_ref[...] = dkp
            dv_ref[...] = dvp

        @pl.when(i > 0)
        def _():
            dk_ref[...] += dkp
            dv_ref[...] += dvp

    qspec = pl.BlockSpec((tq, dh), lambda h, i: (i, h))
    kspec = pl.BlockSpec((ml, dh), lambda h, i: (0, h))
    return _call(body, name=name, grid=(X_HEADS, t // tq), in_specs=[qspec, kspec, kspec, qspec, qspec],
                 out_specs=(qspec, kspec, kspec),
                 out_shape=(_sds((t, d)), _sds((ml, d)), _sds((ml, d))))(q, k, v, o, do)


def scan_lead(a, u, name, reverse, inclusive):
    n, r, c = a.shape
    blk = max(1, min(n, (1 << 18) // (max(r, 8) * c)))
    while n % blk:
        blk -= 1
    nb = n // blk

    def body(a_ref, u_ref, o_ref, carry):
        @pl.when(pl.program_id(0) == 0)
        def _():
            carry[...] = jnp.zeros_like(carry)

        def step(s, h):
            idx = (blk - 1 - s) if reverse else s
            hn = a_ref[idx] * h + u_ref[idx]
            o_ref[idx] = hn if inclusive else h
            return hn

        carry[...] = lax.fori_loop(0, blk, step, carry[...])

    spec = pl.BlockSpec((blk, r, c), (lambda i: (nb - 1 - i, 0, 0)) if reverse else (lambda i: (i, 0, 0)))
    return _call(body, name=name, grid=(nb,), in_specs=[spec, spec], out_specs=spec,
                 out_shape=_sds((n, r, c)), scratch_shapes=[pltpu.VMEM((r, c), F32)])(a, u)


def _chunk_mats(bwd_dir):
    i = lax.broadcasted_iota(jnp.int32, (GLA_TILE, GLA_TILE), 0)
    j = lax.broadcasted_iota(jnp.int32, (GLA_TILE, GLA_TILE), 1)
    same = lax.shift_right_logical(i, 4) == lax.shift_right_logical(j, 4)
    if bwd_dir:
        cm, cm_t = same & (j >= i), same & (i >= j)
        mk, mk_t = same & (j > i), same & (i > j)
    else:
        cm, cm_t = same & (j <= i), same & (i <= j)
        mk, mk_t = same & (j <= i), same & (i <= j)
    f = lambda b: jnp.where(b, 1.0, 0.0).astype(F32)
    return f(cm), f(cm_t), mk, mk_t, f(same)


def gla_gates_fwd(zf, zb, w2f, b2f, w2b, b2b, name):
    t = zf.shape[0]
    tm = min(256, t)

    def body(zf_ref, zb_ref, wf_ref, bf_ref, wb_ref, bb_ref, lf_ref, lb_ref):
        lf_ref[...] = -_softplus(-(_dot(zf_ref[...], wf_ref[...], hi=True) + bf_ref[...])) / GATE_TAU
        lb_ref[...] = -_softplus(-(_dot(zb_ref[...], wb_ref[...], hi=True) + bb_ref[...])) / GATE_TAU

    zs = pl.BlockSpec((tm, GATE_RANK), lambda i: (i, 0))
    ws = pl.BlockSpec((GATE_RANK, B_QK), lambda i: (0, 0))
    bs = pl.BlockSpec((1, B_QK), lambda i: (0, 0))
    os_ = pl.BlockSpec((tm, B_QK), lambda i: (i, 0))
    return _call(body, name=name, grid=(t // tm,), in_specs=[zs, zs, ws, bs, ws, bs], out_specs=(os_, os_),
                 out_shape=(_sds((t, B_QK)),) * 2)(zf, zb, w2f, b2f.reshape(1, B_QK), w2b, b2b.reshape(1, B_QK))


def gla_gates_bwd(zf, zb, w2f, b2f, w2b, b2b, dlf, dlb, name):
    t = zf.shape[0]
    tm = min(256, t)

    def body(zf_ref, zb_ref, wf_ref, bf_ref, wb_ref, bb_ref, dlf_ref, dlb_ref,
             dzf_ref, dzb_ref, dpf_ref, dpb_ref, dbf_ref, dbb_ref):
        first = pl.program_id(0) == 0
        for z_ref, w_ref, b_ref, dl_ref, dz_ref, dp_ref, db_ref in (
                (zf_ref, wf_ref, bf_ref, dlf_ref, dzf_ref, dpf_ref, dbf_ref),
                (zb_ref, wb_ref, bb_ref, dlb_ref, dzb_ref, dpb_ref, dbb_ref)):
            pre = _dot(z_ref[...], w_ref[...], hi=True) + b_ref[...]
            dpre = dl_ref[...] * (1.0 / GATE_TAU) * _sigmoid(-pre)
            dp_ref[...] = dpre
            dz_ref[...] = _dot(dpre, w_ref[...], NT, hi=True)
            part = jnp.sum(dpre, axis=0, keepdims=True)

            @pl.when(first)
            def _():
                db_ref[...] = part

            @pl.when(jnp.logical_not(first))
            def _():
                db_ref[...] += part

    zs = pl.BlockSpec((tm, GATE_RANK), lambda i: (i, 0))
    ws = pl.BlockSpec((GATE_RANK, B_QK), lambda i: (0, 0))
    bs = pl.BlockSpec((1, B_QK), lambda i: (0, 0))
    os_ = pl.BlockSpec((tm, B_QK), lambda i: (i, 0))
    return _call(body, name=name, grid=(t // tm,), in_specs=[zs, zs, ws, bs, ws, bs, os_, os_],
                 out_specs=(zs, zs, os_, os_, bs, bs),
                 out_shape=(_sds((t, GATE_RANK)),) * 2 + (_sds((t, B_QK)),) * 2 + (_sds((1, B_QK)),) * 2)(
        zf, zb, w2f, b2f.reshape(1, B_QK), w2b, b2b.reshape(1, B_QK), dlf, dlb)


def gla_outer(xt, lat, y, name, bwd_dir, mode):
    t = y.shape[0]
    nchunk = t // GLA_CHUNK
    khat = mode == "khat"
    scale = B_KEY_DIM ** -0.5

    def body(xt_ref, lat_ref, y_ref, *outs):
        _, cm_t, _, _, same = _chunk_mats(bwd_dir)
        lat_v = lat_ref[...]
        bt = _dot(lat_v, cm_t, hi=True)
        if khat:
            mult = jnp.exp(_dot(lat_v, same, hi=True) - bt)
        else:
            mult = jnp.exp(bt) * scale
        xm = xt_ref[...] * mult
        lane = lax.shift_right_logical(lax.broadcasted_iota(jnp.int32, (1, GLA_TILE), 1), 4)
        ones = jnp.ones((GLA_TILE, B_VAL_DIM), F32)
        yv = [y_ref[:, h * B_VAL_DIM:(h + 1) * B_VAL_DIM].astype(BF16) for h in range(B_HEADS)]
        for c in range(CHUNKS_PER_TILE):
            sel = lane == c
            xc = jnp.where(sel, xm, 0.0).astype(BF16)
            for h in range(B_HEADS):
                rows = slice(h * B_KEY_DIM, (h + 1) * B_KEY_DIM)
                outs[0][c, rows, :] = _dot(xc[rows, :], yv[h])
            if khat:
                outs[1][c] = jnp.exp(_dot(jnp.where(sel, lat_v, 0.0), ones, hi=True))

    tspec = pl.BlockSpec((B_QK, GLA_TILE), lambda i: (0, i))
    ospec = pl.BlockSpec((CHUNKS_PER_TILE, B_QK, B_VAL_DIM), lambda i: (i, 0, 0))
    oshape = _sds((nchunk, B_QK, B_VAL_DIM))
    return _call(body, name=name, grid=(t // GLA_TILE,),
                 in_specs=[tspec, tspec, pl.BlockSpec((GLA_TILE, B_V), lambda i: (i, 0))],
                 out_specs=(ospec, ospec) if khat else ospec,
                 out_shape=(oshape, oshape) if khat else oshape)(xt, lat, y)


def _head_lane_mask(h):
    lane = lax.broadcasted_iota(jnp.int32, (1, B_QK), 1)
    return lax.shift_right_logical(lane, 6) == h


def _chunk_rows(c):
    return slice(c * GLA_CHUNK, (c + 1) * GLA_CHUNK)


def gla_inner_fwd(q, k, v, la, sp, name, bwd_dir):
    t = q.shape[0]
    scale = B_KEY_DIM ** -0.5

    def body(q_ref, k_ref, v_ref, la_ref, sp_ref, o_ref):
        cm, _, mk, _, _ = _chunk_mats(bwd_dir)
        b = _dot(cm, la_ref[...], hi=True)
        qt = q_ref[...] * scale * jnp.exp(b)
        kt = k_ref[...] * jnp.exp(jnp.minimum(-b, EXP_CLAMP))
        spb = [sp_ref[c].astype(BF16) for c in range(CHUNKS_PER_TILE)]
        for h in range(B_HEADS):
            lm = _head_lane_mask(h)
            qm = jnp.where(lm, qt, 0.0).astype(BF16)
            km = jnp.where(lm, kt, 0.0).astype(BF16)
            vs = slice(h * B_VAL_DIM, (h + 1) * B_VAL_DIM)
            att = jnp.where(mk, _dot(qm, km, NT), 0.0)
            inter = jnp.concatenate([_dot(qm[_chunk_rows(c), :], spb[c]) for c in range(CHUNKS_PER_TILE)], axis=0)
            o_ref[:, vs] = _dot(att, v_ref[:, vs]) + inter

    qs = pl.BlockSpec((GLA_TILE, B_QK), lambda i: (i, 0))
    vs_ = pl.BlockSpec((GLA_TILE, B_V), lambda i: (i, 0))
    ss = pl.BlockSpec((CHUNKS_PER_TILE, B_QK, B_VAL_DIM), lambda i: (i, 0, 0))
    return _call(body, name=name, grid=(t // GLA_TILE,), in_specs=[qs, qs, vs_, qs, ss], out_specs=vs_,
                 out_shape=_sds((t, B_V)))(q, k, v, la, sp)


def gla_inner_bwd(q, k, v, la, do, sp, gs, dec, name, bwd_dir, add=None):
    t = q.shape[0]
    scale = B_KEY_DIM ** -0.5
    hasadd = add is not None

    def body(*refs):
        it = iter(refs)
        q_ref, k_ref, v_ref, la_ref, do_ref, sp_ref, gs_ref, dec_ref = [next(it) for _ in range(8)]
        adds = [next(it) for _ in range(3)] if hasadd else None
        dq_ref, dk_ref, dv_ref, dla_ref = [next(it) for _ in range(4)]
        cm, cm_t, mk, mk_t, same = _chunk_mats(bwd_dir)
        la_v = la_ref[...]
        b = _dot(cm, la_v, hi=True)
        btot = _dot(same, la_v, hi=True)
        eb = jnp.exp(b)
        ek = jnp.exp(jnp.minimum(-b, EXP_CLAMP))
        ekh = jnp.exp(btot - b)
        qt = q_ref[...] * scale * eb
        kt = k_ref[...] * ek
        kh = k_ref[...] * ekh
        spb = [sp_ref[c].astype(BF16) for c in range(CHUNKS_PER_TILE)]
        gsb = [gs_ref[c].astype(BF16) for c in range(CHUNKS_PER_TILE)]
        dqt = jnp.zeros((GLA_TILE, B_QK), F32)
        dkt = jnp.zeros((GLA_TILE, B_QK), F32)
        dkh = jnp.zeros((GLA_TILE, B_QK), F32)
        for h in range(B_HEADS):
            lm = _head_lane_mask(h)
            qm = jnp.where(lm, qt, 0.0).astype(BF16)
            km = jnp.where(lm, kt, 0.0).astype(BF16)
            khm = jnp.where(lm, kh, 0.0).astype(BF16)
            vs = slice(h * B_VAL_DIM, (h + 1) * B_VAL_DIM)
            vh = v_ref[:, vs].astype(BF16)
            doh = do_ref[:, vs].astype(BF16)
            da = jnp.where(mk, _dot(doh, vh, NT), 0.0)
            da_t = jnp.where(mk_t, _dot(vh, doh, NT), 0.0)
            att_t = jnp.where(mk_t, _dot(km, qm, NT), 0.0)
            dv_h = _dot(att_t, doh) + jnp.concatenate(
                [_dot(khm[_chunk_rows(c), :], gsb[c]) for c in range(CHUNKS_PER_TILE)], axis=0)
            if hasadd:
                dv_h = dv_h + adds[2][:, vs]
            dv_ref[:, vs] = dv_h
            dq_inter = jnp.concatenate(
                [_dot(doh[_chunk_rows(c), :], spb[c], NT) for c in range(CHUNKS_PER_TILE)], axis=0)
            dqt = dqt + _dot(da, km) + jnp.where(lm, dq_inter, 0.0)
            dkt = dkt + _dot(da_t, qm)
            dkh_inter = jnp.concatenate(
                [_dot(vh[_chunk_rows(c), :], gsb[c], NT) for c in range(CHUNKS_PER_TILE)], axis=0)
            dkh = dkh + jnp.where(lm, dkh_inter, 0.0)
        dq = dqt * scale * eb
        dk = dkt * ek + dkh * ekh
        if hasadd:
            dq = dq + adds[0][...]
            dk = dk + adds[1][...]
        dq_ref[...] = dq
        dk_ref[...] = dk
        db = dqt * qt - dkt * kt - dkh * kh
        ones16 = jnp.ones((GLA_CHUNK, B_VAL_DIM), F32)
        t2 = jnp.concatenate(
            [_dot(ones16, gs_ref[c] * dec_ref[c] * sp_ref[c], NT, hi=True) for c in range(CHUNKS_PER_TILE)], axis=0)
        dla_ref[...] = _dot(cm_t, db, hi=True) + _dot(same, dkh * kh, hi=True) + t2

    qs = pl.BlockSpec((GLA_TILE, B_QK), lambda i: (i, 0))
    vs_ = pl.BlockSpec((GLA_TILE, B_V), lambda i: (i, 0))
    ss = pl.BlockSpec((CHUNKS_PER_TILE, B_QK, B_VAL_DIM), lambda i: (i, 0, 0))
    ins = [q, k, v, la, do, sp, gs, dec] + (list(add) if hasadd else [])
    in_specs = [qs, qs, vs_, qs, vs_, ss, ss, ss] + ([qs, qs, vs_] if hasadd else [])
    return _call(body, name=name, grid=(t // GLA_TILE,), in_specs=in_specs, out_specs=(qs, qs, vs_, qs),
                 out_shape=(_sds((t, B_QK)), _sds((t, B_QK)), _sds((t, B_V)), _sds((t, B_QK))),
                 compiler_params=_cparams())(*ins)


def gla_out_fwd(of, ob, g, gn, name):
    t = of.shape[0]
    tm = min(256, t)

    def body(of_ref, ob_ref, g_ref, gn_ref, o_ref):
        for h in range(B_HEADS):
            vs = slice(h * B_VAL_DIM, (h + 1) * B_VAL_DIM)
            o = of_ref[:, vs] + ob_ref[:, vs]
            on = o * lax.rsqrt(jnp.mean(o * o, axis=1, keepdims=True) + EPS)
            gv = g_ref[:, vs]
            o_ref[:, vs] = on * gn_ref[:, vs] * (gv * _sigmoid(gv))

    row = pl.BlockSpec((tm, B_V), lambda i: (i, 0))
    vec = pl.BlockSpec((1, B_V), lambda i: (0, 0))
    return _call(body, name=name, grid=(t // tm,), in_specs=[row, row, row, vec], out_specs=row,
                 out_shape=_sds((t, B_V)))(of, ob, g, gn.reshape(1, B_V))


def gla_out_bwd(of, ob, g, gn, dout, name):
    t = of.shape[0]
    tm = min(256, t)

    def body(of_ref, ob_ref, g_ref, gn_ref, d_ref, do_ref, dg_ref, dgn_ref):
        first = pl.program_id(0) == 0
        for h in range(B_HEADS):
            vs = slice(h * B_VAL_DIM, (h + 1) * B_VAL_DIM)
            o = of_ref[:, vs] + ob_ref[:, vs]
            r = lax.rsqrt(jnp.mean(o * o, axis=1, keepdims=True) + EPS)
            on = o * r
            gv = g_ref[:, vs]
            sg = _sigmoid(gv)
            silu = gv * sg
            dv = d_ref[:, vs]
            gnv = gn_ref[:, vs]
            dg_ref[:, vs] = dv * on * gnv * (sg * (1.0 + gv * (1.0 - sg)))
            don = dv * silu * gnv
            do_ref[:, vs] = r * (don - on * jnp.mean(don * on, axis=1, keepdims=True))
            part = jnp.sum(dv * silu * on, axis=0, keepdims=True)

            @pl.when(first)
            def _():
                dgn_ref[:, vs] = part

            @pl.when(jnp.logical_not(first))
            def _():
                dgn_ref[:, vs] += part

    row = pl.BlockSpec((tm, B_V), lambda i: (i, 0))
    vec = pl.BlockSpec((1, B_V), lambda i: (0, 0))
    return _call(body, name=name, grid=(t // tm,), in_specs=[row, row, row, vec, row], out_specs=(row, row, vec),
                 out_shape=(_sds((t, B_V)), _sds((t, B_V)), _sds((1, B_V))))(of, ob, g, gn.reshape(1, B_V), dout)


def _shift(x, k):
    if k > 0:
        return jnp.concatenate([x[k:], jnp.zeros((k,) + x.shape[1:], x.dtype)], axis=0)
    return jnp.concatenate([jnp.zeros((-k,) + x.shape[1:], x.dtype), x[:k]], axis=0)


def _lru_gates(xc, s, wa_ref, ba_ref, wx_ref, bx_ref, lam_ref):
    cols = [slice(g * C_BLOCK_DIM, (g + 1) * C_BLOCK_DIM) for g in range(C_BLOCKS)]
    zr = jnp.concatenate([_dot(xc[:, cs], wa_ref[s, g]) for g, cs in enumerate(cols)], axis=1) + ba_ref[s:s + 1, :]
    zi = jnp.concatenate([_dot(xc[:, cs], wx_ref[s, g]) for g, cs in enumerate(cols)], axis=1) + bx_ref[s:s + 1, :]
    r = _sigmoid(zr)
    i = _sigmoid(zi)
    sp = _softplus(-lam_ref[s:s + 1, :])
    log_a = -LRU_C * r * sp
    return r, i, sp, log_a


def lru_gates_fwd(x0, xm2, xm1, xp1, cw, cb, wa, ba, wx, bx, lam, name):
    t = x0.shape[0]
    tm = min(256, t)

    def body(x0_ref, xm2_ref, xm1_ref, xp1_ref, cw_ref, cb_ref, wa_ref, ba_ref, wx_ref, bx_ref, lam_ref,
             xc_ref, a0_ref, u0_ref, a1_ref, u1_ref):
        xc = (xm2_ref[...] * cw_ref[0:1, :] + xm1_ref[...] * cw_ref[1:2, :] + x0_ref[...] * cw_ref[2:3, :]
              + xp1_ref[...] * cw_ref[3:4, :] + cb_ref[...])
        xc_ref[...] = xc
        for s, (a_ref, u_ref) in enumerate(((a0_ref, u0_ref), (a1_ref, u1_ref))):
            _, i, _, log_a = _lru_gates(xc, s, wa_ref, ba_ref, wx_ref, bx_ref, lam_ref)
            a_ref[...] = jnp.exp(log_a)
            u_ref[...] = jnp.sqrt(-_expm1(2.0 * log_a)) * (i * xc)

    row = pl.BlockSpec((tm, C_WIDTH), lambda i: (i, 0))
    full = lambda shape: pl.BlockSpec(shape, lambda i: (0,) * len(shape))
    wshape = (2, C_BLOCKS, C_BLOCK_DIM, C_BLOCK_DIM)
    return _call(body, name=name, grid=(t // tm,),
                 in_specs=[row] * 4 + [full((4, C_WIDTH)), full((1, C_WIDTH)), full(wshape), full((2, C_WIDTH)),
                                       full(wshape), full((2, C_WIDTH)), full((2, C_WIDTH))],
                 out_specs=(row,) * 5, out_shape=(_sds((t, C_WIDTH)),) * 5)(
        x0, xm2, xm1, xp1, cw, cb.reshape(1, C_WIDTH), wa, ba, wx, bx, lam)


def lru_gates_bwd(xc, g0, hs0, g1, hs1, wa, ba, wx, bx, lam, name):
    t = xc.shape[0]
    tm = min(256, t)

    def body(xc_ref, g0_ref, hs0_ref, g1_ref, hs1_ref, wa_ref, ba_ref, wx_ref, bx_ref, lam_ref,
             dxc_ref, dzr0_ref, dzi0_ref, dzr1_ref, dzi1_ref, dlam_ref, dba_ref, dbx_ref):
        first = pl.program_id(0) == 0

        @pl.when(first)
        def _():
            dlam_ref[...] = jnp.zeros_like(dlam_ref)
            dba_ref[...] = jnp.zeros_like(dba_ref)
            dbx_ref[...] = jnp.zeros_like(dbx_ref)

        xcv = xc_ref[...]
        dxc = jnp.zeros_like(xcv)
        cols = [slice(g * C_BLOCK_DIM, (g + 1) * C_BLOCK_DIM) for g in range(C_BLOCKS)]
        for s, (g_ref, hs_ref, dzr_ref, dzi_ref) in enumerate(
                ((g0_ref, hs0_ref, dzr0_ref, dzi0_ref), (g1_ref, hs1_ref, dzr1_ref, dzi1_ref))):
            r, i, sp, log_a = _lru_gates(xcv, s, wa_ref, ba_ref, wx_ref, bx_ref, lam_ref)
            du = g_ref[...]
            da = du * hs_ref[...]
            a = jnp.exp(log_a)
            e2 = jnp.exp(2.0 * log_a)
            c = jnp.sqrt(-_expm1(2.0 * log_a))
            ix = i * xcv
            dlog = da * a - du * ix * (e2 / c)
            dix = du * c
            dxc = dxc + dix * i
            dzi = dix * xcv * i * (1.0 - i)
            dzr = dlog * (-LRU_C * sp) * r * (1.0 - r)
            dzr_ref[...] = dzr
            dzi_ref[...] = dzi
            dxc = dxc + jnp.concatenate(
                [_dot(dzr[:, cs], wa_ref[s, g], NT) + _dot(dzi[:, cs], wx_ref[s, g], NT) for g, cs in enumerate(cols)],
                axis=1)
            dsp = jnp.sum(dlog * (-LRU_C * r), axis=0, keepdims=True)
            dlam_ref[s:s + 1, :] += dsp * (-_sigmoid(-lam_ref[s:s + 1, :]))
            dba_ref[s:s + 1, :] += jnp.sum(dzr, axis=0, keepdims=True)
            dbx_ref[s:s + 1, :] += jnp.sum(dzi, axis=0, keepdims=True)
        dxc_ref[...] = dxc

    row = pl.BlockSpec((tm, C_WIDTH), lambda i: (i, 0))
    full = lambda shape: pl.BlockSpec(shape, lambda i: (0,) * len(shape))
    wshape = (2, C_BLOCKS, C_BLOCK_DIM, C_BLOCK_DIM)
    vec2 = full((2, C_WIDTH))
    return _call(body, name=name, grid=(t // tm,),
                 in_specs=[row] * 5 + [full(wshape), vec2, full(wshape), vec2, vec2],
                 out_specs=(row,) * 5 + (vec2,) * 3,
                 out_shape=(_sds((t, C_WIDTH)),) * 5 + (_sds((2, C_WIDTH)),) * 3)(
        xc, g0, hs0, g1, hs1, wa, ba, wx, bx, lam)


def lru_out_fwd(h0, h1, y, name):
    t = y.shape[0]
    tm = min(256, t)

    def body(h0_ref, h1_ref, y_ref, o_ref):
        o_ref[...] = (h0_ref[...] + h1_ref[...]) * _gelu(y_ref[...])

    row = pl.BlockSpec((tm, C_WIDTH), lambda i: (i, 0))
    return _call(body, name=name, grid=(t // tm,), in_specs=[row] * 3, out_specs=row,
                 out_shape=_sds((t, C_WIDTH)))(h0, h1, y)


def lru_out_bwd(h0, h1, y, dout, name):
    t = y.shape[0]
    tm = min(256, t)

    def body(h0_ref, h1_ref, y_ref, d_ref, dh_ref, dy_ref):
        yv = y_ref[...]
        dv = d_ref[...]
        dh_ref[...] = dv * _gelu(yv)
        dy_ref[...] = dv * (h0_ref[...] + h1_ref[...]) * _gelu_grad(yv)

    row = pl.BlockSpec((tm, C_WIDTH), lambda i: (i, 0))
    return _call(body, name=name, grid=(t // tm,), in_specs=[row] * 4, out_specs=(row, row),
                 out_shape=(_sds((t, C_WIDTH)),) * 2)(h0, h1, y, dout)


def conv_bwd(dxc, dp2, dp1, dm1, x0, xm2, xm1, xp1, cw, name):
    t = x0.shape[0]
    tm = min(256, t)

    def body(d_ref, dp2_ref, dp1_ref, dm1_ref, x0_ref, xm2_ref, xm1_ref, xp1_ref, cw_ref, dx_ref, dcw_ref, dcb_ref):
        @pl.when(pl.program_id(0) == 0)
        def _():
            dcw_ref[...] = jnp.zeros_like(dcw_ref)
            dcb_ref[...] = jnp.zeros_like(dcb_ref)

        dv = d_ref[...]
        dx_ref[...] = (dp2_ref[...] * cw_ref[0:1, :] + dp1_ref[...] * cw_ref[1:2, :] + dv * cw_ref[2:3, :]
                       + dm1_ref[...] * cw_ref[3:4, :])
        for j, x_ref in enumerate((xm2_ref, xm1_ref, x0_ref, xp1_ref)):
            dcw_ref[j:j + 1, :] += jnp.sum(dv * x_ref[...], axis=0, keepdims=True)
        dcb_ref[...] += jnp.sum(dv, axis=0, keepdims=True)

    row = pl.BlockSpec((tm, C_WIDTH), lambda i: (i, 0))
    cws = pl.BlockSpec((4, C_WIDTH), lambda i: (0, 0))
    cbs = pl.BlockSpec((1, C_WIDTH), lambda i: (0, 0))
    return _call(body, name=name, grid=(t // tm,), in_specs=[row] * 8 + [cws], out_specs=(row, cws, cbs),
                 out_shape=(_sds((t, C_WIDTH)), _sds((4, C_WIDTH)), _sds((1, C_WIDTH))))(
        dxc, dp2, dp1, dm1, x0, xm2, xm1, xp1, cw)


def _my_place():
    return lax.axis_index("x"), lax.axis_index("y"), lax.axis_index("c")


def all_gather(xs, name):
    r, c = xs.shape

    def body(x_ref, out_ref, send_sems, recv_sems, local_sem):
        x, y, cc = _my_place()
        me, sibling = (x, y, cc), (x, y, 1 - cc)
        chips = [(1 - x, y), (x, 1 - y), (1 - x, 1 - y)]

        def slot(px, py, pc):
            return out_ref.at[4 * px + 2 * py + pc]

        def copy(k, block, to, src=None):
            return pltpu.make_async_remote_copy(
                src_ref=slot(*block) if src is None else src, dst_ref=slot(*block),
                send_sem=send_sems.at[k], recv_sem=recv_sems.at[k], device_id=to, device_id_type=MESH)

        mine = pltpu.make_async_copy(x_ref, slot(*me), local_sem)
        mine.start()
        first = [copy(0, me, sibling, src=x_ref)]
        first += [copy(1 + j, me, (*chip, cc), src=x_ref) for j, chip in enumerate(chips)]
        for cp in first:
            cp.start()
        passed = [copy(4 + j, (*chip, cc), sibling) for j, chip in enumerate(chips)]
        for j, chip in enumerate(chips):
            copy(1 + j, (*chip, cc), me).wait_recv()
            passed[j].start()
        copy(0, sibling, me).wait_recv()
        for j, chip in enumerate(chips):
            copy(4 + j, (*chip, 1 - cc), me).wait_recv()
        for cp in first + passed:
            cp.wait_send()
        mine.wait()

    return _call(body, name=name, in_specs=[pl.BlockSpec(memory_space=pl.ANY)],
                 out_specs=pl.BlockSpec(memory_space=pl.ANY), out_shape=_sds((N_DEV, r, c), xs.dtype),
                 scratch_shapes=[pltpu.SemaphoreType.DMA((7,)), pltpu.SemaphoreType.DMA((7,)),
                                 pltpu.SemaphoreType.DMA])(xs)


def _stream_rows(r):
    nch = SIBLING_STREAMS // 4 if r % (8 * (SIBLING_STREAMS // 4)) == 0 else 1
    return nch, r // nch


def exchange_sibling(gw, name):
    _, r, c = gw.shape
    g5 = gw.reshape(4, 2, r, c)
    nch, rows = _stream_rows(r)

    def body(g_ref, out_ref, send_sems, recv_sems):
        x, y, cc = _my_place()
        swaps = []
        for q in range(4):
            for s in range(nch):
                k = q * nch + s
                win = pl.ds(s * rows, rows)
                swaps.append(pltpu.make_async_remote_copy(
                    src_ref=g_ref.at[q, 1 - cc, win], dst_ref=out_ref.at[q, win], send_sem=send_sems.at[k],
                    recv_sem=recv_sems.at[k], device_id=(x, y, 1 - cc), device_id_type=MESH))
        for cp in swaps:
            cp.start()
        for cp in swaps:
            cp.wait()

    nsem = 4 * nch
    return _call(body, name=name, in_specs=[pl.BlockSpec(memory_space=pl.ANY)],
                 out_specs=pl.BlockSpec(memory_space=pl.ANY), out_shape=_sds((4, r, c), gw.dtype),
                 scratch_shapes=[pltpu.SemaphoreType.DMA((nsem,)), pltpu.SemaphoreType.DMA((nsem,))])(g5)


HBM_SPEC = pl.BlockSpec(memory_space=pltpu.HBM)
SEM_SPEC = pl.BlockSpec(memory_space=pltpu.SEMAPHORE)
DATAFLOW = pltpu.SideEffectType.DATAFLOW_SIDE_EFFECTING


def _hbm(a):
    return pltpu.with_memory_space_constraint(a, pltpu.HBM)


def _peers(x, y, cc):
    return [(x, y, 1 - cc), (1 - x, y, cc), (x, 1 - y, cc), (1 - x, 1 - y, cc)]


def _slot(p):
    return 4 * p[0] + 2 * p[1] + p[2]


def gather_start(blk, name):
    r, c = blk.shape

    def body(v_ref, land_ref, send_sems, recv_sems, v_thru, land_thru, token):
        x, y, cc = _my_place()
        for k, to in enumerate(_peers(x, y, cc)):
            pltpu.make_async_remote_copy(
                src_ref=v_ref, dst_ref=land_ref.at[_slot((x, y, cc))], send_sem=send_sems.at[k],
                recv_sem=recv_sems.at[k], device_id=to, device_id_type=MESH).start()
        token[...] = jnp.zeros_like(token)

    return _call(
        body, name=name,
        out_shape=(pltpu.SemaphoreType.DMA((4,)), pltpu.SemaphoreType.DMA((4,)), pltpu.HBM((r, c), blk.dtype),
                   pltpu.HBM((N_DEV, r, c), blk.dtype), _sds((8, 128))),
        in_specs=(HBM_SPEC, HBM_SPEC),
        out_specs=(SEM_SPEC, SEM_SPEC, HBM_SPEC, HBM_SPEC, pl.BlockSpec(memory_space=pltpu.VMEM)),
        input_output_aliases={0: 2, 1: 3},
        compiler_params=pltpu.CompilerParams(has_side_effects=DATAFLOW),
    )(_hbm(blk), _hbm(lax.empty((N_DEV, r, c), blk.dtype)))


def gather_wait(send_sems, recv_sems, v_thru, land_thru, after, name):
    def body(v_ref, land_ref, send_sems, recv_sems, after_ref, v_out, land_out):
        x, y, cc = _my_place()
        for k, peer in enumerate(_peers(x, y, cc)):
            cp = pltpu.make_async_remote_copy(
                src_ref=v_ref, dst_ref=land_ref.at[_slot(peer)], send_sem=send_sems.at[k], recv_sem=recv_sems.at[k],
                device_id=peer, device_id_type=MESH)
            cp.wait_send()
            cp.wait_recv()

    return _call(
        body, name=name,
        out_shape=(pltpu.HBM(v_thru.shape, v_thru.dtype), pltpu.HBM(land_thru.shape, land_thru.dtype)),
        in_specs=(HBM_SPEC, HBM_SPEC, SEM_SPEC, SEM_SPEC, pl.BlockSpec(memory_space=pl.ANY)),
        out_specs=(HBM_SPEC, HBM_SPEC), input_output_aliases={0: 0, 1: 1},
        compiler_params=pltpu.CompilerParams(has_side_effects=DATAFLOW),
    )(v_thru, land_thru, send_sems, recv_sems, after)


def gather_pass(land, name):
    _, r, c = land.shape
    nch, rows = _stream_rows(r)

    def body(land_ref, out_ref, send_sems, recv_sems):
        x, y, cc = _my_place()
        peers = _peers(x, y, cc)
        copies = []
        for j in range(3):
            mine, theirs = _slot(peers[1 + j]), _slot((peers[1 + j][0], peers[1 + j][1], 1 - cc))
            for s in range(nch):
                k = j * nch + s
                win = pl.ds(s * rows, rows)
                send = pltpu.make_async_remote_copy(
                    src_ref=land_ref.at[mine, win], dst_ref=out_ref.at[mine, win], send_sem=send_sems.at[k],
                    recv_sem=recv_sems.at[k], device_id=peers[0], device_id_type=MESH)
                recv = pltpu.make_async_remote_copy(
                    src_ref=land_ref.at[mine, win], dst_ref=out_ref.at[theirs, win], send_sem=send_sems.at[k],
                    recv_sem=recv_sems.at[k], device_id=peers[0], device_id_type=MESH)
                copies.append((send, recv))
        for send, _ in copies:
            send.start()
        for send, recv in copies:
            send.wait_send()
            recv.wait_recv()

    nsem = 3 * nch
    return _call(body, name=name, in_specs=[pl.BlockSpec(memory_space=pl.ANY)],
                 out_specs=pl.BlockSpec(memory_space=pl.ANY), out_shape=_sds(land.shape, land.dtype),
                 input_output_aliases={0: 0},
                 scratch_shapes=[pltpu.SemaphoreType.DMA((nsem,)), pltpu.SemaphoreType.DMA((nsem,))])(land)


def chips_start(p, name):
    _, r, c = p.shape

    def body(p_ref, land_ref, send_sems, recv_sems, p_thru, land_thru, token):
        x, y, cc = _my_place()
        for j, (px, py, pc) in enumerate(_peers(x, y, cc)[1:]):
            pltpu.make_async_remote_copy(
                src_ref=p_ref.at[2 * px + py], dst_ref=land_ref.at[j], send_sem=send_sems.at[j],
                recv_sem=recv_sems.at[j], device_id=(px, py, pc), device_id_type=MESH).start()
        token[...] = jnp.zeros_like(token)

    return _call(
        body, name=name,
        out_shape=(pltpu.SemaphoreType.DMA((3,)), pltpu.SemaphoreType.DMA((3,)), pltpu.HBM(p.shape, p.dtype),
                   pltpu.HBM((3, r, c), p.dtype), _sds((8, 128))),
        in_specs=(HBM_SPEC, HBM_SPEC),
        out_specs=(SEM_SPEC, SEM_SPEC, HBM_SPEC, HBM_SPEC, pl.BlockSpec(memory_space=pltpu.VMEM)),
        input_output_aliases={0: 2, 1: 3},
        compiler_params=pltpu.CompilerParams(has_side_effects=DATAFLOW),
    )(_hbm(p), _hbm(lax.empty((3, r, c), p.dtype)))


def chips_wait(send_sems, recv_sems, p_thru, land_thru, after, name):
    def body(p_ref, land_ref, send_sems, recv_sems, after_ref, p_out, land_out):
        x, y, cc = _my_place()
        for j, (px, py, pc) in enumerate(_peers(x, y, cc)[1:]):
            cp = pltpu.make_async_remote_copy(
                src_ref=p_ref.at[2 * px + py], dst_ref=land_ref.at[j], send_sem=send_sems.at[j],
                recv_sem=recv_sems.at[j], device_id=(px, py, pc), device_id_type=MESH)
            cp.wait_send()
            cp.wait_recv()

    return _call(
        body, name=name,
        out_shape=(pltpu.HBM(p_thru.shape, p_thru.dtype), pltpu.HBM(land_thru.shape, land_thru.dtype)),
        in_specs=(HBM_SPEC, HBM_SPEC, SEM_SPEC, SEM_SPEC, pl.BlockSpec(memory_space=pl.ANY)),
        out_specs=(HBM_SPEC, HBM_SPEC), input_output_aliases={0: 0, 1: 1},
        compiler_params=pltpu.CompilerParams(has_side_effects=DATAFLOW),
    )(p_thru, land_thru, send_sems, recv_sems, after)


def reduce_scatter_begin(gw, name, tag):
    _, r, c = gw.shape
    theirs = exchange_sibling(gw, name + "_sibling")
    mine = lax.dynamic_index_in_dim(gw.reshape(4, 2, r, c), lax.axis_index("c"), axis=1, keepdims=False)
    chip_sum = add_n([mine, theirs], BF16, name + "_add2")
    return chips_start(chip_sum, name + "_start" + tag)


def reduce_scatter_end(started, after, name, tag):
    send_sems, recv_sems, p_thru, land_thru, _ = started
    parts, land = chips_wait(send_sems, recv_sems, p_thru, land_thru, after, name + "_wait" + tag)
    mine = lax.dynamic_index_in_dim(parts, 2 * lax.axis_index("x") + lax.axis_index("y"), axis=0, keepdims=False)
    return add_own_lead(mine, land, name + "_add4")


def _pack(arrs):
    flat = jnp.concatenate([a.reshape(-1).astype(F32) for a in arrs])
    n = flat.shape[0]
    pad = (-n) % PACK_ELEMS
    return jnp.pad(flat, (0, pad)).reshape(-1, 128)


def _unpack(packed, shapes):
    flat = packed.reshape(-1)
    out, off = [], 0
    for s in shapes:
        n = int(np.prod(s))
        out.append(flat[off:off + n].reshape(s))
        off += n
    return out


def _t5_bucket(rel):
    nb = N_BUCKETS // 2
    max_exact = nb // 2
    ret = jnp.where(rel > 0, nb, 0)
    n = jnp.abs(rel)
    nf = jnp.maximum(n, 1).astype(jnp.float32)
    large = max_exact + (jnp.log(nf / max_exact) / math.log(MAX_DISTANCE / max_exact)
                         * (nb - max_exact)).astype(jnp.int32)
    large = jnp.minimum(large, nb - 1)
    return ret + jnp.where(n < max_exact, n, large)


SMALL_SHARDED = ("gla_w2_f", "gla_w2_b", "conv_w", "lru_ba", "lru_bx", "lru_lambda")
SMALL_REPL = ("rel_bias", "attn_sink", "gla_b2_f", "gla_b2_b", "gla_norm", "conv_b", "lru_wa", "lru_wx",
              "norm_mix_pre", "norm_mix_post", "norm_mem", "norm_x_pre", "norm_x_post", "norm_ff_pre", "norm_ff_post")
BIG = ("w_in", "w_out", "xq", "xk", "xv", "xo", "w_up", "w_down")
WEIGHTS = ['rel_bias', 'w_in', 'w_out', 'attn_sink', 'gla_w2_f', 'gla_b2_f', 'gla_w2_b', 'gla_b2_b', 'gla_norm',
           'conv_w', 'conv_b', 'lru_wa', 'lru_ba', 'lru_wx', 'lru_bx', 'lru_lambda', 'xq', 'xk', 'xv', 'xo', 'w_up',
           'w_down', 'norm_mix_pre', 'norm_mix_post', 'norm_mem', 'norm_x_pre', 'norm_x_post', 'norm_ff_pre',
           'norm_ff_post']


def _step(x, mem, loss_target, w, m, v):
    depth = w["w_in"].shape[0]
    t, d = x.shape[1], x.shape[2]
    ml = mem.shape[1]
    rx = d // N_DEV
    rf = w["w_up"].shape[2]
    r_out = D_MIX // N_DEV
    x = x.reshape(t, d)
    mem = mem.reshape(ml, d)
    loss_target = loss_target.reshape(t, d)
    my_idx = 4 * lax.axis_index("x") + 2 * lax.axis_index("y") + lax.axis_index("c")

    off_in, off_out = 0, W_IN_ROWS
    off_xq = off_out + r_out
    off_xk, off_xv, off_xo = off_xq + rx, off_xq + 2 * rx, off_xq + 3 * rx
    off_up = off_xq + 4 * rx
    off_down = off_up + rf
    r_tot = off_down + rf

    sh_shapes = [w[n].shape for n in SMALL_SHARDED]
    gathered = all_gather(_pack([w[n] for n in SMALL_SHARDED]), "ag_small")
    per_dev = [_unpack(gathered[j], sh_shapes) for j in range(N_DEV)]
    full = {n: jnp.concatenate([per_dev[j][i] for j in range(N_DEV)], axis=-1) for i, n in enumerate(SMALL_SHARDED)}
    for n in SMALL_REPL:
        full[n] = w[n]

    ag_started = []
    for l in range(depth):
        w_in_t = jnp.pad(w["w_in"][l].T, ((0, W_IN_ROWS - W_IN_SHARD), (0, 0)))
        blk = jnp.concatenate([w_in_t, w["w_out"][l], w["xq"][l], w["xk"][l], w["xv"][l], w["xo"][l],
                               w["w_up"][l].T, w["w_down"][l]], axis=0).astype(BF16)
        ag_started.append(gather_start(blk, "ag_start%d" % l))
    x = x + sum(st[4][0, 0] for st in ag_started)
    gws = [None] * depth

    qi = jnp.arange(BLOCK)[:, None]
    kj = jnp.arange(3 * BLOCK)[None, :]
    onehot_t = (jnp.arange(N_BUCKETS)[:, None] == _t5_bucket(kj - BLOCK - qi).reshape(1, -1)).astype(F32)
    bias = mm_plain(full["rel_bias"].T, onehot_t, "rel_bias_lookup", hi=True, tn=3 * BLOCK * 16)
    bias = bias.reshape(A_HEADS, BLOCK, 3 * BLOCK)
    bias_t = jnp.transpose(bias, (0, 2, 1))

    def sink_rows(sink):
        s = jnp.broadcast_to(sink.reshape(A_KV_HEADS, A_GROUP, 1), (A_KV_HEADS, A_GROUP, 128))
        return jnp.pad(s, ((0, 0), (0, 8 - A_GROUP), (0, 0)))

    def split_proj(p):
        outs, off = [], 0
        for s in SPLIT_SIZES:
            outs.append(p[:, off:off + s])
            off += s
        return outs

    def lead(a):
        return a.reshape(a.shape[0], C_WIDTH // 128, 128)

    saved = []
    h = rms_fwd(x, full["norm_mix_pre"][0], "rms_first")
    for l in range(depth):
        send_sems, recv_sems, blk_thru, land_thru, _ = ag_started[l]
        blk_done, land = gather_wait(send_sems, recv_sems, blk_thru, land_thru, x, "ag_wait%d" % l)
        land = gather_pass(land, "ag_pass")
        gw = gws[l] = lax.dynamic_update_index_in_dim(land, blk_done, my_idx, 0)
        sv = {"x": x, "h_in": h}
        proj_pad = mm_wn(h, gw, off_in, W_IN_ROWS, "mm_w_in")
        proj = proj_pad.reshape(t, N_DEV, W_IN_ROWS)[:, :, :W_IN_SHARD].reshape(t, D_IN)
        aq, ak, av, bq, bk, bv, bg, zf, zb, cx, cy = split_proj(proj)
        sv.update(aq=aq, ak=ak, av=av, bq=bq, bk=bk, bv=bv, bg=bg, zf=zf, zb=zb, cx=cx, cy=cy)
        sink_b = sink_rows(full["attn_sink"][l])
        oa = attn_fwd(aq, ak, av, bias, sink_b, "attn_fwd")
        la_f, la_b = gla_gates_fwd(zf, zb, full["gla_w2_f"][l], full["gla_b2_f"][l], full["gla_w2_b"][l],
                                   full["gla_b2_b"][l], "gla_gates_fwd")
        bk_t = bk.T
        gla = {}
        for nm, la, bdir in (("f", la_f, False), ("b", la_b, True)):
            la_t = la.T
            u, dec = gla_outer(bk_t, la_t, bv, "gla_outer_k_" + nm, bdir, "khat")
            sp = scan_lead(dec, u, "gla_state_scan_" + nm, reverse=bdir, inclusive=False)
            o_dir = gla_inner_fwd(bq, bk, bv, la, sp, "gla_inner_fwd_" + nm, bdir)
            gla[nm] = dict(la=la, la_t=la_t, dec=dec, sp=sp, o=o_dir)
        ob = gla_out_fwd(gla["f"]["o"], gla["b"]["o"], bg, full["gla_norm"][l], "gla_out_fwd")
        sv["gla"] = gla
        xm2, xm1, xp1 = _shift(cx, -2), _shift(cx, -1), _shift(cx, 1)
        xc, a0, u0, a1, u1 = lru_gates_fwd(cx, xm2, xm1, xp1, full["conv_w"][l], full["conv_b"][l], full["lru_wa"][l],
                                           full["lru_ba"][l], full["lru_wx"][l], full["lru_bx"][l],
                                           full["lru_lambda"][l], "lru_gates_fwd")
        h0 = scan_lead(lead(a0), lead(u0), "lru_scan_fwd", reverse=False, inclusive=True).reshape(t, C_WIDTH)
        h1 = scan_lead(lead(a1), lead(u1), "lru_scan_rev", reverse=True, inclusive=True).reshape(t, C_WIDTH)
        oc = lru_out_fwd(h0, h1, cy, "lru_out_fwd")
        sv.update(xm2=xm2, xm1=xm1, xp1=xp1, xc=xc, a0=a0, a1=a1, h0=h0, h1=h1, oa=oa)
        cat = jnp.concatenate([oa, ob, oc], axis=1)
        mixed = mm_wk(cat, gw, off_out, r_out, "mm_w_out")
        x1, h2 = resid_rms(x, mixed, full["norm_mix_post"][l], full["norm_x_pre"][l], "resid_rms")
        sv.update(cat=cat, mixed=mixed, x1=x1, h2=h2)
        memn = rms_fwd(mem, full["norm_mem"][l], "rms_mem")
        q = mm_wk(h2, gw, off_xq, rx, "mm_xq")
        k = mm_wk(memn, gw, off_xk, rx, "mm_xkv")
        vv = mm_wk(memn, gw, off_xv, rx, "mm_xkv")
        ox = xattn_fwd(q, k, vv, "xattn_fwd")
        ca = mm_wk(ox, gw, off_xo, rx, "mm_xo")
        x2, h3 = resid_rms(x1, ca, full["norm_x_post"][l], full["norm_ff_pre"][l], "resid_rms")
        sv.update(memn=memn, q=q, k=k, v=vv, ox=ox, ca=ca, x2=x2, h3=h3)
        up = mm_wn(h3, gw, off_up, rf, "mm_w_up")
        ff = mm_wk(up, gw, off_down, rf, "mm_w_down", jb=max(1, min(N_DEV, 2048 // rf)), relu2=True)
        if l + 1 < depth:
            x, h = resid_rms(x2, ff, full["norm_ff_post"][l], full["norm_mix_pre"][l + 1], "resid_rms")
        else:
            x = resid_rms(x2, ff, full["norm_ff_post"][l], None, "resid_rms_last")
        sv.update(up=up, ff=ff)
        saved.append(sv)

    dx, loss_local = loss_and_grad(x, loss_target, "loss")
    loss = lax.psum(loss_local, AXES)

    grads = {n: [None] * depth for n in WEIGHTS if n != "rel_bias"}
    dbias_total = None
    big_grads = [None] * depth
    rs_started = [None] * depth
    bf = lambda a: a.astype(BF16)
    for l in reversed(range(depth)):
        gw = gws[l]
        sv = saved[l]
        dff, grads["norm_ff_post"][l] = rms_bwd(sv["ff"], full["norm_ff_post"][l], dx, "rms_bwd")
        dup = mm_wn(dff, gw, off_down, rf, "mm_w_down_dx", relu_grad_of=sv["up"])
        up_t = sv["up"].T
        act_t = bf(jnp.square(jnp.maximum(up_t, 0.0)))
        g_down = mm_plain(act_t, dff, "mm_dw_f", out_dtype=BF16)
        g_up_t = mm_plain(bf(dup.T), sv["h3"], "mm_dw_f", out_dtype=BF16)
        dh3 = mm_wk(dup, gw, off_up, rf, "mm_w_up_dx", jb=max(1, min(N_DEV, 2048 // rf)))
        dx2, grads["norm_ff_pre"][l] = rms_bwd(sv["x2"], full["norm_ff_pre"][l], dh3, "rms_bwd_add", add=dx)
        dca, grads["norm_x_post"][l] = rms_bwd(sv["ca"], full["norm_x_post"][l], dx2, "rms_bwd")
        dox = mm_wn(dca, gw, off_xo, rx, "mm_x_dx")
        g_xo = mm_plain(bf(sv["ox"].T), dca, "mm_dw_d", out_dtype=BF16)
        dq, dk, dv = xattn_bwd(sv["q"], sv["k"], sv["v"], sv["ox"], dox, "xattn_bwd")
        g_xq = mm_plain(bf(sv["h2"].T), dq, "mm_dw_d", out_dtype=BF16)
        memn_t = bf(sv["memn"].T)
        g_xk = mm_plain(memn_t, dk, "mm_dw_mem", out_dtype=BF16)
        g_xv = mm_plain(memn_t, dv, "mm_dw_mem", out_dtype=BF16)
        dh2 = mm_wn(dq, gw, off_xq, rx, "mm_x_dx")
        dmem_k = mm_wn(dk, gw, off_xk, rx, "mm_x_dx_mem")
        dmem_v = mm_wn(dv, gw, off_xv, rx, "mm_x_dx_mem")
        _, grads["norm_mem"][l] = rms_bwd(mem, full["norm_mem"][l], dmem_k, "rms_bwd_mem", dy2=dmem_v)
        dx1, grads["norm_x_pre"][l] = rms_bwd(sv["x1"], full["norm_x_pre"][l], dh2, "rms_bwd_add", add=dx2)
        dmixed, grads["norm_mix_post"][l] = rms_bwd(sv["mixed"], full["norm_mix_post"][l], dx1, "rms_bwd")
        dcat = mm_wn(dmixed, gw, off_out, r_out, "mm_w_out_dx")
        g_out = mm_plain(bf(sv["cat"].T), dmixed, "mm_dw_d", out_dtype=BF16)
        doa, dob, doc = dcat[:, :A_Q], dcat[:, A_Q:A_Q + B_V], dcat[:, A_Q + B_V:]
        daq, dak, dav, dbias, dsink = attn_bwd(sv["aq"], sv["ak"], sv["av"], bias, bias_t,
                                               sink_rows(full["attn_sink"][l]), doa, sv["oa"], "attn_bwd")
        grads["attn_sink"][l] = dsink[:, :A_GROUP, 0].reshape(A_HEADS)
        dbias_total = dbias if dbias_total is None else add_n([dbias_total, dbias], F32, "add_dbias")
        gf, gb = sv["gla"]["f"], sv["gla"]["b"]
        do_gla, dbg, dgn = gla_out_bwd(gf["o"], gb["o"], sv["bg"], full["gla_norm"][l], dob, "gla_out_bwd")
        grads["gla_norm"][l] = dgn.reshape(B_V)
        bq_t = sv["bq"].T
        acc = None
        dlas = {}
        for nm, gd, bdir in (("f", gf, False), ("b", gb, True)):
            wq = gla_outer(bq_t, gd["la_t"], do_gla, "gla_outer_q_" + nm, bdir, "qtil")
            gs = scan_lead(gd["dec"], wq, "gla_adj_scan_" + nm, reverse=not bdir, inclusive=False)
            dbq, dbk, dbv, dlas[nm] = gla_inner_bwd(sv["bq"], sv["bk"], sv["bv"], gd["la"], do_gla, gd["sp"], gs,
                                                    gd["dec"], "gla_inner_bwd_" + nm, bdir, add=acc)
            acc = (dbq, dbk, dbv)
        dzf, dzb, dpre_f, dpre_b, db2f, db2b = gla_gates_bwd(
            sv["zf"], sv["zb"], full["gla_w2_f"][l], full["gla_b2_f"][l], full["gla_w2_b"][l], full["gla_b2_b"][l],
            dlas["f"], dlas["b"], "gla_gates_bwd")
        grads["gla_b2_f"][l] = db2f.reshape(B_QK)
        grads["gla_b2_b"][l] = db2b.reshape(B_QK)
        grads["gla_w2_f"][l] = mm_plain(sv["zf"].T, dpre_f, "mm_dw_gate", hi=True)
        grads["gla_w2_b"][l] = mm_plain(sv["zb"].T, dpre_b, "mm_dw_gate", hi=True)
        dh, dcy = lru_out_bwd(sv["h0"], sv["h1"], sv["cy"], doc, "lru_out_bwd")
        g0 = scan_lead(lead(_shift(sv["a0"], 1)), lead(dh), "lru_scan_rev", reverse=True,
                       inclusive=True).reshape(t, C_WIDTH)
        g1 = scan_lead(lead(_shift(sv["a1"], -1)), lead(dh), "lru_scan_fwd", reverse=False,
                       inclusive=True).reshape(t, C_WIDTH)
        dxc, dzr0, dzi0, dzr1, dzi1, dlam, dba, dbx = lru_gates_bwd(
            sv["xc"], g0, _shift(sv["h0"], -1), g1, _shift(sv["h1"], 1), full["lru_wa"][l], full["lru_ba"][l],
            full["lru_wx"][l], full["lru_bx"][l], full["lru_lambda"][l], "lru_gates_bwd")
        xc_t = bf(sv["xc"].T)
        grads["lru_wa"][l] = jnp.stack([blockdiag_dw(xc_t, dzr0, "lru_dw"), blockdiag_dw(xc_t, dzr1, "lru_dw")])
        grads["lru_wx"][l] = jnp.stack([blockdiag_dw(xc_t, dzi0, "lru_dw"), blockdiag_dw(xc_t, dzi1, "lru_dw")])
        grads["lru_lambda"][l], grads["lru_ba"][l], grads["lru_bx"][l] = dlam, dba, dbx
        dcx, dcw, dcb = conv_bwd(dxc, _shift(dxc, 2), _shift(dxc, 1), _shift(dxc, -1), sv["cx"], sv["xm2"],
                                 sv["xm1"], sv["xp1"], full["conv_w"][l], "conv_bwd")
        grads["conv_w"][l] = dcw
        grads["conv_b"][l] = dcb.reshape(C_WIDTH)
        dproj = jnp.concatenate([daq, dak, dav, dbq, dbk, dbv, dbg, dzf, dzb, dcx, dcy], axis=1)
        dproj_pad = jnp.pad(dproj.reshape(t, N_DEV, W_IN_SHARD),
                            ((0, 0), (0, 0), (0, W_IN_ROWS - W_IN_SHARD))).reshape(t, N_DEV * W_IN_ROWS)
        g_in_t = mm_plain(bf(dproj_pad.T), sv["h_in"], "mm_dw_in", out_dtype=BF16)
        dh1 = mm_wk(dproj_pad, gw, off_in, W_IN_ROWS, "mm_w_in_dx", jb=2)
        dx, grads["norm_mix_pre"][l] = rms_bwd(sv["x"], full["norm_mix_pre"][l], dh1, "rms_bwd_add", add=dx1)
        parts = [g_in_t, g_out, g_xq, g_xk, g_xv, g_xo, g_up_t, g_down]
        gpack = jnp.concatenate([p.reshape(N_DEV, p.shape[0] // N_DEV, d) for p in parts], axis=1)
        rs_started[l] = reduce_scatter_begin(gpack, "rs_weights", str(l))
        dx = dx + rs_started[l][4][0, 0]
    for l in range(depth):
        big_grads[l] = reduce_scatter_end(rs_started[l], dx, "rs_weights", str(l))

    grad_rel = mm_plain(dbias_total.reshape(A_HEADS, -1), onehot_t, "rel_bias_grad", tb=True, hi=True).T

    small_names = [n for n in WEIGHTS if n not in BIG]
    small_g = {"rel_bias": grad_rel}
    for n in small_names:
        if n != "rel_bias":
            small_g[n] = jnp.stack([g.reshape(full[n].shape[1:]) for g in grads[n]])
    shapes = [small_g[n].shape for n in small_names]
    packed = all_gather(_pack([small_g[n] for n in small_names]), "ag_small_grads")
    summed = sum_lead(packed, tuple(range(N_DEV)), F32, "add8_small")
    small_g = dict(zip(small_names, _unpack(summed, shapes)))
    for n in SMALL_SHARDED:
        wdt = w[n].shape[-1]
        small_g[n] = lax.dynamic_slice_in_dim(small_g[n], my_idx * wdt, wdt, axis=small_g[n].ndim - 1)

    grad_out, delta, new_m, new_v = {}, {}, {}, {}
    sshapes = [w[n].shape for n in small_names]
    ds, ms, vs = adamw(_pack([small_g[n] for n in small_names]), _pack([w[n] for n in small_names]),
                       _pack([m[n] for n in small_names]), _pack([v[n] for n in small_names]), "adamw_small")
    for n, g_, d_, m_, v_ in zip(small_names, [small_g[n] for n in small_names], _unpack(ds, sshapes),
                                 _unpack(ms, sshapes), _unpack(vs, sshapes)):
        grad_out[n], delta[n], new_m[n], new_v[n] = g_, d_, m_, v_

    def rows(l, off, r):
        return big_grads[l][off:off + r]

    big_g = {
        "w_in": jnp.stack([rows(l, off_in, W_IN_SHARD).T for l in range(depth)]),
        "w_out": jnp.stack([rows(l, off_out, r_out) for l in range(depth)]),
        "xq": jnp.stack([rows(l, off_xq, rx) for l in range(depth)]),
        "xk": jnp.stack([rows(l, off_xk, rx) for l in range(depth)]),
        "xv": jnp.stack([rows(l, off_xv, rx) for l in range(depth)]),
        "xo": jnp.stack([rows(l, off_xo, rx) for l in range(depth)]),
        "w_up": jnp.stack([rows(l, off_up, rf).T for l in range(depth)]),
        "w_down": jnp.stack([rows(l, off_down, rf) for l in range(depth)]),
    }
    for n in BIG:
        grad_out[n] = big_g[n]
        delta[n], new_m[n], new_v[n] = adamw(big_g[n], w[n], m[n], v[n], "adamw_" + n)

    return (loss, dx.reshape(1, t, d), *[grad_out[n] for n in WEIGHTS], *[delta[n] for n in WEIGHTS],
            *[new_m[n] for n in WEIGHTS], *[new_v[n] for n in WEIGHTS])


def kernel(x, mem, rel_bias, w_in, w_out, attn_sink, gla_w2_f, gla_b2_f, gla_w2_b, gla_b2_b, gla_norm, conv_w, conv_b, lru_wa, lru_ba, lru_wx, lru_bx, lru_lambda, xq, xk, xv, xo, w_up, w_down, norm_mix_pre, norm_mix_post, norm_mem, norm_x_pre, norm_x_post, norm_ff_pre, norm_ff_post, loss_target, m_rel_bias, m_w_in, m_w_out, m_attn_sink, m_gla_w2_f, m_gla_b2_f, m_gla_w2_b, m_gla_b2_b, m_gla_norm, m_conv_w, m_conv_b, m_lru_wa, m_lru_ba, m_lru_wx, m_lru_bx, m_lru_lambda, m_xq, m_xk, m_xv, m_xo, m_w_up, m_w_down, m_norm_mix_pre, m_norm_mix_post, m_norm_mem, m_norm_x_pre, m_norm_x_post, m_norm_ff_pre, m_norm_ff_post, v_rel_bias, v_w_in, v_w_out, v_attn_sink, v_gla_w2_f, v_gla_b2_f, v_gla_w2_b, v_gla_b2_b, v_gla_norm, v_conv_w, v_conv_b, v_lru_wa, v_lru_ba, v_lru_wx, v_lru_bx, v_lru_lambda, v_xq, v_xk, v_xv, v_xo, v_w_up, v_w_down, v_norm_mix_pre, v_norm_mix_post, v_norm_mem, v_norm_x_pre, v_norm_x_post, v_norm_ff_pre, v_norm_ff_post):
    given = dict(locals())
    w = {n: given[n] for n in WEIGHTS}
    m = {n: given["m_" + n] for n in WEIGHTS}
    v = {n: given["v_" + n] for n in WEIGHTS}
    return _step(x, mem, loss_target, w, m, v)
```

```python
import math

import jax
import jax.numpy as jnp
import numpy as np
from jax import lax
from jax.experimental import pallas as pl
from jax.experimental.pallas import tpu as pltpu

F32 = jnp.float32
BF16 = jnp.bfloat16
HI = lax.Precision.HIGHEST
NN = (((1,), (0,)), ((), ()))
NT = (((1,), (1,)), ((), ()))
MESH = pl.DeviceIdType.MESH
AXES = ("x", "y", "c")
N_DEV = 8

A_HEAD_DIM = 128
A_HEADS = 8
A_KV_HEADS = 2
A_GROUP = 4
WINDOW = 128
BLOCK = 128
N_BUCKETS = 32
MAX_DISTANCE = 128
B_HEADS = 4
B_KEY_DIM = 64
B_VAL_DIM = 128
GATE_RANK = 16
GATE_TAU = 16.0
GLA_CHUNK = 16
C_WIDTH = 512
C_BLOCKS = 4
C_BLOCK_DIM = 128
LRU_C = 8.0
X_HEADS = 4
EPS = 1e-6
NEG_INF = -1e30
A_Q = A_HEADS * A_HEAD_DIM
A_KV = A_KV_HEADS * A_HEAD_DIM
B_QK = B_HEADS * B_KEY_DIM
B_V = B_HEADS * B_VAL_DIM
SPLIT_SIZES = (A_Q, A_KV, A_KV, B_QK, B_QK, B_V, B_V, GATE_RANK, GATE_RANK, C_WIDTH, C_WIDTH)
D_IN = sum(SPLIT_SIZES)
D_MIX = A_Q + B_V + C_WIDTH
W_IN_SHARD = D_IN // N_DEV
W_IN_ROWS = 768
GLA_TILE = 128
CHUNKS_PER_TILE = GLA_TILE // GLA_CHUNK
EXP_CLAMP = 80.0

ADAM_LR = 0.001
ADAM_B1 = 0.9
ADAM_B2 = 0.999
ADAM_EPS = 1e-08
ADAM_WD = 0.01
ADAM_STEP = 10

VMEM_LIMIT_BYTES = 52 * 1024 * 1024
MM_TILE = 1024
SIBLING_STREAMS = 16
PACK_ELEMS = 128 * 2048


def _call(body, **kw):
    return pl.pallas_call(body, **kw)


def _cparams():
    return pltpu.CompilerParams(vmem_limit_bytes=VMEM_LIMIT_BYTES)


def _dot(a, b, dims=NN, hi=False):
    if hi:
        return lax.dot_general(a, b, dims, precision=HI, preferred_element_type=F32)
    return lax.dot_general(a.astype(BF16), b.astype(BF16), dims, preferred_element_type=F32)


def _sds(shape, dtype=F32):
    return jax.ShapeDtypeStruct(tuple(shape), dtype)


def _row_tile(rows, cols, target_elems=1 << 18):
    want = max(8, target_elems // max(cols, 1))
    if rows <= want:
        return rows
    t = (want // 8) * 8
    while t >= 8:
        if rows % t == 0:
            return t
        t -= 8
    return rows


def _expm1(x):
    poly = x * (1.0 + x * (1.0 / 2 + x * (1.0 / 6 + x * (1.0 / 24 + x * (1.0 / 120 + x * (
        1.0 / 720 + x * (1.0 / 5040 + x * (1.0 / 40320))))))))
    return jnp.where(jnp.abs(x) < 0.3, poly, jnp.exp(x) - 1.0)


def _log1p(e):
    w = 1.0 + e
    return jnp.where(w == 1.0, e, jnp.log(w) * e / (w - 1.0))


def _softplus(x):
    return jnp.maximum(x, 0.0) + _log1p(jnp.exp(-jnp.abs(x)))


def _sigmoid(x):
    return jax.nn.sigmoid(x)


GELU_K = math.sqrt(2.0 / math.pi)


def _gelu(y):
    t = jnp.tanh(GELU_K * (y + 0.044715 * y * y * y))
    return 0.5 * y * (1.0 + t)


def _gelu_grad(y):
    t = jnp.tanh(GELU_K * (y + 0.044715 * y * y * y))
    return 0.5 * (1.0 + t) + 0.5 * y * (1.0 - t * t) * GELU_K * (1.0 + 3 * 0.044715 * y * y)


def rms_fwd(x, g, name):
    m, d = x.shape
    tm = _row_tile(m, d)

    def body(x_ref, g_ref, o_ref):
        xv = x_ref[...]
        r = lax.rsqrt(jnp.mean(xv * xv, axis=1, keepdims=True) + EPS)
        o_ref[...] = (xv * r * g_ref[...]).astype(o_ref.dtype)

    return _call(body, name=name, grid=(m // tm,),
                 in_specs=[pl.BlockSpec((tm, d), lambda i: (i, 0)), pl.BlockSpec((1, d), lambda i: (0, 0))],
                 out_specs=pl.BlockSpec((tm, d), lambda i: (i, 0)),
                 out_shape=_sds((m, d), BF16))(x, g.reshape(1, d))


def resid_rms(xres, mid, g_post, g_pre, name):
    m, d = xres.shape
    tm = _row_tile(m, d)
    with_pre = g_pre is not None

    def body(*refs):
        if with_pre:
            x_ref, m_ref, gp_ref, gn_ref, xo_ref, h_ref = refs
        else:
            x_ref, m_ref, gp_ref, xo_ref = refs
        mv = m_ref[...]
        r = lax.rsqrt(jnp.mean(mv * mv, axis=1, keepdims=True) + EPS)
        xn = x_ref[...] + mv * r * gp_ref[...]
        xo_ref[...] = xn
        if with_pre:
            r2 = lax.rsqrt(jnp.mean(xn * xn, axis=1, keepdims=True) + EPS)
            h_ref[...] = (xn * r2 * gn_ref[...]).astype(h_ref.dtype)

    row = pl.BlockSpec((tm, d), lambda i: (i, 0))
    vec = pl.BlockSpec((1, d), lambda i: (0, 0))
    ins = [xres, mid, g_post.reshape(1, d)] + ([g_pre.reshape(1, d)] if with_pre else [])
    in_specs = [row, row, vec] + ([vec] if with_pre else [])
    if with_pre:
        return _call(body, name=name, grid=(m // tm,), in_specs=in_specs, out_specs=(row, row),
                     out_shape=(_sds((m, d)), _sds((m, d), BF16)))(*ins)
    return _call(body, name=name, grid=(m // tm,), in_specs=in_specs, out_specs=row,
                 out_shape=_sds((m, d)))(*ins)


def rms_bwd(x, g, dy, name, dy2=None, add=None):
    m, d = x.shape
    tm = _row_tile(m, d)
    has2, hasadd = dy2 is not None, add is not None

    def body(*refs):
        it = iter(refs)
        x_ref, g_ref, dy_ref = next(it), next(it), next(it)
        dy2_ref = next(it) if has2 else None
        add_ref = next(it) if hasadd else None
        dx_ref, dg_ref = next(it), next(it)
        xv = x_ref[...]
        dyv = dy_ref[...]
        if has2:
            dyv = dyv + dy2_ref[...]
        r = lax.rsqrt(jnp.mean(xv * xv, axis=1, keepdims=True) + EPS)
        xh = xv * r
        dxh = dyv * g_ref[...]
        dx = r * (dxh - xh * jnp.mean(dxh * xh, axis=1, keepdims=True))
        if hasadd:
            dx = dx + add_ref[...]
        dx_ref[...] = dx
        part = jnp.sum(dyv * xh, axis=0, keepdims=True)

        @pl.when(pl.program_id(0) == 0)
        def _():
            dg_ref[...] = part

        @pl.when(pl.program_id(0) > 0)
        def _():
            dg_ref[...] += part

    row = pl.BlockSpec((tm, d), lambda i: (i, 0))
    vec = pl.BlockSpec((1, d), lambda i: (0, 0))
    ins = [x, g.reshape(1, d), dy] + ([dy2] if has2 else []) + ([add] if hasadd else [])
    in_specs = [row, vec, row] + ([row] if has2 else []) + ([row] if hasadd else [])
    return _call(body, name=name, grid=(m // tm,), in_specs=in_specs, out_specs=(row, vec),
                 out_shape=(_sds((m, d)), _sds((1, d))))(*ins)


def loss_and_grad(y, target, name):
    m, d = y.shape
    tm = _row_tile(m, d)

    def body(y_ref, t_ref, dy_ref, l_ref):
        e = y_ref[...] - t_ref[...]
        dy_ref[...] = e * (1.0 / d)
        s = jnp.sum(jnp.sum(e * e, axis=1, keepdims=True), axis=0, keepdims=True) * (0.5 / d)
        part = jnp.broadcast_to(s, (1, 128))

        @pl.when(pl.program_id(0) == 0)
        def _():
            l_ref[...] = part

        @pl.when(pl.program_id(0) > 0)
        def _():
            l_ref[...] += part

    row = pl.BlockSpec((tm, d), lambda i: (i, 0))
    dy, l = _call(body, name=name, grid=(m // tm,), in_specs=[row, row],
                  out_specs=(row, pl.BlockSpec((1, 128), lambda i: (0, 0))),
                  out_shape=(_sds((m, d)), _sds((1, 128))))(y, target)
    return dy, l[0, 0]


def adamw(g, w, m, v, name):
    shape = w.shape
    cols = shape[-1]
    rows = int(np.prod(shape[:-1]))
    tm = _row_tile(rows, cols)
    c1 = 1.0 - ADAM_B1 ** ADAM_STEP
    c2 = 1.0 - ADAM_B2 ** ADAM_STEP

    def body(g_ref, w_ref, m_ref, v_ref, d_ref, mo_ref, vo_ref):
        gv = g_ref[...]
        mn = ADAM_B1 * m_ref[...] + (1.0 - ADAM_B1) * gv
        vn = ADAM_B2 * v_ref[...] + (1.0 - ADAM_B2) * (gv * gv)
        m_hat = mn / c1
        v_hat = vn / c2
        d_ref[...] = -ADAM_LR * (m_hat / (jnp.sqrt(v_hat) + ADAM_EPS) + ADAM_WD * w_ref[...])
        mo_ref[...] = mn
        vo_ref[...] = vn

    row = pl.BlockSpec((tm, cols), lambda i: (i, 0))
    outs = _call(body, name=name, grid=(rows // tm,), in_specs=[row] * 4, out_specs=(row,) * 3,
                 out_shape=(_sds((rows, cols)),) * 3)(*[a.reshape(rows, cols) for a in (g, w, m, v)])
    return tuple(o.reshape(shape) for o in outs)


def sum_lead(x, order, out_dtype, name):
    n, rows, cols = x.shape
    tm = _row_tile(rows, cols)

    def body(x_ref, o_ref):
        acc = x_ref[order[0]].astype(F32)
        for i in order[1:]:
            acc = acc + x_ref[i].astype(F32)
        o_ref[...] = acc.astype(out_dtype)

    return _call(body, name=name, grid=(rows // tm,), in_specs=[pl.BlockSpec((n, tm, cols), lambda i: (0, i, 0))],
                 out_specs=pl.BlockSpec((tm, cols), lambda i: (i, 0)), out_shape=_sds((rows, cols), out_dtype))(x)


def add_own_lead(own, parts, name):
    n, rows, cols = parts.shape
    tm = _row_tile(rows, cols)

    def body(o_ref, p_ref, out_ref):
        acc = o_ref[...].astype(F32)
        for i in range(n):
            acc = acc + p_ref[i].astype(F32)
        out_ref[...] = acc

    row = pl.BlockSpec((tm, cols), lambda i: (i, 0))
    return _call(body, name=name, grid=(rows // tm,),
                 in_specs=[row, pl.BlockSpec((n, tm, cols), lambda i: (0, i, 0))], out_specs=row,
                 out_shape=_sds((rows, cols)))(own, parts)


def add_n(xs, out_dtype, name):
    shape = xs[0].shape
    cols = shape[-1]
    rows = int(np.prod(shape[:-1]))
    tm = _row_tile(rows, cols)
    n = len(xs)

    def body(*refs):
        acc = refs[0][...].astype(F32)
        for r in refs[1:n]:
            acc = acc + r[...].astype(F32)
        refs[n][...] = acc.astype(out_dtype)

    row = pl.BlockSpec((tm, cols), lambda i: (i, 0))
    out = _call(body, name=name, grid=(rows // tm,), in_specs=[row] * n, out_specs=row,
                out_shape=_sds((rows, cols), out_dtype))(*[a.reshape(rows, cols) for a in xs])
    return out.reshape(shape)


def mm_plain(a, b, name, ta=False, tb=False, out_dtype=F32, hi=False, tm=MM_TILE, tn=MM_TILE):
    k, m = a.shape[::1 if ta else -1]
    n = b.shape[0] if tb else b.shape[1]
    tm, tn = min(tm, m), min(tn, n)
    dims = (((0 if ta else 1,), (1 if tb else 0,)), ((), ()))

    def body(a_ref, b_ref, o_ref):
        o_ref[...] = _dot(a_ref[...], b_ref[...], dims, hi).astype(out_dtype)

    a_spec = pl.BlockSpec((k, tm), lambda j, i: (0, i)) if ta else pl.BlockSpec((tm, k), lambda j, i: (i, 0))
    b_spec = pl.BlockSpec((tn, k), lambda j, i: (j, 0)) if tb else pl.BlockSpec((k, tn), lambda j, i: (0, j))
    return _call(body, name=name, grid=(n // tn, m // tm), in_specs=[a_spec, b_spec],
                 out_specs=pl.BlockSpec((tm, tn), lambda j, i: (i, j)),
                 out_shape=_sds((m, n), out_dtype), compiler_params=_cparams())(a, b)


def mm_wk(a, gw, off, r, name, jb=N_DEV, relu2=False, tm=MM_TILE, tn=MM_TILE):
    m = a.shape[0]
    d = gw.shape[2]
    tm, tn = min(tm, m), min(tn, d)
    nk = N_DEV // jb
    ob = off // r
    assert off % r == 0 and a.shape[1] == N_DEV * r

    def body(a_ref, b_ref, o_ref, *acc):
        av = a_ref[...]
        if relu2:
            av = jnp.square(jnp.maximum(av, 0.0))
        av = av.astype(BF16)
        p = _dot(av[:, 0:r], b_ref[0])
        for q in range(1, jb):
            p = p + _dot(av[:, q * r:(q + 1) * r], b_ref[q])
        if nk == 1:
            o_ref[...] = p
        else:
            kk = pl.program_id(2)

            @pl.when(kk == 0)
            def _():
                acc[0][...] = p

            @pl.when(kk > 0)
            def _():
                acc[0][...] += p

            @pl.when(kk == nk - 1)
            def _():
                o_ref[...] = acc[0][...]

    return _call(body, name=name, grid=(m // tm, d // tn, nk),
                 in_specs=[pl.BlockSpec((tm, jb * r), lambda i, j, k: (i, k)),
                           pl.BlockSpec((jb, r, tn), lambda i, j, k: (k, ob, j))],
                 out_specs=pl.BlockSpec((tm, tn), lambda i, j, k: (i, j)),
                 out_shape=_sds((m, d)),
                 scratch_shapes=([pltpu.VMEM((tm, tn), F32)] if nk > 1 else []),
                 compiler_params=_cparams())(a, gw)


def mm_wn(a, gw, off, r, name, relu_grad_of=None, out_dtype=F32, with_relu2=False, tm=MM_TILE):
    m, d = a.shape
    tm = min(tm, m)
    ob = off // r
    assert off % r == 0 and gw.shape[2] == d
    epi = relu_grad_of is not None

    def body(*refs):
        it = iter(refs)
        a_ref, b_ref = next(it), next(it)
        e_ref = next(it) if epi else None
        o_ref = next(it)
        p = _dot(a_ref[...], b_ref[...], NT)
        if epi:
            p = p * (2.0 * jnp.maximum(e_ref[...], 0.0))
        o_ref[...] = p.astype(out_dtype)
        if with_relu2:
            act_ref = next(it)
            act_ref[...] = jnp.square(jnp.maximum(p, 0.0)).astype(act_ref.dtype)

    blk = pl.BlockSpec((tm, r), lambda i, j: (i, j))
    in_specs = [pl.BlockSpec((tm, d), lambda i, j: (i, 0)), pl.BlockSpec((None, r, d), lambda i, j: (j, ob, 0))]
    ins = [a, gw]
    if epi:
        in_specs.append(blk)
        ins.append(relu_grad_of)
    out_shape = _sds((m, N_DEV * r), out_dtype)
    if with_relu2:
        return _call(body, name=name, grid=(m // tm, N_DEV), in_specs=in_specs, out_specs=(blk, blk),
                     out_shape=(out_shape, _sds((m, N_DEV * r), BF16)), compiler_params=_cparams())(*ins)
    return _call(body, name=name, grid=(m // tm, N_DEV), in_specs=in_specs, out_specs=blk,
                 out_shape=out_shape, compiler_params=_cparams())(*ins)


def blockdiag_dw(xt, dz, name):
    t = xt.shape[1]

    def body(a_ref, b_ref, o_ref):
        o_ref[...] = _dot(a_ref[...], b_ref[...])

    return _call(body, name=name, grid=(C_BLOCKS,),
                 in_specs=[pl.BlockSpec((C_BLOCK_DIM, t), lambda g: (g, 0)),
                           pl.BlockSpec((t, C_BLOCK_DIM), lambda g: (0, g))],
                 out_specs=pl.BlockSpec((None, C_BLOCK_DIM, C_BLOCK_DIM), lambda g: (g, 0, 0)),
                 out_shape=_sds((C_BLOCKS, C_BLOCK_DIM, C_BLOCK_DIM)))(xt, dz)


def _band_mask(n, nblk, transposed):
    shape = (3 * BLOCK, BLOCK) if transposed else (BLOCK, 3 * BLOCK)
    qi = lax.broadcasted_iota(jnp.int32, shape, 1 if transposed else 0)
    kj = lax.broadcasted_iota(jnp.int32, shape, 0 if transposed else 1)
    lo = jnp.where(n > 0, 0, BLOCK)
    hi = jnp.where(n < nblk - 1, 3 * BLOCK, 2 * BLOCK)
    return (jnp.abs(kj - BLOCK - qi) <= WINDOW) & (kj >= lo) & (kj < hi)


def _band_rows(ref, n, nblk):
    starts = [jnp.maximum(n - 1, 0), n, jnp.minimum(n + 1, nblk - 1)]
    return jnp.concatenate([ref[pl.ds(pl.multiple_of(s * BLOCK, BLOCK), BLOCK), :] for s in starts], axis=0)


def attn_fwd(q, k, v, bias, sink_b, name):
    t = q.shape[0]
    nblk = t // BLOCK
    scale = A_HEAD_DIM ** -0.5

    def body(q_ref, k_ref, v_ref, b_ref, s_ref, o_ref):
        n = pl.program_id(1)
        kb = _band_rows(k_ref, n, nblk).astype(BF16)
        vb = _band_rows(v_ref, n, nblk).astype(BF16)
        mask = _band_mask(n, nblk, False)
        for j in range(A_GROUP):
            sl = slice(j * A_HEAD_DIM, (j + 1) * A_HEAD_DIM)
            s = _dot(q_ref[:, sl], kb, NT) * scale + b_ref[j]
            s = jnp.where(mask, s, NEG_INF)
            sk = s_ref[j:j + 1, 0:1]
            mx = jnp.maximum(jnp.max(s, axis=1, keepdims=True), sk)
            p = jnp.exp(s - mx)
            den = jnp.sum(p, axis=1, keepdims=True) + jnp.exp(sk - mx)
            o_ref[:, sl] = _dot(p / den, vb)

    gw = A_GROUP * A_HEAD_DIM
    return _call(body, name=name, grid=(A_KV_HEADS, nblk),
                 in_specs=[pl.BlockSpec((BLOCK, gw), lambda g, n: (n, g)),
                           pl.BlockSpec((t, A_HEAD_DIM), lambda g, n: (0, g)),
                           pl.BlockSpec((t, A_HEAD_DIM), lambda g, n: (0, g)),
                           pl.BlockSpec((A_GROUP, BLOCK, 3 * BLOCK), lambda g, n: (g, 0, 0)),
                           pl.BlockSpec((None, 8, 128), lambda g, n: (g, 0, 0))],
                 out_specs=pl.BlockSpec((BLOCK, gw), lambda g, n: (n, g)),
                 out_shape=_sds((t, A_Q)))(q, k, v, bias, sink_b)


def attn_bwd(q, k, v, bias, bias_t, sink_b, do, o, name):
    t = q.shape[0]
    nblk = t // BLOCK
    scale = A_HEAD_DIM ** -0.5

    def body(q_ref, k_ref, v_ref, b_ref, bt_ref, s_ref, do_ref, o_ref, dq_ref, dk_ref, dv_ref, db_ref, ds_ref):
        n = pl.program_id(1)

        @pl.when(n == 0)
        def _():
            dk_ref[...] = jnp.zeros_like(dk_ref)
            dv_ref[...] = jnp.zeros_like(dv_ref)
            db_ref[...] = jnp.zeros_like(db_ref)
            ds_ref[...] = jnp.zeros_like(ds_ref)

        kb = _band_rows(k_ref, n, nblk).astype(BF16)
        vb = _band_rows(v_ref, n, nblk).astype(BF16)
        mask = _band_mask(n, nblk, False)
        mask_t = _band_mask(n, nblk, True)
        ones8 = jnp.ones((8, A_HEAD_DIM), F32)
        dkb = jnp.zeros((3 * BLOCK, A_HEAD_DIM), F32)
        dvb = jnp.zeros((3 * BLOCK, A_HEAD_DIM), F32)
        for j in range(A_GROUP):
            sl = slice(j * A_HEAD_DIM, (j + 1) * A_HEAD_DIM)
            qj = q_ref[:, sl].astype(BF16)
            doj = do_ref[:, sl]
            doo = doj * o_ref[:, sl]
            doj = doj.astype(BF16)
            sk = s_ref[j:j + 1, 0:1]
            s = jnp.where(mask, _dot(qj, kb, NT) * scale + b_ref[j], NEG_INF)
            mx = jnp.maximum(jnp.max(s, axis=1, keepdims=True), sk)
            p = jnp.exp(s - mx)
            den = jnp.sum(p, axis=1, keepdims=True) + jnp.exp(sk - mx)
            p = p / den
            psink = jnp.exp(sk - mx) / den
            delta = jnp.sum(doo, axis=1, keepdims=True)
            dsc = p * (_dot(doj, vb, NT) - delta)
            db_ref[j] += dsc
            ds_ref[j:j + 1, :] += jnp.broadcast_to(-jnp.sum(psink * delta, axis=0, keepdims=True), (1, 128))
            dq_ref[:, sl] = _dot(dsc, kb) * scale
            st = jnp.where(mask_t, _dot(kb, qj, NT) * scale + bt_ref[j], NEG_INF)
            mxt = jnp.maximum(jnp.max(st, axis=0, keepdims=True), sk)
            pt = jnp.exp(st - mxt)
            dent = jnp.sum(pt, axis=0, keepdims=True) + jnp.exp(sk - mxt)
            pt = pt / dent
            delta_t = _dot(ones8, doo, NT, hi=True)[0:1, :]
            dst = pt * (_dot(vb, doj, NT) - delta_t)
            dkb = dkb + _dot(dst, qj) * scale
            dvb = dvb + _dot(pt, doj)
        starts = [jnp.maximum(n - 1, 0), n, jnp.minimum(n + 1, nblk - 1)]
        for c, st_ in enumerate(starts):
            rows = pl.ds(pl.multiple_of(st_ * BLOCK, BLOCK), BLOCK)
            dk_ref[rows, :] += dkb[c * BLOCK:(c + 1) * BLOCK, :]
            dv_ref[rows, :] += dvb[c * BLOCK:(c + 1) * BLOCK, :]

    gw = A_GROUP * A_HEAD_DIM
    qspec = pl.BlockSpec((BLOCK, gw), lambda g, n: (n, g))
    kspec = pl.BlockSpec((t, A_HEAD_DIM), lambda g, n: (0, g))
    sspec = pl.BlockSpec((None, 8, 128), lambda g, n: (g, 0, 0))
    bspec = pl.BlockSpec((A_GROUP, BLOCK, 3 * BLOCK), lambda g, n: (g, 0, 0))
    btspec = pl.BlockSpec((A_GROUP, 3 * BLOCK, BLOCK), lambda g, n: (g, 0, 0))
    return _call(body, name=name, grid=(A_KV_HEADS, nblk),
                 in_specs=[qspec, kspec, kspec, bspec, btspec, sspec, qspec, qspec],
                 out_specs=(qspec, kspec, kspec, bspec, sspec),
                 out_shape=(_sds((t, A_Q)), _sds((t, A_KV)), _sds((t, A_KV)),
                            _sds((A_HEADS, BLOCK, 3 * BLOCK)), _sds((A_KV_HEADS, 8, 128))),
                 compiler_params=_cparams())(q, k, v, bias, bias_t, sink_b, do, o)


def xattn_fwd(q, k, v, name):
    t, d = q.shape
    ml = k.shape[0]
    dh = d // X_HEADS
    tq = min(256, t)
    scale = dh ** -0.5

    def body(q_ref, k_ref, v_ref, o_ref):
        s = _dot(q_ref[...], k_ref[...], NT) * scale
        p = jnp.exp(s - jnp.max(s, axis=1, keepdims=True))
        p = p / jnp.sum(p, axis=1, keepdims=True)
        o_ref[...] = _dot(p, v_ref[...])

    qspec = pl.BlockSpec((tq, dh), lambda h, i: (i, h))
    kspec = pl.BlockSpec((ml, dh), lambda h, i: (0, h))
    return _call(body, name=name, grid=(X_HEADS, t // tq), in_specs=[qspec, kspec, kspec], out_specs=qspec,
                 out_shape=_sds((t, d)))(q, k, v)


def xattn_bwd(q, k, v, o, do, name):
    t, d = q.shape
    ml = k.shape[0]
    dh = d // X_HEADS
    tq = min(256, t)
    scale = dh ** -0.5

    def body(q_ref, k_ref, v_ref, o_ref, do_ref, dq_ref, dk_ref, dv_ref):
        i = pl.program_id(1)
        qv, kv, vv = q_ref[...].astype(BF16), k_ref[...].astype(BF16), v_ref[...].astype(BF16)
        dov = do_ref[...]
        doo = dov * o_ref[...]
        dov = dov.astype(BF16)
        s = _dot(qv, kv, NT) * scale
        p = jnp.exp(s - jnp.max(s, axis=1, keepdims=True))
        p = p / jnp.sum(p, axis=1, keepdims=True)
        ds = p * (_dot(dov, vv, NT) - jnp.sum(doo, axis=1, keepdims=True))
        dq_ref[...] = _dot(ds, kv) * scale
        st = _dot(kv, qv, NT) * scale
        pt = jnp.exp(st - jnp.max(st, axis=0, keepdims=True))
        pt = pt / jnp.sum(pt, axis=0, keepdims=True)
        delta_t = _dot(jnp.ones((8, dh), F32), doo, NT, hi=True)[0:1, :]
        dst = pt * (_dot(vv, dov, NT) - delta_t)
        dkp = _dot(dst, qv) * scale
        dvp = _dot(pt, dov)

        @pl.when(i == 0)
        def _():
            dk_ref[...] = dkp
            dv_ref[...] = dvp

        @pl.when(i > 0)
        def _():
            dk_ref[...] += dkp
            dv_ref[...] += dvp

    qspec = pl.BlockSpec((tq, dh), lambda h, i: (i, h))
    kspec = pl.BlockSpec((ml, dh), lambda h, i: (0, h))
    return _call(body, name=name, grid=(X_HEADS, t // tq), in_specs=[qspec, kspec, kspec, qspec, qspec],
                 out_specs=(qspec, kspec, kspec),
                 out_shape=(_sds((t, d)), _sds((ml, d)), _sds((ml, d))))(q, k, v, o, do)


def scan_lead(a, u, name, reverse, inclusive):
    n, r, c = a.shape
    blk = max(1, min(n, (1 << 18) // (max(r, 8) * c)))
    while n % blk:
        blk -= 1
    nb = n // blk

    def body(a_ref, u_ref, o_ref, carry):
        @pl.when(pl.program_id(0) == 0)
        def _():
            carry[...] = jnp.zeros_like(carry)

        def step(s, h):
            idx = (blk - 1 - s) if reverse else s
            hn = a_ref[idx] * h + u_ref[idx]
            o_ref[idx] = hn if inclusive else h
            return hn

        carry[...] = lax.fori_loop(0, blk, step, carry[...])

    spec = pl.BlockSpec((blk, r, c), (lambda i: (nb - 1 - i, 0, 0)) if reverse else (lambda i: (i, 0, 0)))
    return _call(body, name=name, grid=(nb,), in_specs=[spec, spec], out_specs=spec,
                 out_shape=_sds((n, r, c)), scratch_shapes=[pltpu.VMEM((r, c), F32)])(a, u)


def _chunk_mats(bwd_dir):
    i = lax.broadcasted_iota(jnp.int32, (GLA_TILE, GLA_TILE), 0)
    j = lax.broadcasted_iota(jnp.int32, (GLA_TILE, GLA_TILE), 1)
    same = lax.shift_right_logical(i, 4) == lax.shift_right_logical(j, 4)
    if bwd_dir:
        cm, cm_t = same & (j >= i), same & (i >= j)
        mk, mk_t = same & (j > i), same & (i > j)
    else:
        cm, cm_t = same & (j <= i), same & (i <= j)
        mk, mk_t = same & (j <= i), same & (i <= j)
    f = lambda b: jnp.where(b, 1.0, 0.0).astype(F32)
    return f(cm), f(cm_t), mk, mk_t, f(same)


def gla_gates_fwd(zf, zb, w2f, b2f, w2b, b2b, name):
    t = zf.shape[0]
    tm = min(256, t)

    def body(zf_ref, zb_ref, wf_ref, bf_ref, wb_ref, bb_ref, lf_ref, lb_ref):
        lf_ref[...] = -_softplus(-(_dot(zf_ref[...], wf_ref[...], hi=True) + bf_ref[...])) / GATE_TAU
        lb_ref[...] = -_softplus(-(_dot(zb_ref[...], wb_ref[...], hi=True) + bb_ref[...])) / GATE_TAU

    zs = pl.BlockSpec((tm, GATE_RANK), lambda i: (i, 0))
    ws = pl.BlockSpec((GATE_RANK, B_QK), lambda i: (0, 0))
    bs = pl.BlockSpec((1, B_QK), lambda i: (0, 0))
    os_ = pl.BlockSpec((tm, B_QK), lambda i: (i, 0))
    return _call(body, name=name, grid=(t // tm,), in_specs=[zs, zs, ws, bs, ws, bs], out_specs=(os_, os_),
                 out_shape=(_sds((t, B_QK)),) * 2)(zf, zb, w2f, b2f.reshape(1, B_QK), w2b, b2b.reshape(1, B_QK))


def gla_gates_bwd(zf, zb, w2f, b2f, w2b, b2b, dlf, dlb, name):
    t = zf.shape[0]
    tm = min(256, t)

    def body(zf_ref, zb_ref, wf_ref, bf_ref, wb_ref, bb_ref, dlf_ref, dlb_ref,
             dzf_ref, dzb_ref, dpf_ref, dpb_ref, dbf_ref, dbb_ref):
        first = pl.program_id(0) == 0
        for z_ref, w_ref, b_ref, dl_ref, dz_ref, dp_ref, db_ref in (
                (zf_ref, wf_ref, bf_ref, dlf_ref, dzf_ref, dpf_ref, dbf_ref),
                (zb_ref, wb_ref, bb_ref, dlb_ref, dzb_ref, dpb_ref, dbb_ref)):
            pre = _dot(z_ref[...], w_ref[...], hi=True) + b_ref[...]
            dpre = dl_ref[...] * (1.0 / GATE_TAU) * _sigmoid(-pre)
            dp_ref[...] = dpre
            dz_ref[...] = _dot(dpre, w_ref[...], NT, hi=True)
            part = jnp.sum(dpre, axis=0, keepdims=True)

            @pl.when(first)
            def _():
                db_ref[...] = part

            @pl.when(jnp.logical_not(first))
            def _():
                db_ref[...] += part

    zs = pl.BlockSpec((tm, GATE_RANK), lambda i: (i, 0))
    ws = pl.BlockSpec((GATE_RANK, B_QK), lambda i: (0, 0))
    bs = pl.BlockSpec((1, B_QK), lambda i: (0, 0))
    os_ = pl.BlockSpec((tm, B_QK), lambda i: (i, 0))
    return _call(body, name=name, grid=(t // tm,), in_specs=[zs, zs, ws, bs, ws, bs, os_, os_],
                 out_specs=(zs, zs, os_, os_, bs, bs),
                 out_shape=(_sds((t, GATE_RANK)),) * 2 + (_sds((t, B_QK)),) * 2 + (_sds((1, B_QK)),) * 2)(
        zf, zb, w2f, b2f.reshape(1, B_QK), w2b, b2b.reshape(1, B_QK), dlf, dlb)


def gla_outer(xt, lat, y, name, bwd_dir, mode):
    t = y.shape[0]
    nchunk = t // GLA_CHUNK
    khat = mode == "khat"
    scale = B_KEY_DIM ** -0.5

    def body(xt_ref, lat_ref, y_ref, *outs):
        _, cm_t, _, _, same = _chunk_mats(bwd_dir)
        lat_v = lat_ref[...]
        bt = _dot(lat_v, cm_t, hi=True)
        if khat:
            mult = jnp.exp(_dot(lat_v, same, hi=True) - bt)
        else:
            mult = jnp.exp(bt) * scale
        xm = xt_ref[...] * mult
        lane = lax.shift_right_logical(lax.broadcasted_iota(jnp.int32, (1, GLA_TILE), 1), 4)
        ones = jnp.ones((GLA_TILE, B_VAL_DIM), F32)
        yv = [y_ref[:, h * B_VAL_DIM:(h + 1) * B_VAL_DIM].astype(BF16) for h in range(B_HEADS)]
        for c in range(CHUNKS_PER_TILE):
            sel = lane == c
            xc = jnp.where(sel, xm, 0.0).astype(BF16)
            for h in range(B_HEADS):
                rows = slice(h * B_KEY_DIM, (h + 1) * B_KEY_DIM)
                outs[0][c, rows, :] = _dot(xc[rows, :], yv[h])
            if khat:
                outs[1][c] = jnp.exp(_dot(jnp.where(sel, lat_v, 0.0), ones, hi=True))

    tspec = pl.BlockSpec((B_QK, GLA_TILE), lambda i: (0, i))
    ospec = pl.BlockSpec((CHUNKS_PER_TILE, B_QK, B_VAL_DIM), lambda i: (i, 0, 0))
    oshape = _sds((nchunk, B_QK, B_VAL_DIM))
    return _call(body, name=name, grid=(t // GLA_TILE,),
                 in_specs=[tspec, tspec, pl.BlockSpec((GLA_TILE, B_V), lambda i: (i, 0))],
                 out_specs=(ospec, ospec) if khat else ospec,
                 out_shape=(oshape, oshape) if khat else oshape)(xt, lat, y)


def _head_lane_mask(h):
    lane = lax.broadcasted_iota(jnp.int32, (1, B_QK), 1)
    return lax.shift_right_logical(lane, 6) == h


def _chunk_rows(c):
    return slice(c * GLA_CHUNK, (c + 1) * GLA_CHUNK)


def gla_inner_fwd(q, k, v, la, sp, name, bwd_dir):
    t = q.shape[0]
    scale = B_KEY_DIM ** -0.5

    def body(q_ref, k_ref, v_ref, la_ref, sp_ref, o_ref):
        cm, _, mk, _, _ = _chunk_mats(bwd_dir)
        b = _dot(cm, la_ref[...], hi=True)
        qt = q_ref[...] * scale * jnp.exp(b)
        kt = k_ref[...] * jnp.exp(jnp.minimum(-b, EXP_CLAMP))
        spb = [sp_ref[c].astype(BF16) for c in range(CHUNKS_PER_TILE)]
        for h in range(B_HEADS):
            lm = _head_lane_mask(h)
            qm = jnp.where(lm, qt, 0.0).astype(BF16)
            km = jnp.where(lm, kt, 0.0).astype(BF16)
            vs = slice(h * B_VAL_DIM, (h + 1) * B_VAL_DIM)
            att = jnp.where(mk, _dot(qm, km, NT), 0.0)
            inter = jnp.concatenate([_dot(qm[_chunk_rows(c), :], spb[c]) for c in range(CHUNKS_PER_TILE)], axis=0)
            o_ref[:, vs] = _dot(att, v_ref[:, vs]) + inter

    qs = pl.BlockSpec((GLA_TILE, B_QK), lambda i: (i, 0))
    vs_ = pl.BlockSpec((GLA_TILE, B_V), lambda i: (i, 0))
    ss = pl.BlockSpec((CHUNKS_PER_TILE, B_QK, B_VAL_DIM), lambda i: (i, 0, 0))
    return _call(body, name=name, grid=(t // GLA_TILE,), in_specs=[qs, qs, vs_, qs, ss], out_specs=vs_,
                 out_shape=_sds((t, B_V)))(q, k, v, la, sp)


def gla_inner_bwd(q, k, v, la, do, sp, gs, dec, name, bwd_dir, add=None):
    t = q.shape[0]
    scale = B_KEY_DIM ** -0.5
    hasadd = add is not None

    def body(*refs):
        it = iter(refs)
        q_ref, k_ref, v_ref, la_ref, do_ref, sp_ref, gs_ref, dec_ref = [next(it) for _ in range(8)]
        adds = [next(it) for _ in range(3)] if hasadd else None
        dq_ref, dk_ref, dv_ref, dla_ref = [next(it) for _ in range(4)]
        cm, cm_t, mk, mk_t, same = _chunk_mats(bwd_dir)
        la_v = la_ref[...]
        b = _dot(cm, la_v, hi=True)
        btot = _dot(same, la_v, hi=True)
        eb = jnp.exp(b)
        ek = jnp.exp(jnp.minimum(-b, EXP_CLAMP))
        ekh = jnp.exp(btot - b)
        qt = q_ref[...] * scale * eb
        kt = k_ref[...] * ek
        kh = k_ref[...] * ekh
        spb = [sp_ref[c].astype(BF16) for c in range(CHUNKS_PER_TILE)]
        gsb = [gs_ref[c].astype(BF16) for c in range(CHUNKS_PER_TILE)]
        dqt = jnp.zeros((GLA_TILE, B_QK), F32)
        dkt = jnp.zeros((GLA_TILE, B_QK), F32)
        dkh = jnp.zeros((GLA_TILE, B_QK), F32)
        for h in range(B_HEADS):
            lm = _head_lane_mask(h)
            qm = jnp.where(lm, qt, 0.0).astype(BF16)
            km = jnp.where(lm, kt, 0.0).astype(BF16)
            khm = jnp.where(lm, kh, 0.0).astype(BF16)
            vs = slice(h * B_VAL_DIM, (h + 1) * B_VAL_DIM)
            vh = v_ref[:, vs].astype(BF16)
            doh = do_ref[:, vs].astype(BF16)
            da = jnp.where(mk, _dot(doh, vh, NT), 0.0)
            da_t = jnp.where(mk_t, _dot(vh, doh, NT), 0.0)
            att_t = jnp.where(mk_t, _dot(km, qm, NT), 0.0)
            dv_h = _dot(att_t, doh) + jnp.concatenate(
                [_dot(khm[_chunk_rows(c), :], gsb[c]) for c in range(CHUNKS_PER_TILE)], axis=0)
            if hasadd:
                dv_h = dv_h + adds[2][:, vs]
            dv_ref[:, vs] = dv_h
            dq_inter = jnp.concatenate(
                [_dot(doh[_chunk_rows(c), :], spb[c], NT) for c in range(CHUNKS_PER_TILE)], axis=0)
            dqt = dqt + _dot(da, km) + jnp.where(lm, dq_inter, 0.0)
            dkt = dkt + _dot(da_t, qm)
            dkh_inter = jnp.concatenate(
                [_dot(vh[_chunk_rows(c), :], gsb[c], NT) for c in range(CHUNKS_PER_TILE)], axis=0)
            dkh = dkh + jnp.where(lm, dkh_inter, 0.0)
        dq = dqt * scale * eb
        dk = dkt * ek + dkh * ekh
        if hasadd:
            dq = dq + adds[0][...]
            dk = dk + adds[1][...]
        dq_ref[...] = dq
        dk_ref[...] = dk
        db = dqt * qt - dkt * kt - dkh * kh
        ones16 = jnp.ones((GLA_CHUNK, B_VAL_DIM), F32)
        t2 = jnp.concatenate(
            [_dot(ones16, gs_ref[c] * dec_ref[c] * sp_ref[c], NT, hi=True) for c in range(CHUNKS_PER_TILE)], axis=0)
        dla_ref[...] = _dot(cm_t, db, hi=True) + _dot(same, dkh * kh, hi=True) + t2

    qs = pl.BlockSpec((GLA_TILE, B_QK), lambda i: (i, 0))
    vs_ = pl.BlockSpec((GLA_TILE, B_V), lambda i: (i, 0))
    ss = pl.BlockSpec((CHUNKS_PER_TILE, B_QK, B_VAL_DIM), lambda i: (i, 0, 0))
    ins = [q, k, v, la, do, sp, gs, dec] + (list(add) if hasadd else [])
    in_specs = [qs, qs, vs_, qs, vs_, ss, ss, ss] + ([qs, qs, vs_] if hasadd else [])
    return _call(body, name=name, grid=(t // GLA_TILE,), in_specs=in_specs, out_specs=(qs, qs, vs_, qs),
                 out_shape=(_sds((t, B_QK)), _sds((t, B_QK)), _sds((t, B_V)), _sds((t, B_QK))),
                 compiler_params=_cparams())(*ins)


def gla_out_fwd(of, ob, g, gn, name):
    t = of.shape[0]
    tm = min(256, t)

    def body(of_ref, ob_ref, g_ref, gn_ref, o_ref):
        for h in range(B_HEADS):
            vs = slice(h * B_VAL_DIM, (h + 1) * B_VAL_DIM)
            o = of_ref[:, vs] + ob_ref[:, vs]
            on = o * lax.rsqrt(jnp.mean(o * o, axis=1, keepdims=True) + EPS)
            gv = g_ref[:, vs]
            o_ref[:, vs] = on * gn_ref[:, vs] * (gv * _sigmoid(gv))

    row = pl.BlockSpec((tm, B_V), lambda i: (i, 0))
    vec = pl.BlockSpec((1, B_V), lambda i: (0, 0))
    return _call(body, name=name, grid=(t // tm,), in_specs=[row, row, row, vec], out_specs=row,
                 out_shape=_sds((t, B_V)))(of, ob, g, gn.reshape(1, B_V))


def gla_out_bwd(of, ob, g, gn, dout, name):
    t = of.shape[0]
    tm = min(256, t)

    def body(of_ref, ob_ref, g_ref, gn_ref, d_ref, do_ref, dg_ref, dgn_ref):
        first = pl.program_id(0) == 0
        for h in range(B_HEADS):
            vs = slice(h * B_VAL_DIM, (h + 1) * B_VAL_DIM)
            o = of_ref[:, vs] + ob_ref[:, vs]
            r = lax.rsqrt(jnp.mean(o * o, axis=1, keepdims=True) + EPS)
            on = o * r
            gv = g_ref[:, vs]
            sg = _sigmoid(gv)
            silu = gv * sg
            dv = d_ref[:, vs]
            gnv = gn_ref[:, vs]
            dg_ref[:, vs] = dv * on * gnv * (sg * (1.0 + gv * (1.0 - sg)))
            don = dv * silu * gnv
            do_ref[:, vs] = r * (don - on * jnp.mean(don * on, axis=1, keepdims=True))
            part = jnp.sum(dv * silu * on, axis=0, keepdims=True)

            @pl.when(first)
            def _():
                dgn_ref[:, vs] = part

            @pl.when(jnp.logical_not(first))
            def _():
                dgn_ref[:, vs] += part

    row = pl.BlockSpec((tm, B_V), lambda i: (i, 0))
    vec = pl.BlockSpec((1, B_V), lambda i: (0, 0))
    return _call(body, name=name, grid=(t // tm,), in_specs=[row, row, row, vec, row], out_specs=(row, row, vec),
                 out_shape=(_sds((t, B_V)), _sds((t, B_V)), _sds((1, B_V))))(of, ob, g, gn.reshape(1, B_V), dout)


def _shift(x, k):
    if k > 0:
        return jnp.concatenate([x[k:], jnp.zeros((k,) + x.shape[1:], x.dtype)], axis=0)
    return jnp.concatenate([jnp.zeros((-k,) + x.shape[1:], x.dtype), x[:k]], axis=0)


def _lru_gates(xc, s, wa_ref, ba_ref, wx_ref, bx_ref, lam_ref):
    cols = [slice(g * C_BLOCK_DIM, (g + 1) * C_BLOCK_DIM) for g in range(C_BLOCKS)]
    zr = jnp.concatenate([_dot(xc[:, cs], wa_ref[s, g]) for g, cs in enumerate(cols)], axis=1) + ba_ref[s:s + 1, :]
    zi = jnp.concatenate([_dot(xc[:, cs], wx_ref[s, g]) for g, cs in enumerate(cols)], axis=1) + bx_ref[s:s + 1, :]
    r = _sigmoid(zr)
    i = _sigmoid(zi)
    sp = _softplus(-lam_ref[s:s + 1, :])
    log_a = -LRU_C * r * sp
    return r, i, sp, log_a


def lru_gates_fwd(x0, xm2, xm1, xp1, cw, cb, wa, ba, wx, bx, lam, name):
    t = x0.shape[0]
    tm = min(256, t)

    def body(x0_ref, xm2_ref, xm1_ref, xp1_ref, cw_ref, cb_ref, wa_ref, ba_ref, wx_ref, bx_ref, lam_ref,
             xc_ref, a0_ref, u0_ref, a1_ref, u1_ref):
        xc = (xm2_ref[...] * cw_ref[0:1, :] + xm1_ref[...] * cw_ref[1:2, :] + x0_ref[...] * cw_ref[2:3, :]
              + xp1_ref[...] * cw_ref[3:4, :] + cb_ref[...])
        xc_ref[...] = xc
        for s, (a_ref, u_ref) in enumerate(((a0_ref, u0_ref), (a1_ref, u1_ref))):
            _, i, _, log_a = _lru_gates(xc, s, wa_ref, ba_ref, wx_ref, bx_ref, lam_ref)
            a_ref[...] = jnp.exp(log_a)
            u_ref[...] = jnp.sqrt(-_expm1(2.0 * log_a)) * (i * xc)

    row = pl.BlockSpec((tm, C_WIDTH), lambda i: (i, 0))
    full = lambda shape: pl.BlockSpec(shape, lambda i: (0,) * len(shape))
    wshape = (2, C_BLOCKS, C_BLOCK_DIM, C_BLOCK_DIM)
    return _call(body, name=name, grid=(t // tm,),
                 in_specs=[row] * 4 + [full((4, C_WIDTH)), full((1, C_WIDTH)), full(wshape), full((2, C_WIDTH)),
                                       full(wshape), full((2, C_WIDTH)), full((2, C_WIDTH))],
                 out_specs=(row,) * 5, out_shape=(_sds((t, C_WIDTH)),) * 5)(
        x0, xm2, xm1, xp1, cw, cb.reshape(1, C_WIDTH), wa, ba, wx, bx, lam)


def lru_gates_bwd(xc, g0, hs0, g1, hs1, wa, ba, wx, bx, lam, name):
    t = xc.shape[0]
    tm = min(256, t)

    def body(xc_ref, g0_ref, hs0_ref, g1_ref, hs1_ref, wa_ref, ba_ref, wx_ref, bx_ref, lam_ref,
             dxc_ref, dzr0_ref, dzi0_ref, dzr1_ref, dzi1_ref, dlam_ref, dba_ref, dbx_ref):
        first = pl.program_id(0) == 0

        @pl.when(first)
        def _():
            dlam_ref[...] = jnp.zeros_like(dlam_ref)
            dba_ref[...] = jnp.zeros_like(dba_ref)
            dbx_ref[...] = jnp.zeros_like(dbx_ref)

        xcv = xc_ref[...]
        dxc = jnp.zeros_like(xcv)
        cols = [slice(g * C_BLOCK_DIM, (g + 1) * C_BLOCK_DIM) for g in range(C_BLOCKS)]
        for s, (g_ref, hs_ref, dzr_ref, dzi_ref) in enumerate(
                ((g0_ref, hs0_ref, dzr0_ref, dzi0_ref), (g1_ref, hs1_ref, dzr1_ref, dzi1_ref))):
            r, i, sp, log_a = _lru_gates(xcv, s, wa_ref, ba_ref, wx_ref, bx_ref, lam_ref)
            du = g_ref[...]
            da = du * hs_ref[...]
            a = jnp.exp(log_a)
            e2 = jnp.exp(2.0 * log_a)
            c = jnp.sqrt(-_expm1(2.0 * log_a))
            ix = i * xcv
            dlog = da * a - du * ix * (e2 / c)
            dix = du * c
            dxc = dxc + dix * i
            dzi = dix * xcv * i * (1.0 - i)
            dzr = dlog * (-LRU_C * sp) * r * (1.0 - r)
            dzr_ref[...] = dzr
            dzi_ref[...] = dzi
            dxc = dxc + jnp.concatenate(
                [_dot(dzr[:, cs], wa_ref[s, g], NT) + _dot(dzi[:, cs], wx_ref[s, g], NT) for g, cs in enumerate(cols)],
                axis=1)
            dsp = jnp.sum(dlog * (-LRU_C * r), axis=0, keepdims=True)
            dlam_ref[s:s + 1, :] += dsp * (-_sigmoid(-lam_ref[s:s + 1, :]))
            dba_ref[s:s + 1, :] += jnp.sum(dzr, axis=0, keepdims=True)
            dbx_ref[s:s + 1, :] += jnp.sum(dzi, axis=0, keepdims=True)
        dxc_ref[...] = dxc

    row = pl.BlockSpec((tm, C_WIDTH), lambda i: (i, 0))
    full = lambda shape: pl.BlockSpec(shape, lambda i: (0,) * len(shape))
    wshape = (2, C_BLOCKS, C_BLOCK_DIM, C_BLOCK_DIM)
    vec2 = full((2, C_WIDTH))
    return _call(body, name=name, grid=(t // tm,),
                 in_specs=[row] * 5 + [full(wshape), vec2, full(wshape), vec2, vec2],
                 out_specs=(row,) * 5 + (vec2,) * 3,
                 out_shape=(_sds((t, C_WIDTH)),) * 5 + (_sds((2, C_WIDTH)),) * 3)(
        xc, g0, hs0, g1, hs1, wa, ba, wx, bx, lam)


def lru_out_fwd(h0, h1, y, name):
    t = y.shape[0]
    tm = min(256, t)

    def body(h0_ref, h1_ref, y_ref, o_ref):
        o_ref[...] = (h0_ref[...] + h1_ref[...]) * _gelu(y_ref[...])

    row = pl.BlockSpec((tm, C_WIDTH), lambda i: (i, 0))
    return _call(body, name=name, grid=(t // tm,), in_specs=[row] * 3, out_specs=row,
                 out_shape=_sds((t, C_WIDTH)))(h0, h1, y)


def lru_out_bwd(h0, h1, y, dout, name):
    t = y.shape[0]
    tm = min(256, t)

    def body(h0_ref, h1_ref, y_ref, d_ref, dh_ref, dy_ref):
        yv = y_ref[...]
        dv = d_ref[...]
        dh_ref[...] = dv * _gelu(yv)
        dy_ref[...] = dv * (h0_ref[...] + h1_ref[...]) * _gelu_grad(yv)

    row = pl.BlockSpec((tm, C_WIDTH), lambda i: (i, 0))
    return _call(body, name=name, grid=(t // tm,), in_specs=[row] * 4, out_specs=(row, row),
                 out_shape=(_sds((t, C_WIDTH)),) * 2)(h0, h1, y, dout)


def conv_bwd(dxc, dp2, dp1, dm1, x0, xm2, xm1, xp1, cw, name):
    t = x0.shape[0]
    tm = min(256, t)

    def body(d_ref, dp2_ref, dp1_ref, dm1_ref, x0_ref, xm2_ref, xm1_ref, xp1_ref, cw_ref, dx_ref, dcw_ref, dcb_ref):
        @pl.when(pl.program_id(0) == 0)
        def _():
            dcw_ref[...] = jnp.zeros_like(dcw_ref)
            dcb_ref[...] = jnp.zeros_like(dcb_ref)

        dv = d_ref[...]
        dx_ref[...] = (dp2_ref[...] * cw_ref[0:1, :] + dp1_ref[...] * cw_ref[1:2, :] + dv * cw_ref[2:3, :]
                       + dm1_ref[...] * cw_ref[3:4, :])
        for j, x_ref in enumerate((xm2_ref, xm1_ref, x0_ref, xp1_ref)):
            dcw_ref[j:j + 1, :] += jnp.sum(dv * x_ref[...], axis=0, keepdims=True)
        dcb_ref[...] += jnp.sum(dv, axis=0, keepdims=True)

    row = pl.BlockSpec((tm, C_WIDTH), lambda i: (i, 0))
    cws = pl.BlockSpec((4, C_WIDTH), lambda i: (0, 0))
    cbs = pl.BlockSpec((1, C_WIDTH), lambda i: (0, 0))
    return _call(body, name=name, grid=(t // tm,), in_specs=[row] * 8 + [cws], out_specs=(row, cws, cbs),
                 out_shape=(_sds((t, C_WIDTH)), _sds((4, C_WIDTH)), _sds((1, C_WIDTH))))(
        dxc, dp2, dp1, dm1, x0, xm2, xm1, xp1, cw)


def _my_place():
    return lax.axis_index("x"), lax.axis_index("y"), lax.axis_index("c")


def all_gather(xs, name):
    r, c = xs.shape

    def body(x_ref, out_ref, send_sems, recv_sems, local_sem):
        x, y, cc = _my_place()
        me, sibling = (x, y, cc), (x, y, 1 - cc)
        chips = [(1 - x, y), (x, 1 - y), (1 - x, 1 - y)]

        def slot(px, py, pc):
            return out_ref.at[4 * px + 2 * py + pc]

        def copy(k, block, to, src=None):
            return pltpu.make_async_remote_copy(
                src_ref=slot(*block) if src is None else src, dst_ref=slot(*block),
                send_sem=send_sems.at[k], recv_sem=recv_sems.at[k], device_id=to, device_id_type=MESH)

        mine = pltpu.make_async_copy(x_ref, slot(*me), local_sem)
        mine.start()
        first = [copy(0, me, sibling, src=x_ref)]
        first += [copy(1 + j, me, (*chip, cc), src=x_ref) for j, chip in enumerate(chips)]
        for cp in first:
            cp.start()
        passed = [copy(4 + j, (*chip, cc), sibling) for j, chip in enumerate(chips)]
        for j, chip in enumerate(chips):
            copy(1 + j, (*chip, cc), me).wait_recv()
            passed[j].start()
        copy(0, sibling, me).wait_recv()
        for j, chip in enumerate(chips):
            copy(4 + j, (*chip, 1 - cc), me).wait_recv()
        for cp in first + passed:
            cp.wait_send()
        mine.wait()

    return _call(body, name=name, in_specs=[pl.BlockSpec(memory_space=pl.ANY)],
                 out_specs=pl.BlockSpec(memory_space=pl.ANY), out_shape=_sds((N_DEV, r, c), xs.dtype),
                 scratch_shapes=[pltpu.SemaphoreType.DMA((7,)), pltpu.SemaphoreType.DMA((7,)),
                                 pltpu.SemaphoreType.DMA])(xs)


def _stream_rows(r):
    nch = SIBLING_STREAMS // 4 if r % (8 * (SIBLING_STREAMS // 4)) == 0 else 1
    return nch, r // nch


def exchange_sibling(gw, name):
    _, r, c = gw.shape
    g5 = gw.reshape(4, 2, r, c)
    nch, rows = _stream_rows(r)

    def body(g_ref, out_ref, send_sems, recv_sems):
        x, y, cc = _my_place()
        swaps = []
        for q in range(4):
            for s in range(nch):
                k = q * nch + s
                win = pl.ds(s * rows, rows)
                swaps.append(pltpu.make_async_remote_copy(
                    src_ref=g_ref.at[q, 1 - cc, win], dst_ref=out_ref.at[q, win], send_sem=send_sems.at[k],
                    recv_sem=recv_sems.at[k], device_id=(x, y, 1 - cc), device_id_type=MESH))
        for cp in swaps:
            cp.start()
        for cp in swaps:
            cp.wait()

    nsem = 4 * nch
    return _call(body, name=name, in_specs=[pl.BlockSpec(memory_space=pl.ANY)],
                 out_specs=pl.BlockSpec(memory_space=pl.ANY), out_shape=_sds((4, r, c), gw.dtype),
                 scratch_shapes=[pltpu.SemaphoreType.DMA((nsem,)), pltpu.SemaphoreType.DMA((nsem,))])(g5)


HBM_SPEC = pl.BlockSpec(memory_space=pltpu.HBM)
SEM_SPEC = pl.BlockSpec(memory_space=pltpu.SEMAPHORE)
DATAFLOW = pltpu.SideEffectType.DATAFLOW_SIDE_EFFECTING


def _hbm(a):
    return pltpu.with_memory_space_constraint(a, pltpu.HBM)


def _peers(x, y, cc):
    return [(x, y, 1 - cc), (1 - x, y, cc), (x, 1 - y, cc), (1 - x, 1 - y, cc)]


def _slot(p):
    return 4 * p[0] + 2 * p[1] + p[2]


def gather_start(blk, name):
    r, c = blk.shape

    def body(v_ref, land_ref, send_sems, recv_sems, v_thru, land_thru, token):
        x, y, cc = _my_place()
        for k, to in enumerate(_peers(x, y, cc)):
            pltpu.make_async_remote_copy(
                src_ref=v_ref, dst_ref=land_ref.at[_slot((x, y, cc))], send_sem=send_sems.at[k],
                recv_sem=recv_sems.at[k], device_id=to, device_id_type=MESH).start()
        token[...] = jnp.zeros_like(token)

    return _call(
        body, name=name,
        out_shape=(pltpu.SemaphoreType.DMA((4,)), pltpu.SemaphoreType.DMA((4,)), pltpu.HBM((r, c), blk.dtype),
                   pltpu.HBM((N_DEV, r, c), blk.dtype), _sds((8, 128))),
        in_specs=(HBM_SPEC, HBM_SPEC),
        out_specs=(SEM_SPEC, SEM_SPEC, HBM_SPEC, HBM_SPEC, pl.BlockSpec(memory_space=pltpu.VMEM)),
        input_output_aliases={0: 2, 1: 3},
        compiler_params=pltpu.CompilerParams(has_side_effects=DATAFLOW),
    )(_hbm(blk), _hbm(lax.empty((N_DEV, r, c), blk.dtype)))


def gather_wait(send_sems, recv_sems, v_thru, land_thru, after, name):
    def body(v_ref, land_ref, send_sems, recv_sems, after_ref, v_out, land_out):
        x, y, cc = _my_place()
        for k, peer in enumerate(_peers(x, y, cc)):
            cp = pltpu.make_async_remote_copy(
                src_ref=v_ref, dst_ref=land_ref.at[_slot(peer)], send_sem=send_sems.at[k], recv_sem=recv_sems.at[k],
                device_id=peer, device_id_type=MESH)
            cp.wait_send()
            cp.wait_recv()

    return _call(
        body, name=name,
        out_shape=(pltpu.HBM(v_thru.shape, v_thru.dtype), pltpu.HBM(land_thru.shape, land_thru.dtype)),
        in_specs=(HBM_SPEC, HBM_SPEC, SEM_SPEC, SEM_SPEC, pl.BlockSpec(memory_space=pl.ANY)),
        out_specs=(HBM_SPEC, HBM_SPEC), input_output_aliases={0: 0, 1: 1},
        compiler_params=pltpu.CompilerParams(has_side_effects=DATAFLOW),
    )(v_thru, land_thru, send_sems, recv_sems, after)


def gather_pass(land, name):
    _, r, c = land.shape
    nch, rows = _stream_rows(r)

    def body(land_ref, out_ref, send_sems, recv_sems):
        x, y, cc = _my_place()
        peers = _peers(x, y, cc)
        copies = []
        for j in range(3):
            mine, theirs = _slot(peers[1 + j]), _slot((peers[1 + j][0], peers[1 + j][1], 1 - cc))
            for s in range(nch):
                k = j * nch + s
                win = pl.ds(s * rows, rows)
                send = pltpu.make_async_remote_copy(
                    src_ref=land_ref.at[mine, win], dst_ref=out_ref.at[mine, win], send_sem=send_sems.at[k],
                    recv_sem=recv_sems.at[k], device_id=peers[0], device_id_type=MESH)
                recv = pltpu.make_async_remote_copy(
                    src_ref=land_ref.at[mine, win], dst_ref=out_ref.at[theirs, win], send_sem=send_sems.at[k],
                    recv_sem=recv_sems.at[k], device_id=peers[0], device_id_type=MESH)
                copies.append((send, recv))
        for send, _ in copies:
            send.start()
        for send, recv in copies:
            send.wait_send()
            recv.wait_recv()

    nsem = 3 * nch
    return _call(body, name=name, in_specs=[pl.BlockSpec(memory_space=pl.ANY)],
                 out_specs=pl.BlockSpec(memory_space=pl.ANY), out_shape=_sds(land.shape, land.dtype),
                 input_output_aliases={0: 0},
                 scratch_shapes=[pltpu.SemaphoreType.DMA((nsem,)), pltpu.SemaphoreType.DMA((nsem,))])(land)


def chips_start(p, name):
    _, r, c = p.shape

    def body(p_ref, land_ref, send_sems, recv_sems, p_thru, land_thru, token):
        x, y, cc = _my_place()
        for j, (px, py, pc) in enumerate(_peers(x, y, cc)[1:]):
            pltpu.make_async_remote_copy(
                src_ref=p_ref.at[2 * px + py], dst_ref=land_ref.at[j], send_sem=send_sems.at[j],
                recv_sem=recv_sems.at[j], device_id=(px, py, pc), device_id_type=MESH).start()
        token[...] = jnp.zeros_like(token)

    return _call(
        body, name=name,
        out_shape=(pltpu.SemaphoreType.DMA((3,)), pltpu.SemaphoreType.DMA((3,)), pltpu.HBM(p.shape, p.dtype),
                   pltpu.HBM((3, r, c), p.dtype), _sds((8, 128))),
        in_specs=(HBM_SPEC, HBM_SPEC),
        out_specs=(SEM_SPEC, SEM_SPEC, HBM_SPEC, HBM_SPEC, pl.BlockSpec(memory_space=pltpu.VMEM)),
        input_output_aliases={0: 2, 1: 3},
        compiler_params=pltpu.CompilerParams(has_side_effects=DATAFLOW),
    )(_hbm(p), _hbm(lax.empty((3, r, c), p.dtype)))


def chips_wait(send_sems, recv_sems, p_thru, land_thru, after, name):
    def body(p_ref, land_ref, send_sems, recv_sems, after_ref, p_out, land_out):
        x, y, cc = _my_place()
        for j, (px, py, pc) in enumerate(_peers(x, y, cc)[1:]):
            cp = pltpu.make_async_remote_copy(
                src_ref=p_ref.at[2 * px + py], dst_ref=land_ref.at[j], send_sem=send_sems.at[j],
                recv_sem=recv_sems.at[j], device_id=(px, py, pc), device_id_type=MESH)
            cp.wait_send()
            cp.wait_recv()

    return _call(
        body, name=name,
        out_shape=(pltpu.HBM(p_thru.shape, p_thru.dtype), pltpu.HBM(land_thru.shape, land_thru.dtype)),
        in_specs=(HBM_SPEC, HBM_SPEC, SEM_SPEC, SEM_SPEC, pl.BlockSpec(memory_space=pl.ANY)),
        out_specs=(HBM_SPEC, HBM_SPEC), input_output_aliases={0: 0, 1: 1},
        compiler_params=pltpu.CompilerParams(has_side_effects=DATAFLOW),
    )(p_thru, land_thru, send_sems, recv_sems, after)


def reduce_scatter_begin(gw, name, tag):
    _, r, c = gw.shape
    theirs = exchange_sibling(gw, name + "_sibling")
    mine = lax.dynamic_index_in_dim(gw.reshape(4, 2, r, c), lax.axis_index("c"), axis=1, keepdims=False)
    chip_sum = add_n([mine, theirs], BF16, name + "_add2")
    return chips_start(chip_sum, name + "_start" + tag)


def reduce_scatter_end(started, after, name, tag):
    send_sems, recv_sems, p_thru, land_thru, _ = started
    parts, land = chips_wait(send_sems, recv_sems, p_thru, land_thru, after, name + "_wait" + tag)
    mine = lax.dynamic_index_in_dim(parts, 2 * lax.axis_index("x") + lax.axis_index("y"), axis=0, keepdims=False)
    return add_own_lead(mine, land, name + "_add4")


def _pack(arrs):
    flat = jnp.concatenate([a.reshape(-1).astype(F32) for a in arrs])
    n = flat.shape[0]
    pad = (-n) % PACK_ELEMS
    return jnp.pad(flat, (0, pad)).reshape(-1, 128)


def _unpack(packed, shapes):
    flat = packed.reshape(-1)
    out, off = [], 0
    for s in shapes:
        n = int(np.prod(s))
        out.append(flat[off:off + n].reshape(s))
        off += n
    return out


def _t5_bucket(rel):
    nb = N_BUCKETS // 2
    max_exact = nb // 2
    ret = jnp.where(rel > 0, nb, 0)
    n = jnp.abs(rel)
    nf = jnp.maximum(n, 1).astype(jnp.float32)
    large = max_exact + (jnp.log(nf / max_exact) / math.log(MAX_DISTANCE / max_exact)
                         * (nb - max_exact)).astype(jnp.int32)
    large = jnp.minimum(large, nb - 1)
    return ret + jnp.where(n < max_exact, n, large)


SMALL_SHARDED = ("gla_w2_f", "gla_w2_b", "conv_w", "lru_ba", "lru_bx", "lru_lambda")
SMALL_REPL = ("rel_bias", "attn_sink", "gla_b2_f", "gla_b2_b", "gla_norm", "conv_b", "lru_wa", "lru_wx",
              "norm_mix_pre", "norm_mix_post", "norm_mem", "norm_x_pre", "norm_x_post", "norm_ff_pre", "norm_ff_post")
BIG = ("w_in", "w_out", "xq", "xk", "xv", "xo", "w_up", "w_down")
WEIGHTS = ['rel_bias', 'w_in', 'w_out', 'attn_sink', 'gla_w2_f', 'gla_b2_f', 'gla_w2_b', 'gla_b2_b', 'gla_norm',
           'conv_w', 'conv_b', 'lru_wa', 'lru_ba', 'lru_wx', 'lru_bx', 'lru_lambda', 'xq', 'xk', 'xv', 'xo', 'w_up',
           'w_down', 'norm_mix_pre', 'norm_mix_post', 'norm_mem', 'norm_x_pre', 'norm_x_post', 'norm_ff_pre',
           'norm_ff_post']


def _step(x, mem, loss_target, w, m, v):
    depth = w["w_in"].shape[0]
    t, d = x.shape[1], x.shape[2]
    ml = mem.shape[1]
    rx = d // N_DEV
    rf = w["w_up"].shape[2]
    r_out = D_MIX // N_DEV
    x = x.reshape(t, d)
    mem = mem.reshape(ml, d)
    loss_target = loss_target.reshape(t, d)
    my_idx = 4 * lax.axis_index("x") + 2 * lax.axis_index("y") + lax.axis_index("c")

    off_in, off_out = 0, W_IN_ROWS
    off_xq = off_out + r_out
    off_xk, off_xv, off_xo = off_xq + rx, off_xq + 2 * rx, off_xq + 3 * rx
    off_up = off_xq + 4 * rx
    off_down = off_up + rf
    r_tot = off_down + rf

    sh_shapes = [w[n].shape for n in SMALL_SHARDED]
    gathered = all_gather(_pack([w[n] for n in SMALL_SHARDED]), "ag_small")
    per_dev = [_unpack(gathered[j], sh_shapes) for j in range(N_DEV)]
    full = {n: jnp.concatenate([per_dev[j][i] for j in range(N_DEV)], axis=-1) for i, n in enumerate(SMALL_SHARDED)}
    for n in SMALL_REPL:
        full[n] = w[n]

    ag_started = []
    for l in range(depth):
        w_in_t = jnp.pad(w["w_in"][l].T, ((0, W_IN_ROWS - W_IN_SHARD), (0, 0)))
        blk = jnp.concatenate([w_in_t, w["w_out"][l], w["xq"][l], w["xk"][l], w["xv"][l], w["xo"][l],
                               w["w_up"][l].T, w["w_down"][l]], axis=0).astype(BF16)
        blk, _ = lax.optimization_barrier((blk, gathered))
        ag_started.append(gather_start(blk, "ag_start%d" % l))
    x = x + sum(st[4][0, 0] for st in ag_started)
    gws = [None] * depth

    qi = jnp.arange(BLOCK)[:, None]
    kj = jnp.arange(3 * BLOCK)[None, :]
    onehot_t = (jnp.arange(N_BUCKETS)[:, None] == _t5_bucket(kj - BLOCK - qi).reshape(1, -1)).astype(F32)
    bias = mm_plain(full["rel_bias"].T, onehot_t, "rel_bias_lookup", hi=True, tn=3 * BLOCK * 16)
    bias = bias.reshape(A_HEADS, BLOCK, 3 * BLOCK)
    bias_t = jnp.transpose(bias, (0, 2, 1))

    def sink_rows(sink):
        s = jnp.broadcast_to(sink.reshape(A_KV_HEADS, A_GROUP, 1), (A_KV_HEADS, A_GROUP, 128))
        return jnp.pad(s, ((0, 0), (0, 8 - A_GROUP), (0, 0)))

    def split_proj(p):
        outs, off = [], 0
        for s in SPLIT_SIZES:
            outs.append(p[:, off:off + s])
            off += s
        return outs

    def lead(a):
        return a.reshape(a.shape[0], C_WIDTH // 128, 128)

    saved = []
    h = rms_fwd(x, full["norm_mix_pre"][0], "rms_first")
    for l in range(depth):
        send_sems, recv_sems, blk_thru, land_thru, _ = ag_started[l]
        blk_done, land = gather_wait(send_sems, recv_sems, blk_thru, land_thru, x, "ag_wait%d" % l)
        land = gather_pass(land, "ag_pass")
        gw = gws[l] = lax.dynamic_update_index_in_dim(land, blk_done, my_idx, 0)
        sv = {"x": x, "h_in": h}
        proj_pad = mm_wn(h, gw, off_in, W_IN_ROWS, "mm_w_in")
        proj = proj_pad.reshape(t, N_DEV, W_IN_ROWS)[:, :, :W_IN_SHARD].reshape(t, D_IN)
        aq, ak, av, bq, bk, bv, bg, zf, zb, cx, cy = split_proj(proj)
        sv.update(aq=aq, ak=ak, av=av, bq=bq, bk=bk, bv=bv, bg=bg, zf=zf, zb=zb, cx=cx, cy=cy)
        sink_b = sink_rows(full["attn_sink"][l])
        oa = attn_fwd(aq, ak, av, bias, sink_b, "attn_fwd")
        la_f, la_b = gla_gates_fwd(zf, zb, full["gla_w2_f"][l], full["gla_b2_f"][l], full["gla_w2_b"][l],
                                   full["gla_b2_b"][l], "gla_gates_fwd")
        bk_t = bk.T
        gla = {}
        for nm, la, bdir in (("f", la_f, False), ("b", la_b, True)):
            la_t = la.T
            u, dec = gla_outer(bk_t, la_t, bv, "gla_outer_k_" + nm, bdir, "khat")
            sp = scan_lead(dec, u, "gla_state_scan_" + nm, reverse=bdir, inclusive=False)
            o_dir = gla_inner_fwd(bq, bk, bv, la, sp, "gla_inner_fwd_" + nm, bdir)
            gla[nm] = dict(la=la, la_t=la_t, dec=dec, sp=sp, o=o_dir)
        ob = gla_out_fwd(gla["f"]["o"], gla["b"]["o"], bg, full["gla_norm"][l], "gla_out_fwd")
        sv["gla"] = gla
        xm2, xm1, xp1 = _shift(cx, -2), _shift(cx, -1), _shift(cx, 1)
        xc, a0, u0, a1, u1 = lru_gates_fwd(cx, xm2, xm1, xp1, full["conv_w"][l], full["conv_b"][l], full["lru_wa"][l],
                                           full["lru_ba"][l], full["lru_wx"][l], full["lru_bx"][l],
                                           full["lru_lambda"][l], "lru_gates_fwd")
        h0 = scan_lead(lead(a0), lead(u0), "lru_scan_fwd", reverse=False, inclusive=True).reshape(t, C_WIDTH)
        h1 = scan_lead(lead(a1), lead(u1), "lru_scan_rev", reverse=True, inclusive=True).reshape(t, C_WIDTH)
        oc = lru_out_fwd(h0, h1, cy, "lru_out_fwd")
        sv.update(xm2=xm2, xm1=xm1, xp1=xp1, xc=xc, a0=a0, a1=a1, h0=h0, h1=h1, oa=oa)
        cat = jnp.concatenate([oa, ob, oc], axis=1).astype(BF16)
        mixed = mm_wk(cat, gw, off_out, r_out, "mm_w_out")
        x1, h2 = resid_rms(x, mixed, full["norm_mix_post"][l], full["norm_x_pre"][l], "resid_rms")
        sv.update(cat=cat, mixed=mixed, x1=x1, h2=h2)
        memn = rms_fwd(mem, full["norm_mem"][l], "rms_mem")
        q = mm_wk(h2, gw, off_xq, rx, "mm_xq")
        k = mm_wk(memn, gw, off_xk, rx, "mm_xkv")
        vv = mm_wk(memn, gw, off_xv, rx, "mm_xkv")
        ox = xattn_fwd(q, k, vv, "xattn_fwd")
        ca = mm_wk(ox, gw, off_xo, rx, "mm_xo")
        x2, h3 = resid_rms(x1, ca, full["norm_x_post"][l], full["norm_ff_pre"][l], "resid_rms")
        sv.update(memn=memn, q=q, k=k, v=vv, ox=ox, ca=ca, x2=x2, h3=h3)
        up, act = mm_wn(h3, gw, off_up, rf, "mm_w_up", with_relu2=True)
        ff = mm_wk(act, gw, off_down, rf, "mm_w_down", jb=max(1, min(N_DEV, 2048 // rf)))
        if l + 1 < depth:
            x, h = resid_rms(x2, ff, full["norm_ff_post"][l], full["norm_mix_pre"][l + 1], "resid_rms")
        else:
            x = resid_rms(x2, ff, full["norm_ff_post"][l], None, "resid_rms_last")
        sv.update(up=up, act=act, ff=ff)
        saved.append(sv)

    dx, loss_local = loss_and_grad(x, loss_target, "loss")
    loss = lax.psum(loss_local, AXES)

    grads = {n: [None] * depth for n in WEIGHTS if n != "rel_bias"}
    dbias_total = None
    big_grads = [None] * depth
    rs_started = [None] * depth
    bf = lambda a: a.astype(BF16)
    for l in reversed(range(depth)):
        gw = gws[l]
        sv = saved[l]
        dff, grads["norm_ff_post"][l] = rms_bwd(sv["ff"], full["norm_ff_post"][l], dx, "rms_bwd")
        dup = mm_wn(dff, gw, off_down, rf, "mm_w_down_dx", relu_grad_of=sv["up"], out_dtype=BF16)
        g_down = mm_plain(sv["act"], dff, "mm_dw_down", ta=True, out_dtype=BF16)
        g_up_t = mm_plain(dup, sv["h3"], "mm_dw_up", ta=True, out_dtype=BF16)
        dh3 = mm_wk(dup, gw, off_up, rf, "mm_w_up_dx", jb=max(1, min(N_DEV, 2048 // rf)))
        dx2, grads["norm_ff_pre"][l] = rms_bwd(sv["x2"], full["norm_ff_pre"][l], dh3, "rms_bwd_add", add=dx)
        dca, grads["norm_x_post"][l] = rms_bwd(sv["ca"], full["norm_x_post"][l], dx2, "rms_bwd")
        dox = mm_wn(dca, gw, off_xo, rx, "mm_x_dx")
        g_xo = mm_plain(sv["ox"], dca, "mm_dw_xo", ta=True, out_dtype=BF16)
        dq, dk, dv = xattn_bwd(sv["q"], sv["k"], sv["v"], sv["ox"], dox, "xattn_bwd")
        g_xq = mm_plain(sv["h2"], dq, "mm_dw_d", ta=True, out_dtype=BF16)
        g_xk = mm_plain(sv["memn"], dk, "mm_dw_mem", ta=True, out_dtype=BF16)
        g_xv = mm_plain(sv["memn"], dv, "mm_dw_mem", ta=True, out_dtype=BF16)
        dh2 = mm_wn(dq, gw, off_xq, rx, "mm_x_dx")
        dmem_k = mm_wn(dk, gw, off_xk, rx, "mm_x_dx_mem")
        dmem_v = mm_wn(dv, gw, off_xv, rx, "mm_x_dx_mem")
        _, grads["norm_mem"][l] = rms_bwd(mem, full["norm_mem"][l], dmem_k, "rms_bwd_mem", dy2=dmem_v)
        dx1, grads["norm_x_pre"][l] = rms_bwd(sv["x1"], full["norm_x_pre"][l], dh2, "rms_bwd_add", add=dx2)
        dmixed, grads["norm_mix_post"][l] = rms_bwd(sv["mixed"], full["norm_mix_post"][l], dx1, "rms_bwd")
        dcat = mm_wn(dmixed, gw, off_out, r_out, "mm_w_out_dx")
        g_out = mm_plain(sv["cat"], dmixed, "mm_dw_d", ta=True, out_dtype=BF16)
        doa, dob, doc = dcat[:, :A_Q], dcat[:, A_Q:A_Q + B_V], dcat[:, A_Q + B_V:]
        daq, dak, dav, dbias, dsink = attn_bwd(sv["aq"], sv["ak"], sv["av"], bias, bias_t,
                                               sink_rows(full["attn_sink"][l]), doa, sv["oa"], "attn_bwd")
        grads["attn_sink"][l] = dsink[:, :A_GROUP, 0].reshape(A_HEADS)
        dbias_total = dbias if dbias_total is None else add_n([dbias_total, dbias], F32, "add_dbias")
        gf, gb = sv["gla"]["f"], sv["gla"]["b"]
        do_gla, dbg, dgn = gla_out_bwd(gf["o"], gb["o"], sv["bg"], full["gla_norm"][l], dob, "gla_out_bwd")
        grads["gla_norm"][l] = dgn.reshape(B_V)
        bq_t = sv["bq"].T
        acc = None
        dlas = {}
        for nm, gd, bdir in (("f", gf, False), ("b", gb, True)):
            wq = gla_outer(bq_t, gd["la_t"], do_gla, "gla_outer_q_" + nm, bdir, "qtil")
            gs = scan_lead(gd["dec"], wq, "gla_adj_scan_" + nm, reverse=not bdir, inclusive=False)
            dbq, dbk, dbv, dlas[nm] = gla_inner_bwd(sv["bq"], sv["bk"], sv["bv"], gd["la"], do_gla, gd["sp"], gs,
                                                    gd["dec"], "gla_inner_bwd_" + nm, bdir, add=acc)
            acc = (dbq, dbk, dbv)
        dzf, dzb, dpre_f, dpre_b, db2f, db2b = gla_gates_bwd(
            sv["zf"], sv["zb"], full["gla_w2_f"][l], full["gla_b2_f"][l], full["gla_w2_b"][l], full["gla_b2_b"][l],
            dlas["f"], dlas["b"], "gla_gates_bwd")
        grads["gla_b2_f"][l] = db2f.reshape(B_QK)
        grads["gla_b2_b"][l] = db2b.reshape(B_QK)
        grads["gla_w2_f"][l] = mm_plain(sv["zf"].T, dpre_f, "mm_dw_gate", hi=True)
        grads["gla_w2_b"][l] = mm_plain(sv["zb"].T, dpre_b, "mm_dw_gate", hi=True)
        dh, dcy = lru_out_bwd(sv["h0"], sv["h1"], sv["cy"], doc, "lru_out_bwd")
        g0 = scan_lead(lead(_shift(sv["a0"], 1)), lead(dh), "lru_scan_rev", reverse=True,
                       inclusive=True).reshape(t, C_WIDTH)
        g1 = scan_lead(lead(_shift(sv["a1"], -1)), lead(dh), "lru_scan_fwd", reverse=False,
                       inclusive=True).reshape(t, C_WIDTH)
        dxc, dzr0, dzi0, dzr1, dzi1, dlam, dba, dbx = lru_gates_bwd(
            sv["xc"], g0, _shift(sv["h0"], -1), g1, _shift(sv["h1"], 1), full["lru_wa"][l], full["lru_ba"][l],
            full["lru_wx"][l], full["lru_bx"][l], full["lru_lambda"][l], "lru_gates_bwd")
        xc_t = bf(sv["xc"].T)
        grads["lru_wa"][l] = jnp.stack([blockdiag_dw(xc_t, dzr0, "lru_dw"), blockdiag_dw(xc_t, dzr1, "lru_dw")])
        grads["lru_wx"][l] = jnp.stack([blockdiag_dw(xc_t, dzi0, "lru_dw"), blockdiag_dw(xc_t, dzi1, "lru_dw")])
        grads["lru_lambda"][l], grads["lru_ba"][l], grads["lru_bx"][l] = dlam, dba, dbx
        dcx, dcw, dcb = conv_bwd(dxc, _shift(dxc, 2), _shift(dxc, 1), _shift(dxc, -1), sv["cx"], sv["xm2"],
                                 sv["xm1"], sv["xp1"], full["conv_w"][l], "conv_bwd")
        grads["conv_w"][l] = dcw
        grads["conv_b"][l] = dcb.reshape(C_WIDTH)
        dproj = jnp.concatenate([daq, dak, dav, dbq, dbk, dbv, dbg, dzf, dzb, dcx, dcy], axis=1)
        dproj_pad = jnp.pad(dproj.reshape(t, N_DEV, W_IN_SHARD),
                            ((0, 0), (0, 0), (0, W_IN_ROWS - W_IN_SHARD))).reshape(t, N_DEV * W_IN_ROWS).astype(BF16)
        g_in_t = mm_plain(dproj_pad, sv["h_in"], "mm_dw_in", ta=True, out_dtype=BF16)
        dh1 = mm_wk(dproj_pad, gw, off_in, W_IN_ROWS, "mm_w_in_dx", jb=2)
        dx, grads["norm_mix_pre"][l] = rms_bwd(sv["x"], full["norm_mix_pre"][l], dh1, "rms_bwd_add", add=dx1)
        parts = [g_in_t, g_out, g_xq, g_xk, g_xv, g_xo, g_up_t, g_down]
        gpack = jnp.concatenate([p.reshape(N_DEV, p.shape[0] // N_DEV, d) for p in parts], axis=1)
        rs_started[l] = reduce_scatter_begin(gpack, "rs_weights", str(l))
        dx = dx + rs_started[l][4][0, 0]
    for l in range(depth):
        big_grads[l] = reduce_scatter_end(rs_started[l], dx, "rs_weights", str(l))

    grad_rel = mm_plain(dbias_total.reshape(A_HEADS, -1), onehot_t, "rel_bias_grad", tb=True, hi=True).T

    small_names = [n for n in WEIGHTS if n not in BIG]
    small_g = {"rel_bias": grad_rel}
    for n in small_names:
        if n != "rel_bias":
            small_g[n] = jnp.stack([g.reshape(full[n].shape[1:]) for g in grads[n]])
    shapes = [small_g[n].shape for n in small_names]
    packed = all_gather(_pack([small_g[n] for n in small_names]), "ag_small_grads")
    summed = sum_lead(packed, tuple(range(N_DEV)), F32, "add8_small")
    small_g = dict(zip(small_names, _unpack(summed, shapes)))
    for n in SMALL_SHARDED:
        wdt = w[n].shape[-1]
        small_g[n] = lax.dynamic_slice_in_dim(small_g[n], my_idx * wdt, wdt, axis=small_g[n].ndim - 1)

    grad_out, delta, new_m, new_v = {}, {}, {}, {}
    sshapes = [w[n].shape for n in small_names]
    ds, ms, vs = adamw(_pack([small_g[n] for n in small_names]), _pack([w[n] for n in small_names]),
                       _pack([m[n] for n in small_names]), _pack([v[n] for n in small_names]), "adamw_small")
    for n, g_, d_, m_, v_ in zip(small_names, [small_g[n] for n in small_names], _unpack(ds, sshapes),
                                 _unpack(ms, sshapes), _unpack(vs, sshapes)):
        grad_out[n], delta[n], new_m[n], new_v[n] = g_, d_, m_, v_

    def rows(l, off, r):
        return big_grads[l][off:off + r]

    big_g = {
        "w_in": jnp.stack([rows(l, off_in, W_IN_SHARD).T for l in range(depth)]),
        "w_out": jnp.stack([rows(l, off_out, r_out) for l in range(depth)]),
        "xq": jnp.stack([rows(l, off_xq, rx) for l in range(depth)]),
        "xk": jnp.stack([rows(l, off_xk, rx) for l in range(depth)]),
        "xv": jnp.stack([rows(l, off_xv, rx) for l in range(depth)]),
        "xo": jnp.stack([rows(l, off_xo, rx) for l in range(depth)]),
        "w_up": jnp.stack([rows(l, off_up, rf).T for l in range(depth)]),
        "w_down": jnp.stack([rows(l, off_down, rf) for l in range(depth)]),
    }
    for n in BIG:
        grad_out[n] = big_g[n]
        delta[n], new_m[n], new_v[n] = adamw(big_g[n], w[n], m[n], v[n], "adamw_" + n)

    return (loss, dx.reshape(1, t, d), *[grad_out[n] for n in WEIGHTS], *[delta[n] for n in WEIGHTS],
            *[new_m[n] for n in WEIGHTS], *[new_v[n] for n in WEIGHTS])


def kernel(x, mem, rel_bias, w_in, w_out, attn_sink, gla_w2_f, gla_b2_f, gla_w2_b, gla_b2_b, gla_norm, conv_w, conv_b, lru_wa, lru_ba, lru_wx, lru_bx, lru_lambda, xq, xk, xv, xo, w_up, w_down, norm_mix_pre, norm_mix_post, norm_mem, norm_x_pre, norm_x_post, norm_ff_pre, norm_ff_post, loss_target, m_rel_bias, m_w_in, m_w_out, m_attn_sink, m_gla_w2_f, m_gla_b2_f, m_gla_w2_b, m_gla_b2_b, m_gla_norm, m_conv_w, m_conv_b, m_lru_wa, m_lru_ba, m_lru_wx, m_lru_bx, m_lru_lambda, m_xq, m_xk, m_xv, m_xo, m_w_up, m_w_down, m_norm_mix_pre, m_norm_mix_post, m_norm_mem, m_norm_x_pre, m_norm_x_post, m_norm_ff_pre, m_norm_ff_post, v_rel_bias, v_w_in, v_w_out, v_attn_sink, v_gla_w2_f, v_gla_b2_f, v_gla_w2_b, v_gla_b2_b, v_gla_norm, v_conv_w, v_conv_b, v_lru_wa, v_lru_ba, v_lru_wx, v_lru_bx, v_lru_lambda, v_xq, v_xk, v_xv, v_xo, v_w_up, v_w_down, v_norm_mix_pre, v_norm_mix_post, v_norm_mem, v_norm_x_pre, v_norm_x_post, v_norm_ff_pre, v_norm_ff_post):
    given = dict(locals())
    w = {n: given[n] for n in WEIGHTS}
    m = {n: given["m_" + n] for n in WEIGHTS}
    v = {n: given["v_" + n] for n in WEIGHTS}
    return _step(x, mem, loss_target, w, m, v)
```

```python
import math

import jax
import jax.numpy as jnp
import numpy as np
from jax import lax
from jax.experimental import pallas as pl
from jax.experimental.pallas import tpu as pltpu

F32 = jnp.float32
BF16 = jnp.bfloat16
HI = lax.Precision.HIGHEST
NN = (((1,), (0,)), ((), ()))
NT = (((1,), (1,)), ((), ()))
MESH = pl.DeviceIdType.MESH
AXES = ("x", "y", "c")
N_DEV = 8

A_HEAD_DIM = 128
A_HEADS = 8
A_KV_HEADS = 2
A_GROUP = 4
WINDOW = 128
BLOCK = 128
N_BUCKETS = 32
MAX_DISTANCE = 128
B_HEADS = 4
B_KEY_DIM = 64
B_VAL_DIM = 128
GATE_RANK = 16
GATE_TAU = 16.0
GLA_CHUNK = 16
C_WIDTH = 512
C_BLOCKS = 4
C_BLOCK_DIM = 128
LRU_C = 8.0
X_HEADS = 4
EPS = 1e-6
NEG_INF = -1e30
A_Q = A_HEADS * A_HEAD_DIM
A_KV = A_KV_HEADS * A_HEAD_DIM
B_QK = B_HEADS * B_KEY_DIM
B_V = B_HEADS * B_VAL_DIM
SPLIT_SIZES = (A_Q, A_KV, A_KV, B_QK, B_QK, B_V, B_V, GATE_RANK, GATE_RANK, C_WIDTH, C_WIDTH)
D_IN = sum(SPLIT_SIZES)
D_MIX = A_Q + B_V + C_WIDTH
W_IN_SHARD = D_IN // N_DEV
W_IN_ROWS = 768
GLA_TILE = 128
CHUNKS_PER_TILE = GLA_TILE // GLA_CHUNK
EXP_CLAMP = 80.0

ADAM_LR = 0.001
ADAM_B1 = 0.9
ADAM_B2 = 0.999
ADAM_EPS = 1e-08
ADAM_WD = 0.01
ADAM_STEP = 10

VMEM_LIMIT_BYTES = 52 * 1024 * 1024
MM_TILE = 1024
SIBLING_STREAMS = 16
PACK_ELEMS = 128 * 2048


def _call(body, **kw):
    return pl.pallas_call(body, **kw)


def _cparams():
    return pltpu.CompilerParams(vmem_limit_bytes=VMEM_LIMIT_BYTES)


def _dot(a, b, dims=NN, hi=False):
    if hi:
        return lax.dot_general(a, b, dims, precision=HI, preferred_element_type=F32)
    return lax.dot_general(a.astype(BF16), b.astype(BF16), dims, preferred_element_type=F32)


def _sds(shape, dtype=F32):
    return jax.ShapeDtypeStruct(tuple(shape), dtype)


def _row_tile(rows, cols, target_elems=1 << 18):
    want = max(8, target_elems // max(cols, 1))
    if rows <= want:
        return rows
    t = (want // 8) * 8
    while t >= 8:
        if rows % t == 0:
            return t
        t -= 8
    return rows


def _expm1(x):
    poly = x * (1.0 + x * (1.0 / 2 + x * (1.0 / 6 + x * (1.0 / 24 + x * (1.0 / 120 + x * (
        1.0 / 720 + x * (1.0 / 5040 + x * (1.0 / 40320))))))))
    return jnp.where(jnp.abs(x) < 0.3, poly, jnp.exp(x) - 1.0)


def _log1p(e):
    w = 1.0 + e
    return jnp.where(w == 1.0, e, jnp.log(w) * e / (w - 1.0))


def _softplus(x):
    return jnp.maximum(x, 0.0) + _log1p(jnp.exp(-jnp.abs(x)))


def _sigmoid(x):
    return jax.nn.sigmoid(x)


GELU_K = math.sqrt(2.0 / math.pi)


def _gelu(y):
    t = jnp.tanh(GELU_K * (y + 0.044715 * y * y * y))
    return 0.5 * y * (1.0 + t)


def _gelu_grad(y):
    t = jnp.tanh(GELU_K * (y + 0.044715 * y * y * y))
    return 0.5 * (1.0 + t) + 0.5 * y * (1.0 - t * t) * GELU_K * (1.0 + 3 * 0.044715 * y * y)


def rms_fwd(x, g, name):
    m, d = x.shape
    tm = _row_tile(m, d)

    def body(x_ref, g_ref, o_ref):
        xv = x_ref[...]
        r = lax.rsqrt(jnp.mean(xv * xv, axis=1, keepdims=True) + EPS)
        o_ref[...] = (xv * r * g_ref[...]).astype(o_ref.dtype)

    return _call(body, name=name, grid=(m // tm,),
                 in_specs=[pl.BlockSpec((tm, d), lambda i: (i, 0)), pl.BlockSpec((1, d), lambda i: (0, 0))],
                 out_specs=pl.BlockSpec((tm, d), lambda i: (i, 0)),
                 out_shape=_sds((m, d), BF16))(x, g.reshape(1, d))


def resid_rms(xres, mid, g_post, g_pre, name):
    m, d = xres.shape
    tm = _row_tile(m, d)
    with_pre = g_pre is not None

    def body(*refs):
        if with_pre:
            x_ref, m_ref, gp_ref, gn_ref, xo_ref, h_ref = refs
        else:
            x_ref, m_ref, gp_ref, xo_ref = refs
        mv = m_ref[...]
        r = lax.rsqrt(jnp.mean(mv * mv, axis=1, keepdims=True) + EPS)
        xn = x_ref[...] + mv * r * gp_ref[...]
        xo_ref[...] = xn
        if with_pre:
            r2 = lax.rsqrt(jnp.mean(xn * xn, axis=1, keepdims=True) + EPS)
            h_ref[...] = (xn * r2 * gn_ref[...]).astype(h_ref.dtype)

    row = pl.BlockSpec((tm, d), lambda i: (i, 0))
    vec = pl.BlockSpec((1, d), lambda i: (0, 0))
    ins = [xres, mid, g_post.reshape(1, d)] + ([g_pre.reshape(1, d)] if with_pre else [])
    in_specs = [row, row, vec] + ([vec] if with_pre else [])
    if with_pre:
        return _call(body, name=name, grid=(m // tm,), in_specs=in_specs, out_specs=(row, row),
                     out_shape=(_sds((m, d)), _sds((m, d), BF16)))(*ins)
    return _call(body, name=name, grid=(m // tm,), in_specs=in_specs, out_specs=row,
                 out_shape=_sds((m, d)))(*ins)


def rms_bwd(x, g, dy, name, dy2=None, add=None):
    m, d = x.shape
    tm = _row_tile(m, d)
    has2, hasadd = dy2 is not None, add is not None

    def body(*refs):
        it = iter(refs)
        x_ref, g_ref, dy_ref = next(it), next(it), next(it)
        dy2_ref = next(it) if has2 else None
        add_ref = next(it) if hasadd else None
        dx_ref, dg_ref = next(it), next(it)
        xv = x_ref[...]
        dyv = dy_ref[...]
        if has2:
            dyv = dyv + dy2_ref[...]
        r = lax.rsqrt(jnp.mean(xv * xv, axis=1, keepdims=True) + EPS)
        xh = xv * r
        dxh = dyv * g_ref[...]
        dx = r * (dxh - xh * jnp.mean(dxh * xh, axis=1, keepdims=True))
        if hasadd:
            dx = dx + add_ref[...]
        dx_ref[...] = dx
        part = jnp.sum(dyv * xh, axis=0, keepdims=True)

        @pl.when(pl.program_id(0) == 0)
        def _():
            dg_ref[...] = part

        @pl.when(pl.program_id(0) > 0)
        def _():
            dg_ref[...] += part

    row = pl.BlockSpec((tm, d), lambda i: (i, 0))
    vec = pl.BlockSpec((1, d), lambda i: (0, 0))
    ins = [x, g.reshape(1, d), dy] + ([dy2] if has2 else []) + ([add] if hasadd else [])
    in_specs = [row, vec, row] + ([row] if has2 else []) + ([row] if hasadd else [])
    return _call(body, name=name, grid=(m // tm,), in_specs=in_specs, out_specs=(row, vec),
                 out_shape=(_sds((m, d)), _sds((1, d))))(*ins)


def loss_and_grad(y, target, name):
    m, d = y.shape
    tm = _row_tile(m, d)

    def body(y_ref, t_ref, dy_ref, l_ref):
        e = y_ref[...] - t_ref[...]
        dy_ref[...] = e * (1.0 / d)
        s = jnp.sum(jnp.sum(e * e, axis=1, keepdims=True), axis=0, keepdims=True) * (0.5 / d)
        part = jnp.broadcast_to(s, (1, 128))

        @pl.when(pl.program_id(0) == 0)
        def _():
            l_ref[...] = part

        @pl.when(pl.program_id(0) > 0)
        def _():
            l_ref[...] += part

    row = pl.BlockSpec((tm, d), lambda i: (i, 0))
    dy, l = _call(body, name=name, grid=(m // tm,), in_specs=[row, row],
                  out_specs=(row, pl.BlockSpec((1, 128), lambda i: (0, 0))),
                  out_shape=(_sds((m, d)), _sds((1, 128))))(y, target)
    return dy, l[0, 0]


def adamw(g, w, m, v, name):
    shape = w.shape
    cols = shape[-1]
    rows = int(np.prod(shape[:-1]))
    tm = _row_tile(rows, cols)
    c1 = 1.0 - ADAM_B1 ** ADAM_STEP
    c2 = 1.0 - ADAM_B2 ** ADAM_STEP

    def body(g_ref, w_ref, m_ref, v_ref, d_ref, mo_ref, vo_ref):
        gv = g_ref[...]
        mn = ADAM_B1 * m_ref[...] + (1.0 - ADAM_B1) * gv
        vn = ADAM_B2 * v_ref[...] + (1.0 - ADAM_B2) * (gv * gv)
        m_hat = mn / c1
        v_hat = vn / c2
        d_ref[...] = -ADAM_LR * (m_hat / (jnp.sqrt(v_hat) + ADAM_EPS) + ADAM_WD * w_ref[...])
        mo_ref[...] = mn
        vo_ref[...] = vn

    row = pl.BlockSpec((tm, cols), lambda i: (i, 0))
    outs = _call(body, name=name, grid=(rows // tm,), in_specs=[row] * 4, out_specs=(row,) * 3,
                 out_shape=(_sds((rows, cols)),) * 3)(*[a.reshape(rows, cols) for a in (g, w, m, v)])
    return tuple(o.reshape(shape) for o in outs)


def sum_lead(x, order, out_dtype, name):
    n, rows, cols = x.shape
    tm = _row_tile(rows, cols)

    def body(x_ref, o_ref):
        acc = x_ref[order[0]].astype(F32)
        for i in order[1:]:
            acc = acc + x_ref[i].astype(F32)
        o_ref[...] = acc.astype(out_dtype)

    return _call(body, name=name, grid=(rows // tm,), in_specs=[pl.BlockSpec((n, tm, cols), lambda i: (0, i, 0))],
                 out_specs=pl.BlockSpec((tm, cols), lambda i: (i, 0)), out_shape=_sds((rows, cols), out_dtype))(x)


def add_own_lead(own, parts, name):
    n, rows, cols = parts.shape
    tm = _row_tile(rows, cols)

    def body(o_ref, p_ref, out_ref):
        acc = o_ref[...].astype(F32)
        for i in range(n):
            acc = acc + p_ref[i].astype(F32)
        out_ref[...] = acc

    row = pl.BlockSpec((tm, cols), lambda i: (i, 0))
    return _call(body, name=name, grid=(rows // tm,),
                 in_specs=[row, pl.BlockSpec((n, tm, cols), lambda i: (0, i, 0))], out_specs=row,
                 out_shape=_sds((rows, cols)))(own, parts)


def add_n(xs, out_dtype, name):
    shape = xs[0].shape
    cols = shape[-1]
    rows = int(np.prod(shape[:-1]))
    tm = _row_tile(rows, cols)
    n = len(xs)

    def body(*refs):
        acc = refs[0][...].astype(F32)
        for r in refs[1:n]:
            acc = acc + r[...].astype(F32)
        refs[n][...] = acc.astype(out_dtype)

    row = pl.BlockSpec((tm, cols), lambda i: (i, 0))
    out = _call(body, name=name, grid=(rows // tm,), in_specs=[row] * n, out_specs=row,
                out_shape=_sds((rows, cols), out_dtype))(*[a.reshape(rows, cols) for a in xs])
    return out.reshape(shape)


def mm_plain(a, b, name, ta=False, tb=False, out_dtype=F32, hi=False, tm=MM_TILE, tn=MM_TILE):
    k, m = a.shape[::1 if ta else -1]
    n = b.shape[0] if tb else b.shape[1]
    tm, tn = min(tm, m), min(tn, n)
    dims = (((0 if ta else 1,), (1 if tb else 0,)), ((), ()))

    def body(a_ref, b_ref, o_ref):
        o_ref[...] = _dot(a_ref[...], b_ref[...], dims, hi).astype(out_dtype)

    a_spec = pl.BlockSpec((k, tm), lambda j, i: (0, i)) if ta else pl.BlockSpec((tm, k), lambda j, i: (i, 0))
    b_spec = pl.BlockSpec((tn, k), lambda j, i: (j, 0)) if tb else pl.BlockSpec((k, tn), lambda j, i: (0, j))
    return _call(body, name=name, grid=(n // tn, m // tm), in_specs=[a_spec, b_spec],
                 out_specs=pl.BlockSpec((tm, tn), lambda j, i: (i, j)),
                 out_shape=_sds((m, n), out_dtype), compiler_params=_cparams())(a, b)


def mm_wk(a, gw, off, r, name, jb=N_DEV, tm=MM_TILE, tn=MM_TILE):
    m = a.shape[0]
    d = gw.shape[2]
    tm, tn = min(tm, m), min(tn, d)
    nk = N_DEV // jb
    ob = off // r
    assert off % r == 0 and a.shape[1] == N_DEV * r

    def body(a_ref, b_ref, o_ref, *acc):
        av = a_ref[...].astype(BF16)
        p = _dot(av[:, 0:r], b_ref[0])
        for q in range(1, jb):
            p = p + _dot(av[:, q * r:(q + 1) * r], b_ref[q])
        if nk == 1:
            o_ref[...] = p
        else:
            kk = pl.program_id(2)

            @pl.when(kk == 0)
            def _():
                acc[0][...] = p

            @pl.when(kk > 0)
            def _():
                acc[0][...] += p

            @pl.when(kk == nk - 1)
            def _():
                o_ref[...] = acc[0][...]

    return _call(body, name=name, grid=(m // tm, d // tn, nk),
                 in_specs=[pl.BlockSpec((tm, jb * r), lambda i, j, k: (i, k)),
                           pl.BlockSpec((jb, r, tn), lambda i, j, k: (k, ob, j))],
                 out_specs=pl.BlockSpec((tm, tn), lambda i, j, k: (i, j)),
                 out_shape=_sds((m, d)),
                 scratch_shapes=([pltpu.VMEM((tm, tn), F32)] if nk > 1 else []),
                 compiler_params=_cparams())(a, gw)


def mm_wn(a, gw, off, r, name, relu_grad_of=None, out_dtype=F32, with_relu2=False, tm=MM_TILE):
    m, d = a.shape
    tm = min(tm, m)
    ob = off // r
    assert off % r == 0 and gw.shape[2] == d
    epi = relu_grad_of is not None

    def body(*refs):
        it = iter(refs)
        a_ref, b_ref = next(it), next(it)
        e_ref = next(it) if epi else None
        o_ref = next(it)
        p = _dot(a_ref[...], b_ref[...], NT)
        if epi:
            p = p * (2.0 * jnp.maximum(e_ref[...], 0.0))
        o_ref[...] = p.astype(out_dtype)
        if with_relu2:
            act_ref = next(it)
            act_ref[...] = jnp.square(jnp.maximum(p, 0.0)).astype(act_ref.dtype)

    blk = pl.BlockSpec((tm, r), lambda i, j: (i, j))
    in_specs = [pl.BlockSpec((tm, d), lambda i, j: (i, 0)), pl.BlockSpec((None, r, d), lambda i, j: (j, ob, 0))]
    ins = [a, gw]
    if epi:
        in_specs.append(blk)
        ins.append(relu_grad_of)
    out_shape = _sds((m, N_DEV * r), out_dtype)
    if with_relu2:
        return _call(body, name=name, grid=(m // tm, N_DEV), in_specs=in_specs, out_specs=(blk, blk),
                     out_shape=(out_shape, _sds((m, N_DEV * r), BF16)), compiler_params=_cparams())(*ins)
    return _call(body, name=name, grid=(m // tm, N_DEV), in_specs=in_specs, out_specs=blk,
                 out_shape=out_shape, compiler_params=_cparams())(*ins)


def blockdiag_dw(xt, dz, name):
    t = xt.shape[1]

    def body(a_ref, b_ref, o_ref):
        o_ref[...] = _dot(a_ref[...], b_ref[...])

    return _call(body, name=name, grid=(C_BLOCKS,),
                 in_specs=[pl.BlockSpec((C_BLOCK_DIM, t), lambda g: (g, 0)),
                           pl.BlockSpec((t, C_BLOCK_DIM), lambda g: (0, g))],
                 out_specs=pl.BlockSpec((None, C_BLOCK_DIM, C_BLOCK_DIM), lambda g: (g, 0, 0)),
                 out_shape=_sds((C_BLOCKS, C_BLOCK_DIM, C_BLOCK_DIM)))(xt, dz)


def _band_mask(n, nblk, transposed):
    shape = (3 * BLOCK, BLOCK) if transposed else (BLOCK, 3 * BLOCK)
    qi = lax.broadcasted_iota(jnp.int32, shape, 1 if transposed else 0)
    kj = lax.broadcasted_iota(jnp.int32, shape, 0 if transposed else 1)
    lo = jnp.where(n > 0, 0, BLOCK)
    hi = jnp.where(n < nblk - 1, 3 * BLOCK, 2 * BLOCK)
    return (jnp.abs(kj - BLOCK - qi) <= WINDOW) & (kj >= lo) & (kj < hi)


def _band_rows(ref, n, nblk):
    starts = [jnp.maximum(n - 1, 0), n, jnp.minimum(n + 1, nblk - 1)]
    return jnp.concatenate([ref[pl.ds(pl.multiple_of(s * BLOCK, BLOCK), BLOCK), :] for s in starts], axis=0)


def attn_fwd(q, k, v, bias, sink_b, name):
    t = q.shape[0]
    nblk = t // BLOCK
    scale = A_HEAD_DIM ** -0.5

    def body(q_ref, k_ref, v_ref, b_ref, s_ref, o_ref):
        n = pl.program_id(1)
        kb = _band_rows(k_ref, n, nblk).astype(BF16)
        vb = _band_rows(v_ref, n, nblk).astype(BF16)
        mask = _band_mask(n, nblk, False)
        for j in range(A_GROUP):
            sl = slice(j * A_HEAD_DIM, (j + 1) * A_HEAD_DIM)
            s = _dot(q_ref[:, sl], kb, NT) * scale + b_ref[j]
            s = jnp.where(mask, s, NEG_INF)
            sk = s_ref[j:j + 1, 0:1]
            mx = jnp.maximum(jnp.max(s, axis=1, keepdims=True), sk)
            p = jnp.exp(s - mx)
            den = jnp.sum(p, axis=1, keepdims=True) + jnp.exp(sk - mx)
            o_ref[:, sl] = _dot(p / den, vb)

    gw = A_GROUP * A_HEAD_DIM
    return _call(body, name=name, grid=(A_KV_HEADS, nblk),
                 in_specs=[pl.BlockSpec((BLOCK, gw), lambda g, n: (n, g)),
                           pl.BlockSpec((t, A_HEAD_DIM), lambda g, n: (0, g)),
                           pl.BlockSpec((t, A_HEAD_DIM), lambda g, n: (0, g)),
                           pl.BlockSpec((A_GROUP, BLOCK, 3 * BLOCK), lambda g, n: (g, 0, 0)),
                           pl.BlockSpec((None, 8, 128), lambda g, n: (g, 0, 0))],
                 out_specs=pl.BlockSpec((BLOCK, gw), lambda g, n: (n, g)),
                 out_shape=_sds((t, A_Q)))(q, k, v, bias, sink_b)


def attn_bwd(q, k, v, bias, bias_t, sink_b, do, o, name):
    t = q.shape[0]
    nblk = t // BLOCK
    scale = A_HEAD_DIM ** -0.5

    def body(q_ref, k_ref, v_ref, b_ref, bt_ref, s_ref, do_ref, o_ref, dq_ref, dk_ref, dv_ref, db_ref, ds_ref):
        n = pl.program_id(1)

        @pl.when(n == 0)
        def _():
            dk_ref[...] = jnp.zeros_like(dk_ref)
            dv_ref[...] = jnp.zeros_like(dv_ref)
            db_ref[...] = jnp.zeros_like(db_ref)
            ds_ref[...] = jnp.zeros_like(ds_ref)

        kb = _band_rows(k_ref, n, nblk).astype(BF16)
        vb = _band_rows(v_ref, n, nblk).astype(BF16)
        mask = _band_mask(n, nblk, False)
        mask_t = _band_mask(n, nblk, True)
        ones8 = jnp.ones((8, A_HEAD_DIM), F32)
        dkb = jnp.zeros((3 * BLOCK, A_HEAD_DIM), F32)
        dvb = jnp.zeros((3 * BLOCK, A_HEAD_DIM), F32)
        for j in range(A_GROUP):
            sl = slice(j * A_HEAD_DIM, (j + 1) * A_HEAD_DIM)
            qj = q_ref[:, sl].astype(BF16)
            doj = do_ref[:, sl]
            doo = doj * o_ref[:, sl]
            doj = doj.astype(BF16)
            sk = s_ref[j:j + 1, 0:1]
            s = jnp.where(mask, _dot(qj, kb, NT) * scale + b_ref[j], NEG_INF)
            mx = jnp.maximum(jnp.max(s, axis=1, keepdims=True), sk)
            p = jnp.exp(s - mx)
            den = jnp.sum(p, axis=1, keepdims=True) + jnp.exp(sk - mx)
            p = p / den
            psink = jnp.exp(sk - mx) / den
            delta = jnp.sum(doo, axis=1, keepdims=True)
            dsc = p * (_dot(doj, vb, NT) - delta)
            db_ref[j] += dsc
            ds_ref[j:j + 1, :] += jnp.broadcast_to(-jnp.sum(psink * delta, axis=0, keepdims=True), (1, 128))
            dq_ref[:, sl] = _dot(dsc, kb) * scale
            st = jnp.where(mask_t, _dot(kb, qj, NT) * scale + bt_ref[j], NEG_INF)
            mxt = jnp.maximum(jnp.max(st, axis=0, keepdims=True), sk)
            pt = jnp.exp(st - mxt)
            dent = jnp.sum(pt, axis=0, keepdims=True) + jnp.exp(sk - mxt)
            pt = pt / dent
            delta_t = _dot(ones8, doo, NT, hi=True)[0:1, :]
            dst = pt * (_dot(vb, doj, NT) - delta_t)
            dkb = dkb + _dot(dst, qj) * scale
            dvb = dvb + _dot(pt, doj)
        starts = [jnp.maximum(n - 1, 0), n, jnp.minimum(n + 1, nblk - 1)]
        for c, st_ in enumerate(starts):
            rows = pl.ds(pl.multiple_of(st_ * BLOCK, BLOCK), BLOCK)
            dk_ref[rows, :] += dkb[c * BLOCK:(c + 1) * BLOCK, :]
            dv_ref[rows, :] += dvb[c * BLOCK:(c + 1) * BLOCK, :]

    gw = A_GROUP * A_HEAD_DIM
    qspec = pl.BlockSpec((BLOCK, gw), lambda g, n: (n, g))
    kspec = pl.BlockSpec((t, A_HEAD_DIM), lambda g, n: (0, g))
    sspec = pl.BlockSpec((None, 8, 128), lambda g, n: (g, 0, 0))
    bspec = pl.BlockSpec((A_GROUP, BLOCK, 3 * BLOCK), lambda g, n: (g, 0, 0))
    btspec = pl.BlockSpec((A_GROUP, 3 * BLOCK, BLOCK), lambda g, n: (g, 0, 0))
    return _call(body, name=name, grid=(A_KV_HEADS, nblk),
                 in_specs=[qspec, kspec, kspec, bspec, btspec, sspec, qspec, qspec],
                 out_specs=(qspec, kspec, kspec, bspec, sspec),
                 out_shape=(_sds((t, A_Q)), _sds((t, A_KV)), _sds((t, A_KV)),
                            _sds((A_HEADS, BLOCK, 3 * BLOCK)), _sds((A_KV_HEADS, 8, 128))),
                 compiler_params=_cparams())(q, k, v, bias, bias_t, sink_b, do, o)


def xattn_fwd(q, k, v, name):
    t, d = q.shape
    ml = k.shape[0]
    dh = d // X_HEADS
    tq = min(256, t)
    scale = dh ** -0.5

    def body(q_ref, k_ref, v_ref, o_ref):
        s = _dot(q_ref[...], k_ref[...], NT) * scale
        p = jnp.exp(s - jnp.max(s, axis=1, keepdims=True))
        p = p / jnp.sum(p, axis=1, keepdims=True)
        o_ref[...] = _dot(p, v_ref[...])

    qspec = pl.BlockSpec((tq, dh), lambda h, i: (i, h))
    kspec = pl.BlockSpec((ml, dh), lambda h, i: (0, h))
    return _call(body, name=name, grid=(X_HEADS, t // tq), in_specs=[qspec, kspec, kspec], out_specs=qspec,
                 out_shape=_sds((t, d)))(q, k, v)


def xattn_bwd(q, k, v, o, do, name):
    t, d = q.shape
    ml = k.shape[0]
    dh = d // X_HEADS
    tq = min(256, t)
    scale = dh ** -0.5

    def body(q_ref, k_ref, v_ref, o_ref, do_ref, dq_ref, dk_ref, dv_ref):
        i = pl.program_id(1)
        qv, kv, vv = q_ref[...].astype(BF16), k_ref[...].astype(BF16), v_ref[...].astype(BF16)
        dov = do_ref[...]
        doo = dov * o_ref[...]
        dov = dov.astype(BF16)
        s = _dot(qv, kv, NT) * scale
        p = jnp.exp(s - jnp.max(s, axis=1, keepdims=True))
        p = p / jnp.sum(p, axis=1, keepdims=True)
        ds = p * (_dot(dov, vv, NT) - jnp.sum(doo, axis=1, keepdims=True))
        dq_ref[...] = _dot(ds, kv) * scale
        st = _dot(kv, qv, NT) * scale
        pt = jnp.exp(st - jnp.max(st, axis=0, keepdims=True))
        pt = pt / jnp.sum(pt, axis=0, keepdims=True)
        delta_t = _dot(jnp.ones((8, dh), F32), doo, NT, hi=True)[0:1, :]
        dst = pt * (_dot(vv, dov, NT) - delta_t)
        dkp = _dot(dst, qv) * scale
        dvp = _dot(pt, dov)

        @pl.when(i == 0)
        def _():
            dk_ref[...] = dkp
            dv_ref[...] = dvp

        @pl.when(i > 0)
        def _():
            dk_ref[...] += dkp
            dv_ref[...] += dvp

    qspec = pl.BlockSpec((tq, dh), lambda h, i: (i, h))
    kspec = pl.BlockSpec((ml, dh), lambda h, i: (0, h))
    return _call(body, name=name, grid=(X_HEADS, t // tq), in_specs=[qspec, kspec, kspec, qspec, qspec],
                 out_specs=(qspec, kspec, kspec),
                 out_shape=(_sds((t, d)), _sds((ml, d)), _sds((ml, d))))(q, k, v, o, do)


def scan_lead(a, u, name, reverse, inclusive):
    n, r, c = a.shape
    blk = max(1, min(n, (1 << 18) // (max(r, 8) * c)))
    while n % blk:
        blk -= 1
    nb = n // blk

    def body(a_ref, u_ref, o_ref, carry):
        @pl.when(pl.program_id(0) == 0)
        def _():
            carry[...] = jnp.zeros_like(carry)

        def step(s, h):
            idx = (blk - 1 - s) if reverse else s
            hn = a_ref[idx] * h + u_ref[idx]
            o_ref[idx] = hn if inclusive else h
            return hn

        carry[...] = lax.fori_loop(0, blk, step, carry[...])

    spec = pl.BlockSpec((blk, r, c), (lambda i: (nb - 1 - i, 0, 0)) if reverse else (lambda i: (i, 0, 0)))
    return _call(body, name=name, grid=(nb,), in_specs=[spec, spec], out_specs=spec,
                 out_shape=_sds((n, r, c)), scratch_shapes=[pltpu.VMEM((r, c), F32)])(a, u)


def _chunk_mats(bwd_dir):
    i = lax.broadcasted_iota(jnp.int32, (GLA_TILE, GLA_TILE), 0)
    j = lax.broadcasted_iota(jnp.int32, (GLA_TILE, GLA_TILE), 1)
    same = lax.shift_right_logical(i, 4) == lax.shift_right_logical(j, 4)
    if bwd_dir:
        cm, cm_t = same & (j >= i), same & (i >= j)
        mk, mk_t = same & (j > i), same & (i > j)
    else:
        cm, cm_t = same & (j <= i), same & (i <= j)
        mk, mk_t = same & (j <= i), same & (i <= j)
    f = lambda b: jnp.where(b, 1.0, 0.0).astype(F32)
    return f(cm), f(cm_t), mk, mk_t, f(same)


def gla_gates_fwd(zf, zb, w2f, b2f, w2b, b2b, name):
    t = zf.shape[0]
    tm = min(256, t)

    def body(zf_ref, zb_ref, wf_ref, bf_ref, wb_ref, bb_ref, lf_ref, lb_ref):
        lf_ref[...] = -_softplus(-(_dot(zf_ref[...], wf_ref[...], hi=True) + bf_ref[...])) / GATE_TAU
        lb_ref[...] = -_softplus(-(_dot(zb_ref[...], wb_ref[...], hi=True) + bb_ref[...])) / GATE_TAU

    zs = pl.BlockSpec((tm, GATE_RANK), lambda i: (i, 0))
    ws = pl.BlockSpec((GATE_RANK, B_QK), lambda i: (0, 0))
    bs = pl.BlockSpec((1, B_QK), lambda i: (0, 0))
    os_ = pl.BlockSpec((tm, B_QK), lambda i: (i, 0))
    return _call(body, name=name, grid=(t // tm,), in_specs=[zs, zs, ws, bs, ws, bs], out_specs=(os_, os_),
                 out_shape=(_sds((t, B_QK)),) * 2)(zf, zb, w2f, b2f.reshape(1, B_QK), w2b, b2b.reshape(1, B_QK))


def gla_gates_bwd(zf, zb, w2f, b2f, w2b, b2b, dlf, dlb, name):
    t = zf.shape[0]
    tm = min(256, t)

    def body(zf_ref, zb_ref, wf_ref, bf_ref, wb_ref, bb_ref, dlf_ref, dlb_ref,
             dzf_ref, dzb_ref, dpf_ref, dpb_ref, dbf_ref, dbb_ref):
        first = pl.program_id(0) == 0
        for z_ref, w_ref, b_ref, dl_ref, dz_ref, dp_ref, db_ref in (
                (zf_ref, wf_ref, bf_ref, dlf_ref, dzf_ref, dpf_ref, dbf_ref),
                (zb_ref, wb_ref, bb_ref, dlb_ref, dzb_ref, dpb_ref, dbb_ref)):
            pre = _dot(z_ref[...], w_ref[...], hi=True) + b_ref[...]
            dpre = dl_ref[...] * (1.0 / GATE_TAU) * _sigmoid(-pre)
            dp_ref[...] = dpre
            dz_ref[...] = _dot(dpre, w_ref[...], NT, hi=True)
            part = jnp.sum(dpre, axis=0, keepdims=True)

            @pl.when(first)
            def _():
                db_ref[...] = part

            @pl.when(jnp.logical_not(first))
            def _():
                db_ref[...] += part

    zs = pl.BlockSpec((tm, GATE_RANK), lambda i: (i, 0))
    ws = pl.BlockSpec((GATE_RANK, B_QK), lambda i: (0, 0))
    bs = pl.BlockSpec((1, B_QK), lambda i: (0, 0))
    os_ = pl.BlockSpec((tm, B_QK), lambda i: (i, 0))
    return _call(body, name=name, grid=(t // tm,), in_specs=[zs, zs, ws, bs, ws, bs, os_, os_],
                 out_specs=(zs, zs, os_, os_, bs, bs),
                 out_shape=(_sds((t, GATE_RANK)),) * 2 + (_sds((t, B_QK)),) * 2 + (_sds((1, B_QK)),) * 2)(
        zf, zb, w2f, b2f.reshape(1, B_QK), w2b, b2b.reshape(1, B_QK), dlf, dlb)


def gla_outer(xt, lat, y, name, bwd_dir, mode):
    t = y.shape[0]
    nchunk = t // GLA_CHUNK
    khat = mode == "khat"
    scale = B_KEY_DIM ** -0.5

    def body(xt_ref, lat_ref, y_ref, *outs):
        _, cm_t, _, _, same = _chunk_mats(bwd_dir)
        lat_v = lat_ref[...]
        bt = _dot(lat_v, cm_t, hi=True)
        if khat:
            mult = jnp.exp(_dot(lat_v, same, hi=True) - bt)
        else:
            mult = jnp.exp(bt) * scale
        xm = xt_ref[...] * mult
        lane = lax.shift_right_logical(lax.broadcasted_iota(jnp.int32, (1, GLA_TILE), 1), 4)
        ones = jnp.ones((GLA_TILE, B_VAL_DIM), F32)
        yv = [y_ref[:, h * B_VAL_DIM:(h + 1) * B_VAL_DIM].astype(BF16) for h in range(B_HEADS)]
        for c in range(CHUNKS_PER_TILE):
            sel = lane == c
            xc = jnp.where(sel, xm, 0.0).astype(BF16)
            for h in range(B_HEADS):
                rows = slice(h * B_KEY_DIM, (h + 1) * B_KEY_DIM)
                outs[0][c, rows, :] = _dot(xc[rows, :], yv[h])
            if khat:
                outs[1][c] = jnp.exp(_dot(jnp.where(sel, lat_v, 0.0), ones, hi=True))

    tspec = pl.BlockSpec((B_QK, GLA_TILE), lambda i: (0, i))
    ospec = pl.BlockSpec((CHUNKS_PER_TILE, B_QK, B_VAL_DIM), lambda i: (i, 0, 0))
    oshape = _sds((nchunk, B_QK, B_VAL_DIM))
    return _call(body, name=name, grid=(t // GLA_TILE,),
                 in_specs=[tspec, tspec, pl.BlockSpec((GLA_TILE, B_V), lambda i: (i, 0))],
                 out_specs=(ospec, ospec) if khat else ospec,
                 out_shape=(oshape, oshape) if khat else oshape)(xt, lat, y)


def _head_lane_mask(h):
    lane = lax.broadcasted_iota(jnp.int32, (1, B_QK), 1)
    return lax.shift_right_logical(lane, 6) == h


def _chunk_rows(c):
    return slice(c * GLA_CHUNK, (c + 1) * GLA_CHUNK)


def gla_inner_fwd(q, k, v, la, sp, name, bwd_dir):
    t = q.shape[0]
    scale = B_KEY_DIM ** -0.5

    def body(q_ref, k_ref, v_ref, la_ref, sp_ref, o_ref):
        cm, _, mk, _, _ = _chunk_mats(bwd_dir)
        b = _dot(cm, la_ref[...], hi=True)
        qt = q_ref[...] * scale * jnp.exp(b)
        kt = k_ref[...] * jnp.exp(jnp.minimum(-b, EXP_CLAMP))
        spb = [sp_ref[c].astype(BF16) for c in range(CHUNKS_PER_TILE)]
        for h in range(B_HEADS):
            lm = _head_lane_mask(h)
            qm = jnp.where(lm, qt, 0.0).astype(BF16)
            km = jnp.where(lm, kt, 0.0).astype(BF16)
            vs = slice(h * B_VAL_DIM, (h + 1) * B_VAL_DIM)
            att = jnp.where(mk, _dot(qm, km, NT), 0.0)
            inter = jnp.concatenate([_dot(qm[_chunk_rows(c), :], spb[c]) for c in range(CHUNKS_PER_TILE)], axis=0)
            o_ref[:, vs] = _dot(att, v_ref[:, vs]) + inter

    qs = pl.BlockSpec((GLA_TILE, B_QK), lambda i: (i, 0))
    vs_ = pl.BlockSpec((GLA_TILE, B_V), lambda i: (i, 0))
    ss = pl.BlockSpec((CHUNKS_PER_TILE, B_QK, B_VAL_DIM), lambda i: (i, 0, 0))
    return _call(body, name=name, grid=(t // GLA_TILE,), in_specs=[qs, qs, vs_, qs, ss], out_specs=vs_,
                 out_shape=_sds((t, B_V)))(q, k, v, la, sp)


def gla_inner_bwd(q, k, v, la, do, sp, gs, dec, name, bwd_dir, add=None):
    t = q.shape[0]
    scale = B_KEY_DIM ** -0.5
    hasadd = add is not None

    def body(*refs):
        it = iter(refs)
        q_ref, k_ref, v_ref, la_ref, do_ref, sp_ref, gs_ref, dec_ref = [next(it) for _ in range(8)]
        adds = [next(it) for _ in range(3)] if hasadd else None
        dq_ref, dk_ref, dv_ref, dla_ref = [next(it) for _ in range(4)]
        cm, cm_t, mk, mk_t, same = _chunk_mats(bwd_dir)
        la_v = la_ref[...]
        b = _dot(cm, la_v, hi=True)
        btot = _dot(same, la_v, hi=True)
        eb = jnp.exp(b)
        ek = jnp.exp(jnp.minimum(-b, EXP_CLAMP))
        ekh = jnp.exp(btot - b)
        qt = q_ref[...] * scale * eb
        kt = k_ref[...] * ek
        kh = k_ref[...] * ekh
        spb = [sp_ref[c].astype(BF16) for c in range(CHUNKS_PER_TILE)]
        gsb = [gs_ref[c].astype(BF16) for c in range(CHUNKS_PER_TILE)]
        dqt = jnp.zeros((GLA_TILE, B_QK), F32)
        dkt = jnp.zeros((GLA_TILE, B_QK), F32)
        dkh = jnp.zeros((GLA_TILE, B_QK), F32)
        for h in range(B_HEADS):
            lm = _head_lane_mask(h)
            qm = jnp.where(lm, qt, 0.0).astype(BF16)
            km = jnp.where(lm, kt, 0.0).astype(BF16)
            khm = jnp.where(lm, kh, 0.0).astype(BF16)
            vs = slice(h * B_VAL_DIM, (h + 1) * B_VAL_DIM)
            vh = v_ref[:, vs].astype(BF16)
            doh = do_ref[:, vs].astype(BF16)
            da = jnp.where(mk, _dot(doh, vh, NT), 0.0)
            da_t = jnp.where(mk_t, _dot(vh, doh, NT), 0.0)
            att_t = jnp.where(mk_t, _dot(km, qm, NT), 0.0)
            dv_h = _dot(att_t, doh) + jnp.concatenate(
                [_dot(khm[_chunk_rows(c), :], gsb[c]) for c in range(CHUNKS_PER_TILE)], axis=0)
            if hasadd:
                dv_h = dv_h + adds[2][:, vs]
            dv_ref[:, vs] = dv_h
            dq_inter = jnp.concatenate(
                [_dot(doh[_chunk_rows(c), :], spb[c], NT) for c in range(CHUNKS_PER_TILE)], axis=0)
            dqt = dqt + _dot(da, km) + jnp.where(lm, dq_inter, 0.0)
            dkt = dkt + _dot(da_t, qm)
            dkh_inter = jnp.concatenate(
                [_dot(vh[_chunk_rows(c), :], gsb[c], NT) for c in range(CHUNKS_PER_TILE)], axis=0)
            dkh = dkh + jnp.where(lm, dkh_inter, 0.0)
        dq = dqt * scale * eb
        dk = dkt * ek + dkh * ekh
        if hasadd:
            dq = dq + adds[0][...]
            dk = dk + adds[1][...]
        dq_ref[...] = dq
        dk_ref[...] = dk
        db = dqt * qt - dkt * kt - dkh * kh
        ones16 = jnp.ones((GLA_CHUNK, B_VAL_DIM), F32)
        t2 = jnp.concatenate(
            [_dot(ones16, gs_ref[c] * dec_ref[c] * sp_ref[c], NT, hi=True) for c in range(CHUNKS_PER_TILE)], axis=0)
        dla_ref[...] = _dot(cm_t, db, hi=True) + _dot(same, dkh * kh, hi=True) + t2

    qs = pl.BlockSpec((GLA_TILE, B_QK), lambda i: (i, 0))
    vs_ = pl.BlockSpec((GLA_TILE, B_V), lambda i: (i, 0))
    ss = pl.BlockSpec((CHUNKS_PER_TILE, B_QK, B_VAL_DIM), lambda i: (i, 0, 0))
    ins = [q, k, v, la, do, sp, gs, dec] + (list(add) if hasadd else [])
    in_specs = [qs, qs, vs_, qs, vs_, ss, ss, ss] + ([qs, qs, vs_] if hasadd else [])
    return _call(body, name=name, grid=(t // GLA_TILE,), in_specs=in_specs, out_specs=(qs, qs, vs_, qs),
                 out_shape=(_sds((t, B_QK)), _sds((t, B_QK)), _sds((t, B_V)), _sds((t, B_QK))),
                 compiler_params=_cparams())(*ins)


def gla_out_fwd(of, ob, g, gn, name):
    t = of.shape[0]
    tm = min(256, t)

    def body(of_ref, ob_ref, g_ref, gn_ref, o_ref):
        for h in range(B_HEADS):
            vs = slice(h * B_VAL_DIM, (h + 1) * B_VAL_DIM)
            o = of_ref[:, vs] + ob_ref[:, vs]
            on = o * lax.rsqrt(jnp.mean(o * o, axis=1, keepdims=True) + EPS)
            gv = g_ref[:, vs]
            o_ref[:, vs] = on * gn_ref[:, vs] * (gv * _sigmoid(gv))

    row = pl.BlockSpec((tm, B_V), lambda i: (i, 0))
    vec = pl.BlockSpec((1, B_V), lambda i: (0, 0))
    return _call(body, name=name, grid=(t // tm,), in_specs=[row, row, row, vec], out_specs=row,
                 out_shape=_sds((t, B_V)))(of, ob, g, gn.reshape(1, B_V))


def gla_out_bwd(of, ob, g, gn, dout, name):
    t = of.shape[0]
    tm = min(256, t)

    def body(of_ref, ob_ref, g_ref, gn_ref, d_ref, do_ref, dg_ref, dgn_ref):
        first = pl.program_id(0) == 0
        for h in range(B_HEADS):
            vs = slice(h * B_VAL_DIM, (h + 1) * B_VAL_DIM)
            o = of_ref[:, vs] + ob_ref[:, vs]
            r = lax.rsqrt(jnp.mean(o * o, axis=1, keepdims=True) + EPS)
            on = o * r
            gv = g_ref[:, vs]
            sg = _sigmoid(gv)
            silu = gv * sg
            dv = d_ref[:, vs]
            gnv = gn_ref[:, vs]
            dg_ref[:, vs] = dv * on * gnv * (sg * (1.0 + gv * (1.0 - sg)))
            don = dv * silu * gnv
            do_ref[:, vs] = r * (don - on * jnp.mean(don * on, axis=1, keepdims=True))
            part = jnp.sum(dv * silu * on, axis=0, keepdims=True)

            @pl.when(first)
            def _():
                dgn_ref[:, vs] = part

            @pl.when(jnp.logical_not(first))
            def _():
                dgn_ref[:, vs] += part

    row = pl.BlockSpec((tm, B_V), lambda i: (i, 0))
    vec = pl.BlockSpec((1, B_V), lambda i: (0, 0))
    return _call(body, name=name, grid=(t // tm,), in_specs=[row, row, row, vec, row], out_specs=(row, row, vec),
                 out_shape=(_sds((t, B_V)), _sds((t, B_V)), _sds((1, B_V))))(of, ob, g, gn.reshape(1, B_V), dout)


def _shift(x, k):
    if k > 0:
        return jnp.concatenate([x[k:], jnp.zeros((k,) + x.shape[1:], x.dtype)], axis=0)
    return jnp.concatenate([jnp.zeros((-k,) + x.shape[1:], x.dtype), x[:k]], axis=0)


def _lru_gates(xc, s, wa_ref, ba_ref, wx_ref, bx_ref, lam_ref):
    cols = [slice(g * C_BLOCK_DIM, (g + 1) * C_BLOCK_DIM) for g in range(C_BLOCKS)]
    zr = jnp.concatenate([_dot(xc[:, cs], wa_ref[s, g]) for g, cs in enumerate(cols)], axis=1) + ba_ref[s:s + 1, :]
    zi = jnp.concatenate([_dot(xc[:, cs], wx_ref[s, g]) for g, cs in enumerate(cols)], axis=1) + bx_ref[s:s + 1, :]
    r = _sigmoid(zr)
    i = _sigmoid(zi)
    sp = _softplus(-lam_ref[s:s + 1, :])
    log_a = -LRU_C * r * sp
    return r, i, sp, log_a


def lru_gates_fwd(x0, xm2, xm1, xp1, cw, cb, wa, ba, wx, bx, lam, name):
    t = x0.shape[0]
    tm = min(256, t)

    def body(x0_ref, xm2_ref, xm1_ref, xp1_ref, cw_ref, cb_ref, wa_ref, ba_ref, wx_ref, bx_ref, lam_ref,
             xc_ref, a0_ref, u0_ref, a1_ref, u1_ref):
        xc = (xm2_ref[...] * cw_ref[0:1, :] + xm1_ref[...] * cw_ref[1:2, :] + x0_ref[...] * cw_ref[2:3, :]
              + xp1_ref[...] * cw_ref[3:4, :] + cb_ref[...])
        xc_ref[...] = xc
        for s, (a_ref, u_ref) in enumerate(((a0_ref, u0_ref), (a1_ref, u1_ref))):
            _, i, _, log_a = _lru_gates(xc, s, wa_ref, ba_ref, wx_ref, bx_ref, lam_ref)
            a_ref[...] = jnp.exp(log_a)
            u_ref[...] = jnp.sqrt(-_expm1(2.0 * log_a)) * (i * xc)

    row = pl.BlockSpec((tm, C_WIDTH), lambda i: (i, 0))
    full = lambda shape: pl.BlockSpec(shape, lambda i: (0,) * len(shape))
    wshape = (2, C_BLOCKS, C_BLOCK_DIM, C_BLOCK_DIM)
    return _call(body, name=name, grid=(t // tm,),
                 in_specs=[row] * 4 + [full((4, C_WIDTH)), full((1, C_WIDTH)), full(wshape), full((2, C_WIDTH)),
                                       full(wshape), full((2, C_WIDTH)), full((2, C_WIDTH))],
                 out_specs=(row,) * 5, out_shape=(_sds((t, C_WIDTH)),) * 5)(
        x0, xm2, xm1, xp1, cw, cb.reshape(1, C_WIDTH), wa, ba, wx, bx, lam)


def lru_gates_bwd(xc, g0, hs0, g1, hs1, wa, ba, wx, bx, lam, name):
    t = xc.shape[0]
    tm = min(256, t)

    def body(xc_ref, g0_ref, hs0_ref, g1_ref, hs1_ref, wa_ref, ba_ref, wx_ref, bx_ref, lam_ref,
             dxc_ref, dzr0_ref, dzi0_ref, dzr1_ref, dzi1_ref, dlam_ref, dba_ref, dbx_ref):
        first = pl.program_id(0) == 0

        @pl.when(first)
        def _():
            dlam_ref[...] = jnp.zeros_like(dlam_ref)
            dba_ref[...] = jnp.zeros_like(dba_ref)
            dbx_ref[...] = jnp.zeros_like(dbx_ref)

        xcv = xc_ref[...]
        dxc = jnp.zeros_like(xcv)
        cols = [slice(g * C_BLOCK_DIM, (g + 1) * C_BLOCK_DIM) for g in range(C_BLOCKS)]
        for s, (g_ref, hs_ref, dzr_ref, dzi_ref) in enumerate(
                ((g0_ref, hs0_ref, dzr0_ref, dzi0_ref), (g1_ref, hs1_ref, dzr1_ref, dzi1_ref))):
            r, i, sp, log_a = _lru_gates(xcv, s, wa_ref, ba_ref, wx_ref, bx_ref, lam_ref)
            du = g_ref[...]
            da = du * hs_ref[...]
            a = jnp.exp(log_a)
            e2 = jnp.exp(2.0 * log_a)
            c = jnp.sqrt(-_expm1(2.0 * log_a))
            ix = i * xcv
            dlog = da * a - du * ix * (e2 / c)
            dix = du * c
            dxc = dxc + dix * i
            dzi = dix * xcv * i * (1.0 - i)
            dzr = dlog * (-LRU_C * sp) * r * (1.0 - r)
            dzr_ref[...] = dzr
            dzi_ref[...] = dzi
            dxc = dxc + jnp.concatenate(
                [_dot(dzr[:, cs], wa_ref[s, g], NT) + _dot(dzi[:, cs], wx_ref[s, g], NT) for g, cs in enumerate(cols)],
                axis=1)
            dsp = jnp.sum(dlog * (-LRU_C * r), axis=0, keepdims=True)
            dlam_ref[s:s + 1, :] += dsp * (-_sigmoid(-lam_ref[s:s + 1, :]))
            dba_ref[s:s + 1, :] += jnp.sum(dzr, axis=0, keepdims=True)
            dbx_ref[s:s + 1, :] += jnp.sum(dzi, axis=0, keepdims=True)
        dxc_ref[...] = dxc

    row = pl.BlockSpec((tm, C_WIDTH), lambda i: (i, 0))
    full = lambda shape: pl.BlockSpec(shape, lambda i: (0,) * len(shape))
    wshape = (2, C_BLOCKS, C_BLOCK_DIM, C_BLOCK_DIM)
    vec2 = full((2, C_WIDTH))
    return _call(body, name=name, grid=(t // tm,),
                 in_specs=[row] * 5 + [full(wshape), vec2, full(wshape), vec2, vec2],
                 out_specs=(row,) * 5 + (vec2,) * 3,
                 out_shape=(_sds((t, C_WIDTH)),) * 5 + (_sds((2, C_WIDTH)),) * 3)(
        xc, g0, hs0, g1, hs1, wa, ba, wx, bx, lam)


def lru_out_fwd(h0, h1, y, name):
    t = y.shape[0]
    tm = min(256, t)

    def body(h0_ref, h1_ref, y_ref, o_ref):
        o_ref[...] = (h0_ref[...] + h1_ref[...]) * _gelu(y_ref[...])

    row = pl.BlockSpec((tm, C_WIDTH), lambda i: (i, 0))
    return _call(body, name=name, grid=(t // tm,), in_specs=[row] * 3, out_specs=row,
                 out_shape=_sds((t, C_WIDTH)))(h0, h1, y)


def lru_out_bwd(h0, h1, y, dout, name):
    t = y.shape[0]
    tm = min(256, t)

    def body(h0_ref, h1_ref, y_ref, d_ref, dh_ref, dy_ref):
        yv = y_ref[...]
        dv = d_ref[...]
        dh_ref[...] = dv * _gelu(yv)
        dy_ref[...] = dv * (h0_ref[...] + h1_ref[...]) * _gelu_grad(yv)

    row = pl.BlockSpec((tm, C_WIDTH), lambda i: (i, 0))
    return _call(body, name=name, grid=(t // tm,), in_specs=[row] * 4, out_specs=(row, row),
                 out_shape=(_sds((t, C_WIDTH)),) * 2)(h0, h1, y, dout)


def conv_bwd(dxc, dp2, dp1, dm1, x0, xm2, xm1, xp1, cw, name):
    t = x0.shape[0]
    tm = min(256, t)

    def body(d_ref, dp2_ref, dp1_ref, dm1_ref, x0_ref, xm2_ref, xm1_ref, xp1_ref, cw_ref, dx_ref, dcw_ref, dcb_ref):
        @pl.when(pl.program_id(0) == 0)
        def _():
            dcw_ref[...] = jnp.zeros_like(dcw_ref)
            dcb_ref[...] = jnp.zeros_like(dcb_ref)

        dv = d_ref[...]
        dx_ref[...] = (dp2_ref[...] * cw_ref[0:1, :] + dp1_ref[...] * cw_ref[1:2, :] + dv * cw_ref[2:3, :]
                       + dm1_ref[...] * cw_ref[3:4, :])
        for j, x_ref in enumerate((xm2_ref, xm1_ref, x0_ref, xp1_ref)):
            dcw_ref[j:j + 1, :] += jnp.sum(dv * x_ref[...], axis=0, keepdims=True)
        dcb_ref[...] += jnp.sum(dv, axis=0, keepdims=True)

    row = pl.BlockSpec((tm, C_WIDTH), lambda i: (i, 0))
    cws = pl.BlockSpec((4, C_WIDTH), lambda i: (0, 0))
    cbs = pl.BlockSpec((1, C_WIDTH), lambda i: (0, 0))
    return _call(body, name=name, grid=(t // tm,), in_specs=[row] * 8 + [cws], out_specs=(row, cws, cbs),
                 out_shape=(_sds((t, C_WIDTH)), _sds((4, C_WIDTH)), _sds((1, C_WIDTH))))(
        dxc, dp2, dp1, dm1, x0, xm2, xm1, xp1, cw)


def _my_place():
    return lax.axis_index("x"), lax.axis_index("y"), lax.axis_index("c")


def all_gather(xs, name):
    r, c = xs.shape

    def body(x_ref, out_ref, send_sems, recv_sems, local_sem):
        x, y, cc = _my_place()
        me, sibling = (x, y, cc), (x, y, 1 - cc)
        chips = [(1 - x, y), (x, 1 - y), (1 - x, 1 - y)]

        def slot(px, py, pc):
            return out_ref.at[4 * px + 2 * py + pc]

        def copy(k, block, to, src=None):
            return pltpu.make_async_remote_copy(
                src_ref=slot(*block) if src is None else src, dst_ref=slot(*block),
                send_sem=send_sems.at[k], recv_sem=recv_sems.at[k], device_id=to, device_id_type=MESH)

        mine = pltpu.make_async_copy(x_ref, slot(*me), local_sem)
        mine.start()
        first = [copy(0, me, sibling, src=x_ref)]
        first += [copy(1 + j, me, (*chip, cc), src=x_ref) for j, chip in enumerate(chips)]
        for cp in first:
            cp.start()
        passed = [copy(4 + j, (*chip, cc), sibling) for j, chip in enumerate(chips)]
        for j, chip in enumerate(chips):
            copy(1 + j, (*chip, cc), me).wait_recv()
            passed[j].start()
        copy(0, sibling, me).wait_recv()
        for j, chip in enumerate(chips):
            copy(4 + j, (*chip, 1 - cc), me).wait_recv()
        for cp in first + passed:
            cp.wait_send()
        mine.wait()

    return _call(body, name=name, in_specs=[pl.BlockSpec(memory_space=pl.ANY)],
                 out_specs=pl.BlockSpec(memory_space=pl.ANY), out_shape=_sds((N_DEV, r, c), xs.dtype),
                 scratch_shapes=[pltpu.SemaphoreType.DMA((7,)), pltpu.SemaphoreType.DMA((7,)),
                                 pltpu.SemaphoreType.DMA])(xs)


def _stream_rows(r):
    nch = SIBLING_STREAMS // 4 if r % (8 * (SIBLING_STREAMS // 4)) == 0 else 1
    return nch, r // nch


def exchange_sibling(gw, name):
    _, r, c = gw.shape
    g5 = gw.reshape(4, 2, r, c)
    nch, rows = _stream_rows(r)

    def body(g_ref, out_ref, send_sems, recv_sems):
        x, y, cc = _my_place()
        swaps = []
        for q in range(4):
            for s in range(nch):
                k = q * nch + s
                win = pl.ds(s * rows, rows)
                swaps.append(pltpu.make_async_remote_copy(
                    src_ref=g_ref.at[q, 1 - cc, win], dst_ref=out_ref.at[q, win], send_sem=send_sems.at[k],
                    recv_sem=recv_sems.at[k], device_id=(x, y, 1 - cc), device_id_type=MESH))
        for cp in swaps:
            cp.start()
        for cp in swaps:
            cp.wait()

    nsem = 4 * nch
    return _call(body, name=name, in_specs=[pl.BlockSpec(memory_space=pl.ANY)],
                 out_specs=pl.BlockSpec(memory_space=pl.ANY), out_shape=_sds((4, r, c), gw.dtype),
                 scratch_shapes=[pltpu.SemaphoreType.DMA((nsem,)), pltpu.SemaphoreType.DMA((nsem,))])(g5)


HBM_SPEC = pl.BlockSpec(memory_space=pltpu.HBM)
SEM_SPEC = pl.BlockSpec(memory_space=pltpu.SEMAPHORE)
DATAFLOW = pltpu.SideEffectType.DATAFLOW_SIDE_EFFECTING


def _hbm(a):
    return pltpu.with_memory_space_constraint(a, pltpu.HBM)


def _peers(x, y, cc):
    return [(x, y, 1 - cc), (1 - x, y, cc), (x, 1 - y, cc), (1 - x, 1 - y, cc)]


def _slot(p):
    return 4 * p[0] + 2 * p[1] + p[2]


def gather_start(blk, name):
    r, c = blk.shape

    def body(v_ref, land_ref, send_sems, recv_sems, v_thru, land_thru, token):
        x, y, cc = _my_place()
        for k, to in enumerate(_peers(x, y, cc)):
            pltpu.make_async_remote_copy(
                src_ref=v_ref, dst_ref=land_ref.at[_slot((x, y, cc))], send_sem=send_sems.at[k],
                recv_sem=recv_sems.at[k], device_id=to, device_id_type=MESH).start()
        token[...] = jnp.zeros_like(token)

    return _call(
        body, name=name,
        out_shape=(pltpu.SemaphoreType.DMA((4,)), pltpu.SemaphoreType.DMA((4,)), pltpu.HBM((r, c), blk.dtype),
                   pltpu.HBM((N_DEV, r, c), blk.dtype), _sds((8, 128))),
        in_specs=(HBM_SPEC, HBM_SPEC),
        out_specs=(SEM_SPEC, SEM_SPEC, HBM_SPEC, HBM_SPEC, pl.BlockSpec(memory_space=pltpu.VMEM)),
        input_output_aliases={0: 2, 1: 3},
        compiler_params=pltpu.CompilerParams(has_side_effects=DATAFLOW),
    )(_hbm(blk), _hbm(lax.empty((N_DEV, r, c), blk.dtype)))


def gather_wait(send_sems, recv_sems, v_thru, land_thru, after, name):
    def body(v_ref, land_ref, send_sems, recv_sems, after_ref, v_out, land_out):
        x, y, cc = _my_place()
        for k, peer in enumerate(_peers(x, y, cc)):
            cp = pltpu.make_async_remote_copy(
                src_ref=v_ref, dst_ref=land_ref.at[_slot(peer)], send_sem=send_sems.at[k], recv_sem=recv_sems.at[k],
                device_id=peer, device_id_type=MESH)
            cp.wait_send()
            cp.wait_recv()

    return _call(
        body, name=name,
        out_shape=(pltpu.HBM(v_thru.shape, v_thru.dtype), pltpu.HBM(land_thru.shape, land_thru.dtype)),
        in_specs=(HBM_SPEC, HBM_SPEC, SEM_SPEC, SEM_SPEC, pl.BlockSpec(memory_space=pl.ANY)),
        out_specs=(HBM_SPEC, HBM_SPEC), input_output_aliases={0: 0, 1: 1},
        compiler_params=pltpu.CompilerParams(has_side_effects=DATAFLOW),
    )(v_thru, land_thru, send_sems, recv_sems, after)


def gather_pass(land, name):
    _, r, c = land.shape
    nch, rows = _stream_rows(r)

    def body(land_ref, out_ref, send_sems, recv_sems):
        x, y, cc = _my_place()
        peers = _peers(x, y, cc)
        copies = []
        for j in range(3):
            mine, theirs = _slot(peers[1 + j]), _slot((peers[1 + j][0], peers[1 + j][1], 1 - cc))
            for s in range(nch):
                k = j * nch + s
                win = pl.ds(s * rows, rows)
                send = pltpu.make_async_remote_copy(
                    src_ref=land_ref.at[mine, win], dst_ref=out_ref.at[mine, win], send_sem=send_sems.at[k],
                    recv_sem=recv_sems.at[k], device_id=peers[0], device_id_type=MESH)
                recv = pltpu.make_async_remote_copy(
                    src_ref=land_ref.at[mine, win], dst_ref=out_ref.at[theirs, win], send_sem=send_sems.at[k],
                    recv_sem=recv_sems.at[k], device_id=peers[0], device_id_type=MESH)
                copies.append((send, recv))
        for send, _ in copies:
            send.start()
        for send, recv in copies:
            send.wait_send()
            recv.wait_recv()

    nsem = 3 * nch
    return _call(body, name=name, in_specs=[pl.BlockSpec(memory_space=pl.ANY)],
                 out_specs=pl.BlockSpec(memory_space=pl.ANY), out_shape=_sds(land.shape, land.dtype),
                 input_output_aliases={0: 0},
                 scratch_shapes=[pltpu.SemaphoreType.DMA((nsem,)), pltpu.SemaphoreType.DMA((nsem,))])(land)


def chips_start(p, name):
    _, r, c = p.shape

    def body(p_ref, land_ref, send_sems, recv_sems, p_thru, land_thru, token):
        x, y, cc = _my_place()
        for j, (px, py, pc) in enumerate(_peers(x, y, cc)[1:]):
            pltpu.make_async_remote_copy(
                src_ref=p_ref.at[2 * px + py], dst_ref=land_ref.at[j], send_sem=send_sems.at[j],
                recv_sem=recv_sems.at[j], device_id=(px, py, pc), device_id_type=MESH).start()
        token[...] = jnp.zeros_like(token)

    return _call(
        body, name=name,
        out_shape=(pltpu.SemaphoreType.DMA((3,)), pltpu.SemaphoreType.DMA((3,)), pltpu.HBM(p.shape, p.dtype),
                   pltpu.HBM((3, r, c), p.dtype), _sds((8, 128))),
        in_specs=(HBM_SPEC, HBM_SPEC),
        out_specs=(SEM_SPEC, SEM_SPEC, HBM_SPEC, HBM_SPEC, pl.BlockSpec(memory_space=pltpu.VMEM)),
        input_output_aliases={0: 2, 1: 3},
        compiler_params=pltpu.CompilerParams(has_side_effects=DATAFLOW),
    )(_hbm(p), _hbm(lax.empty((3, r, c), p.dtype)))


def chips_wait(send_sems, recv_sems, p_thru, land_thru, after, name):
    def body(p_ref, land_ref, send_sems, recv_sems, after_ref, p_out, land_out):
        x, y, cc = _my_place()
        for j, (px, py, pc) in enumerate(_peers(x, y, cc)[1:]):
            cp = pltpu.make_async_remote_copy(
                src_ref=p_ref.at[2 * px + py], dst_ref=land_ref.at[j], send_sem=send_sems.at[j],
                recv_sem=recv_sems.at[j], device_id=(px, py, pc), device_id_type=MESH)
            cp.wait_send()
            cp.wait_recv()

    return _call(
        body, name=name,
        out_shape=(pltpu.HBM(p_thru.shape, p_thru.dtype), pltpu.HBM(land_thru.shape, land_thru.dtype)),
        in_specs=(HBM_SPEC, HBM_SPEC, SEM_SPEC, SEM_SPEC, pl.BlockSpec(memory_space=pl.ANY)),
        out_specs=(HBM_SPEC, HBM_SPEC), input_output_aliases={0: 0, 1: 1},
        compiler_params=pltpu.CompilerParams(has_side_effects=DATAFLOW),
    )(p_thru, land_thru, send_sems, recv_sems, after)


def reduce_scatter_begin(gw, name, tag):
    _, r, c = gw.shape
    theirs = exchange_sibling(gw, name + "_sibling")
    mine = lax.dynamic_index_in_dim(gw.reshape(4, 2, r, c), lax.axis_index("c"), axis=1, keepdims=False)
    chip_sum = add_n([mine, theirs], BF16, name + "_add2")
    return chips_start(chip_sum, name + "_start" + tag)


def reduce_scatter_end(started, after, name, tag):
    send_sems, recv_sems, p_thru, land_thru, _ = started
    parts, land = chips_wait(send_sems, recv_sems, p_thru, land_thru, after, name + "_wait" + tag)
    mine = lax.dynamic_index_in_dim(parts, 2 * lax.axis_index("x") + lax.axis_index("y"), axis=0, keepdims=False)
    return add_own_lead(mine, land, name + "_add4")


def _pack(arrs):
    flat = jnp.concatenate([a.reshape(-1).astype(F32) for a in arrs])
    n = flat.shape[0]
    pad = (-n) % PACK_ELEMS
    return jnp.pad(flat, (0, pad)).reshape(-1, 128)


def _unpack(packed, shapes):
    flat = packed.reshape(-1)
    out, off = [], 0
    for s in shapes:
        n = int(np.prod(s))
        out.append(lax.optimization_barrier(flat[off:off + n]).reshape(s))
        off += n
    return out


def _t5_bucket(rel):
    nb = N_BUCKETS // 2
    max_exact = nb // 2
    ret = jnp.where(rel > 0, nb, 0)
    n = jnp.abs(rel)
    nf = jnp.maximum(n, 1).astype(jnp.float32)
    large = max_exact + (jnp.log(nf / max_exact) / math.log(MAX_DISTANCE / max_exact)
                         * (nb - max_exact)).astype(jnp.int32)
    large = jnp.minimum(large, nb - 1)
    return ret + jnp.where(n < max_exact, n, large)


SMALL_SHARDED = ("gla_w2_f", "gla_w2_b", "conv_w", "lru_ba", "lru_bx", "lru_lambda")
SMALL_REPL = ("rel_bias", "attn_sink", "gla_b2_f", "gla_b2_b", "gla_norm", "conv_b", "lru_wa", "lru_wx",
              "norm_mix_pre", "norm_mix_post", "norm_mem", "norm_x_pre", "norm_x_post", "norm_ff_pre", "norm_ff_post")
BIG = ("w_in", "w_out", "xq", "xk", "xv", "xo", "w_up", "w_down")
WEIGHTS = ['rel_bias', 'w_in', 'w_out', 'attn_sink', 'gla_w2_f', 'gla_b2_f', 'gla_w2_b', 'gla_b2_b', 'gla_norm',
           'conv_w', 'conv_b', 'lru_wa', 'lru_ba', 'lru_wx', 'lru_bx', 'lru_lambda', 'xq', 'xk', 'xv', 'xo', 'w_up',
           'w_down', 'norm_mix_pre', 'norm_mix_post', 'norm_mem', 'norm_x_pre', 'norm_x_post', 'norm_ff_pre',
           'norm_ff_post']


def _step(x, mem, loss_target, w, m, v):
    depth = w["w_in"].shape[0]
    t, d = x.shape[1], x.shape[2]
    ml = mem.shape[1]
    rx = d // N_DEV
    rf = w["w_up"].shape[2]
    r_out = D_MIX // N_DEV
    x = x.reshape(t, d)
    mem = mem.reshape(ml, d)
    loss_target = loss_target.reshape(t, d)
    my_idx = 4 * lax.axis_index("x") + 2 * lax.axis_index("y") + lax.axis_index("c")

    off_in = 0
    off_up, off_down, off_out = 0, rf, 2 * rf
    off_xq = off_out + r_out
    off_xk, off_xv, off_xo = off_xq + rx, off_xq + 2 * rx, off_xq + 3 * rx
    r_rest = off_xo + rx

    sh_shapes = [w[n].shape for n in SMALL_SHARDED]
    gathered = all_gather(_pack([w[n] for n in SMALL_SHARDED]), "ag_small")
    per_dev = [_unpack(gathered[j], sh_shapes) for j in range(N_DEV)]
    full = {n: jnp.concatenate([per_dev[j][i] for j in range(N_DEV)], axis=-1) for i, n in enumerate(SMALL_SHARDED)}
    for n in SMALL_REPL:
        full[n] = w[n]

    ag_started = []
    for l in range(depth):
        blk_in = jnp.pad(w["w_in"][l].T, ((0, W_IN_ROWS - W_IN_SHARD), (0, 0))).astype(BF16)
        blk_rest = jnp.concatenate([w["w_up"][l].T, w["w_down"][l], w["w_out"][l], w["xq"][l], w["xk"][l],
                                    w["xv"][l], w["xo"][l]], axis=0).astype(BF16)
        blk_in, blk_rest, _ = lax.optimization_barrier((blk_in, blk_rest, gathered))
        ag_started.append((gather_start(blk_in, "ag_start_in%d" % l), gather_start(blk_rest, "ag_start_rest%d" % l)))
    x = x + sum(st[4][0, 0] for pair in ag_started for st in pair)
    gws = [None] * depth

    def gather_finish(started, after, name):
        send_sems, recv_sems, blk_thru, land_thru, _ = started
        blk_done, land = gather_wait(send_sems, recv_sems, blk_thru, land_thru, after, name)
        land = gather_pass(land, "ag_pass")
        return lax.dynamic_update_index_in_dim(land, blk_done, my_idx, 0)

    qi = jnp.arange(BLOCK)[:, None]
    kj = jnp.arange(3 * BLOCK)[None, :]
    onehot_t = (jnp.arange(N_BUCKETS)[:, None] == _t5_bucket(kj - BLOCK - qi).reshape(1, -1)).astype(F32)
    bias = mm_plain(full["rel_bias"].T, onehot_t, "rel_bias_lookup", hi=True, tn=3 * BLOCK * 16)
    bias = bias.reshape(A_HEADS, BLOCK, 3 * BLOCK)
    bias_t = jnp.transpose(bias, (0, 2, 1))

    def sink_rows(sink):
        s = jnp.broadcast_to(sink.reshape(A_KV_HEADS, A_GROUP, 1), (A_KV_HEADS, A_GROUP, 128))
        return jnp.pad(s, ((0, 0), (0, 8 - A_GROUP), (0, 0)))

    def split_proj(p):
        outs, off = [], 0
        for s in SPLIT_SIZES:
            outs.append(p[:, off:off + s])
            off += s
        return outs

    def lead(a):
        return a.reshape(a.shape[0], C_WIDTH // 128, 128)

    saved = []
    h = rms_fwd(x, full["norm_mix_pre"][0], "rms_first")
    for l in range(depth):
        gw_in = gather_finish(ag_started[l][0], x, "ag_wait_in%d" % l)
        sv = {"x": x, "h_in": h}
        proj_pad = mm_wn(h, gw_in, off_in, W_IN_ROWS, "mm_w_in")
        proj = jnp.concatenate([proj_pad[:, j * W_IN_ROWS:j * W_IN_ROWS + W_IN_SHARD] for j in range(N_DEV)], axis=1)
        aq, ak, av, bq, bk, bv, bg, zf, zb, cx, cy = split_proj(proj)
        sv.update(aq=aq, ak=ak, av=av, bq=bq, bk=bk, bv=bv, bg=bg, zf=zf, zb=zb, cx=cx, cy=cy)
        sink_b = sink_rows(full["attn_sink"][l])
        oa = attn_fwd(aq, ak, av, bias, sink_b, "attn_fwd")
        la_f, la_b = gla_gates_fwd(zf, zb, full["gla_w2_f"][l], full["gla_b2_f"][l], full["gla_w2_b"][l],
                                   full["gla_b2_b"][l], "gla_gates_fwd")
        bk_t = bk.T
        gla = {}
        for nm, la, bdir in (("f", la_f, False), ("b", la_b, True)):
            la_t = la.T
            u, dec = gla_outer(bk_t, la_t, bv, "gla_outer_k_" + nm, bdir, "khat")
            sp = scan_lead(dec, u, "gla_state_scan_" + nm, reverse=bdir, inclusive=False)
            o_dir = gla_inner_fwd(bq, bk, bv, la, sp, "gla_inner_fwd_" + nm, bdir)
            gla[nm] = dict(la=la, la_t=la_t, dec=dec, sp=sp, o=o_dir)
        ob = gla_out_fwd(gla["f"]["o"], gla["b"]["o"], bg, full["gla_norm"][l], "gla_out_fwd")
        sv["gla"] = gla
        xm2, xm1, xp1 = _shift(cx, -2), _shift(cx, -1), _shift(cx, 1)
        xc, a0, u0, a1, u1 = lru_gates_fwd(cx, xm2, xm1, xp1, full["conv_w"][l], full["conv_b"][l], full["lru_wa"][l],
                                           full["lru_ba"][l], full["lru_wx"][l], full["lru_bx"][l],
                                           full["lru_lambda"][l], "lru_gates_fwd")
        h0 = scan_lead(lead(a0), lead(u0), "lru_scan_fwd", reverse=False, inclusive=True).reshape(t, C_WIDTH)
        h1 = scan_lead(lead(a1), lead(u1), "lru_scan_rev", reverse=True, inclusive=True).reshape(t, C_WIDTH)
        oc = lru_out_fwd(h0, h1, cy, "lru_out_fwd")
        sv.update(xm2=xm2, xm1=xm1, xp1=xp1, xc=xc, a0=a0, a1=a1, h0=h0, h1=h1, oa=oa)
        cat = jnp.concatenate([oa, ob, oc], axis=1).astype(BF16)
        gw = gather_finish(ag_started[l][1], cat, "ag_wait_rest%d" % l)
        gws[l] = (gw_in, gw)
        mixed = mm_wk(cat, gw, off_out, r_out, "mm_w_out")
        x1, h2 = resid_rms(x, mixed, full["norm_mix_post"][l], full["norm_x_pre"][l], "resid_rms")
        sv.update(cat=cat, mixed=mixed, x1=x1, h2=h2)
        memn = rms_fwd(mem, full["norm_mem"][l], "rms_mem")
        q = mm_wk(h2, gw, off_xq, rx, "mm_xq")
        k = mm_wk(memn, gw, off_xk, rx, "mm_xkv")
        vv = mm_wk(memn, gw, off_xv, rx, "mm_xkv")
        ox = xattn_fwd(q, k, vv, "xattn_fwd")
        ca = mm_wk(ox, gw, off_xo, rx, "mm_xo")
        x2, h3 = resid_rms(x1, ca, full["norm_x_post"][l], full["norm_ff_pre"][l], "resid_rms")
        sv.update(memn=memn, q=q, k=k, v=vv, ox=ox, ca=ca, x2=x2, h3=h3)
        up, act = mm_wn(h3, gw, off_up, rf, "mm_w_up", with_relu2=True)
        ff = mm_wk(act, gw, off_down, rf, "mm_w_down", jb=max(1, min(N_DEV, 2048 // rf)))
        if l + 1 < depth:
            x, h = resid_rms(x2, ff, full["norm_ff_post"][l], full["norm_mix_pre"][l + 1], "resid_rms")
        else:
            x = resid_rms(x2, ff, full["norm_ff_post"][l], None, "resid_rms_last")
        sv.update(up=up, act=act, ff=ff)
        saved.append(sv)

    dx, loss_local = loss_and_grad(x, loss_target, "loss")
    loss = lax.psum(loss_local, AXES)

    grads = {n: [None] * depth for n in WEIGHTS if n != "rel_bias"}
    dbias_total = None
    big_grads = [None] * depth
    rs_started = [None] * depth
    bf = lambda a: a.astype(BF16)
    for l in reversed(range(depth)):
        gw_in, gw = gws[l]
        sv = saved[l]
        dff, grads["norm_ff_post"][l] = rms_bwd(sv["ff"], full["norm_ff_post"][l], dx, "rms_bwd")
        dup = mm_wn(dff, gw, off_down, rf, "mm_w_down_dx", relu_grad_of=sv["up"], out_dtype=BF16)
        g_down = mm_plain(sv["act"], dff, "mm_dw_down", ta=True, out_dtype=BF16)
        g_up_t = mm_plain(dup, sv["h3"], "mm_dw_up", ta=True, out_dtype=BF16)
        dh3 = mm_wk(dup, gw, off_up, rf, "mm_w_up_dx", jb=max(1, min(N_DEV, 2048 // rf)))
        dx2, grads["norm_ff_pre"][l] = rms_bwd(sv["x2"], full["norm_ff_pre"][l], dh3, "rms_bwd_add", add=dx)
        dca, grads["norm_x_post"][l] = rms_bwd(sv["ca"], full["norm_x_post"][l], dx2, "rms_bwd")
        dox = mm_wn(dca, gw, off_xo, rx, "mm_x_dx")
        g_xo = mm_plain(sv["ox"], dca, "mm_dw_xo", ta=True, out_dtype=BF16)
        dq, dk, dv = xattn_bwd(sv["q"], sv["k"], sv["v"], sv["ox"], dox, "xattn_bwd")
        g_xq = mm_plain(sv["h2"], dq, "mm_dw_d", ta=True, out_dtype=BF16)
        g_xk = mm_plain(sv["memn"], dk, "mm_dw_mem", ta=True, out_dtype=BF16)
        g_xv = mm_plain(sv["memn"], dv, "mm_dw_mem", ta=True, out_dtype=BF16)
        dh2 = mm_wn(dq, gw, off_xq, rx, "mm_x_dx")
        dmem_k = mm_wn(dk, gw, off_xk, rx, "mm_x_dx_mem")
        dmem_v = mm_wn(dv, gw, off_xv, rx, "mm_x_dx_mem")
        _, grads["norm_mem"][l] = rms_bwd(mem, full["norm_mem"][l], dmem_k, "rms_bwd_mem", dy2=dmem_v)
        dx1, grads["norm_x_pre"][l] = rms_bwd(sv["x1"], full["norm_x_pre"][l], dh2, "rms_bwd_add", add=dx2)
        dmixed, grads["norm_mix_post"][l] = rms_bwd(sv["mixed"], full["norm_mix_post"][l], dx1, "rms_bwd")
        dcat = mm_wn(dmixed, gw, off_out, r_out, "mm_w_out_dx")
        g_out = mm_plain(sv["cat"], dmixed, "mm_dw_d", ta=True, out_dtype=BF16)
        parts = [g_up_t, g_down, g_out, g_xq, g_xk, g_xv, g_xo]
        gpack = jnp.concatenate([p.reshape(N_DEV, p.shape[0] // N_DEV, d) for p in parts], axis=1)
        rs_started[l] = [reduce_scatter_begin(gpack, "rs_rest", str(l)), None]
        dcat = dcat + rs_started[l][0][4][0, 0]
        doa, dob, doc = dcat[:, :A_Q], dcat[:, A_Q:A_Q + B_V], dcat[:, A_Q + B_V:]
        daq, dak, dav, dbias, dsink = attn_bwd(sv["aq"], sv["ak"], sv["av"], bias, bias_t,
                                               sink_rows(full["attn_sink"][l]), doa, sv["oa"], "attn_bwd")
        grads["attn_sink"][l] = dsink[:, :A_GROUP, 0].reshape(A_HEADS)
        dbias_total = dbias if dbias_total is None else add_n([dbias_total, dbias], F32, "add_dbias")
        gf, gb = sv["gla"]["f"], sv["gla"]["b"]
        do_gla, dbg, dgn = gla_out_bwd(gf["o"], gb["o"], sv["bg"], full["gla_norm"][l], dob, "gla_out_bwd")
        grads["gla_norm"][l] = dgn.reshape(B_V)
        bq_t = sv["bq"].T
        acc = None
        dlas = {}
        for nm, gd, bdir in (("f", gf, False), ("b", gb, True)):
            wq = gla_outer(bq_t, gd["la_t"], do_gla, "gla_outer_q_" + nm, bdir, "qtil")
            gs = scan_lead(gd["dec"], wq, "gla_adj_scan_" + nm, reverse=not bdir, inclusive=False)
            dbq, dbk, dbv, dlas[nm] = gla_inner_bwd(sv["bq"], sv["bk"], sv["bv"], gd["la"], do_gla, gd["sp"], gs,
                                                    gd["dec"], "gla_inner_bwd_" + nm, bdir, add=acc)
            acc = (dbq, dbk, dbv)
        dzf, dzb, dpre_f, dpre_b, db2f, db2b = gla_gates_bwd(
            sv["zf"], sv["zb"], full["gla_w2_f"][l], full["gla_b2_f"][l], full["gla_w2_b"][l], full["gla_b2_b"][l],
            dlas["f"], dlas["b"], "gla_gates_bwd")
        grads["gla_b2_f"][l] = db2f.reshape(B_QK)
        grads["gla_b2_b"][l] = db2b.reshape(B_QK)
        grads["gla_w2_f"][l] = mm_plain(sv["zf"].T, dpre_f, "mm_dw_gate", hi=True)
        grads["gla_w2_b"][l] = mm_plain(sv["zb"].T, dpre_b, "mm_dw_gate", hi=True)
        dh, dcy = lru_out_bwd(sv["h0"], sv["h1"], sv["cy"], doc, "lru_out_bwd")
        g0 = scan_lead(lead(_shift(sv["a0"], 1)), lead(dh), "lru_scan_rev", reverse=True,
                       inclusive=True).reshape(t, C_WIDTH)
        g1 = scan_lead(lead(_shift(sv["a1"], -1)), lead(dh), "lru_scan_fwd", reverse=False,
                       inclusive=True).reshape(t, C_WIDTH)
        dxc, dzr0, dzi0, dzr1, dzi1, dlam, dba, dbx = lru_gates_bwd(
            sv["xc"], g0, _shift(sv["h0"], -1), g1, _shift(sv["h1"], 1), full["lru_wa"][l], full["lru_ba"][l],
            full["lru_wx"][l], full["lru_bx"][l], full["lru_lambda"][l], "lru_gates_bwd")
        xc_t = bf(sv["xc"].T)
        grads["lru_wa"][l] = jnp.stack([blockdiag_dw(xc_t, dzr0, "lru_dw"), blockdiag_dw(xc_t, dzr1, "lru_dw")])
        grads["lru_wx"][l] = jnp.stack([blockdiag_dw(xc_t, dzi0, "lru_dw"), blockdiag_dw(xc_t, dzi1, "lru_dw")])
        grads["lru_lambda"][l], grads["lru_ba"][l], grads["lru_bx"][l] = dlam, dba, dbx
        dcx, dcw, dcb = conv_bwd(dxc, _shift(dxc, 2), _shift(dxc, 1), _shift(dxc, -1), sv["cx"], sv["xm2"],
                                 sv["xm1"], sv["xp1"], full["conv_w"][l], "conv_bwd")
        grads["conv_w"][l] = dcw
        grads["conv_b"][l] = dcb.reshape(C_WIDTH)
        dproj = jnp.concatenate([daq, dak, dav, dbq, dbk, dbv, dbg, dzf, dzb, dcx, dcy], axis=1)
        zero_cols = jnp.zeros((t, W_IN_ROWS - W_IN_SHARD), F32)
        dproj_pad = jnp.concatenate(
            [p for j in range(N_DEV) for p in (dproj[:, j * W_IN_SHARD:(j + 1) * W_IN_SHARD], zero_cols)],
            axis=1).astype(BF16)
        g_in_t = mm_plain(dproj_pad, sv["h_in"], "mm_dw_in", ta=True, out_dtype=BF16)
        dh1 = mm_wk(dproj_pad, gw_in, off_in, W_IN_ROWS, "mm_w_in_dx", jb=2)
        dx, grads["norm_mix_pre"][l] = rms_bwd(sv["x"], full["norm_mix_pre"][l], dh1, "rms_bwd_add", add=dx1)
        rs_started[l][1] = reduce_scatter_begin(g_in_t.reshape(N_DEV, W_IN_ROWS, d), "rs_in", str(l))
        dx = dx + rs_started[l][1][4][0, 0]
    for l in range(depth):
        big_grads[l] = (reduce_scatter_end(rs_started[l][1], dx, "rs_in", str(l)),
                        reduce_scatter_end(rs_started[l][0], dx, "rs_rest", str(l)))

    grad_rel = mm_plain(dbias_total.reshape(A_HEADS, -1), onehot_t, "rel_bias_grad", tb=True, hi=True).T

    small_names = [n for n in WEIGHTS if n not in BIG]
    small_g = {"rel_bias": grad_rel}
    for n in small_names:
        if n != "rel_bias":
            small_g[n] = jnp.stack([g.reshape(full[n].shape[1:]) for g in grads[n]])
    shapes = [small_g[n].shape for n in small_names]
    packed = all_gather(_pack([small_g[n] for n in small_names]), "ag_small_grads")
    summed = sum_lead(packed, tuple(range(N_DEV)), F32, "add8_small")
    small_g = dict(zip(small_names, _unpack(summed, shapes)))
    for n in SMALL_SHARDED:
        wdt = w[n].shape[-1]
        small_g[n] = lax.dynamic_slice_in_dim(small_g[n], my_idx * wdt, wdt, axis=small_g[n].ndim - 1)

    grad_out, delta, new_m, new_v = {}, {}, {}, {}
    direct = ("lru_wa", "lru_wx")
    packed_names = [n for n in small_names if n not in direct]
    sshapes = [w[n].shape for n in packed_names]
    ds, ms, vs = adamw(_pack([small_g[n] for n in packed_names]), _pack([w[n] for n in packed_names]),
                       _pack([m[n] for n in packed_names]), _pack([v[n] for n in packed_names]), "adamw_small")
    for n, d_, m_, v_ in zip(packed_names, _unpack(ds, sshapes), _unpack(ms, sshapes), _unpack(vs, sshapes)):
        grad_out[n], delta[n], new_m[n], new_v[n] = small_g[n], d_, m_, v_
    for n in direct:
        grad_out[n] = small_g[n]
        delta[n], new_m[n], new_v[n] = adamw(small_g[n], w[n], m[n], v[n], "adamw_lru")

    def rows(l, off, r):
        return big_grads[l][1][off:off + r]

    big_g = {
        "w_in": jnp.stack([big_grads[l][0][:W_IN_SHARD].T for l in range(depth)]),
        "w_out": jnp.stack([rows(l, off_out, r_out) for l in range(depth)]),
        "xq": jnp.stack([rows(l, off_xq, rx) for l in range(depth)]),
        "xk": jnp.stack([rows(l, off_xk, rx) for l in range(depth)]),
        "xv": jnp.stack([rows(l, off_xv, rx) for l in range(depth)]),
        "xo": jnp.stack([rows(l, off_xo, rx) for l in range(depth)]),
        "w_up": jnp.stack([rows(l, off_up, rf).T for l in range(depth)]),
        "w_down": jnp.stack([rows(l, off_down, rf) for l in range(depth)]),
    }
    for n in BIG:
        grad_out[n] = big_g[n]
        delta[n], new_m[n], new_v[n] = adamw(big_g[n], w[n], m[n], v[n], "adamw_" + n)

    return (loss, dx.reshape(1, t, d), *[grad_out[n] for n in WEIGHTS], *[delta[n] for n in WEIGHTS],
            *[new_m[n] for n in WEIGHTS], *[new_v[n] for n in WEIGHTS])


def kernel(x, mem, rel_bias, w_in, w_out, attn_sink, gla_w2_f, gla_b2_f, gla_w2_b, gla_b2_b, gla_norm, conv_w, conv_b, lru_wa, lru_ba, lru_wx, lru_bx, lru_lambda, xq, xk, xv, xo, w_up, w_down, norm_mix_pre, norm_mix_post, norm_mem, norm_x_pre, norm_x_post, norm_ff_pre, norm_ff_post, loss_target, m_rel_bias, m_w_in, m_w_out, m_attn_sink, m_gla_w2_f, m_gla_b2_f, m_gla_w2_b, m_gla_b2_b, m_gla_norm, m_conv_w, m_conv_b, m_lru_wa, m_lru_ba, m_lru_wx, m_lru_bx, m_lru_lambda, m_xq, m_xk, m_xv, m_xo, m_w_up, m_w_down, m_norm_mix_pre, m_norm_mix_post, m_norm_mem, m_norm_x_pre, m_norm_x_post, m_norm_ff_pre, m_norm_ff_post, v_rel_bias, v_w_in, v_w_out, v_attn_sink, v_gla_w2_f, v_gla_b2_f, v_gla_w2_b, v_gla_b2_b, v_gla_norm, v_conv_w, v_conv_b, v_lru_wa, v_lru_ba, v_lru_wx, v_lru_bx, v_lru_lambda, v_xq, v_xk, v_xv, v_xo, v_w_up, v_w_down, v_norm_mix_pre, v_norm_mix_post, v_norm_mem, v_norm_x_pre, v_norm_x_post, v_norm_ff_pre, v_norm_ff_post):
    given = dict(locals())
    w = {n: given[n] for n in WEIGHTS}
    m = {n: given["m_" + n] for n in WEIGHTS}
    v = {n: given["v_" + n] for n in WEIGHTS}
    return _step(x, mem, loss_target, w, m, v)
```

```python
import math

import jax
import jax.numpy as jnp
import numpy as np
from jax import lax
from jax.experimental import pallas as pl
from jax.experimental.pallas import tpu as pltpu

F32 = jnp.float32
BF16 = jnp.bfloat16
HI = lax.Precision.HIGHEST
NN = (((1,), (0,)), ((), ()))
NT = (((1,), (1,)), ((), ()))
MESH = pl.DeviceIdType.MESH
AXES = ("x", "y", "c")
N_DEV = 8

A_HEAD_DIM = 128
A_HEADS = 8
A_KV_HEADS = 2
A_GROUP = 4
WINDOW = 128
BLOCK = 128
N_BUCKETS = 32
MAX_DISTANCE = 128
B_HEADS = 4
B_KEY_DIM = 64
B_VAL_DIM = 128
GATE_RANK = 16
GATE_TAU = 16.0
GLA_CHUNK = 16
C_WIDTH = 512
C_BLOCKS = 4
C_BLOCK_DIM = 128
LRU_C = 8.0
X_HEADS = 4
EPS = 1e-6
NEG_INF = -1e30
A_Q = A_HEADS * A_HEAD_DIM
A_KV = A_KV_HEADS * A_HEAD_DIM
B_QK = B_HEADS * B_KEY_DIM
B_V = B_HEADS * B_VAL_DIM
SPLIT_SIZES = (A_Q, A_KV, A_KV, B_QK, B_QK, B_V, B_V, GATE_RANK, GATE_RANK, C_WIDTH, C_WIDTH)
D_IN = sum(SPLIT_SIZES)
D_MIX = A_Q + B_V + C_WIDTH
W_IN_SHARD = D_IN // N_DEV
W_IN_ROWS = 768
GLA_TILE = 128
CHUNKS_PER_TILE = GLA_TILE // GLA_CHUNK
EXP_CLAMP = 80.0

ADAM_LR = 0.001
ADAM_B1 = 0.9
ADAM_B2 = 0.999
ADAM_EPS = 1e-08
ADAM_WD = 0.01
ADAM_STEP = 10

VMEM_LIMIT_BYTES = 52 * 1024 * 1024
MM_TILE = 1024
SIBLING_STREAMS = 16
PACK_ELEMS = 128 * 2048


def _call(body, **kw):
    return pl.pallas_call(body, **kw)


def _cparams():
    return pltpu.CompilerParams(vmem_limit_bytes=VMEM_LIMIT_BYTES)


def _dot(a, b, dims=NN, hi=False):
    if hi:
        return lax.dot_general(a, b, dims, precision=HI, preferred_element_type=F32)
    return lax.dot_general(a.astype(BF16), b.astype(BF16), dims, preferred_element_type=F32)


def _sds(shape, dtype=F32):
    return jax.ShapeDtypeStruct(tuple(shape), dtype)


def _row_tile(rows, cols, target_elems=1 << 18):
    want = max(8, target_elems // max(cols, 1))
    if rows <= want:
        return rows
    t = (want // 8) * 8
    while t >= 8:
        if rows % t == 0:
            return t
        t -= 8
    return rows


def _expm1(x):
    poly = x * (1.0 + x * (1.0 / 2 + x * (1.0 / 6 + x * (1.0 / 24 + x * (1.0 / 120 + x * (
        1.0 / 720 + x * (1.0 / 5040 + x * (1.0 / 40320))))))))
    return jnp.where(jnp.abs(x) < 0.3, poly, jnp.exp(x) - 1.0)


def _log1p(e):
    w = 1.0 + e
    return jnp.where(w == 1.0, e, jnp.log(w) * e / (w - 1.0))


def _softplus(x):
    return jnp.maximum(x, 0.0) + _log1p(jnp.exp(-jnp.abs(x)))


def _sigmoid(x):
    return jax.nn.sigmoid(x)


GELU_K = math.sqrt(2.0 / math.pi)


def _gelu(y):
    t = jnp.tanh(GELU_K * (y + 0.044715 * y * y * y))
    return 0.5 * y * (1.0 + t)


def _gelu_grad(y):
    t = jnp.tanh(GELU_K * (y + 0.044715 * y * y * y))
    return 0.5 * (1.0 + t) + 0.5 * y * (1.0 - t * t) * GELU_K * (1.0 + 3 * 0.044715 * y * y)


def rms_fwd(x, g, name):
    m, d = x.shape
    tm = _row_tile(m, d)

    def body(x_ref, g_ref, o_ref):
        xv = x_ref[...]
        r = lax.rsqrt(jnp.mean(xv * xv, axis=1, keepdims=True) + EPS)
        o_ref[...] = (xv * r * g_ref[...]).astype(o_ref.dtype)

    return _call(body, name=name, grid=(m // tm,),
                 in_specs=[pl.BlockSpec((tm, d), lambda i: (i, 0)), pl.BlockSpec((1, d), lambda i: (0, 0))],
                 out_specs=pl.BlockSpec((tm, d), lambda i: (i, 0)),
                 out_shape=_sds((m, d), BF16))(x, g.reshape(1, d))


def resid_rms(xres, mid, g_post, g_pre, name):
    m, d = xres.shape
    tm = _row_tile(m, d)
    with_pre = g_pre is not None

    def body(*refs):
        if with_pre:
            x_ref, m_ref, gp_ref, gn_ref, xo_ref, h_ref = refs
        else:
            x_ref, m_ref, gp_ref, xo_ref = refs
        mv = m_ref[...]
        r = lax.rsqrt(jnp.mean(mv * mv, axis=1, keepdims=True) + EPS)
        xn = x_ref[...] + mv * r * gp_ref[...]
        xo_ref[...] = xn
        if with_pre:
            r2 = lax.rsqrt(jnp.mean(xn * xn, axis=1, keepdims=True) + EPS)
            h_ref[...] = (xn * r2 * gn_ref[...]).astype(h_ref.dtype)

    row = pl.BlockSpec((tm, d), lambda i: (i, 0))
    vec = pl.BlockSpec((1, d), lambda i: (0, 0))
    ins = [xres, mid, g_post.reshape(1, d)] + ([g_pre.reshape(1, d)] if with_pre else [])
    in_specs = [row, row, vec] + ([vec] if with_pre else [])
    if with_pre:
        return _call(body, name=name, grid=(m // tm,), in_specs=in_specs, out_specs=(row, row),
                     out_shape=(_sds((m, d)), _sds((m, d), BF16)))(*ins)
    return _call(body, name=name, grid=(m // tm,), in_specs=in_specs, out_specs=row,
                 out_shape=_sds((m, d)))(*ins)


def rms_bwd(x, g, dy, name, dy2=None, add=None):
    m, d = x.shape
    tm = _row_tile(m, d)
    has2, hasadd = dy2 is not None, add is not None

    def body(*refs):
        it = iter(refs)
        x_ref, g_ref, dy_ref = next(it), next(it), next(it)
        dy2_ref = next(it) if has2 else None
        add_ref = next(it) if hasadd else None
        dx_ref, dg_ref = next(it), next(it)
        xv = x_ref[...]
        dyv = dy_ref[...]
        if has2:
            dyv = dyv + dy2_ref[...]
        r = lax.rsqrt(jnp.mean(xv * xv, axis=1, keepdims=True) + EPS)
        xh = xv * r
        dxh = dyv * g_ref[...]
        dx = r * (dxh - xh * jnp.mean(dxh * xh, axis=1, keepdims=True))
        if hasadd:
            dx = dx + add_ref[...]
        dx_ref[...] = dx
        part = jnp.sum(dyv * xh, axis=0, keepdims=True)

        @pl.when(pl.program_id(0) == 0)
        def _():
            dg_ref[...] = part

        @pl.when(pl.program_id(0) > 0)
        def _():
            dg_ref[...] += part

    row = pl.BlockSpec((tm, d), lambda i: (i, 0))
    vec = pl.BlockSpec((1, d), lambda i: (0, 0))
    ins = [x, g.reshape(1, d), dy] + ([dy2] if has2 else []) + ([add] if hasadd else [])
    in_specs = [row, vec, row] + ([row] if has2 else []) + ([row] if hasadd else [])
    return _call(body, name=name, grid=(m // tm,), in_specs=in_specs, out_specs=(row, vec),
                 out_shape=(_sds((m, d)), _sds((1, d))))(*ins)


def loss_and_grad(y, target, name):
    m, d = y.shape
    tm = _row_tile(m, d)

    def body(y_ref, t_ref, dy_ref, l_ref):
        e = y_ref[...] - t_ref[...]
        dy_ref[...] = e * (1.0 / d)
        s = jnp.sum(jnp.sum(e * e, axis=1, keepdims=True), axis=0, keepdims=True) * (0.5 / d)
        part = jnp.broadcast_to(s, (1, 128))

        @pl.when(pl.program_id(0) == 0)
        def _():
            l_ref[...] = part

        @pl.when(pl.program_id(0) > 0)
        def _():
            l_ref[...] += part

    row = pl.BlockSpec((tm, d), lambda i: (i, 0))
    dy, l = _call(body, name=name, grid=(m // tm,), in_specs=[row, row],
                  out_specs=(row, pl.BlockSpec((1, 128), lambda i: (0, 0))),
                  out_shape=(_sds((m, d)), _sds((1, 128))))(y, target)
    return dy, l[0, 0]


def adamw(g, w, m, v, name):
    shape = w.shape
    cols = shape[-1]
    rows = int(np.prod(shape[:-1]))
    tm = _row_tile(rows, cols)
    c1 = 1.0 - ADAM_B1 ** ADAM_STEP
    c2 = 1.0 - ADAM_B2 ** ADAM_STEP

    def body(g_ref, w_ref, m_ref, v_ref, d_ref, mo_ref, vo_ref):
        gv = g_ref[...]
        mn = ADAM_B1 * m_ref[...] + (1.0 - ADAM_B1) * gv
        vn = ADAM_B2 * v_ref[...] + (1.0 - ADAM_B2) * (gv * gv)
        m_hat = mn / c1
        v_hat = vn / c2
        d_ref[...] = -ADAM_LR * (m_hat / (jnp.sqrt(v_hat) + ADAM_EPS) + ADAM_WD * w_ref[...])
        mo_ref[...] = mn
        vo_ref[...] = vn

    row = pl.BlockSpec((tm, cols), lambda i: (i, 0))
    outs = _call(body, name=name, grid=(rows // tm,), in_specs=[row] * 4, out_specs=(row,) * 3,
                 out_shape=(_sds((rows, cols)),) * 3)(*[a.reshape(rows, cols) for a in (g, w, m, v)])
    return tuple(o.reshape(shape) for o in outs)


def sum_lead(x, order, out_dtype, name):
    n, rows, cols = x.shape
    tm = _row_tile(rows, cols)

    def body(x_ref, o_ref):
        acc = x_ref[order[0]].astype(F32)
        for i in order[1:]:
            acc = acc + x_ref[i].astype(F32)
        o_ref[...] = acc.astype(out_dtype)

    return _call(body, name=name, grid=(rows // tm,), in_specs=[pl.BlockSpec((n, tm, cols), lambda i: (0, i, 0))],
                 out_specs=pl.BlockSpec((tm, cols), lambda i: (i, 0)), out_shape=_sds((rows, cols), out_dtype))(x)


def add_own_lead(own, parts, name):
    n, rows, cols = parts.shape
    tm = _row_tile(rows, cols)

    def body(o_ref, p_ref, out_ref):
        acc = o_ref[...].astype(F32)
        for i in range(n):
            acc = acc + p_ref[i].astype(F32)
        out_ref[...] = acc

    row = pl.BlockSpec((tm, cols), lambda i: (i, 0))
    return _call(body, name=name, grid=(rows // tm,),
                 in_specs=[row, pl.BlockSpec((n, tm, cols), lambda i: (0, i, 0))], out_specs=row,
                 out_shape=_sds((rows, cols)))(own, parts)


def add_own_half(g5, theirs, name):
    _, _, r, c = g5.shape
    tm = _row_tile(r, c)

    def body(cc_ref, g_ref, t_ref, o_ref):
        o_ref[...] = (g_ref[...].astype(F32) + t_ref[...].astype(F32)).astype(o_ref.dtype)

    grid_spec = pltpu.PrefetchScalarGridSpec(
        num_scalar_prefetch=1, grid=(4, r // tm),
        in_specs=[pl.BlockSpec((None, None, tm, c), lambda q, i, cc_ref: (q, cc_ref[0], i, 0)),
                  pl.BlockSpec((None, tm, c), lambda q, i, cc_ref: (q, i, 0))],
        out_specs=pl.BlockSpec((None, tm, c), lambda q, i, cc_ref: (q, i, 0)))
    return _call(body, name=name, grid_spec=grid_spec, out_shape=_sds((4, r, c), BF16))(
        lax.axis_index("c").astype(jnp.int32).reshape(1), g5, theirs)


def add_n(xs, out_dtype, name):
    shape = xs[0].shape
    cols = shape[-1]
    rows = int(np.prod(shape[:-1]))
    tm = _row_tile(rows, cols)
    n = len(xs)

    def body(*refs):
        acc = refs[0][...].astype(F32)
        for r in refs[1:n]:
            acc = acc + r[...].astype(F32)
        refs[n][...] = acc.astype(out_dtype)

    row = pl.BlockSpec((tm, cols), lambda i: (i, 0))
    out = _call(body, name=name, grid=(rows // tm,), in_specs=[row] * n, out_specs=row,
                out_shape=_sds((rows, cols), out_dtype))(*[a.reshape(rows, cols) for a in xs])
    return out.reshape(shape)


def mm_plain(a, b, name, ta=False, tb=False, out_dtype=F32, hi=False, tm=MM_TILE, tn=MM_TILE):
    k, m = a.shape[::1 if ta else -1]
    n = b.shape[0] if tb else b.shape[1]
    tm, tn = min(tm, m), min(tn, n)
    dims = (((0 if ta else 1,), (1 if tb else 0,)), ((), ()))

    def body(a_ref, b_ref, o_ref):
        o_ref[...] = _dot(a_ref[...], b_ref[...], dims, hi).astype(out_dtype)

    a_spec = pl.BlockSpec((k, tm), lambda j, i: (0, i)) if ta else pl.BlockSpec((tm, k), lambda j, i: (i, 0))
    b_spec = pl.BlockSpec((tn, k), lambda j, i: (j, 0)) if tb else pl.BlockSpec((k, tn), lambda j, i: (0, j))
    return _call(body, name=name, grid=(n // tn, m // tm), in_specs=[a_spec, b_spec],
                 out_specs=pl.BlockSpec((tm, tn), lambda j, i: (i, j)),
                 out_shape=_sds((m, n), out_dtype), compiler_params=_cparams())(a, b)


def mm_wk(a, gw, off, r, name, jb=N_DEV, tm=MM_TILE, tn=MM_TILE):
    m = a.shape[0]
    d = gw.shape[2]
    tm, tn = min(tm, m), min(tn, d)
    nk = N_DEV // jb
    ob = off // r
    assert off % r == 0 and a.shape[1] == N_DEV * r

    def body(a_ref, b_ref, o_ref, *acc):
        av = a_ref[...].astype(BF16)
        p = _dot(av[:, 0:r], b_ref[0])
        for q in range(1, jb):
            p = p + _dot(av[:, q * r:(q + 1) * r], b_ref[q])
        if nk == 1:
            o_ref[...] = p
        else:
            kk = pl.program_id(2)

            @pl.when(kk == 0)
            def _():
                acc[0][...] = p

            @pl.when(kk > 0)
            def _():
                acc[0][...] += p

            @pl.when(kk == nk - 1)
            def _():
                o_ref[...] = acc[0][...]

    return _call(body, name=name, grid=(m // tm, d // tn, nk),
                 in_specs=[pl.BlockSpec((tm, jb * r), lambda i, j, k: (i, k)),
                           pl.BlockSpec((jb, r, tn), lambda i, j, k: (k, ob, j))],
                 out_specs=pl.BlockSpec((tm, tn), lambda i, j, k: (i, j)),
                 out_shape=_sds((m, d)),
                 scratch_shapes=([pltpu.VMEM((tm, tn), F32)] if nk > 1 else []),
                 compiler_params=_cparams())(a, gw)


def mm_wn(a, gw, off, r, name, relu_grad_of=None, out_dtype=F32, with_relu2=False, tm=MM_TILE):
    m, d = a.shape
    tm = min(tm, m)
    ob = off // r
    assert off % r == 0 and gw.shape[2] == d
    epi = relu_grad_of is not None

    def body(*refs):
        it = iter(refs)
        a_ref, b_ref = next(it), next(it)
        e_ref = next(it) if epi else None
        o_ref = next(it)
        p = _dot(a_ref[...], b_ref[...], NT)
        if epi:
            p = p * (2.0 * jnp.maximum(e_ref[...], 0.0))
        o_ref[...] = p.astype(out_dtype)
        if with_relu2:
            act_ref = next(it)
            act_ref[...] = jnp.square(jnp.maximum(p, 0.0)).astype(act_ref.dtype)

    blk = pl.BlockSpec((tm, r), lambda i, j: (i, j))
    in_specs = [pl.BlockSpec((tm, d), lambda i, j: (i, 0)), pl.BlockSpec((None, r, d), lambda i, j: (j, ob, 0))]
    ins = [a, gw]
    if epi:
        in_specs.append(blk)
        ins.append(relu_grad_of)
    out_shape = _sds((m, N_DEV * r), out_dtype)
    if with_relu2:
        return _call(body, name=name, grid=(m // tm, N_DEV), in_specs=in_specs, out_specs=(blk, blk),
                     out_shape=(out_shape, _sds((m, N_DEV * r), BF16)), compiler_params=_cparams())(*ins)
    return _call(body, name=name, grid=(m // tm, N_DEV), in_specs=in_specs, out_specs=blk,
                 out_shape=out_shape, compiler_params=_cparams())(*ins)


def blockdiag_dw(xt, dz, name):
    t = xt.shape[1]

    def body(a_ref, b_ref, o_ref):
        o_ref[...] = _dot(a_ref[...], b_ref[...])

    return _call(body, name=name, grid=(C_BLOCKS,),
                 in_specs=[pl.BlockSpec((C_BLOCK_DIM, t), lambda g: (g, 0)),
                           pl.BlockSpec((t, C_BLOCK_DIM), lambda g: (0, g))],
                 out_specs=pl.BlockSpec((None, C_BLOCK_DIM, C_BLOCK_DIM), lambda g: (g, 0, 0)),
                 out_shape=_sds((C_BLOCKS, C_BLOCK_DIM, C_BLOCK_DIM)))(xt, dz)


def _band_mask(n, nblk, transposed):
    shape = (3 * BLOCK, BLOCK) if transposed else (BLOCK, 3 * BLOCK)
    qi = lax.broadcasted_iota(jnp.int32, shape, 1 if transposed else 0)
    kj = lax.broadcasted_iota(jnp.int32, shape, 0 if transposed else 1)
    lo = jnp.where(n > 0, 0, BLOCK)
    hi = jnp.where(n < nblk - 1, 3 * BLOCK, 2 * BLOCK)
    return (jnp.abs(kj - BLOCK - qi) <= WINDOW) & (kj >= lo) & (kj < hi)


def _band_rows(ref, n, nblk):
    starts = [jnp.maximum(n - 1, 0), n, jnp.minimum(n + 1, nblk - 1)]
    return jnp.concatenate([ref[pl.ds(pl.multiple_of(s * BLOCK, BLOCK), BLOCK), :] for s in starts], axis=0)


def attn_fwd(q, k, v, bias, sink_b, name):
    t = q.shape[0]
    nblk = t // BLOCK
    scale = A_HEAD_DIM ** -0.5

    def body(q_ref, k_ref, v_ref, b_ref, s_ref, o_ref):
        n = pl.program_id(1)
        kb = _band_rows(k_ref, n, nblk).astype(BF16)
        vb = _band_rows(v_ref, n, nblk).astype(BF16)
        mask = _band_mask(n, nblk, False)
        for j in range(A_GROUP):
            sl = slice(j * A_HEAD_DIM, (j + 1) * A_HEAD_DIM)
            s = _dot(q_ref[:, sl], kb, NT) * scale + b_ref[j]
            s = jnp.where(mask, s, NEG_INF)
            sk = s_ref[j:j + 1, 0:1]
            mx = jnp.maximum(jnp.max(s, axis=1, keepdims=True), sk)
            p = jnp.exp(s - mx)
            den = jnp.sum(p, axis=1, keepdims=True) + jnp.exp(sk - mx)
            o_ref[:, sl] = _dot(p * (1.0 / den), vb)

    gw = A_GROUP * A_HEAD_DIM
    return _call(body, name=name, grid=(A_KV_HEADS, nblk),
                 in_specs=[pl.BlockSpec((BLOCK, gw), lambda g, n: (n, g)),
                           pl.BlockSpec((t, A_HEAD_DIM), lambda g, n: (0, g)),
                           pl.BlockSpec((t, A_HEAD_DIM), lambda g, n: (0, g)),
                           pl.BlockSpec((A_GROUP, BLOCK, 3 * BLOCK), lambda g, n: (g, 0, 0)),
                           pl.BlockSpec((None, 8, 128), lambda g, n: (g, 0, 0))],
                 out_specs=pl.BlockSpec((BLOCK, gw), lambda g, n: (n, g)),
                 out_shape=_sds((t, A_Q)))(q, k, v, bias, sink_b)


def attn_bwd(q, k, v, bias, bias_t, sink_b, do, o, name):
    t = q.shape[0]
    nblk = t // BLOCK
    scale = A_HEAD_DIM ** -0.5

    def body(q_ref, k_ref, v_ref, b_ref, bt_ref, s_ref, do_ref, o_ref, dq_ref, dk_ref, dv_ref, db_ref, ds_ref):
        n = pl.program_id(1)

        @pl.when(n == 0)
        def _():
            dk_ref[...] = jnp.zeros_like(dk_ref)
            dv_ref[...] = jnp.zeros_like(dv_ref)
            db_ref[...] = jnp.zeros_like(db_ref)
            ds_ref[...] = jnp.zeros_like(ds_ref)

        kb = _band_rows(k_ref, n, nblk).astype(BF16)
        vb = _band_rows(v_ref, n, nblk).astype(BF16)
        mask = _band_mask(n, nblk, False)
        mask_t = _band_mask(n, nblk, True)
        ones8 = jnp.ones((8, A_HEAD_DIM), F32)
        dkb = jnp.zeros((3 * BLOCK, A_HEAD_DIM), F32)
        dvb = jnp.zeros((3 * BLOCK, A_HEAD_DIM), F32)
        for j in range(A_GROUP):
            sl = slice(j * A_HEAD_DIM, (j + 1) * A_HEAD_DIM)
            qj = q_ref[:, sl].astype(BF16)
            doj = do_ref[:, sl]
            doo = doj * o_ref[:, sl]
            doj = doj.astype(BF16)
            sk = s_ref[j:j + 1, 0:1]
            s = jnp.where(mask, _dot(qj, kb, NT) * scale + b_ref[j], NEG_INF)
            mx = jnp.maximum(jnp.max(s, axis=1, keepdims=True), sk)
            p = jnp.exp(s - mx)
            den = jnp.sum(p, axis=1, keepdims=True) + jnp.exp(sk - mx)
            rden = 1.0 / den
            p = p * rden
            psink = jnp.exp(sk - mx) * rden
            delta = jnp.sum(doo, axis=1, keepdims=True)
            dsc = p * (_dot(doj, vb, NT) - delta)
            db_ref[j] += dsc
            ds_ref[j:j + 1, :] += jnp.broadcast_to(-jnp.sum(psink * delta, axis=0, keepdims=True), (1, 128))
            dq_ref[:, sl] = _dot(dsc, kb) * scale
            st = jnp.where(mask_t, _dot(kb, qj, NT) * scale + bt_ref[j], NEG_INF)
            mxt = jnp.maximum(jnp.max(st, axis=0, keepdims=True), sk)
            pt = jnp.exp(st - mxt)
            dent = jnp.sum(pt, axis=0, keepdims=True) + jnp.exp(sk - mxt)
            pt = pt * (1.0 / dent)
            delta_t = _dot(ones8, doo, NT, hi=True)[0:1, :]
            dst = pt * (_dot(vb, doj, NT) - delta_t)
            dkb = dkb + _dot(dst, qj) * scale
            dvb = dvb + _dot(pt, doj)
        starts = [jnp.maximum(n - 1, 0), n, jnp.minimum(n + 1, nblk - 1)]
        for c, st_ in enumerate(starts):
            rows = pl.ds(pl.multiple_of(st_ * BLOCK, BLOCK), BLOCK)
            dk_ref[rows, :] += dkb[c * BLOCK:(c + 1) * BLOCK, :]
            dv_ref[rows, :] += dvb[c * BLOCK:(c + 1) * BLOCK, :]

    gw = A_GROUP * A_HEAD_DIM
    qspec = pl.BlockSpec((BLOCK, gw), lambda g, n: (n, g))
    kspec = pl.BlockSpec((t, A_HEAD_DIM), lambda g, n: (0, g))
    sspec = pl.BlockSpec((None, 8, 128), lambda g, n: (g, 0, 0))
    bspec = pl.BlockSpec((A_GROUP, BLOCK, 3 * BLOCK), lambda g, n: (g, 0, 0))
    btspec = pl.BlockSpec((A_GROUP, 3 * BLOCK, BLOCK), lambda g, n: (g, 0, 0))
    return _call(body, name=name, grid=(A_KV_HEADS, nblk),
                 in_specs=[qspec, kspec, kspec, bspec, btspec, sspec, qspec, qspec],
                 out_specs=(qspec, kspec, kspec, bspec, sspec),
                 out_shape=(_sds((t, A_Q)), _sds((t, A_KV)), _sds((t, A_KV)),
                            _sds((A_HEADS, BLOCK, 3 * BLOCK)), _sds((A_KV_HEADS, 8, 128))),
                 compiler_params=_cparams())(q, k, v, bias, bias_t, sink_b, do, o)


def xattn_fwd(q, k, v, name):
    t, d = q.shape
    ml = k.shape[0]
    dh = d // X_HEADS
    tq = min(256, t)
    scale = dh ** -0.5

    def body(q_ref, k_ref, v_ref, o_ref):
        s = _dot(q_ref[...], k_ref[...], NT) * scale
        p = jnp.exp(s - jnp.max(s, axis=1, keepdims=True))
        p = p * (1.0 / jnp.sum(p, axis=1, keepdims=True))
        o_ref[...] = _dot(p, v_ref[...])

    qspec = pl.BlockSpec((tq, dh), lambda h, i: (i, h))
    kspec = pl.BlockSpec((ml, dh), lambda h, i: (0, h))
    return _call(body, name=name, grid=(X_HEADS, t // tq), in_specs=[qspec, kspec, kspec], out_specs=qspec,
                 out_shape=_sds((t, d)))(q, k, v)


def xattn_bwd(q, k, v, o, do, name):
    t, d = q.shape
    ml = k.shape[0]
    dh = d // X_HEADS
    tq = min(256, t)
    scale = dh ** -0.5

    def body(q_ref, k_ref, v_ref, o_ref, do_ref, dq_ref, dk_ref, dv_ref):
        i = pl.program_id(1)
        qv, kv, vv = q_ref[...].astype(BF16), k_ref[...].astype(BF16), v_ref[...].astype(BF16)
        dov = do_ref[...]
        doo = dov * o_ref[...]
        dov = dov.astype(BF16)
        s = _dot(qv, kv, NT) * scale
        p = jnp.exp(s - jnp.max(s, axis=1, keepdims=True))
        p = p * (1.0 / jnp.sum(p, axis=1, keepdims=True))
        ds = p * (_dot(dov, vv, NT) - jnp.sum(doo, axis=1, keepdims=True))
        dq_ref[...] = _dot(ds, kv) * scale
        st = _dot(kv, qv, NT) * scale
        pt = jnp.exp(st - jnp.max(st, axis=0, keepdims=True))
        pt = pt * (1.0 / jnp.sum(pt, axis=0, keepdims=True))
        delta_t = _dot(jnp.ones((8, dh), F32), doo, NT, hi=True)[0:1, :]
        dst = pt * (_dot(vv, dov, NT) - delta_t)
        dkp = _dot(dst, qv) * scale
        dvp = _dot(pt, dov)

        @pl.when(i == 0)
        def _():
            dk_ref[...] = dkp
            dv_ref[...] = dvp

        @pl.when(i > 0)
        def _():
            dk_ref[...] += dkp
            dv_ref[...] += dvp

    qspec = pl.BlockSpec((tq, dh), lambda h, i: (i, h))
    kspec = pl.BlockSpec((ml, dh), lambda h, i: (0, h))
    return _call(body, name=name, grid=(X_HEADS, t // tq), in_specs=[qspec, kspec, kspec, qspec, qspec],
                 out_specs=(qspec, kspec, kspec),
                 out_shape=(_sds((t, d)), _sds((ml, d)), _sds((ml, d))))(q, k, v, o, do)


def scan_lead(a, u, name, reverse, inclusive):
    n, r, c = a.shape
    blk = max(1, min(n, (1 << 18) // (max(r, 8) * c)))
    while n % blk:
        blk -= 1
    nb = n // blk

    def body(a_ref, u_ref, o_ref, carry):
        @pl.when(pl.program_id(0) == 0)
        def _():
            carry[...] = jnp.zeros_like(carry)

        def step(s, h):
            idx = (blk - 1 - s) if reverse else s
            hn = a_ref[idx] * h + u_ref[idx]
            o_ref[idx] = hn if inclusive else h
            return hn

        carry[...] = lax.fori_loop(0, blk, step, carry[...])

    spec = pl.BlockSpec((blk, r, c), (lambda i: (nb - 1 - i, 0, 0)) if reverse else (lambda i: (i, 0, 0)))
    return _call(body, name=name, grid=(nb,), in_specs=[spec, spec], out_specs=spec,
                 out_shape=_sds((n, r, c)), scratch_shapes=[pltpu.VMEM((r, c), F32)])(a, u)


def _chunk_mats(bwd_dir):
    i = lax.broadcasted_iota(jnp.int32, (GLA_TILE, GLA_TILE), 0)
    j = lax.broadcasted_iota(jnp.int32, (GLA_TILE, GLA_TILE), 1)
    same = lax.shift_right_logical(i, 4) == lax.shift_right_logical(j, 4)
    if bwd_dir:
        cm, cm_t = same & (j >= i), same & (i >= j)
        mk, mk_t = same & (j > i), same & (i > j)
    else:
        cm, cm_t = same & (j <= i), same & (i <= j)
        mk, mk_t = same & (j <= i), same & (i <= j)
    f = lambda b: jnp.where(b, 1.0, 0.0).astype(F32)
    return f(cm), f(cm_t), mk, mk_t, f(same)


def gla_gates_fwd(zf, zb, w2f, b2f, w2b, b2b, name):
    t = zf.shape[0]
    tm = min(256, t)

    def body(zf_ref, zb_ref, wf_ref, bf_ref, wb_ref, bb_ref, lf_ref, lb_ref):
        lf_ref[...] = -_softplus(-(_dot(zf_ref[...], wf_ref[...], hi=True) + bf_ref[...])) / GATE_TAU
        lb_ref[...] = -_softplus(-(_dot(zb_ref[...], wb_ref[...], hi=True) + bb_ref[...])) / GATE_TAU

    zs = pl.BlockSpec((tm, GATE_RANK), lambda i: (i, 0))
    ws = pl.BlockSpec((GATE_RANK, B_QK), lambda i: (0, 0))
    bs = pl.BlockSpec((1, B_QK), lambda i: (0, 0))
    os_ = pl.BlockSpec((tm, B_QK), lambda i: (i, 0))
    return _call(body, name=name, grid=(t // tm,), in_specs=[zs, zs, ws, bs, ws, bs], out_specs=(os_, os_),
                 out_shape=(_sds((t, B_QK)),) * 2)(zf, zb, w2f, b2f.reshape(1, B_QK), w2b, b2b.reshape(1, B_QK))


def gla_gates_bwd(zf, zb, w2f, b2f, w2b, b2b, dlf, dlb, name):
    t = zf.shape[0]
    tm = min(256, t)

    def body(zf_ref, zb_ref, wf_ref, bf_ref, wb_ref, bb_ref, dlf_ref, dlb_ref,
             dzf_ref, dzb_ref, dpf_ref, dpb_ref, dbf_ref, dbb_ref):
        first = pl.program_id(0) == 0
        for z_ref, w_ref, b_ref, dl_ref, dz_ref, dp_ref, db_ref in (
                (zf_ref, wf_ref, bf_ref, dlf_ref, dzf_ref, dpf_ref, dbf_ref),
                (zb_ref, wb_ref, bb_ref, dlb_ref, dzb_ref, dpb_ref, dbb_ref)):
            pre = _dot(z_ref[...], w_ref[...], hi=True) + b_ref[...]
            dpre = dl_ref[...] * (1.0 / GATE_TAU) * _sigmoid(-pre)
            dp_ref[...] = dpre
            dz_ref[...] = _dot(dpre, w_ref[...], NT, hi=True)
            part = jnp.sum(dpre, axis=0, keepdims=True)

            @pl.when(first)
            def _():
                db_ref[...] = part

            @pl.when(jnp.logical_not(first))
            def _():
                db_ref[...] += part

    zs = pl.BlockSpec((tm, GATE_RANK), lambda i: (i, 0))
    ws = pl.BlockSpec((GATE_RANK, B_QK), lambda i: (0, 0))
    bs = pl.BlockSpec((1, B_QK), lambda i: (0, 0))
    os_ = pl.BlockSpec((tm, B_QK), lambda i: (i, 0))
    return _call(body, name=name, grid=(t // tm,), in_specs=[zs, zs, ws, bs, ws, bs, os_, os_],
                 out_specs=(zs, zs, os_, os_, bs, bs),
                 out_shape=(_sds((t, GATE_RANK)),) * 2 + (_sds((t, B_QK)),) * 2 + (_sds((1, B_QK)),) * 2)(
        zf, zb, w2f, b2f.reshape(1, B_QK), w2b, b2b.reshape(1, B_QK), dlf, dlb)


def gla_outer(xt, lat, y, name, bwd_dir, mode):
    t = y.shape[0]
    nchunk = t // GLA_CHUNK
    khat = mode == "khat"
    scale = B_KEY_DIM ** -0.5

    def body(xt_ref, lat_ref, y_ref, *outs):
        _, cm_t, _, _, same = _chunk_mats(bwd_dir)
        lat_v = lat_ref[...]
        bt = _dot(lat_v, cm_t, hi=True)
        if khat:
            mult = jnp.exp(_dot(lat_v, same, hi=True) - bt)
        else:
            mult = jnp.exp(bt) * scale
        xm = xt_ref[...] * mult
        lane = lax.shift_right_logical(lax.broadcasted_iota(jnp.int32, (1, GLA_TILE), 1), 4)
        ones = jnp.ones((GLA_TILE, B_VAL_DIM), F32)
        yv = [y_ref[:, h * B_VAL_DIM:(h + 1) * B_VAL_DIM].astype(BF16) for h in range(B_HEADS)]
        for c in range(CHUNKS_PER_TILE):
            sel = lane == c
            xc = jnp.where(sel, xm, 0.0).astype(BF16)
            for h in range(B_HEADS):
                rows = slice(h * B_KEY_DIM, (h + 1) * B_KEY_DIM)
                outs[0][c, rows, :] = _dot(xc[rows, :], yv[h])
            if khat:
                outs[1][c] = jnp.exp(_dot(jnp.where(sel, lat_v, 0.0), ones, hi=True))

    tspec = pl.BlockSpec((B_QK, GLA_TILE), lambda i: (0, i))
    ospec = pl.BlockSpec((CHUNKS_PER_TILE, B_QK, B_VAL_DIM), lambda i: (i, 0, 0))
    oshape = _sds((nchunk, B_QK, B_VAL_DIM))
    return _call(body, name=name, grid=(t // GLA_TILE,),
                 in_specs=[tspec, tspec, pl.BlockSpec((GLA_TILE, B_V), lambda i: (i, 0))],
                 out_specs=(ospec, ospec) if khat else ospec,
                 out_shape=(oshape, oshape) if khat else oshape)(xt, lat, y)


def _head_lane_mask(h):
    lane = lax.broadcasted_iota(jnp.int32, (1, B_QK), 1)
    return lax.shift_right_logical(lane, 6) == h


def _chunk_rows(c):
    return slice(c * GLA_CHUNK, (c + 1) * GLA_CHUNK)


def gla_inner_fwd(q, k, v, la, sp, name, bwd_dir):
    t = q.shape[0]
    scale = B_KEY_DIM ** -0.5

    def body(q_ref, k_ref, v_ref, la_ref, sp_ref, o_ref):
        cm, _, mk, _, _ = _chunk_mats(bwd_dir)
        b = _dot(cm, la_ref[...], hi=True)
        qt = q_ref[...] * scale * jnp.exp(b)
        kt = k_ref[...] * jnp.exp(jnp.minimum(-b, EXP_CLAMP))
        spb = [sp_ref[c].astype(BF16) for c in range(CHUNKS_PER_TILE)]
        for h in range(B_HEADS):
            lm = _head_lane_mask(h)
            qm = jnp.where(lm, qt, 0.0).astype(BF16)
            km = jnp.where(lm, kt, 0.0).astype(BF16)
            vs = slice(h * B_VAL_DIM, (h + 1) * B_VAL_DIM)
            att = jnp.where(mk, _dot(qm, km, NT), 0.0)
            inter = jnp.concatenate([_dot(qm[_chunk_rows(c), :], spb[c]) for c in range(CHUNKS_PER_TILE)], axis=0)
            o_ref[:, vs] = _dot(att, v_ref[:, vs]) + inter

    qs = pl.BlockSpec((GLA_TILE, B_QK), lambda i: (i, 0))
    vs_ = pl.BlockSpec((GLA_TILE, B_V), lambda i: (i, 0))
    ss = pl.BlockSpec((CHUNKS_PER_TILE, B_QK, B_VAL_DIM), lambda i: (i, 0, 0))
    return _call(body, name=name, grid=(t // GLA_TILE,), in_specs=[qs, qs, vs_, qs, ss], out_specs=vs_,
                 out_shape=_sds((t, B_V)))(q, k, v, la, sp)


def gla_inner_bwd(q, k, v, la, do, sp, gs, dec, name, bwd_dir, add=None):
    t = q.shape[0]
    scale = B_KEY_DIM ** -0.5
    hasadd = add is not None

    def body(*refs):
        it = iter(refs)
        q_ref, k_ref, v_ref, la_ref, do_ref, sp_ref, gs_ref, dec_ref = [next(it) for _ in range(8)]
        adds = [next(it) for _ in range(3)] if hasadd else None
        dq_ref, dk_ref, dv_ref, dla_ref = [next(it) for _ in range(4)]
        cm, cm_t, mk, mk_t, same = _chunk_mats(bwd_dir)
        la_v = la_ref[...]
        b = _dot(cm, la_v, hi=True)
        btot = _dot(same, la_v, hi=True)
        eb = jnp.exp(b)
        ek = jnp.exp(jnp.minimum(-b, EXP_CLAMP))
        ekh = jnp.exp(btot - b)
        qt = q_ref[...] * scale * eb
        kt = k_ref[...] * ek
        kh = k_ref[...] * ekh
        spb = [sp_ref[c].astype(BF16) for c in range(CHUNKS_PER_TILE)]
        gsb = [gs_ref[c].astype(BF16) for c in range(CHUNKS_PER_TILE)]
        dqt = jnp.zeros((GLA_TILE, B_QK), F32)
        dkt = jnp.zeros((GLA_TILE, B_QK), F32)
        dkh = jnp.zeros((GLA_TILE, B_QK), F32)
        for h in range(B_HEADS):
            lm = _head_lane_mask(h)
            qm = jnp.where(lm, qt, 0.0).astype(BF16)
            km = jnp.where(lm, kt, 0.0).astype(BF16)
            khm = jnp.where(lm, kh, 0.0).astype(BF16)
            vs = slice(h * B_VAL_DIM, (h + 1) * B_VAL_DIM)
            vh = v_ref[:, vs].astype(BF16)
            doh = do_ref[:, vs].astype(BF16)
            da = jnp.where(mk, _dot(doh, vh, NT), 0.0)
            da_t = jnp.where(mk_t, _dot(vh, doh, NT), 0.0)
            att_t = jnp.where(mk_t, _dot(km, qm, NT), 0.0)
            dv_h = _dot(att_t, doh) + jnp.concatenate(
                [_dot(khm[_chunk_rows(c), :], gsb[c]) for c in range(CHUNKS_PER_TILE)], axis=0)
            if hasadd:
                dv_h = dv_h + adds[2][:, vs]
            dv_ref[:, vs] = dv_h
            dq_inter = jnp.concatenate(
                [_dot(doh[_chunk_rows(c), :], spb[c], NT) for c in range(CHUNKS_PER_TILE)], axis=0)
            dqt = dqt + _dot(da, km) + jnp.where(lm, dq_inter, 0.0)
            dkt = dkt + _dot(da_t, qm)
            dkh_inter = jnp.concatenate(
                [_dot(vh[_chunk_rows(c), :], gsb[c], NT) for c in range(CHUNKS_PER_TILE)], axis=0)
            dkh = dkh + jnp.where(lm, dkh_inter, 0.0)
        dq = dqt * scale * eb
        dk = dkt * ek + dkh * ekh
        if hasadd:
            dq = dq + adds[0][...]
            dk = dk + adds[1][...]
        dq_ref[...] = dq
        dk_ref[...] = dk
        db = dqt * qt - dkt * kt - dkh * kh
        ones16 = jnp.ones((GLA_CHUNK, B_VAL_DIM), F32)
        t2 = jnp.concatenate(
            [_dot(ones16, gs_ref[c] * dec_ref[c] * sp_ref[c], NT, hi=True) for c in range(CHUNKS_PER_TILE)], axis=0)
        dla_ref[...] = _dot(cm_t, db, hi=True) + _dot(same, dkh * kh, hi=True) + t2

    qs = pl.BlockSpec((GLA_TILE, B_QK), lambda i: (i, 0))
    vs_ = pl.BlockSpec((GLA_TILE, B_V), lambda i: (i, 0))
    ss = pl.BlockSpec((CHUNKS_PER_TILE, B_QK, B_VAL_DIM), lambda i: (i, 0, 0))
    ins = [q, k, v, la, do, sp, gs, dec] + (list(add) if hasadd else [])
    in_specs = [qs, qs, vs_, qs, vs_, ss, ss, ss] + ([qs, qs, vs_] if hasadd else [])
    return _call(body, name=name, grid=(t // GLA_TILE,), in_specs=in_specs, out_specs=(qs, qs, vs_, qs),
                 out_shape=(_sds((t, B_QK)), _sds((t, B_QK)), _sds((t, B_V)), _sds((t, B_QK))),
                 compiler_params=_cparams())(*ins)


def gla_out_fwd(of, ob, g, gn, name):
    t = of.shape[0]
    tm = min(256, t)

    def body(of_ref, ob_ref, g_ref, gn_ref, o_ref):
        for h in range(B_HEADS):
            vs = slice(h * B_VAL_DIM, (h + 1) * B_VAL_DIM)
            o = of_ref[:, vs] + ob_ref[:, vs]
            on = o * lax.rsqrt(jnp.mean(o * o, axis=1, keepdims=True) + EPS)
            gv = g_ref[:, vs]
            o_ref[:, vs] = on * gn_ref[:, vs] * (gv * _sigmoid(gv))

    row = pl.BlockSpec((tm, B_V), lambda i: (i, 0))
    vec = pl.BlockSpec((1, B_V), lambda i: (0, 0))
    return _call(body, name=name, grid=(t // tm,), in_specs=[row, row, row, vec], out_specs=row,
                 out_shape=_sds((t, B_V)))(of, ob, g, gn.reshape(1, B_V))


def gla_out_bwd(of, ob, g, gn, dout, name):
    t = of.shape[0]
    tm = min(256, t)

    def body(of_ref, ob_ref, g_ref, gn_ref, d_ref, do_ref, dg_ref, dgn_ref):
        first = pl.program_id(0) == 0
        for h in range(B_HEADS):
            vs = slice(h * B_VAL_DIM, (h + 1) * B_VAL_DIM)
            o = of_ref[:, vs] + ob_ref[:, vs]
            r = lax.rsqrt(jnp.mean(o * o, axis=1, keepdims=True) + EPS)
            on = o * r
            gv = g_ref[:, vs]
            sg = _sigmoid(gv)
            silu = gv * sg
            dv = d_ref[:, vs]
            gnv = gn_ref[:, vs]
            dg_ref[:, vs] = dv * on * gnv * (sg * (1.0 + gv * (1.0 - sg)))
            don = dv * silu * gnv
            do_ref[:, vs] = r * (don - on * jnp.mean(don * on, axis=1, keepdims=True))
            part = jnp.sum(dv * silu * on, axis=0, keepdims=True)

            @pl.when(first)
            def _():
                dgn_ref[:, vs] = part

            @pl.when(jnp.logical_not(first))
            def _():
                dgn_ref[:, vs] += part

    row = pl.BlockSpec((tm, B_V), lambda i: (i, 0))
    vec = pl.BlockSpec((1, B_V), lambda i: (0, 0))
    return _call(body, name=name, grid=(t // tm,), in_specs=[row, row, row, vec, row], out_specs=(row, row, vec),
                 out_shape=(_sds((t, B_V)), _sds((t, B_V)), _sds((1, B_V))))(of, ob, g, gn.reshape(1, B_V), dout)


def _shift(x, k):
    if k > 0:
        return jnp.concatenate([x[k:], jnp.zeros((k,) + x.shape[1:], x.dtype)], axis=0)
    return jnp.concatenate([jnp.zeros((-k,) + x.shape[1:], x.dtype), x[:k]], axis=0)


def _lru_gates(xc, s, wa_ref, ba_ref, wx_ref, bx_ref, lam_ref):
    cols = [slice(g * C_BLOCK_DIM, (g + 1) * C_BLOCK_DIM) for g in range(C_BLOCKS)]
    zr = jnp.concatenate([_dot(xc[:, cs], wa_ref[s, g]) for g, cs in enumerate(cols)], axis=1) + ba_ref[s:s + 1, :]
    zi = jnp.concatenate([_dot(xc[:, cs], wx_ref[s, g]) for g, cs in enumerate(cols)], axis=1) + bx_ref[s:s + 1, :]
    r = _sigmoid(zr)
    i = _sigmoid(zi)
    sp = _softplus(-lam_ref[s:s + 1, :])
    log_a = -LRU_C * r * sp
    return r, i, sp, log_a


def lru_gates_fwd(x0, xm2, xm1, xp1, cw, cb, wa, ba, wx, bx, lam, name):
    t = x0.shape[0]
    tm = min(256, t)

    def body(x0_ref, xm2_ref, xm1_ref, xp1_ref, cw_ref, cb_ref, wa_ref, ba_ref, wx_ref, bx_ref, lam_ref,
             xc_ref, a0_ref, u0_ref, a1_ref, u1_ref):
        xc = (xm2_ref[...] * cw_ref[0:1, :] + xm1_ref[...] * cw_ref[1:2, :] + x0_ref[...] * cw_ref[2:3, :]
              + xp1_ref[...] * cw_ref[3:4, :] + cb_ref[...])
        xc_ref[...] = xc
        for s, (a_ref, u_ref) in enumerate(((a0_ref, u0_ref), (a1_ref, u1_ref))):
            _, i, _, log_a = _lru_gates(xc, s, wa_ref, ba_ref, wx_ref, bx_ref, lam_ref)
            a_ref[...] = jnp.exp(log_a)
            u_ref[...] = jnp.sqrt(-_expm1(2.0 * log_a)) * (i * xc)

    row = pl.BlockSpec((tm, C_WIDTH), lambda i: (i, 0))
    full = lambda shape: pl.BlockSpec(shape, lambda i: (0,) * len(shape))
    wshape = (2, C_BLOCKS, C_BLOCK_DIM, C_BLOCK_DIM)
    return _call(body, name=name, grid=(t // tm,),
                 in_specs=[row] * 4 + [full((4, C_WIDTH)), full((1, C_WIDTH)), full(wshape), full((2, C_WIDTH)),
                                       full(wshape), full((2, C_WIDTH)), full((2, C_WIDTH))],
                 out_specs=(row,) * 5, out_shape=(_sds((t, C_WIDTH)),) * 5)(
        x0, xm2, xm1, xp1, cw, cb.reshape(1, C_WIDTH), wa, ba, wx, bx, lam)


def lru_gates_bwd(xc, g0, hs0, g1, hs1, wa, ba, wx, bx, lam, name):
    t = xc.shape[0]
    tm = min(256, t)

    def body(xc_ref, g0_ref, hs0_ref, g1_ref, hs1_ref, wa_ref, ba_ref, wx_ref, bx_ref, lam_ref,
             dxc_ref, dzr0_ref, dzi0_ref, dzr1_ref, dzi1_ref, dlam_ref, dba_ref, dbx_ref):
        first = pl.program_id(0) == 0

        @pl.when(first)
        def _():
            dlam_ref[...] = jnp.zeros_like(dlam_ref)
            dba_ref[...] = jnp.zeros_like(dba_ref)
            dbx_ref[...] = jnp.zeros_like(dbx_ref)

        xcv = xc_ref[...]
        dxc = jnp.zeros_like(xcv)
        cols = [slice(g * C_BLOCK_DIM, (g + 1) * C_BLOCK_DIM) for g in range(C_BLOCKS)]
        for s, (g_ref, hs_ref, dzr_ref, dzi_ref) in enumerate(
                ((g0_ref, hs0_ref, dzr0_ref, dzi0_ref), (g1_ref, hs1_ref, dzr1_ref, dzi1_ref))):
            r, i, sp, log_a = _lru_gates(xcv, s, wa_ref, ba_ref, wx_ref, bx_ref, lam_ref)
            du = g_ref[...]
            da = du * hs_ref[...]
            a = jnp.exp(log_a)
            e2 = jnp.exp(2.0 * log_a)
            c = jnp.sqrt(-_expm1(2.0 * log_a))
            ix = i * xcv
            dlog = da * a - du * ix * (e2 / c)
            dix = du * c
            dxc = dxc + dix * i
            dzi = dix * xcv * i * (1.0 - i)
            dzr = dlog * (-LRU_C * sp) * r * (1.0 - r)
            dzr_ref[...] = dzr
            dzi_ref[...] = dzi
            dxc = dxc + jnp.concatenate(
                [_dot(dzr[:, cs], wa_ref[s, g], NT) + _dot(dzi[:, cs], wx_ref[s, g], NT) for g, cs in enumerate(cols)],
                axis=1)
            dsp = jnp.sum(dlog * (-LRU_C * r), axis=0, keepdims=True)
            dlam_ref[s:s + 1, :] += dsp * (-_sigmoid(-lam_ref[s:s + 1, :]))
            dba_ref[s:s + 1, :] += jnp.sum(dzr, axis=0, keepdims=True)
            dbx_ref[s:s + 1, :] += jnp.sum(dzi, axis=0, keepdims=True)
        dxc_ref[...] = dxc

    row = pl.BlockSpec((tm, C_WIDTH), lambda i: (i, 0))
    full = lambda shape: pl.BlockSpec(shape, lambda i: (0,) * len(shape))
    wshape = (2, C_BLOCKS, C_BLOCK_DIM, C_BLOCK_DIM)
    vec2 = full((2, C_WIDTH))
    return _call(body, name=name, grid=(t // tm,),
                 in_specs=[row] * 5 + [full(wshape), vec2, full(wshape), vec2, vec2],
                 out_specs=(row,) * 5 + (vec2,) * 3,
                 out_shape=(_sds((t, C_WIDTH)),) * 5 + (_sds((2, C_WIDTH)),) * 3)(
        xc, g0, hs0, g1, hs1, wa, ba, wx, bx, lam)


def lru_out_fwd(h0, h1, y, name):
    t = y.shape[0]
    tm = min(256, t)

    def body(h0_ref, h1_ref, y_ref, o_ref):
        o_ref[...] = (h0_ref[...] + h1_ref[...]) * _gelu(y_ref[...])

    row = pl.BlockSpec((tm, C_WIDTH), lambda i: (i, 0))
    return _call(body, name=name, grid=(t // tm,), in_specs=[row] * 3, out_specs=row,
                 out_shape=_sds((t, C_WIDTH)))(h0, h1, y)


def lru_out_bwd(h0, h1, y, dout, name):
    t = y.shape[0]
    tm = min(256, t)

    def body(h0_ref, h1_ref, y_ref, d_ref, dh_ref, dy_ref):
        yv = y_ref[...]
        dv = d_ref[...]
        dh_ref[...] = dv * _gelu(yv)
        dy_ref[...] = dv * (h0_ref[...] + h1_ref[...]) * _gelu_grad(yv)

    row = pl.BlockSpec((tm, C_WIDTH), lambda i: (i, 0))
    return _call(body, name=name, grid=(t // tm,), in_specs=[row] * 4, out_specs=(row, row),
                 out_shape=(_sds((t, C_WIDTH)),) * 2)(h0, h1, y, dout)


def conv_bwd(dxc, dp2, dp1, dm1, x0, xm2, xm1, xp1, cw, name):
    t = x0.shape[0]
    tm = min(256, t)

    def body(d_ref, dp2_ref, dp1_ref, dm1_ref, x0_ref, xm2_ref, xm1_ref, xp1_ref, cw_ref, dx_ref, dcw_ref, dcb_ref):
        @pl.when(pl.program_id(0) == 0)
        def _():
            dcw_ref[...] = jnp.zeros_like(dcw_ref)
            dcb_ref[...] = jnp.zeros_like(dcb_ref)

        dv = d_ref[...]
        dx_ref[...] = (dp2_ref[...] * cw_ref[0:1, :] + dp1_ref[...] * cw_ref[1:2, :] + dv * cw_ref[2:3, :]
                       + dm1_ref[...] * cw_ref[3:4, :])
        for j, x_ref in enumerate((xm2_ref, xm1_ref, x0_ref, xp1_ref)):
            dcw_ref[j:j + 1, :] += jnp.sum(dv * x_ref[...], axis=0, keepdims=True)
        dcb_ref[...] += jnp.sum(dv, axis=0, keepdims=True)

    row = pl.BlockSpec((tm, C_WIDTH), lambda i: (i, 0))
    cws = pl.BlockSpec((4, C_WIDTH), lambda i: (0, 0))
    cbs = pl.BlockSpec((1, C_WIDTH), lambda i: (0, 0))
    return _call(body, name=name, grid=(t // tm,), in_specs=[row] * 8 + [cws], out_specs=(row, cws, cbs),
                 out_shape=(_sds((t, C_WIDTH)), _sds((4, C_WIDTH)), _sds((1, C_WIDTH))))(
        dxc, dp2, dp1, dm1, x0, xm2, xm1, xp1, cw)


def _my_place():
    return lax.axis_index("x"), lax.axis_index("y"), lax.axis_index("c")


def all_gather(xs, name):
    r, c = xs.shape

    def body(x_ref, out_ref, send_sems, recv_sems, local_sem):
        x, y, cc = _my_place()
        me, sibling = (x, y, cc), (x, y, 1 - cc)
        chips = [(1 - x, y), (x, 1 - y), (1 - x, 1 - y)]

        def slot(px, py, pc):
            return out_ref.at[4 * px + 2 * py + pc]

        def copy(k, block, to, src=None):
            return pltpu.make_async_remote_copy(
                src_ref=slot(*block) if src is None else src, dst_ref=slot(*block),
                send_sem=send_sems.at[k], recv_sem=recv_sems.at[k], device_id=to, device_id_type=MESH)

        mine = pltpu.make_async_copy(x_ref, slot(*me), local_sem)
        mine.start()
        first = [copy(0, me, sibling, src=x_ref)]
        first += [copy(1 + j, me, (*chip, cc), src=x_ref) for j, chip in enumerate(chips)]
        for cp in first:
            cp.start()
        passed = [copy(4 + j, (*chip, cc), sibling) for j, chip in enumerate(chips)]
        for j, chip in enumerate(chips):
            copy(1 + j, (*chip, cc), me).wait_recv()
            passed[j].start()
        copy(0, sibling, me).wait_recv()
        for j, chip in enumerate(chips):
            copy(4 + j, (*chip, 1 - cc), me).wait_recv()
        for cp in first + passed:
            cp.wait_send()
        mine.wait()

    return _call(body, name=name, in_specs=[pl.BlockSpec(memory_space=pl.ANY)],
                 out_specs=pl.BlockSpec(memory_space=pl.ANY), out_shape=_sds((N_DEV, r, c), xs.dtype),
                 scratch_shapes=[pltpu.SemaphoreType.DMA((7,)), pltpu.SemaphoreType.DMA((7,)),
                                 pltpu.SemaphoreType.DMA])(xs)


def _stream_rows(r):
    nch = SIBLING_STREAMS // 4 if r % (8 * (SIBLING_STREAMS // 4)) == 0 else 1
    return nch, r // nch


def exchange_sibling(gw, name):
    _, r, c = gw.shape
    g5 = gw.reshape(4, 2, r, c)
    nch, rows = _stream_rows(r)

    def body(g_ref, out_ref, send_sems, recv_sems):
        x, y, cc = _my_place()
        swaps = []
        for q in range(4):
            for s in range(nch):
                k = q * nch + s
                win = pl.ds(s * rows, rows)
                swaps.append(pltpu.make_async_remote_copy(
                    src_ref=g_ref.at[q, 1 - cc, win], dst_ref=out_ref.at[q, win], send_sem=send_sems.at[k],
                    recv_sem=recv_sems.at[k], device_id=(x, y, 1 - cc), device_id_type=MESH))
        for cp in swaps:
            cp.start()
        for cp in swaps:
            cp.wait()

    nsem = 4 * nch
    return _call(body, name=name, in_specs=[pl.BlockSpec(memory_space=pl.ANY)],
                 out_specs=pl.BlockSpec(memory_space=pl.ANY), out_shape=_sds((4, r, c), gw.dtype),
                 scratch_shapes=[pltpu.SemaphoreType.DMA((nsem,)), pltpu.SemaphoreType.DMA((nsem,))])(g5)


HBM_SPEC = pl.BlockSpec(memory_space=pltpu.HBM)
SEM_SPEC = pl.BlockSpec(memory_space=pltpu.SEMAPHORE)
DATAFLOW = pltpu.SideEffectType.DATAFLOW_SIDE_EFFECTING


def _hbm(a):
    return pltpu.with_memory_space_constraint(a, pltpu.HBM)


def _peers(x, y, cc):
    return [(x, y, 1 - cc), (1 - x, y, cc), (x, 1 - y, cc), (1 - x, 1 - y, cc)]


def _slot(p):
    return 4 * p[0] + 2 * p[1] + p[2]


def gather_start(blk, name):
    r, c = blk.shape

    def body(v_ref, land_ref, send_sems, recv_sems, v_thru, land_thru, token):
        x, y, cc = _my_place()
        for k, to in enumerate(_peers(x, y, cc)):
            pltpu.make_async_remote_copy(
                src_ref=v_ref, dst_ref=land_ref.at[_slot((x, y, cc))], send_sem=send_sems.at[k],
                recv_sem=recv_sems.at[k], device_id=to, device_id_type=MESH).start()
        token[...] = jnp.zeros_like(token)

    return _call(
        body, name=name,
        out_shape=(pltpu.SemaphoreType.DMA((4,)), pltpu.SemaphoreType.DMA((4,)), pltpu.HBM((r, c), blk.dtype),
                   pltpu.HBM((N_DEV, r, c), blk.dtype), _sds((8, 128))),
        in_specs=(HBM_SPEC, HBM_SPEC),
        out_specs=(SEM_SPEC, SEM_SPEC, HBM_SPEC, HBM_SPEC, pl.BlockSpec(memory_space=pltpu.VMEM)),
        input_output_aliases={0: 2, 1: 3},
        compiler_params=pltpu.CompilerParams(has_side_effects=DATAFLOW),
    )(_hbm(blk), _hbm(lax.empty((N_DEV, r, c), blk.dtype)))


def gather_wait(send_sems, recv_sems, v_thru, land_thru, after, name):
    def body(v_ref, land_ref, send_sems, recv_sems, after_ref, v_out, land_out):
        x, y, cc = _my_place()
        for k, peer in enumerate(_peers(x, y, cc)):
            cp = pltpu.make_async_remote_copy(
                src_ref=v_ref, dst_ref=land_ref.at[_slot(peer)], send_sem=send_sems.at[k], recv_sem=recv_sems.at[k],
                device_id=peer, device_id_type=MESH)
            cp.wait_send()
            cp.wait_recv()

    return _call(
        body, name=name,
        out_shape=(pltpu.HBM(v_thru.shape, v_thru.dtype), pltpu.HBM(land_thru.shape, land_thru.dtype)),
        in_specs=(HBM_SPEC, HBM_SPEC, SEM_SPEC, SEM_SPEC, pl.BlockSpec(memory_space=pl.ANY)),
        out_specs=(HBM_SPEC, HBM_SPEC), input_output_aliases={0: 0, 1: 1},
        compiler_params=pltpu.CompilerParams(has_side_effects=DATAFLOW),
    )(v_thru, land_thru, send_sems, recv_sems, after)


def gather_pass(land, name):
    _, r, c = land.shape
    nch, rows = _stream_rows(r)

    def body(land_ref, out_ref, send_sems, recv_sems):
        x, y, cc = _my_place()
        peers = _peers(x, y, cc)
        copies = []
        for j in range(3):
            mine, theirs = _slot(peers[1 + j]), _slot((peers[1 + j][0], peers[1 + j][1], 1 - cc))
            for s in range(nch):
                k = j * nch + s
                win = pl.ds(s * rows, rows)
                send = pltpu.make_async_remote_copy(
                    src_ref=land_ref.at[mine, win], dst_ref=out_ref.at[mine, win], send_sem=send_sems.at[k],
                    recv_sem=recv_sems.at[k], device_id=peers[0], device_id_type=MESH)
                recv = pltpu.make_async_remote_copy(
                    src_ref=land_ref.at[mine, win], dst_ref=out_ref.at[theirs, win], send_sem=send_sems.at[k],
                    recv_sem=recv_sems.at[k], device_id=peers[0], device_id_type=MESH)
                copies.append((send, recv))
        for send, _ in copies:
            send.start()
        for send, recv in copies:
            send.wait_send()
            recv.wait_recv()

    nsem = 3 * nch
    return _call(body, name=name, in_specs=[pl.BlockSpec(memory_space=pl.ANY)],
                 out_specs=pl.BlockSpec(memory_space=pl.ANY), out_shape=_sds(land.shape, land.dtype),
                 input_output_aliases={0: 0},
                 scratch_shapes=[pltpu.SemaphoreType.DMA((nsem,)), pltpu.SemaphoreType.DMA((nsem,))])(land)


def chips_start(p, name):
    _, r, c = p.shape

    def body(p_ref, land_ref, send_sems, recv_sems, p_thru, land_thru, token):
        x, y, cc = _my_place()
        for j, (px, py, pc) in enumerate(_peers(x, y, cc)[1:]):
            pltpu.make_async_remote_copy(
                src_ref=p_ref.at[2 * px + py], dst_ref=land_ref.at[j], send_sem=send_sems.at[j],
                recv_sem=recv_sems.at[j], device_id=(px, py, pc), device_id_type=MESH).start()
        token[...] = jnp.zeros_like(token)

    return _call(
        body, name=name,
        out_shape=(pltpu.SemaphoreType.DMA((3,)), pltpu.SemaphoreType.DMA((3,)), pltpu.HBM(p.shape, p.dtype),
                   pltpu.HBM((3, r, c), p.dtype), _sds((8, 128))),
        in_specs=(HBM_SPEC, HBM_SPEC),
        out_specs=(SEM_SPEC, SEM_SPEC, HBM_SPEC, HBM_SPEC, pl.BlockSpec(memory_space=pltpu.VMEM)),
        input_output_aliases={0: 2, 1: 3},
        compiler_params=pltpu.CompilerParams(has_side_effects=DATAFLOW),
    )(_hbm(p), _hbm(lax.empty((3, r, c), p.dtype)))


def chips_wait(send_sems, recv_sems, p_thru, land_thru, after, name):
    def body(p_ref, land_ref, send_sems, recv_sems, after_ref, p_out, land_out):
        x, y, cc = _my_place()
        for j, (px, py, pc) in enumerate(_peers(x, y, cc)[1:]):
            cp = pltpu.make_async_remote_copy(
                src_ref=p_ref.at[2 * px + py], dst_ref=land_ref.at[j], send_sem=send_sems.at[j],
                recv_sem=recv_sems.at[j], device_id=(px, py, pc), device_id_type=MESH)
            cp.wait_send()
            cp.wait_recv()

    return _call(
        body, name=name,
        out_shape=(pltpu.HBM(p_thru.shape, p_thru.dtype), pltpu.HBM(land_thru.shape, land_thru.dtype)),
        in_specs=(HBM_SPEC, HBM_SPEC, SEM_SPEC, SEM_SPEC, pl.BlockSpec(memory_space=pl.ANY)),
        out_specs=(HBM_SPEC, HBM_SPEC), input_output_aliases={0: 0, 1: 1},
        compiler_params=pltpu.CompilerParams(has_side_effects=DATAFLOW),
    )(p_thru, land_thru, send_sems, recv_sems, after)


def reduce_scatter_begin(gw, name, tag):
    _, r, c = gw.shape
    theirs = exchange_sibling(gw, name + "_sibling")
    chip_sum = add_own_half(gw.reshape(4, 2, r, c), theirs, name + "_add2")
    return chips_start(chip_sum, name + "_start" + tag)


def reduce_scatter_end(started, after, name, tag):
    send_sems, recv_sems, p_thru, land_thru, _ = started
    parts, land = chips_wait(send_sems, recv_sems, p_thru, land_thru, after, name + "_wait" + tag)
    mine = lax.dynamic_index_in_dim(parts, 2 * lax.axis_index("x") + lax.axis_index("y"), axis=0, keepdims=False)
    return add_own_lead(mine, land, name + "_add4")


def _pack(arrs):
    flat = jnp.concatenate([a.reshape(-1).astype(F32) for a in arrs])
    n = flat.shape[0]
    pad = (-n) % PACK_ELEMS
    return jnp.pad(flat, (0, pad)).reshape(-1, 128)


def _unpack(packed, shapes):
    flat = packed.reshape(-1)
    out, off = [], 0
    for s in shapes:
        n = int(np.prod(s))
        out.append(lax.optimization_barrier(flat[off:off + n]).reshape(s))
        off += n
    return out


def _t5_bucket(rel):
    nb = N_BUCKETS // 2
    max_exact = nb // 2
    ret = jnp.where(rel > 0, nb, 0)
    n = jnp.abs(rel)
    nf = jnp.maximum(n, 1).astype(jnp.float32)
    large = max_exact + (jnp.log(nf / max_exact) / math.log(MAX_DISTANCE / max_exact)
                         * (nb - max_exact)).astype(jnp.int32)
    large = jnp.minimum(large, nb - 1)
    return ret + jnp.where(n < max_exact, n, large)


SMALL_SHARDED = ("gla_w2_f", "gla_w2_b", "conv_w", "lru_ba", "lru_bx", "lru_lambda")
SMALL_REPL = ("rel_bias", "attn_sink", "gla_b2_f", "gla_b2_b", "gla_norm", "conv_b", "lru_wa", "lru_wx",
              "norm_mix_pre", "norm_mix_post", "norm_mem", "norm_x_pre", "norm_x_post", "norm_ff_pre", "norm_ff_post")
BIG = ("w_in", "w_out", "xq", "xk", "xv", "xo", "w_up", "w_down")
WEIGHTS = ['rel_bias', 'w_in', 'w_out', 'attn_sink', 'gla_w2_f', 'gla_b2_f', 'gla_w2_b', 'gla_b2_b', 'gla_norm',
           'conv_w', 'conv_b', 'lru_wa', 'lru_ba', 'lru_wx', 'lru_bx', 'lru_lambda', 'xq', 'xk', 'xv', 'xo', 'w_up',
           'w_down', 'norm_mix_pre', 'norm_mix_post', 'norm_mem', 'norm_x_pre', 'norm_x_post', 'norm_ff_pre',
           'norm_ff_post']


def _step(x, mem, loss_target, w, m, v):
    depth = w["w_in"].shape[0]
    t, d = x.shape[1], x.shape[2]
    ml = mem.shape[1]
    rx = d // N_DEV
    rf = w["w_up"].shape[2]
    r_out = D_MIX // N_DEV
    x = x.reshape(t, d)
    mem = mem.reshape(ml, d)
    loss_target = loss_target.reshape(t, d)
    my_idx = 4 * lax.axis_index("x") + 2 * lax.axis_index("y") + lax.axis_index("c")

    off_in = 0
    off_up, off_down, off_out = 0, rf, 2 * rf
    off_xq = off_out + r_out
    off_xk, off_xv, off_xo = off_xq + rx, off_xq + 2 * rx, off_xq + 3 * rx
    r_rest = off_xo + rx

    sh_shapes = [w[n].shape for n in SMALL_SHARDED]
    gathered = all_gather(_pack([w[n] for n in SMALL_SHARDED]), "ag_small")
    per_dev = [_unpack(gathered[j], sh_shapes) for j in range(N_DEV)]
    full = {n: jnp.concatenate([per_dev[j][i] for j in range(N_DEV)], axis=-1) for i, n in enumerate(SMALL_SHARDED)}
    for n in SMALL_REPL:
        full[n] = w[n]

    ag_started = []
    for l in range(depth):
        blk_in = jnp.pad(w["w_in"][l].T, ((0, W_IN_ROWS - W_IN_SHARD), (0, 0))).astype(BF16)
        blk_rest = jnp.concatenate([w["w_up"][l].T, w["w_down"][l], w["w_out"][l], w["xq"][l], w["xk"][l],
                                    w["xv"][l], w["xo"][l]], axis=0).astype(BF16)
        blk_in, _ = lax.optimization_barrier((blk_in, gathered if l == 0 else ag_started[-1][1][4]))
        start_in = gather_start(blk_in, "ag_start_in%d" % l)
        blk_rest, _ = lax.optimization_barrier((blk_rest, start_in[4]))
        ag_started.append((start_in, gather_start(blk_rest, "ag_start_rest%d" % l)))
    x = x + sum(st[4][0, 0] for pair in ag_started for st in pair)
    gws = [None] * depth

    def gather_finish(started, after, name):
        send_sems, recv_sems, blk_thru, land_thru, _ = started
        blk_done, land = gather_wait(send_sems, recv_sems, blk_thru, land_thru, after, name)
        land = gather_pass(land, "ag_pass")
        return lax.dynamic_update_index_in_dim(land, blk_done, my_idx, 0)

    qi = jnp.arange(BLOCK)[:, None]
    kj = jnp.arange(3 * BLOCK)[None, :]
    onehot_t = (jnp.arange(N_BUCKETS)[:, None] == _t5_bucket(kj - BLOCK - qi).reshape(1, -1)).astype(F32)
    bias = mm_plain(full["rel_bias"].T, onehot_t, "rel_bias_lookup", hi=True, tn=3 * BLOCK * 16)
    bias = bias.reshape(A_HEADS, BLOCK, 3 * BLOCK)
    bias_t = jnp.transpose(bias, (0, 2, 1))

    def sink_rows(sink):
        s = jnp.broadcast_to(sink.reshape(A_KV_HEADS, A_GROUP, 1), (A_KV_HEADS, A_GROUP, 128))
        return jnp.pad(s, ((0, 0), (0, 8 - A_GROUP), (0, 0)))

    def split_proj(p):
        outs, off = [], 0
        for s in SPLIT_SIZES:
            outs.append(p[:, off:off + s])
            off += s
        return outs

    def lead(a):
        return a.reshape(a.shape[0], C_WIDTH // 128, 128)

    saved = []
    h = rms_fwd(x, full["norm_mix_pre"][0], "rms_first")
    for l in range(depth):
        gw_in = gather_finish(ag_started[l][0], x, "ag_wait_in%d" % l)
        sv = {"x": x, "h_in": h}
        proj_pad = mm_wn(h, gw_in, off_in, W_IN_ROWS, "mm_w_in")
        proj = jnp.concatenate([proj_pad[:, j * W_IN_ROWS:j * W_IN_ROWS + W_IN_SHARD] for j in range(N_DEV)], axis=1)
        aq, ak, av, bq, bk, bv, bg, zf, zb, cx, cy = split_proj(proj)
        sv.update(aq=aq, ak=ak, av=av, bq=bq, bk=bk, bv=bv, bg=bg, zf=zf, zb=zb, cx=cx, cy=cy)
        sink_b = sink_rows(full["attn_sink"][l])
        oa = attn_fwd(aq, ak, av, bias, sink_b, "attn_fwd")
        la_f, la_b = gla_gates_fwd(zf, zb, full["gla_w2_f"][l], full["gla_b2_f"][l], full["gla_w2_b"][l],
                                   full["gla_b2_b"][l], "gla_gates_fwd")
        bk_t = bk.T
        gla = {}
        for nm, la, bdir in (("f", la_f, False), ("b", la_b, True)):
            la_t = la.T
            u, dec = gla_outer(bk_t, la_t, bv, "gla_outer_k_" + nm, bdir, "khat")
            sp = scan_lead(dec, u, "gla_state_scan_" + nm, reverse=bdir, inclusive=False)
            o_dir = gla_inner_fwd(bq, bk, bv, la, sp, "gla_inner_fwd_" + nm, bdir)
            gla[nm] = dict(la=la, la_t=la_t, dec=dec, sp=sp, o=o_dir)
        ob = gla_out_fwd(gla["f"]["o"], gla["b"]["o"], bg, full["gla_norm"][l], "gla_out_fwd")
        sv["gla"] = gla
        xm2, xm1, xp1 = _shift(cx, -2), _shift(cx, -1), _shift(cx, 1)
        xc, a0, u0, a1, u1 = lru_gates_fwd(cx, xm2, xm1, xp1, full["conv_w"][l], full["conv_b"][l], full["lru_wa"][l],
                                           full["lru_ba"][l], full["lru_wx"][l], full["lru_bx"][l],
                                           full["lru_lambda"][l], "lru_gates_fwd")
        h0 = scan_lead(lead(a0), lead(u0), "lru_scan_fwd", reverse=False, inclusive=True).reshape(t, C_WIDTH)
        h1 = scan_lead(lead(a1), lead(u1), "lru_scan_rev", reverse=True, inclusive=True).reshape(t, C_WIDTH)
        oc = lru_out_fwd(h0, h1, cy, "lru_out_fwd")
        sv.update(xm2=xm2, xm1=xm1, xp1=xp1, xc=xc, a0=a0, a1=a1, h0=h0, h1=h1, oa=oa)
        cat = jnp.concatenate([oa, ob, oc], axis=1).astype(BF16)
        gw = gather_finish(ag_started[l][1], cat, "ag_wait_rest%d" % l)
        gws[l] = (gw_in, gw)
        mixed = mm_wk(cat, gw, off_out, r_out, "mm_w_out")
        x1, h2 = resid_rms(x, mixed, full["norm_mix_post"][l], full["norm_x_pre"][l], "resid_rms")
        sv.update(cat=cat, mixed=mixed, x1=x1, h2=h2)
        memn = rms_fwd(mem, full["norm_mem"][l], "rms_mem")
        q = mm_wk(h2, gw, off_xq, rx, "mm_xq")
        k = mm_wk(memn, gw, off_xk, rx, "mm_xkv")
        vv = mm_wk(memn, gw, off_xv, rx, "mm_xkv")
        ox = xattn_fwd(q, k, vv, "xattn_fwd")
        ca = mm_wk(ox, gw, off_xo, rx, "mm_xo")
        x2, h3 = resid_rms(x1, ca, full["norm_x_post"][l], full["norm_ff_pre"][l], "resid_rms")
        sv.update(memn=memn, q=q, k=k, v=vv, ox=ox, ca=ca, x2=x2, h3=h3)
        up, act = mm_wn(h3, gw, off_up, rf, "mm_w_up", with_relu2=True)
        ff = mm_wk(act, gw, off_down, rf, "mm_w_down", jb=max(1, min(N_DEV, 2048 // rf)))
        if l + 1 < depth:
            x, h = resid_rms(x2, ff, full["norm_ff_post"][l], full["norm_mix_pre"][l + 1], "resid_rms")
        else:
            x = resid_rms(x2, ff, full["norm_ff_post"][l], None, "resid_rms_last")
        sv.update(up=up, act=act, ff=ff)
        saved.append(sv)

    dx, loss_local = loss_and_grad(x, loss_target, "loss")
    loss = lax.psum(loss_local, AXES)

    grads = {n: [None] * depth for n in WEIGHTS if n != "rel_bias"}
    dbias_total = None
    big_grads = [None] * depth
    rs_started = [None] * depth
    bf = lambda a: a.astype(BF16)
    for l in reversed(range(depth)):
        gw_in, gw = gws[l]
        sv = saved[l]
        dff, grads["norm_ff_post"][l] = rms_bwd(sv["ff"], full["norm_ff_post"][l], dx, "rms_bwd")
        dup = mm_wn(dff, gw, off_down, rf, "mm_w_down_dx", relu_grad_of=sv["up"], out_dtype=BF16)
        g_down = mm_plain(sv["act"], dff, "mm_dw_down", ta=True, out_dtype=BF16)
        g_up_t = mm_plain(dup, sv["h3"], "mm_dw_up", ta=True, out_dtype=BF16)
        dh3 = mm_wk(dup, gw, off_up, rf, "mm_w_up_dx", jb=max(1, min(N_DEV, 2048 // rf)))
        dx2, grads["norm_ff_pre"][l] = rms_bwd(sv["x2"], full["norm_ff_pre"][l], dh3, "rms_bwd_add", add=dx)
        dca, grads["norm_x_post"][l] = rms_bwd(sv["ca"], full["norm_x_post"][l], dx2, "rms_bwd")
        dox = mm_wn(dca, gw, off_xo, rx, "mm_x_dx")
        g_xo = mm_plain(sv["ox"], dca, "mm_dw_xo", ta=True, out_dtype=BF16)
        dq, dk, dv = xattn_bwd(sv["q"], sv["k"], sv["v"], sv["ox"], dox, "xattn_bwd")
        g_xq = mm_plain(sv["h2"], dq, "mm_dw_d", ta=True, out_dtype=BF16)
        g_xk = mm_plain(sv["memn"], dk, "mm_dw_mem", ta=True, out_dtype=BF16)
        g_xv = mm_plain(sv["memn"], dv, "mm_dw_mem", ta=True, out_dtype=BF16)
        dh2 = mm_wn(dq, gw, off_xq, rx, "mm_x_dx")
        dmem_k = mm_wn(dk, gw, off_xk, rx, "mm_x_dx_mem")
        dmem_v = mm_wn(dv, gw, off_xv, rx, "mm_x_dx_mem")
        _, grads["norm_mem"][l] = rms_bwd(mem, full["norm_mem"][l], dmem_k, "rms_bwd_mem", dy2=dmem_v)
        dx1, grads["norm_x_pre"][l] = rms_bwd(sv["x1"], full["norm_x_pre"][l], dh2, "rms_bwd_add", add=dx2)
        dmixed, grads["norm_mix_post"][l] = rms_bwd(sv["mixed"], full["norm_mix_post"][l], dx1, "rms_bwd")
        dcat = mm_wn(dmixed, gw, off_out, r_out, "mm_w_out_dx")
        g_out = mm_plain(sv["cat"], dmixed, "mm_dw_d", ta=True, out_dtype=BF16)
        parts = [g_up_t, g_down, g_out, g_xq, g_xk, g_xv, g_xo]
        gpack = jnp.concatenate([p.reshape(N_DEV, p.shape[0] // N_DEV, d) for p in parts], axis=1)
        rs_started[l] = [reduce_scatter_begin(gpack, "rs_rest", str(l)), None]
        dcat = dcat + rs_started[l][0][4][0, 0]
        doa, dob, doc = dcat[:, :A_Q], dcat[:, A_Q:A_Q + B_V], dcat[:, A_Q + B_V:]
        daq, dak, dav, dbias, dsink = attn_bwd(sv["aq"], sv["ak"], sv["av"], bias, bias_t,
                                               sink_rows(full["attn_sink"][l]), doa, sv["oa"], "attn_bwd")
        grads["attn_sink"][l] = dsink[:, :A_GROUP, 0].reshape(A_HEADS)
        dbias_total = dbias if dbias_total is None else add_n([dbias_total, dbias], F32, "add_dbias")
        gf, gb = sv["gla"]["f"], sv["gla"]["b"]
        do_gla, dbg, dgn = gla_out_bwd(gf["o"], gb["o"], sv["bg"], full["gla_norm"][l], dob, "gla_out_bwd")
        grads["gla_norm"][l] = dgn.reshape(B_V)
        bq_t = sv["bq"].T
        acc = None
        dlas = {}
        for nm, gd, bdir in (("f", gf, False), ("b", gb, True)):
            wq = gla_outer(bq_t, gd["la_t"], do_gla, "gla_outer_q_" + nm, bdir, "qtil")
            gs = scan_lead(gd["dec"], wq, "gla_adj_scan_" + nm, reverse=not bdir, inclusive=False)
            dbq, dbk, dbv, dlas[nm] = gla_inner_bwd(sv["bq"], sv["bk"], sv["bv"], gd["la"], do_gla, gd["sp"], gs,
                                                    gd["dec"], "gla_inner_bwd_" + nm, bdir, add=acc)
            acc = (dbq, dbk, dbv)
        dzf, dzb, dpre_f, dpre_b, db2f, db2b = gla_gates_bwd(
            sv["zf"], sv["zb"], full["gla_w2_f"][l], full["gla_b2_f"][l], full["gla_w2_b"][l], full["gla_b2_b"][l],
            dlas["f"], dlas["b"], "gla_gates_bwd")
        grads["gla_b2_f"][l] = db2f.reshape(B_QK)
        grads["gla_b2_b"][l] = db2b.reshape(B_QK)
        grads["gla_w2_f"][l] = mm_plain(sv["zf"].T, dpre_f, "mm_dw_gate", hi=True)
        grads["gla_w2_b"][l] = mm_plain(sv["zb"].T, dpre_b, "mm_dw_gate", hi=True)
        dh, dcy = lru_out_bwd(sv["h0"], sv["h1"], sv["cy"], doc, "lru_out_bwd")
        g0 = scan_lead(lead(_shift(sv["a0"], 1)), lead(dh), "lru_scan_rev", reverse=True,
                       inclusive=True).reshape(t, C_WIDTH)
        g1 = scan_lead(lead(_shift(sv["a1"], -1)), lead(dh), "lru_scan_fwd", reverse=False,
                       inclusive=True).reshape(t, C_WIDTH)
        dxc, dzr0, dzi0, dzr1, dzi1, dlam, dba, dbx = lru_gates_bwd(
            sv["xc"], g0, _shift(sv["h0"], -1), g1, _shift(sv["h1"], 1), full["lru_wa"][l], full["lru_ba"][l],
            full["lru_wx"][l], full["lru_bx"][l], full["lru_lambda"][l], "lru_gates_bwd")
        xc_t = bf(sv["xc"].T)
        grads["lru_wa"][l] = jnp.stack([blockdiag_dw(xc_t, dzr0, "lru_dw"), blockdiag_dw(xc_t, dzr1, "lru_dw")])
        grads["lru_wx"][l] = jnp.stack([blockdiag_dw(xc_t, dzi0, "lru_dw"), blockdiag_dw(xc_t, dzi1, "lru_dw")])
        grads["lru_lambda"][l], grads["lru_ba"][l], grads["lru_bx"][l] = dlam, dba, dbx
        dcx, dcw, dcb = conv_bwd(dxc, _shift(dxc, 2), _shift(dxc, 1), _shift(dxc, -1), sv["cx"], sv["xm2"],
                                 sv["xm1"], sv["xp1"], full["conv_w"][l], "conv_bwd")
        grads["conv_w"][l] = dcw
        grads["conv_b"][l] = dcb.reshape(C_WIDTH)
        dproj = jnp.concatenate([daq, dak, dav, dbq, dbk, dbv, dbg, dzf, dzb, dcx, dcy], axis=1)
        zero_cols = jnp.zeros((t, W_IN_ROWS - W_IN_SHARD), F32)
        dproj_pad = jnp.concatenate(
            [p for j in range(N_DEV) for p in (dproj[:, j * W_IN_SHARD:(j + 1) * W_IN_SHARD], zero_cols)],
            axis=1).astype(BF16)
        g_in_t = mm_plain(dproj_pad, sv["h_in"], "mm_dw_in", ta=True, out_dtype=BF16)
        dh1 = mm_wk(dproj_pad, gw_in, off_in, W_IN_ROWS, "mm_w_in_dx", jb=2)
        dx, grads["norm_mix_pre"][l] = rms_bwd(sv["x"], full["norm_mix_pre"][l], dh1, "rms_bwd_add", add=dx1)
        rs_started[l][1] = reduce_scatter_begin(g_in_t.reshape(N_DEV, W_IN_ROWS, d), "rs_in", str(l))
        dx = dx + rs_started[l][1][4][0, 0]

    grad_rel = mm_plain(dbias_total.reshape(A_HEADS, -1), onehot_t, "rel_bias_grad", tb=True, hi=True).T

    small_names = [n for n in WEIGHTS if n not in BIG]
    small_g = {"rel_bias": grad_rel}
    for n in small_names:
        if n != "rel_bias":
            small_g[n] = jnp.stack([g.reshape(full[n].shape[1:]) for g in grads[n]])
    shapes = [small_g[n].shape for n in small_names]
    small_started = gather_start(_pack([small_g[n] for n in small_names]), "ag_start_small_grads")
    dx = dx + small_started[4][0, 0]
    for l in range(depth):
        big_grads[l] = (reduce_scatter_end(rs_started[l][1], dx, "rs_in", str(l)),
                        reduce_scatter_end(rs_started[l][0], dx, "rs_rest", str(l)))

    grad_out, delta, new_m, new_v = {}, {}, {}, {}

    def rows(l, off, r):
        return big_grads[l][1][off:off + r]

    big_g = {
        "w_in": jnp.stack([big_grads[l][0][:W_IN_SHARD].T for l in range(depth)]),
        "w_out": jnp.stack([rows(l, off_out, r_out) for l in range(depth)]),
        "xq": jnp.stack([rows(l, off_xq, rx) for l in range(depth)]),
        "xk": jnp.stack([rows(l, off_xk, rx) for l in range(depth)]),
        "xv": jnp.stack([rows(l, off_xv, rx) for l in range(depth)]),
        "xo": jnp.stack([rows(l, off_xo, rx) for l in range(depth)]),
        "w_up": jnp.stack([rows(l, off_up, rf).T for l in range(depth)]),
        "w_down": jnp.stack([rows(l, off_down, rf) for l in range(depth)]),
    }
    for n in BIG:
        grad_out[n] = big_g[n]
        delta[n], new_m[n], new_v[n] = adamw(big_g[n], w[n], m[n], v[n], "adamw_" + n)

    packed = gather_finish(small_started, delta["w_down"], "ag_wait_small_grads")
    summed = sum_lead(packed, tuple(range(N_DEV)), F32, "add8_small")
    small_g = dict(zip(small_names, _unpack(summed, shapes)))
    for n in SMALL_SHARDED:
        wdt = w[n].shape[-1]
        small_g[n] = lax.dynamic_slice_in_dim(small_g[n], my_idx * wdt, wdt, axis=small_g[n].ndim - 1)

    direct = ("lru_wa", "lru_wx")
    packed_names = [n for n in small_names if n not in direct]
    sshapes = [w[n].shape for n in packed_names]
    ds, ms, vs = adamw(_pack([small_g[n] for n in packed_names]), _pack([w[n] for n in packed_names]),
                       _pack([m[n] for n in packed_names]), _pack([v[n] for n in packed_names]), "adamw_small")
    for n, d_, m_, v_ in zip(packed_names, _unpack(ds, sshapes), _unpack(ms, sshapes), _unpack(vs, sshapes)):
        grad_out[n], delta[n], new_m[n], new_v[n] = small_g[n], d_, m_, v_
    for n in direct:
        grad_out[n] = small_g[n]
        delta[n], new_m[n], new_v[n] = adamw(small_g[n], w[n], m[n], v[n], "adamw_lru")

    return (loss, dx.reshape(1, t, d), *[grad_out[n] for n in WEIGHTS], *[delta[n] for n in WEIGHTS],
            *[new_m[n] for n in WEIGHTS], *[new_v[n] for n in WEIGHTS])


def kernel(x, mem, rel_bias, w_in, w_out, attn_sink, gla_w2_f, gla_b2_f, gla_w2_b, gla_b2_b, gla_norm, conv_w, conv_b, lru_wa, lru_ba, lru_wx, lru_bx, lru_lambda, xq, xk, xv, xo, w_up, w_down, norm_mix_pre, norm_mix_post, norm_mem, norm_x_pre, norm_x_post, norm_ff_pre, norm_ff_post, loss_target, m_rel_bias, m_w_in, m_w_out, m_attn_sink, m_gla_w2_f, m_gla_b2_f, m_gla_w2_b, m_gla_b2_b, m_gla_norm, m_conv_w, m_conv_b, m_lru_wa, m_lru_ba, m_lru_wx, m_lru_bx, m_lru_lambda, m_xq, m_xk, m_xv, m_xo, m_w_up, m_w_down, m_norm_mix_pre, m_norm_mix_post, m_norm_mem, m_norm_x_pre, m_norm_x_post, m_norm_ff_pre, m_norm_ff_post, v_rel_bias, v_w_in, v_w_out, v_attn_sink, v_gla_w2_f, v_gla_b2_f, v_gla_w2_b, v_gla_b2_b, v_gla_norm, v_conv_w, v_conv_b, v_lru_wa, v_lru_ba, v_lru_wx, v_lru_bx, v_lru_lambda, v_xq, v_xk, v_xv, v_xo, v_w_up, v_w_down, v_norm_mix_pre, v_norm_mix_post, v_norm_mem, v_norm_x_pre, v_norm_x_post, v_norm_ff_pre, v_norm_ff_post):
    given = dict(locals())
    w = {n: given[n] for n in WEIGHTS}
    m = {n: given["m_" + n] for n in WEIGHTS}
    v = {n: given["v_" + n] for n in WEIGHTS}
    return _step(x, mem, loss_target, w, m, v)
```

```python
import math

import jax
import jax.numpy as jnp
import numpy as np
from jax import lax
from jax.experimental import pallas as pl
from jax.experimental.pallas import tpu as pltpu

F32 = jnp.float32
BF16 = jnp.bfloat16
HI = lax.Precision.HIGHEST
NN = (((1,), (0,)), ((), ()))
NT = (((1,), (1,)), ((), ()))
MESH = pl.DeviceIdType.MESH
AXES = ("x", "y", "c")
N_DEV = 8

A_HEAD_DIM = 128
A_HEADS = 8
A_KV_HEADS = 2
A_GROUP = 4
WINDOW = 128
BLOCK = 128
N_BUCKETS = 32
MAX_DISTANCE = 128
B_HEADS = 4
B_KEY_DIM = 64
B_VAL_DIM = 128
GATE_RANK = 16
GATE_TAU = 16.0
GLA_CHUNK = 16
C_WIDTH = 512
C_BLOCKS = 4
C_BLOCK_DIM = 128
LRU_C = 8.0
X_HEADS = 4
EPS = 1e-6
NEG_INF = -1e30
A_Q = A_HEADS * A_HEAD_DIM
A_KV = A_KV_HEADS * A_HEAD_DIM
B_QK = B_HEADS * B_KEY_DIM
B_V = B_HEADS * B_VAL_DIM
SPLIT_SIZES = (A_Q, A_KV, A_KV, B_QK, B_QK, B_V, B_V, GATE_RANK, GATE_RANK, C_WIDTH, C_WIDTH)
D_IN = sum(SPLIT_SIZES)
D_MIX = A_Q + B_V + C_WIDTH
W_IN_SHARD = D_IN // N_DEV
W_IN_ROWS = 768
GLA_TILE = 128
CHUNKS_PER_TILE = GLA_TILE // GLA_CHUNK
EXP_CLAMP = 80.0

ADAM_LR = 0.001
ADAM_B1 = 0.9
ADAM_B2 = 0.999
ADAM_EPS = 1e-08
ADAM_WD = 0.01
ADAM_STEP = 10

VMEM_LIMIT_BYTES = 52 * 1024 * 1024
MM_TILE = 1024
SIBLING_STREAMS = 16
PACK_ELEMS = 128 * 2048


def _call(body, **kw):
    return pl.pallas_call(body, **kw)


def _cparams():
    return pltpu.CompilerParams(vmem_limit_bytes=VMEM_LIMIT_BYTES)


def _dot(a, b, dims=NN, hi=False):
    if hi:
        return lax.dot_general(a, b, dims, precision=HI, preferred_element_type=F32)
    return lax.dot_general(a.astype(BF16), b.astype(BF16), dims, preferred_element_type=F32)


def _sds(shape, dtype=F32):
    return jax.ShapeDtypeStruct(tuple(shape), dtype)


def _row_tile(rows, cols, target_elems=1 << 18):
    want = max(8, target_elems // max(cols, 1))
    if rows <= want:
        return rows
    t = (want // 8) * 8
    while t >= 8:
        if rows % t == 0:
            return t
        t -= 8
    return rows


def _expm1(x):
    poly = x * (1.0 + x * (1.0 / 2 + x * (1.0 / 6 + x * (1.0 / 24 + x * (1.0 / 120 + x * (
        1.0 / 720 + x * (1.0 / 5040 + x * (1.0 / 40320))))))))
    return jnp.where(jnp.abs(x) < 0.3, poly, jnp.exp(x) - 1.0)


def _log1p(e):
    w = 1.0 + e
    return jnp.where(w == 1.0, e, jnp.log(w) * e / (w - 1.0))


def _softplus(x):
    return jnp.maximum(x, 0.0) + _log1p(jnp.exp(-jnp.abs(x)))


def _sigmoid(x):
    return jax.nn.sigmoid(x)


GELU_K = math.sqrt(2.0 / math.pi)


def _gelu(y):
    t = jnp.tanh(GELU_K * (y + 0.044715 * y * y * y))
    return 0.5 * y * (1.0 + t)


def _gelu_grad(y):
    t = jnp.tanh(GELU_K * (y + 0.044715 * y * y * y))
    return 0.5 * (1.0 + t) + 0.5 * y * (1.0 - t * t) * GELU_K * (1.0 + 3 * 0.044715 * y * y)


def rms_fwd(x, g, name):
    m, d = x.shape
    tm = _row_tile(m, d)

    def body(x_ref, g_ref, o_ref):
        xv = x_ref[...]
        r = lax.rsqrt(jnp.mean(xv * xv, axis=1, keepdims=True) + EPS)
        o_ref[...] = (xv * r * g_ref[...]).astype(o_ref.dtype)

    return _call(body, name=name, grid=(m // tm,),
                 in_specs=[pl.BlockSpec((tm, d), lambda i: (i, 0)), pl.BlockSpec((1, d), lambda i: (0, 0))],
                 out_specs=pl.BlockSpec((tm, d), lambda i: (i, 0)),
                 out_shape=_sds((m, d), BF16))(x, g.reshape(1, d))


def resid_rms(xres, mid, g_post, g_pre, name):
    m, d = xres.shape
    tm = _row_tile(m, d)
    with_pre = g_pre is not None

    def body(*refs):
        if with_pre:
            x_ref, m_ref, gp_ref, gn_ref, xo_ref, h_ref = refs
        else:
            x_ref, m_ref, gp_ref, xo_ref = refs
        mv = m_ref[...]
        r = lax.rsqrt(jnp.mean(mv * mv, axis=1, keepdims=True) + EPS)
        xn = x_ref[...] + mv * r * gp_ref[...]
        xo_ref[...] = xn
        if with_pre:
            r2 = lax.rsqrt(jnp.mean(xn * xn, axis=1, keepdims=True) + EPS)
            h_ref[...] = (xn * r2 * gn_ref[...]).astype(h_ref.dtype)

    row = pl.BlockSpec((tm, d), lambda i: (i, 0))
    vec = pl.BlockSpec((1, d), lambda i: (0, 0))
    ins = [xres, mid, g_post.reshape(1, d)] + ([g_pre.reshape(1, d)] if with_pre else [])
    in_specs = [row, row, vec] + ([vec] if with_pre else [])
    if with_pre:
        return _call(body, name=name, grid=(m // tm,), in_specs=in_specs, out_specs=(row, row),
                     out_shape=(_sds((m, d)), _sds((m, d), BF16)))(*ins)
    return _call(body, name=name, grid=(m // tm,), in_specs=in_specs, out_specs=row,
                 out_shape=_sds((m, d)))(*ins)


def rms_bwd(x, g, dy, name, dy2=None, add=None):
    m, d = x.shape
    tm = _row_tile(m, d)
    has2, hasadd = dy2 is not None, add is not None

    def body(*refs):
        it = iter(refs)
        x_ref, g_ref, dy_ref = next(it), next(it), next(it)
        dy2_ref = next(it) if has2 else None
        add_ref = next(it) if hasadd else None
        dx_ref, dg_ref = next(it), next(it)
        xv = x_ref[...]
        dyv = dy_ref[...]
        if has2:
            dyv = dyv + dy2_ref[...]
        r = lax.rsqrt(jnp.mean(xv * xv, axis=1, keepdims=True) + EPS)
        xh = xv * r
        dxh = dyv * g_ref[...]
        dx = r * (dxh - xh * jnp.mean(dxh * xh, axis=1, keepdims=True))
        if hasadd:
            dx = dx + add_ref[...]
        dx_ref[...] = dx
        part = jnp.sum(dyv * xh, axis=0, keepdims=True)

        @pl.when(pl.program_id(0) == 0)
        def _():
            dg_ref[...] = part

        @pl.when(pl.program_id(0) > 0)
        def _():
            dg_ref[...] += part

    row = pl.BlockSpec((tm, d), lambda i: (i, 0))
    vec = pl.BlockSpec((1, d), lambda i: (0, 0))
    ins = [x, g.reshape(1, d), dy] + ([dy2] if has2 else []) + ([add] if hasadd else [])
    in_specs = [row, vec, row] + ([row] if has2 else []) + ([row] if hasadd else [])
    return _call(body, name=name, grid=(m // tm,), in_specs=in_specs, out_specs=(row, vec),
                 out_shape=(_sds((m, d)), _sds((1, d))))(*ins)


def loss_and_grad(y, target, name):
    m, d = y.shape
    tm = _row_tile(m, d)

    def body(y_ref, t_ref, dy_ref, l_ref):
        e = y_ref[...] - t_ref[...]
        dy_ref[...] = e * (1.0 / d)
        s = jnp.sum(jnp.sum(e * e, axis=1, keepdims=True), axis=0, keepdims=True) * (0.5 / d)
        part = jnp.broadcast_to(s, (1, 128))

        @pl.when(pl.program_id(0) == 0)
        def _():
            l_ref[...] = part

        @pl.when(pl.program_id(0) > 0)
        def _():
            l_ref[...] += part

    row = pl.BlockSpec((tm, d), lambda i: (i, 0))
    dy, l = _call(body, name=name, grid=(m // tm,), in_specs=[row, row],
                  out_specs=(row, pl.BlockSpec((1, 128), lambda i: (0, 0))),
                  out_shape=(_sds((m, d)), _sds((1, 128))))(y, target)
    return dy, l[0, 0]


def adamw(g, w, m, v, name):
    shape = w.shape
    cols = shape[-1]
    rows = int(np.prod(shape[:-1]))
    tm = _row_tile(rows, cols)
    c1 = 1.0 - ADAM_B1 ** ADAM_STEP
    c2 = 1.0 - ADAM_B2 ** ADAM_STEP

    def body(g_ref, w_ref, m_ref, v_ref, d_ref, mo_ref, vo_ref):
        gv = g_ref[...]
        mn = ADAM_B1 * m_ref[...] + (1.0 - ADAM_B1) * gv
        vn = ADAM_B2 * v_ref[...] + (1.0 - ADAM_B2) * (gv * gv)
        m_hat = mn / c1
        v_hat = vn / c2
        d_ref[...] = -ADAM_LR * (m_hat / (jnp.sqrt(v_hat) + ADAM_EPS) + ADAM_WD * w_ref[...])
        mo_ref[...] = mn
        vo_ref[...] = vn

    row = pl.BlockSpec((tm, cols), lambda i: (i, 0))
    outs = _call(body, name=name, grid=(rows // tm,), in_specs=[row] * 4, out_specs=(row,) * 3,
                 out_shape=(_sds((rows, cols)),) * 3)(*[a.reshape(rows, cols) for a in (g, w, m, v)])
    return tuple(o.reshape(shape) for o in outs)


def sum_lead(x, order, out_dtype, name):
    n, rows, cols = x.shape
    tm = _row_tile(rows, cols)

    def body(x_ref, o_ref):
        acc = x_ref[order[0]].astype(F32)
        for i in order[1:]:
            acc = acc + x_ref[i].astype(F32)
        o_ref[...] = acc.astype(out_dtype)

    return _call(body, name=name, grid=(rows // tm,), in_specs=[pl.BlockSpec((n, tm, cols), lambda i: (0, i, 0))],
                 out_specs=pl.BlockSpec((tm, cols), lambda i: (i, 0)), out_shape=_sds((rows, cols), out_dtype))(x)


def add_own_lead(own, parts, name):
    n, rows, cols = parts.shape
    tm = _row_tile(rows, cols)

    def body(o_ref, p_ref, out_ref):
        acc = o_ref[...].astype(F32)
        for i in range(n):
            acc = acc + p_ref[i].astype(F32)
        out_ref[...] = acc

    row = pl.BlockSpec((tm, cols), lambda i: (i, 0))
    return _call(body, name=name, grid=(rows // tm,),
                 in_specs=[row, pl.BlockSpec((n, tm, cols), lambda i: (0, i, 0))], out_specs=row,
                 out_shape=_sds((rows, cols)))(own, parts)


def add_own_half(g5, theirs, name):
    _, _, r, c = g5.shape
    tm = _row_tile(r, c)

    def body(cc_ref, g_ref, t_ref, o_ref):
        o_ref[...] = (g_ref[...].astype(F32) + t_ref[...].astype(F32)).astype(o_ref.dtype)

    grid_spec = pltpu.PrefetchScalarGridSpec(
        num_scalar_prefetch=1, grid=(4, r // tm),
        in_specs=[pl.BlockSpec((None, None, tm, c), lambda q, i, cc_ref: (q, cc_ref[0], i, 0)),
                  pl.BlockSpec((None, tm, c), lambda q, i, cc_ref: (q, i, 0))],
        out_specs=pl.BlockSpec((None, tm, c), lambda q, i, cc_ref: (q, i, 0)))
    return _call(body, name=name, grid_spec=grid_spec, out_shape=_sds((4, r, c), BF16))(
        lax.axis_index("c").astype(jnp.int32).reshape(1), g5, theirs)


def add_n(xs, out_dtype, name):
    shape = xs[0].shape
    cols = shape[-1]
    rows = int(np.prod(shape[:-1]))
    tm = _row_tile(rows, cols)
    n = len(xs)

    def body(*refs):
        acc = refs[0][...].astype(F32)
        for r in refs[1:n]:
            acc = acc + r[...].astype(F32)
        refs[n][...] = acc.astype(out_dtype)

    row = pl.BlockSpec((tm, cols), lambda i: (i, 0))
    out = _call(body, name=name, grid=(rows // tm,), in_specs=[row] * n, out_specs=row,
                out_shape=_sds((rows, cols), out_dtype))(*[a.reshape(rows, cols) for a in xs])
    return out.reshape(shape)


def mm_plain(a, b, name, ta=False, tb=False, out_dtype=F32, hi=False, tm=MM_TILE, tn=MM_TILE):
    k, m = a.shape[::1 if ta else -1]
    n = b.shape[0] if tb else b.shape[1]
    tm, tn = min(tm, m), min(tn, n)
    dims = (((0 if ta else 1,), (1 if tb else 0,)), ((), ()))

    def body(a_ref, b_ref, o_ref):
        o_ref[...] = _dot(a_ref[...], b_ref[...], dims, hi).astype(out_dtype)

    a_spec = pl.BlockSpec((k, tm), lambda j, i: (0, i)) if ta else pl.BlockSpec((tm, k), lambda j, i: (i, 0))
    b_spec = pl.BlockSpec((tn, k), lambda j, i: (j, 0)) if tb else pl.BlockSpec((k, tn), lambda j, i: (0, j))
    return _call(body, name=name, grid=(n // tn, m // tm), in_specs=[a_spec, b_spec],
                 out_specs=pl.BlockSpec((tm, tn), lambda j, i: (i, j)),
                 out_shape=_sds((m, n), out_dtype), compiler_params=_cparams())(a, b)


def mm_wk(a, gw, off, r, name, jb=N_DEV, tm=MM_TILE, tn=MM_TILE):
    m = a.shape[0]
    d = gw.shape[2]
    tm, tn = min(tm, m), min(tn, d)
    nk = N_DEV // jb
    ob = off // r
    assert off % r == 0 and a.shape[1] == N_DEV * r

    def body(a_ref, b_ref, o_ref, *acc):
        av = a_ref[...].astype(BF16)
        p = _dot(av[:, 0:r], b_ref[0])
        for q in range(1, jb):
            p = p + _dot(av[:, q * r:(q + 1) * r], b_ref[q])
        if nk == 1:
            o_ref[...] = p
        else:
            kk = pl.program_id(2)

            @pl.when(kk == 0)
            def _():
                acc[0][...] = p

            @pl.when(kk > 0)
            def _():
                acc[0][...] += p

            @pl.when(kk == nk - 1)
            def _():
                o_ref[...] = acc[0][...]

    return _call(body, name=name, grid=(m // tm, d // tn, nk),
                 in_specs=[pl.BlockSpec((tm, jb * r), lambda i, j, k: (i, k)),
                           pl.BlockSpec((jb, r, tn), lambda i, j, k: (k, ob, j))],
                 out_specs=pl.BlockSpec((tm, tn), lambda i, j, k: (i, j)),
                 out_shape=_sds((m, d)),
                 scratch_shapes=([pltpu.VMEM((tm, tn), F32)] if nk > 1 else []),
                 compiler_params=_cparams())(a, gw)


def mm_wn(a, gw, off, r, name, relu_grad_of=None, out_dtype=F32, with_relu2=False, tm=MM_TILE):
    m, d = a.shape
    tm = min(tm, m)
    ob = off // r
    assert off % r == 0 and gw.shape[2] == d
    epi = relu_grad_of is not None

    def body(*refs):
        it = iter(refs)
        a_ref, b_ref = next(it), next(it)
        e_ref = next(it) if epi else None
        o_ref = next(it)
        p = _dot(a_ref[...], b_ref[...], NT)
        if epi:
            p = p * (2.0 * jnp.maximum(e_ref[...], 0.0))
        o_ref[...] = p.astype(out_dtype)
        if with_relu2:
            act_ref = next(it)
            act_ref[...] = jnp.square(jnp.maximum(p, 0.0)).astype(act_ref.dtype)

    blk = pl.BlockSpec((tm, r), lambda i, j: (i, j))
    in_specs = [pl.BlockSpec((tm, d), lambda i, j: (i, 0)), pl.BlockSpec((None, r, d), lambda i, j: (j, ob, 0))]
    ins = [a, gw]
    if epi:
        in_specs.append(blk)
        ins.append(relu_grad_of)
    out_shape = _sds((m, N_DEV * r), out_dtype)
    if with_relu2:
        return _call(body, name=name, grid=(m // tm, N_DEV), in_specs=in_specs, out_specs=(blk, blk),
                     out_shape=(out_shape, _sds((m, N_DEV * r), BF16)), compiler_params=_cparams())(*ins)
    return _call(body, name=name, grid=(m // tm, N_DEV), in_specs=in_specs, out_specs=blk,
                 out_shape=out_shape, compiler_params=_cparams())(*ins)


def blockdiag_dw(xt, dz, name):
    t = xt.shape[1]

    def body(a_ref, b_ref, o_ref):
        o_ref[...] = _dot(a_ref[...], b_ref[...])

    return _call(body, name=name, grid=(C_BLOCKS,),
                 in_specs=[pl.BlockSpec((C_BLOCK_DIM, t), lambda g: (g, 0)),
                           pl.BlockSpec((t, C_BLOCK_DIM), lambda g: (0, g))],
                 out_specs=pl.BlockSpec((None, C_BLOCK_DIM, C_BLOCK_DIM), lambda g: (g, 0, 0)),
                 out_shape=_sds((C_BLOCKS, C_BLOCK_DIM, C_BLOCK_DIM)))(xt, dz)


def _band_mask(n, nblk, transposed):
    shape = (3 * BLOCK, A_GROUP * BLOCK) if transposed else (A_GROUP * BLOCK, 3 * BLOCK)
    qi = lax.broadcasted_iota(jnp.int32, shape, 1 if transposed else 0) & (BLOCK - 1)
    kj = lax.broadcasted_iota(jnp.int32, shape, 0 if transposed else 1)
    lo = jnp.where(n > 0, 0, BLOCK)
    hi = jnp.where(n < nblk - 1, 3 * BLOCK, 2 * BLOCK)
    return (jnp.abs(kj - BLOCK - qi) <= WINDOW) & (kj >= lo) & (kj < hi)


def _band_rows(ref, n, nblk):
    starts = [jnp.maximum(n - 1, 0), n, jnp.minimum(n + 1, nblk - 1)]
    return jnp.concatenate([ref[pl.ds(pl.multiple_of(s * BLOCK, BLOCK), BLOCK), :] for s in starts], axis=0)


def _head_cols(j):
    return slice(j * A_HEAD_DIM, (j + 1) * A_HEAD_DIM)


def attn_fwd(q, k, v, bias, sink_b, name):
    t = q.shape[0]
    nblk = t // BLOCK
    scale = A_HEAD_DIM ** -0.5

    def body(q_ref, k_ref, v_ref, b_ref, s_ref, o_ref):
        n = pl.program_id(1)
        kb = _band_rows(k_ref, n, nblk).astype(BF16)
        vb = _band_rows(v_ref, n, nblk).astype(BF16)
        mask = _band_mask(n, nblk, False)
        q4 = jnp.concatenate([q_ref[:, _head_cols(j)] for j in range(A_GROUP)], axis=0)
        b4 = jnp.concatenate([b_ref[j] for j in range(A_GROUP)], axis=0)
        sk = jnp.concatenate([jnp.broadcast_to(s_ref[j:j + 1, 0:1], (BLOCK, 1)) for j in range(A_GROUP)], axis=0)
        s = jnp.where(mask, _dot(q4, kb, NT) * scale + b4, NEG_INF)
        mx = jnp.maximum(jnp.max(s, axis=1, keepdims=True), sk)
        p = jnp.exp(s - mx)
        den = jnp.sum(p, axis=1, keepdims=True) + jnp.exp(sk - mx)
        o4 = _dot(p * (1.0 / den), vb)
        for j in range(A_GROUP):
            o_ref[:, _head_cols(j)] = o4[j * BLOCK:(j + 1) * BLOCK, :]

    gw = A_GROUP * A_HEAD_DIM
    return _call(body, name=name, grid=(A_KV_HEADS, nblk),
                 in_specs=[pl.BlockSpec((BLOCK, gw), lambda g, n: (n, g)),
                           pl.BlockSpec((t, A_HEAD_DIM), lambda g, n: (0, g)),
                           pl.BlockSpec((t, A_HEAD_DIM), lambda g, n: (0, g)),
                           pl.BlockSpec((A_GROUP, BLOCK, 3 * BLOCK), lambda g, n: (g, 0, 0)),
                           pl.BlockSpec((None, 8, 128), lambda g, n: (g, 0, 0))],
                 out_specs=pl.BlockSpec((BLOCK, gw), lambda g, n: (n, g)),
                 out_shape=_sds((t, A_Q)))(q, k, v, bias, sink_b)


def attn_bwd(q, k, v, bias, bias_t, sink_b, do, o, name):
    t = q.shape[0]
    nblk = t // BLOCK
    scale = A_HEAD_DIM ** -0.5

    def body(q_ref, k_ref, v_ref, b_ref, bt_ref, s_ref, do_ref, o_ref, dq_ref, dk_ref, dv_ref, db_ref, ds_ref):
        n = pl.program_id(1)

        @pl.when(n == 0)
        def _():
            dk_ref[...] = jnp.zeros_like(dk_ref)
            dv_ref[...] = jnp.zeros_like(dv_ref)
            db_ref[...] = jnp.zeros_like(db_ref)
            ds_ref[...] = jnp.zeros_like(ds_ref)

        kb = _band_rows(k_ref, n, nblk).astype(BF16)
        vb = _band_rows(v_ref, n, nblk).astype(BF16)
        heads = range(A_GROUP)
        mask = _band_mask(n, nblk, False)
        mask_t = _band_mask(n, nblk, True)
        q4 = jnp.concatenate([q_ref[:, _head_cols(j)] for j in heads], axis=0).astype(BF16)
        do4 = jnp.concatenate([do_ref[:, _head_cols(j)] for j in heads], axis=0)
        doo = do4 * jnp.concatenate([o_ref[:, _head_cols(j)] for j in heads], axis=0)
        do4 = do4.astype(BF16)
        b4 = jnp.concatenate([b_ref[j] for j in heads], axis=0)
        bt4 = jnp.concatenate([bt_ref[j] for j in heads], axis=1)
        sk = jnp.concatenate([jnp.broadcast_to(s_ref[j:j + 1, 0:1], (BLOCK, 1)) for j in heads], axis=0)
        sk_t = jnp.concatenate([jnp.broadcast_to(s_ref[j:j + 1, 0:1], (1, BLOCK)) for j in heads], axis=1)
        s = jnp.where(mask, _dot(q4, kb, NT) * scale + b4, NEG_INF)
        mx = jnp.maximum(jnp.max(s, axis=1, keepdims=True), sk)
        p = jnp.exp(s - mx)
        den = jnp.sum(p, axis=1, keepdims=True) + jnp.exp(sk - mx)
        rden = 1.0 / den
        p = p * rden
        psink_delta = jnp.exp(sk - mx) * rden * jnp.sum(doo, axis=1, keepdims=True)
        dsc = p * (_dot(do4, vb, NT) - jnp.sum(doo, axis=1, keepdims=True))
        dq4 = _dot(dsc, kb) * scale
        for j in heads:
            rows = slice(j * BLOCK, (j + 1) * BLOCK)
            db_ref[j] += dsc[rows, :]
            ds_ref[j:j + 1, :] += jnp.broadcast_to(-jnp.sum(psink_delta[rows, :], axis=0, keepdims=True), (1, 128))
            dq_ref[:, _head_cols(j)] = dq4[rows, :]
        st = jnp.where(mask_t, _dot(kb, q4, NT) * scale + bt4, NEG_INF)
        mxt = jnp.maximum(jnp.max(st, axis=0, keepdims=True), sk_t)
        pt = jnp.exp(st - mxt)
        dent = jnp.sum(pt, axis=0, keepdims=True) + jnp.exp(sk_t - mxt)
        pt = pt * (1.0 / dent)
        delta_t = _dot(jnp.ones((8, A_HEAD_DIM), F32), doo, NT, hi=True)[0:1, :]
        dst = pt * (_dot(vb, do4, NT) - delta_t)
        dkb = _dot(dst, q4) * scale
        dvb = _dot(pt, do4)
        starts = [jnp.maximum(n - 1, 0), n, jnp.minimum(n + 1, nblk - 1)]
        for c, st_ in enumerate(starts):
            rows = pl.ds(pl.multiple_of(st_ * BLOCK, BLOCK), BLOCK)
            dk_ref[rows, :] += dkb[c * BLOCK:(c + 1) * BLOCK, :]
            dv_ref[rows, :] += dvb[c * BLOCK:(c + 1) * BLOCK, :]

    gw = A_GROUP * A_HEAD_DIM
    qspec = pl.BlockSpec((BLOCK, gw), lambda g, n: (n, g))
    kspec = pl.BlockSpec((t, A_HEAD_DIM), lambda g, n: (0, g))
    sspec = pl.BlockSpec((None, 8, 128), lambda g, n: (g, 0, 0))
    bspec = pl.BlockSpec((A_GROUP, BLOCK, 3 * BLOCK), lambda g, n: (g, 0, 0))
    btspec = pl.BlockSpec((A_GROUP, 3 * BLOCK, BLOCK), lambda g, n: (g, 0, 0))
    return _call(body, name=name, grid=(A_KV_HEADS, nblk),
                 in_specs=[qspec, kspec, kspec, bspec, btspec, sspec, qspec, qspec],
                 out_specs=(qspec, kspec, kspec, bspec, sspec),
                 out_shape=(_sds((t, A_Q)), _sds((t, A_KV)), _sds((t, A_KV)),
                            _sds((A_HEADS, BLOCK, 3 * BLOCK)), _sds((A_KV_HEADS, 8, 128))),
                 compiler_params=_cparams())(q, k, v, bias, bias_t, sink_b, do, o)


def xattn_fwd(q, k, v, name):
    t, d = q.shape
    ml = k.shape[0]
    dh = d // X_HEADS
    tq = min(256, t)
    scale = dh ** -0.5

    def body(q_ref, k_ref, v_ref, o_ref):
        s = _dot(q_ref[...], k_ref[...], NT) * scale
        p = jnp.exp(s - jnp.max(s, axis=1, keepdims=True))
        p = p * (1.0 / jnp.sum(p, axis=1, keepdims=True))
        o_ref[...] = _dot(p, v_ref[...])

    qspec = pl.BlockSpec((tq, dh), lambda h, i: (i, h))
    kspec = pl.BlockSpec((ml, dh), lambda h, i: (0, h))
    return _call(body, name=name, grid=(X_HEADS, t // tq), in_specs=[qspec, kspec, kspec], out_specs=qspec,
                 out_shape=_sds((t, d)))(q, k, v)


def xattn_bwd(q, k, v, o, do, name):
    t, d = q.shape
    ml = k.shape[0]
    dh = d // X_HEADS
    tq = min(256, t)
    scale = dh ** -0.5

    def body(q_ref, k_ref, v_ref, o_ref, do_ref, dq_ref, dk_ref, dv_ref):
        i = pl.program_id(1)
        qv, kv, vv = q_ref[...].astype(BF16), k_ref[...].astype(BF16), v_ref[...].astype(BF16)
        dov = do_ref[...]
        doo = dov * o_ref[...]
        dov = dov.astype(BF16)
        s = _dot(qv, kv, NT) * scale
        p = jnp.exp(s - jnp.max(s, axis=1, keepdims=True))
        p = p * (1.0 / jnp.sum(p, axis=1, keepdims=True))
        ds = p * (_dot(dov, vv, NT) - jnp.sum(doo, axis=1, keepdims=True))
        dq_ref[...] = _dot(ds, kv) * scale
        st = _dot(kv, qv, NT) * scale
        pt = jnp.exp(st - jnp.max(st, axis=0, keepdims=True))
        pt = pt * (1.0 / jnp.sum(pt, axis=0, keepdims=True))
        delta_t = _dot(jnp.ones((8, dh), F32), doo, NT, hi=True)[0:1, :]
        dst = pt * (_dot(vv, dov, NT) - delta_t)
        dkp = _dot(dst, qv) * scale
        dvp = _dot(pt, dov)

        @pl.when(i == 0)
        def _():
            dk_ref[...] = dkp
            dv_ref[...] = dvp

        @pl.when(i > 0)
        def _():
            dk_ref[...] += dkp
            dv_ref[...] += dvp

    qspec = pl.BlockSpec((tq, dh), lambda h, i: (i, h))
    kspec = pl.BlockSpec((ml, dh), lambda h, i: (0, h))
    return _call(body, name=name, grid=(X_HEADS, t // tq), in_specs=[qspec, kspec, kspec, qspec, qspec],
                 out_specs=(qspec, kspec, kspec),
                 out_shape=(_sds((t, d)), _sds((ml, d)), _sds((ml, d))))(q, k, v, o, do)


def scan_lead(a, u, name, reverse, inclusive):
    n, r, c = a.shape
    blk = max(1, min(n, (1 << 18) // (max(r, 8) * c)))
    while n % blk:
        blk -= 1
    nb = n // blk

    def body(a_ref, u_ref, o_ref, carry):
        @pl.when(pl.program_id(0) == 0)
        def _():
            carry[...] = jnp.zeros_like(carry)

        def step(s, h):
            idx = (blk - 1 - s) if reverse else s
            hn = a_ref[idx] * h + u_ref[idx]
            o_ref[idx] = hn if inclusive else h
            return hn

        carry[...] = lax.fori_loop(0, blk, step, carry[...])

    spec = pl.BlockSpec((blk, r, c), (lambda i: (nb - 1 - i, 0, 0)) if reverse else (lambda i: (i, 0, 0)))
    return _call(body, name=name, grid=(nb,), in_specs=[spec, spec], out_specs=spec,
                 out_shape=_sds((n, r, c)), scratch_shapes=[pltpu.VMEM((r, c), F32)])(a, u)


def _chunk_mats(bwd_dir):
    i = lax.broadcasted_iota(jnp.int32, (GLA_TILE, GLA_TILE), 0)
    j = lax.broadcasted_iota(jnp.int32, (GLA_TILE, GLA_TILE), 1)
    same = lax.shift_right_logical(i, 4) == lax.shift_right_logical(j, 4)
    if bwd_dir:
        cm, cm_t = same & (j >= i), same & (i >= j)
        mk, mk_t = same & (j > i), same & (i > j)
    else:
        cm, cm_t = same & (j <= i), same & (i <= j)
        mk, mk_t = same & (j <= i), same & (i <= j)
    f = lambda b: jnp.where(b, 1.0, 0.0).astype(F32)
    return f(cm), f(cm_t), mk, mk_t, f(same)


def gla_gates_fwd(zf, zb, w2f, b2f, w2b, b2b, name):
    t = zf.shape[0]
    tm = min(256, t)

    def body(zf_ref, zb_ref, wf_ref, bf_ref, wb_ref, bb_ref, lf_ref, lb_ref):
        lf_ref[...] = -_softplus(-(_dot(zf_ref[...], wf_ref[...], hi=True) + bf_ref[...])) / GATE_TAU
        lb_ref[...] = -_softplus(-(_dot(zb_ref[...], wb_ref[...], hi=True) + bb_ref[...])) / GATE_TAU

    zs = pl.BlockSpec((tm, GATE_RANK), lambda i: (i, 0))
    ws = pl.BlockSpec((GATE_RANK, B_QK), lambda i: (0, 0))
    bs = pl.BlockSpec((1, B_QK), lambda i: (0, 0))
    os_ = pl.BlockSpec((tm, B_QK), lambda i: (i, 0))
    return _call(body, name=name, grid=(t // tm,), in_specs=[zs, zs, ws, bs, ws, bs], out_specs=(os_, os_),
                 out_shape=(_sds((t, B_QK)),) * 2)(zf, zb, w2f, b2f.reshape(1, B_QK), w2b, b2b.reshape(1, B_QK))


def gla_gates_bwd(zf, zb, w2f, b2f, w2b, b2b, dlf, dlb, name):
    t = zf.shape[0]
    tm = min(256, t)

    def body(zf_ref, zb_ref, wf_ref, bf_ref, wb_ref, bb_ref, dlf_ref, dlb_ref,
             dzf_ref, dzb_ref, dpf_ref, dpb_ref, dbf_ref, dbb_ref):
        first = pl.program_id(0) == 0
        for z_ref, w_ref, b_ref, dl_ref, dz_ref, dp_ref, db_ref in (
                (zf_ref, wf_ref, bf_ref, dlf_ref, dzf_ref, dpf_ref, dbf_ref),
                (zb_ref, wb_ref, bb_ref, dlb_ref, dzb_ref, dpb_ref, dbb_ref)):
            pre = _dot(z_ref[...], w_ref[...], hi=True) + b_ref[...]
            dpre = dl_ref[...] * (1.0 / GATE_TAU) * _sigmoid(-pre)
            dp_ref[...] = dpre
            dz_ref[...] = _dot(dpre, w_ref[...], NT, hi=True)
            part = jnp.sum(dpre, axis=0, keepdims=True)

            @pl.when(first)
            def _():
                db_ref[...] = part

            @pl.when(jnp.logical_not(first))
            def _():
                db_ref[...] += part

    zs = pl.BlockSpec((tm, GATE_RANK), lambda i: (i, 0))
    ws = pl.BlockSpec((GATE_RANK, B_QK), lambda i: (0, 0))
    bs = pl.BlockSpec((1, B_QK), lambda i: (0, 0))
    os_ = pl.BlockSpec((tm, B_QK), lambda i: (i, 0))
    return _call(body, name=name, grid=(t // tm,), in_specs=[zs, zs, ws, bs, ws, bs, os_, os_],
                 out_specs=(zs, zs, os_, os_, bs, bs),
                 out_shape=(_sds((t, GATE_RANK)),) * 2 + (_sds((t, B_QK)),) * 2 + (_sds((1, B_QK)),) * 2)(
        zf, zb, w2f, b2f.reshape(1, B_QK), w2b, b2b.reshape(1, B_QK), dlf, dlb)


def gla_outer(xt, lat, y, name, bwd_dir, mode):
    t = y.shape[0]
    nchunk = t // GLA_CHUNK
    khat = mode == "khat"
    scale = B_KEY_DIM ** -0.5

    def body(xt_ref, lat_ref, y_ref, *outs):
        _, cm_t, _, _, same = _chunk_mats(bwd_dir)
        lat_v = lat_ref[...]
        bt = _dot(lat_v, cm_t, hi=True)
        if khat:
            mult = jnp.exp(_dot(lat_v, same, hi=True) - bt)
        else:
            mult = jnp.exp(bt) * scale
        xm = xt_ref[...] * mult
        lane = lax.shift_right_logical(lax.broadcasted_iota(jnp.int32, (1, GLA_TILE), 1), 4)
        ones = jnp.ones((GLA_TILE, B_VAL_DIM), F32)
        yv = [y_ref[:, h * B_VAL_DIM:(h + 1) * B_VAL_DIM].astype(BF16) for h in range(B_HEADS)]
        for c in range(CHUNKS_PER_TILE):
            sel = lane == c
            xc = jnp.where(sel, xm, 0.0).astype(BF16)
            for h in range(B_HEADS):
                rows = slice(h * B_KEY_DIM, (h + 1) * B_KEY_DIM)
                outs[0][c, rows, :] = _dot(xc[rows, :], yv[h])
            if khat:
                outs[1][c] = jnp.exp(_dot(jnp.where(sel, lat_v, 0.0), ones, hi=True))

    tspec = pl.BlockSpec((B_QK, GLA_TILE), lambda i: (0, i))
    ospec = pl.BlockSpec((CHUNKS_PER_TILE, B_QK, B_VAL_DIM), lambda i: (i, 0, 0))
    oshape = _sds((nchunk, B_QK, B_VAL_DIM))
    return _call(body, name=name, grid=(t // GLA_TILE,),
                 in_specs=[tspec, tspec, pl.BlockSpec((GLA_TILE, B_V), lambda i: (i, 0))],
                 out_specs=(ospec, ospec) if khat else ospec,
                 out_shape=(oshape, oshape) if khat else oshape)(xt, lat, y)


def _head_lane_mask(h):
    lane = lax.broadcasted_iota(jnp.int32, (1, B_QK), 1)
    return lax.shift_right_logical(lane, 6) == h


def _chunk_rows(c):
    return slice(c * GLA_CHUNK, (c + 1) * GLA_CHUNK)


def gla_inner_fwd(q, k, v, la, sp, name, bwd_dir):
    t = q.shape[0]
    scale = B_KEY_DIM ** -0.5

    def body(q_ref, k_ref, v_ref, la_ref, sp_ref, o_ref):
        cm, _, mk, _, _ = _chunk_mats(bwd_dir)
        b = _dot(cm, la_ref[...], hi=True)
        qt = q_ref[...] * scale * jnp.exp(b)
        kt = k_ref[...] * jnp.exp(jnp.minimum(-b, EXP_CLAMP))
        spb = [sp_ref[c].astype(BF16) for c in range(CHUNKS_PER_TILE)]
        for h in range(B_HEADS):
            lm = _head_lane_mask(h)
            qm = jnp.where(lm, qt, 0.0).astype(BF16)
            km = jnp.where(lm, kt, 0.0).astype(BF16)
            vs = slice(h * B_VAL_DIM, (h + 1) * B_VAL_DIM)
            att = jnp.where(mk, _dot(qm, km, NT), 0.0)
            inter = jnp.concatenate([_dot(qm[_chunk_rows(c), :], spb[c]) for c in range(CHUNKS_PER_TILE)], axis=0)
            o_ref[:, vs] = _dot(att, v_ref[:, vs]) + inter

    qs = pl.BlockSpec((GLA_TILE, B_QK), lambda i: (i, 0))
    vs_ = pl.BlockSpec((GLA_TILE, B_V), lambda i: (i, 0))
    ss = pl.BlockSpec((CHUNKS_PER_TILE, B_QK, B_VAL_DIM), lambda i: (i, 0, 0))
    return _call(body, name=name, grid=(t // GLA_TILE,), in_specs=[qs, qs, vs_, qs, ss], out_specs=vs_,
                 out_shape=_sds((t, B_V)))(q, k, v, la, sp)


def gla_inner_bwd(q, k, v, la, do, sp, gs, dec, name, bwd_dir, add=None):
    t = q.shape[0]
    scale = B_KEY_DIM ** -0.5
    hasadd = add is not None

    def body(*refs):
        it = iter(refs)
        q_ref, k_ref, v_ref, la_ref, do_ref, sp_ref, gs_ref, dec_ref = [next(it) for _ in range(8)]
        adds = [next(it) for _ in range(3)] if hasadd else None
        dq_ref, dk_ref, dv_ref, dla_ref = [next(it) for _ in range(4)]
        cm, cm_t, mk, mk_t, same = _chunk_mats(bwd_dir)
        la_v = la_ref[...]
        b = _dot(cm, la_v, hi=True)
        btot = _dot(same, la_v, hi=True)
        eb = jnp.exp(b)
        ek = jnp.exp(jnp.minimum(-b, EXP_CLAMP))
        ekh = jnp.exp(btot - b)
        qt = q_ref[...] * scale * eb
        kt = k_ref[...] * ek
        kh = k_ref[...] * ekh
        spb = [sp_ref[c].astype(BF16) for c in range(CHUNKS_PER_TILE)]
        gsb = [gs_ref[c].astype(BF16) for c in range(CHUNKS_PER_TILE)]
        dqt = jnp.zeros((GLA_TILE, B_QK), F32)
        dkt = jnp.zeros((GLA_TILE, B_QK), F32)
        dkh = jnp.zeros((GLA_TILE, B_QK), F32)
        for h in range(B_HEADS):
            lm = _head_lane_mask(h)
            qm = jnp.where(lm, qt, 0.0).astype(BF16)
            km = jnp.where(lm, kt, 0.0).astype(BF16)
            khm = jnp.where(lm, kh, 0.0).astype(BF16)
            vs = slice(h * B_VAL_DIM, (h + 1) * B_VAL_DIM)
            vh = v_ref[:, vs].astype(BF16)
            doh = do_ref[:, vs].astype(BF16)
            da = jnp.where(mk, _dot(doh, vh, NT), 0.0)
            da_t = jnp.where(mk_t, _dot(vh, doh, NT), 0.0)
            att_t = jnp.where(mk_t, _dot(km, qm, NT), 0.0)
            dv_h = _dot(att_t, doh) + jnp.concatenate(
                [_dot(khm[_chunk_rows(c), :], gsb[c]) for c in range(CHUNKS_PER_TILE)], axis=0)
            if hasadd:
                dv_h = dv_h + adds[2][:, vs]
            dv_ref[:, vs] = dv_h
            dq_inter = jnp.concatenate(
                [_dot(doh[_chunk_rows(c), :], spb[c], NT) for c in range(CHUNKS_PER_TILE)], axis=0)
            dqt = dqt + _dot(da, km) + jnp.where(lm, dq_inter, 0.0)
            dkt = dkt + _dot(da_t, qm)
            dkh_inter = jnp.concatenate(
                [_dot(vh[_chunk_rows(c), :], gsb[c], NT) for c in range(CHUNKS_PER_TILE)], axis=0)
            dkh = dkh + jnp.where(lm, dkh_inter, 0.0)
        dq = dqt * scale * eb
        dk = dkt * ek + dkh * ekh
        if hasadd:
            dq = dq + adds[0][...]
            dk = dk + adds[1][...]
        dq_ref[...] = dq
        dk_ref[...] = dk
        db = dqt * qt - dkt * kt - dkh * kh
        ones16 = jnp.ones((GLA_CHUNK, B_VAL_DIM), F32)
        t2 = jnp.concatenate(
            [_dot(ones16, gs_ref[c] * dec_ref[c] * sp_ref[c], NT, hi=True) for c in range(CHUNKS_PER_TILE)], axis=0)
        dla_ref[...] = _dot(cm_t, db, hi=True) + _dot(same, dkh * kh, hi=True) + t2

    qs = pl.BlockSpec((GLA_TILE, B_QK), lambda i: (i, 0))
    vs_ = pl.BlockSpec((GLA_TILE, B_V), lambda i: (i, 0))
    ss = pl.BlockSpec((CHUNKS_PER_TILE, B_QK, B_VAL_DIM), lambda i: (i, 0, 0))
    ins = [q, k, v, la, do, sp, gs, dec] + (list(add) if hasadd else [])
    in_specs = [qs, qs, vs_, qs, vs_, ss, ss, ss] + ([qs, qs, vs_] if hasadd else [])
    return _call(body, name=name, grid=(t // GLA_TILE,), in_specs=in_specs, out_specs=(qs, qs, vs_, qs),
                 out_shape=(_sds((t, B_QK)), _sds((t, B_QK)), _sds((t, B_V)), _sds((t, B_QK))),
                 compiler_params=_cparams())(*ins)


def gla_out_fwd(of, ob, g, gn, name):
    t = of.shape[0]
    tm = min(256, t)

    def body(of_ref, ob_ref, g_ref, gn_ref, o_ref):
        for h in range(B_HEADS):
            vs = slice(h * B_VAL_DIM, (h + 1) * B_VAL_DIM)
            o = of_ref[:, vs] + ob_ref[:, vs]
            on = o * lax.rsqrt(jnp.mean(o * o, axis=1, keepdims=True) + EPS)
            gv = g_ref[:, vs]
            o_ref[:, vs] = on * gn_ref[:, vs] * (gv * _sigmoid(gv))

    row = pl.BlockSpec((tm, B_V), lambda i: (i, 0))
    vec = pl.BlockSpec((1, B_V), lambda i: (0, 0))
    return _call(body, name=name, grid=(t // tm,), in_specs=[row, row, row, vec], out_specs=row,
                 out_shape=_sds((t, B_V)))(of, ob, g, gn.reshape(1, B_V))


def gla_out_bwd(of, ob, g, gn, dout, name):
    t = of.shape[0]
    tm = min(256, t)

    def body(of_ref, ob_ref, g_ref, gn_ref, d_ref, do_ref, dg_ref, dgn_ref):
        first = pl.program_id(0) == 0
        for h in range(B_HEADS):
            vs = slice(h * B_VAL_DIM, (h + 1) * B_VAL_DIM)
            o = of_ref[:, vs] + ob_ref[:, vs]
            r = lax.rsqrt(jnp.mean(o * o, axis=1, keepdims=True) + EPS)
            on = o * r
            gv = g_ref[:, vs]
            sg = _sigmoid(gv)
            silu = gv * sg
            dv = d_ref[:, vs]
            gnv = gn_ref[:, vs]
            dg_ref[:, vs] = dv * on * gnv * (sg * (1.0 + gv * (1.0 - sg)))
            don = dv * silu * gnv
            do_ref[:, vs] = r * (don - on * jnp.mean(don * on, axis=1, keepdims=True))
            part = jnp.sum(dv * silu * on, axis=0, keepdims=True)

            @pl.when(first)
            def _():
                dgn_ref[:, vs] = part

            @pl.when(jnp.logical_not(first))
            def _():
                dgn_ref[:, vs] += part

    row = pl.BlockSpec((tm, B_V), lambda i: (i, 0))
    vec = pl.BlockSpec((1, B_V), lambda i: (0, 0))
    return _call(body, name=name, grid=(t // tm,), in_specs=[row, row, row, vec, row], out_specs=(row, row, vec),
                 out_shape=(_sds((t, B_V)), _sds((t, B_V)), _sds((1, B_V))))(of, ob, g, gn.reshape(1, B_V), dout)


def _shift(x, k):
    if k > 0:
        return jnp.concatenate([x[k:], jnp.zeros((k,) + x.shape[1:], x.dtype)], axis=0)
    return jnp.concatenate([jnp.zeros((-k,) + x.shape[1:], x.dtype), x[:k]], axis=0)


def _lru_gates(xc, s, wa_ref, ba_ref, wx_ref, bx_ref, lam_ref):
    cols = [slice(g * C_BLOCK_DIM, (g + 1) * C_BLOCK_DIM) for g in range(C_BLOCKS)]
    zr = jnp.concatenate([_dot(xc[:, cs], wa_ref[s, g]) for g, cs in enumerate(cols)], axis=1) + ba_ref[s:s + 1, :]
    zi = jnp.concatenate([_dot(xc[:, cs], wx_ref[s, g]) for g, cs in enumerate(cols)], axis=1) + bx_ref[s:s + 1, :]
    r = _sigmoid(zr)
    i = _sigmoid(zi)
    sp = _softplus(-lam_ref[s:s + 1, :])
    log_a = -LRU_C * r * sp
    return r, i, sp, log_a


def lru_gates_fwd(x0, xm2, xm1, xp1, cw, cb, wa, ba, wx, bx, lam, name):
    t = x0.shape[0]
    tm = min(256, t)

    def body(x0_ref, xm2_ref, xm1_ref, xp1_ref, cw_ref, cb_ref, wa_ref, ba_ref, wx_ref, bx_ref, lam_ref,
             xc_ref, a0_ref, u0_ref, a1_ref, u1_ref):
        xc = (xm2_ref[...] * cw_ref[0:1, :] + xm1_ref[...] * cw_ref[1:2, :] + x0_ref[...] * cw_ref[2:3, :]
              + xp1_ref[...] * cw_ref[3:4, :] + cb_ref[...])
        xc_ref[...] = xc
        for s, (a_ref, u_ref) in enumerate(((a0_ref, u0_ref), (a1_ref, u1_ref))):
            _, i, _, log_a = _lru_gates(xc, s, wa_ref, ba_ref, wx_ref, bx_ref, lam_ref)
            a_ref[...] = jnp.exp(log_a)
            u_ref[...] = jnp.sqrt(-_expm1(2.0 * log_a)) * (i * xc)

    row = pl.BlockSpec((tm, C_WIDTH), lambda i: (i, 0))
    full = lambda shape: pl.BlockSpec(shape, lambda i: (0,) * len(shape))
    wshape = (2, C_BLOCKS, C_BLOCK_DIM, C_BLOCK_DIM)
    return _call(body, name=name, grid=(t // tm,),
                 in_specs=[row] * 4 + [full((4, C_WIDTH)), full((1, C_WIDTH)), full(wshape), full((2, C_WIDTH)),
                                       full(wshape), full((2, C_WIDTH)), full((2, C_WIDTH))],
                 out_specs=(row,) * 5, out_shape=(_sds((t, C_WIDTH)),) * 5)(
        x0, xm2, xm1, xp1, cw, cb.reshape(1, C_WIDTH), wa, ba, wx, bx, lam)


def lru_gates_bwd(xc, g0, hs0, g1, hs1, wa, ba, wx, bx, lam, name):
    t = xc.shape[0]
    tm = min(256, t)

    def body(xc_ref, g0_ref, hs0_ref, g1_ref, hs1_ref, wa_ref, ba_ref, wx_ref, bx_ref, lam_ref,
             dxc_ref, dzr0_ref, dzi0_ref, dzr1_ref, dzi1_ref, dlam_ref, dba_ref, dbx_ref):
        first = pl.program_id(0) == 0

        @pl.when(first)
        def _():
            dlam_ref[...] = jnp.zeros_like(dlam_ref)
            dba_ref[...] = jnp.zeros_like(dba_ref)
            dbx_ref[...] = jnp.zeros_like(dbx_ref)

        xcv = xc_ref[...]
        dxc = jnp.zeros_like(xcv)
        cols = [slice(g * C_BLOCK_DIM, (g + 1) * C_BLOCK_DIM) for g in range(C_BLOCKS)]
        for s, (g_ref, hs_ref, dzr_ref, dzi_ref) in enumerate(
                ((g0_ref, hs0_ref, dzr0_ref, dzi0_ref), (g1_ref, hs1_ref, dzr1_ref, dzi1_ref))):
            r, i, sp, log_a = _lru_gates(xcv, s, wa_ref, ba_ref, wx_ref, bx_ref, lam_ref)
            du = g_ref[...]
            da = du * hs_ref[...]
            a = jnp.exp(log_a)
            e2 = jnp.exp(2.0 * log_a)
            c = jnp.sqrt(-_expm1(2.0 * log_a))
            ix = i * xcv
            dlog = da * a - du * ix * (e2 / c)
            dix = du * c
            dxc = dxc + dix * i
            dzi = dix * xcv * i * (1.0 - i)
            dzr = dlog * (-LRU_C * sp) * r * (1.0 - r)
            dzr_ref[...] = dzr
            dzi_ref[...] = dzi
            dxc = dxc + jnp.concatenate(
                [_dot(dzr[:, cs], wa_ref[s, g], NT) + _dot(dzi[:, cs], wx_ref[s, g], NT) for g, cs in enumerate(cols)],
                axis=1)
            dsp = jnp.sum(dlog * (-LRU_C * r), axis=0, keepdims=True)
            dlam_ref[s:s + 1, :] += dsp * (-_sigmoid(-lam_ref[s:s + 1, :]))
            dba_ref[s:s + 1, :] += jnp.sum(dzr, axis=0, keepdims=True)
            dbx_ref[s:s + 1, :] += jnp.sum(dzi, axis=0, keepdims=True)
        dxc_ref[...] = dxc

    row = pl.BlockSpec((tm, C_WIDTH), lambda i: (i, 0))
    full = lambda shape: pl.BlockSpec(shape, lambda i: (0,) * len(shape))
    wshape = (2, C_BLOCKS, C_BLOCK_DIM, C_BLOCK_DIM)
    vec2 = full((2, C_WIDTH))
    return _call(body, name=name, grid=(t // tm,),
                 in_specs=[row] * 5 + [full(wshape), vec2, full(wshape), vec2, vec2],
                 out_specs=(row,) * 5 + (vec2,) * 3,
                 out_shape=(_sds((t, C_WIDTH)),) * 5 + (_sds((2, C_WIDTH)),) * 3)(
        xc, g0, hs0, g1, hs1, wa, ba, wx, bx, lam)


def lru_out_fwd(h0, h1, y, name):
    t = y.shape[0]
    tm = min(256, t)

    def body(h0_ref, h1_ref, y_ref, o_ref):
        o_ref[...] = (h0_ref[...] + h1_ref[...]) * _gelu(y_ref[...])

    row = pl.BlockSpec((tm, C_WIDTH), lambda i: (i, 0))
    return _call(body, name=name, grid=(t // tm,), in_specs=[row] * 3, out_specs=row,
                 out_shape=_sds((t, C_WIDTH)))(h0, h1, y)


def lru_out_bwd(h0, h1, y, dout, name):
    t = y.shape[0]
    tm = min(256, t)

    def body(h0_ref, h1_ref, y_ref, d_ref, dh_ref, dy_ref):
        yv = y_ref[...]
        dv = d_ref[...]
        dh_ref[...] = dv * _gelu(yv)
        dy_ref[...] = dv * (h0_ref[...] + h1_ref[...]) * _gelu_grad(yv)

    row = pl.BlockSpec((tm, C_WIDTH), lambda i: (i, 0))
    return _call(body, name=name, grid=(t // tm,), in_specs=[row] * 4, out_specs=(row, row),
                 out_shape=(_sds((t, C_WIDTH)),) * 2)(h0, h1, y, dout)


def conv_bwd(dxc, dp2, dp1, dm1, x0, xm2, xm1, xp1, cw, name):
    t = x0.shape[0]
    tm = min(256, t)

    def body(d_ref, dp2_ref, dp1_ref, dm1_ref, x0_ref, xm2_ref, xm1_ref, xp1_ref, cw_ref, dx_ref, dcw_ref, dcb_ref):
        @pl.when(pl.program_id(0) == 0)
        def _():
            dcw_ref[...] = jnp.zeros_like(dcw_ref)
            dcb_ref[...] = jnp.zeros_like(dcb_ref)

        dv = d_ref[...]
        dx_ref[...] = (dp2_ref[...] * cw_ref[0:1, :] + dp1_ref[...] * cw_ref[1:2, :] + dv * cw_ref[2:3, :]
                       + dm1_ref[...] * cw_ref[3:4, :])
        for j, x_ref in enumerate((xm2_ref, xm1_ref, x0_ref, xp1_ref)):
            dcw_ref[j:j + 1, :] += jnp.sum(dv * x_ref[...], axis=0, keepdims=True)
        dcb_ref[...] += jnp.sum(dv, axis=0, keepdims=True)

    row = pl.BlockSpec((tm, C_WIDTH), lambda i: (i, 0))
    cws = pl.BlockSpec((4, C_WIDTH), lambda i: (0, 0))
    cbs = pl.BlockSpec((1, C_WIDTH), lambda i: (0, 0))
    return _call(body, name=name, grid=(t // tm,), in_specs=[row] * 8 + [cws], out_specs=(row, cws, cbs),
                 out_shape=(_sds((t, C_WIDTH)), _sds((4, C_WIDTH)), _sds((1, C_WIDTH))))(
        dxc, dp2, dp1, dm1, x0, xm2, xm1, xp1, cw)


def _my_place():
    return lax.axis_index("x"), lax.axis_index("y"), lax.axis_index("c")


def all_gather(xs, name):
    r, c = xs.shape

    def body(x_ref, out_ref, send_sems, recv_sems, local_sem):
        x, y, cc = _my_place()
        me, sibling = (x, y, cc), (x, y, 1 - cc)
        chips = [(1 - x, y), (x, 1 - y), (1 - x, 1 - y)]

        def slot(px, py, pc):
            return out_ref.at[4 * px + 2 * py + pc]

        def copy(k, block, to, src=None):
            return pltpu.make_async_remote_copy(
                src_ref=slot(*block) if src is None else src, dst_ref=slot(*block),
                send_sem=send_sems.at[k], recv_sem=recv_sems.at[k], device_id=to, device_id_type=MESH)

        mine = pltpu.make_async_copy(x_ref, slot(*me), local_sem)
        mine.start()
        first = [copy(0, me, sibling, src=x_ref)]
        first += [copy(1 + j, me, (*chip, cc), src=x_ref) for j, chip in enumerate(chips)]
        for cp in first:
            cp.start()
        passed = [copy(4 + j, (*chip, cc), sibling) for j, chip in enumerate(chips)]
        for j, chip in enumerate(chips):
            copy(1 + j, (*chip, cc), me).wait_recv()
            passed[j].start()
        copy(0, sibling, me).wait_recv()
        for j, chip in enumerate(chips):
            copy(4 + j, (*chip, 1 - cc), me).wait_recv()
        for cp in first + passed:
            cp.wait_send()
        mine.wait()

    return _call(body, name=name, in_specs=[pl.BlockSpec(memory_space=pl.ANY)],
                 out_specs=pl.BlockSpec(memory_space=pl.ANY), out_shape=_sds((N_DEV, r, c), xs.dtype),
                 scratch_shapes=[pltpu.SemaphoreType.DMA((7,)), pltpu.SemaphoreType.DMA((7,)),
                                 pltpu.SemaphoreType.DMA])(xs)


def _stream_rows(r):
    nch = SIBLING_STREAMS // 4 if r % (8 * (SIBLING_STREAMS // 4)) == 0 else 1
    return nch, r // nch


def exchange_sibling(gw, name):
    _, r, c = gw.shape
    g5 = gw.reshape(4, 2, r, c)
    nch, rows = _stream_rows(r)

    def body(g_ref, out_ref, send_sems, recv_sems):
        x, y, cc = _my_place()
        swaps = []
        for q in range(4):
            for s in range(nch):
                k = q * nch + s
                win = pl.ds(s * rows, rows)
                swaps.append(pltpu.make_async_remote_copy(
                    src_ref=g_ref.at[q, 1 - cc, win], dst_ref=out_ref.at[q, win], send_sem=send_sems.at[k],
                    recv_sem=recv_sems.at[k], device_id=(x, y, 1 - cc), device_id_type=MESH))
        for cp in swaps:
            cp.start()
        for cp in swaps:
            cp.wait()

    nsem = 4 * nch
    return _call(body, name=name, in_specs=[pl.BlockSpec(memory_space=pl.ANY)],
                 out_specs=pl.BlockSpec(memory_space=pl.ANY), out_shape=_sds((4, r, c), gw.dtype),
                 scratch_shapes=[pltpu.SemaphoreType.DMA((nsem,)), pltpu.SemaphoreType.DMA((nsem,))])(g5)


HBM_SPEC = pl.BlockSpec(memory_space=pltpu.HBM)
SEM_SPEC = pl.BlockSpec(memory_space=pltpu.SEMAPHORE)
DATAFLOW = pltpu.SideEffectType.DATAFLOW_SIDE_EFFECTING


def _hbm(a):
    return pltpu.with_memory_space_constraint(a, pltpu.HBM)


def _peers(x, y, cc):
    return [(x, y, 1 - cc), (1 - x, y, cc), (x, 1 - y, cc), (1 - x, 1 - y, cc)]


def _slot(p):
    return 4 * p[0] + 2 * p[1] + p[2]


def gather_start(blk, name):
    r, c = blk.shape

    def body(v_ref, land_ref, send_sems, recv_sems, v_thru, land_thru, token):
        x, y, cc = _my_place()
        for k, to in enumerate(_peers(x, y, cc)):
            pltpu.make_async_remote_copy(
                src_ref=v_ref, dst_ref=land_ref.at[_slot((x, y, cc))], send_sem=send_sems.at[k],
                recv_sem=recv_sems.at[k], device_id=to, device_id_type=MESH).start()
        token[...] = jnp.zeros_like(token)

    return _call(
        body, name=name,
        out_shape=(pltpu.SemaphoreType.DMA((4,)), pltpu.SemaphoreType.DMA((4,)), pltpu.HBM((r, c), blk.dtype),
                   pltpu.HBM((N_DEV, r, c), blk.dtype), _sds((8, 128))),
        in_specs=(HBM_SPEC, HBM_SPEC),
        out_specs=(SEM_SPEC, SEM_SPEC, HBM_SPEC, HBM_SPEC, pl.BlockSpec(memory_space=pltpu.VMEM)),
        input_output_aliases={0: 2, 1: 3},
        compiler_params=pltpu.CompilerParams(has_side_effects=DATAFLOW),
    )(_hbm(blk), _hbm(lax.empty((N_DEV, r, c), blk.dtype)))


def gather_wait(send_sems, recv_sems, v_thru, land_thru, after, name):
    def body(v_ref, land_ref, send_sems, recv_sems, after_ref, v_out, land_out):
        x, y, cc = _my_place()
        for k, peer in enumerate(_peers(x, y, cc)):
            cp = pltpu.make_async_remote_copy(
                src_ref=v_ref, dst_ref=land_ref.at[_slot(peer)], send_sem=send_sems.at[k], recv_sem=recv_sems.at[k],
                device_id=peer, device_id_type=MESH)
            cp.wait_send()
            cp.wait_recv()

    return _call(
        body, name=name,
        out_shape=(pltpu.HBM(v_thru.shape, v_thru.dtype), pltpu.HBM(land_thru.shape, land_thru.dtype)),
        in_specs=(HBM_SPEC, HBM_SPEC, SEM_SPEC, SEM_SPEC, pl.BlockSpec(memory_space=pl.ANY)),
        out_specs=(HBM_SPEC, HBM_SPEC), input_output_aliases={0: 0, 1: 1},
        compiler_params=pltpu.CompilerParams(has_side_effects=DATAFLOW),
    )(v_thru, land_thru, send_sems, recv_sems, after)


def gather_pass(land, name):
    _, r, c = land.shape
    nch, rows = _stream_rows(r)

    def body(land_ref, out_ref, send_sems, recv_sems):
        x, y, cc = _my_place()
        peers = _peers(x, y, cc)
        copies = []
        for j in range(3):
            mine, theirs = _slot(peers[1 + j]), _slot((peers[1 + j][0], peers[1 + j][1], 1 - cc))
            for s in range(nch):
                k = j * nch + s
                win = pl.ds(s * rows, rows)
                send = pltpu.make_async_remote_copy(
                    src_ref=land_ref.at[mine, win], dst_ref=out_ref.at[mine, win], send_sem=send_sems.at[k],
                    recv_sem=recv_sems.at[k], device_id=peers[0], device_id_type=MESH)
                recv = pltpu.make_async_remote_copy(
                    src_ref=land_ref.at[mine, win], dst_ref=out_ref.at[theirs, win], send_sem=send_sems.at[k],
                    recv_sem=recv_sems.at[k], device_id=peers[0], device_id_type=MESH)
                copies.append((send, recv))
        for send, _ in copies:
            send.start()
        for send, recv in copies:
            send.wait_send()
            recv.wait_recv()

    nsem = 3 * nch
    return _call(body, name=name, in_specs=[pl.BlockSpec(memory_space=pl.ANY)],
                 out_specs=pl.BlockSpec(memory_space=pl.ANY), out_shape=_sds(land.shape, land.dtype),
                 input_output_aliases={0: 0},
                 scratch_shapes=[pltpu.SemaphoreType.DMA((nsem,)), pltpu.SemaphoreType.DMA((nsem,))])(land)


def chips_start(p, name):
    _, r, c = p.shape

    def body(p_ref, land_ref, send_sems, recv_sems, p_thru, land_thru, token):
        x, y, cc = _my_place()
        for j, (px, py, pc) in enumerate(_peers(x, y, cc)[1:]):
            pltpu.make_async_remote_copy(
                src_ref=p_ref.at[2 * px + py], dst_ref=land_ref.at[j], send_sem=send_sems.at[j],
                recv_sem=recv_sems.at[j], device_id=(px, py, pc), device_id_type=MESH).start()
        token[...] = jnp.zeros_like(token)

    return _call(
        body, name=name,
        out_shape=(pltpu.SemaphoreType.DMA((3,)), pltpu.SemaphoreType.DMA((3,)), pltpu.HBM(p.shape, p.dtype),
                   pltpu.HBM((3, r, c), p.dtype), _sds((8, 128))),
        in_specs=(HBM_SPEC, HBM_SPEC),
        out_specs=(SEM_SPEC, SEM_SPEC, HBM_SPEC, HBM_SPEC, pl.BlockSpec(memory_space=pltpu.VMEM)),
        input_output_aliases={0: 2, 1: 3},
        compiler_params=pltpu.CompilerParams(has_side_effects=DATAFLOW),
    )(_hbm(p), _hbm(lax.empty((3, r, c), p.dtype)))


def chips_wait(send_sems, recv_sems, p_thru, land_thru, after, name):
    def body(p_ref, land_ref, send_sems, recv_sems, after_ref, p_out, land_out):
        x, y, cc = _my_place()
        for j, (px, py, pc) in enumerate(_peers(x, y, cc)[1:]):
            cp = pltpu.make_async_remote_copy(
                src_ref=p_ref.at[2 * px + py], dst_ref=land_ref.at[j], send_sem=send_sems.at[j],
                recv_sem=recv_sems.at[j], device_id=(px, py, pc), device_id_type=MESH)
            cp.wait_send()
            cp.wait_recv()

    return _call(
        body, name=name,
        out_shape=(pltpu.HBM(p_thru.shape, p_thru.dtype), pltpu.HBM(land_thru.shape, land_thru.dtype)),
        in_specs=(HBM_SPEC, HBM_SPEC, SEM_SPEC, SEM_SPEC, pl.BlockSpec(memory_space=pl.ANY)),
        out_specs=(HBM_SPEC, HBM_SPEC), input_output_aliases={0: 0, 1: 1},
        compiler_params=pltpu.CompilerParams(has_side_effects=DATAFLOW),
    )(p_thru, land_thru, send_sems, recv_sems, after)


def sibling_start(gw, name):
    _, r, c = gw.shape
    nch, rows = _stream_rows(r)
    nsem = 4 * nch

    def body(g_ref, land_ref, send_sems, recv_sems, g_thru, land_thru, token):
        x, y, cc = _my_place()
        for q in range(4):
            for s in range(nch):
                win = pl.ds(s * rows, rows)
                pltpu.make_async_remote_copy(
                    src_ref=g_ref.at[q, 1 - cc, win], dst_ref=land_ref.at[q, win], send_sem=send_sems.at[q * nch + s],
                    recv_sem=recv_sems.at[q * nch + s], device_id=(x, y, 1 - cc), device_id_type=MESH).start()
        token[...] = jnp.zeros_like(token)

    return _call(
        body, name=name,
        out_shape=(pltpu.SemaphoreType.DMA((nsem,)), pltpu.SemaphoreType.DMA((nsem,)),
                   pltpu.HBM((4, 2, r, c), gw.dtype), pltpu.HBM((4, r, c), gw.dtype), _sds((8, 128))),
        in_specs=(HBM_SPEC, HBM_SPEC),
        out_specs=(SEM_SPEC, SEM_SPEC, HBM_SPEC, HBM_SPEC, pl.BlockSpec(memory_space=pltpu.VMEM)),
        input_output_aliases={0: 2, 1: 3},
        compiler_params=pltpu.CompilerParams(has_side_effects=DATAFLOW),
    )(_hbm(gw.reshape(4, 2, r, c)), _hbm(lax.empty((4, r, c), gw.dtype)))


def sibling_wait(send_sems, recv_sems, g_thru, land_thru, after, name):
    _, _, r, c = g_thru.shape
    nch, rows = _stream_rows(r)

    def body(g_ref, land_ref, send_sems, recv_sems, after_ref, g_out, land_out):
        x, y, cc = _my_place()
        for q in range(4):
            for s in range(nch):
                win = pl.ds(s * rows, rows)
                cp = pltpu.make_async_remote_copy(
                    src_ref=g_ref.at[q, 1 - cc, win], dst_ref=land_ref.at[q, win], send_sem=send_sems.at[q * nch + s],
                    recv_sem=recv_sems.at[q * nch + s], device_id=(x, y, 1 - cc), device_id_type=MESH)
                cp.wait_send()
                cp.wait_recv()

    return _call(
        body, name=name,
        out_shape=(pltpu.HBM(g_thru.shape, g_thru.dtype), pltpu.HBM(land_thru.shape, land_thru.dtype)),
        in_specs=(HBM_SPEC, HBM_SPEC, SEM_SPEC, SEM_SPEC, pl.BlockSpec(memory_space=pl.ANY)),
        out_specs=(HBM_SPEC, HBM_SPEC), input_output_aliases={0: 0, 1: 1},
        compiler_params=pltpu.CompilerParams(has_side_effects=DATAFLOW),
    )(g_thru, land_thru, send_sems, recv_sems, after)


def reduce_scatter_begin(gw, name, tag):
    _, r, c = gw.shape
    theirs = exchange_sibling(gw, name + "_sibling")
    chip_sum = add_own_half(gw.reshape(4, 2, r, c), theirs, name + "_add2")
    return chips_start(chip_sum, name + "_start" + tag)


def reduce_scatter_end(started, after, name, tag):
    send_sems, recv_sems, p_thru, land_thru, _ = started
    parts, land = chips_wait(send_sems, recv_sems, p_thru, land_thru, after, name + "_wait" + tag)
    mine = lax.dynamic_index_in_dim(parts, 2 * lax.axis_index("x") + lax.axis_index("y"), axis=0, keepdims=False)
    return add_own_lead(mine, land, name + "_add4")


def _pack(arrs):
    flat = jnp.concatenate([a.reshape(-1).astype(F32) for a in arrs])
    n = flat.shape[0]
    pad = (-n) % PACK_ELEMS
    return jnp.pad(flat, (0, pad)).reshape(-1, 128)


def _unpack(packed, shapes):
    flat = packed.reshape(-1)
    out, off = [], 0
    for s in shapes:
        n = int(np.prod(s))
        out.append(lax.optimization_barrier(flat[off:off + n]).reshape(s))
        off += n
    return out


def _t5_bucket(rel):
    nb = N_BUCKETS // 2
    max_exact = nb // 2
    ret = jnp.where(rel > 0, nb, 0)
    n = jnp.abs(rel)
    nf = jnp.maximum(n, 1).astype(jnp.float32)
    large = max_exact + (jnp.log(nf / max_exact) / math.log(MAX_DISTANCE / max_exact)
                         * (nb - max_exact)).astype(jnp.int32)
    large = jnp.minimum(large, nb - 1)
    return ret + jnp.where(n < max_exact, n, large)


SMALL_SHARDED = ("gla_w2_f", "gla_w2_b", "conv_w", "lru_ba", "lru_bx", "lru_lambda")
SMALL_REPL = ("rel_bias", "attn_sink", "gla_b2_f", "gla_b2_b", "gla_norm", "conv_b", "lru_wa", "lru_wx",
              "norm_mix_pre", "norm_mix_post", "norm_mem", "norm_x_pre", "norm_x_post", "norm_ff_pre", "norm_ff_post")
BIG = ("w_in", "w_out", "xq", "xk", "xv", "xo", "w_up", "w_down")
WEIGHTS = ['rel_bias', 'w_in', 'w_out', 'attn_sink', 'gla_w2_f', 'gla_b2_f', 'gla_w2_b', 'gla_b2_b', 'gla_norm',
           'conv_w', 'conv_b', 'lru_wa', 'lru_ba', 'lru_wx', 'lru_bx', 'lru_lambda', 'xq', 'xk', 'xv', 'xo', 'w_up',
           'w_down', 'norm_mix_pre', 'norm_mix_post', 'norm_mem', 'norm_x_pre', 'norm_x_post', 'norm_ff_pre',
           'norm_ff_post']


def _step(x, mem, loss_target, w, m, v):
    depth = w["w_in"].shape[0]
    t, d = x.shape[1], x.shape[2]
    ml = mem.shape[1]
    rx = d // N_DEV
    rf = w["w_up"].shape[2]
    r_out = D_MIX // N_DEV
    x = x.reshape(t, d)
    mem = mem.reshape(ml, d)
    loss_target = loss_target.reshape(t, d)
    my_idx = 4 * lax.axis_index("x") + 2 * lax.axis_index("y") + lax.axis_index("c")

    off_in = 0
    off_up, off_down, off_out = 0, rf, 2 * rf
    off_xq = off_out + r_out
    off_xk, off_xv, off_xo = off_xq + rx, off_xq + 2 * rx, off_xq + 3 * rx
    r_rest = off_xo + rx

    sh_shapes = [w[n].shape for n in SMALL_SHARDED]
    gathered = all_gather(_pack([w[n] for n in SMALL_SHARDED]), "ag_small")
    per_dev = [_unpack(gathered[j], sh_shapes) for j in range(N_DEV)]
    full = {n: jnp.concatenate([per_dev[j][i] for j in range(N_DEV)], axis=-1) for i, n in enumerate(SMALL_SHARDED)}
    for n in SMALL_REPL:
        full[n] = w[n]

    ag_started = []
    for l in range(depth):
        blk_in = jnp.pad(w["w_in"][l].T, ((0, W_IN_ROWS - W_IN_SHARD), (0, 0))).astype(BF16)
        blk_rest = jnp.concatenate([w["w_up"][l].T, w["w_down"][l], w["w_out"][l], w["xq"][l], w["xk"][l],
                                    w["xv"][l], w["xo"][l]], axis=0).astype(BF16)
        blk_in, _ = lax.optimization_barrier((blk_in, gathered if l == 0 else ag_started[-1][1][4]))
        start_in = gather_start(blk_in, "ag_start_in%d" % l)
        blk_rest, _ = lax.optimization_barrier((blk_rest, start_in[4]))
        ag_started.append((start_in, gather_start(blk_rest, "ag_start_rest%d" % l)))
    x = x + sum(st[4][0, 0] for pair in ag_started for st in pair)
    gws = [None] * depth

    def gather_finish(started, after, name):
        send_sems, recv_sems, blk_thru, land_thru, _ = started
        blk_done, land = gather_wait(send_sems, recv_sems, blk_thru, land_thru, after, name)
        land = gather_pass(land, "ag_pass")
        return lax.dynamic_update_index_in_dim(land, blk_done, my_idx, 0)

    qi = jnp.arange(BLOCK)[:, None]
    kj = jnp.arange(3 * BLOCK)[None, :]
    onehot_t = (jnp.arange(N_BUCKETS)[:, None] == _t5_bucket(kj - BLOCK - qi).reshape(1, -1)).astype(F32)
    bias = mm_plain(full["rel_bias"].T, onehot_t, "rel_bias_lookup", hi=True, tn=3 * BLOCK * 16)
    bias = bias.reshape(A_HEADS, BLOCK, 3 * BLOCK)
    bias_t = jnp.transpose(bias, (0, 2, 1))

    def sink_rows(sink):
        s = jnp.broadcast_to(sink.reshape(A_KV_HEADS, A_GROUP, 1), (A_KV_HEADS, A_GROUP, 128))
        return jnp.pad(s, ((0, 0), (0, 8 - A_GROUP), (0, 0)))

    bounds = np.concatenate([[0], np.cumsum(SPLIT_SIZES)])

    def split_proj(pp):
        outs = []
        for lo, hi in zip(bounds[:-1], bounds[1:]):
            segs = []
            for j in range(N_DEV):
                a, b = max(lo, j * W_IN_SHARD), min(hi, (j + 1) * W_IN_SHARD)
                if a < b:
                    base = j * W_IN_ROWS - j * W_IN_SHARD
                    segs.append(pp[:, base + a:base + b])
            outs.append(segs[0] if len(segs) == 1 else jnp.concatenate(segs, axis=1))
        return outs

    def join_dproj(pieces):
        zero_cols = jnp.zeros((t, W_IN_ROWS - W_IN_SHARD), F32)
        segs = []
        for j in range(N_DEV):
            for p, lo, hi in zip(pieces, bounds[:-1], bounds[1:]):
                a, b = max(lo, j * W_IN_SHARD), min(hi, (j + 1) * W_IN_SHARD)
                if a < b:
                    segs.append(p[:, a - lo:b - lo])
            segs.append(zero_cols)
        return jnp.concatenate(segs, axis=1).astype(BF16)

    def lead(a):
        return a.reshape(a.shape[0], C_WIDTH // 128, 128)

    saved = []
    h = rms_fwd(x, full["norm_mix_pre"][0], "rms_first")
    for l in range(depth):
        gw_in = gather_finish(ag_started[l][0], x, "ag_wait_in%d" % l)
        sv = {"x": x, "h_in": h}
        proj_pad = mm_wn(h, gw_in, off_in, W_IN_ROWS, "mm_w_in")
        aq, ak, av, bq, bk, bv, bg, zf, zb, cx, cy = split_proj(proj_pad)
        sv.update(aq=aq, ak=ak, av=av, bq=bq, bk=bk, bv=bv, bg=bg, zf=zf, zb=zb, cx=cx, cy=cy)
        sink_b = sink_rows(full["attn_sink"][l])
        oa = attn_fwd(aq, ak, av, bias, sink_b, "attn_fwd")
        la_f, la_b = gla_gates_fwd(zf, zb, full["gla_w2_f"][l], full["gla_b2_f"][l], full["gla_w2_b"][l],
                                   full["gla_b2_b"][l], "gla_gates_fwd")
        bk_t = bk.T
        gla = {}
        for nm, la, bdir in (("f", la_f, False), ("b", la_b, True)):
            la_t = la.T
            u, dec = gla_outer(bk_t, la_t, bv, "gla_outer_k_" + nm, bdir, "khat")
            sp = scan_lead(dec, u, "gla_state_scan_" + nm, reverse=bdir, inclusive=False)
            o_dir = gla_inner_fwd(bq, bk, bv, la, sp, "gla_inner_fwd_" + nm, bdir)
            gla[nm] = dict(la=la, la_t=la_t, dec=dec, sp=sp, o=o_dir)
        ob = gla_out_fwd(gla["f"]["o"], gla["b"]["o"], bg, full["gla_norm"][l], "gla_out_fwd")
        sv["gla"] = gla
        xm2, xm1, xp1 = _shift(cx, -2), _shift(cx, -1), _shift(cx, 1)
        xc, a0, u0, a1, u1 = lru_gates_fwd(cx, xm2, xm1, xp1, full["conv_w"][l], full["conv_b"][l], full["lru_wa"][l],
                                           full["lru_ba"][l], full["lru_wx"][l], full["lru_bx"][l],
                                           full["lru_lambda"][l], "lru_gates_fwd")
        h0 = scan_lead(lead(a0), lead(u0), "lru_scan_fwd", reverse=False, inclusive=True).reshape(t, C_WIDTH)
        h1 = scan_lead(lead(a1), lead(u1), "lru_scan_rev", reverse=True, inclusive=True).reshape(t, C_WIDTH)
        oc = lru_out_fwd(h0, h1, cy, "lru_out_fwd")
        sv.update(xm2=xm2, xm1=xm1, xp1=xp1, xc=xc, a0=a0, a1=a1, h0=h0, h1=h1, oa=oa)
        cat = jnp.concatenate([oa, ob, oc], axis=1).astype(BF16)
        gw = gather_finish(ag_started[l][1], cat, "ag_wait_rest%d" % l)
        gws[l] = (gw_in, gw)
        mixed = mm_wk(cat, gw, off_out, r_out, "mm_w_out")
        x1, h2 = resid_rms(x, mixed, full["norm_mix_post"][l], full["norm_x_pre"][l], "resid_rms")
        sv.update(cat=cat, mixed=mixed, x1=x1, h2=h2)
        memn = rms_fwd(mem, full["norm_mem"][l], "rms_mem")
        q = mm_wk(h2, gw, off_xq, rx, "mm_xq")
        k = mm_wk(memn, gw, off_xk, rx, "mm_xkv")
        vv = mm_wk(memn, gw, off_xv, rx, "mm_xkv")
        ox = xattn_fwd(q, k, vv, "xattn_fwd")
        ca = mm_wk(ox, gw, off_xo, rx, "mm_xo")
        x2, h3 = resid_rms(x1, ca, full["norm_x_post"][l], full["norm_ff_pre"][l], "resid_rms")
        sv.update(memn=memn, q=q, k=k, v=vv, ox=ox, ca=ca, x2=x2, h3=h3)
        up, act = mm_wn(h3, gw, off_up, rf, "mm_w_up", with_relu2=True)
        ff = mm_wk(act, gw, off_down, rf, "mm_w_down", jb=max(1, min(N_DEV, 2048 // rf)))
        if l + 1 < depth:
            x, h = resid_rms(x2, ff, full["norm_ff_post"][l], full["norm_mix_pre"][l + 1], "resid_rms")
        else:
            x = resid_rms(x2, ff, full["norm_ff_post"][l], None, "resid_rms_last")
        sv.update(up=up, act=act, ff=ff)
        saved.append(sv)

    dx, loss_local = loss_and_grad(x, loss_target, "loss")
    loss = lax.psum(loss_local, AXES)

    grads = {n: [None] * depth for n in WEIGHTS if n != "rel_bias"}
    dbias_total = None
    big_grads = [None] * depth
    rs_started = [None] * depth
    bf = lambda a: a.astype(BF16)
    for l in reversed(range(depth)):
        gw_in, gw = gws[l]
        sv = saved[l]
        dff, grads["norm_ff_post"][l] = rms_bwd(sv["ff"], full["norm_ff_post"][l], dx, "rms_bwd")
        dup = mm_wn(dff, gw, off_down, rf, "mm_w_down_dx", relu_grad_of=sv["up"], out_dtype=BF16)
        g_down = mm_plain(sv["act"], dff, "mm_dw_down", ta=True, out_dtype=BF16)
        g_up_t = mm_plain(dup, sv["h3"], "mm_dw_up", ta=True, out_dtype=BF16)
        dh3 = mm_wk(dup, gw, off_up, rf, "mm_w_up_dx", jb=max(1, min(N_DEV, 2048 // rf)))
        dx2, grads["norm_ff_pre"][l] = rms_bwd(sv["x2"], full["norm_ff_pre"][l], dh3, "rms_bwd_add", add=dx)
        dca, grads["norm_x_post"][l] = rms_bwd(sv["ca"], full["norm_x_post"][l], dx2, "rms_bwd")
        dox = mm_wn(dca, gw, off_xo, rx, "mm_x_dx")
        g_xo = mm_plain(sv["ox"], dca, "mm_dw_xo", ta=True, out_dtype=BF16)
        dq, dk, dv = xattn_bwd(sv["q"], sv["k"], sv["v"], sv["ox"], dox, "xattn_bwd")
        g_xq = mm_plain(sv["h2"], dq, "mm_dw_d", ta=True, out_dtype=BF16)
        g_xk = mm_plain(sv["memn"], dk, "mm_dw_mem", ta=True, out_dtype=BF16)
        g_xv = mm_plain(sv["memn"], dv, "mm_dw_mem", ta=True, out_dtype=BF16)
        dh2 = mm_wn(dq, gw, off_xq, rx, "mm_x_dx")
        dmem_k = mm_wn(dk, gw, off_xk, rx, "mm_x_dx_mem")
        dmem_v = mm_wn(dv, gw, off_xv, rx, "mm_x_dx_mem")
        _, grads["norm_mem"][l] = rms_bwd(mem, full["norm_mem"][l], dmem_k, "rms_bwd_mem", dy2=dmem_v)
        dx1, grads["norm_x_pre"][l] = rms_bwd(sv["x1"], full["norm_x_pre"][l], dh2, "rms_bwd_add", add=dx2)
        dmixed, grads["norm_mix_post"][l] = rms_bwd(sv["mixed"], full["norm_mix_post"][l], dx1, "rms_bwd")
        dcat = mm_wn(dmixed, gw, off_out, r_out, "mm_w_out_dx")
        g_out = mm_plain(sv["cat"], dmixed, "mm_dw_d", ta=True, out_dtype=BF16)
        parts = [g_up_t, g_down, g_out, g_xq, g_xk, g_xv, g_xo]
        gpack = jnp.concatenate([p.reshape(N_DEV, p.shape[0] // N_DEV, d) for p in parts], axis=1)
        if l == 0:
            rs_started[l] = [reduce_scatter_begin(gpack, "rs_rest", str(l)), None]
            dcat = dcat + rs_started[l][0][4][0, 0]
        else:
            sib = sibling_start(gpack, "rs_rest_sib_start%d" % l)
            dcat = dcat + sib[4][0, 0]
        doa, dob, doc = dcat[:, :A_Q], dcat[:, A_Q:A_Q + B_V], dcat[:, A_Q + B_V:]
        daq, dak, dav, dbias, dsink = attn_bwd(sv["aq"], sv["ak"], sv["av"], bias, bias_t,
                                               sink_rows(full["attn_sink"][l]), doa, sv["oa"], "attn_bwd")
        grads["attn_sink"][l] = dsink[:, :A_GROUP, 0].reshape(A_HEADS)
        dbias_total = dbias if dbias_total is None else add_n([dbias_total, dbias], F32, "add_dbias")
        gf, gb = sv["gla"]["f"], sv["gla"]["b"]
        do_gla, dbg, dgn = gla_out_bwd(gf["o"], gb["o"], sv["bg"], full["gla_norm"][l], dob, "gla_out_bwd")
        grads["gla_norm"][l] = dgn.reshape(B_V)
        bq_t = sv["bq"].T
        acc = None
        dlas = {}
        for nm, gd, bdir in (("f", gf, False), ("b", gb, True)):
            wq = gla_outer(bq_t, gd["la_t"], do_gla, "gla_outer_q_" + nm, bdir, "qtil")
            gs = scan_lead(gd["dec"], wq, "gla_adj_scan_" + nm, reverse=not bdir, inclusive=False)
            dbq, dbk, dbv, dlas[nm] = gla_inner_bwd(sv["bq"], sv["bk"], sv["bv"], gd["la"], do_gla, gd["sp"], gs,
                                                    gd["dec"], "gla_inner_bwd_" + nm, bdir, add=acc)
            acc = (dbq, dbk, dbv)
        dzf, dzb, dpre_f, dpre_b, db2f, db2b = gla_gates_bwd(
            sv["zf"], sv["zb"], full["gla_w2_f"][l], full["gla_b2_f"][l], full["gla_w2_b"][l], full["gla_b2_b"][l],
            dlas["f"], dlas["b"], "gla_gates_bwd")
        grads["gla_b2_f"][l] = db2f.reshape(B_QK)
        grads["gla_b2_b"][l] = db2b.reshape(B_QK)
        grads["gla_w2_f"][l] = mm_plain(sv["zf"].T, dpre_f, "mm_dw_gate", hi=True)
        grads["gla_w2_b"][l] = mm_plain(sv["zb"].T, dpre_b, "mm_dw_gate", hi=True)
        dh, dcy = lru_out_bwd(sv["h0"], sv["h1"], sv["cy"], doc, "lru_out_bwd")
        g0 = scan_lead(lead(_shift(sv["a0"], 1)), lead(dh), "lru_scan_rev", reverse=True,
                       inclusive=True).reshape(t, C_WIDTH)
        g1 = scan_lead(lead(_shift(sv["a1"], -1)), lead(dh), "lru_scan_fwd", reverse=False,
                       inclusive=True).reshape(t, C_WIDTH)
        dxc, dzr0, dzi0, dzr1, dzi1, dlam, dba, dbx = lru_gates_bwd(
            sv["xc"], g0, _shift(sv["h0"], -1), g1, _shift(sv["h1"], 1), full["lru_wa"][l], full["lru_ba"][l],
            full["lru_wx"][l], full["lru_bx"][l], full["lru_lambda"][l], "lru_gates_bwd")
        xc_t = bf(sv["xc"].T)
        grads["lru_wa"][l] = jnp.stack([blockdiag_dw(xc_t, dzr0, "lru_dw"), blockdiag_dw(xc_t, dzr1, "lru_dw")])
        grads["lru_wx"][l] = jnp.stack([blockdiag_dw(xc_t, dzi0, "lru_dw"), blockdiag_dw(xc_t, dzi1, "lru_dw")])
        grads["lru_lambda"][l], grads["lru_ba"][l], grads["lru_bx"][l] = dlam, dba, dbx
        dcx, dcw, dcb = conv_bwd(dxc, _shift(dxc, 2), _shift(dxc, 1), _shift(dxc, -1), sv["cx"], sv["xm2"],
                                 sv["xm1"], sv["xp1"], full["conv_w"][l], "conv_bwd")
        grads["conv_w"][l] = dcw
        grads["conv_b"][l] = dcb.reshape(C_WIDTH)
        dproj_pad = join_dproj([daq, dak, dav, dbq, dbk, dbv, dbg, dzf, dzb, dcx, dcy])
        g_in_t = mm_plain(dproj_pad, sv["h_in"], "mm_dw_in", ta=True, out_dtype=BF16)
        dh1 = mm_wk(dproj_pad, gw_in, off_in, W_IN_ROWS, "mm_w_in_dx", jb=2)
        dx, grads["norm_mix_pre"][l] = rms_bwd(sv["x"], full["norm_mix_pre"][l], dh1, "rms_bwd_add", add=dx1)
        if l > 0:
            g5, theirs = sibling_wait(sib[0], sib[1], sib[2], sib[3], dx, "rs_rest_sib_wait%d" % l)
            chip_sum = add_own_half(g5, theirs, "rs_rest_add2")
            rs_started[l] = [chips_start(chip_sum, "rs_rest_start%d" % l), None]
            dx = dx + rs_started[l][0][4][0, 0]
        rs_started[l][1] = reduce_scatter_begin(g_in_t.reshape(N_DEV, W_IN_ROWS, d), "rs_in", str(l))
        dx = dx + rs_started[l][1][4][0, 0]

    grad_rel = mm_plain(dbias_total.reshape(A_HEADS, -1), onehot_t, "rel_bias_grad", tb=True, hi=True).T

    small_names = [n for n in WEIGHTS if n not in BIG]
    small_g = {"rel_bias": grad_rel}
    for n in small_names:
        if n != "rel_bias":
            small_g[n] = jnp.stack([g.reshape(full[n].shape[1:]) for g in grads[n]])
    shapes = [small_g[n].shape for n in small_names]
    small_started = gather_start(_pack([small_g[n] for n in small_names]), "ag_start_small_grads")
    dx = dx + small_started[4][0, 0]
    for l in range(depth):
        big_grads[l] = (reduce_scatter_end(rs_started[l][1], dx, "rs_in", str(l)),
                        reduce_scatter_end(rs_started[l][0], dx, "rs_rest", str(l)))

    grad_out, delta, new_m, new_v = {}, {}, {}, {}

    def rows(l, off, r):
        return big_grads[l][1][off:off + r]

    big_g = {
        "w_in": jnp.stack([big_grads[l][0][:W_IN_SHARD].T for l in range(depth)]),
        "w_out": jnp.stack([rows(l, off_out, r_out) for l in range(depth)]),
        "xq": jnp.stack([rows(l, off_xq, rx) for l in range(depth)]),
        "xk": jnp.stack([rows(l, off_xk, rx) for l in range(depth)]),
        "xv": jnp.stack([rows(l, off_xv, rx) for l in range(depth)]),
        "xo": jnp.stack([rows(l, off_xo, rx) for l in range(depth)]),
        "w_up": jnp.stack([rows(l, off_up, rf).T for l in range(depth)]),
        "w_down": jnp.stack([rows(l, off_down, rf) for l in range(depth)]),
    }
    for n in BIG:
        grad_out[n] = big_g[n]
        delta[n], new_m[n], new_v[n] = adamw(big_g[n], w[n], m[n], v[n], "adamw_" + n)

    packed = gather_finish(small_started, delta["w_down"], "ag_wait_small_grads")
    summed = sum_lead(packed, tuple(range(N_DEV)), F32, "add8_small")
    small_g = dict(zip(small_names, _unpack(summed, shapes)))
    for n in SMALL_SHARDED:
        wdt = w[n].shape[-1]
        small_g[n] = lax.dynamic_slice_in_dim(small_g[n], my_idx * wdt, wdt, axis=small_g[n].ndim - 1)

    direct = ("lru_wa", "lru_wx")
    packed_names = [n for n in small_names if n not in direct]
    sshapes = [w[n].shape for n in packed_names]
    ds, ms, vs = adamw(_pack([small_g[n] for n in packed_names]), _pack([w[n] for n in packed_names]),
                       _pack([m[n] for n in packed_names]), _pack([v[n] for n in packed_names]), "adamw_small")
    for n, d_, m_, v_ in zip(packed_names, _unpack(ds, sshapes), _unpack(ms, sshapes), _unpack(vs, sshapes)):
        grad_out[n], delta[n], new_m[n], new_v[n] = small_g[n], d_, m_, v_
    for n in direct:
        grad_out[n] = small_g[n]
        delta[n], new_m[n], new_v[n] = adamw(small_g[n], w[n], m[n], v[n], "adamw_lru")

    return (loss, dx.reshape(1, t, d), *[grad_out[n] for n in WEIGHTS], *[delta[n] for n in WEIGHTS],
            *[new_m[n] for n in WEIGHTS], *[new_v[n] for n in WEIGHTS])


def kernel(x, mem, rel_bias, w_in, w_out, attn_sink, gla_w2_f, gla_b2_f, gla_w2_b, gla_b2_b, gla_norm, conv_w, conv_b, lru_wa, lru_ba, lru_wx, lru_bx, lru_lambda, xq, xk, xv, xo, w_up, w_down, norm_mix_pre, norm_mix_post, norm_mem, norm_x_pre, norm_x_post, norm_ff_pre, norm_ff_post, loss_target, m_rel_bias, m_w_in, m_w_out, m_attn_sink, m_gla_w2_f, m_gla_b2_f, m_gla_w2_b, m_gla_b2_b, m_gla_norm, m_conv_w, m_conv_b, m_lru_wa, m_lru_ba, m_lru_wx, m_lru_bx, m_lru_lambda, m_xq, m_xk, m_xv, m_xo, m_w_up, m_w_down, m_norm_mix_pre, m_norm_mix_post, m_norm_mem, m_norm_x_pre, m_norm_x_post, m_norm_ff_pre, m_norm_ff_post, v_rel_bias, v_w_in, v_w_out, v_attn_sink, v_gla_w2_f, v_gla_b2_f, v_gla_w2_b, v_gla_b2_b, v_gla_norm, v_conv_w, v_conv_b, v_lru_wa, v_lru_ba, v_lru_wx, v_lru_bx, v_lru_lambda, v_xq, v_xk, v_xv, v_xo, v_w_up, v_w_down, v_norm_mix_pre, v_norm_mix_post, v_norm_mem, v_norm_x_pre, v_norm_x_post, v_norm_ff_pre, v_norm_ff_post):
    given = dict(locals())
    w = {n: given[n] for n in WEIGHTS}
    m = {n: given["m_" + n] for n in WEIGHTS}
    v = {n: given["v_" + n] for n in WEIGHTS}
    return _step(x, mem, loss_target, w, m, v)
```

```python
import math

import jax
import jax.numpy as jnp
import numpy as np
from jax import lax
from jax.experimental import pallas as pl
from jax.experimental.pallas import tpu as pltpu

F32 = jnp.float32
BF16 = jnp.bfloat16
HI = lax.Precision.HIGHEST
NN = (((1,), (0,)), ((), ()))
NT = (((1,), (1,)), ((), ()))
MESH = pl.DeviceIdType.MESH
AXES = ("x", "y", "c")
N_DEV = 8

A_HEAD_DIM = 128
A_HEADS = 8
A_KV_HEADS = 2
A_GROUP = 4
WINDOW = 128
BLOCK = 128
N_BUCKETS = 32
MAX_DISTANCE = 128
B_HEADS = 4
B_KEY_DIM = 64
B_VAL_DIM = 128
GATE_RANK = 16
GATE_TAU = 16.0
GLA_CHUNK = 16
C_WIDTH = 512
C_BLOCKS = 4
C_BLOCK_DIM = 128
LRU_C = 8.0
X_HEADS = 4
EPS = 1e-6
NEG_INF = -1e30
A_Q = A_HEADS * A_HEAD_DIM
A_KV = A_KV_HEADS * A_HEAD_DIM
B_QK = B_HEADS * B_KEY_DIM
B_V = B_HEADS * B_VAL_DIM
SPLIT_SIZES = (A_Q, A_KV, A_KV, B_QK, B_QK, B_V, B_V, GATE_RANK, GATE_RANK, C_WIDTH, C_WIDTH)
D_IN = sum(SPLIT_SIZES)
D_MIX = A_Q + B_V + C_WIDTH
W_IN_SHARD = D_IN // N_DEV
W_IN_ROWS = 768
GLA_TILE = 128
CHUNKS_PER_TILE = GLA_TILE // GLA_CHUNK
EXP_CLAMP = 80.0

ADAM_LR = 0.001
ADAM_B1 = 0.9
ADAM_B2 = 0.999
ADAM_EPS = 1e-08
ADAM_WD = 0.01
ADAM_STEP = 10

VMEM_LIMIT_BYTES = 52 * 1024 * 1024
MM_TILE = 1024
SIBLING_STREAMS = 16
PACK_ELEMS = 128 * 2048


def _call(body, **kw):
    return pl.pallas_call(body, **kw)


def _cparams():
    return pltpu.CompilerParams(vmem_limit_bytes=VMEM_LIMIT_BYTES)


def _dot(a, b, dims=NN, hi=False):
    if hi:
        return lax.dot_general(a, b, dims, precision=HI, preferred_element_type=F32)
    return lax.dot_general(a.astype(BF16), b.astype(BF16), dims, preferred_element_type=F32)


def _sds(shape, dtype=F32):
    return jax.ShapeDtypeStruct(tuple(shape), dtype)


def _row_tile(rows, cols, target_elems=1 << 18):
    want = max(8, target_elems // max(cols, 1))
    if rows <= want:
        return rows
    t = (want // 8) * 8
    while t >= 8:
        if rows % t == 0:
            return t
        t -= 8
    return rows


def _expm1(x):
    poly = x * (1.0 + x * (1.0 / 2 + x * (1.0 / 6 + x * (1.0 / 24 + x * (1.0 / 120 + x * (
        1.0 / 720 + x * (1.0 / 5040 + x * (1.0 / 40320))))))))
    return jnp.where(jnp.abs(x) < 0.3, poly, jnp.exp(x) - 1.0)


def _log1p(e):
    w = 1.0 + e
    return jnp.where(w == 1.0, e, jnp.log(w) * e / (w - 1.0))


def _softplus(x):
    return jnp.maximum(x, 0.0) + _log1p(jnp.exp(-jnp.abs(x)))


def _sigmoid(x):
    return jax.nn.sigmoid(x)


GELU_K = math.sqrt(2.0 / math.pi)


def _gelu(y):
    t = jnp.tanh(GELU_K * (y + 0.044715 * y * y * y))
    return 0.5 * y * (1.0 + t)


def _gelu_grad(y):
    t = jnp.tanh(GELU_K * (y + 0.044715 * y * y * y))
    return 0.5 * (1.0 + t) + 0.5 * y * (1.0 - t * t) * GELU_K * (1.0 + 3 * 0.044715 * y * y)


def rms_fwd(x, g, name):
    m, d = x.shape
    tm = _row_tile(m, d)

    def body(x_ref, g_ref, o_ref):
        xv = x_ref[...]
        r = lax.rsqrt(jnp.mean(xv * xv, axis=1, keepdims=True) + EPS)
        o_ref[...] = (xv * r * g_ref[...]).astype(o_ref.dtype)

    return _call(body, name=name, grid=(m // tm,),
                 in_specs=[pl.BlockSpec((tm, d), lambda i: (i, 0)), pl.BlockSpec((1, d), lambda i: (0, 0))],
                 out_specs=pl.BlockSpec((tm, d), lambda i: (i, 0)),
                 out_shape=_sds((m, d), BF16))(x, g.reshape(1, d))


def resid_rms(xres, mid, g_post, g_pre, name):
    m, d = xres.shape
    tm = _row_tile(m, d)
    with_pre = g_pre is not None

    def body(*refs):
        if with_pre:
            x_ref, m_ref, gp_ref, gn_ref, xo_ref, h_ref = refs
        else:
            x_ref, m_ref, gp_ref, xo_ref = refs
        mv = m_ref[...]
        r = lax.rsqrt(jnp.mean(mv * mv, axis=1, keepdims=True) + EPS)
        xn = x_ref[...] + mv * r * gp_ref[...]
        xo_ref[...] = xn
        if with_pre:
            r2 = lax.rsqrt(jnp.mean(xn * xn, axis=1, keepdims=True) + EPS)
            h_ref[...] = (xn * r2 * gn_ref[...]).astype(h_ref.dtype)

    row = pl.BlockSpec((tm, d), lambda i: (i, 0))
    vec = pl.BlockSpec((1, d), lambda i: (0, 0))
    ins = [xres, mid, g_post.reshape(1, d)] + ([g_pre.reshape(1, d)] if with_pre else [])
    in_specs = [row, row, vec] + ([vec] if with_pre else [])
    if with_pre:
        return _call(body, name=name, grid=(m // tm,), in_specs=in_specs, out_specs=(row, row),
                     out_shape=(_sds((m, d)), _sds((m, d), BF16)))(*ins)
    return _call(body, name=name, grid=(m // tm,), in_specs=in_specs, out_specs=row,
                 out_shape=_sds((m, d)))(*ins)


def rms_bwd(x, g, dy, name, dy2=None, add=None):
    m, d = x.shape
    tm = _row_tile(m, d)
    has2, hasadd = dy2 is not None, add is not None

    def body(*refs):
        it = iter(refs)
        x_ref, g_ref, dy_ref = next(it), next(it), next(it)
        dy2_ref = next(it) if has2 else None
        add_ref = next(it) if hasadd else None
        dx_ref, dg_ref = next(it), next(it)
        xv = x_ref[...]
        dyv = dy_ref[...]
        if has2:
            dyv = dyv + dy2_ref[...]
        r = lax.rsqrt(jnp.mean(xv * xv, axis=1, keepdims=True) + EPS)
        xh = xv * r
        dxh = dyv * g_ref[...]
        dx = r * (dxh - xh * jnp.mean(dxh * xh, axis=1, keepdims=True))
        if hasadd:
            dx = dx + add_ref[...]
        dx_ref[...] = dx
        part = jnp.sum(dyv * xh, axis=0, keepdims=True)

        @pl.when(pl.program_id(0) == 0)
        def _():
            dg_ref[...] = part

        @pl.when(pl.program_id(0) > 0)
        def _():
            dg_ref[...] += part

    row = pl.BlockSpec((tm, d), lambda i: (i, 0))
    vec = pl.BlockSpec((1, d), lambda i: (0, 0))
    ins = [x, g.reshape(1, d), dy] + ([dy2] if has2 else []) + ([add] if hasadd else [])
    in_specs = [row, vec, row] + ([row] if has2 else []) + ([row] if hasadd else [])
    return _call(body, name=name, grid=(m // tm,), in_specs=in_specs, out_specs=(row, vec),
                 out_shape=(_sds((m, d)), _sds((1, d))))(*ins)


def loss_and_grad(y, target, name):
    m, d = y.shape
    tm = _row_tile(m, d)

    def body(y_ref, t_ref, dy_ref, l_ref):
        e = y_ref[...] - t_ref[...]
        dy_ref[...] = e * (1.0 / d)
        s = jnp.sum(jnp.sum(e * e, axis=1, keepdims=True), axis=0, keepdims=True) * (0.5 / d)
        part = jnp.broadcast_to(s, (1, 128))

        @pl.when(pl.program_id(0) == 0)
        def _():
            l_ref[...] = part

        @pl.when(pl.program_id(0) > 0)
        def _():
            l_ref[...] += part

    row = pl.BlockSpec((tm, d), lambda i: (i, 0))
    dy, l = _call(body, name=name, grid=(m // tm,), in_specs=[row, row],
                  out_specs=(row, pl.BlockSpec((1, 128), lambda i: (0, 0))),
                  out_shape=(_sds((m, d)), _sds((1, 128))))(y, target)
    return dy, l[0, 0]


def adamw(g, w, m, v, name):
    shape = w.shape
    cols = shape[-1]
    rows = int(np.prod(shape[:-1]))
    tm = _row_tile(rows, cols)
    c1 = 1.0 - ADAM_B1 ** ADAM_STEP
    c2 = 1.0 - ADAM_B2 ** ADAM_STEP

    def body(g_ref, w_ref, m_ref, v_ref, d_ref, mo_ref, vo_ref):
        gv = g_ref[...]
        mn = ADAM_B1 * m_ref[...] + (1.0 - ADAM_B1) * gv
        vn = ADAM_B2 * v_ref[...] + (1.0 - ADAM_B2) * (gv * gv)
        m_hat = mn / c1
        v_hat = vn / c2
        d_ref[...] = -ADAM_LR * (m_hat / (jnp.sqrt(v_hat) + ADAM_EPS) + ADAM_WD * w_ref[...])
        mo_ref[...] = mn
        vo_ref[...] = vn

    row = pl.BlockSpec((tm, cols), lambda i: (i, 0))
    outs = _call(body, name=name, grid=(rows // tm,), in_specs=[row] * 4, out_specs=(row,) * 3,
                 out_shape=(_sds((rows, cols)),) * 3)(*[a.reshape(rows, cols) for a in (g, w, m, v)])
    return tuple(o.reshape(shape) for o in outs)


def sum_lead(x, order, out_dtype, name):
    n, rows, cols = x.shape
    tm = _row_tile(rows, cols)

    def body(x_ref, o_ref):
        acc = x_ref[order[0]].astype(F32)
        for i in order[1:]:
            acc = acc + x_ref[i].astype(F32)
        o_ref[...] = acc.astype(out_dtype)

    return _call(body, name=name, grid=(rows // tm,), in_specs=[pl.BlockSpec((n, tm, cols), lambda i: (0, i, 0))],
                 out_specs=pl.BlockSpec((tm, cols), lambda i: (i, 0)), out_shape=_sds((rows, cols), out_dtype))(x)


def add_own_lead(own, parts, name):
    n, rows, cols = parts.shape
    tm = _row_tile(rows, cols)

    def body(o_ref, p_ref, out_ref):
        acc = o_ref[...].astype(F32)
        for i in range(n):
            acc = acc + p_ref[i].astype(F32)
        out_ref[...] = acc

    row = pl.BlockSpec((tm, cols), lambda i: (i, 0))
    return _call(body, name=name, grid=(rows // tm,),
                 in_specs=[row, pl.BlockSpec((n, tm, cols), lambda i: (0, i, 0))], out_specs=row,
                 out_shape=_sds((rows, cols)))(own, parts)


def add_own_half(g5, theirs, name):
    _, _, r, c = g5.shape
    tm = _row_tile(r, c, 1 << 20)

    def body(cc_ref, g_ref, t_ref, o_ref):
        o_ref[...] = (g_ref[...].astype(F32) + t_ref[...].astype(F32)).astype(o_ref.dtype)

    grid_spec = pltpu.PrefetchScalarGridSpec(
        num_scalar_prefetch=1, grid=(4, r // tm),
        in_specs=[pl.BlockSpec((None, None, tm, c), lambda q, i, cc_ref: (q, cc_ref[0], i, 0)),
                  pl.BlockSpec((None, tm, c), lambda q, i, cc_ref: (q, i, 0))],
        out_specs=pl.BlockSpec((None, tm, c), lambda q, i, cc_ref: (q, i, 0)))
    return _call(body, name=name, grid_spec=grid_spec, out_shape=_sds((4, r, c), BF16))(
        lax.axis_index("c").astype(jnp.int32).reshape(1), g5, theirs)


def add_n(xs, out_dtype, name):
    shape = xs[0].shape
    cols = shape[-1]
    rows = int(np.prod(shape[:-1]))
    tm = _row_tile(rows, cols)
    n = len(xs)

    def body(*refs):
        acc = refs[0][...].astype(F32)
        for r in refs[1:n]:
            acc = acc + r[...].astype(F32)
        refs[n][...] = acc.astype(out_dtype)

    row = pl.BlockSpec((tm, cols), lambda i: (i, 0))
    out = _call(body, name=name, grid=(rows // tm,), in_specs=[row] * n, out_specs=row,
                out_shape=_sds((rows, cols), out_dtype))(*[a.reshape(rows, cols) for a in xs])
    return out.reshape(shape)


def mm_plain(a, b, name, ta=False, tb=False, out_dtype=F32, hi=False, tm=MM_TILE, tn=MM_TILE):
    k, m = a.shape[::1 if ta else -1]
    n = b.shape[0] if tb else b.shape[1]
    tm, tn = min(tm, m), min(tn, n)
    dims = (((0 if ta else 1,), (1 if tb else 0,)), ((), ()))

    def body(a_ref, b_ref, o_ref):
        o_ref[...] = _dot(a_ref[...], b_ref[...], dims, hi).astype(out_dtype)

    a_spec = pl.BlockSpec((k, tm), lambda j, i: (0, i)) if ta else pl.BlockSpec((tm, k), lambda j, i: (i, 0))
    b_spec = pl.BlockSpec((tn, k), lambda j, i: (j, 0)) if tb else pl.BlockSpec((k, tn), lambda j, i: (0, j))
    return _call(body, name=name, grid=(n // tn, m // tm), in_specs=[a_spec, b_spec],
                 out_specs=pl.BlockSpec((tm, tn), lambda j, i: (i, j)),
                 out_shape=_sds((m, n), out_dtype), compiler_params=_cparams())(a, b)


def mm_dw_into(a, b, buf, off, r, name, tn=MM_TILE):
    k, m = a.shape
    n = b.shape[1]
    tm, tn = min(MM_TILE, r), min(tn, n)
    assert m == N_DEV * r and off % tm == 0 and r % tm == 0
    per = r // tm
    dims = (((0,), (0,)), ((), ()))

    def body(a_ref, b_ref, buf_ref, o_ref):
        o_ref[...] = _dot(a_ref[...], b_ref[...], dims).astype(o_ref.dtype)

    return _call(body, name=name, grid=(n // tn, m // tm),
                 in_specs=[pl.BlockSpec((k, tm), lambda j, i: (0, i)), pl.BlockSpec((k, tn), lambda j, i: (0, j)),
                           pl.BlockSpec(memory_space=pl.ANY)],
                 out_specs=pl.BlockSpec((None, tm, tn), lambda j, i: (i // per, off // tm + i % per, j)),
                 out_shape=_sds(buf.shape, buf.dtype), input_output_aliases={2: 0},
                 compiler_params=_cparams())(a, b, buf)


def mm_wk(a, gw, off, r, name, jb=N_DEV, tm=MM_TILE, tn=MM_TILE):
    m = a.shape[0]
    d = gw.shape[2]
    tm, tn = min(tm, m), min(tn, d)
    nk = N_DEV // jb
    ob = off // r
    assert off % r == 0 and a.shape[1] == N_DEV * r

    def body(a_ref, b_ref, o_ref, *acc):
        av = a_ref[...].astype(BF16)
        p = _dot(av[:, 0:r], b_ref[0])
        for q in range(1, jb):
            p = p + _dot(av[:, q * r:(q + 1) * r], b_ref[q])
        if nk == 1:
            o_ref[...] = p
        else:
            kk = pl.program_id(2)

            @pl.when(kk == 0)
            def _():
                acc[0][...] = p

            @pl.when(kk > 0)
            def _():
                acc[0][...] += p

            @pl.when(kk == nk - 1)
            def _():
                o_ref[...] = acc[0][...]

    return _call(body, name=name, grid=(m // tm, d // tn, nk),
                 in_specs=[pl.BlockSpec((tm, jb * r), lambda i, j, k: (i, k)),
                           pl.BlockSpec((jb, r, tn), lambda i, j, k: (k, ob, j))],
                 out_specs=pl.BlockSpec((tm, tn), lambda i, j, k: (i, j)),
                 out_shape=_sds((m, d)),
                 scratch_shapes=([pltpu.VMEM((tm, tn), F32)] if nk > 1 else []),
                 compiler_params=_cparams())(a, gw)


def mm_wn(a, gw, off, r, name, relu_grad_of=None, out_dtype=F32, with_relu2=False, tm=MM_TILE):
    m, d = a.shape
    tm = min(tm, m)
    ob = off // r
    assert off % r == 0 and gw.shape[2] == d
    epi = relu_grad_of is not None

    def body(*refs):
        it = iter(refs)
        a_ref, b_ref = next(it), next(it)
        e_ref = next(it) if epi else None
        o_ref = next(it)
        p = _dot(a_ref[...], b_ref[...], NT)
        if epi:
            p = p * (2.0 * jnp.maximum(e_ref[...], 0.0))
        o_ref[...] = p.astype(out_dtype)
        if with_relu2:
            act_ref = next(it)
            act_ref[...] = jnp.square(jnp.maximum(p, 0.0)).astype(act_ref.dtype)

    blk = pl.BlockSpec((tm, r), lambda i, j: (i, j))
    in_specs = [pl.BlockSpec((tm, d), lambda i, j: (i, 0)), pl.BlockSpec((None, r, d), lambda i, j: (j, ob, 0))]
    ins = [a, gw]
    if epi:
        in_specs.append(blk)
        ins.append(relu_grad_of)
    out_shape = _sds((m, N_DEV * r), out_dtype)
    if with_relu2:
        return _call(body, name=name, grid=(m // tm, N_DEV), in_specs=in_specs, out_specs=(blk, blk),
                     out_shape=(out_shape, _sds((m, N_DEV * r), BF16)), compiler_params=_cparams())(*ins)
    return _call(body, name=name, grid=(m // tm, N_DEV), in_specs=in_specs, out_specs=blk,
                 out_shape=out_shape, compiler_params=_cparams())(*ins)


def blockdiag_dw(xt, dz, name):
    t = xt.shape[1]

    def body(a_ref, b_ref, o_ref):
        o_ref[...] = _dot(a_ref[...], b_ref[...])

    return _call(body, name=name, grid=(C_BLOCKS,),
                 in_specs=[pl.BlockSpec((C_BLOCK_DIM, t), lambda g: (g, 0)),
                           pl.BlockSpec((t, C_BLOCK_DIM), lambda g: (0, g))],
                 out_specs=pl.BlockSpec((None, C_BLOCK_DIM, C_BLOCK_DIM), lambda g: (g, 0, 0)),
                 out_shape=_sds((C_BLOCKS, C_BLOCK_DIM, C_BLOCK_DIM)))(xt, dz)


def _band_mask(n, nblk, transposed):
    shape = (3 * BLOCK, A_GROUP * BLOCK) if transposed else (A_GROUP * BLOCK, 3 * BLOCK)
    qi = lax.broadcasted_iota(jnp.int32, shape, 1 if transposed else 0) & (BLOCK - 1)
    kj = lax.broadcasted_iota(jnp.int32, shape, 0 if transposed else 1)
    lo = jnp.where(n > 0, 0, BLOCK)
    hi = jnp.where(n < nblk - 1, 3 * BLOCK, 2 * BLOCK)
    return (jnp.abs(kj - BLOCK - qi) <= WINDOW) & (kj >= lo) & (kj < hi)


def _band_rows(ref, n, nblk):
    starts = [jnp.maximum(n - 1, 0), n, jnp.minimum(n + 1, nblk - 1)]
    return jnp.concatenate([ref[pl.ds(pl.multiple_of(s * BLOCK, BLOCK), BLOCK), :] for s in starts], axis=0)


def _head_cols(j):
    return slice(j * A_HEAD_DIM, (j + 1) * A_HEAD_DIM)


def attn_fwd(q, k, v, bias, sink_b, name):
    t = q.shape[0]
    nblk = t // BLOCK
    scale = A_HEAD_DIM ** -0.5

    def body(q_ref, k_ref, v_ref, b_ref, s_ref, o_ref):
        n = pl.program_id(1)
        kb = _band_rows(k_ref, n, nblk).astype(BF16)
        vb = _band_rows(v_ref, n, nblk).astype(BF16)
        mask = _band_mask(n, nblk, False)
        q4 = jnp.concatenate([q_ref[:, _head_cols(j)] for j in range(A_GROUP)], axis=0)
        b4 = jnp.concatenate([b_ref[j] for j in range(A_GROUP)], axis=0)
        sk = jnp.concatenate([jnp.broadcast_to(s_ref[j:j + 1, 0:1], (BLOCK, 1)) for j in range(A_GROUP)], axis=0)
        s = jnp.where(mask, _dot(q4, kb, NT) * scale + b4, NEG_INF)
        mx = jnp.maximum(jnp.max(s, axis=1, keepdims=True), sk)
        p = jnp.exp(s - mx)
        den = jnp.sum(p, axis=1, keepdims=True) + jnp.exp(sk - mx)
        o4 = _dot(p * (1.0 / den), vb)
        for j in range(A_GROUP):
            o_ref[:, _head_cols(j)] = o4[j * BLOCK:(j + 1) * BLOCK, :]

    gw = A_GROUP * A_HEAD_DIM
    return _call(body, name=name, grid=(A_KV_HEADS, nblk),
                 in_specs=[pl.BlockSpec((BLOCK, gw), lambda g, n: (n, g)),
                           pl.BlockSpec((t, A_HEAD_DIM), lambda g, n: (0, g)),
                           pl.BlockSpec((t, A_HEAD_DIM), lambda g, n: (0, g)),
                           pl.BlockSpec((A_GROUP, BLOCK, 3 * BLOCK), lambda g, n: (g, 0, 0)),
                           pl.BlockSpec((None, 8, 128), lambda g, n: (g, 0, 0))],
                 out_specs=pl.BlockSpec((BLOCK, gw), lambda g, n: (n, g)),
                 out_shape=_sds((t, A_Q)))(q, k, v, bias, sink_b)


def attn_bwd(q, k, v, bias, bias_t, sink_b, do, o, name):
    t = q.shape[0]
    nblk = t // BLOCK
    scale = A_HEAD_DIM ** -0.5

    def body(q_ref, k_ref, v_ref, b_ref, bt_ref, s_ref, do_ref, o_ref, dq_ref, dk_ref, dv_ref, db_ref, ds_ref):
        n = pl.program_id(1)

        @pl.when(n == 0)
        def _():
            dk_ref[...] = jnp.zeros_like(dk_ref)
            dv_ref[...] = jnp.zeros_like(dv_ref)
            db_ref[...] = jnp.zeros_like(db_ref)
            ds_ref[...] = jnp.zeros_like(ds_ref)

        kb = _band_rows(k_ref, n, nblk).astype(BF16)
        vb = _band_rows(v_ref, n, nblk).astype(BF16)
        heads = range(A_GROUP)
        mask = _band_mask(n, nblk, False)
        mask_t = _band_mask(n, nblk, True)
        q4 = jnp.concatenate([q_ref[:, _head_cols(j)] for j in heads], axis=0).astype(BF16)
        do4 = jnp.concatenate([do_ref[:, _head_cols(j)] for j in heads], axis=0)
        doo = do4 * jnp.concatenate([o_ref[:, _head_cols(j)] for j in heads], axis=0)
        do4 = do4.astype(BF16)
        b4 = jnp.concatenate([b_ref[j] for j in heads], axis=0)
        bt4 = jnp.concatenate([bt_ref[j] for j in heads], axis=1)
        sk = jnp.concatenate([jnp.broadcast_to(s_ref[j:j + 1, 0:1], (BLOCK, 1)) for j in heads], axis=0)
        sk_t = jnp.concatenate([jnp.broadcast_to(s_ref[j:j + 1, 0:1], (1, BLOCK)) for j in heads], axis=1)
        s = jnp.where(mask, _dot(q4, kb, NT) * scale + b4, NEG_INF)
        mx = jnp.maximum(jnp.max(s, axis=1, keepdims=True), sk)
        p = jnp.exp(s - mx)
        den = jnp.sum(p, axis=1, keepdims=True) + jnp.exp(sk - mx)
        rden = 1.0 / den
        p = p * rden
        psink_delta = jnp.exp(sk - mx) * rden * jnp.sum(doo, axis=1, keepdims=True)
        dsc = p * (_dot(do4, vb, NT) - jnp.sum(doo, axis=1, keepdims=True))
        dq4 = _dot(dsc, kb) * scale
        for j in heads:
            rows = slice(j * BLOCK, (j + 1) * BLOCK)
            db_ref[j] += dsc[rows, :]
            ds_ref[j:j + 1, :] += jnp.broadcast_to(-jnp.sum(psink_delta[rows, :], axis=0, keepdims=True), (1, 128))
            dq_ref[:, _head_cols(j)] = dq4[rows, :]
        st = jnp.where(mask_t, _dot(kb, q4, NT) * scale + bt4, NEG_INF)
        mxt = jnp.maximum(jnp.max(st, axis=0, keepdims=True), sk_t)
        pt = jnp.exp(st - mxt)
        dent = jnp.sum(pt, axis=0, keepdims=True) + jnp.exp(sk_t - mxt)
        pt = pt * (1.0 / dent)
        delta_t = _dot(jnp.ones((8, A_HEAD_DIM), F32), doo, NT, hi=True)[0:1, :]
        dst = pt * (_dot(vb, do4, NT) - delta_t)
        dkb = _dot(dst, q4) * scale
        dvb = _dot(pt, do4)
        starts = [jnp.maximum(n - 1, 0), n, jnp.minimum(n + 1, nblk - 1)]
        for c, st_ in enumerate(starts):
            rows = pl.ds(pl.multiple_of(st_ * BLOCK, BLOCK), BLOCK)
            dk_ref[rows, :] += dkb[c * BLOCK:(c + 1) * BLOCK, :]
            dv_ref[rows, :] += dvb[c * BLOCK:(c + 1) * BLOCK, :]

    gw = A_GROUP * A_HEAD_DIM
    qspec = pl.BlockSpec((BLOCK, gw), lambda g, n: (n, g))
    kspec = pl.BlockSpec((t, A_HEAD_DIM), lambda g, n: (0, g))
    sspec = pl.BlockSpec((None, 8, 128), lambda g, n: (g, 0, 0))
    bspec = pl.BlockSpec((A_GROUP, BLOCK, 3 * BLOCK), lambda g, n: (g, 0, 0))
    btspec = pl.BlockSpec((A_GROUP, 3 * BLOCK, BLOCK), lambda g, n: (g, 0, 0))
    return _call(body, name=name, grid=(A_KV_HEADS, nblk),
                 in_specs=[qspec, kspec, kspec, bspec, btspec, sspec, qspec, qspec],
                 out_specs=(qspec, kspec, kspec, bspec, sspec),
                 out_shape=(_sds((t, A_Q)), _sds((t, A_KV)), _sds((t, A_KV)),
                            _sds((A_HEADS, BLOCK, 3 * BLOCK)), _sds((A_KV_HEADS, 8, 128))),
                 compiler_params=_cparams())(q, k, v, bias, bias_t, sink_b, do, o)


def xattn_fwd(q, k, v, name):
    t, d = q.shape
    ml = k.shape[0]
    dh = d // X_HEADS
    tq = min(512, t)
    scale = dh ** -0.5

    def body(q_ref, k_ref, v_ref, o_ref):
        s = _dot(q_ref[...], k_ref[...], NT) * scale
        p = jnp.exp(s - jnp.max(s, axis=1, keepdims=True))
        p = p * (1.0 / jnp.sum(p, axis=1, keepdims=True))
        o_ref[...] = _dot(p, v_ref[...])

    qspec = pl.BlockSpec((tq, dh), lambda h, i: (i, h))
    kspec = pl.BlockSpec((ml, dh), lambda h, i: (0, h))
    return _call(body, name=name, grid=(X_HEADS, t // tq), in_specs=[qspec, kspec, kspec], out_specs=qspec,
                 out_shape=_sds((t, d)))(q, k, v)


def xattn_bwd(q, k, v, o, do, name):
    t, d = q.shape
    ml = k.shape[0]
    dh = d // X_HEADS
    tq = min(512, t)
    scale = dh ** -0.5

    def body(q_ref, k_ref, v_ref, o_ref, do_ref, dq_ref, dk_ref, dv_ref):
        i = pl.program_id(1)
        qv, kv, vv = q_ref[...].astype(BF16), k_ref[...].astype(BF16), v_ref[...].astype(BF16)
        dov = do_ref[...]
        doo = dov * o_ref[...]
        dov = dov.astype(BF16)
        s = _dot(qv, kv, NT) * scale
        p = jnp.exp(s - jnp.max(s, axis=1, keepdims=True))
        p = p * (1.0 / jnp.sum(p, axis=1, keepdims=True))
        ds = p * (_dot(dov, vv, NT) - jnp.sum(doo, axis=1, keepdims=True))
        dq_ref[...] = _dot(ds, kv) * scale
        st = _dot(kv, qv, NT) * scale
        pt = jnp.exp(st - jnp.max(st, axis=0, keepdims=True))
        pt = pt * (1.0 / jnp.sum(pt, axis=0, keepdims=True))
        delta_t = _dot(jnp.ones((8, dh), F32), doo, NT, hi=True)[0:1, :]
        dst = pt * (_dot(vv, dov, NT) - delta_t)
        dkp = _dot(dst, qv) * scale
        dvp = _dot(pt, dov)

        @pl.when(i == 0)
        def _():
            dk_ref[...] = dkp
            dv_ref[...] = dvp

        @pl.when(i > 0)
        def _():
            dk_ref[...] += dkp
            dv_ref[...] += dvp

    qspec = pl.BlockSpec((tq, dh), lambda h, i: (i, h))
    kspec = pl.BlockSpec((ml, dh), lambda h, i: (0, h))
    return _call(body, name=name, grid=(X_HEADS, t // tq), in_specs=[qspec, kspec, kspec, qspec, qspec],
                 out_specs=(qspec, kspec, kspec),
                 out_shape=(_sds((t, d)), _sds((ml, d)), _sds((ml, d))))(q, k, v, o, do)


def scan_lead(a, u, name, reverse, inclusive):
    n, r, c = a.shape
    blk = max(1, min(n, (1 << 18) // (max(r, 8) * c)))
    while n % blk:
        blk -= 1
    nb = n // blk

    def body(a_ref, u_ref, o_ref, carry):
        @pl.when(pl.program_id(0) == 0)
        def _():
            carry[...] = jnp.zeros_like(carry)

        def step(s, h):
            idx = (blk - 1 - s) if reverse else s
            hn = a_ref[idx] * h + u_ref[idx]
            o_ref[idx] = hn if inclusive else h
            return hn

        carry[...] = lax.fori_loop(0, blk, step, carry[...])

    spec = pl.BlockSpec((blk, r, c), (lambda i: (nb - 1 - i, 0, 0)) if reverse else (lambda i: (i, 0, 0)))
    return _call(body, name=name, grid=(nb,), in_specs=[spec, spec], out_specs=spec,
                 out_shape=_sds((n, r, c)), scratch_shapes=[pltpu.VMEM((r, c), F32)])(a, u)


def _chunk_mats(bwd_dir):
    i = lax.broadcasted_iota(jnp.int32, (GLA_TILE, GLA_TILE), 0)
    j = lax.broadcasted_iota(jnp.int32, (GLA_TILE, GLA_TILE), 1)
    same = lax.shift_right_logical(i, 4) == lax.shift_right_logical(j, 4)
    if bwd_dir:
        cm, cm_t = same & (j >= i), same & (i >= j)
        mk, mk_t = same & (j > i), same & (i > j)
    else:
        cm, cm_t = same & (j <= i), same & (i <= j)
        mk, mk_t = same & (j <= i), same & (i <= j)
    f = lambda b: jnp.where(b, 1.0, 0.0).astype(F32)
    return f(cm), f(cm_t), mk, mk_t, f(same)


def gla_gates_fwd(zf, zb, w2f, b2f, w2b, b2b, name):
    t = zf.shape[0]
    tm = min(256, t)

    def body(zf_ref, zb_ref, wf_ref, bf_ref, wb_ref, bb_ref, lf_ref, lb_ref):
        lf_ref[...] = -_softplus(-(_dot(zf_ref[...], wf_ref[...], hi=True) + bf_ref[...])) / GATE_TAU
        lb_ref[...] = -_softplus(-(_dot(zb_ref[...], wb_ref[...], hi=True) + bb_ref[...])) / GATE_TAU

    zs = pl.BlockSpec((tm, GATE_RANK), lambda i: (i, 0))
    ws = pl.BlockSpec((GATE_RANK, B_QK), lambda i: (0, 0))
    bs = pl.BlockSpec((1, B_QK), lambda i: (0, 0))
    os_ = pl.BlockSpec((tm, B_QK), lambda i: (i, 0))
    return _call(body, name=name, grid=(t // tm,), in_specs=[zs, zs, ws, bs, ws, bs], out_specs=(os_, os_),
                 out_shape=(_sds((t, B_QK)),) * 2)(zf, zb, w2f, b2f.reshape(1, B_QK), w2b, b2b.reshape(1, B_QK))


def gla_gates_bwd(zf, zb, w2f, b2f, w2b, b2b, dlf, dlb, name):
    t = zf.shape[0]
    tm = min(256, t)

    def body(zf_ref, zb_ref, wf_ref, bf_ref, wb_ref, bb_ref, dlf_ref, dlb_ref,
             dzf_ref, dzb_ref, dpf_ref, dpb_ref, dbf_ref, dbb_ref):
        first = pl.program_id(0) == 0
        for z_ref, w_ref, b_ref, dl_ref, dz_ref, dp_ref, db_ref in (
                (zf_ref, wf_ref, bf_ref, dlf_ref, dzf_ref, dpf_ref, dbf_ref),
                (zb_ref, wb_ref, bb_ref, dlb_ref, dzb_ref, dpb_ref, dbb_ref)):
            pre = _dot(z_ref[...], w_ref[...], hi=True) + b_ref[...]
            dpre = dl_ref[...] * (1.0 / GATE_TAU) * _sigmoid(-pre)
            dp_ref[...] = dpre
            dz_ref[...] = _dot(dpre, w_ref[...], NT, hi=True)
            part = jnp.sum(dpre, axis=0, keepdims=True)

            @pl.when(first)
            def _():
                db_ref[...] = part

            @pl.when(jnp.logical_not(first))
            def _():
                db_ref[...] += part

    zs = pl.BlockSpec((tm, GATE_RANK), lambda i: (i, 0))
    ws = pl.BlockSpec((GATE_RANK, B_QK), lambda i: (0, 0))
    bs = pl.BlockSpec((1, B_QK), lambda i: (0, 0))
    os_ = pl.BlockSpec((tm, B_QK), lambda i: (i, 0))
    return _call(body, name=name, grid=(t // tm,), in_specs=[zs, zs, ws, bs, ws, bs, os_, os_],
                 out_specs=(zs, zs, os_, os_, bs, bs),
                 out_shape=(_sds((t, GATE_RANK)),) * 2 + (_sds((t, B_QK)),) * 2 + (_sds((1, B_QK)),) * 2)(
        zf, zb, w2f, b2f.reshape(1, B_QK), w2b, b2b.reshape(1, B_QK), dlf, dlb)


def gla_outer(xt, lat, y, name, bwd_dir, mode):
    t = y.shape[0]
    nchunk = t // GLA_CHUNK
    khat = mode == "khat"
    scale = B_KEY_DIM ** -0.5

    def body(xt_ref, lat_ref, y_ref, *outs):
        _, cm_t, _, _, same = _chunk_mats(bwd_dir)
        lat_v = lat_ref[...]
        bt = _dot(lat_v, cm_t, hi=True)
        if khat:
            mult = jnp.exp(_dot(lat_v, same, hi=True) - bt)
        else:
            mult = jnp.exp(bt) * scale
        xm = xt_ref[...] * mult
        lane = lax.shift_right_logical(lax.broadcasted_iota(jnp.int32, (1, GLA_TILE), 1), 4)
        ones = jnp.ones((GLA_TILE, B_VAL_DIM), F32)
        yv = [y_ref[:, h * B_VAL_DIM:(h + 1) * B_VAL_DIM].astype(BF16) for h in range(B_HEADS)]
        for c in range(CHUNKS_PER_TILE):
            sel = lane == c
            xc = jnp.where(sel, xm, 0.0).astype(BF16)
            for h in range(B_HEADS):
                rows = slice(h * B_KEY_DIM, (h + 1) * B_KEY_DIM)
                outs[0][c, rows, :] = _dot(xc[rows, :], yv[h])
            if khat:
                outs[1][c] = jnp.exp(_dot(jnp.where(sel, lat_v, 0.0), ones, hi=True))

    tspec = pl.BlockSpec((B_QK, GLA_TILE), lambda i: (0, i))
    ospec = pl.BlockSpec((CHUNKS_PER_TILE, B_QK, B_VAL_DIM), lambda i: (i, 0, 0))
    oshape = _sds((nchunk, B_QK, B_VAL_DIM))
    return _call(body, name=name, grid=(t // GLA_TILE,),
                 in_specs=[tspec, tspec, pl.BlockSpec((GLA_TILE, B_V), lambda i: (i, 0))],
                 out_specs=(ospec, ospec) if khat else ospec,
                 out_shape=(oshape, oshape) if khat else oshape)(xt, lat, y)


def _head_lane_mask(h):
    lane = lax.broadcasted_iota(jnp.int32, (1, B_QK), 1)
    return lax.shift_right_logical(lane, 6) == h


def _chunk_rows(c):
    return slice(c * GLA_CHUNK, (c + 1) * GLA_CHUNK)


def gla_inner_fwd(q, k, v, la, sp, name, bwd_dir):
    t = q.shape[0]
    scale = B_KEY_DIM ** -0.5

    def body(q_ref, k_ref, v_ref, la_ref, sp_ref, o_ref):
        cm, _, mk, _, _ = _chunk_mats(bwd_dir)
        b = _dot(cm, la_ref[...], hi=True)
        qt = q_ref[...] * scale * jnp.exp(b)
        kt = k_ref[...] * jnp.exp(jnp.minimum(-b, EXP_CLAMP))
        spb = [sp_ref[c].astype(BF16) for c in range(CHUNKS_PER_TILE)]
        for h in range(B_HEADS):
            lm = _head_lane_mask(h)
            qm = jnp.where(lm, qt, 0.0).astype(BF16)
            km = jnp.where(lm, kt, 0.0).astype(BF16)
            vs = slice(h * B_VAL_DIM, (h + 1) * B_VAL_DIM)
            att = jnp.where(mk, _dot(qm, km, NT), 0.0)
            inter = jnp.concatenate([_dot(qm[_chunk_rows(c), :], spb[c]) for c in range(CHUNKS_PER_TILE)], axis=0)
            o_ref[:, vs] = _dot(att, v_ref[:, vs]) + inter

    qs = pl.BlockSpec((GLA_TILE, B_QK), lambda i: (i, 0))
    vs_ = pl.BlockSpec((GLA_TILE, B_V), lambda i: (i, 0))
    ss = pl.BlockSpec((CHUNKS_PER_TILE, B_QK, B_VAL_DIM), lambda i: (i, 0, 0))
    return _call(body, name=name, grid=(t // GLA_TILE,), in_specs=[qs, qs, vs_, qs, ss], out_specs=vs_,
                 out_shape=_sds((t, B_V)))(q, k, v, la, sp)


def gla_inner_bwd(q, k, v, la, do, sp, gs, dec, name, bwd_dir, add=None):
    t = q.shape[0]
    scale = B_KEY_DIM ** -0.5
    hasadd = add is not None

    def body(*refs):
        it = iter(refs)
        q_ref, k_ref, v_ref, la_ref, do_ref, sp_ref, gs_ref, dec_ref = [next(it) for _ in range(8)]
        adds = [next(it) for _ in range(3)] if hasadd else None
        dq_ref, dk_ref, dv_ref, dla_ref = [next(it) for _ in range(4)]
        cm, cm_t, mk, mk_t, same = _chunk_mats(bwd_dir)
        la_v = la_ref[...]
        b = _dot(cm, la_v, hi=True)
        btot = _dot(same, la_v, hi=True)
        eb = jnp.exp(b)
        ek = jnp.exp(jnp.minimum(-b, EXP_CLAMP))
        ekh = jnp.exp(btot - b)
        qt = q_ref[...] * scale * eb
        kt = k_ref[...] * ek
        kh = k_ref[...] * ekh
        spb = [sp_ref[c].astype(BF16) for c in range(CHUNKS_PER_TILE)]
        gsb = [gs_ref[c].astype(BF16) for c in range(CHUNKS_PER_TILE)]
        dqt = jnp.zeros((GLA_TILE, B_QK), F32)
        dkt = jnp.zeros((GLA_TILE, B_QK), F32)
        dkh = jnp.zeros((GLA_TILE, B_QK), F32)
        for h in range(B_HEADS):
            lm = _head_lane_mask(h)
            qm = jnp.where(lm, qt, 0.0).astype(BF16)
            km = jnp.where(lm, kt, 0.0).astype(BF16)
            khm = jnp.where(lm, kh, 0.0).astype(BF16)
            vs = slice(h * B_VAL_DIM, (h + 1) * B_VAL_DIM)
            vh = v_ref[:, vs].astype(BF16)
            doh = do_ref[:, vs].astype(BF16)
            da = jnp.where(mk, _dot(doh, vh, NT), 0.0)
            da_t = jnp.where(mk_t, _dot(vh, doh, NT), 0.0)
            att_t = jnp.where(mk_t, _dot(km, qm, NT), 0.0)
            dv_h = _dot(att_t, doh) + jnp.concatenate(
                [_dot(khm[_chunk_rows(c), :], gsb[c]) for c in range(CHUNKS_PER_TILE)], axis=0)
            if hasadd:
                dv_h = dv_h + adds[2][:, vs]
            dv_ref[:, vs] = dv_h
            dq_inter = jnp.concatenate(
                [_dot(doh[_chunk_rows(c), :], spb[c], NT) for c in range(CHUNKS_PER_TILE)], axis=0)
            dqt = dqt + _dot(da, km) + jnp.where(lm, dq_inter, 0.0)
            dkt = dkt + _dot(da_t, qm)
            dkh_inter = jnp.concatenate(
                [_dot(vh[_chunk_rows(c), :], gsb[c], NT) for c in range(CHUNKS_PER_TILE)], axis=0)
            dkh = dkh + jnp.where(lm, dkh_inter, 0.0)
        dq = dqt * scale * eb
        dk = dkt * ek + dkh * ekh
        if hasadd:
            dq = dq + adds[0][...]
            dk = dk + adds[1][...]
        dq_ref[...] = dq
        dk_ref[...] = dk
        db = dqt * qt - dkt * kt - dkh * kh
        ones16 = jnp.ones((GLA_CHUNK, B_VAL_DIM), F32)
        t2 = jnp.concatenate(
            [_dot(ones16, gs_ref[c] * dec_ref[c] * sp_ref[c], NT, hi=True) for c in range(CHUNKS_PER_TILE)], axis=0)
        dla_ref[...] = _dot(cm_t, db, hi=True) + _dot(same, dkh * kh, hi=True) + t2

    qs = pl.BlockSpec((GLA_TILE, B_QK), lambda i: (i, 0))
    vs_ = pl.BlockSpec((GLA_TILE, B_V), lambda i: (i, 0))
    ss = pl.BlockSpec((CHUNKS_PER_TILE, B_QK, B_VAL_DIM), lambda i: (i, 0, 0))
    ins = [q, k, v, la, do, sp, gs, dec] + (list(add) if hasadd else [])
    in_specs = [qs, qs, vs_, qs, vs_, ss, ss, ss] + ([qs, qs, vs_] if hasadd else [])
    return _call(body, name=name, grid=(t // GLA_TILE,), in_specs=in_specs, out_specs=(qs, qs, vs_, qs),
                 out_shape=(_sds((t, B_QK)), _sds((t, B_QK)), _sds((t, B_V)), _sds((t, B_QK))),
                 compiler_params=_cparams())(*ins)


def gla_out_fwd(of, ob, g, gn, name):
    t = of.shape[0]
    tm = min(256, t)

    def body(of_ref, ob_ref, g_ref, gn_ref, o_ref):
        for h in range(B_HEADS):
            vs = slice(h * B_VAL_DIM, (h + 1) * B_VAL_DIM)
            o = of_ref[:, vs] + ob_ref[:, vs]
            on = o * lax.rsqrt(jnp.mean(o * o, axis=1, keepdims=True) + EPS)
            gv = g_ref[:, vs]
            o_ref[:, vs] = on * gn_ref[:, vs] * (gv * _sigmoid(gv))

    row = pl.BlockSpec((tm, B_V), lambda i: (i, 0))
    vec = pl.BlockSpec((1, B_V), lambda i: (0, 0))
    return _call(body, name=name, grid=(t // tm,), in_specs=[row, row, row, vec], out_specs=row,
                 out_shape=_sds((t, B_V)))(of, ob, g, gn.reshape(1, B_V))


def gla_out_bwd(of, ob, g, gn, dout, name):
    t = of.shape[0]
    tm = min(256, t)

    def body(of_ref, ob_ref, g_ref, gn_ref, d_ref, do_ref, dg_ref, dgn_ref):
        first = pl.program_id(0) == 0
        for h in range(B_HEADS):
            vs = slice(h * B_VAL_DIM, (h + 1) * B_VAL_DIM)
            o = of_ref[:, vs] + ob_ref[:, vs]
            r = lax.rsqrt(jnp.mean(o * o, axis=1, keepdims=True) + EPS)
            on = o * r
            gv = g_ref[:, vs]
            sg = _sigmoid(gv)
            silu = gv * sg
            dv = d_ref[:, vs]
            gnv = gn_ref[:, vs]
            dg_ref[:, vs] = dv * on * gnv * (sg * (1.0 + gv * (1.0 - sg)))
            don = dv * silu * gnv
            do_ref[:, vs] = r * (don - on * jnp.mean(don * on, axis=1, keepdims=True))
            part = jnp.sum(dv * silu * on, axis=0, keepdims=True)

            @pl.when(first)
            def _():
                dgn_ref[:, vs] = part

            @pl.when(jnp.logical_not(first))
            def _():
                dgn_ref[:, vs] += part

    row = pl.BlockSpec((tm, B_V), lambda i: (i, 0))
    vec = pl.BlockSpec((1, B_V), lambda i: (0, 0))
    return _call(body, name=name, grid=(t // tm,), in_specs=[row, row, row, vec, row], out_specs=(row, row, vec),
                 out_shape=(_sds((t, B_V)), _sds((t, B_V)), _sds((1, B_V))))(of, ob, g, gn.reshape(1, B_V), dout)


def _shift(x, k):
    if k > 0:
        return jnp.concatenate([x[k:], jnp.zeros((k,) + x.shape[1:], x.dtype)], axis=0)
    return jnp.concatenate([jnp.zeros((-k,) + x.shape[1:], x.dtype), x[:k]], axis=0)


def _lru_gates(xc, s, wa_ref, ba_ref, wx_ref, bx_ref, lam_ref):
    cols = [slice(g * C_BLOCK_DIM, (g + 1) * C_BLOCK_DIM) for g in range(C_BLOCKS)]
    zr = jnp.concatenate([_dot(xc[:, cs], wa_ref[s, g]) for g, cs in enumerate(cols)], axis=1) + ba_ref[s:s + 1, :]
    zi = jnp.concatenate([_dot(xc[:, cs], wx_ref[s, g]) for g, cs in enumerate(cols)], axis=1) + bx_ref[s:s + 1, :]
    r = _sigmoid(zr)
    i = _sigmoid(zi)
    sp = _softplus(-lam_ref[s:s + 1, :])
    log_a = -LRU_C * r * sp
    return r, i, sp, log_a


def lru_gates_fwd(x0, xm2, xm1, xp1, cw, cb, wa, ba, wx, bx, lam, name):
    t = x0.shape[0]
    tm = min(256, t)

    def body(x0_ref, xm2_ref, xm1_ref, xp1_ref, cw_ref, cb_ref, wa_ref, ba_ref, wx_ref, bx_ref, lam_ref,
             xc_ref, a0_ref, u0_ref, a1_ref, u1_ref):
        xc = (xm2_ref[...] * cw_ref[0:1, :] + xm1_ref[...] * cw_ref[1:2, :] + x0_ref[...] * cw_ref[2:3, :]
              + xp1_ref[...] * cw_ref[3:4, :] + cb_ref[...])
        xc_ref[...] = xc
        for s, (a_ref, u_ref) in enumerate(((a0_ref, u0_ref), (a1_ref, u1_ref))):
            _, i, _, log_a = _lru_gates(xc, s, wa_ref, ba_ref, wx_ref, bx_ref, lam_ref)
            a_ref[...] = jnp.exp(log_a)
            u_ref[...] = jnp.sqrt(-_expm1(2.0 * log_a)) * (i * xc)

    row = pl.BlockSpec((tm, C_WIDTH), lambda i: (i, 0))
    full = lambda shape: pl.BlockSpec(shape, lambda i: (0,) * len(shape))
    wshape = (2, C_BLOCKS, C_BLOCK_DIM, C_BLOCK_DIM)
    return _call(body, name=name, grid=(t // tm,),
                 in_specs=[row] * 4 + [full((4, C_WIDTH)), full((1, C_WIDTH)), full(wshape), full((2, C_WIDTH)),
                                       full(wshape), full((2, C_WIDTH)), full((2, C_WIDTH))],
                 out_specs=(row,) * 5, out_shape=(_sds((t, C_WIDTH)),) * 5)(
        x0, xm2, xm1, xp1, cw, cb.reshape(1, C_WIDTH), wa, ba, wx, bx, lam)


def lru_gates_bwd(xc, g0, hs0, g1, hs1, wa, ba, wx, bx, lam, name):
    t = xc.shape[0]
    tm = min(256, t)

    def body(xc_ref, g0_ref, hs0_ref, g1_ref, hs1_ref, wa_ref, ba_ref, wx_ref, bx_ref, lam_ref,
             dxc_ref, dzr0_ref, dzi0_ref, dzr1_ref, dzi1_ref, dlam_ref, dba_ref, dbx_ref):
        first = pl.program_id(0) == 0

        @pl.when(first)
        def _():
            dlam_ref[...] = jnp.zeros_like(dlam_ref)
            dba_ref[...] = jnp.zeros_like(dba_ref)
            dbx_ref[...] = jnp.zeros_like(dbx_ref)

        xcv = xc_ref[...]
        dxc = jnp.zeros_like(xcv)
        cols = [slice(g * C_BLOCK_DIM, (g + 1) * C_BLOCK_DIM) for g in range(C_BLOCKS)]
        for s, (g_ref, hs_ref, dzr_ref, dzi_ref) in enumerate(
                ((g0_ref, hs0_ref, dzr0_ref, dzi0_ref), (g1_ref, hs1_ref, dzr1_ref, dzi1_ref))):
            r, i, sp, log_a = _lru_gates(xcv, s, wa_ref, ba_ref, wx_ref, bx_ref, lam_ref)
            du = g_ref[...]
            da = du * hs_ref[...]
            a = jnp.exp(log_a)
            e2 = jnp.exp(2.0 * log_a)
            c = jnp.sqrt(-_expm1(2.0 * log_a))
            ix = i * xcv
            dlog = da * a - du * ix * (e2 / c)
            dix = du * c
            dxc = dxc + dix * i
            dzi = dix * xcv * i * (1.0 - i)
            dzr = dlog * (-LRU_C * sp) * r * (1.0 - r)
            dzr_ref[...] = dzr
            dzi_ref[...] = dzi
            dxc = dxc + jnp.concatenate(
                [_dot(dzr[:, cs], wa_ref[s, g], NT) + _dot(dzi[:, cs], wx_ref[s, g], NT) for g, cs in enumerate(cols)],
                axis=1)
            dsp = jnp.sum(dlog * (-LRU_C * r), axis=0, keepdims=True)
            dlam_ref[s:s + 1, :] += dsp * (-_sigmoid(-lam_ref[s:s + 1, :]))
            dba_ref[s:s + 1, :] += jnp.sum(dzr, axis=0, keepdims=True)
            dbx_ref[s:s + 1, :] += jnp.sum(dzi, axis=0, keepdims=True)
        dxc_ref[...] = dxc

    row = pl.BlockSpec((tm, C_WIDTH), lambda i: (i, 0))
    full = lambda shape: pl.BlockSpec(shape, lambda i: (0,) * len(shape))
    wshape = (2, C_BLOCKS, C_BLOCK_DIM, C_BLOCK_DIM)
    vec2 = full((2, C_WIDTH))
    return _call(body, name=name, grid=(t // tm,),
                 in_specs=[row] * 5 + [full(wshape), vec2, full(wshape), vec2, vec2],
                 out_specs=(row,) * 5 + (vec2,) * 3,
                 out_shape=(_sds((t, C_WIDTH)),) * 5 + (_sds((2, C_WIDTH)),) * 3)(
        xc, g0, hs0, g1, hs1, wa, ba, wx, bx, lam)


def lru_out_fwd(h0, h1, y, name):
    t = y.shape[0]
    tm = min(256, t)

    def body(h0_ref, h1_ref, y_ref, o_ref):
        o_ref[...] = (h0_ref[...] + h1_ref[...]) * _gelu(y_ref[...])

    row = pl.BlockSpec((tm, C_WIDTH), lambda i: (i, 0))
    return _call(body, name=name, grid=(t // tm,), in_specs=[row] * 3, out_specs=row,
                 out_shape=_sds((t, C_WIDTH)))(h0, h1, y)


def lru_out_bwd(h0, h1, y, dout, name):
    t = y.shape[0]
    tm = min(256, t)

    def body(h0_ref, h1_ref, y_ref, d_ref, dh_ref, dy_ref):
        yv = y_ref[...]
        dv = d_ref[...]
        dh_ref[...] = dv * _gelu(yv)
        dy_ref[...] = dv * (h0_ref[...] + h1_ref[...]) * _gelu_grad(yv)

    row = pl.BlockSpec((tm, C_WIDTH), lambda i: (i, 0))
    return _call(body, name=name, grid=(t // tm,), in_specs=[row] * 4, out_specs=(row, row),
                 out_shape=(_sds((t, C_WIDTH)),) * 2)(h0, h1, y, dout)


def conv_bwd(dxc, dp2, dp1, dm1, x0, xm2, xm1, xp1, cw, name):
    t = x0.shape[0]
    tm = min(256, t)

    def body(d_ref, dp2_ref, dp1_ref, dm1_ref, x0_ref, xm2_ref, xm1_ref, xp1_ref, cw_ref, dx_ref, dcw_ref, dcb_ref):
        @pl.when(pl.program_id(0) == 0)
        def _():
            dcw_ref[...] = jnp.zeros_like(dcw_ref)
            dcb_ref[...] = jnp.zeros_like(dcb_ref)

        dv = d_ref[...]
        dx_ref[...] = (dp2_ref[...] * cw_ref[0:1, :] + dp1_ref[...] * cw_ref[1:2, :] + dv * cw_ref[2:3, :]
                       + dm1_ref[...] * cw_ref[3:4, :])
        for j, x_ref in enumerate((xm2_ref, xm1_ref, x0_ref, xp1_ref)):
            dcw_ref[j:j + 1, :] += jnp.sum(dv * x_ref[...], axis=0, keepdims=True)
        dcb_ref[...] += jnp.sum(dv, axis=0, keepdims=True)

    row = pl.BlockSpec((tm, C_WIDTH), lambda i: (i, 0))
    cws = pl.BlockSpec((4, C_WIDTH), lambda i: (0, 0))
    cbs = pl.BlockSpec((1, C_WIDTH), lambda i: (0, 0))
    return _call(body, name=name, grid=(t // tm,), in_specs=[row] * 8 + [cws], out_specs=(row, cws, cbs),
                 out_shape=(_sds((t, C_WIDTH)), _sds((4, C_WIDTH)), _sds((1, C_WIDTH))))(
        dxc, dp2, dp1, dm1, x0, xm2, xm1, xp1, cw)


def _my_place():
    return lax.axis_index("x"), lax.axis_index("y"), lax.axis_index("c")


def all_gather(xs, name):
    r, c = xs.shape

    def body(x_ref, out_ref, send_sems, recv_sems, local_sem):
        x, y, cc = _my_place()
        me, sibling = (x, y, cc), (x, y, 1 - cc)
        chips = [(1 - x, y), (x, 1 - y), (1 - x, 1 - y)]

        def slot(px, py, pc):
            return out_ref.at[4 * px + 2 * py + pc]

        def copy(k, block, to, src=None):
            return pltpu.make_async_remote_copy(
                src_ref=slot(*block) if src is None else src, dst_ref=slot(*block),
                send_sem=send_sems.at[k], recv_sem=recv_sems.at[k], device_id=to, device_id_type=MESH)

        mine = pltpu.make_async_copy(x_ref, slot(*me), local_sem)
        mine.start()
        first = [copy(0, me, sibling, src=x_ref)]
        first += [copy(1 + j, me, (*chip, cc), src=x_ref) for j, chip in enumerate(chips)]
        for cp in first:
            cp.start()
        passed = [copy(4 + j, (*chip, cc), sibling) for j, chip in enumerate(chips)]
        for j, chip in enumerate(chips):
            copy(1 + j, (*chip, cc), me).wait_recv()
            passed[j].start()
        copy(0, sibling, me).wait_recv()
        for j, chip in enumerate(chips):
            copy(4 + j, (*chip, 1 - cc), me).wait_recv()
        for cp in first + passed:
            cp.wait_send()
        mine.wait()

    return _call(body, name=name, in_specs=[pl.BlockSpec(memory_space=pl.ANY)],
                 out_specs=pl.BlockSpec(memory_space=pl.ANY), out_shape=_sds((N_DEV, r, c), xs.dtype),
                 scratch_shapes=[pltpu.SemaphoreType.DMA((7,)), pltpu.SemaphoreType.DMA((7,)),
                                 pltpu.SemaphoreType.DMA])(xs)


def _stream_rows(r):
    nch = SIBLING_STREAMS // 4 if r % (8 * (SIBLING_STREAMS // 4)) == 0 else 1
    return nch, r // nch


def exchange_sibling(gw, name):
    _, r, c = gw.shape
    g5 = gw.reshape(4, 2, r, c)
    nch, rows = _stream_rows(r)

    def body(g_ref, out_ref, send_sems, recv_sems):
        x, y, cc = _my_place()
        swaps = []
        for q in range(4):
            for s in range(nch):
                k = q * nch + s
                win = pl.ds(s * rows, rows)
                swaps.append(pltpu.make_async_remote_copy(
                    src_ref=g_ref.at[q, 1 - cc, win], dst_ref=out_ref.at[q, win], send_sem=send_sems.at[k],
                    recv_sem=recv_sems.at[k], device_id=(x, y, 1 - cc), device_id_type=MESH))
        for cp in swaps:
            cp.start()
        for cp in swaps:
            cp.wait()

    nsem = 4 * nch
    return _call(body, name=name, in_specs=[pl.BlockSpec(memory_space=pl.ANY)],
                 out_specs=pl.BlockSpec(memory_space=pl.ANY), out_shape=_sds((4, r, c), gw.dtype),
                 scratch_shapes=[pltpu.SemaphoreType.DMA((nsem,)), pltpu.SemaphoreType.DMA((nsem,))])(g5)


HBM_SPEC = pl.BlockSpec(memory_space=pltpu.HBM)
SEM_SPEC = pl.BlockSpec(memory_space=pltpu.SEMAPHORE)
DATAFLOW = pltpu.SideEffectType.DATAFLOW_SIDE_EFFECTING


def _hbm(a):
    return pltpu.with_memory_space_constraint(a, pltpu.HBM)


def _peers(x, y, cc):
    return [(x, y, 1 - cc), (1 - x, y, cc), (x, 1 - y, cc), (1 - x, 1 - y, cc)]


def _slot(p):
    return 4 * p[0] + 2 * p[1] + p[2]


def gather_start(blk, name):
    r, c = blk.shape

    def body(v_ref, land_ref, send_sems, recv_sems, v_thru, land_thru, token):
        x, y, cc = _my_place()
        for k, to in enumerate(_peers(x, y, cc)):
            pltpu.make_async_remote_copy(
                src_ref=v_ref, dst_ref=land_ref.at[_slot((x, y, cc))], send_sem=send_sems.at[k],
                recv_sem=recv_sems.at[k], device_id=to, device_id_type=MESH).start()
        pltpu.make_async_copy(v_ref, land_ref.at[_slot((x, y, cc))], send_sems.at[4]).start()
        token[...] = jnp.zeros_like(token)

    return _call(
        body, name=name,
        out_shape=(pltpu.SemaphoreType.DMA((5,)), pltpu.SemaphoreType.DMA((4,)), pltpu.HBM((r, c), blk.dtype),
                   pltpu.HBM((N_DEV, r, c), blk.dtype), _sds((8, 128))),
        in_specs=(HBM_SPEC, HBM_SPEC),
        out_specs=(SEM_SPEC, SEM_SPEC, HBM_SPEC, HBM_SPEC, pl.BlockSpec(memory_space=pltpu.VMEM)),
        input_output_aliases={0: 2, 1: 3},
        compiler_params=pltpu.CompilerParams(has_side_effects=DATAFLOW),
    )(_hbm(blk), _hbm(lax.empty((N_DEV, r, c), blk.dtype)))


def gather_wait(send_sems, recv_sems, v_thru, land_thru, after, name):
    def body(v_ref, land_ref, send_sems, recv_sems, after_ref, v_out, land_out):
        x, y, cc = _my_place()
        for k, peer in enumerate(_peers(x, y, cc)):
            cp = pltpu.make_async_remote_copy(
                src_ref=v_ref, dst_ref=land_ref.at[_slot(peer)], send_sem=send_sems.at[k], recv_sem=recv_sems.at[k],
                device_id=peer, device_id_type=MESH)
            cp.wait_send()
            cp.wait_recv()
        pltpu.make_async_copy(v_ref, land_ref.at[_slot((x, y, cc))], send_sems.at[4]).wait()

    return _call(
        body, name=name,
        out_shape=(pltpu.HBM(v_thru.shape, v_thru.dtype), pltpu.HBM(land_thru.shape, land_thru.dtype)),
        in_specs=(HBM_SPEC, HBM_SPEC, SEM_SPEC, SEM_SPEC, pl.BlockSpec(memory_space=pl.ANY)),
        out_specs=(HBM_SPEC, HBM_SPEC), input_output_aliases={0: 0, 1: 1},
        compiler_params=pltpu.CompilerParams(has_side_effects=DATAFLOW),
    )(v_thru, land_thru, send_sems, recv_sems, after)


def gather_pass(land, name):
    _, r, c = land.shape
    nch, rows = _stream_rows(r)

    def body(land_ref, out_ref, send_sems, recv_sems):
        x, y, cc = _my_place()
        peers = _peers(x, y, cc)
        copies = []
        for j in range(3):
            mine, theirs = _slot(peers[1 + j]), _slot((peers[1 + j][0], peers[1 + j][1], 1 - cc))
            for s in range(nch):
                k = j * nch + s
                win = pl.ds(s * rows, rows)
                send = pltpu.make_async_remote_copy(
                    src_ref=land_ref.at[mine, win], dst_ref=out_ref.at[mine, win], send_sem=send_sems.at[k],
                    recv_sem=recv_sems.at[k], device_id=peers[0], device_id_type=MESH)
                recv = pltpu.make_async_remote_copy(
                    src_ref=land_ref.at[mine, win], dst_ref=out_ref.at[theirs, win], send_sem=send_sems.at[k],
                    recv_sem=recv_sems.at[k], device_id=peers[0], device_id_type=MESH)
                copies.append((send, recv))
        for send, _ in copies:
            send.start()
        for send, recv in copies:
            send.wait_send()
            recv.wait_recv()

    nsem = 3 * nch
    return _call(body, name=name, in_specs=[pl.BlockSpec(memory_space=pl.ANY)],
                 out_specs=pl.BlockSpec(memory_space=pl.ANY), out_shape=_sds(land.shape, land.dtype),
                 input_output_aliases={0: 0},
                 scratch_shapes=[pltpu.SemaphoreType.DMA((nsem,)), pltpu.SemaphoreType.DMA((nsem,))])(land)


def chips_start(p, name):
    _, r, c = p.shape

    def body(p_ref, land_ref, send_sems, recv_sems, p_thru, land_thru, token):
        x, y, cc = _my_place()
        for j, (px, py, pc) in enumerate(_peers(x, y, cc)[1:]):
            pltpu.make_async_remote_copy(
                src_ref=p_ref.at[2 * px + py], dst_ref=land_ref.at[j], send_sem=send_sems.at[j],
                recv_sem=recv_sems.at[j], device_id=(px, py, pc), device_id_type=MESH).start()
        token[...] = jnp.zeros_like(token)

    return _call(
        body, name=name,
        out_shape=(pltpu.SemaphoreType.DMA((3,)), pltpu.SemaphoreType.DMA((3,)), pltpu.HBM(p.shape, p.dtype),
                   pltpu.HBM((3, r, c), p.dtype), _sds((8, 128))),
        in_specs=(HBM_SPEC, HBM_SPEC),
        out_specs=(SEM_SPEC, SEM_SPEC, HBM_SPEC, HBM_SPEC, pl.BlockSpec(memory_space=pltpu.VMEM)),
        input_output_aliases={0: 2, 1: 3},
        compiler_params=pltpu.CompilerParams(has_side_effects=DATAFLOW),
    )(_hbm(p), _hbm(lax.empty((3, r, c), p.dtype)))


def chips_wait(send_sems, recv_sems, p_thru, land_thru, after, name):
    def body(p_ref, land_ref, send_sems, recv_sems, after_ref, p_out, land_out):
        x, y, cc = _my_place()
        for j, (px, py, pc) in enumerate(_peers(x, y, cc)[1:]):
            cp = pltpu.make_async_remote_copy(
                src_ref=p_ref.at[2 * px + py], dst_ref=land_ref.at[j], send_sem=send_sems.at[j],
                recv_sem=recv_sems.at[j], device_id=(px, py, pc), device_id_type=MESH)
            cp.wait_send()
            cp.wait_recv()

    return _call(
        body, name=name,
        out_shape=(pltpu.HBM(p_thru.shape, p_thru.dtype), pltpu.HBM(land_thru.shape, land_thru.dtype)),
        in_specs=(HBM_SPEC, HBM_SPEC, SEM_SPEC, SEM_SPEC, pl.BlockSpec(memory_space=pl.ANY)),
        out_specs=(HBM_SPEC, HBM_SPEC), input_output_aliases={0: 0, 1: 1},
        compiler_params=pltpu.CompilerParams(has_side_effects=DATAFLOW),
    )(p_thru, land_thru, send_sems, recv_sems, after)


def sibling_start(gw, name):
    _, r, c = gw.shape
    nch, rows = _stream_rows(r)
    nsem = 4 * nch

    def body(g_ref, land_ref, send_sems, recv_sems, g_thru, land_thru, token):
        x, y, cc = _my_place()
        for q in range(4):
            for s in range(nch):
                win = pl.ds(s * rows, rows)
                pltpu.make_async_remote_copy(
                    src_ref=g_ref.at[q, 1 - cc, win], dst_ref=land_ref.at[q, win], send_sem=send_sems.at[q * nch + s],
                    recv_sem=recv_sems.at[q * nch + s], device_id=(x, y, 1 - cc), device_id_type=MESH).start()
        token[...] = jnp.zeros_like(token)

    return _call(
        body, name=name,
        out_shape=(pltpu.SemaphoreType.DMA((nsem,)), pltpu.SemaphoreType.DMA((nsem,)),
                   pltpu.HBM((4, 2, r, c), gw.dtype), pltpu.HBM((4, r, c), gw.dtype), _sds((8, 128))),
        in_specs=(HBM_SPEC, HBM_SPEC),
        out_specs=(SEM_SPEC, SEM_SPEC, HBM_SPEC, HBM_SPEC, pl.BlockSpec(memory_space=pltpu.VMEM)),
        input_output_aliases={0: 2, 1: 3},
        compiler_params=pltpu.CompilerParams(has_side_effects=DATAFLOW),
    )(_hbm(gw.reshape(4, 2, r, c)), _hbm(lax.empty((4, r, c), gw.dtype)))


def sibling_wait(send_sems, recv_sems, g_thru, land_thru, after, name):
    _, _, r, c = g_thru.shape
    nch, rows = _stream_rows(r)

    def body(g_ref, land_ref, send_sems, recv_sems, after_ref, g_out, land_out):
        x, y, cc = _my_place()
        for q in range(4):
            for s in range(nch):
                win = pl.ds(s * rows, rows)
                cp = pltpu.make_async_remote_copy(
                    src_ref=g_ref.at[q, 1 - cc, win], dst_ref=land_ref.at[q, win], send_sem=send_sems.at[q * nch + s],
                    recv_sem=recv_sems.at[q * nch + s], device_id=(x, y, 1 - cc), device_id_type=MESH)
                cp.wait_send()
                cp.wait_recv()

    return _call(
        body, name=name,
        out_shape=(pltpu.HBM(g_thru.shape, g_thru.dtype), pltpu.HBM(land_thru.shape, land_thru.dtype)),
        in_specs=(HBM_SPEC, HBM_SPEC, SEM_SPEC, SEM_SPEC, pl.BlockSpec(memory_space=pl.ANY)),
        out_specs=(HBM_SPEC, HBM_SPEC), input_output_aliases={0: 0, 1: 1},
        compiler_params=pltpu.CompilerParams(has_side_effects=DATAFLOW),
    )(g_thru, land_thru, send_sems, recv_sems, after)


def reduce_scatter_begin(gw, name, tag):
    _, r, c = gw.shape
    theirs = exchange_sibling(gw, name + "_sibling")
    chip_sum = add_own_half(gw.reshape(4, 2, r, c), theirs, name + "_add2")
    return chips_start(chip_sum, name + "_start" + tag)


def reduce_scatter_end(started, after, name, tag):
    send_sems, recv_sems, p_thru, land_thru, _ = started
    parts, land = chips_wait(send_sems, recv_sems, p_thru, land_thru, after, name + "_wait" + tag)
    mine = lax.dynamic_index_in_dim(parts, 2 * lax.axis_index("x") + lax.axis_index("y"), axis=0, keepdims=False)
    return add_own_lead(mine, land, name + "_add4")


def _pack(arrs):
    flat = jnp.concatenate([a.reshape(-1).astype(F32) for a in arrs])
    n = flat.shape[0]
    pad = (-n) % PACK_ELEMS
    return jnp.pad(flat, (0, pad)).reshape(-1, 128)


def _unpack(packed, shapes):
    flat = packed.reshape(-1)
    out, off = [], 0
    for s in shapes:
        n = int(np.prod(s))
        out.append(lax.optimization_barrier(flat[off:off + n]).reshape(s))
        off += n
    return out


def _t5_bucket(rel):
    nb = N_BUCKETS // 2
    max_exact = nb // 2
    ret = jnp.where(rel > 0, nb, 0)
    n = jnp.abs(rel)
    nf = jnp.maximum(n, 1).astype(jnp.float32)
    large = max_exact + (jnp.log(nf / max_exact) / math.log(MAX_DISTANCE / max_exact)
                         * (nb - max_exact)).astype(jnp.int32)
    large = jnp.minimum(large, nb - 1)
    return ret + jnp.where(n < max_exact, n, large)


SMALL_SHARDED = ("gla_w2_f", "gla_w2_b", "conv_w", "lru_ba", "lru_bx", "lru_lambda")
SMALL_REPL = ("rel_bias", "attn_sink", "gla_b2_f", "gla_b2_b", "gla_norm", "conv_b", "lru_wa", "lru_wx",
              "norm_mix_pre", "norm_mix_post", "norm_mem", "norm_x_pre", "norm_x_post", "norm_ff_pre", "norm_ff_post")
BIG = ("w_in", "w_out", "xq", "xk", "xv", "xo", "w_up", "w_down")
WEIGHTS = ['rel_bias', 'w_in', 'w_out', 'attn_sink', 'gla_w2_f', 'gla_b2_f', 'gla_w2_b', 'gla_b2_b', 'gla_norm',
           'conv_w', 'conv_b', 'lru_wa', 'lru_ba', 'lru_wx', 'lru_bx', 'lru_lambda', 'xq', 'xk', 'xv', 'xo', 'w_up',
           'w_down', 'norm_mix_pre', 'norm_mix_post', 'norm_mem', 'norm_x_pre', 'norm_x_post', 'norm_ff_pre',
           'norm_ff_post']


def _step(x, mem, loss_target, w, m, v):
    depth = w["w_in"].shape[0]
    t, d = x.shape[1], x.shape[2]
    ml = mem.shape[1]
    rx = d // N_DEV
    rf = w["w_up"].shape[2]
    r_out = D_MIX // N_DEV
    x = x.reshape(t, d)
    mem = mem.reshape(ml, d)
    loss_target = loss_target.reshape(t, d)
    my_idx = 4 * lax.axis_index("x") + 2 * lax.axis_index("y") + lax.axis_index("c")

    off_in = 0
    off_up, off_down, off_out = 0, rf, 2 * rf
    off_xq = off_out + r_out
    off_xk, off_xv, off_xo = off_xq + rx, off_xq + 2 * rx, off_xq + 3 * rx
    r_rest = off_xo + rx

    sh_shapes = [w[n].shape for n in SMALL_SHARDED]
    gathered = all_gather(_pack([w[n] for n in SMALL_SHARDED]), "ag_small")
    per_dev = [_unpack(gathered[j], sh_shapes) for j in range(N_DEV)]
    full = {n: jnp.concatenate([per_dev[j][i] for j in range(N_DEV)], axis=-1) for i, n in enumerate(SMALL_SHARDED)}
    for n in SMALL_REPL:
        full[n] = w[n]

    ag_started = []
    for l in range(depth):
        blk_in = jnp.pad(w["w_in"][l].T, ((0, W_IN_ROWS - W_IN_SHARD), (0, 0))).astype(BF16)
        blk_rest = jnp.concatenate([w["w_up"][l].T, w["w_down"][l], w["w_out"][l], w["xq"][l], w["xk"][l],
                                    w["xv"][l], w["xo"][l]], axis=0).astype(BF16)
        blk_in, _ = lax.optimization_barrier((blk_in, gathered if l == 0 else ag_started[-1][1][4]))
        start_in = gather_start(blk_in, "ag_start_in%d" % l)
        blk_rest, _ = lax.optimization_barrier((blk_rest, start_in[4]))
        ag_started.append((start_in, gather_start(blk_rest, "ag_start_rest%d" % l)))
    x = x + sum(st[4][0, 0] for pair in ag_started for st in pair)
    gws = [None] * depth

    def gather_finish(started, after, name):
        send_sems, recv_sems, blk_thru, land_thru, _ = started
        _, land = gather_wait(send_sems, recv_sems, blk_thru, land_thru, after, name)
        return gather_pass(land, "ag_pass")

    qi = jnp.arange(BLOCK)[:, None]
    kj = jnp.arange(3 * BLOCK)[None, :]
    onehot_t = (jnp.arange(N_BUCKETS)[:, None] == _t5_bucket(kj - BLOCK - qi).reshape(1, -1)).astype(F32)
    bias = mm_plain(full["rel_bias"].T, onehot_t, "rel_bias_lookup", hi=True, tn=3 * BLOCK * 16)
    bias = bias.reshape(A_HEADS, BLOCK, 3 * BLOCK)
    bias_t = jnp.transpose(bias, (0, 2, 1))

    def sink_rows(sink):
        s = jnp.broadcast_to(sink.reshape(A_KV_HEADS, A_GROUP, 1), (A_KV_HEADS, A_GROUP, 128))
        return jnp.pad(s, ((0, 0), (0, 8 - A_GROUP), (0, 0)))

    bounds = np.concatenate([[0], np.cumsum(SPLIT_SIZES)])

    def split_proj(pp):
        outs = []
        for lo, hi in zip(bounds[:-1], bounds[1:]):
            segs = []
            for j in range(N_DEV):
                a, b = max(lo, j * W_IN_SHARD), min(hi, (j + 1) * W_IN_SHARD)
                if a < b:
                    base = j * W_IN_ROWS - j * W_IN_SHARD
                    segs.append(pp[:, base + a:base + b])
            outs.append(segs[0] if len(segs) == 1 else jnp.concatenate(segs, axis=1))
        return outs

    def join_dproj(pieces):
        zero_cols = jnp.zeros((t, W_IN_ROWS - W_IN_SHARD), F32)
        segs = []
        for j in range(N_DEV):
            for p, lo, hi in zip(pieces, bounds[:-1], bounds[1:]):
                a, b = max(lo, j * W_IN_SHARD), min(hi, (j + 1) * W_IN_SHARD)
                if a < b:
                    segs.append(p[:, a - lo:b - lo])
            segs.append(zero_cols)
        return jnp.concatenate(segs, axis=1).astype(BF16)

    def lead(a):
        return a.reshape(a.shape[0], C_WIDTH // 128, 128)

    saved = []
    h = rms_fwd(x, full["norm_mix_pre"][0], "rms_first")
    for l in range(depth):
        gw_in = gather_finish(ag_started[l][0], x, "ag_wait_in%d" % l)
        sv = {"x": x, "h_in": h}
        proj_pad = mm_wn(h, gw_in, off_in, W_IN_ROWS, "mm_w_in")
        aq, ak, av, bq, bk, bv, bg, zf, zb, cx, cy = split_proj(proj_pad)
        sv.update(aq=aq, ak=ak, av=av, bq=bq, bk=bk, bv=bv, bg=bg, zf=zf, zb=zb, cx=cx, cy=cy)
        sink_b = sink_rows(full["attn_sink"][l])
        oa = attn_fwd(aq, ak, av, bias, sink_b, "attn_fwd")
        la_f, la_b = gla_gates_fwd(zf, zb, full["gla_w2_f"][l], full["gla_b2_f"][l], full["gla_w2_b"][l],
                                   full["gla_b2_b"][l], "gla_gates_fwd")
        bk_t = bk.T
        gla = {}
        for nm, la, bdir in (("f", la_f, False), ("b", la_b, True)):
            la_t = la.T
            u, dec = gla_outer(bk_t, la_t, bv, "gla_outer_k_" + nm, bdir, "khat")
            sp = scan_lead(dec, u, "gla_state_scan_" + nm, reverse=bdir, inclusive=False)
            o_dir = gla_inner_fwd(bq, bk, bv, la, sp, "gla_inner_fwd_" + nm, bdir)
            gla[nm] = dict(la=la, la_t=la_t, dec=dec, sp=sp, o=o_dir)
        ob = gla_out_fwd(gla["f"]["o"], gla["b"]["o"], bg, full["gla_norm"][l], "gla_out_fwd")
        sv["gla"] = gla
        xm2, xm1, xp1 = _shift(cx, -2), _shift(cx, -1), _shift(cx, 1)
        xc, a0, u0, a1, u1 = lru_gates_fwd(cx, xm2, xm1, xp1, full["conv_w"][l], full["conv_b"][l], full["lru_wa"][l],
                                           full["lru_ba"][l], full["lru_wx"][l], full["lru_bx"][l],
                                           full["lru_lambda"][l], "lru_gates_fwd")
        h0 = scan_lead(lead(a0), lead(u0), "lru_scan_fwd", reverse=False, inclusive=True).reshape(t, C_WIDTH)
        h1 = scan_lead(lead(a1), lead(u1), "lru_scan_rev", reverse=True, inclusive=True).reshape(t, C_WIDTH)
        oc = lru_out_fwd(h0, h1, cy, "lru_out_fwd")
        sv.update(xm2=xm2, xm1=xm1, xp1=xp1, xc=xc, a0=a0, a1=a1, h0=h0, h1=h1, oa=oa)
        cat = jnp.concatenate([oa, ob, oc], axis=1).astype(BF16)
        gw = gather_finish(ag_started[l][1], cat, "ag_wait_rest%d" % l)
        gws[l] = (gw_in, gw)
        mixed = mm_wk(cat, gw, off_out, r_out, "mm_w_out")
        x1, h2 = resid_rms(x, mixed, full["norm_mix_post"][l], full["norm_x_pre"][l], "resid_rms")
        sv.update(cat=cat, mixed=mixed, x1=x1, h2=h2)
        memn = rms_fwd(mem, full["norm_mem"][l], "rms_mem")
        q = mm_wk(h2, gw, off_xq, rx, "mm_xq")
        k = mm_wk(memn, gw, off_xk, rx, "mm_xkv")
        vv = mm_wk(memn, gw, off_xv, rx, "mm_xkv")
        ox = xattn_fwd(q, k, vv, "xattn_fwd")
        ca = mm_wk(ox, gw, off_xo, rx, "mm_xo")
        x2, h3 = resid_rms(x1, ca, full["norm_x_post"][l], full["norm_ff_pre"][l], "resid_rms")
        sv.update(memn=memn, q=q, k=k, v=vv, ox=ox, ca=ca, x2=x2, h3=h3)
        up, act = mm_wn(h3, gw, off_up, rf, "mm_w_up", with_relu2=True)
        ff = mm_wk(act, gw, off_down, rf, "mm_w_down", jb=max(1, min(N_DEV, 2048 // rf)))
        if l + 1 < depth:
            x, h = resid_rms(x2, ff, full["norm_ff_post"][l], full["norm_mix_pre"][l + 1], "resid_rms")
        else:
            x = resid_rms(x2, ff, full["norm_ff_post"][l], None, "resid_rms_last")
        sv.update(up=up, act=act, ff=ff)
        saved.append(sv)

    dx, loss_local = loss_and_grad(x, loss_target, "loss")
    loss = lax.psum(loss_local, AXES)

    grads = {n: [None] * depth for n in WEIGHTS if n != "rel_bias"}
    dbias_total = None
    big_grads = [None] * depth
    rs_started = [None] * depth
    bf = lambda a: a.astype(BF16)
    for l in reversed(range(depth)):
        gw_in, gw = gws[l]
        sv = saved[l]
        dff, grads["norm_ff_post"][l] = rms_bwd(sv["ff"], full["norm_ff_post"][l], dx, "rms_bwd")
        dup = mm_wn(dff, gw, off_down, rf, "mm_w_down_dx", relu_grad_of=sv["up"], out_dtype=BF16)
        gpack = mm_dw_into(sv["act"], dff, lax.empty((N_DEV, r_rest, d), BF16), off_down, rf, "mm_dw_down")
        gpack = mm_dw_into(dup, sv["h3"], gpack, off_up, rf, "mm_dw_up")
        dh3 = mm_wk(dup, gw, off_up, rf, "mm_w_up_dx", jb=max(1, min(N_DEV, 2048 // rf)))
        dx2, grads["norm_ff_pre"][l] = rms_bwd(sv["x2"], full["norm_ff_pre"][l], dh3, "rms_bwd_add", add=dx)
        dca, grads["norm_x_post"][l] = rms_bwd(sv["ca"], full["norm_x_post"][l], dx2, "rms_bwd")
        dox = mm_wn(dca, gw, off_xo, rx, "mm_x_dx")
        gpack = mm_dw_into(sv["ox"], dca, gpack, off_xo, rx, "mm_dw_xo")
        dq, dk, dv = xattn_bwd(sv["q"], sv["k"], sv["v"], sv["ox"], dox, "xattn_bwd")
        gpack = mm_dw_into(sv["h2"], dq, gpack, off_xq, rx, "mm_dw_xq")
        gpack = mm_dw_into(sv["memn"], dk, gpack, off_xk, rx, "mm_dw_xk")
        gpack = mm_dw_into(sv["memn"], dv, gpack, off_xv, rx, "mm_dw_xv")
        dh2 = mm_wn(dq, gw, off_xq, rx, "mm_x_dx")
        dmem_k = mm_wn(dk, gw, off_xk, rx, "mm_x_dx_mem")
        dmem_v = mm_wn(dv, gw, off_xv, rx, "mm_x_dx_mem")
        _, grads["norm_mem"][l] = rms_bwd(mem, full["norm_mem"][l], dmem_k, "rms_bwd_mem", dy2=dmem_v)
        dx1, grads["norm_x_pre"][l] = rms_bwd(sv["x1"], full["norm_x_pre"][l], dh2, "rms_bwd_add", add=dx2)
        dmixed, grads["norm_mix_post"][l] = rms_bwd(sv["mixed"], full["norm_mix_post"][l], dx1, "rms_bwd")
        dcat = mm_wn(dmixed, gw, off_out, r_out, "mm_w_out_dx")
        gpack = mm_dw_into(sv["cat"], dmixed, gpack, off_out, r_out, "mm_dw_out")
        if l == 0:
            rs_started[l] = [reduce_scatter_begin(gpack, "rs_rest", str(l)), None]
            dcat = dcat + rs_started[l][0][4][0, 0]
        else:
            sib = sibling_start(gpack, "rs_rest_sib_start%d" % l)
            dcat = dcat + sib[4][0, 0]
        doa, dob, doc = dcat[:, :A_Q], dcat[:, A_Q:A_Q + B_V], dcat[:, A_Q + B_V:]
        daq, dak, dav, dbias, dsink = attn_bwd(sv["aq"], sv["ak"], sv["av"], bias, bias_t,
                                               sink_rows(full["attn_sink"][l]), doa, sv["oa"], "attn_bwd")
        grads["attn_sink"][l] = dsink[:, :A_GROUP, 0].reshape(A_HEADS)
        dbias_total = dbias if dbias_total is None else add_n([dbias_total, dbias], F32, "add_dbias")
        gf, gb = sv["gla"]["f"], sv["gla"]["b"]
        do_gla, dbg, dgn = gla_out_bwd(gf["o"], gb["o"], sv["bg"], full["gla_norm"][l], dob, "gla_out_bwd")
        grads["gla_norm"][l] = dgn.reshape(B_V)
        bq_t = sv["bq"].T
        acc = None
        dlas = {}
        for nm, gd, bdir in (("f", gf, False), ("b", gb, True)):
            wq = gla_outer(bq_t, gd["la_t"], do_gla, "gla_outer_q_" + nm, bdir, "qtil")
            gs = scan_lead(gd["dec"], wq, "gla_adj_scan_" + nm, reverse=not bdir, inclusive=False)
            dbq, dbk, dbv, dlas[nm] = gla_inner_bwd(sv["bq"], sv["bk"], sv["bv"], gd["la"], do_gla, gd["sp"], gs,
                                                    gd["dec"], "gla_inner_bwd_" + nm, bdir, add=acc)
            acc = (dbq, dbk, dbv)
        dzf, dzb, dpre_f, dpre_b, db2f, db2b = gla_gates_bwd(
            sv["zf"], sv["zb"], full["gla_w2_f"][l], full["gla_b2_f"][l], full["gla_w2_b"][l], full["gla_b2_b"][l],
            dlas["f"], dlas["b"], "gla_gates_bwd")
        grads["gla_b2_f"][l] = db2f.reshape(B_QK)
        grads["gla_b2_b"][l] = db2b.reshape(B_QK)
        grads["gla_w2_f"][l] = mm_plain(sv["zf"].T, dpre_f, "mm_dw_gate", hi=True)
        grads["gla_w2_b"][l] = mm_plain(sv["zb"].T, dpre_b, "mm_dw_gate", hi=True)
        dh, dcy = lru_out_bwd(sv["h0"], sv["h1"], sv["cy"], doc, "lru_out_bwd")
        g0 = scan_lead(lead(_shift(sv["a0"], 1)), lead(dh), "lru_scan_rev", reverse=True,
                       inclusive=True).reshape(t, C_WIDTH)
        g1 = scan_lead(lead(_shift(sv["a1"], -1)), lead(dh), "lru_scan_fwd", reverse=False,
                       inclusive=True).reshape(t, C_WIDTH)
        dxc, dzr0, dzi0, dzr1, dzi1, dlam, dba, dbx = lru_gates_bwd(
            sv["xc"], g0, _shift(sv["h0"], -1), g1, _shift(sv["h1"], 1), full["lru_wa"][l], full["lru_ba"][l],
            full["lru_wx"][l], full["lru_bx"][l], full["lru_lambda"][l], "lru_gates_bwd")
        xc_t = bf(sv["xc"].T)
        grads["lru_wa"][l] = jnp.stack([blockdiag_dw(xc_t, dzr0, "lru_dw"), blockdiag_dw(xc_t, dzr1, "lru_dw")])
        grads["lru_wx"][l] = jnp.stack([blockdiag_dw(xc_t, dzi0, "lru_dw"), blockdiag_dw(xc_t, dzi1, "lru_dw")])
        grads["lru_lambda"][l], grads["lru_ba"][l], grads["lru_bx"][l] = dlam, dba, dbx
        dcx, dcw, dcb = conv_bwd(dxc, _shift(dxc, 2), _shift(dxc, 1), _shift(dxc, -1), sv["cx"], sv["xm2"],
                                 sv["xm1"], sv["xp1"], full["conv_w"][l], "conv_bwd")
        grads["conv_w"][l] = dcw
        grads["conv_b"][l] = dcb.reshape(C_WIDTH)
        dproj_pad = join_dproj([daq, dak, dav, dbq, dbk, dbv, dbg, dzf, dzb, dcx, dcy])
        g_in_t = mm_plain(dproj_pad, sv["h_in"], "mm_dw_in", ta=True, out_dtype=BF16)
        dh1 = mm_wk(dproj_pad, gw_in, off_in, W_IN_ROWS, "mm_w_in_dx", jb=2)
        dx, grads["norm_mix_pre"][l] = rms_bwd(sv["x"], full["norm_mix_pre"][l], dh1, "rms_bwd_add", add=dx1)
        if l > 0:
            g5, theirs = sibling_wait(sib[0], sib[1], sib[2], sib[3], dx, "rs_rest_sib_wait%d" % l)
            chip_sum = add_own_half(g5, theirs, "rs_rest_add2")
            rs_started[l] = [chips_start(chip_sum, "rs_rest_start%d" % l), None]
            dx = dx + rs_started[l][0][4][0, 0]
        rs_started[l][1] = reduce_scatter_begin(g_in_t.reshape(N_DEV, W_IN_ROWS, d), "rs_in", str(l))
        dx = dx + rs_started[l][1][4][0, 0]

    grad_rel = mm_plain(dbias_total.reshape(A_HEADS, -1), onehot_t, "rel_bias_grad", tb=True, hi=True).T

    small_names = [n for n in WEIGHTS if n not in BIG]
    small_g = {"rel_bias": grad_rel}
    for n in small_names:
        if n != "rel_bias":
            small_g[n] = jnp.stack([g.reshape(full[n].shape[1:]) for g in grads[n]])
    shapes = [small_g[n].shape for n in small_names]
    small_started = gather_start(_pack([small_g[n] for n in small_names]), "ag_start_small_grads")
    dx = dx + small_started[4][0, 0]
    for l in range(depth):
        big_grads[l] = (reduce_scatter_end(rs_started[l][1], dx, "rs_in", str(l)),
                        reduce_scatter_end(rs_started[l][0], dx, "rs_rest", str(l)))

    grad_out, delta, new_m, new_v = {}, {}, {}, {}

    def rows(l, off, r):
        return big_grads[l][1][off:off + r]

    big_g = {
        "w_in": jnp.stack([big_grads[l][0][:W_IN_SHARD].T for l in range(depth)]),
        "w_out": jnp.stack([rows(l, off_out, r_out) for l in range(depth)]),
        "xq": jnp.stack([rows(l, off_xq, rx) for l in range(depth)]),
        "xk": jnp.stack([rows(l, off_xk, rx) for l in range(depth)]),
        "xv": jnp.stack([rows(l, off_xv, rx) for l in range(depth)]),
        "xo": jnp.stack([rows(l, off_xo, rx) for l in range(depth)]),
        "w_up": jnp.stack([rows(l, off_up, rf).T for l in range(depth)]),
        "w_down": jnp.stack([rows(l, off_down, rf) for l in range(depth)]),
    }
    for n in BIG:
        grad_out[n] = big_g[n]
        delta[n], new_m[n], new_v[n] = adamw(big_g[n], w[n], m[n], v[n], "adamw_" + n)

    packed = gather_finish(small_started, delta["w_down"], "ag_wait_small_grads")
    summed = sum_lead(packed, tuple(range(N_DEV)), F32, "add8_small")
    small_g = dict(zip(small_names, _unpack(summed, shapes)))
    for n in SMALL_SHARDED:
        wdt = w[n].shape[-1]
        small_g[n] = lax.dynamic_slice_in_dim(small_g[n], my_idx * wdt, wdt, axis=small_g[n].ndim - 1)

    direct = ("lru_wa", "lru_wx")
    packed_names = [n for n in small_names if n not in direct]
    sshapes = [w[n].shape for n in packed_names]
    ds, ms, vs = adamw(_pack([small_g[n] for n in packed_names]), _pack([w[n] for n in packed_names]),
                       _pack([m[n] for n in packed_names]), _pack([v[n] for n in packed_names]), "adamw_small")
    for n, d_, m_, v_ in zip(packed_names, _unpack(ds, sshapes), _unpack(ms, sshapes), _unpack(vs, sshapes)):
        grad_out[n], delta[n], new_m[n], new_v[n] = small_g[n], d_, m_, v_
    for n in direct:
        grad_out[n] = small_g[n]
        delta[n], new_m[n], new_v[n] = adamw(small_g[n], w[n], m[n], v[n], "adamw_lru")

    return (loss, dx.reshape(1, t, d), *[grad_out[n] for n in WEIGHTS], *[delta[n] for n in WEIGHTS],
            *[new_m[n] for n in WEIGHTS], *[new_v[n] for n in WEIGHTS])


def kernel(x, mem, rel_bias, w_in, w_out, attn_sink, gla_w2_f, gla_b2_f, gla_w2_b, gla_b2_b, gla_norm, conv_w, conv_b, lru_wa, lru_ba, lru_wx, lru_bx, lru_lambda, xq, xk, xv, xo, w_up, w_down, norm_mix_pre, norm_mix_post, norm_mem, norm_x_pre, norm_x_post, norm_ff_pre, norm_ff_post, loss_target, m_rel_bias, m_w_in, m_w_out, m_attn_sink, m_gla_w2_f, m_gla_b2_f, m_gla_w2_b, m_gla_b2_b, m_gla_norm, m_conv_w, m_conv_b, m_lru_wa, m_lru_ba, m_lru_wx, m_lru_bx, m_lru_lambda, m_xq, m_xk, m_xv, m_xo, m_w_up, m_w_down, m_norm_mix_pre, m_norm_mix_post, m_norm_mem, m_norm_x_pre, m_norm_x_post, m_norm_ff_pre, m_norm_ff_post, v_rel_bias, v_w_in, v_w_out, v_attn_sink, v_gla_w2_f, v_gla_b2_f, v_gla_w2_b, v_gla_b2_b, v_gla_norm, v_conv_w, v_conv_b, v_lru_wa, v_lru_ba, v_lru_wx, v_lru_bx, v_lru_lambda, v_xq, v_xk, v_xv, v_xo, v_w_up, v_w_down, v_norm_mix_pre, v_norm_mix_post, v_norm_mem, v_norm_x_pre, v_norm_x_post, v_norm_ff_pre, v_norm_ff_post):
    given = dict(locals())
    w = {n: given[n] for n in WEIGHTS}
    m = {n: given["m_" + n] for n in WEIGHTS}
    v = {n: given["v_" + n] for n in WEIGHTS}
    return _step(x, mem, loss_target, w, m, v)
```

```python
import math

import jax
import jax.numpy as jnp
import numpy as np
from jax import lax
from jax.experimental import pallas as pl
from jax.experimental.pallas import tpu as pltpu

F32 = jnp.float32
BF16 = jnp.bfloat16
HI = lax.Precision.HIGHEST
NN = (((1,), (0,)), ((), ()))
NT = (((1,), (1,)), ((), ()))
MESH = pl.DeviceIdType.MESH
AXES = ("x", "y", "c")
N_DEV = 8

A_HEAD_DIM = 128
A_HEADS = 8
A_KV_HEADS = 2
A_GROUP = 4
WINDOW = 128
BLOCK = 128
N_BUCKETS = 32
MAX_DISTANCE = 128
B_HEADS = 4
B_KEY_DIM = 64
B_VAL_DIM = 128
GATE_RANK = 16
GATE_TAU = 16.0
GLA_CHUNK = 16
C_WIDTH = 512
C_BLOCKS = 4
C_BLOCK_DIM = 128
LRU_C = 8.0
X_HEADS = 4
EPS = 1e-6
NEG_INF = -1e30
A_Q = A_HEADS * A_HEAD_DIM
A_KV = A_KV_HEADS * A_HEAD_DIM
B_QK = B_HEADS * B_KEY_DIM
B_V = B_HEADS * B_VAL_DIM
SPLIT_SIZES = (A_Q, A_KV, A_KV, B_QK, B_QK, B_V, B_V, GATE_RANK, GATE_RANK, C_WIDTH, C_WIDTH)
D_IN = sum(SPLIT_SIZES)
D_MIX = A_Q + B_V + C_WIDTH
W_IN_SHARD = D_IN // N_DEV
W_IN_ROWS = 768
GLA_TILE = 128
CHUNKS_PER_TILE = GLA_TILE // GLA_CHUNK
EXP_CLAMP = 80.0

ADAM_LR = 0.001
ADAM_B1 = 0.9
ADAM_B2 = 0.999
ADAM_EPS = 1e-08
ADAM_WD = 0.01
ADAM_STEP = 10

VMEM_LIMIT_BYTES = 52 * 1024 * 1024
MM_TILE = 1024
SIBLING_STREAMS = 16
PACK_ELEMS = 128 * 2048


def _call(body, **kw):
    return pl.pallas_call(body, **kw)


def _cparams():
    return pltpu.CompilerParams(vmem_limit_bytes=VMEM_LIMIT_BYTES)


def _dot(a, b, dims=NN, hi=False):
    if hi:
        return lax.dot_general(a, b, dims, precision=HI, preferred_element_type=F32)
    return lax.dot_general(a.astype(BF16), b.astype(BF16), dims, preferred_element_type=F32)


def _sds(shape, dtype=F32):
    return jax.ShapeDtypeStruct(tuple(shape), dtype)


def _row_tile(rows, cols, target_elems=1 << 18):
    want = max(8, target_elems // max(cols, 1))
    if rows <= want:
        return rows
    t = (want // 8) * 8
    while t >= 8:
        if rows % t == 0:
            return t
        t -= 8
    return rows


def _expm1(x):
    poly = x * (1.0 + x * (1.0 / 2 + x * (1.0 / 6 + x * (1.0 / 24 + x * (1.0 / 120 + x * (
        1.0 / 720 + x * (1.0 / 5040 + x * (1.0 / 40320))))))))
    return jnp.where(jnp.abs(x) < 0.3, poly, jnp.exp(x) - 1.0)


def _log1p(e):
    w = 1.0 + e
    return jnp.where(w == 1.0, e, jnp.log(w) * e / (w - 1.0))


def _softplus(x):
    return jnp.maximum(x, 0.0) + _log1p(jnp.exp(-jnp.abs(x)))


def _sigmoid(x):
    return jax.nn.sigmoid(x)


GELU_K = math.sqrt(2.0 / math.pi)


def _gelu(y):
    t = jnp.tanh(GELU_K * (y + 0.044715 * y * y * y))
    return 0.5 * y * (1.0 + t)


def _gelu_grad(y):
    t = jnp.tanh(GELU_K * (y + 0.044715 * y * y * y))
    return 0.5 * (1.0 + t) + 0.5 * y * (1.0 - t * t) * GELU_K * (1.0 + 3 * 0.044715 * y * y)


def rms_fwd(x, g, name):
    m, d = x.shape
    tm = _row_tile(m, d)

    def body(x_ref, g_ref, o_ref):
        xv = x_ref[...]
        r = lax.rsqrt(jnp.mean(xv * xv, axis=1, keepdims=True) + EPS)
        o_ref[...] = (xv * r * g_ref[...]).astype(o_ref.dtype)

    return _call(body, name=name, grid=(m // tm,),
                 in_specs=[pl.BlockSpec((tm, d), lambda i: (i, 0)), pl.BlockSpec((1, d), lambda i: (0, 0))],
                 out_specs=pl.BlockSpec((tm, d), lambda i: (i, 0)),
                 out_shape=_sds((m, d), BF16))(x, g.reshape(1, d))


def resid_rms(xres, mid, g_post, g_pre, name):
    m, d = xres.shape
    tm = _row_tile(m, d)
    with_pre = g_pre is not None

    def body(*refs):
        if with_pre:
            x_ref, m_ref, gp_ref, gn_ref, xo_ref, h_ref = refs
        else:
            x_ref, m_ref, gp_ref, xo_ref = refs
        mv = m_ref[...]
        r = lax.rsqrt(jnp.mean(mv * mv, axis=1, keepdims=True) + EPS)
        xn = x_ref[...] + mv * r * gp_ref[...]
        xo_ref[...] = xn
        if with_pre:
            r2 = lax.rsqrt(jnp.mean(xn * xn, axis=1, keepdims=True) + EPS)
            h_ref[...] = (xn * r2 * gn_ref[...]).astype(h_ref.dtype)

    row = pl.BlockSpec((tm, d), lambda i: (i, 0))
    vec = pl.BlockSpec((1, d), lambda i: (0, 0))
    ins = [xres, mid, g_post.reshape(1, d)] + ([g_pre.reshape(1, d)] if with_pre else [])
    in_specs = [row, row, vec] + ([vec] if with_pre else [])
    if with_pre:
        return _call(body, name=name, grid=(m // tm,), in_specs=in_specs, out_specs=(row, row),
                     out_shape=(_sds((m, d)), _sds((m, d), BF16)))(*ins)
    return _call(body, name=name, grid=(m // tm,), in_specs=in_specs, out_specs=row,
                 out_shape=_sds((m, d)))(*ins)


def rms_bwd(x, g, dy, name, dy2=None, add=None):
    m, d = x.shape
    tm = _row_tile(m, d)
    has2, hasadd = dy2 is not None, add is not None

    def body(*refs):
        it = iter(refs)
        x_ref, g_ref, dy_ref = next(it), next(it), next(it)
        dy2_ref = next(it) if has2 else None
        add_ref = next(it) if hasadd else None
        dx_ref, dg_ref = next(it), next(it)
        xv = x_ref[...]
        dyv = dy_ref[...]
        if has2:
            dyv = dyv + dy2_ref[...]
        r = lax.rsqrt(jnp.mean(xv * xv, axis=1, keepdims=True) + EPS)
        xh = xv * r
        dxh = dyv * g_ref[...]
        dx = r * (dxh - xh * jnp.mean(dxh * xh, axis=1, keepdims=True))
        if hasadd:
            dx = dx + add_ref[...]
        dx_ref[...] = dx
        part = jnp.sum(dyv * xh, axis=0, keepdims=True)

        @pl.when(pl.program_id(0) == 0)
        def _():
            dg_ref[...] = part

        @pl.when(pl.program_id(0) > 0)
        def _():
            dg_ref[...] += part

    row = pl.BlockSpec((tm, d), lambda i: (i, 0))
    vec = pl.BlockSpec((1, d), lambda i: (0, 0))
    ins = [x, g.reshape(1, d), dy] + ([dy2] if has2 else []) + ([add] if hasadd else [])
    in_specs = [row, vec, row] + ([row] if has2 else []) + ([row] if hasadd else [])
    return _call(body, name=name, grid=(m // tm,), in_specs=in_specs, out_specs=(row, vec),
                 out_shape=(_sds((m, d)), _sds((1, d))))(*ins)


def loss_and_grad(y, target, name):
    m, d = y.shape
    tm = _row_tile(m, d)

    def body(y_ref, t_ref, dy_ref, l_ref):
        e = y_ref[...] - t_ref[...]
        dy_ref[...] = e * (1.0 / d)
        s = jnp.sum(jnp.sum(e * e, axis=1, keepdims=True), axis=0, keepdims=True) * (0.5 / d)
        part = jnp.broadcast_to(s, (1, 128))

        @pl.when(pl.program_id(0) == 0)
        def _():
            l_ref[...] = part

        @pl.when(pl.program_id(0) > 0)
        def _():
            l_ref[...] += part

    row = pl.BlockSpec((tm, d), lambda i: (i, 0))
    dy, l = _call(body, name=name, grid=(m // tm,), in_specs=[row, row],
                  out_specs=(row, pl.BlockSpec((1, 128), lambda i: (0, 0))),
                  out_shape=(_sds((m, d)), _sds((1, 128))))(y, target)
    return dy, l[0, 0]


def adamw(g, w, m, v, name):
    shape = w.shape
    cols = shape[-1]
    rows = int(np.prod(shape[:-1]))
    tm = _row_tile(rows, cols)
    c1 = 1.0 - ADAM_B1 ** ADAM_STEP
    c2 = 1.0 - ADAM_B2 ** ADAM_STEP

    def body(g_ref, w_ref, m_ref, v_ref, d_ref, mo_ref, vo_ref):
        gv = g_ref[...]
        mn = ADAM_B1 * m_ref[...] + (1.0 - ADAM_B1) * gv
        vn = ADAM_B2 * v_ref[...] + (1.0 - ADAM_B2) * (gv * gv)
        m_hat = mn / c1
        v_hat = vn / c2
        d_ref[...] = -ADAM_LR * (m_hat / (jnp.sqrt(v_hat) + ADAM_EPS) + ADAM_WD * w_ref[...])
        mo_ref[...] = mn
        vo_ref[...] = vn

    row = pl.BlockSpec((tm, cols), lambda i: (i, 0))
    outs = _call(body, name=name, grid=(rows // tm,), in_specs=[row] * 4, out_specs=(row,) * 3,
                 out_shape=(_sds((rows, cols)),) * 3)(*[a.reshape(rows, cols) for a in (g, w, m, v)])
    return tuple(o.reshape(shape) for o in outs)


def sum_lead(x, order, out_dtype, name):
    n, rows, cols = x.shape
    tm = _row_tile(rows, cols)

    def body(x_ref, o_ref):
        acc = x_ref[order[0]].astype(F32)
        for i in order[1:]:
            acc = acc + x_ref[i].astype(F32)
        o_ref[...] = acc.astype(out_dtype)

    return _call(body, name=name, grid=(rows // tm,), in_specs=[pl.BlockSpec((n, tm, cols), lambda i: (0, i, 0))],
                 out_specs=pl.BlockSpec((tm, cols), lambda i: (i, 0)), out_shape=_sds((rows, cols), out_dtype))(x)


def add_own_lead(own, parts, name):
    n, rows, cols = parts.shape
    tm = _row_tile(rows, cols)

    def body(o_ref, p_ref, out_ref):
        acc = o_ref[...].astype(F32)
        for i in range(n):
            acc = acc + p_ref[i].astype(F32)
        out_ref[...] = acc

    row = pl.BlockSpec((tm, cols), lambda i: (i, 0))
    return _call(body, name=name, grid=(rows // tm,),
                 in_specs=[row, pl.BlockSpec((n, tm, cols), lambda i: (0, i, 0))], out_specs=row,
                 out_shape=_sds((rows, cols)))(own, parts)


def add_own_half(g5, theirs, name):
    _, _, r, c = g5.shape
    tm = _row_tile(r, c, 1 << 20)

    def body(cc_ref, g_ref, t_ref, o_ref):
        o_ref[...] = (g_ref[...].astype(F32) + t_ref[...].astype(F32)).astype(o_ref.dtype)

    grid_spec = pltpu.PrefetchScalarGridSpec(
        num_scalar_prefetch=1, grid=(4, r // tm),
        in_specs=[pl.BlockSpec((None, None, tm, c), lambda q, i, cc_ref: (q, cc_ref[0], i, 0)),
                  pl.BlockSpec((None, tm, c), lambda q, i, cc_ref: (q, i, 0))],
        out_specs=pl.BlockSpec((None, tm, c), lambda q, i, cc_ref: (q, i, 0)))
    return _call(body, name=name, grid_spec=grid_spec, out_shape=_sds((4, r, c), BF16))(
        lax.axis_index("c").astype(jnp.int32).reshape(1), g5, theirs)


def add_n(xs, out_dtype, name):
    shape = xs[0].shape
    cols = shape[-1]
    rows = int(np.prod(shape[:-1]))
    tm = _row_tile(rows, cols)
    n = len(xs)

    def body(*refs):
        acc = refs[0][...].astype(F32)
        for r in refs[1:n]:
            acc = acc + r[...].astype(F32)
        refs[n][...] = acc.astype(out_dtype)

    row = pl.BlockSpec((tm, cols), lambda i: (i, 0))
    out = _call(body, name=name, grid=(rows // tm,), in_specs=[row] * n, out_specs=row,
                out_shape=_sds((rows, cols), out_dtype))(*[a.reshape(rows, cols) for a in xs])
    return out.reshape(shape)


def mm_plain(a, b, name, ta=False, tb=False, out_dtype=F32, hi=False, tm=MM_TILE, tn=MM_TILE):
    k, m = a.shape[::1 if ta else -1]
    n = b.shape[0] if tb else b.shape[1]
    tm, tn = min(tm, m), min(tn, n)
    dims = (((0 if ta else 1,), (1 if tb else 0,)), ((), ()))

    def body(a_ref, b_ref, o_ref):
        o_ref[...] = _dot(a_ref[...], b_ref[...], dims, hi).astype(out_dtype)

    a_spec = pl.BlockSpec((k, tm), lambda j, i: (0, i)) if ta else pl.BlockSpec((tm, k), lambda j, i: (i, 0))
    b_spec = pl.BlockSpec((tn, k), lambda j, i: (j, 0)) if tb else pl.BlockSpec((k, tn), lambda j, i: (0, j))
    return _call(body, name=name, grid=(n // tn, m // tm), in_specs=[a_spec, b_spec],
                 out_specs=pl.BlockSpec((tm, tn), lambda j, i: (i, j)),
                 out_shape=_sds((m, n), out_dtype), compiler_params=_cparams())(a, b)


def mm_dw_into(a, b, buf, off, r, name, tn=MM_TILE):
    k, m = a.shape
    n = b.shape[1]
    tm, tn = min(MM_TILE, r), min(tn, n)
    assert m == N_DEV * r and off % tm == 0 and r % tm == 0
    per = r // tm
    dims = (((0,), (0,)), ((), ()))

    def body(a_ref, b_ref, buf_ref, o_ref):
        o_ref[...] = _dot(a_ref[...], b_ref[...], dims).astype(o_ref.dtype)

    return _call(body, name=name, grid=(n // tn, m // tm),
                 in_specs=[pl.BlockSpec((k, tm), lambda j, i: (0, i)), pl.BlockSpec((k, tn), lambda j, i: (0, j)),
                           pl.BlockSpec(memory_space=pl.ANY)],
                 out_specs=pl.BlockSpec((None, tm, tn), lambda j, i: (i // per, off // tm + i % per, j)),
                 out_shape=_sds(buf.shape, buf.dtype), input_output_aliases={2: 0},
                 compiler_params=_cparams())(a, b, buf)


def mm_wk(a, gw, off, r, name, jb=N_DEV, tm=MM_TILE, tn=MM_TILE):
    m = a.shape[0]
    d = gw.shape[2]
    tm, tn = min(tm, m), min(tn, d)
    nk = N_DEV // jb
    ob = off // r
    assert off % r == 0 and a.shape[1] == N_DEV * r

    def body(a_ref, b_ref, o_ref, *acc):
        av = a_ref[...].astype(BF16)
        p = _dot(av[:, 0:r], b_ref[0])
        for q in range(1, jb):
            p = p + _dot(av[:, q * r:(q + 1) * r], b_ref[q])
        if nk == 1:
            o_ref[...] = p
        else:
            kk = pl.program_id(2)

            @pl.when(kk == 0)
            def _():
                acc[0][...] = p

            @pl.when(kk > 0)
            def _():
                acc[0][...] += p

            @pl.when(kk == nk - 1)
            def _():
                o_ref[...] = acc[0][...]

    return _call(body, name=name, grid=(m // tm, d // tn, nk),
                 in_specs=[pl.BlockSpec((tm, jb * r), lambda i, j, k: (i, k)),
                           pl.BlockSpec((jb, r, tn), lambda i, j, k: (k, ob, j))],
                 out_specs=pl.BlockSpec((tm, tn), lambda i, j, k: (i, j)),
                 out_shape=_sds((m, d)),
                 scratch_shapes=([pltpu.VMEM((tm, tn), F32)] if nk > 1 else []),
                 compiler_params=_cparams())(a, gw)


def mm_wn(a, gw, off, r, name, relu_grad_of=None, out_dtype=F32, with_relu2=False, tm=MM_TILE):
    m, d = a.shape
    tm = min(tm, m)
    ob = off // r
    assert off % r == 0 and gw.shape[2] == d
    epi = relu_grad_of is not None

    def body(*refs):
        it = iter(refs)
        a_ref, b_ref = next(it), next(it)
        e_ref = next(it) if epi else None
        o_ref = next(it)
        p = _dot(a_ref[...], b_ref[...], NT)
        if epi:
            p = p * (2.0 * jnp.maximum(e_ref[...], 0.0))
        o_ref[...] = p.astype(out_dtype)
        if with_relu2:
            act_ref = next(it)
            act_ref[...] = jnp.square(jnp.maximum(p, 0.0)).astype(act_ref.dtype)

    blk = pl.BlockSpec((tm, r), lambda i, j: (i, j))
    in_specs = [pl.BlockSpec((tm, d), lambda i, j: (i, 0)), pl.BlockSpec((None, r, d), lambda i, j: (j, ob, 0))]
    ins = [a, gw]
    if epi:
        in_specs.append(blk)
        ins.append(relu_grad_of)
    out_shape = _sds((m, N_DEV * r), out_dtype)
    if with_relu2:
        return _call(body, name=name, grid=(m // tm, N_DEV), in_specs=in_specs, out_specs=(blk, blk),
                     out_shape=(out_shape, _sds((m, N_DEV * r), BF16)), compiler_params=_cparams())(*ins)
    return _call(body, name=name, grid=(m // tm, N_DEV), in_specs=in_specs, out_specs=blk,
                 out_shape=out_shape, compiler_params=_cparams())(*ins)


def blockdiag_dw(xt, dz, name):
    t = xt.shape[1]

    def body(a_ref, b_ref, o_ref):
        o_ref[...] = _dot(a_ref[...], b_ref[...])

    return _call(body, name=name, grid=(C_BLOCKS,),
                 in_specs=[pl.BlockSpec((C_BLOCK_DIM, t), lambda g: (g, 0)),
                           pl.BlockSpec((t, C_BLOCK_DIM), lambda g: (0, g))],
                 out_specs=pl.BlockSpec((None, C_BLOCK_DIM, C_BLOCK_DIM), lambda g: (g, 0, 0)),
                 out_shape=_sds((C_BLOCKS, C_BLOCK_DIM, C_BLOCK_DIM)))(xt, dz)


def _band_mask(n, nblk, transposed):
    shape = (3 * BLOCK, A_GROUP * BLOCK) if transposed else (A_GROUP * BLOCK, 3 * BLOCK)
    qi = lax.broadcasted_iota(jnp.int32, shape, 1 if transposed else 0) & (BLOCK - 1)
    kj = lax.broadcasted_iota(jnp.int32, shape, 0 if transposed else 1)
    lo = jnp.where(n > 0, 0, BLOCK)
    hi = jnp.where(n < nblk - 1, 3 * BLOCK, 2 * BLOCK)
    return (jnp.abs(kj - BLOCK - qi) <= WINDOW) & (kj >= lo) & (kj < hi)


def _band_rows(ref, n, nblk):
    starts = [jnp.maximum(n - 1, 0), n, jnp.minimum(n + 1, nblk - 1)]
    return jnp.concatenate([ref[pl.ds(pl.multiple_of(s * BLOCK, BLOCK), BLOCK), :] for s in starts], axis=0)


def _head_cols(j):
    return slice(j * A_HEAD_DIM, (j + 1) * A_HEAD_DIM)


def attn_fwd(q, k, v, bias, sink_b, name):
    t = q.shape[0]
    nblk = t // BLOCK
    scale = A_HEAD_DIM ** -0.5

    def body(q_ref, k_ref, v_ref, b_ref, s_ref, o_ref):
        n = pl.program_id(1)
        kb = _band_rows(k_ref, n, nblk).astype(BF16)
        vb = _band_rows(v_ref, n, nblk).astype(BF16)
        mask = _band_mask(n, nblk, False)
        q4 = jnp.concatenate([q_ref[:, _head_cols(j)] for j in range(A_GROUP)], axis=0)
        b4 = jnp.concatenate([b_ref[j] for j in range(A_GROUP)], axis=0)
        sk = jnp.concatenate([jnp.broadcast_to(s_ref[j:j + 1, 0:1], (BLOCK, 1)) for j in range(A_GROUP)], axis=0)
        s = jnp.where(mask, _dot(q4, kb, NT) * scale + b4, NEG_INF)
        mx = jnp.maximum(jnp.max(s, axis=1, keepdims=True), sk)
        p = jnp.exp(s - mx)
        den = jnp.sum(p, axis=1, keepdims=True) + jnp.exp(sk - mx)
        o4 = _dot(p * (1.0 / den), vb)
        for j in range(A_GROUP):
            o_ref[:, _head_cols(j)] = o4[j * BLOCK:(j + 1) * BLOCK, :]

    gw = A_GROUP * A_HEAD_DIM
    return _call(body, name=name, grid=(A_KV_HEADS, nblk),
                 in_specs=[pl.BlockSpec((BLOCK, gw), lambda g, n: (n, g)),
                           pl.BlockSpec((t, A_HEAD_DIM), lambda g, n: (0, g)),
                           pl.BlockSpec((t, A_HEAD_DIM), lambda g, n: (0, g)),
                           pl.BlockSpec((A_GROUP, BLOCK, 3 * BLOCK), lambda g, n: (g, 0, 0)),
                           pl.BlockSpec((None, 8, 128), lambda g, n: (g, 0, 0))],
                 out_specs=pl.BlockSpec((BLOCK, gw), lambda g, n: (n, g)),
                 out_shape=_sds((t, A_Q)))(q, k, v, bias, sink_b)


def attn_bwd(q, k, v, bias, bias_t, sink_b, do, o, name):
    t = q.shape[0]
    nblk = t // BLOCK
    scale = A_HEAD_DIM ** -0.5

    def body(q_ref, k_ref, v_ref, b_ref, bt_ref, s_ref, do_ref, o_ref, dq_ref, dk_ref, dv_ref, db_ref, ds_ref):
        n = pl.program_id(1)

        @pl.when(n == 0)
        def _():
            dk_ref[...] = jnp.zeros_like(dk_ref)
            dv_ref[...] = jnp.zeros_like(dv_ref)
            db_ref[...] = jnp.zeros_like(db_ref)
            ds_ref[...] = jnp.zeros_like(ds_ref)

        kb = _band_rows(k_ref, n, nblk).astype(BF16)
        vb = _band_rows(v_ref, n, nblk).astype(BF16)
        heads = range(A_GROUP)
        mask = _band_mask(n, nblk, False)
        mask_t = _band_mask(n, nblk, True)
        q4 = jnp.concatenate([q_ref[:, _head_cols(j)] for j in heads], axis=0).astype(BF16)
        do4 = jnp.concatenate([do_ref[:, _head_cols(j)] for j in heads], axis=0)
        doo = do4 * jnp.concatenate([o_ref[:, _head_cols(j)] for j in heads], axis=0)
        do4 = do4.astype(BF16)
        b4 = jnp.concatenate([b_ref[j] for j in heads], axis=0)
        bt4 = jnp.concatenate([bt_ref[j] for j in heads], axis=1)
        sk = jnp.concatenate([jnp.broadcast_to(s_ref[j:j + 1, 0:1], (BLOCK, 1)) for j in heads], axis=0)
        sk_t = jnp.concatenate([jnp.broadcast_to(s_ref[j:j + 1, 0:1], (1, BLOCK)) for j in heads], axis=1)
        s = jnp.where(mask, _dot(q4, kb, NT) * scale + b4, NEG_INF)
        mx = jnp.maximum(jnp.max(s, axis=1, keepdims=True), sk)
        p = jnp.exp(s - mx)
        den = jnp.sum(p, axis=1, keepdims=True) + jnp.exp(sk - mx)
        rden = 1.0 / den
        p = p * rden
        psink_delta = jnp.exp(sk - mx) * rden * jnp.sum(doo, axis=1, keepdims=True)
        dsc = p * (_dot(do4, vb, NT) - jnp.sum(doo, axis=1, keepdims=True))
        dq4 = _dot(dsc, kb) * scale
        for j in heads:
            rows = slice(j * BLOCK, (j + 1) * BLOCK)
            db_ref[j] += dsc[rows, :]
            ds_ref[j:j + 1, :] += jnp.broadcast_to(-jnp.sum(psink_delta[rows, :], axis=0, keepdims=True), (1, 128))
            dq_ref[:, _head_cols(j)] = dq4[rows, :]
        st = jnp.where(mask_t, _dot(kb, q4, NT) * scale + bt4, NEG_INF)
        mxt = jnp.maximum(jnp.max(st, axis=0, keepdims=True), sk_t)
        pt = jnp.exp(st - mxt)
        dent = jnp.sum(pt, axis=0, keepdims=True) + jnp.exp(sk_t - mxt)
        pt = pt * (1.0 / dent)
        delta_t = _dot(jnp.ones((8, A_HEAD_DIM), F32), doo, NT, hi=True)[0:1, :]
        dst = pt * (_dot(vb, do4, NT) - delta_t)
        dkb = _dot(dst, q4) * scale
        dvb = _dot(pt, do4)
        starts = [jnp.maximum(n - 1, 0), n, jnp.minimum(n + 1, nblk - 1)]
        for c, st_ in enumerate(starts):
            rows = pl.ds(pl.multiple_of(st_ * BLOCK, BLOCK), BLOCK)
            dk_ref[rows, :] += dkb[c * BLOCK:(c + 1) * BLOCK, :]
            dv_ref[rows, :] += dvb[c * BLOCK:(c + 1) * BLOCK, :]

    gw = A_GROUP * A_HEAD_DIM
    qspec = pl.BlockSpec((BLOCK, gw), lambda g, n: (n, g))
    kspec = pl.BlockSpec((t, A_HEAD_DIM), lambda g, n: (0, g))
    sspec = pl.BlockSpec((None, 8, 128), lambda g, n: (g, 0, 0))
    bspec = pl.BlockSpec((A_GROUP, BLOCK, 3 * BLOCK), lambda g, n: (g, 0, 0))
    btspec = pl.BlockSpec((A_GROUP, 3 * BLOCK, BLOCK), lambda g, n: (g, 0, 0))
    return _call(body, name=name, grid=(A_KV_HEADS, nblk),
                 in_specs=[qspec, kspec, kspec, bspec, btspec, sspec, qspec, qspec],
                 out_specs=(qspec, kspec, kspec, bspec, sspec),
                 out_shape=(_sds((t, A_Q)), _sds((t, A_KV)), _sds((t, A_KV)),
                            _sds((A_HEADS, BLOCK, 3 * BLOCK)), _sds((A_KV_HEADS, 8, 128))),
                 compiler_params=_cparams())(q, k, v, bias, bias_t, sink_b, do, o)


def xattn_fwd(q, k, v, name):
    t, d = q.shape
    ml = k.shape[0]
    dh = d // X_HEADS
    tq = min(512, t)
    scale = dh ** -0.5

    def body(q_ref, k_ref, v_ref, o_ref):
        s = _dot(q_ref[...], k_ref[...], NT) * scale
        p = jnp.exp(s - jnp.max(s, axis=1, keepdims=True))
        p = p * (1.0 / jnp.sum(p, axis=1, keepdims=True))
        o_ref[...] = _dot(p, v_ref[...])

    qspec = pl.BlockSpec((tq, dh), lambda h, i: (i, h))
    kspec = pl.BlockSpec((ml, dh), lambda h, i: (0, h))
    return _call(body, name=name, grid=(X_HEADS, t // tq), in_specs=[qspec, kspec, kspec], out_specs=qspec,
                 out_shape=_sds((t, d)))(q, k, v)


def xattn_bwd(q, k, v, o, do, name):
    t, d = q.shape
    ml = k.shape[0]
    dh = d // X_HEADS
    tq = min(512, t)
    scale = dh ** -0.5

    def body(q_ref, k_ref, v_ref, o_ref, do_ref, dq_ref, dk_ref, dv_ref):
        i = pl.program_id(1)
        qv, kv, vv = q_ref[...].astype(BF16), k_ref[...].astype(BF16), v_ref[...].astype(BF16)
        dov = do_ref[...]
        doo = dov * o_ref[...]
        dov = dov.astype(BF16)
        s = _dot(qv, kv, NT) * scale
        p = jnp.exp(s - jnp.max(s, axis=1, keepdims=True))
        p = p * (1.0 / jnp.sum(p, axis=1, keepdims=True))
        ds = p * (_dot(dov, vv, NT) - jnp.sum(doo, axis=1, keepdims=True))
        dq_ref[...] = _dot(ds, kv) * scale
        st = _dot(kv, qv, NT) * scale
        pt = jnp.exp(st - jnp.max(st, axis=0, keepdims=True))
        pt = pt * (1.0 / jnp.sum(pt, axis=0, keepdims=True))
        delta_t = _dot(jnp.ones((8, dh), F32), doo, NT, hi=True)[0:1, :]
        dst = pt * (_dot(vv, dov, NT) - delta_t)
        dkp = _dot(dst, qv) * scale
        dvp = _dot(pt, dov)

        @pl.when(i == 0)
        def _():
            dk_ref[...] = dkp
            dv_ref[...] = dvp

        @pl.when(i > 0)
        def _():
            dk_ref[...] += dkp
            dv_ref[...] += dvp

    qspec = pl.BlockSpec((tq, dh), lambda h, i: (i, h))
    kspec = pl.BlockSpec((ml, dh), lambda h, i: (0, h))
    return _call(body, name=name, grid=(X_HEADS, t // tq), in_specs=[qspec, kspec, kspec, qspec, qspec],
                 out_specs=(qspec, kspec, kspec),
                 out_shape=(_sds((t, d)), _sds((ml, d)), _sds((ml, d))))(q, k, v, o, do)


def scan_lead(a, u, name, reverse, inclusive):
    n, r, c = a.shape
    blk = max(1, min(n, (1 << 18) // (max(r, 8) * c)))
    while n % blk:
        blk -= 1
    nb = n // blk

    def body(a_ref, u_ref, o_ref, carry):
        @pl.when(pl.program_id(0) == 0)
        def _():
            carry[...] = jnp.zeros_like(carry)

        def step(s, h):
            idx = (blk - 1 - s) if reverse else s
            hn = a_ref[idx] * h + u_ref[idx]
            o_ref[idx] = hn if inclusive else h
            return hn

        carry[...] = lax.fori_loop(0, blk, step, carry[...])

    spec = pl.BlockSpec((blk, r, c), (lambda i: (nb - 1 - i, 0, 0)) if reverse else (lambda i: (i, 0, 0)))
    return _call(body, name=name, grid=(nb,), in_specs=[spec, spec], out_specs=spec,
                 out_shape=_sds((n, r, c)), scratch_shapes=[pltpu.VMEM((r, c), F32)])(a, u)


def _chunk_mats(bwd_dir):
    i = lax.broadcasted_iota(jnp.int32, (GLA_TILE, GLA_TILE), 0)
    j = lax.broadcasted_iota(jnp.int32, (GLA_TILE, GLA_TILE), 1)
    same = lax.shift_right_logical(i, 4) == lax.shift_right_logical(j, 4)
    if bwd_dir:
        cm, cm_t = same & (j >= i), same & (i >= j)
        mk, mk_t = same & (j > i), same & (i > j)
    else:
        cm, cm_t = same & (j <= i), same & (i <= j)
        mk, mk_t = same & (j <= i), same & (i <= j)
    f = lambda b: jnp.where(b, 1.0, 0.0).astype(F32)
    return f(cm), f(cm_t), mk, mk_t, f(same)


def gla_gates_fwd(zf, zb, w2f, b2f, w2b, b2b, name):
    t = zf.shape[0]
    tm = min(256, t)

    def body(zf_ref, zb_ref, wf_ref, bf_ref, wb_ref, bb_ref, lf_ref, lb_ref):
        lf_ref[...] = -_softplus(-(_dot(zf_ref[...], wf_ref[...], hi=True) + bf_ref[...])) / GATE_TAU
        lb_ref[...] = -_softplus(-(_dot(zb_ref[...], wb_ref[...], hi=True) + bb_ref[...])) / GATE_TAU

    zs = pl.BlockSpec((tm, GATE_RANK), lambda i: (i, 0))
    ws = pl.BlockSpec((GATE_RANK, B_QK), lambda i: (0, 0))
    bs = pl.BlockSpec((1, B_QK), lambda i: (0, 0))
    os_ = pl.BlockSpec((tm, B_QK), lambda i: (i, 0))
    return _call(body, name=name, grid=(t // tm,), in_specs=[zs, zs, ws, bs, ws, bs], out_specs=(os_, os_),
                 out_shape=(_sds((t, B_QK)),) * 2)(zf, zb, w2f, b2f.reshape(1, B_QK), w2b, b2b.reshape(1, B_QK))


def gla_gates_bwd(zf, zb, w2f, b2f, w2b, b2b, dlf, dlb, name):
    t = zf.shape[0]
    tm = min(256, t)

    def body(zf_ref, zb_ref, wf_ref, bf_ref, wb_ref, bb_ref, dlf_ref, dlb_ref,
             dzf_ref, dzb_ref, dpf_ref, dpb_ref, dbf_ref, dbb_ref):
        first = pl.program_id(0) == 0
        for z_ref, w_ref, b_ref, dl_ref, dz_ref, dp_ref, db_ref in (
                (zf_ref, wf_ref, bf_ref, dlf_ref, dzf_ref, dpf_ref, dbf_ref),
                (zb_ref, wb_ref, bb_ref, dlb_ref, dzb_ref, dpb_ref, dbb_ref)):
            pre = _dot(z_ref[...], w_ref[...], hi=True) + b_ref[...]
            dpre = dl_ref[...] * (1.0 / GATE_TAU) * _sigmoid(-pre)
            dp_ref[...] = dpre
            dz_ref[...] = _dot(dpre, w_ref[...], NT, hi=True)
            part = jnp.sum(dpre, axis=0, keepdims=True)

            @pl.when(first)
            def _():
                db_ref[...] = part

            @pl.when(jnp.logical_not(first))
            def _():
                db_ref[...] += part

    zs = pl.BlockSpec((tm, GATE_RANK), lambda i: (i, 0))
    ws = pl.BlockSpec((GATE_RANK, B_QK), lambda i: (0, 0))
    bs = pl.BlockSpec((1, B_QK), lambda i: (0, 0))
    os_ = pl.BlockSpec((tm, B_QK), lambda i: (i, 0))
    return _call(body, name=name, grid=(t // tm,), in_specs=[zs, zs, ws, bs, ws, bs, os_, os_],
                 out_specs=(zs, zs, os_, os_, bs, bs),
                 out_shape=(_sds((t, GATE_RANK)),) * 2 + (_sds((t, B_QK)),) * 2 + (_sds((1, B_QK)),) * 2)(
        zf, zb, w2f, b2f.reshape(1, B_QK), w2b, b2b.reshape(1, B_QK), dlf, dlb)


def gla_outer(xt, lat, y, name, bwd_dir, mode):
    t = y.shape[0]
    nchunk = t // GLA_CHUNK
    khat = mode == "khat"
    scale = B_KEY_DIM ** -0.5

    def body(xt_ref, lat_ref, y_ref, *outs):
        _, cm_t, _, _, same = _chunk_mats(bwd_dir)
        lat_v = lat_ref[...]
        bt = _dot(lat_v, cm_t, hi=True)
        if khat:
            mult = jnp.exp(_dot(lat_v, same, hi=True) - bt)
        else:
            mult = jnp.exp(bt) * scale
        xm = xt_ref[...] * mult
        lane = lax.shift_right_logical(lax.broadcasted_iota(jnp.int32, (1, GLA_TILE), 1), 4)
        ones = jnp.ones((GLA_TILE, B_VAL_DIM), F32)
        yv = [y_ref[:, h * B_VAL_DIM:(h + 1) * B_VAL_DIM].astype(BF16) for h in range(B_HEADS)]
        for c in range(CHUNKS_PER_TILE):
            sel = lane == c
            xc = jnp.where(sel, xm, 0.0).astype(BF16)
            for h in range(B_HEADS):
                rows = slice(h * B_KEY_DIM, (h + 1) * B_KEY_DIM)
                outs[0][c, rows, :] = _dot(xc[rows, :], yv[h])
            if khat:
                outs[1][c] = jnp.exp(_dot(jnp.where(sel, lat_v, 0.0), ones, hi=True))

    tspec = pl.BlockSpec((B_QK, GLA_TILE), lambda i: (0, i))
    ospec = pl.BlockSpec((CHUNKS_PER_TILE, B_QK, B_VAL_DIM), lambda i: (i, 0, 0))
    oshape = _sds((nchunk, B_QK, B_VAL_DIM))
    return _call(body, name=name, grid=(t // GLA_TILE,),
                 in_specs=[tspec, tspec, pl.BlockSpec((GLA_TILE, B_V), lambda i: (i, 0))],
                 out_specs=(ospec, ospec) if khat else ospec,
                 out_shape=(oshape, oshape) if khat else oshape)(xt, lat, y)


def _head_lane_mask(h):
    lane = lax.broadcasted_iota(jnp.int32, (1, B_QK), 1)
    return lax.shift_right_logical(lane, 6) == h


def _chunk_rows(c):
    return slice(c * GLA_CHUNK, (c + 1) * GLA_CHUNK)


def gla_inner_fwd(q, k, v, la, sp, name, bwd_dir):
    t = q.shape[0]
    scale = B_KEY_DIM ** -0.5

    def body(q_ref, k_ref, v_ref, la_ref, sp_ref, o_ref):
        cm, _, mk, _, _ = _chunk_mats(bwd_dir)
        b = _dot(cm, la_ref[...], hi=True)
        qt = q_ref[...] * scale * jnp.exp(b)
        kt = k_ref[...] * jnp.exp(jnp.minimum(-b, EXP_CLAMP))
        spb = [sp_ref[c].astype(BF16) for c in range(CHUNKS_PER_TILE)]
        for h in range(B_HEADS):
            lm = _head_lane_mask(h)
            qm = jnp.where(lm, qt, 0.0).astype(BF16)
            km = jnp.where(lm, kt, 0.0).astype(BF16)
            vs = slice(h * B_VAL_DIM, (h + 1) * B_VAL_DIM)
            att = jnp.where(mk, _dot(qm, km, NT), 0.0)
            inter = jnp.concatenate([_dot(qm[_chunk_rows(c), :], spb[c]) for c in range(CHUNKS_PER_TILE)], axis=0)
            o_ref[:, vs] = _dot(att, v_ref[:, vs]) + inter

    qs = pl.BlockSpec((GLA_TILE, B_QK), lambda i: (i, 0))
    vs_ = pl.BlockSpec((GLA_TILE, B_V), lambda i: (i, 0))
    ss = pl.BlockSpec((CHUNKS_PER_TILE, B_QK, B_VAL_DIM), lambda i: (i, 0, 0))
    return _call(body, name=name, grid=(t // GLA_TILE,), in_specs=[qs, qs, vs_, qs, ss], out_specs=vs_,
                 out_shape=_sds((t, B_V)))(q, k, v, la, sp)


def gla_inner_bwd(q, k, v, la, do, sp, gs, dec, name, bwd_dir, add=None):
    t = q.shape[0]
    scale = B_KEY_DIM ** -0.5
    hasadd = add is not None

    def body(*refs):
        it = iter(refs)
        q_ref, k_ref, v_ref, la_ref, do_ref, sp_ref, gs_ref, dec_ref = [next(it) for _ in range(8)]
        adds = [next(it) for _ in range(3)] if hasadd else None
        dq_ref, dk_ref, dv_ref, dla_ref = [next(it) for _ in range(4)]
        cm, cm_t, mk, mk_t, same = _chunk_mats(bwd_dir)
        la_v = la_ref[...]
        b = _dot(cm, la_v, hi=True)
        btot = _dot(same, la_v, hi=True)
        eb = jnp.exp(b)
        ek = jnp.exp(jnp.minimum(-b, EXP_CLAMP))
        ekh = jnp.exp(btot - b)
        qt = q_ref[...] * scale * eb
        kt = k_ref[...] * ek
        kh = k_ref[...] * ekh
        spb = [sp_ref[c].astype(BF16) for c in range(CHUNKS_PER_TILE)]
        gsb = [gs_ref[c].astype(BF16) for c in range(CHUNKS_PER_TILE)]
        dqt = jnp.zeros((GLA_TILE, B_QK), F32)
        dkt = jnp.zeros((GLA_TILE, B_QK), F32)
        dkh = jnp.zeros((GLA_TILE, B_QK), F32)
        for h in range(B_HEADS):
            lm = _head_lane_mask(h)
            qm = jnp.where(lm, qt, 0.0).astype(BF16)
            km = jnp.where(lm, kt, 0.0).astype(BF16)
            khm = jnp.where(lm, kh, 0.0).astype(BF16)
            vs = slice(h * B_VAL_DIM, (h + 1) * B_VAL_DIM)
            vh = v_ref[:, vs].astype(BF16)
            doh = do_ref[:, vs].astype(BF16)
            da = jnp.where(mk, _dot(doh, vh, NT), 0.0)
            da_t = jnp.where(mk_t, _dot(vh, doh, NT), 0.0)
            att_t = jnp.where(mk_t, _dot(km, qm, NT), 0.0)
            dv_h = _dot(att_t, doh) + jnp.concatenate(
                [_dot(khm[_chunk_rows(c), :], gsb[c]) for c in range(CHUNKS_PER_TILE)], axis=0)
            if hasadd:
                dv_h = dv_h + adds[2][:, vs]
            dv_ref[:, vs] = dv_h
            dq_inter = jnp.concatenate(
                [_dot(doh[_chunk_rows(c), :], spb[c], NT) for c in range(CHUNKS_PER_TILE)], axis=0)
            dqt = dqt + _dot(da, km) + jnp.where(lm, dq_inter, 0.0)
            dkt = dkt + _dot(da_t, qm)
            dkh_inter = jnp.concatenate(
                [_dot(vh[_chunk_rows(c), :], gsb[c], NT) for c in range(CHUNKS_PER_TILE)], axis=0)
            dkh = dkh + jnp.where(lm, dkh_inter, 0.0)
        dq = dqt * scale * eb
        dk = dkt * ek + dkh * ekh
        if hasadd:
            dq = dq + adds[0][...]
            dk = dk + adds[1][...]
        dq_ref[...] = dq
        dk_ref[...] = dk
        db = dqt * qt - dkt * kt - dkh * kh
        ones16 = jnp.ones((GLA_CHUNK, B_VAL_DIM), F32)
        t2 = jnp.concatenate(
            [_dot(ones16, gs_ref[c] * dec_ref[c] * sp_ref[c], NT, hi=True) for c in range(CHUNKS_PER_TILE)], axis=0)
        dla_ref[...] = _dot(cm_t, db, hi=True) + _dot(same, dkh * kh, hi=True) + t2

    qs = pl.BlockSpec((GLA_TILE, B_QK), lambda i: (i, 0))
    vs_ = pl.BlockSpec((GLA_TILE, B_V), lambda i: (i, 0))
    ss = pl.BlockSpec((CHUNKS_PER_TILE, B_QK, B_VAL_DIM), lambda i: (i, 0, 0))
    ins = [q, k, v, la, do, sp, gs, dec] + (list(add) if hasadd else [])
    in_specs = [qs, qs, vs_, qs, vs_, ss, ss, ss] + ([qs, qs, vs_] if hasadd else [])
    return _call(body, name=name, grid=(t // GLA_TILE,), in_specs=in_specs, out_specs=(qs, qs, vs_, qs),
                 out_shape=(_sds((t, B_QK)), _sds((t, B_QK)), _sds((t, B_V)), _sds((t, B_QK))),
                 compiler_params=_cparams())(*ins)


def gla_out_fwd(of, ob, g, gn, name):
    t = of.shape[0]
    tm = min(256, t)

    def body(of_ref, ob_ref, g_ref, gn_ref, o_ref):
        for h in range(B_HEADS):
            vs = slice(h * B_VAL_DIM, (h + 1) * B_VAL_DIM)
            o = of_ref[:, vs] + ob_ref[:, vs]
            on = o * lax.rsqrt(jnp.mean(o * o, axis=1, keepdims=True) + EPS)
            gv = g_ref[:, vs]
            o_ref[:, vs] = on * gn_ref[:, vs] * (gv * _sigmoid(gv))

    row = pl.BlockSpec((tm, B_V), lambda i: (i, 0))
    vec = pl.BlockSpec((1, B_V), lambda i: (0, 0))
    return _call(body, name=name, grid=(t // tm,), in_specs=[row, row, row, vec], out_specs=row,
                 out_shape=_sds((t, B_V)))(of, ob, g, gn.reshape(1, B_V))


def gla_out_bwd(of, ob, g, gn, dout, name):
    t = of.shape[0]
    tm = min(256, t)

    def body(of_ref, ob_ref, g_ref, gn_ref, d_ref, do_ref, dg_ref, dgn_ref):
        first = pl.program_id(0) == 0
        for h in range(B_HEADS):
            vs = slice(h * B_VAL_DIM, (h + 1) * B_VAL_DIM)
            o = of_ref[:, vs] + ob_ref[:, vs]
            r = lax.rsqrt(jnp.mean(o * o, axis=1, keepdims=True) + EPS)
            on = o * r
            gv = g_ref[:, vs]
            sg = _sigmoid(gv)
            silu = gv * sg
            dv = d_ref[:, vs]
            gnv = gn_ref[:, vs]
            dg_ref[:, vs] = dv * on * gnv * (sg * (1.0 + gv * (1.0 - sg)))
            don = dv * silu * gnv
            do_ref[:, vs] = r * (don - on * jnp.mean(don * on, axis=1, keepdims=True))
            part = jnp.sum(dv * silu * on, axis=0, keepdims=True)

            @pl.when(first)
            def _():
                dgn_ref[:, vs] = part

            @pl.when(jnp.logical_not(first))
            def _():
                dgn_ref[:, vs] += part

    row = pl.BlockSpec((tm, B_V), lambda i: (i, 0))
    vec = pl.BlockSpec((1, B_V), lambda i: (0, 0))
    return _call(body, name=name, grid=(t // tm,), in_specs=[row, row, row, vec, row], out_specs=(row, row, vec),
                 out_shape=(_sds((t, B_V)), _sds((t, B_V)), _sds((1, B_V))))(of, ob, g, gn.reshape(1, B_V), dout)


def _shift(x, k):
    if k > 0:
        return jnp.concatenate([x[k:], jnp.zeros((k,) + x.shape[1:], x.dtype)], axis=0)
    return jnp.concatenate([jnp.zeros((-k,) + x.shape[1:], x.dtype), x[:k]], axis=0)


def _lru_gates(xc, s, wa_ref, ba_ref, wx_ref, bx_ref, lam_ref):
    cols = [slice(g * C_BLOCK_DIM, (g + 1) * C_BLOCK_DIM) for g in range(C_BLOCKS)]
    zr = jnp.concatenate([_dot(xc[:, cs], wa_ref[s, g]) for g, cs in enumerate(cols)], axis=1) + ba_ref[s:s + 1, :]
    zi = jnp.concatenate([_dot(xc[:, cs], wx_ref[s, g]) for g, cs in enumerate(cols)], axis=1) + bx_ref[s:s + 1, :]
    r = _sigmoid(zr)
    i = _sigmoid(zi)
    sp = _softplus(-lam_ref[s:s + 1, :])
    log_a = -LRU_C * r * sp
    return r, i, sp, log_a


def lru_gates_fwd(x0, xm2, xm1, xp1, cw, cb, wa, ba, wx, bx, lam, name):
    t = x0.shape[0]
    tm = min(256, t)

    def body(x0_ref, xm2_ref, xm1_ref, xp1_ref, cw_ref, cb_ref, wa_ref, ba_ref, wx_ref, bx_ref, lam_ref,
             xc_ref, a0_ref, u0_ref, a1_ref, u1_ref):
        xc = (xm2_ref[...] * cw_ref[0:1, :] + xm1_ref[...] * cw_ref[1:2, :] + x0_ref[...] * cw_ref[2:3, :]
              + xp1_ref[...] * cw_ref[3:4, :] + cb_ref[...])
        xc_ref[...] = xc
        for s, (a_ref, u_ref) in enumerate(((a0_ref, u0_ref), (a1_ref, u1_ref))):
            _, i, _, log_a = _lru_gates(xc, s, wa_ref, ba_ref, wx_ref, bx_ref, lam_ref)
            a_ref[...] = jnp.exp(log_a)
            u_ref[...] = jnp.sqrt(-_expm1(2.0 * log_a)) * (i * xc)

    row = pl.BlockSpec((tm, C_WIDTH), lambda i: (i, 0))
    full = lambda shape: pl.BlockSpec(shape, lambda i: (0,) * len(shape))
    wshape = (2, C_BLOCKS, C_BLOCK_DIM, C_BLOCK_DIM)
    return _call(body, name=name, grid=(t // tm,),
                 in_specs=[row] * 4 + [full((4, C_WIDTH)), full((1, C_WIDTH)), full(wshape), full((2, C_WIDTH)),
                                       full(wshape), full((2, C_WIDTH)), full((2, C_WIDTH))],
                 out_specs=(row,) * 5, out_shape=(_sds((t, C_WIDTH)),) * 5)(
        x0, xm2, xm1, xp1, cw, cb.reshape(1, C_WIDTH), wa, ba, wx, bx, lam)


def lru_gates_bwd(xc, g0, hs0, g1, hs1, wa, ba, wx, bx, lam, name):
    t = xc.shape[0]
    tm = min(256, t)

    def body(xc_ref, g0_ref, hs0_ref, g1_ref, hs1_ref, wa_ref, ba_ref, wx_ref, bx_ref, lam_ref,
             dxc_ref, dzr0_ref, dzi0_ref, dzr1_ref, dzi1_ref, dlam_ref, dba_ref, dbx_ref):
        first = pl.program_id(0) == 0

        @pl.when(first)
        def _():
            dlam_ref[...] = jnp.zeros_like(dlam_ref)
            dba_ref[...] = jnp.zeros_like(dba_ref)
            dbx_ref[...] = jnp.zeros_like(dbx_ref)

        xcv = xc_ref[...]
        dxc = jnp.zeros_like(xcv)
        cols = [slice(g * C_BLOCK_DIM, (g + 1) * C_BLOCK_DIM) for g in range(C_BLOCKS)]
        for s, (g_ref, hs_ref, dzr_ref, dzi_ref) in enumerate(
                ((g0_ref, hs0_ref, dzr0_ref, dzi0_ref), (g1_ref, hs1_ref, dzr1_ref, dzi1_ref))):
            r, i, sp, log_a = _lru_gates(xcv, s, wa_ref, ba_ref, wx_ref, bx_ref, lam_ref)
            du = g_ref[...]
            da = du * hs_ref[...]
            a = jnp.exp(log_a)
            e2 = jnp.exp(2.0 * log_a)
            c = jnp.sqrt(-_expm1(2.0 * log_a))
            ix = i * xcv
            dlog = da * a - du * ix * (e2 / c)
            dix = du * c
            dxc = dxc + dix * i
            dzi = dix * xcv * i * (1.0 - i)
            dzr = dlog * (-LRU_C * sp) * r * (1.0 - r)
            dzr_ref[...] = dzr
            dzi_ref[...] = dzi
            dxc = dxc + jnp.concatenate(
                [_dot(dzr[:, cs], wa_ref[s, g], NT) + _dot(dzi[:, cs], wx_ref[s, g], NT) for g, cs in enumerate(cols)],
                axis=1)
            dsp = jnp.sum(dlog * (-LRU_C * r), axis=0, keepdims=True)
            dlam_ref[s:s + 1, :] += dsp * (-_sigmoid(-lam_ref[s:s + 1, :]))
            dba_ref[s:s + 1, :] += jnp.sum(dzr, axis=0, keepdims=True)
            dbx_ref[s:s + 1, :] += jnp.sum(dzi, axis=0, keepdims=True)
        dxc_ref[...] = dxc

    row = pl.BlockSpec((tm, C_WIDTH), lambda i: (i, 0))
    full = lambda shape: pl.BlockSpec(shape, lambda i: (0,) * len(shape))
    wshape = (2, C_BLOCKS, C_BLOCK_DIM, C_BLOCK_DIM)
    vec2 = full((2, C_WIDTH))
    return _call(body, name=name, grid=(t // tm,),
                 in_specs=[row] * 5 + [full(wshape), vec2, full(wshape), vec2, vec2],
                 out_specs=(row,) * 5 + (vec2,) * 3,
                 out_shape=(_sds((t, C_WIDTH)),) * 5 + (_sds((2, C_WIDTH)),) * 3)(
        xc, g0, hs0, g1, hs1, wa, ba, wx, bx, lam)


def lru_out_fwd(h0, h1, y, name):
    t = y.shape[0]
    tm = min(256, t)

    def body(h0_ref, h1_ref, y_ref, o_ref):
        o_ref[...] = (h0_ref[...] + h1_ref[...]) * _gelu(y_ref[...])

    row = pl.BlockSpec((tm, C_WIDTH), lambda i: (i, 0))
    return _call(body, name=name, grid=(t // tm,), in_specs=[row] * 3, out_specs=row,
                 out_shape=_sds((t, C_WIDTH)))(h0, h1, y)


def lru_out_bwd(h0, h1, y, dout, name):
    t = y.shape[0]
    tm = min(256, t)

    def body(h0_ref, h1_ref, y_ref, d_ref, dh_ref, dy_ref):
        yv = y_ref[...]
        dv = d_ref[...]
        dh_ref[...] = dv * _gelu(yv)
        dy_ref[...] = dv * (h0_ref[...] + h1_ref[...]) * _gelu_grad(yv)

    row = pl.BlockSpec((tm, C_WIDTH), lambda i: (i, 0))
    return _call(body, name=name, grid=(t // tm,), in_specs=[row] * 4, out_specs=(row, row),
                 out_shape=(_sds((t, C_WIDTH)),) * 2)(h0, h1, y, dout)


def conv_bwd(dxc, dp2, dp1, dm1, x0, xm2, xm1, xp1, cw, name):
    t = x0.shape[0]
    tm = min(256, t)

    def body(d_ref, dp2_ref, dp1_ref, dm1_ref, x0_ref, xm2_ref, xm1_ref, xp1_ref, cw_ref, dx_ref, dcw_ref, dcb_ref):
        @pl.when(pl.program_id(0) == 0)
        def _():
            dcw_ref[...] = jnp.zeros_like(dcw_ref)
            dcb_ref[...] = jnp.zeros_like(dcb_ref)

        dv = d_ref[...]
        dx_ref[...] = (dp2_ref[...] * cw_ref[0:1, :] + dp1_ref[...] * cw_ref[1:2, :] + dv * cw_ref[2:3, :]
                       + dm1_ref[...] * cw_ref[3:4, :])
        for j, x_ref in enumerate((xm2_ref, xm1_ref, x0_ref, xp1_ref)):
            dcw_ref[j:j + 1, :] += jnp.sum(dv * x_ref[...], axis=0, keepdims=True)
        dcb_ref[...] += jnp.sum(dv, axis=0, keepdims=True)

    row = pl.BlockSpec((tm, C_WIDTH), lambda i: (i, 0))
    cws = pl.BlockSpec((4, C_WIDTH), lambda i: (0, 0))
    cbs = pl.BlockSpec((1, C_WIDTH), lambda i: (0, 0))
    return _call(body, name=name, grid=(t // tm,), in_specs=[row] * 8 + [cws], out_specs=(row, cws, cbs),
                 out_shape=(_sds((t, C_WIDTH)), _sds((4, C_WIDTH)), _sds((1, C_WIDTH))))(
        dxc, dp2, dp1, dm1, x0, xm2, xm1, xp1, cw)


def _my_place():
    return lax.axis_index("x"), lax.axis_index("y"), lax.axis_index("c")


def all_gather(xs, name):
    r, c = xs.shape

    def body(x_ref, out_ref, send_sems, recv_sems, local_sem):
        x, y, cc = _my_place()
        me, sibling = (x, y, cc), (x, y, 1 - cc)
        chips = [(1 - x, y), (x, 1 - y), (1 - x, 1 - y)]

        def slot(px, py, pc):
            return out_ref.at[4 * px + 2 * py + pc]

        def copy(k, block, to, src=None):
            return pltpu.make_async_remote_copy(
                src_ref=slot(*block) if src is None else src, dst_ref=slot(*block),
                send_sem=send_sems.at[k], recv_sem=recv_sems.at[k], device_id=to, device_id_type=MESH)

        mine = pltpu.make_async_copy(x_ref, slot(*me), local_sem)
        mine.start()
        first = [copy(0, me, sibling, src=x_ref)]
        first += [copy(1 + j, me, (*chip, cc), src=x_ref) for j, chip in enumerate(chips)]
        for cp in first:
            cp.start()
        passed = [copy(4 + j, (*chip, cc), sibling) for j, chip in enumerate(chips)]
        for j, chip in enumerate(chips):
            copy(1 + j, (*chip, cc), me).wait_recv()
            passed[j].start()
        copy(0, sibling, me).wait_recv()
        for j, chip in enumerate(chips):
            copy(4 + j, (*chip, 1 - cc), me).wait_recv()
        for cp in first + passed:
            cp.wait_send()
        mine.wait()

    return _call(body, name=name, in_specs=[pl.BlockSpec(memory_space=pl.ANY)],
                 out_specs=pl.BlockSpec(memory_space=pl.ANY), out_shape=_sds((N_DEV, r, c), xs.dtype),
                 scratch_shapes=[pltpu.SemaphoreType.DMA((7,)), pltpu.SemaphoreType.DMA((7,)),
                                 pltpu.SemaphoreType.DMA])(xs)


def _stream_rows(r):
    nch = SIBLING_STREAMS // 4 if r % (8 * (SIBLING_STREAMS // 4)) == 0 else 1
    return nch, r // nch


def exchange_sibling(gw, name):
    _, r, c = gw.shape
    g5 = gw.reshape(4, 2, r, c)
    nch, rows = _stream_rows(r)

    def body(g_ref, out_ref, send_sems, recv_sems):
        x, y, cc = _my_place()
        swaps = []
        for q in range(4):
            for s in range(nch):
                k = q * nch + s
                win = pl.ds(s * rows, rows)
                swaps.append(pltpu.make_async_remote_copy(
                    src_ref=g_ref.at[q, 1 - cc, win], dst_ref=out_ref.at[q, win], send_sem=send_sems.at[k],
                    recv_sem=recv_sems.at[k], device_id=(x, y, 1 - cc), device_id_type=MESH))
        for cp in swaps:
            cp.start()
        for cp in swaps:
            cp.wait()

    nsem = 4 * nch
    return _call(body, name=name, in_specs=[pl.BlockSpec(memory_space=pl.ANY)],
                 out_specs=pl.BlockSpec(memory_space=pl.ANY), out_shape=_sds((4, r, c), gw.dtype),
                 scratch_shapes=[pltpu.SemaphoreType.DMA((nsem,)), pltpu.SemaphoreType.DMA((nsem,))])(g5)


HBM_SPEC = pl.BlockSpec(memory_space=pltpu.HBM)
SEM_SPEC = pl.BlockSpec(memory_space=pltpu.SEMAPHORE)
DATAFLOW = pltpu.SideEffectType.DATAFLOW_SIDE_EFFECTING


def _hbm(a):
    return pltpu.with_memory_space_constraint(a, pltpu.HBM)


def _peers(x, y, cc):
    return [(x, y, 1 - cc), (1 - x, y, cc), (x, 1 - y, cc), (1 - x, 1 - y, cc)]


def _slot(p):
    return 4 * p[0] + 2 * p[1] + p[2]


def gather_start(blk, name):
    r, c = blk.shape

    def body(v_ref, land_ref, send_sems, recv_sems, v_thru, land_thru, token):
        x, y, cc = _my_place()
        for k, to in enumerate(_peers(x, y, cc)):
            pltpu.make_async_remote_copy(
                src_ref=v_ref, dst_ref=land_ref.at[_slot((x, y, cc))], send_sem=send_sems.at[k],
                recv_sem=recv_sems.at[k], device_id=to, device_id_type=MESH).start()
        pltpu.make_async_copy(v_ref, land_ref.at[_slot((x, y, cc))], send_sems.at[4]).start()
        token[...] = jnp.zeros_like(token)

    return _call(
        body, name=name,
        out_shape=(pltpu.SemaphoreType.DMA((5,)), pltpu.SemaphoreType.DMA((4,)), pltpu.HBM((r, c), blk.dtype),
                   pltpu.HBM((N_DEV, r, c), blk.dtype), _sds((8, 128))),
        in_specs=(HBM_SPEC, HBM_SPEC),
        out_specs=(SEM_SPEC, SEM_SPEC, HBM_SPEC, HBM_SPEC, pl.BlockSpec(memory_space=pltpu.VMEM)),
        input_output_aliases={0: 2, 1: 3},
        compiler_params=pltpu.CompilerParams(has_side_effects=DATAFLOW),
    )(_hbm(blk), _hbm(lax.empty((N_DEV, r, c), blk.dtype)))


def gather_wait(send_sems, recv_sems, v_thru, land_thru, after, name):
    def body(v_ref, land_ref, send_sems, recv_sems, after_ref, v_out, land_out):
        x, y, cc = _my_place()
        for k, peer in enumerate(_peers(x, y, cc)):
            cp = pltpu.make_async_remote_copy(
                src_ref=v_ref, dst_ref=land_ref.at[_slot(peer)], send_sem=send_sems.at[k], recv_sem=recv_sems.at[k],
                device_id=peer, device_id_type=MESH)
            cp.wait_send()
            cp.wait_recv()
        pltpu.make_async_copy(v_ref, land_ref.at[_slot((x, y, cc))], send_sems.at[4]).wait()

    return _call(
        body, name=name,
        out_shape=(pltpu.HBM(v_thru.shape, v_thru.dtype), pltpu.HBM(land_thru.shape, land_thru.dtype)),
        in_specs=(HBM_SPEC, HBM_SPEC, SEM_SPEC, SEM_SPEC, pl.BlockSpec(memory_space=pl.ANY)),
        out_specs=(HBM_SPEC, HBM_SPEC), input_output_aliases={0: 0, 1: 1},
        compiler_params=pltpu.CompilerParams(has_side_effects=DATAFLOW),
    )(v_thru, land_thru, send_sems, recv_sems, after)


def gather_pass(land, name):
    _, r, c = land.shape
    nch, rows = _stream_rows(r)

    def body(land_ref, out_ref, send_sems, recv_sems):
        x, y, cc = _my_place()
        peers = _peers(x, y, cc)
        copies = []
        for j in range(3):
            mine, theirs = _slot(peers[1 + j]), _slot((peers[1 + j][0], peers[1 + j][1], 1 - cc))
            for s in range(nch):
                k = j * nch + s
                win = pl.ds(s * rows, rows)
                send = pltpu.make_async_remote_copy(
                    src_ref=land_ref.at[mine, win], dst_ref=out_ref.at[mine, win], send_sem=send_sems.at[k],
                    recv_sem=recv_sems.at[k], device_id=peers[0], device_id_type=MESH)
                recv = pltpu.make_async_remote_copy(
                    src_ref=land_ref.at[mine, win], dst_ref=out_ref.at[theirs, win], send_sem=send_sems.at[k],
                    recv_sem=recv_sems.at[k], device_id=peers[0], device_id_type=MESH)
                copies.append((send, recv))
        for send, _ in copies:
            send.start()
        for send, recv in copies:
            send.wait_send()
            recv.wait_recv()

    nsem = 3 * nch
    return _call(body, name=name, in_specs=[pl.BlockSpec(memory_space=pl.ANY)],
                 out_specs=pl.BlockSpec(memory_space=pl.ANY), out_shape=_sds(land.shape, land.dtype),
                 input_output_aliases={0: 0},
                 scratch_shapes=[pltpu.SemaphoreType.DMA((nsem,)), pltpu.SemaphoreType.DMA((nsem,))])(land)


def chips_start(p, name):
    _, r, c = p.shape

    def body(p_ref, land_ref, send_sems, recv_sems, p_thru, land_thru, token):
        x, y, cc = _my_place()
        for j, (px, py, pc) in enumerate(_peers(x, y, cc)[1:]):
            pltpu.make_async_remote_copy(
                src_ref=p_ref.at[2 * px + py], dst_ref=land_ref.at[j], send_sem=send_sems.at[j],
                recv_sem=recv_sems.at[j], device_id=(px, py, pc), device_id_type=MESH).start()
        token[...] = jnp.zeros_like(token)

    return _call(
        body, name=name,
        out_shape=(pltpu.SemaphoreType.DMA((3,)), pltpu.SemaphoreType.DMA((3,)), pltpu.HBM(p.shape, p.dtype),
                   pltpu.HBM((3, r, c), p.dtype), _sds((8, 128))),
        in_specs=(HBM_SPEC, HBM_SPEC),
        out_specs=(SEM_SPEC, SEM_SPEC, HBM_SPEC, HBM_SPEC, pl.BlockSpec(memory_space=pltpu.VMEM)),
        input_output_aliases={0: 2, 1: 3},
        compiler_params=pltpu.CompilerParams(has_side_effects=DATAFLOW),
    )(_hbm(p), _hbm(lax.empty((3, r, c), p.dtype)))


def chips_wait(send_sems, recv_sems, p_thru, land_thru, after, name):
    def body(p_ref, land_ref, send_sems, recv_sems, after_ref, p_out, land_out):
        x, y, cc = _my_place()
        for j, (px, py, pc) in enumerate(_peers(x, y, cc)[1:]):
            cp = pltpu.make_async_remote_copy(
                src_ref=p_ref.at[2 * px + py], dst_ref=land_ref.at[j], send_sem=send_sems.at[j],
                recv_sem=recv_sems.at[j], device_id=(px, py, pc), device_id_type=MESH)
            cp.wait_send()
            cp.wait_recv()

    return _call(
        body, name=name,
        out_shape=(pltpu.HBM(p_thru.shape, p_thru.dtype), pltpu.HBM(land_thru.shape, land_thru.dtype)),
        in_specs=(HBM_SPEC, HBM_SPEC, SEM_SPEC, SEM_SPEC, pl.BlockSpec(memory_space=pl.ANY)),
        out_specs=(HBM_SPEC, HBM_SPEC), input_output_aliases={0: 0, 1: 1},
        compiler_params=pltpu.CompilerParams(has_side_effects=DATAFLOW),
    )(p_thru, land_thru, send_sems, recv_sems, after)


def sibling_start(gw, name):
    _, r, c = gw.shape
    nch, rows = _stream_rows(r)
    nsem = 4 * nch

    def body(g_ref, land_ref, send_sems, recv_sems, g_thru, land_thru, token):
        x, y, cc = _my_place()
        for q in range(4):
            for s in range(nch):
                win = pl.ds(s * rows, rows)
                pltpu.make_async_remote_copy(
                    src_ref=g_ref.at[q, 1 - cc, win], dst_ref=land_ref.at[q, win], send_sem=send_sems.at[q * nch + s],
                    recv_sem=recv_sems.at[q * nch + s], device_id=(x, y, 1 - cc), device_id_type=MESH).start()
        token[...] = jnp.zeros_like(token)

    return _call(
        body, name=name,
        out_shape=(pltpu.SemaphoreType.DMA((nsem,)), pltpu.SemaphoreType.DMA((nsem,)),
                   pltpu.HBM((4, 2, r, c), gw.dtype), pltpu.HBM((4, r, c), gw.dtype), _sds((8, 128))),
        in_specs=(HBM_SPEC, HBM_SPEC),
        out_specs=(SEM_SPEC, SEM_SPEC, HBM_SPEC, HBM_SPEC, pl.BlockSpec(memory_space=pltpu.VMEM)),
        input_output_aliases={0: 2, 1: 3},
        compiler_params=pltpu.CompilerParams(has_side_effects=DATAFLOW),
    )(_hbm(gw.reshape(4, 2, r, c)), _hbm(lax.empty((4, r, c), gw.dtype)))


def sibling_wait(send_sems, recv_sems, g_thru, land_thru, after, name):
    _, _, r, c = g_thru.shape
    nch, rows = _stream_rows(r)

    def body(g_ref, land_ref, send_sems, recv_sems, after_ref, g_out, land_out):
        x, y, cc = _my_place()
        for q in range(4):
            for s in range(nch):
                win = pl.ds(s * rows, rows)
                cp = pltpu.make_async_remote_copy(
                    src_ref=g_ref.at[q, 1 - cc, win], dst_ref=land_ref.at[q, win], send_sem=send_sems.at[q * nch + s],
                    recv_sem=recv_sems.at[q * nch + s], device_id=(x, y, 1 - cc), device_id_type=MESH)
                cp.wait_send()
                cp.wait_recv()

    return _call(
        body, name=name,
        out_shape=(pltpu.HBM(g_thru.shape, g_thru.dtype), pltpu.HBM(land_thru.shape, land_thru.dtype)),
        in_specs=(HBM_SPEC, HBM_SPEC, SEM_SPEC, SEM_SPEC, pl.BlockSpec(memory_space=pl.ANY)),
        out_specs=(HBM_SPEC, HBM_SPEC), input_output_aliases={0: 0, 1: 1},
        compiler_params=pltpu.CompilerParams(has_side_effects=DATAFLOW),
    )(g_thru, land_thru, send_sems, recv_sems, after)


def reduce_scatter_begin(gw, name, tag):
    _, r, c = gw.shape
    theirs = exchange_sibling(gw, name + "_sibling")
    chip_sum = add_own_half(gw.reshape(4, 2, r, c), theirs, name + "_add2")
    return chips_start(chip_sum, name + "_start" + tag)


def reduce_scatter_end(started, after, name, tag):
    send_sems, recv_sems, p_thru, land_thru, _ = started
    parts, land = chips_wait(send_sems, recv_sems, p_thru, land_thru, after, name + "_wait" + tag)
    mine = lax.dynamic_index_in_dim(parts, 2 * lax.axis_index("x") + lax.axis_index("y"), axis=0, keepdims=False)
    return add_own_lead(mine, land, name + "_add4")


def _pack(arrs):
    flat = jnp.concatenate([a.reshape(-1).astype(F32) for a in arrs])
    n = flat.shape[0]
    pad = (-n) % PACK_ELEMS
    return jnp.pad(flat, (0, pad)).reshape(-1, 128)


def _unpack(packed, shapes):
    flat = packed.reshape(-1)
    out, off = [], 0
    for s in shapes:
        n = int(np.prod(s))
        out.append(lax.optimization_barrier(flat[off:off + n]).reshape(s))
        off += n
    return out


def _t5_bucket(rel):
    nb = N_BUCKETS // 2
    max_exact = nb // 2
    ret = jnp.where(rel > 0, nb, 0)
    n = jnp.abs(rel)
    nf = jnp.maximum(n, 1).astype(jnp.float32)
    large = max_exact + (jnp.log(nf / max_exact) / math.log(MAX_DISTANCE / max_exact)
                         * (nb - max_exact)).astype(jnp.int32)
    large = jnp.minimum(large, nb - 1)
    return ret + jnp.where(n < max_exact, n, large)


SMALL_SHARDED = ("gla_w2_f", "gla_w2_b", "conv_w", "lru_ba", "lru_bx", "lru_lambda")
SMALL_REPL = ("rel_bias", "attn_sink", "gla_b2_f", "gla_b2_b", "gla_norm", "conv_b", "lru_wa", "lru_wx",
              "norm_mix_pre", "norm_mix_post", "norm_mem", "norm_x_pre", "norm_x_post", "norm_ff_pre", "norm_ff_post")
BIG = ("w_in", "w_out", "xq", "xk", "xv", "xo", "w_up", "w_down")
WEIGHTS = ['rel_bias', 'w_in', 'w_out', 'attn_sink', 'gla_w2_f', 'gla_b2_f', 'gla_w2_b', 'gla_b2_b', 'gla_norm',
           'conv_w', 'conv_b', 'lru_wa', 'lru_ba', 'lru_wx', 'lru_bx', 'lru_lambda', 'xq', 'xk', 'xv', 'xo', 'w_up',
           'w_down', 'norm_mix_pre', 'norm_mix_post', 'norm_mem', 'norm_x_pre', 'norm_x_post', 'norm_ff_pre',
           'norm_ff_post']


def _step(x, mem, loss_target, w, m, v):
    depth = w["w_in"].shape[0]
    t, d = x.shape[1], x.shape[2]
    ml = mem.shape[1]
    rx = d // N_DEV
    rf = w["w_up"].shape[2]
    r_out = D_MIX // N_DEV
    x = x.reshape(t, d)
    mem = mem.reshape(ml, d)
    loss_target = loss_target.reshape(t, d)
    my_idx = 4 * lax.axis_index("x") + 2 * lax.axis_index("y") + lax.axis_index("c")

    off_in = 0
    off_up, off_down, off_out = 0, rf, 2 * rf
    off_xq = off_out + r_out
    off_xk, off_xv, off_xo = off_xq + rx, off_xq + 2 * rx, off_xq + 3 * rx
    r_rest = off_xo + rx

    sh_shapes = [w[n].shape for n in SMALL_SHARDED]
    gathered = all_gather(_pack([w[n] for n in SMALL_SHARDED]), "ag_small")
    per_dev = [_unpack(gathered[j], sh_shapes) for j in range(N_DEV)]
    full = {n: jnp.concatenate([per_dev[j][i] for j in range(N_DEV)], axis=-1) for i, n in enumerate(SMALL_SHARDED)}
    for n in SMALL_REPL:
        full[n] = w[n]

    ag_started = []
    for l in range(depth):
        blk_in = jnp.pad(w["w_in"][l].T, ((0, W_IN_ROWS - W_IN_SHARD), (0, 0))).astype(BF16)
        blk_rest = jnp.concatenate([w["w_up"][l].T, w["w_down"][l], w["w_out"][l], w["xq"][l], w["xk"][l],
                                    w["xv"][l], w["xo"][l]], axis=0).astype(BF16)
        blk_in, _ = lax.optimization_barrier((blk_in, gathered if l == 0 else ag_started[-1][1][4]))
        start_in = gather_start(blk_in, "ag_start_in%d" % l)
        blk_rest, _ = lax.optimization_barrier((blk_rest, start_in[4]))
        ag_started.append((start_in, gather_start(blk_rest, "ag_start_rest%d" % l)))
    gather_token = sum(st[4][0, 0] for pair in ag_started for st in pair)
    gws = [None] * depth

    def gather_finish(started, after, name):
        send_sems, recv_sems, blk_thru, land_thru, _ = started
        _, land = gather_wait(send_sems, recv_sems, blk_thru, land_thru, after, name)
        return gather_pass(land, "ag_pass")

    qi = jnp.arange(BLOCK)[:, None]
    kj = jnp.arange(3 * BLOCK)[None, :]
    onehot_t = (jnp.arange(N_BUCKETS)[:, None] == _t5_bucket(kj - BLOCK - qi).reshape(1, -1)).astype(F32)
    bias = mm_plain(full["rel_bias"].T, onehot_t, "rel_bias_lookup", hi=True, tn=3 * BLOCK * 16)
    bias = bias.reshape(A_HEADS, BLOCK, 3 * BLOCK)
    bias_t = jnp.transpose(bias, (0, 2, 1))

    def sink_rows(sink):
        s = jnp.broadcast_to(sink.reshape(A_KV_HEADS, A_GROUP, 1), (A_KV_HEADS, A_GROUP, 128))
        return jnp.pad(s, ((0, 0), (0, 8 - A_GROUP), (0, 0)))

    bounds = np.concatenate([[0], np.cumsum(SPLIT_SIZES)])

    def split_proj(pp):
        outs = []
        for lo, hi in zip(bounds[:-1], bounds[1:]):
            segs = []
            for j in range(N_DEV):
                a, b = max(lo, j * W_IN_SHARD), min(hi, (j + 1) * W_IN_SHARD)
                if a < b:
                    base = j * W_IN_ROWS - j * W_IN_SHARD
                    segs.append(pp[:, base + a:base + b])
            outs.append(segs[0] if len(segs) == 1 else jnp.concatenate(segs, axis=1))
        return outs

    def join_dproj(pieces):
        zero_cols = jnp.zeros((t, W_IN_ROWS - W_IN_SHARD), F32)
        segs = []
        for j in range(N_DEV):
            for p, lo, hi in zip(pieces, bounds[:-1], bounds[1:]):
                a, b = max(lo, j * W_IN_SHARD), min(hi, (j + 1) * W_IN_SHARD)
                if a < b:
                    segs.append(p[:, a - lo:b - lo])
            segs.append(zero_cols)
        return jnp.concatenate(segs, axis=1).astype(BF16)

    def lead(a):
        return a.reshape(a.shape[0], C_WIDTH // 128, 128)

    saved = []
    h = rms_fwd(x, full["norm_mix_pre"][0] + gather_token, "rms_first")
    for l in range(depth):
        gw_in = gather_finish(ag_started[l][0], x, "ag_wait_in%d" % l)
        sv = {"x": x, "h_in": h}
        proj_pad = mm_wn(h, gw_in, off_in, W_IN_ROWS, "mm_w_in")
        aq, ak, av, bq, bk, bv, bg, zf, zb, cx, cy = split_proj(proj_pad)
        sv.update(aq=aq, ak=ak, av=av, bq=bq, bk=bk, bv=bv, bg=bg, zf=zf, zb=zb, cx=cx, cy=cy)
        sink_b = sink_rows(full["attn_sink"][l])
        oa = attn_fwd(aq, ak, av, bias, sink_b, "attn_fwd")
        la_f, la_b = gla_gates_fwd(zf, zb, full["gla_w2_f"][l], full["gla_b2_f"][l], full["gla_w2_b"][l],
                                   full["gla_b2_b"][l], "gla_gates_fwd")
        bk_t = bk.T
        gla = {}
        for nm, la, bdir in (("f", la_f, False), ("b", la_b, True)):
            la_t = la.T
            u, dec = gla_outer(bk_t, la_t, bv, "gla_outer_k_" + nm, bdir, "khat")
            sp = scan_lead(dec, u, "gla_state_scan_" + nm, reverse=bdir, inclusive=False)
            o_dir = gla_inner_fwd(bq, bk, bv, la, sp, "gla_inner_fwd_" + nm, bdir)
            gla[nm] = dict(la=la, la_t=la_t, dec=dec, sp=sp, o=o_dir)
        ob = gla_out_fwd(gla["f"]["o"], gla["b"]["o"], bg, full["gla_norm"][l], "gla_out_fwd")
        sv["gla"] = gla
        xm2, xm1, xp1 = _shift(cx, -2), _shift(cx, -1), _shift(cx, 1)
        xc, a0, u0, a1, u1 = lru_gates_fwd(cx, xm2, xm1, xp1, full["conv_w"][l], full["conv_b"][l], full["lru_wa"][l],
                                           full["lru_ba"][l], full["lru_wx"][l], full["lru_bx"][l],
                                           full["lru_lambda"][l], "lru_gates_fwd")
        h0 = scan_lead(lead(a0), lead(u0), "lru_scan_fwd", reverse=False, inclusive=True).reshape(t, C_WIDTH)
        h1 = scan_lead(lead(a1), lead(u1), "lru_scan_rev", reverse=True, inclusive=True).reshape(t, C_WIDTH)
        oc = lru_out_fwd(h0, h1, cy, "lru_out_fwd")
        sv.update(xm2=xm2, xm1=xm1, xp1=xp1, xc=xc, a0=a0, a1=a1, h0=h0, h1=h1, oa=oa)
        cat = jnp.concatenate([oa, ob, oc], axis=1).astype(BF16)
        gw = gather_finish(ag_started[l][1], cat, "ag_wait_rest%d" % l)
        gws[l] = (gw_in, gw)
        mixed = mm_wk(cat, gw, off_out, r_out, "mm_w_out")
        x1, h2 = resid_rms(x, mixed, full["norm_mix_post"][l], full["norm_x_pre"][l], "resid_rms")
        sv.update(cat=cat, mixed=mixed, x1=x1, h2=h2)
        memn = rms_fwd(mem, full["norm_mem"][l], "rms_mem")
        q = mm_wk(h2, gw, off_xq, rx, "mm_xq")
        k = mm_wk(memn, gw, off_xk, rx, "mm_xkv")
        vv = mm_wk(memn, gw, off_xv, rx, "mm_xkv")
        ox = xattn_fwd(q, k, vv, "xattn_fwd")
        ca = mm_wk(ox, gw, off_xo, rx, "mm_xo")
        x2, h3 = resid_rms(x1, ca, full["norm_x_post"][l], full["norm_ff_pre"][l], "resid_rms")
        sv.update(memn=memn, q=q, k=k, v=vv, ox=ox, ca=ca, x2=x2, h3=h3)
        up, act = mm_wn(h3, gw, off_up, rf, "mm_w_up", with_relu2=True)
        ff = mm_wk(act, gw, off_down, rf, "mm_w_down", jb=max(1, min(N_DEV, 2048 // rf)))
        if l + 1 < depth:
            x, h = resid_rms(x2, ff, full["norm_ff_post"][l], full["norm_mix_pre"][l + 1], "resid_rms")
        else:
            x = resid_rms(x2, ff, full["norm_ff_post"][l], None, "resid_rms_last")
        sv.update(up=up, act=act, ff=ff)
        saved.append(sv)

    dx, loss_local = loss_and_grad(x, loss_target, "loss")
    loss = lax.psum(loss_local, AXES)

    grads = {n: [None] * depth for n in WEIGHTS if n != "rel_bias"}
    dbias_total = None
    big_grads = [None] * depth
    rs_started = [None] * depth
    rs_token = 0.0
    bf = lambda a: a.astype(BF16)
    for l in reversed(range(depth)):
        gw_in, gw = gws[l]
        sv = saved[l]
        dff, grads["norm_ff_post"][l] = rms_bwd(sv["ff"], full["norm_ff_post"][l] + rs_token, dx, "rms_bwd")
        dup = mm_wn(dff, gw, off_down, rf, "mm_w_down_dx", relu_grad_of=sv["up"], out_dtype=BF16)
        gpack = mm_dw_into(sv["act"], dff, lax.empty((N_DEV, r_rest, d), BF16), off_down, rf, "mm_dw_down")
        gpack = mm_dw_into(dup, sv["h3"], gpack, off_up, rf, "mm_dw_up")
        dh3 = mm_wk(dup, gw, off_up, rf, "mm_w_up_dx", jb=max(1, min(N_DEV, 2048 // rf)))
        dx2, grads["norm_ff_pre"][l] = rms_bwd(sv["x2"], full["norm_ff_pre"][l], dh3, "rms_bwd_add", add=dx)
        dca, grads["norm_x_post"][l] = rms_bwd(sv["ca"], full["norm_x_post"][l], dx2, "rms_bwd")
        dox = mm_wn(dca, gw, off_xo, rx, "mm_x_dx")
        gpack = mm_dw_into(sv["ox"], dca, gpack, off_xo, rx, "mm_dw_xo")
        dq, dk, dv = xattn_bwd(sv["q"], sv["k"], sv["v"], sv["ox"], dox, "xattn_bwd")
        gpack = mm_dw_into(sv["h2"], dq, gpack, off_xq, rx, "mm_dw_xq")
        gpack = mm_dw_into(sv["memn"], dk, gpack, off_xk, rx, "mm_dw_xk")
        gpack = mm_dw_into(sv["memn"], dv, gpack, off_xv, rx, "mm_dw_xv")
        dh2 = mm_wn(dq, gw, off_xq, rx, "mm_x_dx")
        dmem_k = mm_wn(dk, gw, off_xk, rx, "mm_x_dx_mem")
        dmem_v = mm_wn(dv, gw, off_xv, rx, "mm_x_dx_mem")
        _, grads["norm_mem"][l] = rms_bwd(mem, full["norm_mem"][l], dmem_k, "rms_bwd_mem", dy2=dmem_v)
        dx1, grads["norm_x_pre"][l] = rms_bwd(sv["x1"], full["norm_x_pre"][l], dh2, "rms_bwd_add", add=dx2)
        dmixed, grads["norm_mix_post"][l] = rms_bwd(sv["mixed"], full["norm_mix_post"][l], dx1, "rms_bwd")
        dcat = mm_wn(dmixed, gw, off_out, r_out, "mm_w_out_dx")
        gpack = mm_dw_into(sv["cat"], dmixed, gpack, off_out, r_out, "mm_dw_out")
        if l == 0:
            rs_started[l] = [reduce_scatter_begin(gpack, "rs_rest", str(l)), None]
            mix_token = rs_started[l][0][4][0, 0]
        else:
            sib = sibling_start(gpack, "rs_rest_sib_start%d" % l)
            mix_token = sib[4][0, 0]
        doa, dob, doc = dcat[:, :A_Q], dcat[:, A_Q:A_Q + B_V], dcat[:, A_Q + B_V:]
        daq, dak, dav, dbias, dsink = attn_bwd(sv["aq"], sv["ak"], sv["av"], bias, bias_t,
                                               sink_rows(full["attn_sink"][l] + mix_token), doa, sv["oa"], "attn_bwd")
        grads["attn_sink"][l] = dsink[:, :A_GROUP, 0].reshape(A_HEADS)
        dbias_total = dbias if dbias_total is None else add_n([dbias_total, dbias], F32, "add_dbias")
        gf, gb = sv["gla"]["f"], sv["gla"]["b"]
        do_gla, dbg, dgn = gla_out_bwd(gf["o"], gb["o"], sv["bg"], full["gla_norm"][l] + mix_token, dob,
                                       "gla_out_bwd")
        grads["gla_norm"][l] = dgn.reshape(B_V)
        bq_t = sv["bq"].T
        acc = None
        dlas = {}
        for nm, gd, bdir in (("f", gf, False), ("b", gb, True)):
            wq = gla_outer(bq_t, gd["la_t"], do_gla, "gla_outer_q_" + nm, bdir, "qtil")
            gs = scan_lead(gd["dec"], wq, "gla_adj_scan_" + nm, reverse=not bdir, inclusive=False)
            dbq, dbk, dbv, dlas[nm] = gla_inner_bwd(sv["bq"], sv["bk"], sv["bv"], gd["la"], do_gla, gd["sp"], gs,
                                                    gd["dec"], "gla_inner_bwd_" + nm, bdir, add=acc)
            acc = (dbq, dbk, dbv)
        dzf, dzb, dpre_f, dpre_b, db2f, db2b = gla_gates_bwd(
            sv["zf"], sv["zb"], full["gla_w2_f"][l], full["gla_b2_f"][l], full["gla_w2_b"][l], full["gla_b2_b"][l],
            dlas["f"], dlas["b"], "gla_gates_bwd")
        grads["gla_b2_f"][l] = db2f.reshape(B_QK)
        grads["gla_b2_b"][l] = db2b.reshape(B_QK)
        grads["gla_w2_f"][l] = mm_plain(sv["zf"].T, dpre_f, "mm_dw_gate", hi=True)
        grads["gla_w2_b"][l] = mm_plain(sv["zb"].T, dpre_b, "mm_dw_gate", hi=True)
        dh, dcy = lru_out_bwd(sv["h0"], sv["h1"], sv["cy"], doc, "lru_out_bwd")
        g0 = scan_lead(lead(_shift(sv["a0"], 1)), lead(dh), "lru_scan_rev", reverse=True,
                       inclusive=True).reshape(t, C_WIDTH)
        g1 = scan_lead(lead(_shift(sv["a1"], -1)), lead(dh), "lru_scan_fwd", reverse=False,
                       inclusive=True).reshape(t, C_WIDTH)
        dxc, dzr0, dzi0, dzr1, dzi1, dlam, dba, dbx = lru_gates_bwd(
            sv["xc"], g0, _shift(sv["h0"], -1), g1, _shift(sv["h1"], 1), full["lru_wa"][l], full["lru_ba"][l],
            full["lru_wx"][l], full["lru_bx"][l], full["lru_lambda"][l], "lru_gates_bwd")
        xc_t = bf(sv["xc"].T)
        grads["lru_wa"][l] = jnp.stack([blockdiag_dw(xc_t, dzr0, "lru_dw"), blockdiag_dw(xc_t, dzr1, "lru_dw")])
        grads["lru_wx"][l] = jnp.stack([blockdiag_dw(xc_t, dzi0, "lru_dw"), blockdiag_dw(xc_t, dzi1, "lru_dw")])
        grads["lru_lambda"][l], grads["lru_ba"][l], grads["lru_bx"][l] = dlam, dba, dbx
        dcx, dcw, dcb = conv_bwd(dxc, _shift(dxc, 2), _shift(dxc, 1), _shift(dxc, -1), sv["cx"], sv["xm2"],
                                 sv["xm1"], sv["xp1"], full["conv_w"][l], "conv_bwd")
        grads["conv_w"][l] = dcw
        grads["conv_b"][l] = dcb.reshape(C_WIDTH)
        dproj_pad = join_dproj([daq, dak, dav, dbq, dbk, dbv, dbg, dzf, dzb, dcx, dcy])
        g_in_t = mm_plain(dproj_pad, sv["h_in"], "mm_dw_in", ta=True, out_dtype=BF16)
        rs_in = reduce_scatter_begin(g_in_t.reshape(N_DEV, W_IN_ROWS, d), "rs_in", str(l))
        rs_token = rs_in[4][0, 0]
        dh1 = mm_wk(dproj_pad, gw_in, off_in, W_IN_ROWS, "mm_w_in_dx", jb=2)
        dx, grads["norm_mix_pre"][l] = rms_bwd(sv["x"], full["norm_mix_pre"][l] + rs_token, dh1, "rms_bwd_add",
                                               add=dx1)
        if l > 0:
            g5, theirs = sibling_wait(sib[0], sib[1], sib[2], sib[3], dx, "rs_rest_sib_wait%d" % l)
            chip_sum = add_own_half(g5, theirs, "rs_rest_add2")
            rs_started[l] = [chips_start(chip_sum, "rs_rest_start%d" % l), rs_in]
            rs_token = rs_token + rs_started[l][0][4][0, 0]
        else:
            rs_started[l][1] = rs_in

    grad_rel = mm_plain(dbias_total.reshape(A_HEADS, -1), onehot_t, "rel_bias_grad", tb=True, hi=True).T

    small_names = [n for n in WEIGHTS if n not in BIG]
    small_g = {"rel_bias": grad_rel}
    for n in small_names:
        if n != "rel_bias":
            small_g[n] = jnp.stack([g.reshape(full[n].shape[1:]) for g in grads[n]])
    shapes = [small_g[n].shape for n in small_names]
    small_started = gather_start(_pack([small_g[n] for n in small_names]), "ag_start_small_grads")
    for l in range(depth):
        big_grads[l] = (reduce_scatter_end(rs_started[l][1], small_started[4], "rs_in", str(l)),
                        reduce_scatter_end(rs_started[l][0], small_started[4], "rs_rest", str(l)))

    grad_out, delta, new_m, new_v = {}, {}, {}, {}

    def rows(l, off, r):
        return big_grads[l][1][off:off + r]

    big_g = {
        "w_in": jnp.stack([big_grads[l][0][:W_IN_SHARD].T for l in range(depth)]),
        "w_out": jnp.stack([rows(l, off_out, r_out) for l in range(depth)]),
        "xq": jnp.stack([rows(l, off_xq, rx) for l in range(depth)]),
        "xk": jnp.stack([rows(l, off_xk, rx) for l in range(depth)]),
        "xv": jnp.stack([rows(l, off_xv, rx) for l in range(depth)]),
        "xo": jnp.stack([rows(l, off_xo, rx) for l in range(depth)]),
        "w_up": jnp.stack([rows(l, off_up, rf).T for l in range(depth)]),
        "w_down": jnp.stack([rows(l, off_down, rf) for l in range(depth)]),
    }
    for n in BIG:
        grad_out[n] = big_g[n]
        delta[n], new_m[n], new_v[n] = adamw(big_g[n], w[n], m[n], v[n], "adamw_" + n)

    packed = gather_finish(small_started, delta["w_down"], "ag_wait_small_grads")
    summed = sum_lead(packed, tuple(range(N_DEV)), F32, "add8_small")
    small_g = dict(zip(small_names, _unpack(summed, shapes)))
    for n in SMALL_SHARDED:
        wdt = w[n].shape[-1]
        small_g[n] = lax.dynamic_slice_in_dim(small_g[n], my_idx * wdt, wdt, axis=small_g[n].ndim - 1)

    direct = ("lru_wa", "lru_wx")
    packed_names = [n for n in small_names if n not in direct]
    sshapes = [w[n].shape for n in packed_names]
    ds, ms, vs = adamw(_pack([small_g[n] for n in packed_names]), _pack([w[n] for n in packed_names]),
                       _pack([m[n] for n in packed_names]), _pack([v[n] for n in packed_names]), "adamw_small")
    for n, d_, m_, v_ in zip(packed_names, _unpack(ds, sshapes), _unpack(ms, sshapes), _unpack(vs, sshapes)):
        grad_out[n], delta[n], new_m[n], new_v[n] = small_g[n], d_, m_, v_
    for n in direct:
        grad_out[n] = small_g[n]
        delta[n], new_m[n], new_v[n] = adamw(small_g[n], w[n], m[n], v[n], "adamw_lru")

    return (loss, dx.reshape(1, t, d), *[grad_out[n] for n in WEIGHTS], *[delta[n] for n in WEIGHTS],
            *[new_m[n] for n in WEIGHTS], *[new_v[n] for n in WEIGHTS])


def kernel(x, mem, rel_bias, w_in, w_out, attn_sink, gla_w2_f, gla_b2_f, gla_w2_b, gla_b2_b, gla_norm, conv_w, conv_b, lru_wa, lru_ba, lru_wx, lru_bx, lru_lambda, xq, xk, xv, xo, w_up, w_down, norm_mix_pre, norm_mix_post, norm_mem, norm_x_pre, norm_x_post, norm_ff_pre, norm_ff_post, loss_target, m_rel_bias, m_w_in, m_w_out, m_attn_sink, m_gla_w2_f, m_gla_b2_f, m_gla_w2_b, m_gla_b2_b, m_gla_norm, m_conv_w, m_conv_b, m_lru_wa, m_lru_ba, m_lru_wx, m_lru_bx, m_lru_lambda, m_xq, m_xk, m_xv, m_xo, m_w_up, m_w_down, m_norm_mix_pre, m_norm_mix_post, m_norm_mem, m_norm_x_pre, m_norm_x_post, m_norm_ff_pre, m_norm_ff_post, v_rel_bias, v_w_in, v_w_out, v_attn_sink, v_gla_w2_f, v_gla_b2_f, v_gla_w2_b, v_gla_b2_b, v_gla_norm, v_conv_w, v_conv_b, v_lru_wa, v_lru_ba, v_lru_wx, v_lru_bx, v_lru_lambda, v_xq, v_xk, v_xv, v_xo, v_w_up, v_w_down, v_norm_mix_pre, v_norm_mix_post, v_norm_mem, v_norm_x_pre, v_norm_x_post, v_norm_ff_pre, v_norm_ff_post):
    given = dict(locals())
    w = {n: given[n] for n in WEIGHTS}
    m = {n: given["m_" + n] for n in WEIGHTS}
    v = {n: given["v_" + n] for n in WEIGHTS}
    return _step(x, mem, loss_target, w, m, v)
```

```python
import math

import jax
import jax.numpy as jnp
import numpy as np
from jax import lax
from jax.experimental import pallas as pl
from jax.experimental.pallas import tpu as pltpu

F32 = jnp.float32
BF16 = jnp.bfloat16
HI = lax.Precision.HIGHEST
NN = (((1,), (0,)), ((), ()))
NT = (((1,), (1,)), ((), ()))
MESH = pl.DeviceIdType.MESH
AXES = ("x", "y", "c")
N_DEV = 8

A_HEAD_DIM = 128
A_HEADS = 8
A_KV_HEADS = 2
A_GROUP = 4
WINDOW = 128
BLOCK = 128
N_BUCKETS = 32
MAX_DISTANCE = 128
B_HEADS = 4
B_KEY_DIM = 64
B_VAL_DIM = 128
GATE_RANK = 16
GATE_TAU = 16.0
GLA_CHUNK = 16
C_WIDTH = 512
C_BLOCKS = 4
C_BLOCK_DIM = 128
LRU_C = 8.0
X_HEADS = 4
EPS = 1e-6
NEG_INF = -1e30
A_Q = A_HEADS * A_HEAD_DIM
A_KV = A_KV_HEADS * A_HEAD_DIM
B_QK = B_HEADS * B_KEY_DIM
B_V = B_HEADS * B_VAL_DIM
SPLIT_SIZES = (A_Q, A_KV, A_KV, B_QK, B_QK, B_V, B_V, GATE_RANK, GATE_RANK, C_WIDTH, C_WIDTH)
D_IN = sum(SPLIT_SIZES)
D_MIX = A_Q + B_V + C_WIDTH
W_IN_SHARD = D_IN // N_DEV
W_IN_ROWS = 768
GLA_TILE = 128
CHUNKS_PER_TILE = GLA_TILE // GLA_CHUNK
EXP_CLAMP = 80.0

ADAM_LR = 0.001
ADAM_B1 = 0.9
ADAM_B2 = 0.999
ADAM_EPS = 1e-08
ADAM_WD = 0.01
ADAM_STEP = 10

VMEM_LIMIT_BYTES = 52 * 1024 * 1024
MM_TILE = 1024
NORM_TILE_ELEMS = 1 << 19
SCAN_UNROLL = 8
SIBLING_STREAMS = 16
PACK_ELEMS = 128 * 2048


def _call(body, **kw):
    return pl.pallas_call(body, **kw)


def _cparams():
    return pltpu.CompilerParams(vmem_limit_bytes=VMEM_LIMIT_BYTES)


def _dot(a, b, dims=NN, hi=False):
    if hi:
        return lax.dot_general(a, b, dims, precision=HI, preferred_element_type=F32)
    return lax.dot_general(a.astype(BF16), b.astype(BF16), dims, preferred_element_type=F32)


def _sds(shape, dtype=F32):
    return jax.ShapeDtypeStruct(tuple(shape), dtype)


def _row_tile(rows, cols, target_elems=1 << 18):
    want = max(8, target_elems // max(cols, 1))
    if rows <= want:
        return rows
    t = (want // 8) * 8
    while t >= 8:
        if rows % t == 0:
            return t
        t -= 8
    return rows


def _expm1(x):
    poly = x * (1.0 + x * (1.0 / 2 + x * (1.0 / 6 + x * (1.0 / 24 + x * (1.0 / 120 + x * (
        1.0 / 720 + x * (1.0 / 5040 + x * (1.0 / 40320))))))))
    return jnp.where(jnp.abs(x) < 0.3, poly, jnp.exp(x) - 1.0)


def _log1p(e):
    w = 1.0 + e
    return jnp.where(w == 1.0, e, jnp.log(w) * e / (w - 1.0))


def _softplus(x):
    return jnp.maximum(x, 0.0) + _log1p(jnp.exp(-jnp.abs(x)))


def _sigmoid(x):
    return jax.nn.sigmoid(x)


GELU_K = math.sqrt(2.0 / math.pi)


def _gelu(y):
    t = jnp.tanh(GELU_K * (y + 0.044715 * y * y * y))
    return 0.5 * y * (1.0 + t)


def _gelu_grad(y):
    t = jnp.tanh(GELU_K * (y + 0.044715 * y * y * y))
    return 0.5 * (1.0 + t) + 0.5 * y * (1.0 - t * t) * GELU_K * (1.0 + 3 * 0.044715 * y * y)


def rms_fwd(x, g, name):
    m, d = x.shape
    tm = _row_tile(m, d, NORM_TILE_ELEMS)

    def body(x_ref, g_ref, o_ref):
        xv = x_ref[...]
        r = lax.rsqrt(jnp.mean(xv * xv, axis=1, keepdims=True) + EPS)
        o_ref[...] = (xv * r * g_ref[...]).astype(o_ref.dtype)

    return _call(body, name=name, grid=(m // tm,),
                 in_specs=[pl.BlockSpec((tm, d), lambda i: (i, 0)), pl.BlockSpec((1, d), lambda i: (0, 0))],
                 out_specs=pl.BlockSpec((tm, d), lambda i: (i, 0)),
                 out_shape=_sds((m, d), BF16))(x, g.reshape(1, d))


def resid_rms(xres, mid, g_post, g_pre, name):
    m, d = xres.shape
    tm = _row_tile(m, d, NORM_TILE_ELEMS)
    with_pre = g_pre is not None

    def body(*refs):
        if with_pre:
            x_ref, m_ref, gp_ref, gn_ref, xo_ref, h_ref = refs
        else:
            x_ref, m_ref, gp_ref, xo_ref = refs
        mv = m_ref[...]
        r = lax.rsqrt(jnp.mean(mv * mv, axis=1, keepdims=True) + EPS)
        xn = x_ref[...] + mv * r * gp_ref[...]
        xo_ref[...] = xn
        if with_pre:
            r2 = lax.rsqrt(jnp.mean(xn * xn, axis=1, keepdims=True) + EPS)
            h_ref[...] = (xn * r2 * gn_ref[...]).astype(h_ref.dtype)

    row = pl.BlockSpec((tm, d), lambda i: (i, 0))
    vec = pl.BlockSpec((1, d), lambda i: (0, 0))
    ins = [xres, mid, g_post.reshape(1, d)] + ([g_pre.reshape(1, d)] if with_pre else [])
    in_specs = [row, row, vec] + ([vec] if with_pre else [])
    if with_pre:
        return _call(body, name=name, grid=(m // tm,), in_specs=in_specs, out_specs=(row, row),
                     out_shape=(_sds((m, d)), _sds((m, d), BF16)))(*ins)
    return _call(body, name=name, grid=(m // tm,), in_specs=in_specs, out_specs=row,
                 out_shape=_sds((m, d)))(*ins)


def rms_bwd(x, g, dy, name, dy2=None, add=None):
    m, d = x.shape
    tm = _row_tile(m, d, NORM_TILE_ELEMS)
    has2, hasadd = dy2 is not None, add is not None

    def body(*refs):
        it = iter(refs)
        x_ref, g_ref, dy_ref = next(it), next(it), next(it)
        dy2_ref = next(it) if has2 else None
        add_ref = next(it) if hasadd else None
        dx_ref, dg_ref = next(it), next(it)
        xv = x_ref[...]
        dyv = dy_ref[...]
        if has2:
            dyv = dyv + dy2_ref[...]
        r = lax.rsqrt(jnp.mean(xv * xv, axis=1, keepdims=True) + EPS)
        xh = xv * r
        dxh = dyv * g_ref[...]
        dx = r * (dxh - xh * jnp.mean(dxh * xh, axis=1, keepdims=True))
        if hasadd:
            dx = dx + add_ref[...]
        dx_ref[...] = dx
        part = jnp.sum(dyv * xh, axis=0, keepdims=True)

        @pl.when(pl.program_id(0) == 0)
        def _():
            dg_ref[...] = part

        @pl.when(pl.program_id(0) > 0)
        def _():
            dg_ref[...] += part

    row = pl.BlockSpec((tm, d), lambda i: (i, 0))
    vec = pl.BlockSpec((1, d), lambda i: (0, 0))
    ins = [x, g.reshape(1, d), dy] + ([dy2] if has2 else []) + ([add] if hasadd else [])
    in_specs = [row, vec, row] + ([row] if has2 else []) + ([row] if hasadd else [])
    return _call(body, name=name, grid=(m // tm,), in_specs=in_specs, out_specs=(row, vec),
                 out_shape=(_sds((m, d)), _sds((1, d))))(*ins)


def loss_and_grad(y, target, name):
    m, d = y.shape
    tm = _row_tile(m, d, NORM_TILE_ELEMS)

    def body(y_ref, t_ref, dy_ref, l_ref):
        e = y_ref[...] - t_ref[...]
        dy_ref[...] = e * (1.0 / d)
        s = jnp.sum(jnp.sum(e * e, axis=1, keepdims=True), axis=0, keepdims=True) * (0.5 / d)
        part = jnp.broadcast_to(s, (1, 128))

        @pl.when(pl.program_id(0) == 0)
        def _():
            l_ref[...] = part

        @pl.when(pl.program_id(0) > 0)
        def _():
            l_ref[...] += part

    row = pl.BlockSpec((tm, d), lambda i: (i, 0))
    dy, l = _call(body, name=name, grid=(m // tm,), in_specs=[row, row],
                  out_specs=(row, pl.BlockSpec((1, 128), lambda i: (0, 0))),
                  out_shape=(_sds((m, d)), _sds((1, 128))))(y, target)
    return dy, l[0, 0]


def adamw(g, w, m, v, name):
    shape = w.shape
    cols = shape[-1]
    rows = int(np.prod(shape[:-1]))
    tm = _row_tile(rows, cols)
    c1 = 1.0 - ADAM_B1 ** ADAM_STEP
    c2 = 1.0 - ADAM_B2 ** ADAM_STEP

    def body(g_ref, w_ref, m_ref, v_ref, d_ref, mo_ref, vo_ref):
        gv = g_ref[...]
        mn = ADAM_B1 * m_ref[...] + (1.0 - ADAM_B1) * gv
        vn = ADAM_B2 * v_ref[...] + (1.0 - ADAM_B2) * (gv * gv)
        m_hat = mn / c1
        v_hat = vn / c2
        d_ref[...] = -ADAM_LR * (m_hat / (jnp.sqrt(v_hat) + ADAM_EPS) + ADAM_WD * w_ref[...])
        mo_ref[...] = mn
        vo_ref[...] = vn

    row = pl.BlockSpec((tm, cols), lambda i: (i, 0))
    outs = _call(body, name=name, grid=(rows // tm,), in_specs=[row] * 4, out_specs=(row,) * 3,
                 out_shape=(_sds((rows, cols)),) * 3)(*[a.reshape(rows, cols) for a in (g, w, m, v)])
    return tuple(o.reshape(shape) for o in outs)


def sum_lead(x, order, out_dtype, name):
    n, rows, cols = x.shape
    tm = _row_tile(rows, cols)

    def body(x_ref, o_ref):
        acc = x_ref[order[0]].astype(F32)
        for i in order[1:]:
            acc = acc + x_ref[i].astype(F32)
        o_ref[...] = acc.astype(out_dtype)

    return _call(body, name=name, grid=(rows // tm,), in_specs=[pl.BlockSpec((n, tm, cols), lambda i: (0, i, 0))],
                 out_specs=pl.BlockSpec((tm, cols), lambda i: (i, 0)), out_shape=_sds((rows, cols), out_dtype))(x)


def add_own_lead(own, parts, name):
    n, rows, cols = parts.shape
    tm = _row_tile(rows, cols)

    def body(o_ref, p_ref, out_ref):
        acc = o_ref[...].astype(F32)
        for i in range(n):
            acc = acc + p_ref[i].astype(F32)
        out_ref[...] = acc

    row = pl.BlockSpec((tm, cols), lambda i: (i, 0))
    return _call(body, name=name, grid=(rows // tm,),
                 in_specs=[row, pl.BlockSpec((n, tm, cols), lambda i: (0, i, 0))], out_specs=row,
                 out_shape=_sds((rows, cols)))(own, parts)


def add_own_half(g5, theirs, name):
    _, _, r, c = g5.shape
    tm = _row_tile(r, c, 1 << 20)

    def body(cc_ref, g_ref, t_ref, o_ref):
        o_ref[...] = (g_ref[...].astype(F32) + t_ref[...].astype(F32)).astype(o_ref.dtype)

    grid_spec = pltpu.PrefetchScalarGridSpec(
        num_scalar_prefetch=1, grid=(4, r // tm),
        in_specs=[pl.BlockSpec((None, None, tm, c), lambda q, i, cc_ref: (q, cc_ref[0], i, 0)),
                  pl.BlockSpec((None, tm, c), lambda q, i, cc_ref: (q, i, 0))],
        out_specs=pl.BlockSpec((None, tm, c), lambda q, i, cc_ref: (q, i, 0)))
    return _call(body, name=name, grid_spec=grid_spec, out_shape=_sds((4, r, c), BF16))(
        lax.axis_index("c").astype(jnp.int32).reshape(1), g5, theirs)


def add_n(xs, out_dtype, name):
    shape = xs[0].shape
    cols = shape[-1]
    rows = int(np.prod(shape[:-1]))
    tm = _row_tile(rows, cols)
    n = len(xs)

    def body(*refs):
        acc = refs[0][...].astype(F32)
        for r in refs[1:n]:
            acc = acc + r[...].astype(F32)
        refs[n][...] = acc.astype(out_dtype)

    row = pl.BlockSpec((tm, cols), lambda i: (i, 0))
    out = _call(body, name=name, grid=(rows // tm,), in_specs=[row] * n, out_specs=row,
                out_shape=_sds((rows, cols), out_dtype))(*[a.reshape(rows, cols) for a in xs])
    return out.reshape(shape)


def mm_plain(a, b, name, ta=False, tb=False, out_dtype=F32, hi=False, tm=MM_TILE, tn=MM_TILE):
    k, m = a.shape[::1 if ta else -1]
    n = b.shape[0] if tb else b.shape[1]
    tm, tn = min(tm, m), min(tn, n)
    dims = (((0 if ta else 1,), (1 if tb else 0,)), ((), ()))

    def body(a_ref, b_ref, o_ref):
        o_ref[...] = _dot(a_ref[...], b_ref[...], dims, hi).astype(out_dtype)

    a_spec = pl.BlockSpec((k, tm), lambda j, i: (0, i)) if ta else pl.BlockSpec((tm, k), lambda j, i: (i, 0))
    b_spec = pl.BlockSpec((tn, k), lambda j, i: (j, 0)) if tb else pl.BlockSpec((k, tn), lambda j, i: (0, j))
    return _call(body, name=name, grid=(n // tn, m // tm), in_specs=[a_spec, b_spec],
                 out_specs=pl.BlockSpec((tm, tn), lambda j, i: (i, j)),
                 out_shape=_sds((m, n), out_dtype), compiler_params=_cparams())(a, b)


def mm_dw_into(a, b, buf, off, r, name, tn=MM_TILE):
    k, m = a.shape
    n = b.shape[1]
    tm, tn = min(MM_TILE, r), min(tn, n)
    assert m == N_DEV * r and off % tm == 0 and r % tm == 0
    per = r // tm
    dims = (((0,), (0,)), ((), ()))

    def body(a_ref, b_ref, buf_ref, o_ref):
        o_ref[...] = _dot(a_ref[...], b_ref[...], dims).astype(o_ref.dtype)

    return _call(body, name=name, grid=(n // tn, m // tm),
                 in_specs=[pl.BlockSpec((k, tm), lambda j, i: (0, i)), pl.BlockSpec((k, tn), lambda j, i: (0, j)),
                           pl.BlockSpec(memory_space=pl.ANY)],
                 out_specs=pl.BlockSpec((None, tm, tn), lambda j, i: (i // per, off // tm + i % per, j)),
                 out_shape=_sds(buf.shape, buf.dtype), input_output_aliases={2: 0},
                 compiler_params=_cparams())(a, b, buf)


def mm_wk(a, gw, off, r, name, jb=N_DEV, tm=MM_TILE, tn=MM_TILE):
    m = a.shape[0]
    d = gw.shape[2]
    tm, tn = min(tm, m), min(tn, d)
    nk = N_DEV // jb
    ob = off // r
    assert off % r == 0 and a.shape[1] == N_DEV * r

    def body(a_ref, b_ref, o_ref, *acc):
        av = a_ref[...].astype(BF16)
        p = _dot(av[:, 0:r], b_ref[0])
        for q in range(1, jb):
            p = p + _dot(av[:, q * r:(q + 1) * r], b_ref[q])
        if nk == 1:
            o_ref[...] = p
        else:
            kk = pl.program_id(2)

            @pl.when(kk == 0)
            def _():
                acc[0][...] = p

            @pl.when(kk > 0)
            def _():
                acc[0][...] += p

            @pl.when(kk == nk - 1)
            def _():
                o_ref[...] = acc[0][...]

    return _call(body, name=name, grid=(m // tm, d // tn, nk),
                 in_specs=[pl.BlockSpec((tm, jb * r), lambda i, j, k: (i, k)),
                           pl.BlockSpec((jb, r, tn), lambda i, j, k: (k, ob, j))],
                 out_specs=pl.BlockSpec((tm, tn), lambda i, j, k: (i, j)),
                 out_shape=_sds((m, d)),
                 scratch_shapes=([pltpu.VMEM((tm, tn), F32)] if nk > 1 else []),
                 compiler_params=_cparams())(a, gw)


def mm_wn(a, gw, off, r, name, relu_grad_of=None, out_dtype=F32, with_relu2=False, tm=MM_TILE):
    m, d = a.shape
    tm = min(tm, m)
    ob = off // r
    assert off % r == 0 and gw.shape[2] == d
    epi = relu_grad_of is not None

    def body(*refs):
        it = iter(refs)
        a_ref, b_ref = next(it), next(it)
        e_ref = next(it) if epi else None
        o_ref = next(it)
        p = _dot(a_ref[...], b_ref[...], NT)
        if epi:
            p = p * (2.0 * jnp.maximum(e_ref[...], 0.0))
        o_ref[...] = p.astype(out_dtype)
        if with_relu2:
            act_ref = next(it)
            act_ref[...] = jnp.square(jnp.maximum(p, 0.0)).astype(act_ref.dtype)

    blk = pl.BlockSpec((tm, r), lambda i, j: (i, j))
    in_specs = [pl.BlockSpec((tm, d), lambda i, j: (i, 0)), pl.BlockSpec((None, r, d), lambda i, j: (j, ob, 0))]
    ins = [a, gw]
    if epi:
        in_specs.append(blk)
        ins.append(relu_grad_of)
    out_shape = _sds((m, N_DEV * r), out_dtype)
    if with_relu2:
        return _call(body, name=name, grid=(m // tm, N_DEV), in_specs=in_specs, out_specs=(blk, blk),
                     out_shape=(out_shape, _sds((m, N_DEV * r), BF16)), compiler_params=_cparams())(*ins)
    return _call(body, name=name, grid=(m // tm, N_DEV), in_specs=in_specs, out_specs=blk,
                 out_shape=out_shape, compiler_params=_cparams())(*ins)


def blockdiag_dw(xt, dz, name):
    t = xt.shape[1]

    def body(a_ref, b_ref, o_ref):
        o_ref[...] = _dot(a_ref[...], b_ref[...])

    return _call(body, name=name, grid=(C_BLOCKS,),
                 in_specs=[pl.BlockSpec((C_BLOCK_DIM, t), lambda g: (g, 0)),
                           pl.BlockSpec((t, C_BLOCK_DIM), lambda g: (0, g))],
                 out_specs=pl.BlockSpec((None, C_BLOCK_DIM, C_BLOCK_DIM), lambda g: (g, 0, 0)),
                 out_shape=_sds((C_BLOCKS, C_BLOCK_DIM, C_BLOCK_DIM)))(xt, dz)


def _band_mask(n, nblk, transposed):
    shape = (3 * BLOCK, A_GROUP * BLOCK) if transposed else (A_GROUP * BLOCK, 3 * BLOCK)
    qi = lax.broadcasted_iota(jnp.int32, shape, 1 if transposed else 0) & (BLOCK - 1)
    kj = lax.broadcasted_iota(jnp.int32, shape, 0 if transposed else 1)
    lo = jnp.where(n > 0, 0, BLOCK)
    hi = jnp.where(n < nblk - 1, 3 * BLOCK, 2 * BLOCK)
    return (jnp.abs(kj - BLOCK - qi) <= WINDOW) & (kj >= lo) & (kj < hi)


def _band_rows(ref, n, nblk):
    starts = [jnp.maximum(n - 1, 0), n, jnp.minimum(n + 1, nblk - 1)]
    return jnp.concatenate([ref[pl.ds(pl.multiple_of(s * BLOCK, BLOCK), BLOCK), :] for s in starts], axis=0)


def _head_cols(j):
    return slice(j * A_HEAD_DIM, (j + 1) * A_HEAD_DIM)


def attn_fwd(q, k, v, bias, sink_b, name):
    t = q.shape[0]
    nblk = t // BLOCK
    scale = A_HEAD_DIM ** -0.5

    def body(q_ref, k_ref, v_ref, b_ref, s_ref, o_ref):
        n = pl.program_id(1)
        kb = _band_rows(k_ref, n, nblk).astype(BF16)
        vb = _band_rows(v_ref, n, nblk).astype(BF16)
        mask = _band_mask(n, nblk, False)
        q4 = jnp.concatenate([q_ref[:, _head_cols(j)] for j in range(A_GROUP)], axis=0)
        b4 = jnp.concatenate([b_ref[j] for j in range(A_GROUP)], axis=0)
        sk = jnp.concatenate([jnp.broadcast_to(s_ref[j:j + 1, 0:1], (BLOCK, 1)) for j in range(A_GROUP)], axis=0)
        s = jnp.where(mask, _dot(q4, kb, NT) * scale + b4, NEG_INF)
        mx = jnp.maximum(jnp.max(s, axis=1, keepdims=True), sk)
        p = jnp.exp(s - mx)
        den = jnp.sum(p, axis=1, keepdims=True) + jnp.exp(sk - mx)
        o4 = _dot(p * (1.0 / den), vb)
        for j in range(A_GROUP):
            o_ref[:, _head_cols(j)] = o4[j * BLOCK:(j + 1) * BLOCK, :]

    gw = A_GROUP * A_HEAD_DIM
    return _call(body, name=name, grid=(A_KV_HEADS, nblk),
                 in_specs=[pl.BlockSpec((BLOCK, gw), lambda g, n: (n, g)),
                           pl.BlockSpec((t, A_HEAD_DIM), lambda g, n: (0, g)),
                           pl.BlockSpec((t, A_HEAD_DIM), lambda g, n: (0, g)),
                           pl.BlockSpec((A_GROUP, BLOCK, 3 * BLOCK), lambda g, n: (g, 0, 0)),
                           pl.BlockSpec((None, 8, 128), lambda g, n: (g, 0, 0))],
                 out_specs=pl.BlockSpec((BLOCK, gw), lambda g, n: (n, g)),
                 out_shape=_sds((t, A_Q)))(q, k, v, bias, sink_b)


def attn_bwd(q, k, v, bias, bias_t, sink_b, do, o, name):
    t = q.shape[0]
    nblk = t // BLOCK
    scale = A_HEAD_DIM ** -0.5

    def body(q_ref, k_ref, v_ref, b_ref, bt_ref, s_ref, do_ref, o_ref, dq_ref, dk_ref, dv_ref, db_ref, ds_ref):
        n = pl.program_id(1)

        @pl.when(n == 0)
        def _():
            dk_ref[...] = jnp.zeros_like(dk_ref)
            dv_ref[...] = jnp.zeros_like(dv_ref)
            db_ref[...] = jnp.zeros_like(db_ref)
            ds_ref[...] = jnp.zeros_like(ds_ref)

        kb = _band_rows(k_ref, n, nblk).astype(BF16)
        vb = _band_rows(v_ref, n, nblk).astype(BF16)
        heads = range(A_GROUP)
        mask = _band_mask(n, nblk, False)
        mask_t = _band_mask(n, nblk, True)
        q4 = jnp.concatenate([q_ref[:, _head_cols(j)] for j in heads], axis=0).astype(BF16)
        do4 = jnp.concatenate([do_ref[:, _head_cols(j)] for j in heads], axis=0)
        doo = do4 * jnp.concatenate([o_ref[:, _head_cols(j)] for j in heads], axis=0)
        do4 = do4.astype(BF16)
        b4 = jnp.concatenate([b_ref[j] for j in heads], axis=0)
        bt4 = jnp.concatenate([bt_ref[j] for j in heads], axis=1)
        sk = jnp.concatenate([jnp.broadcast_to(s_ref[j:j + 1, 0:1], (BLOCK, 1)) for j in heads], axis=0)
        sk_t = jnp.concatenate([jnp.broadcast_to(s_ref[j:j + 1, 0:1], (1, BLOCK)) for j in heads], axis=1)
        s = jnp.where(mask, _dot(q4, kb, NT) * scale + b4, NEG_INF)
        mx = jnp.maximum(jnp.max(s, axis=1, keepdims=True), sk)
        p = jnp.exp(s - mx)
        den = jnp.sum(p, axis=1, keepdims=True) + jnp.exp(sk - mx)
        rden = 1.0 / den
        p = p * rden
        psink_delta = jnp.exp(sk - mx) * rden * jnp.sum(doo, axis=1, keepdims=True)
        dsc = p * (_dot(do4, vb, NT) - jnp.sum(doo, axis=1, keepdims=True))
        dq4 = _dot(dsc, kb) * scale
        for j in heads:
            rows = slice(j * BLOCK, (j + 1) * BLOCK)
            db_ref[j] += dsc[rows, :]
            ds_ref[j:j + 1, :] += jnp.broadcast_to(-jnp.sum(psink_delta[rows, :], axis=0, keepdims=True), (1, 128))
            dq_ref[:, _head_cols(j)] = dq4[rows, :]
        st = jnp.where(mask_t, _dot(kb, q4, NT) * scale + bt4, NEG_INF)
        mxt = jnp.maximum(jnp.max(st, axis=0, keepdims=True), sk_t)
        pt = jnp.exp(st - mxt)
        dent = jnp.sum(pt, axis=0, keepdims=True) + jnp.exp(sk_t - mxt)
        pt = pt * (1.0 / dent)
        delta_t = _dot(jnp.ones((8, A_HEAD_DIM), F32), doo, NT, hi=True)[0:1, :]
        dst = pt * (_dot(vb, do4, NT) - delta_t)
        dkb = _dot(dst, q4) * scale
        dvb = _dot(pt, do4)
        starts = [jnp.maximum(n - 1, 0), n, jnp.minimum(n + 1, nblk - 1)]
        for c, st_ in enumerate(starts):
            rows = pl.ds(pl.multiple_of(st_ * BLOCK, BLOCK), BLOCK)
            dk_ref[rows, :] += dkb[c * BLOCK:(c + 1) * BLOCK, :]
            dv_ref[rows, :] += dvb[c * BLOCK:(c + 1) * BLOCK, :]

    gw = A_GROUP * A_HEAD_DIM
    qspec = pl.BlockSpec((BLOCK, gw), lambda g, n: (n, g))
    kspec = pl.BlockSpec((t, A_HEAD_DIM), lambda g, n: (0, g))
    sspec = pl.BlockSpec((None, 8, 128), lambda g, n: (g, 0, 0))
    bspec = pl.BlockSpec((A_GROUP, BLOCK, 3 * BLOCK), lambda g, n: (g, 0, 0))
    btspec = pl.BlockSpec((A_GROUP, 3 * BLOCK, BLOCK), lambda g, n: (g, 0, 0))
    return _call(body, name=name, grid=(A_KV_HEADS, nblk),
                 in_specs=[qspec, kspec, kspec, bspec, btspec, sspec, qspec, qspec],
                 out_specs=(qspec, kspec, kspec, bspec, sspec),
                 out_shape=(_sds((t, A_Q)), _sds((t, A_KV)), _sds((t, A_KV)),
                            _sds((A_HEADS, BLOCK, 3 * BLOCK)), _sds((A_KV_HEADS, 8, 128))),
                 compiler_params=_cparams())(q, k, v, bias, bias_t, sink_b, do, o)


def xattn_fwd(q, k, v, name):
    t, d = q.shape
    ml = k.shape[0]
    dh = d // X_HEADS
    tq = min(512, t)
    scale = dh ** -0.5

    def body(q_ref, k_ref, v_ref, o_ref):
        s = _dot(q_ref[...], k_ref[...], NT) * scale
        p = jnp.exp(s - jnp.max(s, axis=1, keepdims=True))
        p = p * (1.0 / jnp.sum(p, axis=1, keepdims=True))
        o_ref[...] = _dot(p, v_ref[...])

    qspec = pl.BlockSpec((tq, dh), lambda h, i: (i, h))
    kspec = pl.BlockSpec((ml, dh), lambda h, i: (0, h))
    return _call(body, name=name, grid=(X_HEADS, t // tq), in_specs=[qspec, kspec, kspec], out_specs=qspec,
                 out_shape=_sds((t, d)))(q, k, v)


def xattn_bwd(q, k, v, o, do, name):
    t, d = q.shape
    ml = k.shape[0]
    dh = d // X_HEADS
    tq = min(512, t)
    scale = dh ** -0.5

    def body(q_ref, k_ref, v_ref, o_ref, do_ref, dq_ref, dk_ref, dv_ref):
        i = pl.program_id(1)
        qv, kv, vv = q_ref[...].astype(BF16), k_ref[...].astype(BF16), v_ref[...].astype(BF16)
        dov = do_ref[...]
        doo = dov * o_ref[...]
        dov = dov.astype(BF16)
        s = _dot(qv, kv, NT) * scale
        p = jnp.exp(s - jnp.max(s, axis=1, keepdims=True))
        p = p * (1.0 / jnp.sum(p, axis=1, keepdims=True))
        ds = p * (_dot(dov, vv, NT) - jnp.sum(doo, axis=1, keepdims=True))
        dq_ref[...] = _dot(ds, kv) * scale
        st = _dot(kv, qv, NT) * scale
        pt = jnp.exp(st - jnp.max(st, axis=0, keepdims=True))
        pt = pt * (1.0 / jnp.sum(pt, axis=0, keepdims=True))
        delta_t = _dot(jnp.ones((8, dh), F32), doo, NT, hi=True)[0:1, :]
        dst = pt * (_dot(vv, dov, NT) - delta_t)
        dkp = _dot(dst, qv) * scale
        dvp = _dot(pt, dov)

        @pl.when(i == 0)
        def _():
            dk_ref[...] = dkp
            dv_ref[...] = dvp

        @pl.when(i > 0)
        def _():
            dk_ref[...] += dkp
            dv_ref[...] += dvp

    qspec = pl.BlockSpec((tq, dh), lambda h, i: (i, h))
    kspec = pl.BlockSpec((ml, dh), lambda h, i: (0, h))
    return _call(body, name=name, grid=(X_HEADS, t // tq), in_specs=[qspec, kspec, kspec, qspec, qspec],
                 out_specs=(qspec, kspec, kspec),
                 out_shape=(_sds((t, d)), _sds((ml, d)), _sds((ml, d))))(q, k, v, o, do)


def scan_lead(a, u, name, reverse, inclusive):
    n, r, c = a.shape
    blk = max(1, min(n, (1 << 18) // (max(r, 8) * c)))
    while n % blk:
        blk -= 1
    nb = n // blk

    def body(a_ref, u_ref, o_ref, carry):
        @pl.when(pl.program_id(0) == 0)
        def _():
            carry[...] = jnp.zeros_like(carry)

        def step(s, h):
            idx = (blk - 1 - s) if reverse else s
            hn = a_ref[idx] * h + u_ref[idx]
            o_ref[idx] = hn if inclusive else h
            return hn

        carry[...] = lax.fori_loop(0, blk, step, carry[...], unroll=min(blk, SCAN_UNROLL))

    spec = pl.BlockSpec((blk, r, c), (lambda i: (nb - 1 - i, 0, 0)) if reverse else (lambda i: (i, 0, 0)))
    return _call(body, name=name, grid=(nb,), in_specs=[spec, spec], out_specs=spec,
                 out_shape=_sds((n, r, c)), scratch_shapes=[pltpu.VMEM((r, c), F32)])(a, u)


def _chunk_mats(bwd_dir):
    i = lax.broadcasted_iota(jnp.int32, (GLA_TILE, GLA_TILE), 0)
    j = lax.broadcasted_iota(jnp.int32, (GLA_TILE, GLA_TILE), 1)
    same = lax.shift_right_logical(i, 4) == lax.shift_right_logical(j, 4)
    if bwd_dir:
        cm, cm_t = same & (j >= i), same & (i >= j)
        mk, mk_t = same & (j > i), same & (i > j)
    else:
        cm, cm_t = same & (j <= i), same & (i <= j)
        mk, mk_t = same & (j <= i), same & (i <= j)
    f = lambda b: jnp.where(b, 1.0, 0.0).astype(F32)
    return f(cm), f(cm_t), mk, mk_t, f(same)


def gla_gates_fwd(zf, zb, w2f, b2f, w2b, b2b, name):
    t = zf.shape[0]
    tm = min(256, t)

    def body(zf_ref, zb_ref, wf_ref, bf_ref, wb_ref, bb_ref, lf_ref, lb_ref):
        lf_ref[...] = -_softplus(-(_dot(zf_ref[...], wf_ref[...], hi=True) + bf_ref[...])) / GATE_TAU
        lb_ref[...] = -_softplus(-(_dot(zb_ref[...], wb_ref[...], hi=True) + bb_ref[...])) / GATE_TAU

    zs = pl.BlockSpec((tm, GATE_RANK), lambda i: (i, 0))
    ws = pl.BlockSpec((GATE_RANK, B_QK), lambda i: (0, 0))
    bs = pl.BlockSpec((1, B_QK), lambda i: (0, 0))
    os_ = pl.BlockSpec((tm, B_QK), lambda i: (i, 0))
    return _call(body, name=name, grid=(t // tm,), in_specs=[zs, zs, ws, bs, ws, bs], out_specs=(os_, os_),
                 out_shape=(_sds((t, B_QK)),) * 2)(zf, zb, w2f, b2f.reshape(1, B_QK), w2b, b2b.reshape(1, B_QK))


def gla_gates_bwd(zf, zb, w2f, b2f, w2b, b2b, dlf, dlb, name):
    t = zf.shape[0]
    tm = min(256, t)

    def body(zf_ref, zb_ref, wf_ref, bf_ref, wb_ref, bb_ref, dlf_ref, dlb_ref,
             dzf_ref, dzb_ref, dpf_ref, dpb_ref, dbf_ref, dbb_ref):
        first = pl.program_id(0) == 0
        for z_ref, w_ref, b_ref, dl_ref, dz_ref, dp_ref, db_ref in (
                (zf_ref, wf_ref, bf_ref, dlf_ref, dzf_ref, dpf_ref, dbf_ref),
                (zb_ref, wb_ref, bb_ref, dlb_ref, dzb_ref, dpb_ref, dbb_ref)):
            pre = _dot(z_ref[...], w_ref[...], hi=True) + b_ref[...]
            dpre = dl_ref[...] * (1.0 / GATE_TAU) * _sigmoid(-pre)
            dp_ref[...] = dpre
            dz_ref[...] = _dot(dpre, w_ref[...], NT, hi=True)
            part = jnp.sum(dpre, axis=0, keepdims=True)

            @pl.when(first)
            def _():
                db_ref[...] = part

            @pl.when(jnp.logical_not(first))
            def _():
                db_ref[...] += part

    zs = pl.BlockSpec((tm, GATE_RANK), lambda i: (i, 0))
    ws = pl.BlockSpec((GATE_RANK, B_QK), lambda i: (0, 0))
    bs = pl.BlockSpec((1, B_QK), lambda i: (0, 0))
    os_ = pl.BlockSpec((tm, B_QK), lambda i: (i, 0))
    return _call(body, name=name, grid=(t // tm,), in_specs=[zs, zs, ws, bs, ws, bs, os_, os_],
                 out_specs=(zs, zs, os_, os_, bs, bs),
                 out_shape=(_sds((t, GATE_RANK)),) * 2 + (_sds((t, B_QK)),) * 2 + (_sds((1, B_QK)),) * 2)(
        zf, zb, w2f, b2f.reshape(1, B_QK), w2b, b2b.reshape(1, B_QK), dlf, dlb)


def gla_outer(xt, lat, y, name, bwd_dir, mode):
    t = y.shape[0]
    nchunk = t // GLA_CHUNK
    khat = mode == "khat"
    scale = B_KEY_DIM ** -0.5

    def body(xt_ref, lat_ref, y_ref, *outs):
        _, cm_t, _, _, same = _chunk_mats(bwd_dir)
        lat_v = lat_ref[...]
        bt = _dot(lat_v, cm_t, hi=True)
        if khat:
            mult = jnp.exp(_dot(lat_v, same, hi=True) - bt)
        else:
            mult = jnp.exp(bt) * scale
        xm = xt_ref[...] * mult
        lane = lax.shift_right_logical(lax.broadcasted_iota(jnp.int32, (1, GLA_TILE), 1), 4)
        ones = jnp.ones((GLA_TILE, B_VAL_DIM), F32)
        yv = [y_ref[:, h * B_VAL_DIM:(h + 1) * B_VAL_DIM].astype(BF16) for h in range(B_HEADS)]
        for c in range(CHUNKS_PER_TILE):
            sel = lane == c
            xc = jnp.where(sel, xm, 0.0).astype(BF16)
            for h in range(B_HEADS):
                rows = slice(h * B_KEY_DIM, (h + 1) * B_KEY_DIM)
                outs[0][c, rows, :] = _dot(xc[rows, :], yv[h])
            if khat:
                outs[1][c] = jnp.exp(_dot(jnp.where(sel, lat_v, 0.0), ones, hi=True))

    tspec = pl.BlockSpec((B_QK, GLA_TILE), lambda i: (0, i))
    ospec = pl.BlockSpec((CHUNKS_PER_TILE, B_QK, B_VAL_DIM), lambda i: (i, 0, 0))
    oshape = _sds((nchunk, B_QK, B_VAL_DIM))
    return _call(body, name=name, grid=(t // GLA_TILE,),
                 in_specs=[tspec, tspec, pl.BlockSpec((GLA_TILE, B_V), lambda i: (i, 0))],
                 out_specs=(ospec, ospec) if khat else ospec,
                 out_shape=(oshape, oshape) if khat else oshape)(xt, lat, y)


def _head_lane_mask(h):
    lane = lax.broadcasted_iota(jnp.int32, (1, B_QK), 1)
    return lax.shift_right_logical(lane, 6) == h


def _chunk_rows(c):
    return slice(c * GLA_CHUNK, (c + 1) * GLA_CHUNK)


def gla_inner_fwd(q, k, v, la, sp, name, bwd_dir):
    t = q.shape[0]
    scale = B_KEY_DIM ** -0.5

    def body(q_ref, k_ref, v_ref, la_ref, sp_ref, o_ref):
        cm, _, mk, _, _ = _chunk_mats(bwd_dir)
        b = _dot(cm, la_ref[...], hi=True)
        qt = q_ref[...] * scale * jnp.exp(b)
        kt = k_ref[...] * jnp.exp(jnp.minimum(-b, EXP_CLAMP))
        spb = [sp_ref[c].astype(BF16) for c in range(CHUNKS_PER_TILE)]
        for h in range(B_HEADS):
            lm = _head_lane_mask(h)
            qm = jnp.where(lm, qt, 0.0).astype(BF16)
            km = jnp.where(lm, kt, 0.0).astype(BF16)
            vs = slice(h * B_VAL_DIM, (h + 1) * B_VAL_DIM)
            att = jnp.where(mk, _dot(qm, km, NT), 0.0)
            inter = jnp.concatenate([_dot(qm[_chunk_rows(c), :], spb[c]) for c in range(CHUNKS_PER_TILE)], axis=0)
            o_ref[:, vs] = _dot(att, v_ref[:, vs]) + inter

    qs = pl.BlockSpec((GLA_TILE, B_QK), lambda i: (i, 0))
    vs_ = pl.BlockSpec((GLA_TILE, B_V), lambda i: (i, 0))
    ss = pl.BlockSpec((CHUNKS_PER_TILE, B_QK, B_VAL_DIM), lambda i: (i, 0, 0))
    return _call(body, name=name, grid=(t // GLA_TILE,), in_specs=[qs, qs, vs_, qs, ss], out_specs=vs_,
                 out_shape=_sds((t, B_V)))(q, k, v, la, sp)


def gla_inner_bwd(q, k, v, la, do, sp, gs, dec, name, bwd_dir, add=None):
    t = q.shape[0]
    scale = B_KEY_DIM ** -0.5
    hasadd = add is not None

    def body(*refs):
        it = iter(refs)
        q_ref, k_ref, v_ref, la_ref, do_ref, sp_ref, gs_ref, dec_ref = [next(it) for _ in range(8)]
        adds = [next(it) for _ in range(3)] if hasadd else None
        dq_ref, dk_ref, dv_ref, dla_ref = [next(it) for _ in range(4)]
        cm, cm_t, mk, mk_t, same = _chunk_mats(bwd_dir)
        la_v = la_ref[...]
        b = _dot(cm, la_v, hi=True)
        btot = _dot(same, la_v, hi=True)
        eb = jnp.exp(b)
        ek = jnp.exp(jnp.minimum(-b, EXP_CLAMP))
        ekh = jnp.exp(btot - b)
        qt = q_ref[...] * scale * eb
        kt = k_ref[...] * ek
        kh = k_ref[...] * ekh
        spb = [sp_ref[c].astype(BF16) for c in range(CHUNKS_PER_TILE)]
        gsb = [gs_ref[c].astype(BF16) for c in range(CHUNKS_PER_TILE)]
        dqt = jnp.zeros((GLA_TILE, B_QK), F32)
        dkt = jnp.zeros((GLA_TILE, B_QK), F32)
        dkh = jnp.zeros((GLA_TILE, B_QK), F32)
        for h in range(B_HEADS):
            lm = _head_lane_mask(h)
            qm = jnp.where(lm, qt, 0.0).astype(BF16)
            km = jnp.where(lm, kt, 0.0).astype(BF16)
            khm = jnp.where(lm, kh, 0.0).astype(BF16)
            vs = slice(h * B_VAL_DIM, (h + 1) * B_VAL_DIM)
            vh = v_ref[:, vs].astype(BF16)
            doh = do_ref[:, vs].astype(BF16)
            da = jnp.where(mk, _dot(doh, vh, NT), 0.0)
            da_t = jnp.where(mk_t, _dot(vh, doh, NT), 0.0)
            att_t = jnp.where(mk_t, _dot(km, qm, NT), 0.0)
            dv_h = _dot(att_t, doh) + jnp.concatenate(
                [_dot(khm[_chunk_rows(c), :], gsb[c]) for c in range(CHUNKS_PER_TILE)], axis=0)
            if hasadd:
                dv_h = dv_h + adds[2][:, vs]
            dv_ref[:, vs] = dv_h
            dq_inter = jnp.concatenate(
                [_dot(doh[_chunk_rows(c), :], spb[c], NT) for c in range(CHUNKS_PER_TILE)], axis=0)
            dqt = dqt + _dot(da, km) + jnp.where(lm, dq_inter, 0.0)
            dkt = dkt + _dot(da_t, qm)
            dkh_inter = jnp.concatenate(
                [_dot(vh[_chunk_rows(c), :], gsb[c], NT) for c in range(CHUNKS_PER_TILE)], axis=0)
            dkh = dkh + jnp.where(lm, dkh_inter, 0.0)
        dq = dqt * scale * eb
        dk = dkt * ek + dkh * ekh
        if hasadd:
            dq = dq + adds[0][...]
            dk = dk + adds[1][...]
        dq_ref[...] = dq
        dk_ref[...] = dk
        db = dqt * qt - dkt * kt - dkh * kh
        ones16 = jnp.ones((GLA_CHUNK, B_VAL_DIM), F32)
        t2 = jnp.concatenate(
            [_dot(ones16, gs_ref[c] * dec_ref[c] * sp_ref[c], NT, hi=True) for c in range(CHUNKS_PER_TILE)], axis=0)
        dla_ref[...] = _dot(cm_t, db, hi=True) + _dot(same, dkh * kh, hi=True) + t2

    qs = pl.BlockSpec((GLA_TILE, B_QK), lambda i: (i, 0))
    vs_ = pl.BlockSpec((GLA_TILE, B_V), lambda i: (i, 0))
    ss = pl.BlockSpec((CHUNKS_PER_TILE, B_QK, B_VAL_DIM), lambda i: (i, 0, 0))
    ins = [q, k, v, la, do, sp, gs, dec] + (list(add) if hasadd else [])
    in_specs = [qs, qs, vs_, qs, vs_, ss, ss, ss] + ([qs, qs, vs_] if hasadd else [])
    return _call(body, name=name, grid=(t // GLA_TILE,), in_specs=in_specs, out_specs=(qs, qs, vs_, qs),
                 out_shape=(_sds((t, B_QK)), _sds((t, B_QK)), _sds((t, B_V)), _sds((t, B_QK))),
                 compiler_params=_cparams())(*ins)


def gla_out_fwd(of, ob, g, gn, name):
    t = of.shape[0]
    tm = min(256, t)

    def body(of_ref, ob_ref, g_ref, gn_ref, o_ref):
        for h in range(B_HEADS):
            vs = slice(h * B_VAL_DIM, (h + 1) * B_VAL_DIM)
            o = of_ref[:, vs] + ob_ref[:, vs]
            on = o * lax.rsqrt(jnp.mean(o * o, axis=1, keepdims=True) + EPS)
            gv = g_ref[:, vs]
            o_ref[:, vs] = on * gn_ref[:, vs] * (gv * _sigmoid(gv))

    row = pl.BlockSpec((tm, B_V), lambda i: (i, 0))
    vec = pl.BlockSpec((1, B_V), lambda i: (0, 0))
    return _call(body, name=name, grid=(t // tm,), in_specs=[row, row, row, vec], out_specs=row,
                 out_shape=_sds((t, B_V)))(of, ob, g, gn.reshape(1, B_V))


def gla_out_bwd(of, ob, g, gn, dout, name):
    t = of.shape[0]
    tm = min(256, t)

    def body(of_ref, ob_ref, g_ref, gn_ref, d_ref, do_ref, dg_ref, dgn_ref):
        first = pl.program_id(0) == 0
        for h in range(B_HEADS):
            vs = slice(h * B_VAL_DIM, (h + 1) * B_VAL_DIM)
            o = of_ref[:, vs] + ob_ref[:, vs]
            r = lax.rsqrt(jnp.mean(o * o, axis=1, keepdims=True) + EPS)
            on = o * r
            gv = g_ref[:, vs]
            sg = _sigmoid(gv)
            silu = gv * sg
            dv = d_ref[:, vs]
            gnv = gn_ref[:, vs]
            dg_ref[:, vs] = dv * on * gnv * (sg * (1.0 + gv * (1.0 - sg)))
            don = dv * silu * gnv
            do_ref[:, vs] = r * (don - on * jnp.mean(don * on, axis=1, keepdims=True))
            part = jnp.sum(dv * silu * on, axis=0, keepdims=True)

            @pl.when(first)
            def _():
                dgn_ref[:, vs] = part

            @pl.when(jnp.logical_not(first))
            def _():
                dgn_ref[:, vs] += part

    row = pl.BlockSpec((tm, B_V), lambda i: (i, 0))
    vec = pl.BlockSpec((1, B_V), lambda i: (0, 0))
    return _call(body, name=name, grid=(t // tm,), in_specs=[row, row, row, vec, row], out_specs=(row, row, vec),
                 out_shape=(_sds((t, B_V)), _sds((t, B_V)), _sds((1, B_V))))(of, ob, g, gn.reshape(1, B_V), dout)


def _shift(x, k):
    if k > 0:
        return jnp.concatenate([x[k:], jnp.zeros((k,) + x.shape[1:], x.dtype)], axis=0)
    return jnp.concatenate([jnp.zeros((-k,) + x.shape[1:], x.dtype), x[:k]], axis=0)


def _lru_gates(xc, s, wa_ref, ba_ref, wx_ref, bx_ref, lam_ref):
    cols = [slice(g * C_BLOCK_DIM, (g + 1) * C_BLOCK_DIM) for g in range(C_BLOCKS)]
    zr = jnp.concatenate([_dot(xc[:, cs], wa_ref[s, g]) for g, cs in enumerate(cols)], axis=1) + ba_ref[s:s + 1, :]
    zi = jnp.concatenate([_dot(xc[:, cs], wx_ref[s, g]) for g, cs in enumerate(cols)], axis=1) + bx_ref[s:s + 1, :]
    r = _sigmoid(zr)
    i = _sigmoid(zi)
    sp = _softplus(-lam_ref[s:s + 1, :])
    log_a = -LRU_C * r * sp
    return r, i, sp, log_a


def lru_gates_fwd(x0, xm2, xm1, xp1, cw, cb, wa, ba, wx, bx, lam, name):
    t = x0.shape[0]
    tm = min(256, t)

    def body(x0_ref, xm2_ref, xm1_ref, xp1_ref, cw_ref, cb_ref, wa_ref, ba_ref, wx_ref, bx_ref, lam_ref,
             xc_ref, a0_ref, u0_ref, a1_ref, u1_ref):
        xc = (xm2_ref[...] * cw_ref[0:1, :] + xm1_ref[...] * cw_ref[1:2, :] + x0_ref[...] * cw_ref[2:3, :]
              + xp1_ref[...] * cw_ref[3:4, :] + cb_ref[...])
        xc_ref[...] = xc
        for s, (a_ref, u_ref) in enumerate(((a0_ref, u0_ref), (a1_ref, u1_ref))):
            _, i, _, log_a = _lru_gates(xc, s, wa_ref, ba_ref, wx_ref, bx_ref, lam_ref)
            a_ref[...] = jnp.exp(log_a)
            u_ref[...] = jnp.sqrt(-_expm1(2.0 * log_a)) * (i * xc)

    row = pl.BlockSpec((tm, C_WIDTH), lambda i: (i, 0))
    full = lambda shape: pl.BlockSpec(shape, lambda i: (0,) * len(shape))
    wshape = (2, C_BLOCKS, C_BLOCK_DIM, C_BLOCK_DIM)
    return _call(body, name=name, grid=(t // tm,),
                 in_specs=[row] * 4 + [full((4, C_WIDTH)), full((1, C_WIDTH)), full(wshape), full((2, C_WIDTH)),
                                       full(wshape), full((2, C_WIDTH)), full((2, C_WIDTH))],
                 out_specs=(row,) * 5, out_shape=(_sds((t, C_WIDTH)),) * 5)(
        x0, xm2, xm1, xp1, cw, cb.reshape(1, C_WIDTH), wa, ba, wx, bx, lam)


def lru_gates_bwd(xc, g0, hs0, g1, hs1, wa, ba, wx, bx, lam, name):
    t = xc.shape[0]
    tm = min(256, t)

    def body(xc_ref, g0_ref, hs0_ref, g1_ref, hs1_ref, wa_ref, ba_ref, wx_ref, bx_ref, lam_ref,
             dxc_ref, dzr0_ref, dzi0_ref, dzr1_ref, dzi1_ref, dlam_ref, dba_ref, dbx_ref):
        first = pl.program_id(0) == 0

        @pl.when(first)
        def _():
            dlam_ref[...] = jnp.zeros_like(dlam_ref)
            dba_ref[...] = jnp.zeros_like(dba_ref)
            dbx_ref[...] = jnp.zeros_like(dbx_ref)

        xcv = xc_ref[...]
        dxc = jnp.zeros_like(xcv)
        cols = [slice(g * C_BLOCK_DIM, (g + 1) * C_BLOCK_DIM) for g in range(C_BLOCKS)]
        for s, (g_ref, hs_ref, dzr_ref, dzi_ref) in enumerate(
                ((g0_ref, hs0_ref, dzr0_ref, dzi0_ref), (g1_ref, hs1_ref, dzr1_ref, dzi1_ref))):
            r, i, sp, log_a = _lru_gates(xcv, s, wa_ref, ba_ref, wx_ref, bx_ref, lam_ref)
            du = g_ref[...]
            da = du * hs_ref[...]
            a = jnp.exp(log_a)
            e2 = jnp.exp(2.0 * log_a)
            c = jnp.sqrt(-_expm1(2.0 * log_a))
            ix = i * xcv
            dlog = da * a - du * ix * (e2 / c)
            dix = du * c
            dxc = dxc + dix * i
            dzi = dix * xcv * i * (1.0 - i)
            dzr = dlog * (-LRU_C * sp) * r * (1.0 - r)
            dzr_ref[...] = dzr
            dzi_ref[...] = dzi
            dxc = dxc + jnp.concatenate(
                [_dot(dzr[:, cs], wa_ref[s, g], NT) + _dot(dzi[:, cs], wx_ref[s, g], NT) for g, cs in enumerate(cols)],
                axis=1)
            dsp = jnp.sum(dlog * (-LRU_C * r), axis=0, keepdims=True)
            dlam_ref[s:s + 1, :] += dsp * (-_sigmoid(-lam_ref[s:s + 1, :]))
            dba_ref[s:s + 1, :] += jnp.sum(dzr, axis=0, keepdims=True)
            dbx_ref[s:s + 1, :] += jnp.sum(dzi, axis=0, keepdims=True)
        dxc_ref[...] = dxc

    row = pl.BlockSpec((tm, C_WIDTH), lambda i: (i, 0))
    full = lambda shape: pl.BlockSpec(shape, lambda i: (0,) * len(shape))
    wshape = (2, C_BLOCKS, C_BLOCK_DIM, C_BLOCK_DIM)
    vec2 = full((2, C_WIDTH))
    return _call(body, name=name, grid=(t // tm,),
                 in_specs=[row] * 5 + [full(wshape), vec2, full(wshape), vec2, vec2],
                 out_specs=(row,) * 5 + (vec2,) * 3,
                 out_shape=(_sds((t, C_WIDTH)),) * 5 + (_sds((2, C_WIDTH)),) * 3)(
        xc, g0, hs0, g1, hs1, wa, ba, wx, bx, lam)


def lru_out_fwd(h0, h1, y, name):
    t = y.shape[0]
    tm = min(256, t)

    def body(h0_ref, h1_ref, y_ref, o_ref):
        o_ref[...] = (h0_ref[...] + h1_ref[...]) * _gelu(y_ref[...])

    row = pl.BlockSpec((tm, C_WIDTH), lambda i: (i, 0))
    return _call(body, name=name, grid=(t // tm,), in_specs=[row] * 3, out_specs=row,
                 out_shape=_sds((t, C_WIDTH)))(h0, h1, y)


def lru_out_bwd(h0, h1, y, dout, name):
    t = y.shape[0]
    tm = min(256, t)

    def body(h0_ref, h1_ref, y_ref, d_ref, dh_ref, dy_ref):
        yv = y_ref[...]
        dv = d_ref[...]
        dh_ref[...] = dv * _gelu(yv)
        dy_ref[...] = dv * (h0_ref[...] + h1_ref[...]) * _gelu_grad(yv)

    row = pl.BlockSpec((tm, C_WIDTH), lambda i: (i, 0))
    return _call(body, name=name, grid=(t // tm,), in_specs=[row] * 4, out_specs=(row, row),
                 out_shape=(_sds((t, C_WIDTH)),) * 2)(h0, h1, y, dout)


def conv_bwd(dxc, dp2, dp1, dm1, x0, xm2, xm1, xp1, cw, name):
    t = x0.shape[0]
    tm = min(256, t)

    def body(d_ref, dp2_ref, dp1_ref, dm1_ref, x0_ref, xm2_ref, xm1_ref, xp1_ref, cw_ref, dx_ref, dcw_ref, dcb_ref):
        @pl.when(pl.program_id(0) == 0)
        def _():
            dcw_ref[...] = jnp.zeros_like(dcw_ref)
            dcb_ref[...] = jnp.zeros_like(dcb_ref)

        dv = d_ref[...]
        dx_ref[...] = (dp2_ref[...] * cw_ref[0:1, :] + dp1_ref[...] * cw_ref[1:2, :] + dv * cw_ref[2:3, :]
                       + dm1_ref[...] * cw_ref[3:4, :])
        for j, x_ref in enumerate((xm2_ref, xm1_ref, x0_ref, xp1_ref)):
            dcw_ref[j:j + 1, :] += jnp.sum(dv * x_ref[...], axis=0, keepdims=True)
        dcb_ref[...] += jnp.sum(dv, axis=0, keepdims=True)

    row = pl.BlockSpec((tm, C_WIDTH), lambda i: (i, 0))
    cws = pl.BlockSpec((4, C_WIDTH), lambda i: (0, 0))
    cbs = pl.BlockSpec((1, C_WIDTH), lambda i: (0, 0))
    return _call(body, name=name, grid=(t // tm,), in_specs=[row] * 8 + [cws], out_specs=(row, cws, cbs),
                 out_shape=(_sds((t, C_WIDTH)), _sds((4, C_WIDTH)), _sds((1, C_WIDTH))))(
        dxc, dp2, dp1, dm1, x0, xm2, xm1, xp1, cw)


def _my_place():
    return lax.axis_index("x"), lax.axis_index("y"), lax.axis_index("c")


def all_gather(xs, name):
    r, c = xs.shape

    def body(x_ref, out_ref, send_sems, recv_sems, local_sem):
        x, y, cc = _my_place()
        me, sibling = (x, y, cc), (x, y, 1 - cc)
        chips = [(1 - x, y), (x, 1 - y), (1 - x, 1 - y)]

        def slot(px, py, pc):
            return out_ref.at[4 * px + 2 * py + pc]

        def copy(k, block, to, src=None):
            return pltpu.make_async_remote_copy(
                src_ref=slot(*block) if src is None else src, dst_ref=slot(*block),
                send_sem=send_sems.at[k], recv_sem=recv_sems.at[k], device_id=to, device_id_type=MESH)

        mine = pltpu.make_async_copy(x_ref, slot(*me), local_sem)
        mine.start()
        first = [copy(0, me, sibling, src=x_ref)]
        first += [copy(1 + j, me, (*chip, cc), src=x_ref) for j, chip in enumerate(chips)]
        for cp in first:
            cp.start()
        passed = [copy(4 + j, (*chip, cc), sibling) for j, chip in enumerate(chips)]
        for j, chip in enumerate(chips):
            copy(1 + j, (*chip, cc), me).wait_recv()
            passed[j].start()
        copy(0, sibling, me).wait_recv()
        for j, chip in enumerate(chips):
            copy(4 + j, (*chip, 1 - cc), me).wait_recv()
        for cp in first + passed:
            cp.wait_send()
        mine.wait()

    return _call(body, name=name, in_specs=[pl.BlockSpec(memory_space=pl.ANY)],
                 out_specs=pl.BlockSpec(memory_space=pl.ANY), out_shape=_sds((N_DEV, r, c), xs.dtype),
                 scratch_shapes=[pltpu.SemaphoreType.DMA((7,)), pltpu.SemaphoreType.DMA((7,)),
                                 pltpu.SemaphoreType.DMA])(xs)


def _stream_rows(r):
    nch = SIBLING_STREAMS // 4 if r % (8 * (SIBLING_STREAMS // 4)) == 0 else 1
    return nch, r // nch


def exchange_sibling(gw, name):
    _, r, c = gw.shape
    g5 = gw.reshape(4, 2, r, c)
    nch, rows = _stream_rows(r)

    def body(g_ref, out_ref, send_sems, recv_sems):
        x, y, cc = _my_place()
        swaps = []
        for q in range(4):
            for s in range(nch):
                k = q * nch + s
                win = pl.ds(s * rows, rows)
                swaps.append(pltpu.make_async_remote_copy(
                    src_ref=g_ref.at[q, 1 - cc, win], dst_ref=out_ref.at[q, win], send_sem=send_sems.at[k],
                    recv_sem=recv_sems.at[k], device_id=(x, y, 1 - cc), device_id_type=MESH))
        for cp in swaps:
            cp.start()
        for cp in swaps:
            cp.wait()

    nsem = 4 * nch
    return _call(body, name=name, in_specs=[pl.BlockSpec(memory_space=pl.ANY)],
                 out_specs=pl.BlockSpec(memory_space=pl.ANY), out_shape=_sds((4, r, c), gw.dtype),
                 scratch_shapes=[pltpu.SemaphoreType.DMA((nsem,)), pltpu.SemaphoreType.DMA((nsem,))])(g5)


HBM_SPEC = pl.BlockSpec(memory_space=pltpu.HBM)
SEM_SPEC = pl.BlockSpec(memory_space=pltpu.SEMAPHORE)
DATAFLOW = pltpu.SideEffectType.DATAFLOW_SIDE_EFFECTING


def _hbm(a):
    return pltpu.with_memory_space_constraint(a, pltpu.HBM)


def _peers(x, y, cc):
    return [(x, y, 1 - cc), (1 - x, y, cc), (x, 1 - y, cc), (1 - x, 1 - y, cc)]


def _slot(p):
    return 4 * p[0] + 2 * p[1] + p[2]


def gather_start(blk, name):
    r, c = blk.shape

    def body(v_ref, land_ref, send_sems, recv_sems, v_thru, land_thru, token):
        x, y, cc = _my_place()
        for k, to in enumerate(_peers(x, y, cc)):
            pltpu.make_async_remote_copy(
                src_ref=v_ref, dst_ref=land_ref.at[_slot((x, y, cc))], send_sem=send_sems.at[k],
                recv_sem=recv_sems.at[k], device_id=to, device_id_type=MESH).start()
        pltpu.make_async_copy(v_ref, land_ref.at[_slot((x, y, cc))], send_sems.at[4]).start()
        token[...] = jnp.zeros_like(token)

    return _call(
        body, name=name,
        out_shape=(pltpu.SemaphoreType.DMA((5,)), pltpu.SemaphoreType.DMA((4,)), pltpu.HBM((r, c), blk.dtype),
                   pltpu.HBM((N_DEV, r, c), blk.dtype), _sds((8, 128))),
        in_specs=(HBM_SPEC, HBM_SPEC),
        out_specs=(SEM_SPEC, SEM_SPEC, HBM_SPEC, HBM_SPEC, pl.BlockSpec(memory_space=pltpu.VMEM)),
        input_output_aliases={0: 2, 1: 3},
        compiler_params=pltpu.CompilerParams(has_side_effects=DATAFLOW),
    )(_hbm(blk), _hbm(lax.empty((N_DEV, r, c), blk.dtype)))


def gather_wait(send_sems, recv_sems, v_thru, land_thru, after, name):
    def body(v_ref, land_ref, send_sems, recv_sems, after_ref, v_out, land_out):
        x, y, cc = _my_place()
        for k, peer in enumerate(_peers(x, y, cc)):
            cp = pltpu.make_async_remote_copy(
                src_ref=v_ref, dst_ref=land_ref.at[_slot(peer)], send_sem=send_sems.at[k], recv_sem=recv_sems.at[k],
                device_id=peer, device_id_type=MESH)
            cp.wait_send()
            cp.wait_recv()
        pltpu.make_async_copy(v_ref, land_ref.at[_slot((x, y, cc))], send_sems.at[4]).wait()

    return _call(
        body, name=name,
        out_shape=(pltpu.HBM(v_thru.shape, v_thru.dtype), pltpu.HBM(land_thru.shape, land_thru.dtype)),
        in_specs=(HBM_SPEC, HBM_SPEC, SEM_SPEC, SEM_SPEC, pl.BlockSpec(memory_space=pl.ANY)),
        out_specs=(HBM_SPEC, HBM_SPEC), input_output_aliases={0: 0, 1: 1},
        compiler_params=pltpu.CompilerParams(has_side_effects=DATAFLOW),
    )(v_thru, land_thru, send_sems, recv_sems, after)


def gather_pass(land, name):
    _, r, c = land.shape
    nch, rows = _stream_rows(r)

    def body(land_ref, out_ref, send_sems, recv_sems):
        x, y, cc = _my_place()
        peers = _peers(x, y, cc)
        copies = []
        for j in range(3):
            mine, theirs = _slot(peers[1 + j]), _slot((peers[1 + j][0], peers[1 + j][1], 1 - cc))
            for s in range(nch):
                k = j * nch + s
                win = pl.ds(s * rows, rows)
                send = pltpu.make_async_remote_copy(
                    src_ref=land_ref.at[mine, win], dst_ref=out_ref.at[mine, win], send_sem=send_sems.at[k],
                    recv_sem=recv_sems.at[k], device_id=peers[0], device_id_type=MESH)
                recv = pltpu.make_async_remote_copy(
                    src_ref=land_ref.at[mine, win], dst_ref=out_ref.at[theirs, win], send_sem=send_sems.at[k],
                    recv_sem=recv_sems.at[k], device_id=peers[0], device_id_type=MESH)
                copies.append((send, recv))
        for send, _ in copies:
            send.start()
        for send, recv in copies:
            send.wait_send()
            recv.wait_recv()

    nsem = 3 * nch
    return _call(body, name=name, in_specs=[pl.BlockSpec(memory_space=pl.ANY)],
                 out_specs=pl.BlockSpec(memory_space=pl.ANY), out_shape=_sds(land.shape, land.dtype),
                 input_output_aliases={0: 0},
                 scratch_shapes=[pltpu.SemaphoreType.DMA((nsem,)), pltpu.SemaphoreType.DMA((nsem,))])(land)


def chips_start(p, name):
    _, r, c = p.shape

    def body(p_ref, land_ref, send_sems, recv_sems, p_thru, land_thru, token):
        x, y, cc = _my_place()
        for j, (px, py, pc) in enumerate(_peers(x, y, cc)[1:]):
            pltpu.make_async_remote_copy(
                src_ref=p_ref.at[2 * px + py], dst_ref=land_ref.at[j], send_sem=send_sems.at[j],
                recv_sem=recv_sems.at[j], device_id=(px, py, pc), device_id_type=MESH).start()
        token[...] = jnp.zeros_like(token)

    return _call(
        body, name=name,
        out_shape=(pltpu.SemaphoreType.DMA((3,)), pltpu.SemaphoreType.DMA((3,)), pltpu.HBM(p.shape, p.dtype),
                   pltpu.HBM((3, r, c), p.dtype), _sds((8, 128))),
        in_specs=(HBM_SPEC, HBM_SPEC),
        out_specs=(SEM_SPEC, SEM_SPEC, HBM_SPEC, HBM_SPEC, pl.BlockSpec(memory_space=pltpu.VMEM)),
        input_output_aliases={0: 2, 1: 3},
        compiler_params=pltpu.CompilerParams(has_side_effects=DATAFLOW),
    )(_hbm(p), _hbm(lax.empty((3, r, c), p.dtype)))


def chips_wait(send_sems, recv_sems, p_thru, land_thru, after, name):
    def body(p_ref, land_ref, send_sems, recv_sems, after_ref, p_out, land_out):
        x, y, cc = _my_place()
        for j, (px, py, pc) in enumerate(_peers(x, y, cc)[1:]):
            cp = pltpu.make_async_remote_copy(
                src_ref=p_ref.at[2 * px + py], dst_ref=land_ref.at[j], send_sem=send_sems.at[j],
                recv_sem=recv_sems.at[j], device_id=(px, py, pc), device_id_type=MESH)
            cp.wait_send()
            cp.wait_recv()

    return _call(
        body, name=name,
        out_shape=(pltpu.HBM(p_thru.shape, p_thru.dtype), pltpu.HBM(land_thru.shape, land_thru.dtype)),
        in_specs=(HBM_SPEC, HBM_SPEC, SEM_SPEC, SEM_SPEC, pl.BlockSpec(memory_space=pl.ANY)),
        out_specs=(HBM_SPEC, HBM_SPEC), input_output_aliases={0: 0, 1: 1},
        compiler_params=pltpu.CompilerParams(has_side_effects=DATAFLOW),
    )(p_thru, land_thru, send_sems, recv_sems, after)


def sibling_start(gw, name):
    _, r, c = gw.shape
    nch, rows = _stream_rows(r)
    nsem = 4 * nch

    def body(g_ref, land_ref, send_sems, recv_sems, g_thru, land_thru, token):
        x, y, cc = _my_place()
        for q in range(4):
            for s in range(nch):
                win = pl.ds(s * rows, rows)
                pltpu.make_async_remote_copy(
                    src_ref=g_ref.at[q, 1 - cc, win], dst_ref=land_ref.at[q, win], send_sem=send_sems.at[q * nch + s],
                    recv_sem=recv_sems.at[q * nch + s], device_id=(x, y, 1 - cc), device_id_type=MESH).start()
        token[...] = jnp.zeros_like(token)

    return _call(
        body, name=name,
        out_shape=(pltpu.SemaphoreType.DMA((nsem,)), pltpu.SemaphoreType.DMA((nsem,)),
                   pltpu.HBM((4, 2, r, c), gw.dtype), pltpu.HBM((4, r, c), gw.dtype), _sds((8, 128))),
        in_specs=(HBM_SPEC, HBM_SPEC),
        out_specs=(SEM_SPEC, SEM_SPEC, HBM_SPEC, HBM_SPEC, pl.BlockSpec(memory_space=pltpu.VMEM)),
        input_output_aliases={0: 2, 1: 3},
        compiler_params=pltpu.CompilerParams(has_side_effects=DATAFLOW),
    )(_hbm(gw.reshape(4, 2, r, c)), _hbm(lax.empty((4, r, c), gw.dtype)))


def sibling_wait(send_sems, recv_sems, g_thru, land_thru, after, name):
    _, _, r, c = g_thru.shape
    nch, rows = _stream_rows(r)

    def body(g_ref, land_ref, send_sems, recv_sems, after_ref, g_out, land_out):
        x, y, cc = _my_place()
        for q in range(4):
            for s in range(nch):
                win = pl.ds(s * rows, rows)
                cp = pltpu.make_async_remote_copy(
                    src_ref=g_ref.at[q, 1 - cc, win], dst_ref=land_ref.at[q, win], send_sem=send_sems.at[q * nch + s],
                    recv_sem=recv_sems.at[q * nch + s], device_id=(x, y, 1 - cc), device_id_type=MESH)
                cp.wait_send()
                cp.wait_recv()

    return _call(
        body, name=name,
        out_shape=(pltpu.HBM(g_thru.shape, g_thru.dtype), pltpu.HBM(land_thru.shape, land_thru.dtype)),
        in_specs=(HBM_SPEC, HBM_SPEC, SEM_SPEC, SEM_SPEC, pl.BlockSpec(memory_space=pl.ANY)),
        out_specs=(HBM_SPEC, HBM_SPEC), input_output_aliases={0: 0, 1: 1},
        compiler_params=pltpu.CompilerParams(has_side_effects=DATAFLOW),
    )(g_thru, land_thru, send_sems, recv_sems, after)


def reduce_scatter_begin(gw, name, tag):
    _, r, c = gw.shape
    theirs = exchange_sibling(gw, name + "_sibling")
    chip_sum = add_own_half(gw.reshape(4, 2, r, c), theirs, name + "_add2")
    return chips_start(chip_sum, name + "_start" + tag)


def reduce_scatter_end(started, after, name, tag):
    send_sems, recv_sems, p_thru, land_thru, _ = started
    parts, land = chips_wait(send_sems, recv_sems, p_thru, land_thru, after, name + "_wait" + tag)
    mine = lax.dynamic_index_in_dim(parts, 2 * lax.axis_index("x") + lax.axis_index("y"), axis=0, keepdims=False)
    return add_own_lead(mine, land, name + "_add4")


def _pack(arrs):
    flat = jnp.concatenate([a.reshape(-1).astype(F32) for a in arrs])
    n = flat.shape[0]
    pad = (-n) % PACK_ELEMS
    return jnp.pad(flat, (0, pad)).reshape(-1, 128)


def _unpack(packed, shapes):
    flat = packed.reshape(-1)
    out, off = [], 0
    for s in shapes:
        n = int(np.prod(s))
        out.append(lax.optimization_barrier(flat[off:off + n]).reshape(s))
        off += n
    return out


def _t5_bucket(rel):
    nb = N_BUCKETS // 2
    max_exact = nb // 2
    ret = jnp.where(rel > 0, nb, 0)
    n = jnp.abs(rel)
    nf = jnp.maximum(n, 1).astype(jnp.float32)
    large = max_exact + (jnp.log(nf / max_exact) / math.log(MAX_DISTANCE / max_exact)
                         * (nb - max_exact)).astype(jnp.int32)
    large = jnp.minimum(large, nb - 1)
    return ret + jnp.where(n < max_exact, n, large)


SMALL_SHARDED = ("gla_w2_f", "gla_w2_b", "conv_w", "lru_ba", "lru_bx", "lru_lambda")
SMALL_REPL = ("rel_bias", "attn_sink", "gla_b2_f", "gla_b2_b", "gla_norm", "conv_b", "lru_wa", "lru_wx",
              "norm_mix_pre", "norm_mix_post", "norm_mem", "norm_x_pre", "norm_x_post", "norm_ff_pre", "norm_ff_post")
BIG = ("w_in", "w_out", "xq", "xk", "xv", "xo", "w_up", "w_down")
WEIGHTS = ['rel_bias', 'w_in', 'w_out', 'attn_sink', 'gla_w2_f', 'gla_b2_f', 'gla_w2_b', 'gla_b2_b', 'gla_norm',
           'conv_w', 'conv_b', 'lru_wa', 'lru_ba', 'lru_wx', 'lru_bx', 'lru_lambda', 'xq', 'xk', 'xv', 'xo', 'w_up',
           'w_down', 'norm_mix_pre', 'norm_mix_post', 'norm_mem', 'norm_x_pre', 'norm_x_post', 'norm_ff_pre',
           'norm_ff_post']


def _step(x, mem, loss_target, w, m, v):
    depth = w["w_in"].shape[0]
    t, d = x.shape[1], x.shape[2]
    ml = mem.shape[1]
    rx = d // N_DEV
    rf = w["w_up"].shape[2]
    r_out = D_MIX // N_DEV
    x = x.reshape(t, d)
    mem = mem.reshape(ml, d)
    loss_target = loss_target.reshape(t, d)
    my_idx = 4 * lax.axis_index("x") + 2 * lax.axis_index("y") + lax.axis_index("c")

    off_in = 0
    off_up, off_down, off_out = 0, rf, 2 * rf
    off_xq = off_out + r_out
    off_xk, off_xv, off_xo = off_xq + rx, off_xq + 2 * rx, off_xq + 3 * rx
    r_rest = off_xo + rx

    sh_shapes = [w[n].shape for n in SMALL_SHARDED]
    gathered = all_gather(_pack([w[n] for n in SMALL_SHARDED]), "ag_small")
    per_dev = [_unpack(gathered[j], sh_shapes) for j in range(N_DEV)]
    full = {n: jnp.concatenate([per_dev[j][i] for j in range(N_DEV)], axis=-1) for i, n in enumerate(SMALL_SHARDED)}
    for n in SMALL_REPL:
        full[n] = w[n]

    ag_started = []
    for l in range(depth):
        blk_in = jnp.pad(w["w_in"][l].T, ((0, W_IN_ROWS - W_IN_SHARD), (0, 0))).astype(BF16)
        blk_rest = jnp.concatenate([w["w_up"][l].T, w["w_down"][l], w["w_out"][l], w["xq"][l], w["xk"][l],
                                    w["xv"][l], w["xo"][l]], axis=0).astype(BF16)
        blk_in, _ = lax.optimization_barrier((blk_in, gathered if l == 0 else ag_started[-1][1][4]))
        start_in = gather_start(blk_in, "ag_start_in%d" % l)
        blk_rest, _ = lax.optimization_barrier((blk_rest, start_in[4]))
        ag_started.append((start_in, gather_start(blk_rest, "ag_start_rest%d" % l)))
    gather_token = sum(st[4][0, 0] for pair in ag_started for st in pair)
    gws = [None] * depth

    def gather_finish(started, after, name):
        send_sems, recv_sems, blk_thru, land_thru, _ = started
        _, land = gather_wait(send_sems, recv_sems, blk_thru, land_thru, after, name)
        return gather_pass(land, "ag_pass")

    qi = jnp.arange(BLOCK)[:, None]
    kj = jnp.arange(3 * BLOCK)[None, :]
    onehot_t = (jnp.arange(N_BUCKETS)[:, None] == _t5_bucket(kj - BLOCK - qi).reshape(1, -1)).astype(F32)
    bias = mm_plain(full["rel_bias"].T, onehot_t, "rel_bias_lookup", hi=True, tn=3 * BLOCK * 16)
    bias = bias.reshape(A_HEADS, BLOCK, 3 * BLOCK)
    bias_t = jnp.transpose(bias, (0, 2, 1))

    def sink_rows(sink):
        s = jnp.broadcast_to(sink.reshape(A_KV_HEADS, A_GROUP, 1), (A_KV_HEADS, A_GROUP, 128))
        return jnp.pad(s, ((0, 0), (0, 8 - A_GROUP), (0, 0)))

    bounds = np.concatenate([[0], np.cumsum(SPLIT_SIZES)])

    def split_proj(pp):
        outs = []
        for lo, hi in zip(bounds[:-1], bounds[1:]):
            segs = []
            for j in range(N_DEV):
                a, b = max(lo, j * W_IN_SHARD), min(hi, (j + 1) * W_IN_SHARD)
                if a < b:
                    base = j * W_IN_ROWS - j * W_IN_SHARD
                    segs.append(pp[:, base + a:base + b])
            outs.append(segs[0] if len(segs) == 1 else jnp.concatenate(segs, axis=1))
        return outs

    def join_dproj(pieces):
        zero_cols = jnp.zeros((t, W_IN_ROWS - W_IN_SHARD), F32)
        segs = []
        for j in range(N_DEV):
            for p, lo, hi in zip(pieces, bounds[:-1], bounds[1:]):
                a, b = max(lo, j * W_IN_SHARD), min(hi, (j + 1) * W_IN_SHARD)
                if a < b:
                    segs.append(p[:, a - lo:b - lo])
            segs.append(zero_cols)
        return jnp.concatenate(segs, axis=1).astype(BF16)

    def lead(a):
        return a.reshape(a.shape[0], C_WIDTH // 128, 128)

    saved = []
    h = rms_fwd(x, full["norm_mix_pre"][0] + gather_token, "rms_first")
    for l in range(depth):
        gw_in = gather_finish(ag_started[l][0], x, "ag_wait_in%d" % l)
        sv = {"x": x, "h_in": h}
        proj_pad = mm_wn(h, gw_in, off_in, W_IN_ROWS, "mm_w_in")
        aq, ak, av, bq, bk, bv, bg, zf, zb, cx, cy = split_proj(proj_pad)
        sv.update(aq=aq, ak=ak, av=av, bq=bq, bk=bk, bv=bv, bg=bg, zf=zf, zb=zb, cx=cx, cy=cy)
        sink_b = sink_rows(full["attn_sink"][l])
        oa = attn_fwd(aq, ak, av, bias, sink_b, "attn_fwd")
        la_f, la_b = gla_gates_fwd(zf, zb, full["gla_w2_f"][l], full["gla_b2_f"][l], full["gla_w2_b"][l],
                                   full["gla_b2_b"][l], "gla_gates_fwd")
        bk_t = bk.T
        gla = {}
        for nm, la, bdir in (("f", la_f, False), ("b", la_b, True)):
            la_t = la.T
            u, dec = gla_outer(bk_t, la_t, bv, "gla_outer_k_" + nm, bdir, "khat")
            sp = scan_lead(dec, u, "gla_state_scan_" + nm, reverse=bdir, inclusive=False)
            o_dir = gla_inner_fwd(bq, bk, bv, la, sp, "gla_inner_fwd_" + nm, bdir)
            gla[nm] = dict(la=la, la_t=la_t, dec=dec, sp=sp, o=o_dir)
        ob = gla_out_fwd(gla["f"]["o"], gla["b"]["o"], bg, full["gla_norm"][l], "gla_out_fwd")
        sv["gla"] = gla
        xm2, xm1, xp1 = _shift(cx, -2), _shift(cx, -1), _shift(cx, 1)
        xc, a0, u0, a1, u1 = lru_gates_fwd(cx, xm2, xm1, xp1, full["conv_w"][l], full["conv_b"][l], full["lru_wa"][l],
                                           full["lru_ba"][l], full["lru_wx"][l], full["lru_bx"][l],
                                           full["lru_lambda"][l], "lru_gates_fwd")
        h0 = scan_lead(lead(a0), lead(u0), "lru_scan_fwd", reverse=False, inclusive=True).reshape(t, C_WIDTH)
        h1 = scan_lead(lead(a1), lead(u1), "lru_scan_rev", reverse=True, inclusive=True).reshape(t, C_WIDTH)
        oc = lru_out_fwd(h0, h1, cy, "lru_out_fwd")
        sv.update(xm2=xm2, xm1=xm1, xp1=xp1, xc=xc, a0=a0, a1=a1, h0=h0, h1=h1, oa=oa)
        cat = jnp.concatenate([oa, ob, oc], axis=1).astype(BF16)
        gw = gather_finish(ag_started[l][1], cat, "ag_wait_rest%d" % l)
        gws[l] = (gw_in, gw)
        mixed = mm_wk(cat, gw, off_out, r_out, "mm_w_out")
        x1, h2 = resid_rms(x, mixed, full["norm_mix_post"][l], full["norm_x_pre"][l], "resid_rms")
        sv.update(cat=cat, mixed=mixed, x1=x1, h2=h2)
        memn = rms_fwd(mem, full["norm_mem"][l], "rms_mem")
        q = mm_wk(h2, gw, off_xq, rx, "mm_xq")
        k = mm_wk(memn, gw, off_xk, rx, "mm_xkv")
        vv = mm_wk(memn, gw, off_xv, rx, "mm_xkv")
        ox = xattn_fwd(q, k, vv, "xattn_fwd")
        ca = mm_wk(ox, gw, off_xo, rx, "mm_xo")
        x2, h3 = resid_rms(x1, ca, full["norm_x_post"][l], full["norm_ff_pre"][l], "resid_rms")
        sv.update(memn=memn, q=q, k=k, v=vv, ox=ox, ca=ca, x2=x2, h3=h3)
        up, act = mm_wn(h3, gw, off_up, rf, "mm_w_up", with_relu2=True)
        ff = mm_wk(act, gw, off_down, rf, "mm_w_down", jb=max(1, min(N_DEV, 2048 // rf)))
        if l + 1 < depth:
            x, h = resid_rms(x2, ff, full["norm_ff_post"][l], full["norm_mix_pre"][l + 1], "resid_rms")
        else:
            x = resid_rms(x2, ff, full["norm_ff_post"][l], None, "resid_rms_last")
        sv.update(up=up, act=act, ff=ff)
        saved.append(sv)

    dx, loss_local = loss_and_grad(x, loss_target, "loss")
    loss = lax.psum(loss_local, AXES)

    grads = {n: [None] * depth for n in WEIGHTS if n != "rel_bias"}
    dbias_total = None
    big_grads = [None] * depth
    rs_started = [None] * depth
    rs_token = 0.0
    bf = lambda a: a.astype(BF16)
    for l in reversed(range(depth)):
        gw_in, gw = gws[l]
        sv = saved[l]
        dff, grads["norm_ff_post"][l] = rms_bwd(sv["ff"], full["norm_ff_post"][l] + rs_token, dx, "rms_bwd")
        dup = mm_wn(dff, gw, off_down, rf, "mm_w_down_dx", relu_grad_of=sv["up"], out_dtype=BF16)
        gpack = mm_dw_into(sv["act"], dff, lax.empty((N_DEV, r_rest, d), BF16), off_down, rf, "mm_dw_down")
        gpack = mm_dw_into(dup, sv["h3"], gpack, off_up, rf, "mm_dw_up")
        dh3 = mm_wk(dup, gw, off_up, rf, "mm_w_up_dx", jb=max(1, min(N_DEV, 2048 // rf)))
        dx2, grads["norm_ff_pre"][l] = rms_bwd(sv["x2"], full["norm_ff_pre"][l], dh3, "rms_bwd_add", add=dx)
        dca, grads["norm_x_post"][l] = rms_bwd(sv["ca"], full["norm_x_post"][l], dx2, "rms_bwd")
        dox = mm_wn(dca, gw, off_xo, rx, "mm_x_dx")
        gpack = mm_dw_into(sv["ox"], dca, gpack, off_xo, rx, "mm_dw_xo")
        dq, dk, dv = xattn_bwd(sv["q"], sv["k"], sv["v"], sv["ox"], dox, "xattn_bwd")
        gpack = mm_dw_into(sv["h2"], dq, gpack, off_xq, rx, "mm_dw_xq")
        gpack = mm_dw_into(sv["memn"], dk, gpack, off_xk, rx, "mm_dw_xk")
        gpack = mm_dw_into(sv["memn"], dv, gpack, off_xv, rx, "mm_dw_xv")
        dh2 = mm_wn(dq, gw, off_xq, rx, "mm_x_dx")
        dmem_k = mm_wn(dk, gw, off_xk, rx, "mm_x_dx_mem")
        dmem_v = mm_wn(dv, gw, off_xv, rx, "mm_x_dx_mem")
        _, grads["norm_mem"][l] = rms_bwd(mem, full["norm_mem"][l], dmem_k, "rms_bwd_mem", dy2=dmem_v)
        dx1, grads["norm_x_pre"][l] = rms_bwd(sv["x1"], full["norm_x_pre"][l], dh2, "rms_bwd_add", add=dx2)
        dmixed, grads["norm_mix_post"][l] = rms_bwd(sv["mixed"], full["norm_mix_post"][l], dx1, "rms_bwd")
        dcat = mm_wn(dmixed, gw, off_out, r_out, "mm_w_out_dx")
        gpack = mm_dw_into(sv["cat"], dmixed, gpack, off_out, r_out, "mm_dw_out")
        if l == 0:
            rs_started[l] = [reduce_scatter_begin(gpack, "rs_rest", str(l)), None]
            mix_token = rs_started[l][0][4][0, 0]
        else:
            sib = sibling_start(gpack, "rs_rest_sib_start%d" % l)
            mix_token = sib[4][0, 0]
        doa, dob, doc = dcat[:, :A_Q], dcat[:, A_Q:A_Q + B_V], dcat[:, A_Q + B_V:]
        daq, dak, dav, dbias, dsink = attn_bwd(sv["aq"], sv["ak"], sv["av"], bias, bias_t,
                                               sink_rows(full["attn_sink"][l] + mix_token), doa, sv["oa"], "attn_bwd")
        grads["attn_sink"][l] = dsink[:, :A_GROUP, 0].reshape(A_HEADS)
        dbias_total = dbias if dbias_total is None else add_n([dbias_total, dbias], F32, "add_dbias")
        gf, gb = sv["gla"]["f"], sv["gla"]["b"]
        do_gla, dbg, dgn = gla_out_bwd(gf["o"], gb["o"], sv["bg"], full["gla_norm"][l] + mix_token, dob,
                                       "gla_out_bwd")
        grads["gla_norm"][l] = dgn.reshape(B_V)
        bq_t = sv["bq"].T
        acc = None
        dlas = {}
        for nm, gd, bdir in (("f", gf, False), ("b", gb, True)):
            wq = gla_outer(bq_t, gd["la_t"], do_gla, "gla_outer_q_" + nm, bdir, "qtil")
            gs = scan_lead(gd["dec"], wq, "gla_adj_scan_" + nm, reverse=not bdir, inclusive=False)
            dbq, dbk, dbv, dlas[nm] = gla_inner_bwd(sv["bq"], sv["bk"], sv["bv"], gd["la"], do_gla, gd["sp"], gs,
                                                    gd["dec"], "gla_inner_bwd_" + nm, bdir, add=acc)
            acc = (dbq, dbk, dbv)
        dzf, dzb, dpre_f, dpre_b, db2f, db2b = gla_gates_bwd(
            sv["zf"], sv["zb"], full["gla_w2_f"][l], full["gla_b2_f"][l], full["gla_w2_b"][l], full["gla_b2_b"][l],
            dlas["f"], dlas["b"], "gla_gates_bwd")
        grads["gla_b2_f"][l] = db2f.reshape(B_QK)
        grads["gla_b2_b"][l] = db2b.reshape(B_QK)
        grads["gla_w2_f"][l] = mm_plain(sv["zf"].T, dpre_f, "mm_dw_gate", hi=True)
        grads["gla_w2_b"][l] = mm_plain(sv["zb"].T, dpre_b, "mm_dw_gate", hi=True)
        dh, dcy = lru_out_bwd(sv["h0"], sv["h1"], sv["cy"], doc, "lru_out_bwd")
        g0 = scan_lead(lead(_shift(sv["a0"], 1)), lead(dh), "lru_scan_rev", reverse=True,
                       inclusive=True).reshape(t, C_WIDTH)
        g1 = scan_lead(lead(_shift(sv["a1"], -1)), lead(dh), "lru_scan_fwd", reverse=False,
                       inclusive=True).reshape(t, C_WIDTH)
        dxc, dzr0, dzi0, dzr1, dzi1, dlam, dba, dbx = lru_gates_bwd(
            sv["xc"], g0, _shift(sv["h0"], -1), g1, _shift(sv["h1"], 1), full["lru_wa"][l], full["lru_ba"][l],
            full["lru_wx"][l], full["lru_bx"][l], full["lru_lambda"][l], "lru_gates_bwd")
        xc_t = bf(sv["xc"].T)
        grads["lru_wa"][l] = jnp.stack([blockdiag_dw(xc_t, dzr0, "lru_dw"), blockdiag_dw(xc_t, dzr1, "lru_dw")])
        grads["lru_wx"][l] = jnp.stack([blockdiag_dw(xc_t, dzi0, "lru_dw"), blockdiag_dw(xc_t, dzi1, "lru_dw")])
        grads["lru_lambda"][l], grads["lru_ba"][l], grads["lru_bx"][l] = dlam, dba, dbx
        dcx, dcw, dcb = conv_bwd(dxc, _shift(dxc, 2), _shift(dxc, 1), _shift(dxc, -1), sv["cx"], sv["xm2"],
                                 sv["xm1"], sv["xp1"], full["conv_w"][l], "conv_bwd")
        grads["conv_w"][l] = dcw
        grads["conv_b"][l] = dcb.reshape(C_WIDTH)
        dproj_pad = join_dproj([daq, dak, dav, dbq, dbk, dbv, dbg, dzf, dzb, dcx, dcy])
        g_in_t = mm_plain(dproj_pad, sv["h_in"], "mm_dw_in", ta=True, out_dtype=BF16)
        rs_in = reduce_scatter_begin(g_in_t.reshape(N_DEV, W_IN_ROWS, d), "rs_in", str(l))
        rs_token = rs_in[4][0, 0]
        dh1 = mm_wk(dproj_pad, gw_in, off_in, W_IN_ROWS, "mm_w_in_dx", jb=2)
        dx, grads["norm_mix_pre"][l] = rms_bwd(sv["x"], full["norm_mix_pre"][l] + rs_token, dh1, "rms_bwd_add",
                                               add=dx1)
        if l > 0:
            g5, theirs = sibling_wait(sib[0], sib[1], sib[2], sib[3], dx, "rs_rest_sib_wait%d" % l)
            chip_sum = add_own_half(g5, theirs, "rs_rest_add2")
            rs_started[l] = [chips_start(chip_sum, "rs_rest_start%d" % l), rs_in]
            rs_token = rs_token + rs_started[l][0][4][0, 0]
        else:
            rs_started[l][1] = rs_in

    grad_rel = mm_plain(dbias_total.reshape(A_HEADS, -1), onehot_t, "rel_bias_grad", tb=True, hi=True).T

    small_names = [n for n in WEIGHTS if n not in BIG]
    small_g = {"rel_bias": grad_rel}
    for n in small_names:
        if n != "rel_bias":
            small_g[n] = jnp.stack([g.reshape(full[n].shape[1:]) for g in grads[n]])
    shapes = [small_g[n].shape for n in small_names]
    small_started = gather_start(_pack([small_g[n] for n in small_names]), "ag_start_small_grads")
    for l in range(depth):
        big_grads[l] = (reduce_scatter_end(rs_started[l][1], small_started[4], "rs_in", str(l)),
                        reduce_scatter_end(rs_started[l][0], small_started[4], "rs_rest", str(l)))

    grad_out, delta, new_m, new_v = {}, {}, {}, {}

    def rows(l, off, r):
        return big_grads[l][1][off:off + r]

    big_g = {
        "w_in": jnp.stack([big_grads[l][0][:W_IN_SHARD].T for l in range(depth)]),
        "w_out": jnp.stack([rows(l, off_out, r_out) for l in range(depth)]),
        "xq": jnp.stack([rows(l, off_xq, rx) for l in range(depth)]),
        "xk": jnp.stack([rows(l, off_xk, rx) for l in range(depth)]),
        "xv": jnp.stack([rows(l, off_xv, rx) for l in range(depth)]),
        "xo": jnp.stack([rows(l, off_xo, rx) for l in range(depth)]),
        "w_up": jnp.stack([rows(l, off_up, rf).T for l in range(depth)]),
        "w_down": jnp.stack([rows(l, off_down, rf) for l in range(depth)]),
    }
    for n in BIG:
        grad_out[n] = big_g[n]
        delta[n], new_m[n], new_v[n] = adamw(big_g[n], w[n], m[n], v[n], "adamw_" + n)

    packed = gather_finish(small_started, delta["w_down"], "ag_wait_small_grads")
    summed = sum_lead(packed, tuple(range(N_DEV)), F32, "add8_small")
    small_g = dict(zip(small_names, _unpack(summed, shapes)))
    for n in SMALL_SHARDED:
        wdt = w[n].shape[-1]
        small_g[n] = lax.dynamic_slice_in_dim(small_g[n], my_idx * wdt, wdt, axis=small_g[n].ndim - 1)

    direct = ("lru_wa", "lru_wx")
    packed_names = [n for n in small_names if n not in direct]
    sshapes = [w[n].shape for n in packed_names]
    ds, ms, vs = adamw(_pack([small_g[n] for n in packed_names]), _pack([w[n] for n in packed_names]),
                       _pack([m[n] for n in packed_names]), _pack([v[n] for n in packed_names]), "adamw_small")
    for n, d_, m_, v_ in zip(packed_names, _unpack(ds, sshapes), _unpack(ms, sshapes), _unpack(vs, sshapes)):
        grad_out[n], delta[n], new_m[n], new_v[n] = small_g[n], d_, m_, v_
    for n in direct:
        grad_out[n] = small_g[n]
        delta[n], new_m[n], new_v[n] = adamw(small_g[n], w[n], m[n], v[n], "adamw_lru")

    return (loss, dx.reshape(1, t, d), *[grad_out[n] for n in WEIGHTS], *[delta[n] for n in WEIGHTS],
            *[new_m[n] for n in WEIGHTS], *[new_v[n] for n in WEIGHTS])


def kernel(x, mem, rel_bias, w_in, w_out, attn_sink, gla_w2_f, gla_b2_f, gla_w2_b, gla_b2_b, gla_norm, conv_w, conv_b, lru_wa, lru_ba, lru_wx, lru_bx, lru_lambda, xq, xk, xv, xo, w_up, w_down, norm_mix_pre, norm_mix_post, norm_mem, norm_x_pre, norm_x_post, norm_ff_pre, norm_ff_post, loss_target, m_rel_bias, m_w_in, m_w_out, m_attn_sink, m_gla_w2_f, m_gla_b2_f, m_gla_w2_b, m_gla_b2_b, m_gla_norm, m_conv_w, m_conv_b, m_lru_wa, m_lru_ba, m_lru_wx, m_lru_bx, m_lru_lambda, m_xq, m_xk, m_xv, m_xo, m_w_up, m_w_down, m_norm_mix_pre, m_norm_mix_post, m_norm_mem, m_norm_x_pre, m_norm_x_post, m_norm_ff_pre, m_norm_ff_post, v_rel_bias, v_w_in, v_w_out, v_attn_sink, v_gla_w2_f, v_gla_b2_f, v_gla_w2_b, v_gla_b2_b, v_gla_norm, v_conv_w, v_conv_b, v_lru_wa, v_lru_ba, v_lru_wx, v_lru_bx, v_lru_lambda, v_xq, v_xk, v_xv, v_xo, v_w_up, v_w_down, v_norm_mix_pre, v_norm_mix_post, v_norm_mem, v_norm_x_pre, v_norm_x_post, v_norm_ff_pre, v_norm_ff_post):
    given = dict(locals())
    w = {n: given[n] for n in WEIGHTS}
    m = {n: given["m_" + n] for n in WEIGHTS}
    v = {n: given["v_" + n] for n in WEIGHTS}
    return _step(x, mem, loss_target, w, m, v)
```

```python
import math

import jax
import jax.numpy as jnp
import numpy as np
from jax import lax
from jax.experimental import pallas as pl
from jax.experimental.pallas import tpu as pltpu

F32 = jnp.float32
BF16 = jnp.bfloat16
HI = lax.Precision.HIGHEST
NN = (((1,), (0,)), ((), ()))
NT = (((1,), (1,)), ((), ()))
MESH = pl.DeviceIdType.MESH
AXES = ("x", "y", "c")
N_DEV = 8

A_HEAD_DIM = 128
A_HEADS = 8
A_KV_HEADS = 2
A_GROUP = 4
WINDOW = 128
BLOCK = 128
N_BUCKETS = 32
MAX_DISTANCE = 128
B_HEADS = 4
B_KEY_DIM = 64
B_VAL_DIM = 128
GATE_RANK = 16
GATE_TAU = 16.0
GLA_CHUNK = 16
C_WIDTH = 512
C_BLOCKS = 4
C_BLOCK_DIM = 128
LRU_C = 8.0
X_HEADS = 4
EPS = 1e-6
NEG_INF = -1e30
A_Q = A_HEADS * A_HEAD_DIM
A_KV = A_KV_HEADS * A_HEAD_DIM
B_QK = B_HEADS * B_KEY_DIM
B_V = B_HEADS * B_VAL_DIM
SPLIT_SIZES = (A_Q, A_KV, A_KV, B_QK, B_QK, B_V, B_V, GATE_RANK, GATE_RANK, C_WIDTH, C_WIDTH)
D_IN = sum(SPLIT_SIZES)
D_MIX = A_Q + B_V + C_WIDTH
W_IN_SHARD = D_IN // N_DEV
W_IN_ROWS = 768
GLA_TILE = 128
CHUNKS_PER_TILE = GLA_TILE // GLA_CHUNK
EXP_CLAMP = 80.0

ADAM_LR = 0.001
ADAM_B1 = 0.9
ADAM_B2 = 0.999
ADAM_EPS = 1e-08
ADAM_WD = 0.01
ADAM_STEP = 10

VMEM_LIMIT_BYTES = 52 * 1024 * 1024
MM_TILE = 1024
NORM_TILE_ELEMS = 1 << 19
SCAN_UNROLL = 8
EARLY_PASS_FROM_LAYER = 2
SIBLING_STREAMS = 16
PACK_ELEMS = 128 * 2048


def _call(body, **kw):
    return pl.pallas_call(body, **kw)


def _cparams():
    return pltpu.CompilerParams(vmem_limit_bytes=VMEM_LIMIT_BYTES)


def _dot(a, b, dims=NN, hi=False):
    if hi:
        return lax.dot_general(a, b, dims, precision=HI, preferred_element_type=F32)
    return lax.dot_general(a.astype(BF16), b.astype(BF16), dims, preferred_element_type=F32)


def _sds(shape, dtype=F32):
    return jax.ShapeDtypeStruct(tuple(shape), dtype)


def _row_tile(rows, cols, target_elems=1 << 18):
    want = max(8, target_elems // max(cols, 1))
    if rows <= want:
        return rows
    t = (want // 8) * 8
    while t >= 8:
        if rows % t == 0:
            return t
        t -= 8
    return rows


def _expm1(x):
    poly = x * (1.0 + x * (1.0 / 2 + x * (1.0 / 6 + x * (1.0 / 24 + x * (1.0 / 120 + x * (
        1.0 / 720 + x * (1.0 / 5040 + x * (1.0 / 40320))))))))
    return jnp.where(jnp.abs(x) < 0.3, poly, jnp.exp(x) - 1.0)


def _log1p(e):
    w = 1.0 + e
    return jnp.where(w == 1.0, e, jnp.log(w) * e / (w - 1.0))


def _softplus(x):
    return jnp.maximum(x, 0.0) + _log1p(jnp.exp(-jnp.abs(x)))


def _sigmoid(x):
    return jax.nn.sigmoid(x)


GELU_K = math.sqrt(2.0 / math.pi)


def _gelu(y):
    t = jnp.tanh(GELU_K * (y + 0.044715 * y * y * y))
    return 0.5 * y * (1.0 + t)


def _gelu_grad(y):
    t = jnp.tanh(GELU_K * (y + 0.044715 * y * y * y))
    return 0.5 * (1.0 + t) + 0.5 * y * (1.0 - t * t) * GELU_K * (1.0 + 3 * 0.044715 * y * y)


def rms_fwd(x, g, name):
    m, d = x.shape
    tm = _row_tile(m, d, NORM_TILE_ELEMS)

    def body(x_ref, g_ref, o_ref):
        xv = x_ref[...]
        r = lax.rsqrt(jnp.mean(xv * xv, axis=1, keepdims=True) + EPS)
        o_ref[...] = (xv * r * g_ref[...]).astype(o_ref.dtype)

    return _call(body, name=name, grid=(m // tm,),
                 in_specs=[pl.BlockSpec((tm, d), lambda i: (i, 0)), pl.BlockSpec((1, d), lambda i: (0, 0))],
                 out_specs=pl.BlockSpec((tm, d), lambda i: (i, 0)),
                 out_shape=_sds((m, d), BF16))(x, g.reshape(1, d))


def resid_rms(xres, mid, g_post, g_pre, name):
    m, d = xres.shape
    tm = _row_tile(m, d, NORM_TILE_ELEMS)
    with_pre = g_pre is not None

    def body(*refs):
        if with_pre:
            x_ref, m_ref, gp_ref, gn_ref, xo_ref, h_ref = refs
        else:
            x_ref, m_ref, gp_ref, xo_ref = refs
        mv = m_ref[...]
        r = lax.rsqrt(jnp.mean(mv * mv, axis=1, keepdims=True) + EPS)
        xn = x_ref[...] + mv * r * gp_ref[...]
        xo_ref[...] = xn
        if with_pre:
            r2 = lax.rsqrt(jnp.mean(xn * xn, axis=1, keepdims=True) + EPS)
            h_ref[...] = (xn * r2 * gn_ref[...]).astype(h_ref.dtype)

    row = pl.BlockSpec((tm, d), lambda i: (i, 0))
    vec = pl.BlockSpec((1, d), lambda i: (0, 0))
    ins = [xres, mid, g_post.reshape(1, d)] + ([g_pre.reshape(1, d)] if with_pre else [])
    in_specs = [row, row, vec] + ([vec] if with_pre else [])
    if with_pre:
        return _call(body, name=name, grid=(m // tm,), in_specs=in_specs, out_specs=(row, row),
                     out_shape=(_sds((m, d)), _sds((m, d), BF16)))(*ins)
    return _call(body, name=name, grid=(m // tm,), in_specs=in_specs, out_specs=row,
                 out_shape=_sds((m, d)))(*ins)


def rms_bwd(x, g, dy, name, dy2=None, add=None):
    m, d = x.shape
    tm = _row_tile(m, d, NORM_TILE_ELEMS)
    has2, hasadd = dy2 is not None, add is not None

    def body(*refs):
        it = iter(refs)
        x_ref, g_ref, dy_ref = next(it), next(it), next(it)
        dy2_ref = next(it) if has2 else None
        add_ref = next(it) if hasadd else None
        dx_ref, dg_ref = next(it), next(it)
        xv = x_ref[...]
        dyv = dy_ref[...]
        if has2:
            dyv = dyv + dy2_ref[...]
        r = lax.rsqrt(jnp.mean(xv * xv, axis=1, keepdims=True) + EPS)
        xh = xv * r
        dxh = dyv * g_ref[...]
        dx = r * (dxh - xh * jnp.mean(dxh * xh, axis=1, keepdims=True))
        if hasadd:
            dx = dx + add_ref[...]
        dx_ref[...] = dx
        part = jnp.sum(dyv * xh, axis=0, keepdims=True)

        @pl.when(pl.program_id(0) == 0)
        def _():
            dg_ref[...] = part

        @pl.when(pl.program_id(0) > 0)
        def _():
            dg_ref[...] += part

    row = pl.BlockSpec((tm, d), lambda i: (i, 0))
    vec = pl.BlockSpec((1, d), lambda i: (0, 0))
    ins = [x, g.reshape(1, d), dy] + ([dy2] if has2 else []) + ([add] if hasadd else [])
    in_specs = [row, vec, row] + ([row] if has2 else []) + ([row] if hasadd else [])
    return _call(body, name=name, grid=(m // tm,), in_specs=in_specs, out_specs=(row, vec),
                 out_shape=(_sds((m, d)), _sds((1, d))))(*ins)


def loss_and_grad(y, target, name):
    m, d = y.shape
    tm = _row_tile(m, d, NORM_TILE_ELEMS)

    def body(y_ref, t_ref, dy_ref, l_ref):
        e = y_ref[...] - t_ref[...]
        dy_ref[...] = e * (1.0 / d)
        s = jnp.sum(jnp.sum(e * e, axis=1, keepdims=True), axis=0, keepdims=True) * (0.5 / d)
        part = jnp.broadcast_to(s, (1, 128))

        @pl.when(pl.program_id(0) == 0)
        def _():
            l_ref[...] = part

        @pl.when(pl.program_id(0) > 0)
        def _():
            l_ref[...] += part

    row = pl.BlockSpec((tm, d), lambda i: (i, 0))
    dy, l = _call(body, name=name, grid=(m // tm,), in_specs=[row, row],
                  out_specs=(row, pl.BlockSpec((1, 128), lambda i: (0, 0))),
                  out_shape=(_sds((m, d)), _sds((1, 128))))(y, target)
    return dy, l[0, 0]


def adamw(g, w, m, v, name):
    shape = w.shape
    cols = shape[-1]
    rows = int(np.prod(shape[:-1]))
    tm = _row_tile(rows, cols)
    c1 = 1.0 - ADAM_B1 ** ADAM_STEP
    c2 = 1.0 - ADAM_B2 ** ADAM_STEP

    def body(g_ref, w_ref, m_ref, v_ref, d_ref, mo_ref, vo_ref):
        gv = g_ref[...]
        mn = ADAM_B1 * m_ref[...] + (1.0 - ADAM_B1) * gv
        vn = ADAM_B2 * v_ref[...] + (1.0 - ADAM_B2) * (gv * gv)
        m_hat = mn / c1
        v_hat = vn / c2
        d_ref[...] = -ADAM_LR * (m_hat / (jnp.sqrt(v_hat) + ADAM_EPS) + ADAM_WD * w_ref[...])
        mo_ref[...] = mn
        vo_ref[...] = vn

    row = pl.BlockSpec((tm, cols), lambda i: (i, 0))
    outs = _call(body, name=name, grid=(rows // tm,), in_specs=[row] * 4, out_specs=(row,) * 3,
                 out_shape=(_sds((rows, cols)),) * 3)(*[a.reshape(rows, cols) for a in (g, w, m, v)])
    return tuple(o.reshape(shape) for o in outs)


def sum_lead(x, order, out_dtype, name):
    n, rows, cols = x.shape
    tm = _row_tile(rows, cols)

    def body(x_ref, o_ref):
        acc = x_ref[order[0]].astype(F32)
        for i in order[1:]:
            acc = acc + x_ref[i].astype(F32)
        o_ref[...] = acc.astype(out_dtype)

    return _call(body, name=name, grid=(rows // tm,), in_specs=[pl.BlockSpec((n, tm, cols), lambda i: (0, i, 0))],
                 out_specs=pl.BlockSpec((tm, cols), lambda i: (i, 0)), out_shape=_sds((rows, cols), out_dtype))(x)


def add_own_lead(own, parts, name):
    n, rows, cols = parts.shape
    tm = _row_tile(rows, cols)

    def body(o_ref, p_ref, out_ref):
        acc = o_ref[...].astype(F32)
        for i in range(n):
            acc = acc + p_ref[i].astype(F32)
        out_ref[...] = acc

    row = pl.BlockSpec((tm, cols), lambda i: (i, 0))
    return _call(body, name=name, grid=(rows // tm,),
                 in_specs=[row, pl.BlockSpec((n, tm, cols), lambda i: (0, i, 0))], out_specs=row,
                 out_shape=_sds((rows, cols)))(own, parts)


def add_own_half(g5, theirs, name):
    _, _, r, c = g5.shape
    tm = _row_tile(r, c, 1 << 20)

    def body(cc_ref, g_ref, t_ref, o_ref):
        o_ref[...] = (g_ref[...].astype(F32) + t_ref[...].astype(F32)).astype(o_ref.dtype)

    grid_spec = pltpu.PrefetchScalarGridSpec(
        num_scalar_prefetch=1, grid=(4, r // tm),
        in_specs=[pl.BlockSpec((None, None, tm, c), lambda q, i, cc_ref: (q, cc_ref[0], i, 0)),
                  pl.BlockSpec((None, tm, c), lambda q, i, cc_ref: (q, i, 0))],
        out_specs=pl.BlockSpec((None, tm, c), lambda q, i, cc_ref: (q, i, 0)))
    return _call(body, name=name, grid_spec=grid_spec, out_shape=_sds((4, r, c), BF16))(
        lax.axis_index("c").astype(jnp.int32).reshape(1), g5, theirs)


def add_n(xs, out_dtype, name):
    shape = xs[0].shape
    cols = shape[-1]
    rows = int(np.prod(shape[:-1]))
    tm = _row_tile(rows, cols)
    n = len(xs)

    def body(*refs):
        acc = refs[0][...].astype(F32)
        for r in refs[1:n]:
            acc = acc + r[...].astype(F32)
        refs[n][...] = acc.astype(out_dtype)

    row = pl.BlockSpec((tm, cols), lambda i: (i, 0))
    out = _call(body, name=name, grid=(rows // tm,), in_specs=[row] * n, out_specs=row,
                out_shape=_sds((rows, cols), out_dtype))(*[a.reshape(rows, cols) for a in xs])
    return out.reshape(shape)


def mm_plain(a, b, name, ta=False, tb=False, out_dtype=F32, hi=False, tm=MM_TILE, tn=MM_TILE):
    k, m = a.shape[::1 if ta else -1]
    n = b.shape[0] if tb else b.shape[1]
    tm, tn = min(tm, m), min(tn, n)
    dims = (((0 if ta else 1,), (1 if tb else 0,)), ((), ()))

    def body(a_ref, b_ref, o_ref):
        o_ref[...] = _dot(a_ref[...], b_ref[...], dims, hi).astype(out_dtype)

    a_spec = pl.BlockSpec((k, tm), lambda j, i: (0, i)) if ta else pl.BlockSpec((tm, k), lambda j, i: (i, 0))
    b_spec = pl.BlockSpec((tn, k), lambda j, i: (j, 0)) if tb else pl.BlockSpec((k, tn), lambda j, i: (0, j))
    return _call(body, name=name, grid=(n // tn, m // tm), in_specs=[a_spec, b_spec],
                 out_specs=pl.BlockSpec((tm, tn), lambda j, i: (i, j)),
                 out_shape=_sds((m, n), out_dtype), compiler_params=_cparams())(a, b)


def mm_dw_into(a, b, buf, off, r, name, tn=MM_TILE):
    k, m = a.shape
    n = b.shape[1]
    tm, tn = min(MM_TILE, r), min(tn, n)
    assert m == N_DEV * r and off % tm == 0 and r % tm == 0
    per = r // tm
    dims = (((0,), (0,)), ((), ()))

    def body(a_ref, b_ref, buf_ref, o_ref):
        o_ref[...] = _dot(a_ref[...], b_ref[...], dims).astype(o_ref.dtype)

    return _call(body, name=name, grid=(n // tn, m // tm),
                 in_specs=[pl.BlockSpec((k, tm), lambda j, i: (0, i)), pl.BlockSpec((k, tn), lambda j, i: (0, j)),
                           pl.BlockSpec(memory_space=pl.ANY)],
                 out_specs=pl.BlockSpec((None, tm, tn), lambda j, i: (i // per, off // tm + i % per, j)),
                 out_shape=_sds(buf.shape, buf.dtype), input_output_aliases={2: 0},
                 compiler_params=_cparams())(a, b, buf)


def mm_wk(a, gw, off, r, name, jb=N_DEV, tm=MM_TILE, tn=MM_TILE):
    m = a.shape[0]
    d = gw.shape[2]
    tm, tn = min(tm, m), min(tn, d)
    nk = N_DEV // jb
    ob = off // r
    assert off % r == 0 and a.shape[1] == N_DEV * r

    def body(a_ref, b_ref, o_ref, *acc):
        av = a_ref[...].astype(BF16)
        p = _dot(av[:, 0:r], b_ref[0])
        for q in range(1, jb):
            p = p + _dot(av[:, q * r:(q + 1) * r], b_ref[q])
        if nk == 1:
            o_ref[...] = p
        else:
            kk = pl.program_id(2)

            @pl.when(kk == 0)
            def _():
                acc[0][...] = p

            @pl.when(kk > 0)
            def _():
                acc[0][...] += p

            @pl.when(kk == nk - 1)
            def _():
                o_ref[...] = acc[0][...]

    return _call(body, name=name, grid=(m // tm, d // tn, nk),
                 in_specs=[pl.BlockSpec((tm, jb * r), lambda i, j, k: (i, k)),
                           pl.BlockSpec((jb, r, tn), lambda i, j, k: (k, ob, j))],
                 out_specs=pl.BlockSpec((tm, tn), lambda i, j, k: (i, j)),
                 out_shape=_sds((m, d)),
                 scratch_shapes=([pltpu.VMEM((tm, tn), F32)] if nk > 1 else []),
                 compiler_params=_cparams())(a, gw)


def mm_wn(a, gw, off, r, name, relu_grad_of=None, out_dtype=F32, with_relu2=False, tm=MM_TILE):
    m, d = a.shape
    tm = min(tm, m)
    ob = off // r
    assert off % r == 0 and gw.shape[2] == d
    epi = relu_grad_of is not None

    def body(*refs):
        it = iter(refs)
        a_ref, b_ref = next(it), next(it)
        e_ref = next(it) if epi else None
        o_ref = next(it)
        p = _dot(a_ref[...], b_ref[...], NT)
        if epi:
            p = p * (2.0 * jnp.maximum(e_ref[...], 0.0))
        o_ref[...] = p.astype(out_dtype)
        if with_relu2:
            act_ref = next(it)
            act_ref[...] = jnp.square(jnp.maximum(p, 0.0)).astype(act_ref.dtype)

    blk = pl.BlockSpec((tm, r), lambda i, j: (i, j))
    in_specs = [pl.BlockSpec((tm, d), lambda i, j: (i, 0)), pl.BlockSpec((None, r, d), lambda i, j: (j, ob, 0))]
    ins = [a, gw]
    if epi:
        in_specs.append(blk)
        ins.append(relu_grad_of)
    out_shape = _sds((m, N_DEV * r), out_dtype)
    if with_relu2:
        return _call(body, name=name, grid=(m // tm, N_DEV), in_specs=in_specs, out_specs=(blk, blk),
                     out_shape=(out_shape, _sds((m, N_DEV * r), BF16)), compiler_params=_cparams())(*ins)
    return _call(body, name=name, grid=(m // tm, N_DEV), in_specs=in_specs, out_specs=blk,
                 out_shape=out_shape, compiler_params=_cparams())(*ins)


def blockdiag_dw(xt, dz, name):
    t = xt.shape[1]

    def body(a_ref, b_ref, o_ref):
        o_ref[...] = _dot(a_ref[...], b_ref[...])

    return _call(body, name=name, grid=(C_BLOCKS,),
                 in_specs=[pl.BlockSpec((C_BLOCK_DIM, t), lambda g: (g, 0)),
                           pl.BlockSpec((t, C_BLOCK_DIM), lambda g: (0, g))],
                 out_specs=pl.BlockSpec((None, C_BLOCK_DIM, C_BLOCK_DIM), lambda g: (g, 0, 0)),
                 out_shape=_sds((C_BLOCKS, C_BLOCK_DIM, C_BLOCK_DIM)))(xt, dz)


def _band_mask(n, nblk, transposed):
    shape = (3 * BLOCK, A_GROUP * BLOCK) if transposed else (A_GROUP * BLOCK, 3 * BLOCK)
    qi = lax.broadcasted_iota(jnp.int32, shape, 1 if transposed else 0) & (BLOCK - 1)
    kj = lax.broadcasted_iota(jnp.int32, shape, 0 if transposed else 1)
    lo = jnp.where(n > 0, 0, BLOCK)
    hi = jnp.where(n < nblk - 1, 3 * BLOCK, 2 * BLOCK)
    return (jnp.abs(kj - BLOCK - qi) <= WINDOW) & (kj >= lo) & (kj < hi)


def _band_rows(ref, n, nblk):
    starts = [jnp.maximum(n - 1, 0), n, jnp.minimum(n + 1, nblk - 1)]
    return jnp.concatenate([ref[pl.ds(pl.multiple_of(s * BLOCK, BLOCK), BLOCK), :] for s in starts], axis=0)


def _head_cols(j):
    return slice(j * A_HEAD_DIM, (j + 1) * A_HEAD_DIM)


def attn_fwd(q, k, v, bias, sink_b, name):
    t = q.shape[0]
    nblk = t // BLOCK
    scale = A_HEAD_DIM ** -0.5

    def body(q_ref, k_ref, v_ref, b_ref, s_ref, o_ref):
        n = pl.program_id(1)
        kb = _band_rows(k_ref, n, nblk).astype(BF16)
        vb = _band_rows(v_ref, n, nblk).astype(BF16)
        mask = _band_mask(n, nblk, False)
        q4 = jnp.concatenate([q_ref[:, _head_cols(j)] for j in range(A_GROUP)], axis=0)
        b4 = jnp.concatenate([b_ref[j] for j in range(A_GROUP)], axis=0)
        sk = jnp.concatenate([jnp.broadcast_to(s_ref[j:j + 1, 0:1], (BLOCK, 1)) for j in range(A_GROUP)], axis=0)
        s = jnp.where(mask, _dot(q4, kb, NT) * scale + b4, NEG_INF)
        mx = jnp.maximum(jnp.max(s, axis=1, keepdims=True), sk)
        p = jnp.exp(s - mx)
        den = jnp.sum(p, axis=1, keepdims=True) + jnp.exp(sk - mx)
        o4 = _dot(p * (1.0 / den), vb)
        for j in range(A_GROUP):
            o_ref[:, _head_cols(j)] = o4[j * BLOCK:(j + 1) * BLOCK, :]

    gw = A_GROUP * A_HEAD_DIM
    return _call(body, name=name, grid=(A_KV_HEADS, nblk),
                 in_specs=[pl.BlockSpec((BLOCK, gw), lambda g, n: (n, g)),
                           pl.BlockSpec((t, A_HEAD_DIM), lambda g, n: (0, g)),
                           pl.BlockSpec((t, A_HEAD_DIM), lambda g, n: (0, g)),
                           pl.BlockSpec((A_GROUP, BLOCK, 3 * BLOCK), lambda g, n: (g, 0, 0)),
                           pl.BlockSpec((None, 8, 128), lambda g, n: (g, 0, 0))],
                 out_specs=pl.BlockSpec((BLOCK, gw), lambda g, n: (n, g)),
                 out_shape=_sds((t, A_Q)))(q, k, v, bias, sink_b)


def attn_bwd(q, k, v, bias, bias_t, sink_b, do, o, name):
    t = q.shape[0]
    nblk = t // BLOCK
    scale = A_HEAD_DIM ** -0.5

    def body(q_ref, k_ref, v_ref, b_ref, bt_ref, s_ref, do_ref, o_ref, dq_ref, dk_ref, dv_ref, db_ref, ds_ref):
        n = pl.program_id(1)

        @pl.when(n == 0)
        def _():
            dk_ref[...] = jnp.zeros_like(dk_ref)
            dv_ref[...] = jnp.zeros_like(dv_ref)
            db_ref[...] = jnp.zeros_like(db_ref)
            ds_ref[...] = jnp.zeros_like(ds_ref)

        kb = _band_rows(k_ref, n, nblk).astype(BF16)
        vb = _band_rows(v_ref, n, nblk).astype(BF16)
        heads = range(A_GROUP)
        mask = _band_mask(n, nblk, False)
        mask_t = _band_mask(n, nblk, True)
        q4 = jnp.concatenate([q_ref[:, _head_cols(j)] for j in heads], axis=0).astype(BF16)
        do4 = jnp.concatenate([do_ref[:, _head_cols(j)] for j in heads], axis=0)
        doo = do4 * jnp.concatenate([o_ref[:, _head_cols(j)] for j in heads], axis=0)
        do4 = do4.astype(BF16)
        b4 = jnp.concatenate([b_ref[j] for j in heads], axis=0)
        bt4 = jnp.concatenate([bt_ref[j] for j in heads], axis=1)
        sk = jnp.concatenate([jnp.broadcast_to(s_ref[j:j + 1, 0:1], (BLOCK, 1)) for j in heads], axis=0)
        sk_t = jnp.concatenate([jnp.broadcast_to(s_ref[j:j + 1, 0:1], (1, BLOCK)) for j in heads], axis=1)
        s = jnp.where(mask, _dot(q4, kb, NT) * scale + b4, NEG_INF)
        mx = jnp.maximum(jnp.max(s, axis=1, keepdims=True), sk)
        p = jnp.exp(s - mx)
        den = jnp.sum(p, axis=1, keepdims=True) + jnp.exp(sk - mx)
        rden = 1.0 / den
        p = p * rden
        psink_delta = jnp.exp(sk - mx) * rden * jnp.sum(doo, axis=1, keepdims=True)
        dsc = p * (_dot(do4, vb, NT) - jnp.sum(doo, axis=1, keepdims=True))
        dq4 = _dot(dsc, kb) * scale
        for j in heads:
            rows = slice(j * BLOCK, (j + 1) * BLOCK)
            db_ref[j] += dsc[rows, :]
            ds_ref[j:j + 1, :] += jnp.broadcast_to(-jnp.sum(psink_delta[rows, :], axis=0, keepdims=True), (1, 128))
            dq_ref[:, _head_cols(j)] = dq4[rows, :]
        st = jnp.where(mask_t, _dot(kb, q4, NT) * scale + bt4, NEG_INF)
        mxt = jnp.maximum(jnp.max(st, axis=0, keepdims=True), sk_t)
        pt = jnp.exp(st - mxt)
        dent = jnp.sum(pt, axis=0, keepdims=True) + jnp.exp(sk_t - mxt)
        pt = pt * (1.0 / dent)
        delta_t = _dot(jnp.ones((8, A_HEAD_DIM), F32), doo, NT, hi=True)[0:1, :]
        dst = pt * (_dot(vb, do4, NT) - delta_t)
        dkb = _dot(dst, q4) * scale
        dvb = _dot(pt, do4)
        starts = [jnp.maximum(n - 1, 0), n, jnp.minimum(n + 1, nblk - 1)]
        for c, st_ in enumerate(starts):
            rows = pl.ds(pl.multiple_of(st_ * BLOCK, BLOCK), BLOCK)
            dk_ref[rows, :] += dkb[c * BLOCK:(c + 1) * BLOCK, :]
            dv_ref[rows, :] += dvb[c * BLOCK:(c + 1) * BLOCK, :]

    gw = A_GROUP * A_HEAD_DIM
    qspec = pl.BlockSpec((BLOCK, gw), lambda g, n: (n, g))
    kspec = pl.BlockSpec((t, A_HEAD_DIM), lambda g, n: (0, g))
    sspec = pl.BlockSpec((None, 8, 128), lambda g, n: (g, 0, 0))
    bspec = pl.BlockSpec((A_GROUP, BLOCK, 3 * BLOCK), lambda g, n: (g, 0, 0))
    btspec = pl.BlockSpec((A_GROUP, 3 * BLOCK, BLOCK), lambda g, n: (g, 0, 0))
    return _call(body, name=name, grid=(A_KV_HEADS, nblk),
                 in_specs=[qspec, kspec, kspec, bspec, btspec, sspec, qspec, qspec],
                 out_specs=(qspec, kspec, kspec, bspec, sspec),
                 out_shape=(_sds((t, A_Q)), _sds((t, A_KV)), _sds((t, A_KV)),
                            _sds((A_HEADS, BLOCK, 3 * BLOCK)), _sds((A_KV_HEADS, 8, 128))),
                 compiler_params=_cparams())(q, k, v, bias, bias_t, sink_b, do, o)


def xattn_fwd(q, k, v, name):
    t, d = q.shape
    ml = k.shape[0]
    dh = d // X_HEADS
    tq = min(512, t)
    scale = dh ** -0.5

    def body(q_ref, k_ref, v_ref, o_ref):
        s = _dot(q_ref[...], k_ref[...], NT) * scale
        p = jnp.exp(s - jnp.max(s, axis=1, keepdims=True))
        p = p * (1.0 / jnp.sum(p, axis=1, keepdims=True))
        o_ref[...] = _dot(p, v_ref[...])

    qspec = pl.BlockSpec((tq, dh), lambda h, i: (i, h))
    kspec = pl.BlockSpec((ml, dh), lambda h, i: (0, h))
    return _call(body, name=name, grid=(X_HEADS, t // tq), in_specs=[qspec, kspec, kspec], out_specs=qspec,
                 out_shape=_sds((t, d)))(q, k, v)


def xattn_bwd(q, k, v, o, do, name):
    t, d = q.shape
    ml = k.shape[0]
    dh = d // X_HEADS
    tq = min(512, t)
    scale = dh ** -0.5

    def body(q_ref, k_ref, v_ref, o_ref, do_ref, dq_ref, dk_ref, dv_ref):
        i = pl.program_id(1)
        qv, kv, vv = q_ref[...].astype(BF16), k_ref[...].astype(BF16), v_ref[...].astype(BF16)
        dov = do_ref[...]
        doo = dov * o_ref[...]
        dov = dov.astype(BF16)
        s = _dot(qv, kv, NT) * scale
        p = jnp.exp(s - jnp.max(s, axis=1, keepdims=True))
        p = p * (1.0 / jnp.sum(p, axis=1, keepdims=True))
        ds = p * (_dot(dov, vv, NT) - jnp.sum(doo, axis=1, keepdims=True))
        dq_ref[...] = _dot(ds, kv) * scale
        st = _dot(kv, qv, NT) * scale
        pt = jnp.exp(st - jnp.max(st, axis=0, keepdims=True))
        pt = pt * (1.0 / jnp.sum(pt, axis=0, keepdims=True))
        delta_t = _dot(jnp.ones((8, dh), F32), doo, NT, hi=True)[0:1, :]
        dst = pt * (_dot(vv, dov, NT) - delta_t)
        dkp = _dot(dst, qv) * scale
        dvp = _dot(pt, dov)

        @pl.when(i == 0)
        def _():
            dk_ref[...] = dkp
            dv_ref[...] = dvp

        @pl.when(i > 0)
        def _():
            dk_ref[...] += dkp
            dv_ref[...] += dvp

    qspec = pl.BlockSpec((tq, dh), lambda h, i: (i, h))
    kspec = pl.BlockSpec((ml, dh), lambda h, i: (0, h))
    return _call(body, name=name, grid=(X_HEADS, t // tq), in_specs=[qspec, kspec, kspec, qspec, qspec],
                 out_specs=(qspec, kspec, kspec),
                 out_shape=(_sds((t, d)), _sds((ml, d)), _sds((ml, d))))(q, k, v, o, do)


def scan_lead(a, u, name, reverse, inclusive):
    n, r, c = a.shape
    blk = max(1, min(n, (1 << 18) // (max(r, 8) * c)))
    while n % blk:
        blk -= 1
    nb = n // blk

    def body(a_ref, u_ref, o_ref, carry):
        @pl.when(pl.program_id(0) == 0)
        def _():
            carry[...] = jnp.zeros_like(carry)

        def step(s, h):
            idx = (blk - 1 - s) if reverse else s
            hn = a_ref[idx] * h + u_ref[idx]
            o_ref[idx] = hn if inclusive else h
            return hn

        carry[...] = lax.fori_loop(0, blk, step, carry[...], unroll=min(blk, SCAN_UNROLL))

    spec = pl.BlockSpec((blk, r, c), (lambda i: (nb - 1 - i, 0, 0)) if reverse else (lambda i: (i, 0, 0)))
    return _call(body, name=name, grid=(nb,), in_specs=[spec, spec], out_specs=spec,
                 out_shape=_sds((n, r, c)), scratch_shapes=[pltpu.VMEM((r, c), F32)])(a, u)


def _chunk_mats(bwd_dir):
    i = lax.broadcasted_iota(jnp.int32, (GLA_TILE, GLA_TILE), 0)
    j = lax.broadcasted_iota(jnp.int32, (GLA_TILE, GLA_TILE), 1)
    same = lax.shift_right_logical(i, 4) == lax.shift_right_logical(j, 4)
    if bwd_dir:
        cm, cm_t = same & (j >= i), same & (i >= j)
        mk, mk_t = same & (j > i), same & (i > j)
    else:
        cm, cm_t = same & (j <= i), same & (i <= j)
        mk, mk_t = same & (j <= i), same & (i <= j)
    f = lambda b: jnp.where(b, 1.0, 0.0).astype(F32)
    return f(cm), f(cm_t), mk, mk_t, f(same)


def gla_gates_fwd(zf, zb, w2f, b2f, w2b, b2b, name):
    t = zf.shape[0]
    tm = min(256, t)

    def body(zf_ref, zb_ref, wf_ref, bf_ref, wb_ref, bb_ref, lf_ref, lb_ref):
        lf_ref[...] = -_softplus(-(_dot(zf_ref[...], wf_ref[...], hi=True) + bf_ref[...])) / GATE_TAU
        lb_ref[...] = -_softplus(-(_dot(zb_ref[...], wb_ref[...], hi=True) + bb_ref[...])) / GATE_TAU

    zs = pl.BlockSpec((tm, GATE_RANK), lambda i: (i, 0))
    ws = pl.BlockSpec((GATE_RANK, B_QK), lambda i: (0, 0))
    bs = pl.BlockSpec((1, B_QK), lambda i: (0, 0))
    os_ = pl.BlockSpec((tm, B_QK), lambda i: (i, 0))
    return _call(body, name=name, grid=(t // tm,), in_specs=[zs, zs, ws, bs, ws, bs], out_specs=(os_, os_),
                 out_shape=(_sds((t, B_QK)),) * 2)(zf, zb, w2f, b2f.reshape(1, B_QK), w2b, b2b.reshape(1, B_QK))


def gla_gates_bwd(zf, zb, w2f, b2f, w2b, b2b, dlf, dlb, name):
    t = zf.shape[0]
    tm = min(256, t)

    def body(zf_ref, zb_ref, wf_ref, bf_ref, wb_ref, bb_ref, dlf_ref, dlb_ref,
             dzf_ref, dzb_ref, dpf_ref, dpb_ref, dbf_ref, dbb_ref):
        first = pl.program_id(0) == 0
        for z_ref, w_ref, b_ref, dl_ref, dz_ref, dp_ref, db_ref in (
                (zf_ref, wf_ref, bf_ref, dlf_ref, dzf_ref, dpf_ref, dbf_ref),
                (zb_ref, wb_ref, bb_ref, dlb_ref, dzb_ref, dpb_ref, dbb_ref)):
            pre = _dot(z_ref[...], w_ref[...], hi=True) + b_ref[...]
            dpre = dl_ref[...] * (1.0 / GATE_TAU) * _sigmoid(-pre)
            dp_ref[...] = dpre
            dz_ref[...] = _dot(dpre, w_ref[...], NT, hi=True)
            part = jnp.sum(dpre, axis=0, keepdims=True)

            @pl.when(first)
            def _():
                db_ref[...] = part

            @pl.when(jnp.logical_not(first))
            def _():
                db_ref[...] += part

    zs = pl.BlockSpec((tm, GATE_RANK), lambda i: (i, 0))
    ws = pl.BlockSpec((GATE_RANK, B_QK), lambda i: (0, 0))
    bs = pl.BlockSpec((1, B_QK), lambda i: (0, 0))
    os_ = pl.BlockSpec((tm, B_QK), lambda i: (i, 0))
    return _call(body, name=name, grid=(t // tm,), in_specs=[zs, zs, ws, bs, ws, bs, os_, os_],
                 out_specs=(zs, zs, os_, os_, bs, bs),
                 out_shape=(_sds((t, GATE_RANK)),) * 2 + (_sds((t, B_QK)),) * 2 + (_sds((1, B_QK)),) * 2)(
        zf, zb, w2f, b2f.reshape(1, B_QK), w2b, b2b.reshape(1, B_QK), dlf, dlb)


def gla_outer(xt, lat, y, name, bwd_dir, mode):
    t = y.shape[0]
    nchunk = t // GLA_CHUNK
    khat = mode == "khat"
    scale = B_KEY_DIM ** -0.5

    def body(xt_ref, lat_ref, y_ref, *outs):
        _, cm_t, _, _, same = _chunk_mats(bwd_dir)
        lat_v = lat_ref[...]
        bt = _dot(lat_v, cm_t, hi=True)
        if khat:
            mult = jnp.exp(_dot(lat_v, same, hi=True) - bt)
        else:
            mult = jnp.exp(bt) * scale
        xm = xt_ref[...] * mult
        lane = lax.shift_right_logical(lax.broadcasted_iota(jnp.int32, (1, GLA_TILE), 1), 4)
        ones = jnp.ones((GLA_TILE, B_VAL_DIM), F32)
        yv = [y_ref[:, h * B_VAL_DIM:(h + 1) * B_VAL_DIM].astype(BF16) for h in range(B_HEADS)]
        for c in range(CHUNKS_PER_TILE):
            sel = lane == c
            xc = jnp.where(sel, xm, 0.0).astype(BF16)
            for h in range(B_HEADS):
                rows = slice(h * B_KEY_DIM, (h + 1) * B_KEY_DIM)
                outs[0][c, rows, :] = _dot(xc[rows, :], yv[h])
            if khat:
                outs[1][c] = jnp.exp(_dot(jnp.where(sel, lat_v, 0.0), ones, hi=True))

    tspec = pl.BlockSpec((B_QK, GLA_TILE), lambda i: (0, i))
    ospec = pl.BlockSpec((CHUNKS_PER_TILE, B_QK, B_VAL_DIM), lambda i: (i, 0, 0))
    oshape = _sds((nchunk, B_QK, B_VAL_DIM))
    return _call(body, name=name, grid=(t // GLA_TILE,),
                 in_specs=[tspec, tspec, pl.BlockSpec((GLA_TILE, B_V), lambda i: (i, 0))],
                 out_specs=(ospec, ospec) if khat else ospec,
                 out_shape=(oshape, oshape) if khat else oshape)(xt, lat, y)


def _head_lane_mask(h):
    lane = lax.broadcasted_iota(jnp.int32, (1, B_QK), 1)
    return lax.shift_right_logical(lane, 6) == h


def _chunk_rows(c):
    return slice(c * GLA_CHUNK, (c + 1) * GLA_CHUNK)


def gla_inner_fwd(q, k, v, la, sp, name, bwd_dir):
    t = q.shape[0]
    scale = B_KEY_DIM ** -0.5

    def body(q_ref, k_ref, v_ref, la_ref, sp_ref, o_ref):
        cm, _, mk, _, _ = _chunk_mats(bwd_dir)
        b = _dot(cm, la_ref[...], hi=True)
        qt = q_ref[...] * scale * jnp.exp(b)
        kt = k_ref[...] * jnp.exp(jnp.minimum(-b, EXP_CLAMP))
        spb = [sp_ref[c].astype(BF16) for c in range(CHUNKS_PER_TILE)]
        for h in range(B_HEADS):
            lm = _head_lane_mask(h)
            qm = jnp.where(lm, qt, 0.0).astype(BF16)
            km = jnp.where(lm, kt, 0.0).astype(BF16)
            vs = slice(h * B_VAL_DIM, (h + 1) * B_VAL_DIM)
            att = jnp.where(mk, _dot(qm, km, NT), 0.0)
            inter = jnp.concatenate([_dot(qm[_chunk_rows(c), :], spb[c]) for c in range(CHUNKS_PER_TILE)], axis=0)
            o_ref[:, vs] = _dot(att, v_ref[:, vs]) + inter

    qs = pl.BlockSpec((GLA_TILE, B_QK), lambda i: (i, 0))
    vs_ = pl.BlockSpec((GLA_TILE, B_V), lambda i: (i, 0))
    ss = pl.BlockSpec((CHUNKS_PER_TILE, B_QK, B_VAL_DIM), lambda i: (i, 0, 0))
    return _call(body, name=name, grid=(t // GLA_TILE,), in_specs=[qs, qs, vs_, qs, ss], out_specs=vs_,
                 out_shape=_sds((t, B_V)))(q, k, v, la, sp)


def gla_inner_bwd(q, k, v, la, do, sp, gs, dec, name, bwd_dir, add=None):
    t = q.shape[0]
    scale = B_KEY_DIM ** -0.5
    hasadd = add is not None

    def body(*refs):
        it = iter(refs)
        q_ref, k_ref, v_ref, la_ref, do_ref, sp_ref, gs_ref, dec_ref = [next(it) for _ in range(8)]
        adds = [next(it) for _ in range(3)] if hasadd else None
        dq_ref, dk_ref, dv_ref, dla_ref = [next(it) for _ in range(4)]
        cm, cm_t, mk, mk_t, same = _chunk_mats(bwd_dir)
        la_v = la_ref[...]
        b = _dot(cm, la_v, hi=True)
        btot = _dot(same, la_v, hi=True)
        eb = jnp.exp(b)
        ek = jnp.exp(jnp.minimum(-b, EXP_CLAMP))
        ekh = jnp.exp(btot - b)
        qt = q_ref[...] * scale * eb
        kt = k_ref[...] * ek
        kh = k_ref[...] * ekh
        spb = [sp_ref[c].astype(BF16) for c in range(CHUNKS_PER_TILE)]
        gsb = [gs_ref[c].astype(BF16) for c in range(CHUNKS_PER_TILE)]
        dqt = jnp.zeros((GLA_TILE, B_QK), F32)
        dkt = jnp.zeros((GLA_TILE, B_QK), F32)
        dkh = jnp.zeros((GLA_TILE, B_QK), F32)
        for h in range(B_HEADS):
            lm = _head_lane_mask(h)
            qm = jnp.where(lm, qt, 0.0).astype(BF16)
            km = jnp.where(lm, kt, 0.0).astype(BF16)
            khm = jnp.where(lm, kh, 0.0).astype(BF16)
            vs = slice(h * B_VAL_DIM, (h + 1) * B_VAL_DIM)
            vh = v_ref[:, vs].astype(BF16)
            doh = do_ref[:, vs].astype(BF16)
            da = jnp.where(mk, _dot(doh, vh, NT), 0.0)
            da_t = jnp.where(mk_t, _dot(vh, doh, NT), 0.0)
            att_t = jnp.where(mk_t, _dot(km, qm, NT), 0.0)
            dv_h = _dot(att_t, doh) + jnp.concatenate(
                [_dot(khm[_chunk_rows(c), :], gsb[c]) for c in range(CHUNKS_PER_TILE)], axis=0)
            if hasadd:
                dv_h = dv_h + adds[2][:, vs]
            dv_ref[:, vs] = dv_h
            dq_inter = jnp.concatenate(
                [_dot(doh[_chunk_rows(c), :], spb[c], NT) for c in range(CHUNKS_PER_TILE)], axis=0)
            dqt = dqt + _dot(da, km) + jnp.where(lm, dq_inter, 0.0)
            dkt = dkt + _dot(da_t, qm)
            dkh_inter = jnp.concatenate(
                [_dot(vh[_chunk_rows(c), :], gsb[c], NT) for c in range(CHUNKS_PER_TILE)], axis=0)
            dkh = dkh + jnp.where(lm, dkh_inter, 0.0)
        dq = dqt * scale * eb
        dk = dkt * ek + dkh * ekh
        if hasadd:
            dq = dq + adds[0][...]
            dk = dk + adds[1][...]
        dq_ref[...] = dq
        dk_ref[...] = dk
        db = dqt * qt - dkt * kt - dkh * kh
        ones16 = jnp.ones((GLA_CHUNK, B_VAL_DIM), F32)
        t2 = jnp.concatenate(
            [_dot(ones16, gs_ref[c] * dec_ref[c] * sp_ref[c], NT, hi=True) for c in range(CHUNKS_PER_TILE)], axis=0)
        dla_ref[...] = _dot(cm_t, db, hi=True) + _dot(same, dkh * kh, hi=True) + t2

    qs = pl.BlockSpec((GLA_TILE, B_QK), lambda i: (i, 0))
    vs_ = pl.BlockSpec((GLA_TILE, B_V), lambda i: (i, 0))
    ss = pl.BlockSpec((CHUNKS_PER_TILE, B_QK, B_VAL_DIM), lambda i: (i, 0, 0))
    ins = [q, k, v, la, do, sp, gs, dec] + (list(add) if hasadd else [])
    in_specs = [qs, qs, vs_, qs, vs_, ss, ss, ss] + ([qs, qs, vs_] if hasadd else [])
    return _call(body, name=name, grid=(t // GLA_TILE,), in_specs=in_specs, out_specs=(qs, qs, vs_, qs),
                 out_shape=(_sds((t, B_QK)), _sds((t, B_QK)), _sds((t, B_V)), _sds((t, B_QK))),
                 compiler_params=_cparams())(*ins)


def gla_out_fwd(of, ob, g, gn, name):
    t = of.shape[0]
    tm = min(256, t)

    def body(of_ref, ob_ref, g_ref, gn_ref, o_ref):
        for h in range(B_HEADS):
            vs = slice(h * B_VAL_DIM, (h + 1) * B_VAL_DIM)
            o = of_ref[:, vs] + ob_ref[:, vs]
            on = o * lax.rsqrt(jnp.mean(o * o, axis=1, keepdims=True) + EPS)
            gv = g_ref[:, vs]
            o_ref[:, vs] = on * gn_ref[:, vs] * (gv * _sigmoid(gv))

    row = pl.BlockSpec((tm, B_V), lambda i: (i, 0))
    vec = pl.BlockSpec((1, B_V), lambda i: (0, 0))
    return _call(body, name=name, grid=(t // tm,), in_specs=[row, row, row, vec], out_specs=row,
                 out_shape=_sds((t, B_V)))(of, ob, g, gn.reshape(1, B_V))


def gla_out_bwd(of, ob, g, gn, dout, name):
    t = of.shape[0]
    tm = min(256, t)

    def body(of_ref, ob_ref, g_ref, gn_ref, d_ref, do_ref, dg_ref, dgn_ref):
        first = pl.program_id(0) == 0
        for h in range(B_HEADS):
            vs = slice(h * B_VAL_DIM, (h + 1) * B_VAL_DIM)
            o = of_ref[:, vs] + ob_ref[:, vs]
            r = lax.rsqrt(jnp.mean(o * o, axis=1, keepdims=True) + EPS)
            on = o * r
            gv = g_ref[:, vs]
            sg = _sigmoid(gv)
            silu = gv * sg
            dv = d_ref[:, vs]
            gnv = gn_ref[:, vs]
            dg_ref[:, vs] = dv * on * gnv * (sg * (1.0 + gv * (1.0 - sg)))
            don = dv * silu * gnv
            do_ref[:, vs] = r * (don - on * jnp.mean(don * on, axis=1, keepdims=True))
            part = jnp.sum(dv * silu * on, axis=0, keepdims=True)

            @pl.when(first)
            def _():
                dgn_ref[:, vs] = part

            @pl.when(jnp.logical_not(first))
            def _():
                dgn_ref[:, vs] += part

    row = pl.BlockSpec((tm, B_V), lambda i: (i, 0))
    vec = pl.BlockSpec((1, B_V), lambda i: (0, 0))
    return _call(body, name=name, grid=(t // tm,), in_specs=[row, row, row, vec, row], out_specs=(row, row, vec),
                 out_shape=(_sds((t, B_V)), _sds((t, B_V)), _sds((1, B_V))))(of, ob, g, gn.reshape(1, B_V), dout)


def _shift(x, k):
    if k > 0:
        return jnp.concatenate([x[k:], jnp.zeros((k,) + x.shape[1:], x.dtype)], axis=0)
    return jnp.concatenate([jnp.zeros((-k,) + x.shape[1:], x.dtype), x[:k]], axis=0)


def _lru_gates(xc, s, wa_ref, ba_ref, wx_ref, bx_ref, lam_ref):
    cols = [slice(g * C_BLOCK_DIM, (g + 1) * C_BLOCK_DIM) for g in range(C_BLOCKS)]
    zr = jnp.concatenate([_dot(xc[:, cs], wa_ref[s, g]) for g, cs in enumerate(cols)], axis=1) + ba_ref[s:s + 1, :]
    zi = jnp.concatenate([_dot(xc[:, cs], wx_ref[s, g]) for g, cs in enumerate(cols)], axis=1) + bx_ref[s:s + 1, :]
    r = _sigmoid(zr)
    i = _sigmoid(zi)
    sp = _softplus(-lam_ref[s:s + 1, :])
    log_a = -LRU_C * r * sp
    return r, i, sp, log_a


def lru_gates_fwd(x0, xm2, xm1, xp1, cw, cb, wa, ba, wx, bx, lam, name):
    t = x0.shape[0]
    tm = min(256, t)

    def body(x0_ref, xm2_ref, xm1_ref, xp1_ref, cw_ref, cb_ref, wa_ref, ba_ref, wx_ref, bx_ref, lam_ref,
             xc_ref, a0_ref, u0_ref, a1_ref, u1_ref):
        xc = (xm2_ref[...] * cw_ref[0:1, :] + xm1_ref[...] * cw_ref[1:2, :] + x0_ref[...] * cw_ref[2:3, :]
              + xp1_ref[...] * cw_ref[3:4, :] + cb_ref[...])
        xc_ref[...] = xc
        for s, (a_ref, u_ref) in enumerate(((a0_ref, u0_ref), (a1_ref, u1_ref))):
            _, i, _, log_a = _lru_gates(xc, s, wa_ref, ba_ref, wx_ref, bx_ref, lam_ref)
            a_ref[...] = jnp.exp(log_a)
            u_ref[...] = jnp.sqrt(-_expm1(2.0 * log_a)) * (i * xc)

    row = pl.BlockSpec((tm, C_WIDTH), lambda i: (i, 0))
    full = lambda shape: pl.BlockSpec(shape, lambda i: (0,) * len(shape))
    wshape = (2, C_BLOCKS, C_BLOCK_DIM, C_BLOCK_DIM)
    return _call(body, name=name, grid=(t // tm,),
                 in_specs=[row] * 4 + [full((4, C_WIDTH)), full((1, C_WIDTH)), full(wshape), full((2, C_WIDTH)),
                                       full(wshape), full((2, C_WIDTH)), full((2, C_WIDTH))],
                 out_specs=(row,) * 5, out_shape=(_sds((t, C_WIDTH)),) * 5)(
        x0, xm2, xm1, xp1, cw, cb.reshape(1, C_WIDTH), wa, ba, wx, bx, lam)


def lru_gates_bwd(xc, g0, hs0, g1, hs1, wa, ba, wx, bx, lam, name):
    t = xc.shape[0]
    tm = min(256, t)

    def body(xc_ref, g0_ref, hs0_ref, g1_ref, hs1_ref, wa_ref, ba_ref, wx_ref, bx_ref, lam_ref,
             dxc_ref, dzr0_ref, dzi0_ref, dzr1_ref, dzi1_ref, dlam_ref, dba_ref, dbx_ref):
        first = pl.program_id(0) == 0

        @pl.when(first)
        def _():
            dlam_ref[...] = jnp.zeros_like(dlam_ref)
            dba_ref[...] = jnp.zeros_like(dba_ref)
            dbx_ref[...] = jnp.zeros_like(dbx_ref)

        xcv = xc_ref[...]
        dxc = jnp.zeros_like(xcv)
        cols = [slice(g * C_BLOCK_DIM, (g + 1) * C_BLOCK_DIM) for g in range(C_BLOCKS)]
        for s, (g_ref, hs_ref, dzr_ref, dzi_ref) in enumerate(
                ((g0_ref, hs0_ref, dzr0_ref, dzi0_ref), (g1_ref, hs1_ref, dzr1_ref, dzi1_ref))):
            r, i, sp, log_a = _lru_gates(xcv, s, wa_ref, ba_ref, wx_ref, bx_ref, lam_ref)
            du = g_ref[...]
            da = du * hs_ref[...]
            a = jnp.exp(log_a)
            e2 = jnp.exp(2.0 * log_a)
            c = jnp.sqrt(-_expm1(2.0 * log_a))
            ix = i * xcv
            dlog = da * a - du * ix * (e2 / c)
            dix = du * c
            dxc = dxc + dix * i
            dzi = dix * xcv * i * (1.0 - i)
            dzr = dlog * (-LRU_C * sp) * r * (1.0 - r)
            dzr_ref[...] = dzr
            dzi_ref[...] = dzi
            dxc = dxc + jnp.concatenate(
                [_dot(dzr[:, cs], wa_ref[s, g], NT) + _dot(dzi[:, cs], wx_ref[s, g], NT) for g, cs in enumerate(cols)],
                axis=1)
            dsp = jnp.sum(dlog * (-LRU_C * r), axis=0, keepdims=True)
            dlam_ref[s:s + 1, :] += dsp * (-_sigmoid(-lam_ref[s:s + 1, :]))
            dba_ref[s:s + 1, :] += jnp.sum(dzr, axis=0, keepdims=True)
            dbx_ref[s:s + 1, :] += jnp.sum(dzi, axis=0, keepdims=True)
        dxc_ref[...] = dxc

    row = pl.BlockSpec((tm, C_WIDTH), lambda i: (i, 0))
    full = lambda shape: pl.BlockSpec(shape, lambda i: (0,) * len(shape))
    wshape = (2, C_BLOCKS, C_BLOCK_DIM, C_BLOCK_DIM)
    vec2 = full((2, C_WIDTH))
    return _call(body, name=name, grid=(t // tm,),
                 in_specs=[row] * 5 + [full(wshape), vec2, full(wshape), vec2, vec2],
                 out_specs=(row,) * 5 + (vec2,) * 3,
                 out_shape=(_sds((t, C_WIDTH)),) * 5 + (_sds((2, C_WIDTH)),) * 3)(
        xc, g0, hs0, g1, hs1, wa, ba, wx, bx, lam)


def lru_out_fwd(h0, h1, y, name):
    t = y.shape[0]
    tm = min(256, t)

    def body(h0_ref, h1_ref, y_ref, o_ref):
        o_ref[...] = (h0_ref[...] + h1_ref[...]) * _gelu(y_ref[...])

    row = pl.BlockSpec((tm, C_WIDTH), lambda i: (i, 0))
    return _call(body, name=name, grid=(t // tm,), in_specs=[row] * 3, out_specs=row,
                 out_shape=_sds((t, C_WIDTH)))(h0, h1, y)


def lru_out_bwd(h0, h1, y, dout, name):
    t = y.shape[0]
    tm = min(256, t)

    def body(h0_ref, h1_ref, y_ref, d_ref, dh_ref, dy_ref):
        yv = y_ref[...]
        dv = d_ref[...]
        dh_ref[...] = dv * _gelu(yv)
        dy_ref[...] = dv * (h0_ref[...] + h1_ref[...]) * _gelu_grad(yv)

    row = pl.BlockSpec((tm, C_WIDTH), lambda i: (i, 0))
    return _call(body, name=name, grid=(t // tm,), in_specs=[row] * 4, out_specs=(row, row),
                 out_shape=(_sds((t, C_WIDTH)),) * 2)(h0, h1, y, dout)


def conv_bwd(dxc, dp2, dp1, dm1, x0, xm2, xm1, xp1, cw, name):
    t = x0.shape[0]
    tm = min(256, t)

    def body(d_ref, dp2_ref, dp1_ref, dm1_ref, x0_ref, xm2_ref, xm1_ref, xp1_ref, cw_ref, dx_ref, dcw_ref, dcb_ref):
        @pl.when(pl.program_id(0) == 0)
        def _():
            dcw_ref[...] = jnp.zeros_like(dcw_ref)
            dcb_ref[...] = jnp.zeros_like(dcb_ref)

        dv = d_ref[...]
        dx_ref[...] = (dp2_ref[...] * cw_ref[0:1, :] + dp1_ref[...] * cw_ref[1:2, :] + dv * cw_ref[2:3, :]
                       + dm1_ref[...] * cw_ref[3:4, :])
        for j, x_ref in enumerate((xm2_ref, xm1_ref, x0_ref, xp1_ref)):
            dcw_ref[j:j + 1, :] += jnp.sum(dv * x_ref[...], axis=0, keepdims=True)
        dcb_ref[...] += jnp.sum(dv, axis=0, keepdims=True)

    row = pl.BlockSpec((tm, C_WIDTH), lambda i: (i, 0))
    cws = pl.BlockSpec((4, C_WIDTH), lambda i: (0, 0))
    cbs = pl.BlockSpec((1, C_WIDTH), lambda i: (0, 0))
    return _call(body, name=name, grid=(t // tm,), in_specs=[row] * 8 + [cws], out_specs=(row, cws, cbs),
                 out_shape=(_sds((t, C_WIDTH)), _sds((4, C_WIDTH)), _sds((1, C_WIDTH))))(
        dxc, dp2, dp1, dm1, x0, xm2, xm1, xp1, cw)


def _my_place():
    return lax.axis_index("x"), lax.axis_index("y"), lax.axis_index("c")


def all_gather(xs, name):
    r, c = xs.shape

    def body(x_ref, out_ref, send_sems, recv_sems, local_sem):
        x, y, cc = _my_place()
        me, sibling = (x, y, cc), (x, y, 1 - cc)
        chips = [(1 - x, y), (x, 1 - y), (1 - x, 1 - y)]

        def slot(px, py, pc):
            return out_ref.at[4 * px + 2 * py + pc]

        def copy(k, block, to, src=None):
            return pltpu.make_async_remote_copy(
                src_ref=slot(*block) if src is None else src, dst_ref=slot(*block),
                send_sem=send_sems.at[k], recv_sem=recv_sems.at[k], device_id=to, device_id_type=MESH)

        mine = pltpu.make_async_copy(x_ref, slot(*me), local_sem)
        mine.start()
        first = [copy(0, me, sibling, src=x_ref)]
        first += [copy(1 + j, me, (*chip, cc), src=x_ref) for j, chip in enumerate(chips)]
        for cp in first:
            cp.start()
        passed = [copy(4 + j, (*chip, cc), sibling) for j, chip in enumerate(chips)]
        for j, chip in enumerate(chips):
            copy(1 + j, (*chip, cc), me).wait_recv()
            passed[j].start()
        copy(0, sibling, me).wait_recv()
        for j, chip in enumerate(chips):
            copy(4 + j, (*chip, 1 - cc), me).wait_recv()
        for cp in first + passed:
            cp.wait_send()
        mine.wait()

    return _call(body, name=name, in_specs=[pl.BlockSpec(memory_space=pl.ANY)],
                 out_specs=pl.BlockSpec(memory_space=pl.ANY), out_shape=_sds((N_DEV, r, c), xs.dtype),
                 scratch_shapes=[pltpu.SemaphoreType.DMA((7,)), pltpu.SemaphoreType.DMA((7,)),
                                 pltpu.SemaphoreType.DMA])(xs)


def _stream_rows(r):
    nch = SIBLING_STREAMS // 4 if r % (8 * (SIBLING_STREAMS // 4)) == 0 else 1
    return nch, r // nch


def exchange_sibling(gw, name):
    _, r, c = gw.shape
    g5 = gw.reshape(4, 2, r, c)
    nch, rows = _stream_rows(r)

    def body(g_ref, out_ref, send_sems, recv_sems):
        x, y, cc = _my_place()
        swaps = []
        for q in range(4):
            for s in range(nch):
                k = q * nch + s
                win = pl.ds(s * rows, rows)
                swaps.append(pltpu.make_async_remote_copy(
                    src_ref=g_ref.at[q, 1 - cc, win], dst_ref=out_ref.at[q, win], send_sem=send_sems.at[k],
                    recv_sem=recv_sems.at[k], device_id=(x, y, 1 - cc), device_id_type=MESH))
        for cp in swaps:
            cp.start()
        for cp in swaps:
            cp.wait()

    nsem = 4 * nch
    return _call(body, name=name, in_specs=[pl.BlockSpec(memory_space=pl.ANY)],
                 out_specs=pl.BlockSpec(memory_space=pl.ANY), out_shape=_sds((4, r, c), gw.dtype),
                 scratch_shapes=[pltpu.SemaphoreType.DMA((nsem,)), pltpu.SemaphoreType.DMA((nsem,))])(g5)


HBM_SPEC = pl.BlockSpec(memory_space=pltpu.HBM)
SEM_SPEC = pl.BlockSpec(memory_space=pltpu.SEMAPHORE)
DATAFLOW = pltpu.SideEffectType.DATAFLOW_SIDE_EFFECTING


def _hbm(a):
    return pltpu.with_memory_space_constraint(a, pltpu.HBM)


def _peers(x, y, cc):
    return [(x, y, 1 - cc), (1 - x, y, cc), (x, 1 - y, cc), (1 - x, 1 - y, cc)]


def _slot(p):
    return 4 * p[0] + 2 * p[1] + p[2]


def gather_start(blk, name):
    r, c = blk.shape

    def body(v_ref, land_ref, send_sems, recv_sems, v_thru, land_thru, token):
        x, y, cc = _my_place()
        for k, to in enumerate(_peers(x, y, cc)):
            pltpu.make_async_remote_copy(
                src_ref=v_ref, dst_ref=land_ref.at[_slot((x, y, cc))], send_sem=send_sems.at[k],
                recv_sem=recv_sems.at[k], device_id=to, device_id_type=MESH).start()
        pltpu.make_async_copy(v_ref, land_ref.at[_slot((x, y, cc))], send_sems.at[4]).start()
        token[...] = jnp.zeros_like(token)

    return _call(
        body, name=name,
        out_shape=(pltpu.SemaphoreType.DMA((5,)), pltpu.SemaphoreType.DMA((4,)), pltpu.HBM((r, c), blk.dtype),
                   pltpu.HBM((N_DEV, r, c), blk.dtype), _sds((8, 128))),
        in_specs=(HBM_SPEC, HBM_SPEC),
        out_specs=(SEM_SPEC, SEM_SPEC, HBM_SPEC, HBM_SPEC, pl.BlockSpec(memory_space=pltpu.VMEM)),
        input_output_aliases={0: 2, 1: 3},
        compiler_params=pltpu.CompilerParams(has_side_effects=DATAFLOW),
    )(_hbm(blk), _hbm(lax.empty((N_DEV, r, c), blk.dtype)))


def gather_wait(send_sems, recv_sems, v_thru, land_thru, after, name):
    def body(v_ref, land_ref, send_sems, recv_sems, after_ref, v_out, land_out):
        x, y, cc = _my_place()
        for k, peer in enumerate(_peers(x, y, cc)):
            cp = pltpu.make_async_remote_copy(
                src_ref=v_ref, dst_ref=land_ref.at[_slot(peer)], send_sem=send_sems.at[k], recv_sem=recv_sems.at[k],
                device_id=peer, device_id_type=MESH)
            cp.wait_send()
            cp.wait_recv()
        pltpu.make_async_copy(v_ref, land_ref.at[_slot((x, y, cc))], send_sems.at[4]).wait()

    return _call(
        body, name=name,
        out_shape=(pltpu.HBM(v_thru.shape, v_thru.dtype), pltpu.HBM(land_thru.shape, land_thru.dtype)),
        in_specs=(HBM_SPEC, HBM_SPEC, SEM_SPEC, SEM_SPEC, pl.BlockSpec(memory_space=pl.ANY)),
        out_specs=(HBM_SPEC, HBM_SPEC), input_output_aliases={0: 0, 1: 1},
        compiler_params=pltpu.CompilerParams(has_side_effects=DATAFLOW),
    )(v_thru, land_thru, send_sems, recv_sems, after)


def gather_pass(land, name):
    _, r, c = land.shape
    nch, rows = _stream_rows(r)

    def body(land_ref, out_ref, send_sems, recv_sems):
        x, y, cc = _my_place()
        peers = _peers(x, y, cc)
        copies = []
        for j in range(3):
            mine, theirs = _slot(peers[1 + j]), _slot((peers[1 + j][0], peers[1 + j][1], 1 - cc))
            for s in range(nch):
                k = j * nch + s
                win = pl.ds(s * rows, rows)
                send = pltpu.make_async_remote_copy(
                    src_ref=land_ref.at[mine, win], dst_ref=out_ref.at[mine, win], send_sem=send_sems.at[k],
                    recv_sem=recv_sems.at[k], device_id=peers[0], device_id_type=MESH)
                recv = pltpu.make_async_remote_copy(
                    src_ref=land_ref.at[mine, win], dst_ref=out_ref.at[theirs, win], send_sem=send_sems.at[k],
                    recv_sem=recv_sems.at[k], device_id=peers[0], device_id_type=MESH)
                copies.append((send, recv))
        for send, _ in copies:
            send.start()
        for send, recv in copies:
            send.wait_send()
            recv.wait_recv()

    nsem = 3 * nch
    return _call(body, name=name, in_specs=[pl.BlockSpec(memory_space=pl.ANY)],
                 out_specs=pl.BlockSpec(memory_space=pl.ANY), out_shape=_sds(land.shape, land.dtype),
                 input_output_aliases={0: 0},
                 scratch_shapes=[pltpu.SemaphoreType.DMA((nsem,)), pltpu.SemaphoreType.DMA((nsem,))])(land)


def pass_start(land, name):
    _, r, c = land.shape
    nch, rows = _stream_rows(r)
    nsem = 3 * nch

    def body(land_ref, send_sems, recv_sems, land_thru, token):
        x, y, cc = _my_place()
        peers = _peers(x, y, cc)
        for j in range(3):
            mine = _slot(peers[1 + j])
            for s in range(nch):
                win = pl.ds(s * rows, rows)
                pltpu.make_async_remote_copy(
                    src_ref=land_ref.at[mine, win], dst_ref=land_ref.at[mine, win], send_sem=send_sems.at[j * nch + s],
                    recv_sem=recv_sems.at[j * nch + s], device_id=peers[0], device_id_type=MESH).start()
        token[...] = jnp.zeros_like(token)

    return _call(
        body, name=name,
        out_shape=(pltpu.SemaphoreType.DMA((nsem,)), pltpu.SemaphoreType.DMA((nsem,)),
                   pltpu.HBM(land.shape, land.dtype), _sds((8, 128))),
        in_specs=(HBM_SPEC,),
        out_specs=(SEM_SPEC, SEM_SPEC, HBM_SPEC, pl.BlockSpec(memory_space=pltpu.VMEM)),
        input_output_aliases={0: 2},
        compiler_params=pltpu.CompilerParams(has_side_effects=DATAFLOW),
    )(_hbm(land))


def pass_wait(send_sems, recv_sems, land_thru, after, name):
    _, r, c = land_thru.shape
    nch, rows = _stream_rows(r)

    def body(land_ref, send_sems, recv_sems, after_ref, land_out):
        x, y, cc = _my_place()
        peers = _peers(x, y, cc)
        for j in range(3):
            mine, theirs = _slot(peers[1 + j]), _slot((peers[1 + j][0], peers[1 + j][1], 1 - cc))
            for s in range(nch):
                win = pl.ds(s * rows, rows)
                k = j * nch + s
                pltpu.make_async_remote_copy(
                    src_ref=land_ref.at[mine, win], dst_ref=land_ref.at[mine, win], send_sem=send_sems.at[k],
                    recv_sem=recv_sems.at[k], device_id=peers[0], device_id_type=MESH).wait_send()
                pltpu.make_async_remote_copy(
                    src_ref=land_ref.at[mine, win], dst_ref=land_ref.at[theirs, win], send_sem=send_sems.at[k],
                    recv_sem=recv_sems.at[k], device_id=peers[0], device_id_type=MESH).wait_recv()

    return _call(
        body, name=name, out_shape=pltpu.HBM(land_thru.shape, land_thru.dtype),
        in_specs=(HBM_SPEC, SEM_SPEC, SEM_SPEC, pl.BlockSpec(memory_space=pl.ANY)),
        out_specs=HBM_SPEC, input_output_aliases={0: 0},
        compiler_params=pltpu.CompilerParams(has_side_effects=DATAFLOW),
    )(land_thru, send_sems, recv_sems, after)


def chips_start(p, name):
    _, r, c = p.shape

    def body(p_ref, land_ref, send_sems, recv_sems, p_thru, land_thru, token):
        x, y, cc = _my_place()
        for j, (px, py, pc) in enumerate(_peers(x, y, cc)[1:]):
            pltpu.make_async_remote_copy(
                src_ref=p_ref.at[2 * px + py], dst_ref=land_ref.at[j], send_sem=send_sems.at[j],
                recv_sem=recv_sems.at[j], device_id=(px, py, pc), device_id_type=MESH).start()
        token[...] = jnp.zeros_like(token)

    return _call(
        body, name=name,
        out_shape=(pltpu.SemaphoreType.DMA((3,)), pltpu.SemaphoreType.DMA((3,)), pltpu.HBM(p.shape, p.dtype),
                   pltpu.HBM((3, r, c), p.dtype), _sds((8, 128))),
        in_specs=(HBM_SPEC, HBM_SPEC),
        out_specs=(SEM_SPEC, SEM_SPEC, HBM_SPEC, HBM_SPEC, pl.BlockSpec(memory_space=pltpu.VMEM)),
        input_output_aliases={0: 2, 1: 3},
        compiler_params=pltpu.CompilerParams(has_side_effects=DATAFLOW),
    )(_hbm(p), _hbm(lax.empty((3, r, c), p.dtype)))


def chips_wait(send_sems, recv_sems, p_thru, land_thru, after, name):
    def body(p_ref, land_ref, send_sems, recv_sems, after_ref, p_out, land_out):
        x, y, cc = _my_place()
        for j, (px, py, pc) in enumerate(_peers(x, y, cc)[1:]):
            cp = pltpu.make_async_remote_copy(
                src_ref=p_ref.at[2 * px + py], dst_ref=land_ref.at[j], send_sem=send_sems.at[j],
                recv_sem=recv_sems.at[j], device_id=(px, py, pc), device_id_type=MESH)
            cp.wait_send()
            cp.wait_recv()

    return _call(
        body, name=name,
        out_shape=(pltpu.HBM(p_thru.shape, p_thru.dtype), pltpu.HBM(land_thru.shape, land_thru.dtype)),
        in_specs=(HBM_SPEC, HBM_SPEC, SEM_SPEC, SEM_SPEC, pl.BlockSpec(memory_space=pl.ANY)),
        out_specs=(HBM_SPEC, HBM_SPEC), input_output_aliases={0: 0, 1: 1},
        compiler_params=pltpu.CompilerParams(has_side_effects=DATAFLOW),
    )(p_thru, land_thru, send_sems, recv_sems, after)


def sibling_start(gw, name):
    _, r, c = gw.shape
    nch, rows = _stream_rows(r)
    nsem = 4 * nch

    def body(g_ref, land_ref, send_sems, recv_sems, g_thru, land_thru, token):
        x, y, cc = _my_place()
        for q in range(4):
            for s in range(nch):
                win = pl.ds(s * rows, rows)
                pltpu.make_async_remote_copy(
                    src_ref=g_ref.at[q, 1 - cc, win], dst_ref=land_ref.at[q, win], send_sem=send_sems.at[q * nch + s],
                    recv_sem=recv_sems.at[q * nch + s], device_id=(x, y, 1 - cc), device_id_type=MESH).start()
        token[...] = jnp.zeros_like(token)

    return _call(
        body, name=name,
        out_shape=(pltpu.SemaphoreType.DMA((nsem,)), pltpu.SemaphoreType.DMA((nsem,)),
                   pltpu.HBM((4, 2, r, c), gw.dtype), pltpu.HBM((4, r, c), gw.dtype), _sds((8, 128))),
        in_specs=(HBM_SPEC, HBM_SPEC),
        out_specs=(SEM_SPEC, SEM_SPEC, HBM_SPEC, HBM_SPEC, pl.BlockSpec(memory_space=pltpu.VMEM)),
        input_output_aliases={0: 2, 1: 3},
        compiler_params=pltpu.CompilerParams(has_side_effects=DATAFLOW),
    )(_hbm(gw.reshape(4, 2, r, c)), _hbm(lax.empty((4, r, c), gw.dtype)))


def sibling_wait(send_sems, recv_sems, g_thru, land_thru, after, name):
    _, _, r, c = g_thru.shape
    nch, rows = _stream_rows(r)

    def body(g_ref, land_ref, send_sems, recv_sems, after_ref, g_out, land_out):
        x, y, cc = _my_place()
        for q in range(4):
            for s in range(nch):
                win = pl.ds(s * rows, rows)
                cp = pltpu.make_async_remote_copy(
                    src_ref=g_ref.at[q, 1 - cc, win], dst_ref=land_ref.at[q, win], send_sem=send_sems.at[q * nch + s],
                    recv_sem=recv_sems.at[q * nch + s], device_id=(x, y, 1 - cc), device_id_type=MESH)
                cp.wait_send()
                cp.wait_recv()

    return _call(
        body, name=name,
        out_shape=(pltpu.HBM(g_thru.shape, g_thru.dtype), pltpu.HBM(land_thru.shape, land_thru.dtype)),
        in_specs=(HBM_SPEC, HBM_SPEC, SEM_SPEC, SEM_SPEC, pl.BlockSpec(memory_space=pl.ANY)),
        out_specs=(HBM_SPEC, HBM_SPEC), input_output_aliases={0: 0, 1: 1},
        compiler_params=pltpu.CompilerParams(has_side_effects=DATAFLOW),
    )(g_thru, land_thru, send_sems, recv_sems, after)


def reduce_scatter_begin(gw, name, tag):
    _, r, c = gw.shape
    theirs = exchange_sibling(gw, name + "_sibling")
    chip_sum = add_own_half(gw.reshape(4, 2, r, c), theirs, name + "_add2")
    return chips_start(chip_sum, name + "_start" + tag)


def reduce_scatter_end(started, after, name, tag):
    send_sems, recv_sems, p_thru, land_thru, _ = started
    parts, land = chips_wait(send_sems, recv_sems, p_thru, land_thru, after, name + "_wait" + tag)
    mine = lax.dynamic_index_in_dim(parts, 2 * lax.axis_index("x") + lax.axis_index("y"), axis=0, keepdims=False)
    return add_own_lead(mine, land, name + "_add4")


def _pack(arrs):
    flat = jnp.concatenate([a.reshape(-1).astype(F32) for a in arrs])
    n = flat.shape[0]
    pad = (-n) % PACK_ELEMS
    return jnp.pad(flat, (0, pad)).reshape(-1, 128)


def _unpack(packed, shapes):
    flat = packed.reshape(-1)
    out, off = [], 0
    for s in shapes:
        n = int(np.prod(s))
        out.append(lax.optimization_barrier(flat[off:off + n]).reshape(s))
        off += n
    return out


def _t5_bucket(rel):
    nb = N_BUCKETS // 2
    max_exact = nb // 2
    ret = jnp.where(rel > 0, nb, 0)
    n = jnp.abs(rel)
    nf = jnp.maximum(n, 1).astype(jnp.float32)
    large = max_exact + (jnp.log(nf / max_exact) / math.log(MAX_DISTANCE / max_exact)
                         * (nb - max_exact)).astype(jnp.int32)
    large = jnp.minimum(large, nb - 1)
    return ret + jnp.where(n < max_exact, n, large)


SMALL_SHARDED = ("gla_w2_f", "gla_w2_b", "conv_w", "lru_ba", "lru_bx", "lru_lambda")
SMALL_REPL = ("rel_bias", "attn_sink", "gla_b2_f", "gla_b2_b", "gla_norm", "conv_b", "lru_wa", "lru_wx",
              "norm_mix_pre", "norm_mix_post", "norm_mem", "norm_x_pre", "norm_x_post", "norm_ff_pre", "norm_ff_post")
BIG = ("w_in", "w_out", "xq", "xk", "xv", "xo", "w_up", "w_down")
WEIGHTS = ['rel_bias', 'w_in', 'w_out', 'attn_sink', 'gla_w2_f', 'gla_b2_f', 'gla_w2_b', 'gla_b2_b', 'gla_norm',
           'conv_w', 'conv_b', 'lru_wa', 'lru_ba', 'lru_wx', 'lru_bx', 'lru_lambda', 'xq', 'xk', 'xv', 'xo', 'w_up',
           'w_down', 'norm_mix_pre', 'norm_mix_post', 'norm_mem', 'norm_x_pre', 'norm_x_post', 'norm_ff_pre',
           'norm_ff_post']


def _step(x, mem, loss_target, w, m, v):
    depth = w["w_in"].shape[0]
    t, d = x.shape[1], x.shape[2]
    ml = mem.shape[1]
    rx = d // N_DEV
    rf = w["w_up"].shape[2]
    r_out = D_MIX // N_DEV
    x = x.reshape(t, d)
    mem = mem.reshape(ml, d)
    loss_target = loss_target.reshape(t, d)
    my_idx = 4 * lax.axis_index("x") + 2 * lax.axis_index("y") + lax.axis_index("c")

    off_in = 0
    off_up, off_down, off_out = 0, rf, 2 * rf
    off_xq = off_out + r_out
    off_xk, off_xv, off_xo = off_xq + rx, off_xq + 2 * rx, off_xq + 3 * rx
    r_rest = off_xo + rx

    sh_shapes = [w[n].shape for n in SMALL_SHARDED]
    gathered = all_gather(_pack([w[n] for n in SMALL_SHARDED]), "ag_small")
    per_dev = [_unpack(gathered[j], sh_shapes) for j in range(N_DEV)]
    full = {n: jnp.concatenate([per_dev[j][i] for j in range(N_DEV)], axis=-1) for i, n in enumerate(SMALL_SHARDED)}
    for n in SMALL_REPL:
        full[n] = w[n]

    ag_started = []
    for l in range(depth):
        blk_in = jnp.pad(w["w_in"][l].T, ((0, W_IN_ROWS - W_IN_SHARD), (0, 0))).astype(BF16)
        blk_rest = jnp.concatenate([w["w_up"][l].T, w["w_down"][l], w["w_out"][l], w["xq"][l], w["xk"][l],
                                    w["xv"][l], w["xo"][l]], axis=0).astype(BF16)
        blk_in, _ = lax.optimization_barrier((blk_in, gathered if l == 0 else ag_started[-1][1][4]))
        start_in = gather_start(blk_in, "ag_start_in%d" % l)
        blk_rest, _ = lax.optimization_barrier((blk_rest, start_in[4]))
        ag_started.append((start_in, gather_start(blk_rest, "ag_start_rest%d" % l)))
    gather_token = sum(st[4][0, 0] for pair in ag_started for st in pair)
    gws = [None] * depth

    def gather_finish(started, after, name):
        send_sems, recv_sems, blk_thru, land_thru, _ = started
        _, land = gather_wait(send_sems, recv_sems, blk_thru, land_thru, after, name)
        return gather_pass(land, "ag_pass")

    qi = jnp.arange(BLOCK)[:, None]
    kj = jnp.arange(3 * BLOCK)[None, :]
    onehot_t = (jnp.arange(N_BUCKETS)[:, None] == _t5_bucket(kj - BLOCK - qi).reshape(1, -1)).astype(F32)
    bias = mm_plain(full["rel_bias"].T, onehot_t, "rel_bias_lookup", hi=True, tn=3 * BLOCK * 16)
    bias = bias.reshape(A_HEADS, BLOCK, 3 * BLOCK)
    bias_t = jnp.transpose(bias, (0, 2, 1))

    def sink_rows(sink):
        s = jnp.broadcast_to(sink.reshape(A_KV_HEADS, A_GROUP, 1), (A_KV_HEADS, A_GROUP, 128))
        return jnp.pad(s, ((0, 0), (0, 8 - A_GROUP), (0, 0)))

    bounds = np.concatenate([[0], np.cumsum(SPLIT_SIZES)])

    def split_proj(pp):
        outs = []
        for lo, hi in zip(bounds[:-1], bounds[1:]):
            segs = []
            for j in range(N_DEV):
                a, b = max(lo, j * W_IN_SHARD), min(hi, (j + 1) * W_IN_SHARD)
                if a < b:
                    base = j * W_IN_ROWS - j * W_IN_SHARD
                    segs.append(pp[:, base + a:base + b])
            outs.append(segs[0] if len(segs) == 1 else jnp.concatenate(segs, axis=1))
        return outs

    def join_dproj(pieces):
        zero_cols = jnp.zeros((t, W_IN_ROWS - W_IN_SHARD), F32)
        segs = []
        for j in range(N_DEV):
            for p, lo, hi in zip(pieces, bounds[:-1], bounds[1:]):
                a, b = max(lo, j * W_IN_SHARD), min(hi, (j + 1) * W_IN_SHARD)
                if a < b:
                    segs.append(p[:, a - lo:b - lo])
            segs.append(zero_cols)
        return jnp.concatenate(segs, axis=1).astype(BF16)

    def lead(a):
        return a.reshape(a.shape[0], C_WIDTH // 128, 128)

    saved = []
    h = rms_fwd(x, full["norm_mix_pre"][0] + gather_token, "rms_first")
    for l in range(depth):
        gw_in = gather_finish(ag_started[l][0], x, "ag_wait_in%d" % l)
        sv = {"x": x, "h_in": h}
        proj_pad = mm_wn(h, gw_in, off_in, W_IN_ROWS, "mm_w_in")
        aq, ak, av, bq, bk, bv, bg, zf, zb, cx, cy = split_proj(proj_pad)
        sv.update(aq=aq, ak=ak, av=av, bq=bq, bk=bk, bv=bv, bg=bg, zf=zf, zb=zb, cx=cx, cy=cy)
        sink_b = sink_rows(full["attn_sink"][l])
        passing = None
        if l >= EARLY_PASS_FROM_LAYER:
            send_sems, recv_sems, blk_thru, land_thru, _ = ag_started[l][1]
            _, land = gather_wait(send_sems, recv_sems, blk_thru, land_thru, proj_pad, "ag_wait_rest%d" % l)
            passing = pass_start(land, "ag_pass_start%d" % l)
            sink_b = sink_rows(full["attn_sink"][l] + passing[3][0, 0])
        oa = attn_fwd(aq, ak, av, bias, sink_b, "attn_fwd")
        la_f, la_b = gla_gates_fwd(zf, zb, full["gla_w2_f"][l], full["gla_b2_f"][l], full["gla_w2_b"][l],
                                   full["gla_b2_b"][l], "gla_gates_fwd")
        bk_t = bk.T
        gla = {}
        for nm, la, bdir in (("f", la_f, False), ("b", la_b, True)):
            la_t = la.T
            u, dec = gla_outer(bk_t, la_t, bv, "gla_outer_k_" + nm, bdir, "khat")
            sp = scan_lead(dec, u, "gla_state_scan_" + nm, reverse=bdir, inclusive=False)
            o_dir = gla_inner_fwd(bq, bk, bv, la, sp, "gla_inner_fwd_" + nm, bdir)
            gla[nm] = dict(la=la, la_t=la_t, dec=dec, sp=sp, o=o_dir)
        ob = gla_out_fwd(gla["f"]["o"], gla["b"]["o"], bg, full["gla_norm"][l], "gla_out_fwd")
        sv["gla"] = gla
        xm2, xm1, xp1 = _shift(cx, -2), _shift(cx, -1), _shift(cx, 1)
        xc, a0, u0, a1, u1 = lru_gates_fwd(cx, xm2, xm1, xp1, full["conv_w"][l], full["conv_b"][l], full["lru_wa"][l],
                                           full["lru_ba"][l], full["lru_wx"][l], full["lru_bx"][l],
                                           full["lru_lambda"][l], "lru_gates_fwd")
        h0 = scan_lead(lead(a0), lead(u0), "lru_scan_fwd", reverse=False, inclusive=True).reshape(t, C_WIDTH)
        h1 = scan_lead(lead(a1), lead(u1), "lru_scan_rev", reverse=True, inclusive=True).reshape(t, C_WIDTH)
        oc = lru_out_fwd(h0, h1, cy, "lru_out_fwd")
        sv.update(xm2=xm2, xm1=xm1, xp1=xp1, xc=xc, a0=a0, a1=a1, h0=h0, h1=h1, oa=oa)
        cat = jnp.concatenate([oa, ob, oc], axis=1).astype(BF16)
        if passing is None:
            gw = gather_finish(ag_started[l][1], cat, "ag_wait_rest%d" % l)
        else:
            gw = pass_wait(passing[0], passing[1], passing[2], cat, "ag_pass_wait%d" % l)
        gws[l] = (gw_in, gw)
        mixed = mm_wk(cat, gw, off_out, r_out, "mm_w_out")
        x1, h2 = resid_rms(x, mixed, full["norm_mix_post"][l], full["norm_x_pre"][l], "resid_rms")
        sv.update(cat=cat, mixed=mixed, x1=x1, h2=h2)
        memn = rms_fwd(mem, full["norm_mem"][l], "rms_mem")
        q = mm_wk(h2, gw, off_xq, rx, "mm_xq")
        k = mm_wk(memn, gw, off_xk, rx, "mm_xkv")
        vv = mm_wk(memn, gw, off_xv, rx, "mm_xkv")
        ox = xattn_fwd(q, k, vv, "xattn_fwd")
        ca = mm_wk(ox, gw, off_xo, rx, "mm_xo")
        x2, h3 = resid_rms(x1, ca, full["norm_x_post"][l], full["norm_ff_pre"][l], "resid_rms")
        sv.update(memn=memn, q=q, k=k, v=vv, ox=ox, ca=ca, x2=x2, h3=h3)
        up, act = mm_wn(h3, gw, off_up, rf, "mm_w_up", with_relu2=True)
        ff = mm_wk(act, gw, off_down, rf, "mm_w_down", jb=max(1, min(N_DEV, 2048 // rf)))
        if l + 1 < depth:
            x, h = resid_rms(x2, ff, full["norm_ff_post"][l], full["norm_mix_pre"][l + 1], "resid_rms")
        else:
            x = resid_rms(x2, ff, full["norm_ff_post"][l], None, "resid_rms_last")
        sv.update(up=up, act=act, ff=ff)
        saved.append(sv)

    dx, loss_local = loss_and_grad(x, loss_target, "loss")
    loss = lax.psum(loss_local, AXES)

    grads = {n: [None] * depth for n in WEIGHTS if n != "rel_bias"}
    dbias_total = None
    big_grads = [None] * depth
    rs_started = [None] * depth
    rs_token = 0.0
    bf = lambda a: a.astype(BF16)
    for l in reversed(range(depth)):
        gw_in, gw = gws[l]
        sv = saved[l]
        dff, grads["norm_ff_post"][l] = rms_bwd(sv["ff"], full["norm_ff_post"][l] + rs_token, dx, "rms_bwd")
        dup = mm_wn(dff, gw, off_down, rf, "mm_w_down_dx", relu_grad_of=sv["up"], out_dtype=BF16)
        gpack = mm_dw_into(sv["act"], dff, lax.empty((N_DEV, r_rest, d), BF16), off_down, rf, "mm_dw_down")
        gpack = mm_dw_into(dup, sv["h3"], gpack, off_up, rf, "mm_dw_up")
        dh3 = mm_wk(dup, gw, off_up, rf, "mm_w_up_dx", jb=max(1, min(N_DEV, 2048 // rf)))
        dx2, grads["norm_ff_pre"][l] = rms_bwd(sv["x2"], full["norm_ff_pre"][l], dh3, "rms_bwd_add", add=dx)
        dca, grads["norm_x_post"][l] = rms_bwd(sv["ca"], full["norm_x_post"][l], dx2, "rms_bwd")
        dox = mm_wn(dca, gw, off_xo, rx, "mm_x_dx")
        gpack = mm_dw_into(sv["ox"], dca, gpack, off_xo, rx, "mm_dw_xo")
        dq, dk, dv = xattn_bwd(sv["q"], sv["k"], sv["v"], sv["ox"], dox, "xattn_bwd")
        gpack = mm_dw_into(sv["h2"], dq, gpack, off_xq, rx, "mm_dw_xq")
        gpack = mm_dw_into(sv["memn"], dk, gpack, off_xk, rx, "mm_dw_xk")
        gpack = mm_dw_into(sv["memn"], dv, gpack, off_xv, rx, "mm_dw_xv")
        dh2 = mm_wn(dq, gw, off_xq, rx, "mm_x_dx")
        dmem_k = mm_wn(dk, gw, off_xk, rx, "mm_x_dx_mem")
        dmem_v = mm_wn(dv, gw, off_xv, rx, "mm_x_dx_mem")
        _, grads["norm_mem"][l] = rms_bwd(mem, full["norm_mem"][l], dmem_k, "rms_bwd_mem", dy2=dmem_v)
        dx1, grads["norm_x_pre"][l] = rms_bwd(sv["x1"], full["norm_x_pre"][l], dh2, "rms_bwd_add", add=dx2)
        dmixed, grads["norm_mix_post"][l] = rms_bwd(sv["mixed"], full["norm_mix_post"][l], dx1, "rms_bwd")
        dcat = mm_wn(dmixed, gw, off_out, r_out, "mm_w_out_dx")
        gpack = mm_dw_into(sv["cat"], dmixed, gpack, off_out, r_out, "mm_dw_out")
        if l == 0:
            rs_started[l] = [reduce_scatter_begin(gpack, "rs_rest", str(l)), None]
            mix_token = rs_started[l][0][4][0, 0]
        else:
            sib = sibling_start(gpack, "rs_rest_sib_start%d" % l)
            mix_token = sib[4][0, 0]
        doa, dob, doc = dcat[:, :A_Q], dcat[:, A_Q:A_Q + B_V], dcat[:, A_Q + B_V:]
        daq, dak, dav, dbias, dsink = attn_bwd(sv["aq"], sv["ak"], sv["av"], bias, bias_t,
                                               sink_rows(full["attn_sink"][l] + mix_token), doa, sv["oa"], "attn_bwd")
        grads["attn_sink"][l] = dsink[:, :A_GROUP, 0].reshape(A_HEADS)
        dbias_total = dbias if dbias_total is None else add_n([dbias_total, dbias], F32, "add_dbias")
        gf, gb = sv["gla"]["f"], sv["gla"]["b"]
        do_gla, dbg, dgn = gla_out_bwd(gf["o"], gb["o"], sv["bg"], full["gla_norm"][l] + mix_token, dob,
                                       "gla_out_bwd")
        grads["gla_norm"][l] = dgn.reshape(B_V)
        bq_t = sv["bq"].T
        acc = None
        dlas = {}
        for nm, gd, bdir in (("f", gf, False), ("b", gb, True)):
            wq = gla_outer(bq_t, gd["la_t"], do_gla, "gla_outer_q_" + nm, bdir, "qtil")
            gs = scan_lead(gd["dec"], wq, "gla_adj_scan_" + nm, reverse=not bdir, inclusive=False)
            dbq, dbk, dbv, dlas[nm] = gla_inner_bwd(sv["bq"], sv["bk"], sv["bv"], gd["la"], do_gla, gd["sp"], gs,
                                                    gd["dec"], "gla_inner_bwd_" + nm, bdir, add=acc)
            acc = (dbq, dbk, dbv)
        dzf, dzb, dpre_f, dpre_b, db2f, db2b = gla_gates_bwd(
            sv["zf"], sv["zb"], full["gla_w2_f"][l], full["gla_b2_f"][l], full["gla_w2_b"][l], full["gla_b2_b"][l],
            dlas["f"], dlas["b"], "gla_gates_bwd")
        grads["gla_b2_f"][l] = db2f.reshape(B_QK)
        grads["gla_b2_b"][l] = db2b.reshape(B_QK)
        grads["gla_w2_f"][l] = mm_plain(sv["zf"].T, dpre_f, "mm_dw_gate", hi=True)
        grads["gla_w2_b"][l] = mm_plain(sv["zb"].T, dpre_b, "mm_dw_gate", hi=True)
        dh, dcy = lru_out_bwd(sv["h0"], sv["h1"], sv["cy"], doc, "lru_out_bwd")
        g0 = scan_lead(lead(_shift(sv["a0"], 1)), lead(dh), "lru_scan_rev", reverse=True,
                       inclusive=True).reshape(t, C_WIDTH)
        g1 = scan_lead(lead(_shift(sv["a1"], -1)), lead(dh), "lru_scan_fwd", reverse=False,
                       inclusive=True).reshape(t, C_WIDTH)
        dxc, dzr0, dzi0, dzr1, dzi1, dlam, dba, dbx = lru_gates_bwd(
            sv["xc"], g0, _shift(sv["h0"], -1), g1, _shift(sv["h1"], 1), full["lru_wa"][l], full["lru_ba"][l],
            full["lru_wx"][l], full["lru_bx"][l], full["lru_lambda"][l], "lru_gates_bwd")
        xc_t = bf(sv["xc"].T)
        grads["lru_wa"][l] = jnp.stack([blockdiag_dw(xc_t, dzr0, "lru_dw"), blockdiag_dw(xc_t, dzr1, "lru_dw")])
        grads["lru_wx"][l] = jnp.stack([blockdiag_dw(xc_t, dzi0, "lru_dw"), blockdiag_dw(xc_t, dzi1, "lru_dw")])
        grads["lru_lambda"][l], grads["lru_ba"][l], grads["lru_bx"][l] = dlam, dba, dbx
        dcx, dcw, dcb = conv_bwd(dxc, _shift(dxc, 2), _shift(dxc, 1), _shift(dxc, -1), sv["cx"], sv["xm2"],
                                 sv["xm1"], sv["xp1"], full["conv_w"][l], "conv_bwd")
        grads["conv_w"][l] = dcw
        grads["conv_b"][l] = dcb.reshape(C_WIDTH)
        dproj_pad = join_dproj([daq, dak, dav, dbq, dbk, dbv, dbg, dzf, dzb, dcx, dcy])
        g_in_t = mm_plain(dproj_pad, sv["h_in"], "mm_dw_in", ta=True, out_dtype=BF16)
        rs_in = reduce_scatter_begin(g_in_t.reshape(N_DEV, W_IN_ROWS, d), "rs_in", str(l))
        rs_token = rs_in[4][0, 0]
        dh1 = mm_wk(dproj_pad, gw_in, off_in, W_IN_ROWS, "mm_w_in_dx", jb=2)
        dx, grads["norm_mix_pre"][l] = rms_bwd(sv["x"], full["norm_mix_pre"][l] + rs_token, dh1, "rms_bwd_add",
                                               add=dx1)
        if l > 0:
            g5, theirs = sibling_wait(sib[0], sib[1], sib[2], sib[3], dx, "rs_rest_sib_wait%d" % l)
            chip_sum = add_own_half(g5, theirs, "rs_rest_add2")
            rs_started[l] = [chips_start(chip_sum, "rs_rest_start%d" % l), rs_in]
            rs_token = rs_token + rs_started[l][0][4][0, 0]
        else:
            rs_started[l][1] = rs_in

    grad_rel = mm_plain(dbias_total.reshape(A_HEADS, -1), onehot_t, "rel_bias_grad", tb=True, hi=True).T

    small_names = [n for n in WEIGHTS if n not in BIG]
    small_g = {"rel_bias": grad_rel}
    for n in small_names:
        if n != "rel_bias":
            small_g[n] = jnp.stack([g.reshape(full[n].shape[1:]) for g in grads[n]])
    shapes = [small_g[n].shape for n in small_names]
    small_started = gather_start(_pack([small_g[n] for n in small_names]), "ag_start_small_grads")
    for l in range(depth):
        big_grads[l] = (reduce_scatter_end(rs_started[l][1], small_started[4], "rs_in", str(l)),
                        reduce_scatter_end(rs_started[l][0], small_started[4], "rs_rest", str(l)))

    grad_out, delta, new_m, new_v = {}, {}, {}, {}

    def rows(l, off, r):
        return big_grads[l][1][off:off + r]

    big_g = {
        "w_in": jnp.stack([big_grads[l][0][:W_IN_SHARD].T for l in range(depth)]),
        "w_out": jnp.stack([rows(l, off_out, r_out) for l in range(depth)]),
        "xq": jnp.stack([rows(l, off_xq, rx) for l in range(depth)]),
        "xk": jnp.stack([rows(l, off_xk, rx) for l in range(depth)]),
        "xv": jnp.stack([rows(l, off_xv, rx) for l in range(depth)]),
        "xo": jnp.stack([rows(l, off_xo, rx) for l in range(depth)]),
        "w_up": jnp.stack([rows(l, off_up, rf).T for l in range(depth)]),
        "w_down": jnp.stack([rows(l, off_down, rf) for l in range(depth)]),
    }
    for n in BIG:
        grad_out[n] = big_g[n]
        delta[n], new_m[n], new_v[n] = adamw(big_g[n], w[n], m[n], v[n], "adamw_" + n)

    packed = gather_finish(small_started, delta["w_down"], "ag_wait_small_grads")
    summed = sum_lead(packed, tuple(range(N_DEV)), F32, "add8_small")
    small_g = dict(zip(small_names, _unpack(summed, shapes)))
    for n in SMALL_SHARDED:
        wdt = w[n].shape[-1]
        small_g[n] = lax.dynamic_slice_in_dim(small_g[n], my_idx * wdt, wdt, axis=small_g[n].ndim - 1)

    direct = ("lru_wa", "lru_wx")
    packed_names = [n for n in small_names if n not in direct]
    sshapes = [w[n].shape for n in packed_names]
    ds, ms, vs = adamw(_pack([small_g[n] for n in packed_names]), _pack([w[n] for n in packed_names]),
                       _pack([m[n] for n in packed_names]), _pack([v[n] for n in packed_names]), "adamw_small")
    for n, d_, m_, v_ in zip(packed_names, _unpack(ds, sshapes), _unpack(ms, sshapes), _unpack(vs, sshapes)):
        grad_out[n], delta[n], new_m[n], new_v[n] = small_g[n], d_, m_, v_
    for n in direct:
        grad_out[n] = small_g[n]
        delta[n], new_m[n], new_v[n] = adamw(small_g[n], w[n], m[n], v[n], "adamw_lru")

    return (loss, dx.reshape(1, t, d), *[grad_out[n] for n in WEIGHTS], *[delta[n] for n in WEIGHTS],
            *[new_m[n] for n in WEIGHTS], *[new_v[n] for n in WEIGHTS])


def kernel(x, mem, rel_bias, w_in, w_out, attn_sink, gla_w2_f, gla_b2_f, gla_w2_b, gla_b2_b, gla_norm, conv_w, conv_b, lru_wa, lru_ba, lru_wx, lru_bx, lru_lambda, xq, xk, xv, xo, w_up, w_down, norm_mix_pre, norm_mix_post, norm_mem, norm_x_pre, norm_x_post, norm_ff_pre, norm_ff_post, loss_target, m_rel_bias, m_w_in, m_w_out, m_attn_sink, m_gla_w2_f, m_gla_b2_f, m_gla_w2_b, m_gla_b2_b, m_gla_norm, m_conv_w, m_conv_b, m_lru_wa, m_lru_ba, m_lru_wx, m_lru_bx, m_lru_lambda, m_xq, m_xk, m_xv, m_xo, m_w_up, m_w_down, m_norm_mix_pre, m_norm_mix_post, m_norm_mem, m_norm_x_pre, m_norm_x_post, m_norm_ff_pre, m_norm_ff_post, v_rel_bias, v_w_in, v_w_out, v_attn_sink, v_gla_w2_f, v_gla_b2_f, v_gla_w2_b, v_gla_b2_b, v_gla_norm, v_conv_w, v_conv_b, v_lru_wa, v_lru_ba, v_lru_wx, v_lru_bx, v_lru_lambda, v_xq, v_xk, v_xv, v_xo, v_w_up, v_w_down, v_norm_mix_pre, v_norm_mix_post, v_norm_mem, v_norm_x_pre, v_norm_x_post, v_norm_ff_pre, v_norm_ff_post):
    given = dict(locals())
    w = {n: given[n] for n in WEIGHTS}
    m = {n: given["m_" + n] for n in WEIGHTS}
    v = {n: given["v_" + n] for n in WEIGHTS}
    return _step(x, mem, loss_target, w, m, v)
```

```python
import math

import jax
import jax.numpy as jnp
import numpy as np
from jax import lax
from jax.experimental import pallas as pl
from jax.experimental.pallas import tpu as pltpu

F32 = jnp.float32
BF16 = jnp.bfloat16
HI = lax.Precision.HIGHEST
NN = (((1,), (0,)), ((), ()))
NT = (((1,), (1,)), ((), ()))
MESH = pl.DeviceIdType.MESH
AXES = ("x", "y", "c")
N_DEV = 8

A_HEAD_DIM = 128
A_HEADS = 8
A_KV_HEADS = 2
A_GROUP = 4
WINDOW = 128
BLOCK = 128
N_BUCKETS = 32
MAX_DISTANCE = 128
B_HEADS = 4
B_KEY_DIM = 64
B_VAL_DIM = 128
GATE_RANK = 16
GATE_TAU = 16.0
GLA_CHUNK = 16
C_WIDTH = 512
C_BLOCKS = 4
C_BLOCK_DIM = 128
LRU_C = 8.0
X_HEADS = 4
EPS = 1e-6
NEG_INF = -1e30
A_Q = A_HEADS * A_HEAD_DIM
A_KV = A_KV_HEADS * A_HEAD_DIM
B_QK = B_HEADS * B_KEY_DIM
B_V = B_HEADS * B_VAL_DIM
SPLIT_SIZES = (A_Q, A_KV, A_KV, B_QK, B_QK, B_V, B_V, GATE_RANK, GATE_RANK, C_WIDTH, C_WIDTH)
D_IN = sum(SPLIT_SIZES)
D_MIX = A_Q + B_V + C_WIDTH
W_IN_SHARD = D_IN // N_DEV
W_IN_ROWS = 768
GLA_TILE = 128
CHUNKS_PER_TILE = GLA_TILE // GLA_CHUNK
EXP_CLAMP = 80.0

ADAM_LR = 0.001
ADAM_B1 = 0.9
ADAM_B2 = 0.999
ADAM_EPS = 1e-08
ADAM_WD = 0.01
ADAM_STEP = 10

VMEM_LIMIT_BYTES = 52 * 1024 * 1024
MM_TILE = 1024
NORM_TILE_ELEMS = 1 << 19
SCAN_UNROLL = 8
EARLY_PASS_FROM_LAYER = 2
SIBLING_STREAMS = 16
PACK_ELEMS = 128 * 2048


def _call(body, **kw):
    return pl.pallas_call(body, **kw)


def _cparams():
    return pltpu.CompilerParams(vmem_limit_bytes=VMEM_LIMIT_BYTES)


def _dot(a, b, dims=NN, hi=False):
    if hi:
        return lax.dot_general(a, b, dims, precision=HI, preferred_element_type=F32)
    return lax.dot_general(a.astype(BF16), b.astype(BF16), dims, preferred_element_type=F32)


def _sds(shape, dtype=F32):
    return jax.ShapeDtypeStruct(tuple(shape), dtype)


def _row_tile(rows, cols, target_elems=1 << 18):
    want = max(8, target_elems // max(cols, 1))
    if rows <= want:
        return rows
    t = (want // 8) * 8
    while t >= 8:
        if rows % t == 0:
            return t
        t -= 8
    return rows


def _expm1(x):
    poly = x * (1.0 + x * (1.0 / 2 + x * (1.0 / 6 + x * (1.0 / 24 + x * (1.0 / 120 + x * (
        1.0 / 720 + x * (1.0 / 5040 + x * (1.0 / 40320))))))))
    return jnp.where(jnp.abs(x) < 0.3, poly, jnp.exp(x) - 1.0)


def _log1p(e):
    w = 1.0 + e
    return jnp.where(w == 1.0, e, jnp.log(w) * e / (w - 1.0))


def _softplus(x):
    return jnp.maximum(x, 0.0) + _log1p(jnp.exp(-jnp.abs(x)))


def _sigmoid(x):
    return jax.nn.sigmoid(x)


GELU_K = math.sqrt(2.0 / math.pi)


def _gelu(y):
    t = jnp.tanh(GELU_K * (y + 0.044715 * y * y * y))
    return 0.5 * y * (1.0 + t)


def _gelu_grad(y):
    t = jnp.tanh(GELU_K * (y + 0.044715 * y * y * y))
    return 0.5 * (1.0 + t) + 0.5 * y * (1.0 - t * t) * GELU_K * (1.0 + 3 * 0.044715 * y * y)


def rms_fwd(x, g, name):
    m, d = x.shape
    tm = _row_tile(m, d, NORM_TILE_ELEMS)

    def body(x_ref, g_ref, o_ref):
        xv = x_ref[...]
        r = lax.rsqrt(jnp.mean(xv * xv, axis=1, keepdims=True) + EPS)
        o_ref[...] = (xv * r * g_ref[...]).astype(o_ref.dtype)

    return _call(body, name=name, grid=(m // tm,),
                 in_specs=[pl.BlockSpec((tm, d), lambda i: (i, 0)), pl.BlockSpec((1, d), lambda i: (0, 0))],
                 out_specs=pl.BlockSpec((tm, d), lambda i: (i, 0)),
                 out_shape=_sds((m, d), BF16))(x, g.reshape(1, d))


def resid_rms(xres, mid, g_post, g_pre, name):
    m, d = xres.shape
    tm = _row_tile(m, d, NORM_TILE_ELEMS)
    with_pre = g_pre is not None

    def body(*refs):
        if with_pre:
            x_ref, m_ref, gp_ref, gn_ref, xo_ref, h_ref = refs
        else:
            x_ref, m_ref, gp_ref, xo_ref = refs
        mv = m_ref[...]
        r = lax.rsqrt(jnp.mean(mv * mv, axis=1, keepdims=True) + EPS)
        xn = x_ref[...] + mv * r * gp_ref[...]
        xo_ref[...] = xn
        if with_pre:
            r2 = lax.rsqrt(jnp.mean(xn * xn, axis=1, keepdims=True) + EPS)
            h_ref[...] = (xn * r2 * gn_ref[...]).astype(h_ref.dtype)

    row = pl.BlockSpec((tm, d), lambda i: (i, 0))
    vec = pl.BlockSpec((1, d), lambda i: (0, 0))
    ins = [xres, mid, g_post.reshape(1, d)] + ([g_pre.reshape(1, d)] if with_pre else [])
    in_specs = [row, row, vec] + ([vec] if with_pre else [])
    if with_pre:
        return _call(body, name=name, grid=(m // tm,), in_specs=in_specs, out_specs=(row, row),
                     out_shape=(_sds((m, d)), _sds((m, d), BF16)))(*ins)
    return _call(body, name=name, grid=(m // tm,), in_specs=in_specs, out_specs=row,
                 out_shape=_sds((m, d)))(*ins)


def rms_bwd(x, g, dy, name, dy2=None, add=None, matmul_operand=False):
    m, d = x.shape
    tm = _row_tile(m, d, NORM_TILE_ELEMS)
    has2, hasadd = dy2 is not None, add is not None

    def body(*refs):
        it = iter(refs)
        x_ref, g_ref, dy_ref = next(it), next(it), next(it)
        dy2_ref = next(it) if has2 else None
        add_ref = next(it) if hasadd else None
        dx_ref, dg_ref = next(it), next(it)
        xv = x_ref[...]
        dyv = dy_ref[...]
        if has2:
            dyv = dyv + dy2_ref[...]
        r = lax.rsqrt(jnp.mean(xv * xv, axis=1, keepdims=True) + EPS)
        xh = xv * r
        dxh = dyv * g_ref[...]
        dx = r * (dxh - xh * jnp.mean(dxh * xh, axis=1, keepdims=True))
        if hasadd:
            dx = dx + add_ref[...]
        dx_ref[...] = dx.astype(dx_ref.dtype)
        part = jnp.sum(dyv * xh, axis=0, keepdims=True)

        @pl.when(pl.program_id(0) == 0)
        def _():
            dg_ref[...] = part

        @pl.when(pl.program_id(0) > 0)
        def _():
            dg_ref[...] += part

    row = pl.BlockSpec((tm, d), lambda i: (i, 0))
    vec = pl.BlockSpec((1, d), lambda i: (0, 0))
    ins = [x, g.reshape(1, d), dy] + ([dy2] if has2 else []) + ([add] if hasadd else [])
    in_specs = [row, vec, row] + ([row] if has2 else []) + ([row] if hasadd else [])
    return _call(body, name=name, grid=(m // tm,), in_specs=in_specs, out_specs=(row, vec),
                 out_shape=(_sds((m, d), BF16 if matmul_operand else F32), _sds((1, d))))(*ins)


def loss_and_grad(y, target, name):
    m, d = y.shape
    tm = _row_tile(m, d, NORM_TILE_ELEMS)

    def body(y_ref, t_ref, dy_ref, l_ref):
        e = y_ref[...] - t_ref[...]
        dy_ref[...] = e * (1.0 / d)
        s = jnp.sum(jnp.sum(e * e, axis=1, keepdims=True), axis=0, keepdims=True) * (0.5 / d)
        part = jnp.broadcast_to(s, (1, 128))

        @pl.when(pl.program_id(0) == 0)
        def _():
            l_ref[...] = part

        @pl.when(pl.program_id(0) > 0)
        def _():
            l_ref[...] += part

    row = pl.BlockSpec((tm, d), lambda i: (i, 0))
    dy, l = _call(body, name=name, grid=(m // tm,), in_specs=[row, row],
                  out_specs=(row, pl.BlockSpec((1, 128), lambda i: (0, 0))),
                  out_shape=(_sds((m, d)), _sds((1, 128))))(y, target)
    return dy, l[0, 0]


def adamw(g, w, m, v, name):
    shape = w.shape
    cols = shape[-1]
    rows = int(np.prod(shape[:-1]))
    tm = _row_tile(rows, cols)
    c1 = 1.0 - ADAM_B1 ** ADAM_STEP
    c2 = 1.0 - ADAM_B2 ** ADAM_STEP

    def body(g_ref, w_ref, m_ref, v_ref, d_ref, mo_ref, vo_ref):
        gv = g_ref[...]
        mn = ADAM_B1 * m_ref[...] + (1.0 - ADAM_B1) * gv
        vn = ADAM_B2 * v_ref[...] + (1.0 - ADAM_B2) * (gv * gv)
        m_hat = mn / c1
        v_hat = vn / c2
        d_ref[...] = -ADAM_LR * (m_hat / (jnp.sqrt(v_hat) + ADAM_EPS) + ADAM_WD * w_ref[...])
        mo_ref[...] = mn
        vo_ref[...] = vn

    row = pl.BlockSpec((tm, cols), lambda i: (i, 0))
    outs = _call(body, name=name, grid=(rows // tm,), in_specs=[row] * 4, out_specs=(row,) * 3,
                 out_shape=(_sds((rows, cols)),) * 3)(*[a.reshape(rows, cols) for a in (g, w, m, v)])
    return tuple(o.reshape(shape) for o in outs)


def sum_lead(x, order, out_dtype, name):
    n, rows, cols = x.shape
    tm = _row_tile(rows, cols)

    def body(x_ref, o_ref):
        acc = x_ref[order[0]].astype(F32)
        for i in order[1:]:
            acc = acc + x_ref[i].astype(F32)
        o_ref[...] = acc.astype(out_dtype)

    return _call(body, name=name, grid=(rows // tm,), in_specs=[pl.BlockSpec((n, tm, cols), lambda i: (0, i, 0))],
                 out_specs=pl.BlockSpec((tm, cols), lambda i: (i, 0)), out_shape=_sds((rows, cols), out_dtype))(x)


def add_own_lead(own, parts, name):
    n, rows, cols = parts.shape
    tm = _row_tile(rows, cols)

    def body(o_ref, p_ref, out_ref):
        acc = o_ref[...].astype(F32)
        for i in range(n):
            acc = acc + p_ref[i].astype(F32)
        out_ref[...] = acc

    row = pl.BlockSpec((tm, cols), lambda i: (i, 0))
    return _call(body, name=name, grid=(rows // tm,),
                 in_specs=[row, pl.BlockSpec((n, tm, cols), lambda i: (0, i, 0))], out_specs=row,
                 out_shape=_sds((rows, cols)))(own, parts)


def add_own_half(g5, theirs, name):
    _, _, r, c = g5.shape
    tm = _row_tile(r, c, 1 << 20)

    def body(cc_ref, g_ref, t_ref, o_ref):
        o_ref[...] = (g_ref[...].astype(F32) + t_ref[...].astype(F32)).astype(o_ref.dtype)

    grid_spec = pltpu.PrefetchScalarGridSpec(
        num_scalar_prefetch=1, grid=(4, r // tm),
        in_specs=[pl.BlockSpec((None, None, tm, c), lambda q, i, cc_ref: (q, cc_ref[0], i, 0)),
                  pl.BlockSpec((None, tm, c), lambda q, i, cc_ref: (q, i, 0))],
        out_specs=pl.BlockSpec((None, tm, c), lambda q, i, cc_ref: (q, i, 0)))
    return _call(body, name=name, grid_spec=grid_spec, out_shape=_sds((4, r, c), BF16))(
        lax.axis_index("c").astype(jnp.int32).reshape(1), g5, theirs)


def add_n(xs, out_dtype, name):
    shape = xs[0].shape
    cols = shape[-1]
    rows = int(np.prod(shape[:-1]))
    tm = _row_tile(rows, cols)
    n = len(xs)

    def body(*refs):
        acc = refs[0][...].astype(F32)
        for r in refs[1:n]:
            acc = acc + r[...].astype(F32)
        refs[n][...] = acc.astype(out_dtype)

    row = pl.BlockSpec((tm, cols), lambda i: (i, 0))
    out = _call(body, name=name, grid=(rows // tm,), in_specs=[row] * n, out_specs=row,
                out_shape=_sds((rows, cols), out_dtype))(*[a.reshape(rows, cols) for a in xs])
    return out.reshape(shape)


def mm_plain(a, b, name, ta=False, tb=False, out_dtype=F32, hi=False, tm=MM_TILE, tn=MM_TILE):
    k, m = a.shape[::1 if ta else -1]
    n = b.shape[0] if tb else b.shape[1]
    tm, tn = min(tm, m), min(tn, n)
    dims = (((0 if ta else 1,), (1 if tb else 0,)), ((), ()))

    def body(a_ref, b_ref, o_ref):
        o_ref[...] = _dot(a_ref[...], b_ref[...], dims, hi).astype(out_dtype)

    a_spec = pl.BlockSpec((k, tm), lambda j, i: (0, i)) if ta else pl.BlockSpec((tm, k), lambda j, i: (i, 0))
    b_spec = pl.BlockSpec((tn, k), lambda j, i: (j, 0)) if tb else pl.BlockSpec((k, tn), lambda j, i: (0, j))
    return _call(body, name=name, grid=(n // tn, m // tm), in_specs=[a_spec, b_spec],
                 out_specs=pl.BlockSpec((tm, tn), lambda j, i: (i, j)),
                 out_shape=_sds((m, n), out_dtype), compiler_params=_cparams())(a, b)


def mm_dw_into(a, b, buf, off, r, name, tn=MM_TILE):
    k, m = a.shape
    n = b.shape[1]
    tm, tn = min(MM_TILE, r), min(tn, n)
    assert m == N_DEV * r and off % tm == 0 and r % tm == 0
    per = r // tm
    dims = (((0,), (0,)), ((), ()))

    def body(a_ref, b_ref, buf_ref, o_ref):
        o_ref[...] = _dot(a_ref[...], b_ref[...], dims).astype(o_ref.dtype)

    return _call(body, name=name, grid=(n // tn, m // tm),
                 in_specs=[pl.BlockSpec((k, tm), lambda j, i: (0, i)), pl.BlockSpec((k, tn), lambda j, i: (0, j)),
                           pl.BlockSpec(memory_space=pl.ANY)],
                 out_specs=pl.BlockSpec((None, tm, tn), lambda j, i: (i // per, off // tm + i % per, j)),
                 out_shape=_sds(buf.shape, buf.dtype), input_output_aliases={2: 0},
                 compiler_params=_cparams())(a, b, buf)


def mm_wk(a, gw, off, r, name, jb=N_DEV, tm=MM_TILE, tn=MM_TILE):
    m = a.shape[0]
    d = gw.shape[2]
    tm, tn = min(tm, m), min(tn, d)
    nk = N_DEV // jb
    ob = off // r
    assert off % r == 0 and a.shape[1] == N_DEV * r

    def body(a_ref, b_ref, o_ref, *acc):
        av = a_ref[...].astype(BF16)
        p = _dot(av[:, 0:r], b_ref[0])
        for q in range(1, jb):
            p = p + _dot(av[:, q * r:(q + 1) * r], b_ref[q])
        if nk == 1:
            o_ref[...] = p
        else:
            kk = pl.program_id(2)

            @pl.when(kk == 0)
            def _():
                acc[0][...] = p

            @pl.when(kk > 0)
            def _():
                acc[0][...] += p

            @pl.when(kk == nk - 1)
            def _():
                o_ref[...] = acc[0][...]

    return _call(body, name=name, grid=(m // tm, d // tn, nk),
                 in_specs=[pl.BlockSpec((tm, jb * r), lambda i, j, k: (i, k)),
                           pl.BlockSpec((jb, r, tn), lambda i, j, k: (k, ob, j))],
                 out_specs=pl.BlockSpec((tm, tn), lambda i, j, k: (i, j)),
                 out_shape=_sds((m, d)),
                 scratch_shapes=([pltpu.VMEM((tm, tn), F32)] if nk > 1 else []),
                 compiler_params=_cparams())(a, gw)


def mm_wn(a, gw, off, r, name, relu_grad_of=None, out_dtype=F32, with_relu2=False, tm=MM_TILE):
    m, d = a.shape
    tm = min(tm, m)
    ob = off // r
    assert off % r == 0 and gw.shape[2] == d
    epi = relu_grad_of is not None

    def body(*refs):
        it = iter(refs)
        a_ref, b_ref = next(it), next(it)
        e_ref = next(it) if epi else None
        o_ref = next(it)
        p = _dot(a_ref[...], b_ref[...], NT)
        if epi:
            p = p * (2.0 * jnp.maximum(e_ref[...], 0.0))
        o_ref[...] = p.astype(out_dtype)
        if with_relu2:
            act_ref = next(it)
            act_ref[...] = jnp.square(jnp.maximum(p, 0.0)).astype(act_ref.dtype)

    blk = pl.BlockSpec((tm, r), lambda i, j: (i, j))
    in_specs = [pl.BlockSpec((tm, d), lambda i, j: (i, 0)), pl.BlockSpec((None, r, d), lambda i, j: (j, ob, 0))]
    ins = [a, gw]
    if epi:
        in_specs.append(blk)
        ins.append(relu_grad_of)
    out_shape = _sds((m, N_DEV * r), out_dtype)
    if with_relu2:
        return _call(body, name=name, grid=(m // tm, N_DEV), in_specs=in_specs, out_specs=(blk, blk),
                     out_shape=(out_shape, _sds((m, N_DEV * r), BF16)), compiler_params=_cparams())(*ins)
    return _call(body, name=name, grid=(m // tm, N_DEV), in_specs=in_specs, out_specs=blk,
                 out_shape=out_shape, compiler_params=_cparams())(*ins)


def blockdiag_dw(xt, dz, name):
    t = xt.shape[1]

    def body(a_ref, b_ref, o_ref):
        o_ref[...] = _dot(a_ref[...], b_ref[...])

    return _call(body, name=name, grid=(C_BLOCKS,),
                 in_specs=[pl.BlockSpec((C_BLOCK_DIM, t), lambda g: (g, 0)),
                           pl.BlockSpec((t, C_BLOCK_DIM), lambda g: (0, g))],
                 out_specs=pl.BlockSpec((None, C_BLOCK_DIM, C_BLOCK_DIM), lambda g: (g, 0, 0)),
                 out_shape=_sds((C_BLOCKS, C_BLOCK_DIM, C_BLOCK_DIM)))(xt, dz)


def _band_mask(n, nblk, transposed):
    shape = (3 * BLOCK, A_GROUP * BLOCK) if transposed else (A_GROUP * BLOCK, 3 * BLOCK)
    qi = lax.broadcasted_iota(jnp.int32, shape, 1 if transposed else 0) & (BLOCK - 1)
    kj = lax.broadcasted_iota(jnp.int32, shape, 0 if transposed else 1)
    lo = jnp.where(n > 0, 0, BLOCK)
    hi = jnp.where(n < nblk - 1, 3 * BLOCK, 2 * BLOCK)
    return (jnp.abs(kj - BLOCK - qi) <= WINDOW) & (kj >= lo) & (kj < hi)


def _band_rows(ref, n, nblk):
    starts = [jnp.maximum(n - 1, 0), n, jnp.minimum(n + 1, nblk - 1)]
    return jnp.concatenate([ref[pl.ds(pl.multiple_of(s * BLOCK, BLOCK), BLOCK), :] for s in starts], axis=0)


def _head_cols(j):
    return slice(j * A_HEAD_DIM, (j + 1) * A_HEAD_DIM)


def attn_fwd(q, k, v, bias, sink_b, name):
    t = q.shape[0]
    nblk = t // BLOCK
    scale = A_HEAD_DIM ** -0.5

    def body(q_ref, k_ref, v_ref, b_ref, s_ref, o_ref):
        n = pl.program_id(1)
        kb = _band_rows(k_ref, n, nblk).astype(BF16)
        vb = _band_rows(v_ref, n, nblk).astype(BF16)
        mask = _band_mask(n, nblk, False)
        q4 = jnp.concatenate([q_ref[:, _head_cols(j)] for j in range(A_GROUP)], axis=0)
        b4 = jnp.concatenate([b_ref[j] for j in range(A_GROUP)], axis=0)
        sk = jnp.concatenate([jnp.broadcast_to(s_ref[j:j + 1, 0:1], (BLOCK, 1)) for j in range(A_GROUP)], axis=0)
        s = jnp.where(mask, _dot(q4, kb, NT) * scale + b4, NEG_INF)
        mx = jnp.maximum(jnp.max(s, axis=1, keepdims=True), sk)
        p = jnp.exp(s - mx)
        den = jnp.sum(p, axis=1, keepdims=True) + jnp.exp(sk - mx)
        o4 = _dot(p * (1.0 / den), vb)
        for j in range(A_GROUP):
            o_ref[:, _head_cols(j)] = o4[j * BLOCK:(j + 1) * BLOCK, :]

    gw = A_GROUP * A_HEAD_DIM
    return _call(body, name=name, grid=(A_KV_HEADS, nblk),
                 in_specs=[pl.BlockSpec((BLOCK, gw), lambda g, n: (n, g)),
                           pl.BlockSpec((t, A_HEAD_DIM), lambda g, n: (0, g)),
                           pl.BlockSpec((t, A_HEAD_DIM), lambda g, n: (0, g)),
                           pl.BlockSpec((A_GROUP, BLOCK, 3 * BLOCK), lambda g, n: (g, 0, 0)),
                           pl.BlockSpec((None, 8, 128), lambda g, n: (g, 0, 0))],
                 out_specs=pl.BlockSpec((BLOCK, gw), lambda g, n: (n, g)),
                 out_shape=_sds((t, A_Q)))(q, k, v, bias, sink_b)


def attn_bwd(q, k, v, bias, bias_t, sink_b, do, o, name):
    t = q.shape[0]
    nblk = t // BLOCK
    scale = A_HEAD_DIM ** -0.5

    def body(q_ref, k_ref, v_ref, b_ref, bt_ref, s_ref, do_ref, o_ref, dq_ref, dk_ref, dv_ref, db_ref, ds_ref):
        n = pl.program_id(1)

        @pl.when(n == 0)
        def _():
            dk_ref[...] = jnp.zeros_like(dk_ref)
            dv_ref[...] = jnp.zeros_like(dv_ref)
            db_ref[...] = jnp.zeros_like(db_ref)
            ds_ref[...] = jnp.zeros_like(ds_ref)

        kb = _band_rows(k_ref, n, nblk).astype(BF16)
        vb = _band_rows(v_ref, n, nblk).astype(BF16)
        heads = range(A_GROUP)
        mask = _band_mask(n, nblk, False)
        mask_t = _band_mask(n, nblk, True)
        q4 = jnp.concatenate([q_ref[:, _head_cols(j)] for j in heads], axis=0).astype(BF16)
        do4 = jnp.concatenate([do_ref[:, _head_cols(j)] for j in heads], axis=0)
        doo = do4 * jnp.concatenate([o_ref[:, _head_cols(j)] for j in heads], axis=0)
        do4 = do4.astype(BF16)
        b4 = jnp.concatenate([b_ref[j] for j in heads], axis=0)
        bt4 = jnp.concatenate([bt_ref[j] for j in heads], axis=1)
        sk = jnp.concatenate([jnp.broadcast_to(s_ref[j:j + 1, 0:1], (BLOCK, 1)) for j in heads], axis=0)
        sk_t = jnp.concatenate([jnp.broadcast_to(s_ref[j:j + 1, 0:1], (1, BLOCK)) for j in heads], axis=1)
        s = jnp.where(mask, _dot(q4, kb, NT) * scale + b4, NEG_INF)
        mx = jnp.maximum(jnp.max(s, axis=1, keepdims=True), sk)
        p = jnp.exp(s - mx)
        den = jnp.sum(p, axis=1, keepdims=True) + jnp.exp(sk - mx)
        rden = 1.0 / den
        p = p * rden
        psink_delta = jnp.exp(sk - mx) * rden * jnp.sum(doo, axis=1, keepdims=True)
        dsc = p * (_dot(do4, vb, NT) - jnp.sum(doo, axis=1, keepdims=True))
        dq4 = _dot(dsc, kb) * scale
        for j in heads:
            rows = slice(j * BLOCK, (j + 1) * BLOCK)
            db_ref[j] += dsc[rows, :]
            ds_ref[j:j + 1, :] += jnp.broadcast_to(-jnp.sum(psink_delta[rows, :], axis=0, keepdims=True), (1, 128))
            dq_ref[:, _head_cols(j)] = dq4[rows, :]
        st = jnp.where(mask_t, _dot(kb, q4, NT) * scale + bt4, NEG_INF)
        mxt = jnp.maximum(jnp.max(st, axis=0, keepdims=True), sk_t)
        pt = jnp.exp(st - mxt)
        dent = jnp.sum(pt, axis=0, keepdims=True) + jnp.exp(sk_t - mxt)
        pt = pt * (1.0 / dent)
        delta_t = _dot(jnp.ones((8, A_HEAD_DIM), F32), doo, NT, hi=True)[0:1, :]
        dst = pt * (_dot(vb, do4, NT) - delta_t)
        dkb = _dot(dst, q4) * scale
        dvb = _dot(pt, do4)
        starts = [jnp.maximum(n - 1, 0), n, jnp.minimum(n + 1, nblk - 1)]
        for c, st_ in enumerate(starts):
            rows = pl.ds(pl.multiple_of(st_ * BLOCK, BLOCK), BLOCK)
            dk_ref[rows, :] += dkb[c * BLOCK:(c + 1) * BLOCK, :]
            dv_ref[rows, :] += dvb[c * BLOCK:(c + 1) * BLOCK, :]

    gw = A_GROUP * A_HEAD_DIM
    qspec = pl.BlockSpec((BLOCK, gw), lambda g, n: (n, g))
    kspec = pl.BlockSpec((t, A_HEAD_DIM), lambda g, n: (0, g))
    sspec = pl.BlockSpec((None, 8, 128), lambda g, n: (g, 0, 0))
    bspec = pl.BlockSpec((A_GROUP, BLOCK, 3 * BLOCK), lambda g, n: (g, 0, 0))
    btspec = pl.BlockSpec((A_GROUP, 3 * BLOCK, BLOCK), lambda g, n: (g, 0, 0))
    return _call(body, name=name, grid=(A_KV_HEADS, nblk),
                 in_specs=[qspec, kspec, kspec, bspec, btspec, sspec, qspec, qspec],
                 out_specs=(qspec, kspec, kspec, bspec, sspec),
                 out_shape=(_sds((t, A_Q)), _sds((t, A_KV)), _sds((t, A_KV)),
                            _sds((A_HEADS, BLOCK, 3 * BLOCK)), _sds((A_KV_HEADS, 8, 128))),
                 compiler_params=_cparams())(q, k, v, bias, bias_t, sink_b, do, o)


def xattn_fwd(q, k, v, name):
    t, d = q.shape
    ml = k.shape[0]
    dh = d // X_HEADS
    tq = min(512, t)
    scale = dh ** -0.5

    def body(q_ref, k_ref, v_ref, o_ref):
        s = _dot(q_ref[...], k_ref[...], NT) * scale
        p = jnp.exp(s - jnp.max(s, axis=1, keepdims=True))
        p = p * (1.0 / jnp.sum(p, axis=1, keepdims=True))
        o_ref[...] = _dot(p, v_ref[...])

    qspec = pl.BlockSpec((tq, dh), lambda h, i: (i, h))
    kspec = pl.BlockSpec((ml, dh), lambda h, i: (0, h))
    return _call(body, name=name, grid=(X_HEADS, t // tq), in_specs=[qspec, kspec, kspec], out_specs=qspec,
                 out_shape=_sds((t, d)))(q, k, v)


def xattn_bwd(q, k, v, o, do, name):
    t, d = q.shape
    ml = k.shape[0]
    dh = d // X_HEADS
    tq = min(512, t)
    scale = dh ** -0.5

    def body(q_ref, k_ref, v_ref, o_ref, do_ref, dq_ref, dk_ref, dv_ref):
        i = pl.program_id(1)
        qv, kv, vv = q_ref[...].astype(BF16), k_ref[...].astype(BF16), v_ref[...].astype(BF16)
        dov = do_ref[...]
        doo = dov * o_ref[...]
        dov = dov.astype(BF16)
        s = _dot(qv, kv, NT) * scale
        p = jnp.exp(s - jnp.max(s, axis=1, keepdims=True))
        p = p * (1.0 / jnp.sum(p, axis=1, keepdims=True))
        ds = p * (_dot(dov, vv, NT) - jnp.sum(doo, axis=1, keepdims=True))
        dq_ref[...] = _dot(ds, kv) * scale
        st = _dot(kv, qv, NT) * scale
        pt = jnp.exp(st - jnp.max(st, axis=0, keepdims=True))
        pt = pt * (1.0 / jnp.sum(pt, axis=0, keepdims=True))
        delta_t = _dot(jnp.ones((8, dh), F32), doo, NT, hi=True)[0:1, :]
        dst = pt * (_dot(vv, dov, NT) - delta_t)
        dkp = _dot(dst, qv) * scale
        dvp = _dot(pt, dov)

        @pl.when(i == 0)
        def _():
            dk_ref[...] = dkp
            dv_ref[...] = dvp

        @pl.when(i > 0)
        def _():
            dk_ref[...] += dkp
            dv_ref[...] += dvp

    qspec = pl.BlockSpec((tq, dh), lambda h, i: (i, h))
    kspec = pl.BlockSpec((ml, dh), lambda h, i: (0, h))
    return _call(body, name=name, grid=(X_HEADS, t // tq), in_specs=[qspec, kspec, kspec, qspec, qspec],
                 out_specs=(qspec, kspec, kspec),
                 out_shape=(_sds((t, d)), _sds((ml, d)), _sds((ml, d))))(q, k, v, o, do)


def scan_lead(a, u, name, reverse, inclusive):
    n, r, c = a.shape
    blk = max(1, min(n, (1 << 18) // (max(r, 8) * c)))
    while n % blk:
        blk -= 1
    nb = n // blk

    def body(a_ref, u_ref, o_ref, carry):
        @pl.when(pl.program_id(0) == 0)
        def _():
            carry[...] = jnp.zeros_like(carry)

        def step(s, h):
            idx = (blk - 1 - s) if reverse else s
            hn = a_ref[idx] * h + u_ref[idx]
            o_ref[idx] = hn if inclusive else h
            return hn

        carry[...] = lax.fori_loop(0, blk, step, carry[...], unroll=min(blk, SCAN_UNROLL))

    spec = pl.BlockSpec((blk, r, c), (lambda i: (nb - 1 - i, 0, 0)) if reverse else (lambda i: (i, 0, 0)))
    return _call(body, name=name, grid=(nb,), in_specs=[spec, spec], out_specs=spec,
                 out_shape=_sds((n, r, c)), scratch_shapes=[pltpu.VMEM((r, c), F32)])(a, u)


def _chunk_mats(bwd_dir):
    i = lax.broadcasted_iota(jnp.int32, (GLA_TILE, GLA_TILE), 0)
    j = lax.broadcasted_iota(jnp.int32, (GLA_TILE, GLA_TILE), 1)
    same = lax.shift_right_logical(i, 4) == lax.shift_right_logical(j, 4)
    if bwd_dir:
        cm, cm_t = same & (j >= i), same & (i >= j)
        mk, mk_t = same & (j > i), same & (i > j)
    else:
        cm, cm_t = same & (j <= i), same & (i <= j)
        mk, mk_t = same & (j <= i), same & (i <= j)
    f = lambda b: jnp.where(b, 1.0, 0.0).astype(F32)
    return f(cm), f(cm_t), mk, mk_t, f(same)


def gla_gates_fwd(zf, zb, w2f, b2f, w2b, b2b, name):
    t = zf.shape[0]
    tm = min(256, t)

    def body(zf_ref, zb_ref, wf_ref, bf_ref, wb_ref, bb_ref, lf_ref, lb_ref):
        lf_ref[...] = -_softplus(-(_dot(zf_ref[...], wf_ref[...], hi=True) + bf_ref[...])) / GATE_TAU
        lb_ref[...] = -_softplus(-(_dot(zb_ref[...], wb_ref[...], hi=True) + bb_ref[...])) / GATE_TAU

    zs = pl.BlockSpec((tm, GATE_RANK), lambda i: (i, 0))
    ws = pl.BlockSpec((GATE_RANK, B_QK), lambda i: (0, 0))
    bs = pl.BlockSpec((1, B_QK), lambda i: (0, 0))
    os_ = pl.BlockSpec((tm, B_QK), lambda i: (i, 0))
    return _call(body, name=name, grid=(t // tm,), in_specs=[zs, zs, ws, bs, ws, bs], out_specs=(os_, os_),
                 out_shape=(_sds((t, B_QK)),) * 2)(zf, zb, w2f, b2f.reshape(1, B_QK), w2b, b2b.reshape(1, B_QK))


def gla_gates_bwd(zf, zb, w2f, b2f, w2b, b2b, dlf, dlb, name):
    t = zf.shape[0]
    tm = min(256, t)

    def body(zf_ref, zb_ref, wf_ref, bf_ref, wb_ref, bb_ref, dlf_ref, dlb_ref,
             dzf_ref, dzb_ref, dpf_ref, dpb_ref, dbf_ref, dbb_ref):
        first = pl.program_id(0) == 0
        for z_ref, w_ref, b_ref, dl_ref, dz_ref, dp_ref, db_ref in (
                (zf_ref, wf_ref, bf_ref, dlf_ref, dzf_ref, dpf_ref, dbf_ref),
                (zb_ref, wb_ref, bb_ref, dlb_ref, dzb_ref, dpb_ref, dbb_ref)):
            pre = _dot(z_ref[...], w_ref[...], hi=True) + b_ref[...]
            dpre = dl_ref[...] * (1.0 / GATE_TAU) * _sigmoid(-pre)
            dp_ref[...] = dpre
            dz_ref[...] = _dot(dpre, w_ref[...], NT, hi=True)
            part = jnp.sum(dpre, axis=0, keepdims=True)

            @pl.when(first)
            def _():
                db_ref[...] = part

            @pl.when(jnp.logical_not(first))
            def _():
                db_ref[...] += part

    zs = pl.BlockSpec((tm, GATE_RANK), lambda i: (i, 0))
    ws = pl.BlockSpec((GATE_RANK, B_QK), lambda i: (0, 0))
    bs = pl.BlockSpec((1, B_QK), lambda i: (0, 0))
    os_ = pl.BlockSpec((tm, B_QK), lambda i: (i, 0))
    return _call(body, name=name, grid=(t // tm,), in_specs=[zs, zs, ws, bs, ws, bs, os_, os_],
                 out_specs=(zs, zs, os_, os_, bs, bs),
                 out_shape=(_sds((t, GATE_RANK)),) * 2 + (_sds((t, B_QK)),) * 2 + (_sds((1, B_QK)),) * 2)(
        zf, zb, w2f, b2f.reshape(1, B_QK), w2b, b2b.reshape(1, B_QK), dlf, dlb)


def gla_outer(xt, lat, y, name, bwd_dir, mode):
    t = y.shape[0]
    nchunk = t // GLA_CHUNK
    khat = mode == "khat"
    scale = B_KEY_DIM ** -0.5

    def body(xt_ref, lat_ref, y_ref, *outs):
        _, cm_t, _, _, same = _chunk_mats(bwd_dir)
        lat_v = lat_ref[...]
        bt = _dot(lat_v, cm_t, hi=True)
        if khat:
            mult = jnp.exp(_dot(lat_v, same, hi=True) - bt)
        else:
            mult = jnp.exp(bt) * scale
        xm = xt_ref[...] * mult
        lane = lax.shift_right_logical(lax.broadcasted_iota(jnp.int32, (1, GLA_TILE), 1), 4)
        ones = jnp.ones((GLA_TILE, B_VAL_DIM), F32)
        yv = [y_ref[:, h * B_VAL_DIM:(h + 1) * B_VAL_DIM].astype(BF16) for h in range(B_HEADS)]
        for c in range(CHUNKS_PER_TILE):
            sel = lane == c
            xc = jnp.where(sel, xm, 0.0).astype(BF16)
            for h in range(B_HEADS):
                rows = slice(h * B_KEY_DIM, (h + 1) * B_KEY_DIM)
                outs[0][c, rows, :] = _dot(xc[rows, :], yv[h])
            if khat:
                outs[1][c] = jnp.exp(_dot(jnp.where(sel, lat_v, 0.0), ones, hi=True))

    tspec = pl.BlockSpec((B_QK, GLA_TILE), lambda i: (0, i))
    ospec = pl.BlockSpec((CHUNKS_PER_TILE, B_QK, B_VAL_DIM), lambda i: (i, 0, 0))
    oshape = _sds((nchunk, B_QK, B_VAL_DIM))
    return _call(body, name=name, grid=(t // GLA_TILE,),
                 in_specs=[tspec, tspec, pl.BlockSpec((GLA_TILE, B_V), lambda i: (i, 0))],
                 out_specs=(ospec, ospec) if khat else ospec,
                 out_shape=(oshape, oshape) if khat else oshape)(xt, lat, y)


def _head_lane_mask(h):
    lane = lax.broadcasted_iota(jnp.int32, (1, B_QK), 1)
    return lax.shift_right_logical(lane, 6) == h


def _chunk_rows(c):
    return slice(c * GLA_CHUNK, (c + 1) * GLA_CHUNK)


def gla_inner_fwd(q, k, v, la, sp, name, bwd_dir):
    t = q.shape[0]
    scale = B_KEY_DIM ** -0.5

    def body(q_ref, k_ref, v_ref, la_ref, sp_ref, o_ref):
        cm, _, mk, _, _ = _chunk_mats(bwd_dir)
        b = _dot(cm, la_ref[...], hi=True)
        qt = q_ref[...] * scale * jnp.exp(b)
        kt = k_ref[...] * jnp.exp(jnp.minimum(-b, EXP_CLAMP))
        spb = [sp_ref[c].astype(BF16) for c in range(CHUNKS_PER_TILE)]
        for h in range(B_HEADS):
            lm = _head_lane_mask(h)
            qm = jnp.where(lm, qt, 0.0).astype(BF16)
            km = jnp.where(lm, kt, 0.0).astype(BF16)
            vs = slice(h * B_VAL_DIM, (h + 1) * B_VAL_DIM)
            att = jnp.where(mk, _dot(qm, km, NT), 0.0)
            inter = jnp.concatenate([_dot(qm[_chunk_rows(c), :], spb[c]) for c in range(CHUNKS_PER_TILE)], axis=0)
            o_ref[:, vs] = _dot(att, v_ref[:, vs]) + inter

    qs = pl.BlockSpec((GLA_TILE, B_QK), lambda i: (i, 0))
    vs_ = pl.BlockSpec((GLA_TILE, B_V), lambda i: (i, 0))
    ss = pl.BlockSpec((CHUNKS_PER_TILE, B_QK, B_VAL_DIM), lambda i: (i, 0, 0))
    return _call(body, name=name, grid=(t // GLA_TILE,), in_specs=[qs, qs, vs_, qs, ss], out_specs=vs_,
                 out_shape=_sds((t, B_V)))(q, k, v, la, sp)


def gla_inner_bwd(q, k, v, la, do, sp, gs, dec, name, bwd_dir, add=None):
    t = q.shape[0]
    scale = B_KEY_DIM ** -0.5
    hasadd = add is not None

    def body(*refs):
        it = iter(refs)
        q_ref, k_ref, v_ref, la_ref, do_ref, sp_ref, gs_ref, dec_ref = [next(it) for _ in range(8)]
        adds = [next(it) for _ in range(3)] if hasadd else None
        dq_ref, dk_ref, dv_ref, dla_ref = [next(it) for _ in range(4)]
        cm, cm_t, mk, mk_t, same = _chunk_mats(bwd_dir)
        la_v = la_ref[...]
        b = _dot(cm, la_v, hi=True)
        btot = _dot(same, la_v, hi=True)
        eb = jnp.exp(b)
        ek = jnp.exp(jnp.minimum(-b, EXP_CLAMP))
        ekh = jnp.exp(btot - b)
        qt = q_ref[...] * scale * eb
        kt = k_ref[...] * ek
        kh = k_ref[...] * ekh
        spb = [sp_ref[c].astype(BF16) for c in range(CHUNKS_PER_TILE)]
        gsb = [gs_ref[c].astype(BF16) for c in range(CHUNKS_PER_TILE)]
        dqt = jnp.zeros((GLA_TILE, B_QK), F32)
        dkt = jnp.zeros((GLA_TILE, B_QK), F32)
        dkh = jnp.zeros((GLA_TILE, B_QK), F32)
        for h in range(B_HEADS):
            lm = _head_lane_mask(h)
            qm = jnp.where(lm, qt, 0.0).astype(BF16)
            km = jnp.where(lm, kt, 0.0).astype(BF16)
            khm = jnp.where(lm, kh, 0.0).astype(BF16)
            vs = slice(h * B_VAL_DIM, (h + 1) * B_VAL_DIM)
            vh = v_ref[:, vs].astype(BF16)
            doh = do_ref[:, vs].astype(BF16)
            da = jnp.where(mk, _dot(doh, vh, NT), 0.0)
            da_t = jnp.where(mk_t, _dot(vh, doh, NT), 0.0)
            att_t = jnp.where(mk_t, _dot(km, qm, NT), 0.0)
            dv_h = _dot(att_t, doh) + jnp.concatenate(
                [_dot(khm[_chunk_rows(c), :], gsb[c]) for c in range(CHUNKS_PER_TILE)], axis=0)
            if hasadd:
                dv_h = dv_h + adds[2][:, vs]
            dv_ref[:, vs] = dv_h
            dq_inter = jnp.concatenate(
                [_dot(doh[_chunk_rows(c), :], spb[c], NT) for c in range(CHUNKS_PER_TILE)], axis=0)
            dqt = dqt + _dot(da, km) + jnp.where(lm, dq_inter, 0.0)
            dkt = dkt + _dot(da_t, qm)
            dkh_inter = jnp.concatenate(
                [_dot(vh[_chunk_rows(c), :], gsb[c], NT) for c in range(CHUNKS_PER_TILE)], axis=0)
            dkh = dkh + jnp.where(lm, dkh_inter, 0.0)
        dq = dqt * scale * eb
        dk = dkt * ek + dkh * ekh
        if hasadd:
            dq = dq + adds[0][...]
            dk = dk + adds[1][...]
        dq_ref[...] = dq
        dk_ref[...] = dk
        db = dqt * qt - dkt * kt - dkh * kh
        ones16 = jnp.ones((GLA_CHUNK, B_VAL_DIM), F32)
        t2 = jnp.concatenate(
            [_dot(ones16, gs_ref[c] * dec_ref[c] * sp_ref[c], NT, hi=True) for c in range(CHUNKS_PER_TILE)], axis=0)
        dla_ref[...] = _dot(cm_t, db, hi=True) + _dot(same, dkh * kh, hi=True) + t2

    qs = pl.BlockSpec((GLA_TILE, B_QK), lambda i: (i, 0))
    vs_ = pl.BlockSpec((GLA_TILE, B_V), lambda i: (i, 0))
    ss = pl.BlockSpec((CHUNKS_PER_TILE, B_QK, B_VAL_DIM), lambda i: (i, 0, 0))
    ins = [q, k, v, la, do, sp, gs, dec] + (list(add) if hasadd else [])
    in_specs = [qs, qs, vs_, qs, vs_, ss, ss, ss] + ([qs, qs, vs_] if hasadd else [])
    return _call(body, name=name, grid=(t // GLA_TILE,), in_specs=in_specs, out_specs=(qs, qs, vs_, qs),
                 out_shape=(_sds((t, B_QK)), _sds((t, B_QK)), _sds((t, B_V)), _sds((t, B_QK))),
                 compiler_params=_cparams())(*ins)


def gla_out_fwd(of, ob, g, gn, name):
    t = of.shape[0]
    tm = min(256, t)

    def body(of_ref, ob_ref, g_ref, gn_ref, o_ref):
        for h in range(B_HEADS):
            vs = slice(h * B_VAL_DIM, (h + 1) * B_VAL_DIM)
            o = of_ref[:, vs] + ob_ref[:, vs]
            on = o * lax.rsqrt(jnp.mean(o * o, axis=1, keepdims=True) + EPS)
            gv = g_ref[:, vs]
            o_ref[:, vs] = on * gn_ref[:, vs] * (gv * _sigmoid(gv))

    row = pl.BlockSpec((tm, B_V), lambda i: (i, 0))
    vec = pl.BlockSpec((1, B_V), lambda i: (0, 0))
    return _call(body, name=name, grid=(t // tm,), in_specs=[row, row, row, vec], out_specs=row,
                 out_shape=_sds((t, B_V)))(of, ob, g, gn.reshape(1, B_V))


def gla_out_bwd(of, ob, g, gn, dout, name):
    t = of.shape[0]
    tm = min(256, t)

    def body(of_ref, ob_ref, g_ref, gn_ref, d_ref, do_ref, dg_ref, dgn_ref):
        first = pl.program_id(0) == 0
        for h in range(B_HEADS):
            vs = slice(h * B_VAL_DIM, (h + 1) * B_VAL_DIM)
            o = of_ref[:, vs] + ob_ref[:, vs]
            r = lax.rsqrt(jnp.mean(o * o, axis=1, keepdims=True) + EPS)
            on = o * r
            gv = g_ref[:, vs]
            sg = _sigmoid(gv)
            silu = gv * sg
            dv = d_ref[:, vs]
            gnv = gn_ref[:, vs]
            dg_ref[:, vs] = dv * on * gnv * (sg * (1.0 + gv * (1.0 - sg)))
            don = dv * silu * gnv
            do_ref[:, vs] = r * (don - on * jnp.mean(don * on, axis=1, keepdims=True))
            part = jnp.sum(dv * silu * on, axis=0, keepdims=True)

            @pl.when(first)
            def _():
                dgn_ref[:, vs] = part

            @pl.when(jnp.logical_not(first))
            def _():
                dgn_ref[:, vs] += part

    row = pl.BlockSpec((tm, B_V), lambda i: (i, 0))
    vec = pl.BlockSpec((1, B_V), lambda i: (0, 0))
    return _call(body, name=name, grid=(t // tm,), in_specs=[row, row, row, vec, row], out_specs=(row, row, vec),
                 out_shape=(_sds((t, B_V)), _sds((t, B_V)), _sds((1, B_V))))(of, ob, g, gn.reshape(1, B_V), dout)


def _shift(x, k):
    if k > 0:
        return jnp.concatenate([x[k:], jnp.zeros((k,) + x.shape[1:], x.dtype)], axis=0)
    return jnp.concatenate([jnp.zeros((-k,) + x.shape[1:], x.dtype), x[:k]], axis=0)


def _lru_gates(xc, s, wa_ref, ba_ref, wx_ref, bx_ref, lam_ref):
    cols = [slice(g * C_BLOCK_DIM, (g + 1) * C_BLOCK_DIM) for g in range(C_BLOCKS)]
    zr = jnp.concatenate([_dot(xc[:, cs], wa_ref[s, g]) for g, cs in enumerate(cols)], axis=1) + ba_ref[s:s + 1, :]
    zi = jnp.concatenate([_dot(xc[:, cs], wx_ref[s, g]) for g, cs in enumerate(cols)], axis=1) + bx_ref[s:s + 1, :]
    r = _sigmoid(zr)
    i = _sigmoid(zi)
    sp = _softplus(-lam_ref[s:s + 1, :])
    log_a = -LRU_C * r * sp
    return r, i, sp, log_a


def lru_gates_fwd(x0, xm2, xm1, xp1, cw, cb, wa, ba, wx, bx, lam, name):
    t = x0.shape[0]
    tm = min(256, t)

    def body(x0_ref, xm2_ref, xm1_ref, xp1_ref, cw_ref, cb_ref, wa_ref, ba_ref, wx_ref, bx_ref, lam_ref,
             xc_ref, a0_ref, u0_ref, a1_ref, u1_ref):
        xc = (xm2_ref[...] * cw_ref[0:1, :] + xm1_ref[...] * cw_ref[1:2, :] + x0_ref[...] * cw_ref[2:3, :]
              + xp1_ref[...] * cw_ref[3:4, :] + cb_ref[...])
        xc_ref[...] = xc
        for s, (a_ref, u_ref) in enumerate(((a0_ref, u0_ref), (a1_ref, u1_ref))):
            _, i, _, log_a = _lru_gates(xc, s, wa_ref, ba_ref, wx_ref, bx_ref, lam_ref)
            a_ref[...] = jnp.exp(log_a)
            u_ref[...] = jnp.sqrt(-_expm1(2.0 * log_a)) * (i * xc)

    row = pl.BlockSpec((tm, C_WIDTH), lambda i: (i, 0))
    full = lambda shape: pl.BlockSpec(shape, lambda i: (0,) * len(shape))
    wshape = (2, C_BLOCKS, C_BLOCK_DIM, C_BLOCK_DIM)
    return _call(body, name=name, grid=(t // tm,),
                 in_specs=[row] * 4 + [full((4, C_WIDTH)), full((1, C_WIDTH)), full(wshape), full((2, C_WIDTH)),
                                       full(wshape), full((2, C_WIDTH)), full((2, C_WIDTH))],
                 out_specs=(row,) * 5, out_shape=(_sds((t, C_WIDTH)),) * 5)(
        x0, xm2, xm1, xp1, cw, cb.reshape(1, C_WIDTH), wa, ba, wx, bx, lam)


def lru_gates_bwd(xc, g0, hs0, g1, hs1, wa, ba, wx, bx, lam, name):
    t = xc.shape[0]
    tm = min(256, t)

    def body(xc_ref, g0_ref, hs0_ref, g1_ref, hs1_ref, wa_ref, ba_ref, wx_ref, bx_ref, lam_ref,
             dxc_ref, dzr0_ref, dzi0_ref, dzr1_ref, dzi1_ref, dlam_ref, dba_ref, dbx_ref):
        first = pl.program_id(0) == 0

        @pl.when(first)
        def _():
            dlam_ref[...] = jnp.zeros_like(dlam_ref)
            dba_ref[...] = jnp.zeros_like(dba_ref)
            dbx_ref[...] = jnp.zeros_like(dbx_ref)

        xcv = xc_ref[...]
        dxc = jnp.zeros_like(xcv)
        cols = [slice(g * C_BLOCK_DIM, (g + 1) * C_BLOCK_DIM) for g in range(C_BLOCKS)]
        for s, (g_ref, hs_ref, dzr_ref, dzi_ref) in enumerate(
                ((g0_ref, hs0_ref, dzr0_ref, dzi0_ref), (g1_ref, hs1_ref, dzr1_ref, dzi1_ref))):
            r, i, sp, log_a = _lru_gates(xcv, s, wa_ref, ba_ref, wx_ref, bx_ref, lam_ref)
            du = g_ref[...]
            da = du * hs_ref[...]
            a = jnp.exp(log_a)
            e2 = jnp.exp(2.0 * log_a)
            c = jnp.sqrt(-_expm1(2.0 * log_a))
            ix = i * xcv
            dlog = da * a - du * ix * (e2 / c)
            dix = du * c
            dxc = dxc + dix * i
            dzi = dix * xcv * i * (1.0 - i)
            dzr = dlog * (-LRU_C * sp) * r * (1.0 - r)
            dzr_ref[...] = dzr
            dzi_ref[...] = dzi
            dxc = dxc + jnp.concatenate(
                [_dot(dzr[:, cs], wa_ref[s, g], NT) + _dot(dzi[:, cs], wx_ref[s, g], NT) for g, cs in enumerate(cols)],
                axis=1)
            dsp = jnp.sum(dlog * (-LRU_C * r), axis=0, keepdims=True)
            dlam_ref[s:s + 1, :] += dsp * (-_sigmoid(-lam_ref[s:s + 1, :]))
            dba_ref[s:s + 1, :] += jnp.sum(dzr, axis=0, keepdims=True)
            dbx_ref[s:s + 1, :] += jnp.sum(dzi, axis=0, keepdims=True)
        dxc_ref[...] = dxc

    row = pl.BlockSpec((tm, C_WIDTH), lambda i: (i, 0))
    full = lambda shape: pl.BlockSpec(shape, lambda i: (0,) * len(shape))
    wshape = (2, C_BLOCKS, C_BLOCK_DIM, C_BLOCK_DIM)
    vec2 = full((2, C_WIDTH))
    return _call(body, name=name, grid=(t // tm,),
                 in_specs=[row] * 5 + [full(wshape), vec2, full(wshape), vec2, vec2],
                 out_specs=(row,) * 5 + (vec2,) * 3,
                 out_shape=(_sds((t, C_WIDTH)),) * 5 + (_sds((2, C_WIDTH)),) * 3)(
        xc, g0, hs0, g1, hs1, wa, ba, wx, bx, lam)


def lru_out_fwd(h0, h1, y, name):
    t = y.shape[0]
    tm = min(256, t)

    def body(h0_ref, h1_ref, y_ref, o_ref):
        o_ref[...] = (h0_ref[...] + h1_ref[...]) * _gelu(y_ref[...])

    row = pl.BlockSpec((tm, C_WIDTH), lambda i: (i, 0))
    return _call(body, name=name, grid=(t // tm,), in_specs=[row] * 3, out_specs=row,
                 out_shape=_sds((t, C_WIDTH)))(h0, h1, y)


def lru_out_bwd(h0, h1, y, dout, name):
    t = y.shape[0]
    tm = min(256, t)

    def body(h0_ref, h1_ref, y_ref, d_ref, dh_ref, dy_ref):
        yv = y_ref[...]
        dv = d_ref[...]
        dh_ref[...] = dv * _gelu(yv)
        dy_ref[...] = dv * (h0_ref[...] + h1_ref[...]) * _gelu_grad(yv)

    row = pl.BlockSpec((tm, C_WIDTH), lambda i: (i, 0))
    return _call(body, name=name, grid=(t // tm,), in_specs=[row] * 4, out_specs=(row, row),
                 out_shape=(_sds((t, C_WIDTH)),) * 2)(h0, h1, y, dout)


def conv_bwd(dxc, dp2, dp1, dm1, x0, xm2, xm1, xp1, cw, name):
    t = x0.shape[0]
    tm = min(256, t)

    def body(d_ref, dp2_ref, dp1_ref, dm1_ref, x0_ref, xm2_ref, xm1_ref, xp1_ref, cw_ref, dx_ref, dcw_ref, dcb_ref):
        @pl.when(pl.program_id(0) == 0)
        def _():
            dcw_ref[...] = jnp.zeros_like(dcw_ref)
            dcb_ref[...] = jnp.zeros_like(dcb_ref)

        dv = d_ref[...]
        dx_ref[...] = (dp2_ref[...] * cw_ref[0:1, :] + dp1_ref[...] * cw_ref[1:2, :] + dv * cw_ref[2:3, :]
                       + dm1_ref[...] * cw_ref[3:4, :])
        for j, x_ref in enumerate((xm2_ref, xm1_ref, x0_ref, xp1_ref)):
            dcw_ref[j:j + 1, :] += jnp.sum(dv * x_ref[...], axis=0, keepdims=True)
        dcb_ref[...] += jnp.sum(dv, axis=0, keepdims=True)

    row = pl.BlockSpec((tm, C_WIDTH), lambda i: (i, 0))
    cws = pl.BlockSpec((4, C_WIDTH), lambda i: (0, 0))
    cbs = pl.BlockSpec((1, C_WIDTH), lambda i: (0, 0))
    return _call(body, name=name, grid=(t // tm,), in_specs=[row] * 8 + [cws], out_specs=(row, cws, cbs),
                 out_shape=(_sds((t, C_WIDTH)), _sds((4, C_WIDTH)), _sds((1, C_WIDTH))))(
        dxc, dp2, dp1, dm1, x0, xm2, xm1, xp1, cw)


def _my_place():
    return lax.axis_index("x"), lax.axis_index("y"), lax.axis_index("c")


def all_gather(xs, name):
    r, c = xs.shape

    def body(x_ref, out_ref, send_sems, recv_sems, local_sem):
        x, y, cc = _my_place()
        me, sibling = (x, y, cc), (x, y, 1 - cc)
        chips = [(1 - x, y), (x, 1 - y), (1 - x, 1 - y)]

        def slot(px, py, pc):
            return out_ref.at[4 * px + 2 * py + pc]

        def copy(k, block, to, src=None):
            return pltpu.make_async_remote_copy(
                src_ref=slot(*block) if src is None else src, dst_ref=slot(*block),
                send_sem=send_sems.at[k], recv_sem=recv_sems.at[k], device_id=to, device_id_type=MESH)

        mine = pltpu.make_async_copy(x_ref, slot(*me), local_sem)
        mine.start()
        first = [copy(0, me, sibling, src=x_ref)]
        first += [copy(1 + j, me, (*chip, cc), src=x_ref) for j, chip in enumerate(chips)]
        for cp in first:
            cp.start()
        passed = [copy(4 + j, (*chip, cc), sibling) for j, chip in enumerate(chips)]
        for j, chip in enumerate(chips):
            copy(1 + j, (*chip, cc), me).wait_recv()
            passed[j].start()
        copy(0, sibling, me).wait_recv()
        for j, chip in enumerate(chips):
            copy(4 + j, (*chip, 1 - cc), me).wait_recv()
        for cp in first + passed:
            cp.wait_send()
        mine.wait()

    return _call(body, name=name, in_specs=[pl.BlockSpec(memory_space=pl.ANY)],
                 out_specs=pl.BlockSpec(memory_space=pl.ANY), out_shape=_sds((N_DEV, r, c), xs.dtype),
                 scratch_shapes=[pltpu.SemaphoreType.DMA((7,)), pltpu.SemaphoreType.DMA((7,)),
                                 pltpu.SemaphoreType.DMA])(xs)


def _stream_rows(r):
    nch = SIBLING_STREAMS // 4 if r % (8 * (SIBLING_STREAMS // 4)) == 0 else 1
    return nch, r // nch


def exchange_sibling(gw, name):
    _, r, c = gw.shape
    g5 = gw.reshape(4, 2, r, c)
    nch, rows = _stream_rows(r)

    def body(g_ref, out_ref, send_sems, recv_sems):
        x, y, cc = _my_place()
        swaps = []
        for q in range(4):
            for s in range(nch):
                k = q * nch + s
                win = pl.ds(s * rows, rows)
                swaps.append(pltpu.make_async_remote_copy(
                    src_ref=g_ref.at[q, 1 - cc, win], dst_ref=out_ref.at[q, win], send_sem=send_sems.at[k],
                    recv_sem=recv_sems.at[k], device_id=(x, y, 1 - cc), device_id_type=MESH))
        for cp in swaps:
            cp.start()
        for cp in swaps:
            cp.wait()

    nsem = 4 * nch
    return _call(body, name=name, in_specs=[pl.BlockSpec(memory_space=pl.ANY)],
                 out_specs=pl.BlockSpec(memory_space=pl.ANY), out_shape=_sds((4, r, c), gw.dtype),
                 scratch_shapes=[pltpu.SemaphoreType.DMA((nsem,)), pltpu.SemaphoreType.DMA((nsem,))])(g5)


HBM_SPEC = pl.BlockSpec(memory_space=pltpu.HBM)
SEM_SPEC = pl.BlockSpec(memory_space=pltpu.SEMAPHORE)
DATAFLOW = pltpu.SideEffectType.DATAFLOW_SIDE_EFFECTING


def _hbm(a):
    return pltpu.with_memory_space_constraint(a, pltpu.HBM)


def _peers(x, y, cc):
    return [(x, y, 1 - cc), (1 - x, y, cc), (x, 1 - y, cc), (1 - x, 1 - y, cc)]


def _slot(p):
    return 4 * p[0] + 2 * p[1] + p[2]


def gather_start(blk, name):
    r, c = blk.shape

    def body(v_ref, land_ref, send_sems, recv_sems, v_thru, land_thru, token):
        x, y, cc = _my_place()
        for k, to in enumerate(_peers(x, y, cc)):
            pltpu.make_async_remote_copy(
                src_ref=v_ref, dst_ref=land_ref.at[_slot((x, y, cc))], send_sem=send_sems.at[k],
                recv_sem=recv_sems.at[k], device_id=to, device_id_type=MESH).start()
        pltpu.make_async_copy(v_ref, land_ref.at[_slot((x, y, cc))], send_sems.at[4]).start()
        token[...] = jnp.zeros_like(token)

    return _call(
        body, name=name,
        out_shape=(pltpu.SemaphoreType.DMA((5,)), pltpu.SemaphoreType.DMA((4,)), pltpu.HBM((r, c), blk.dtype),
                   pltpu.HBM((N_DEV, r, c), blk.dtype), _sds((8, 128))),
        in_specs=(HBM_SPEC, HBM_SPEC),
        out_specs=(SEM_SPEC, SEM_SPEC, HBM_SPEC, HBM_SPEC, pl.BlockSpec(memory_space=pltpu.VMEM)),
        input_output_aliases={0: 2, 1: 3},
        compiler_params=pltpu.CompilerParams(has_side_effects=DATAFLOW),
    )(_hbm(blk), _hbm(lax.empty((N_DEV, r, c), blk.dtype)))


def gather_wait(send_sems, recv_sems, v_thru, land_thru, after, name):
    def body(v_ref, land_ref, send_sems, recv_sems, after_ref, v_out, land_out):
        x, y, cc = _my_place()
        for k, peer in enumerate(_peers(x, y, cc)):
            cp = pltpu.make_async_remote_copy(
                src_ref=v_ref, dst_ref=land_ref.at[_slot(peer)], send_sem=send_sems.at[k], recv_sem=recv_sems.at[k],
                device_id=peer, device_id_type=MESH)
            cp.wait_send()
            cp.wait_recv()
        pltpu.make_async_copy(v_ref, land_ref.at[_slot((x, y, cc))], send_sems.at[4]).wait()

    return _call(
        body, name=name,
        out_shape=(pltpu.HBM(v_thru.shape, v_thru.dtype), pltpu.HBM(land_thru.shape, land_thru.dtype)),
        in_specs=(HBM_SPEC, HBM_SPEC, SEM_SPEC, SEM_SPEC, pl.BlockSpec(memory_space=pl.ANY)),
        out_specs=(HBM_SPEC, HBM_SPEC), input_output_aliases={0: 0, 1: 1},
        compiler_params=pltpu.CompilerParams(has_side_effects=DATAFLOW),
    )(v_thru, land_thru, send_sems, recv_sems, after)


def gather_pass(land, name):
    _, r, c = land.shape
    nch, rows = _stream_rows(r)

    def body(land_ref, out_ref, send_sems, recv_sems):
        x, y, cc = _my_place()
        peers = _peers(x, y, cc)
        copies = []
        for j in range(3):
            mine, theirs = _slot(peers[1 + j]), _slot((peers[1 + j][0], peers[1 + j][1], 1 - cc))
            for s in range(nch):
                k = j * nch + s
                win = pl.ds(s * rows, rows)
                send = pltpu.make_async_remote_copy(
                    src_ref=land_ref.at[mine, win], dst_ref=out_ref.at[mine, win], send_sem=send_sems.at[k],
                    recv_sem=recv_sems.at[k], device_id=peers[0], device_id_type=MESH)
                recv = pltpu.make_async_remote_copy(
                    src_ref=land_ref.at[mine, win], dst_ref=out_ref.at[theirs, win], send_sem=send_sems.at[k],
                    recv_sem=recv_sems.at[k], device_id=peers[0], device_id_type=MESH)
                copies.append((send, recv))
        for send, _ in copies:
            send.start()
        for send, recv in copies:
            send.wait_send()
            recv.wait_recv()

    nsem = 3 * nch
    return _call(body, name=name, in_specs=[pl.BlockSpec(memory_space=pl.ANY)],
                 out_specs=pl.BlockSpec(memory_space=pl.ANY), out_shape=_sds(land.shape, land.dtype),
                 input_output_aliases={0: 0},
                 scratch_shapes=[pltpu.SemaphoreType.DMA((nsem,)), pltpu.SemaphoreType.DMA((nsem,))])(land)


def pass_start(land, name):
    _, r, c = land.shape
    nch, rows = _stream_rows(r)
    nsem = 3 * nch

    def body(land_ref, send_sems, recv_sems, land_thru, token):
        x, y, cc = _my_place()
        peers = _peers(x, y, cc)
        for j in range(3):
            mine = _slot(peers[1 + j])
            for s in range(nch):
                win = pl.ds(s * rows, rows)
                pltpu.make_async_remote_copy(
                    src_ref=land_ref.at[mine, win], dst_ref=land_ref.at[mine, win], send_sem=send_sems.at[j * nch + s],
                    recv_sem=recv_sems.at[j * nch + s], device_id=peers[0], device_id_type=MESH).start()
        token[...] = jnp.zeros_like(token)

    return _call(
        body, name=name,
        out_shape=(pltpu.SemaphoreType.DMA((nsem,)), pltpu.SemaphoreType.DMA((nsem,)),
                   pltpu.HBM(land.shape, land.dtype), _sds((8, 128))),
        in_specs=(HBM_SPEC,),
        out_specs=(SEM_SPEC, SEM_SPEC, HBM_SPEC, pl.BlockSpec(memory_space=pltpu.VMEM)),
        input_output_aliases={0: 2},
        compiler_params=pltpu.CompilerParams(has_side_effects=DATAFLOW),
    )(_hbm(land))


def pass_wait(send_sems, recv_sems, land_thru, after, name):
    _, r, c = land_thru.shape
    nch, rows = _stream_rows(r)

    def body(land_ref, send_sems, recv_sems, after_ref, land_out):
        x, y, cc = _my_place()
        peers = _peers(x, y, cc)
        for j in range(3):
            mine, theirs = _slot(peers[1 + j]), _slot((peers[1 + j][0], peers[1 + j][1], 1 - cc))
            for s in range(nch):
                win = pl.ds(s * rows, rows)
                k = j * nch + s
                pltpu.make_async_remote_copy(
                    src_ref=land_ref.at[mine, win], dst_ref=land_ref.at[mine, win], send_sem=send_sems.at[k],
                    recv_sem=recv_sems.at[k], device_id=peers[0], device_id_type=MESH).wait_send()
                pltpu.make_async_remote_copy(
                    src_ref=land_ref.at[mine, win], dst_ref=land_ref.at[theirs, win], send_sem=send_sems.at[k],
                    recv_sem=recv_sems.at[k], device_id=peers[0], device_id_type=MESH).wait_recv()

    return _call(
        body, name=name, out_shape=pltpu.HBM(land_thru.shape, land_thru.dtype),
        in_specs=(HBM_SPEC, SEM_SPEC, SEM_SPEC, pl.BlockSpec(memory_space=pl.ANY)),
        out_specs=HBM_SPEC, input_output_aliases={0: 0},
        compiler_params=pltpu.CompilerParams(has_side_effects=DATAFLOW),
    )(land_thru, send_sems, recv_sems, after)


def chips_start(p, name):
    _, r, c = p.shape

    def body(p_ref, land_ref, send_sems, recv_sems, p_thru, land_thru, token):
        x, y, cc = _my_place()
        for j, (px, py, pc) in enumerate(_peers(x, y, cc)[1:]):
            pltpu.make_async_remote_copy(
                src_ref=p_ref.at[2 * px + py], dst_ref=land_ref.at[j], send_sem=send_sems.at[j],
                recv_sem=recv_sems.at[j], device_id=(px, py, pc), device_id_type=MESH).start()
        token[...] = jnp.zeros_like(token)

    return _call(
        body, name=name,
        out_shape=(pltpu.SemaphoreType.DMA((3,)), pltpu.SemaphoreType.DMA((3,)), pltpu.HBM(p.shape, p.dtype),
                   pltpu.HBM((3, r, c), p.dtype), _sds((8, 128))),
        in_specs=(HBM_SPEC, HBM_SPEC),
        out_specs=(SEM_SPEC, SEM_SPEC, HBM_SPEC, HBM_SPEC, pl.BlockSpec(memory_space=pltpu.VMEM)),
        input_output_aliases={0: 2, 1: 3},
        compiler_params=pltpu.CompilerParams(has_side_effects=DATAFLOW),
    )(_hbm(p), _hbm(lax.empty((3, r, c), p.dtype)))


def chips_wait(send_sems, recv_sems, p_thru, land_thru, after, name):
    def body(p_ref, land_ref, send_sems, recv_sems, after_ref, p_out, land_out):
        x, y, cc = _my_place()
        for j, (px, py, pc) in enumerate(_peers(x, y, cc)[1:]):
            cp = pltpu.make_async_remote_copy(
                src_ref=p_ref.at[2 * px + py], dst_ref=land_ref.at[j], send_sem=send_sems.at[j],
                recv_sem=recv_sems.at[j], device_id=(px, py, pc), device_id_type=MESH)
            cp.wait_send()
            cp.wait_recv()

    return _call(
        body, name=name,
        out_shape=(pltpu.HBM(p_thru.shape, p_thru.dtype), pltpu.HBM(land_thru.shape, land_thru.dtype)),
        in_specs=(HBM_SPEC, HBM_SPEC, SEM_SPEC, SEM_SPEC, pl.BlockSpec(memory_space=pl.ANY)),
        out_specs=(HBM_SPEC, HBM_SPEC), input_output_aliases={0: 0, 1: 1},
        compiler_params=pltpu.CompilerParams(has_side_effects=DATAFLOW),
    )(p_thru, land_thru, send_sems, recv_sems, after)


def sibling_start(gw, name):
    _, r, c = gw.shape
    nch, rows = _stream_rows(r)
    nsem = 4 * nch

    def body(g_ref, land_ref, send_sems, recv_sems, g_thru, land_thru, token):
        x, y, cc = _my_place()
        for q in range(4):
            for s in range(nch):
                win = pl.ds(s * rows, rows)
                pltpu.make_async_remote_copy(
                    src_ref=g_ref.at[q, 1 - cc, win], dst_ref=land_ref.at[q, win], send_sem=send_sems.at[q * nch + s],
                    recv_sem=recv_sems.at[q * nch + s], device_id=(x, y, 1 - cc), device_id_type=MESH).start()
        token[...] = jnp.zeros_like(token)

    return _call(
        body, name=name,
        out_shape=(pltpu.SemaphoreType.DMA((nsem,)), pltpu.SemaphoreType.DMA((nsem,)),
                   pltpu.HBM((4, 2, r, c), gw.dtype), pltpu.HBM((4, r, c), gw.dtype), _sds((8, 128))),
        in_specs=(HBM_SPEC, HBM_SPEC),
        out_specs=(SEM_SPEC, SEM_SPEC, HBM_SPEC, HBM_SPEC, pl.BlockSpec(memory_space=pltpu.VMEM)),
        input_output_aliases={0: 2, 1: 3},
        compiler_params=pltpu.CompilerParams(has_side_effects=DATAFLOW),
    )(_hbm(gw.reshape(4, 2, r, c)), _hbm(lax.empty((4, r, c), gw.dtype)))


def sibling_wait(send_sems, recv_sems, g_thru, land_thru, after, name):
    _, _, r, c = g_thru.shape
    nch, rows = _stream_rows(r)

    def body(g_ref, land_ref, send_sems, recv_sems, after_ref, g_out, land_out):
        x, y, cc = _my_place()
        for q in range(4):
            for s in range(nch):
                win = pl.ds(s * rows, rows)
                cp = pltpu.make_async_remote_copy(
                    src_ref=g_ref.at[q, 1 - cc, win], dst_ref=land_ref.at[q, win], send_sem=send_sems.at[q * nch + s],
                    recv_sem=recv_sems.at[q * nch + s], device_id=(x, y, 1 - cc), device_id_type=MESH)
                cp.wait_send()
                cp.wait_recv()

    return _call(
        body, name=name,
        out_shape=(pltpu.HBM(g_thru.shape, g_thru.dtype), pltpu.HBM(land_thru.shape, land_thru.dtype)),
        in_specs=(HBM_SPEC, HBM_SPEC, SEM_SPEC, SEM_SPEC, pl.BlockSpec(memory_space=pl.ANY)),
        out_specs=(HBM_SPEC, HBM_SPEC), input_output_aliases={0: 0, 1: 1},
        compiler_params=pltpu.CompilerParams(has_side_effects=DATAFLOW),
    )(g_thru, land_thru, send_sems, recv_sems, after)


def reduce_scatter_begin(gw, name, tag):
    _, r, c = gw.shape
    theirs = exchange_sibling(gw, name + "_sibling")
    chip_sum = add_own_half(gw.reshape(4, 2, r, c), theirs, name + "_add2")
    return chips_start(chip_sum, name + "_start" + tag)


def reduce_scatter_end(started, after, name, tag):
    send_sems, recv_sems, p_thru, land_thru, _ = started
    parts, land = chips_wait(send_sems, recv_sems, p_thru, land_thru, after, name + "_wait" + tag)
    mine = lax.dynamic_index_in_dim(parts, 2 * lax.axis_index("x") + lax.axis_index("y"), axis=0, keepdims=False)
    return add_own_lead(mine, land, name + "_add4")


def _pack(arrs):
    flat = jnp.concatenate([a.reshape(-1).astype(F32) for a in arrs])
    n = flat.shape[0]
    pad = (-n) % PACK_ELEMS
    return jnp.pad(flat, (0, pad)).reshape(-1, 128)


def _unpack(packed, shapes):
    flat = packed.reshape(-1)
    out, off = [], 0
    for s in shapes:
        n = int(np.prod(s))
        out.append(lax.optimization_barrier(flat[off:off + n]).reshape(s))
        off += n
    return out


def _t5_bucket(rel):
    nb = N_BUCKETS // 2
    max_exact = nb // 2
    ret = jnp.where(rel > 0, nb, 0)
    n = jnp.abs(rel)
    nf = jnp.maximum(n, 1).astype(jnp.float32)
    large = max_exact + (jnp.log(nf / max_exact) / math.log(MAX_DISTANCE / max_exact)
                         * (nb - max_exact)).astype(jnp.int32)
    large = jnp.minimum(large, nb - 1)
    return ret + jnp.where(n < max_exact, n, large)


SMALL_SHARDED = ("gla_w2_f", "gla_w2_b", "conv_w", "lru_ba", "lru_bx", "lru_lambda")
SMALL_REPL = ("rel_bias", "attn_sink", "gla_b2_f", "gla_b2_b", "gla_norm", "conv_b", "lru_wa", "lru_wx",
              "norm_mix_pre", "norm_mix_post", "norm_mem", "norm_x_pre", "norm_x_post", "norm_ff_pre", "norm_ff_post")
BIG = ("w_in", "w_out", "xq", "xk", "xv", "xo", "w_up", "w_down")
WEIGHTS = ['rel_bias', 'w_in', 'w_out', 'attn_sink', 'gla_w2_f', 'gla_b2_f', 'gla_w2_b', 'gla_b2_b', 'gla_norm',
           'conv_w', 'conv_b', 'lru_wa', 'lru_ba', 'lru_wx', 'lru_bx', 'lru_lambda', 'xq', 'xk', 'xv', 'xo', 'w_up',
           'w_down', 'norm_mix_pre', 'norm_mix_post', 'norm_mem', 'norm_x_pre', 'norm_x_post', 'norm_ff_pre',
           'norm_ff_post']


def _step(x, mem, loss_target, w, m, v):
    depth = w["w_in"].shape[0]
    t, d = x.shape[1], x.shape[2]
    ml = mem.shape[1]
    rx = d // N_DEV
    rf = w["w_up"].shape[2]
    r_out = D_MIX // N_DEV
    x = x.reshape(t, d)
    mem = mem.reshape(ml, d)
    loss_target = loss_target.reshape(t, d)
    my_idx = 4 * lax.axis_index("x") + 2 * lax.axis_index("y") + lax.axis_index("c")

    off_in = 0
    off_up, off_down, off_out = 0, rf, 2 * rf
    off_xq = off_out + r_out
    off_xk, off_xv, off_xo = off_xq + rx, off_xq + 2 * rx, off_xq + 3 * rx
    r_rest = off_xo + rx

    sh_shapes = [w[n].shape for n in SMALL_SHARDED]
    gathered = all_gather(_pack([w[n] for n in SMALL_SHARDED]), "ag_small")
    per_dev = [_unpack(gathered[j], sh_shapes) for j in range(N_DEV)]
    full = {n: jnp.concatenate([per_dev[j][i] for j in range(N_DEV)], axis=-1) for i, n in enumerate(SMALL_SHARDED)}
    for n in SMALL_REPL:
        full[n] = w[n]

    ag_started = []
    for l in range(depth):
        blk_in = jnp.pad(w["w_in"][l].T, ((0, W_IN_ROWS - W_IN_SHARD), (0, 0))).astype(BF16)
        blk_rest = jnp.concatenate([w["w_up"][l].T, w["w_down"][l], w["w_out"][l], w["xq"][l], w["xk"][l],
                                    w["xv"][l], w["xo"][l]], axis=0).astype(BF16)
        blk_in, _ = lax.optimization_barrier((blk_in, gathered if l == 0 else ag_started[-1][1][4]))
        start_in = gather_start(blk_in, "ag_start_in%d" % l)
        blk_rest, _ = lax.optimization_barrier((blk_rest, start_in[4]))
        ag_started.append((start_in, gather_start(blk_rest, "ag_start_rest%d" % l)))
    gather_token = sum(st[4][0, 0] for pair in ag_started for st in pair)
    gws = [None] * depth

    def gather_finish(started, after, name):
        send_sems, recv_sems, blk_thru, land_thru, _ = started
        _, land = gather_wait(send_sems, recv_sems, blk_thru, land_thru, after, name)
        return gather_pass(land, "ag_pass")

    qi = jnp.arange(BLOCK)[:, None]
    kj = jnp.arange(3 * BLOCK)[None, :]
    onehot_t = (jnp.arange(N_BUCKETS)[:, None] == _t5_bucket(kj - BLOCK - qi).reshape(1, -1)).astype(F32)
    bias = mm_plain(full["rel_bias"].T, onehot_t, "rel_bias_lookup", hi=True, tn=3 * BLOCK * 16)
    bias = bias.reshape(A_HEADS, BLOCK, 3 * BLOCK)
    bias_t = jnp.transpose(bias, (0, 2, 1))

    def sink_rows(sink):
        s = jnp.broadcast_to(sink.reshape(A_KV_HEADS, A_GROUP, 1), (A_KV_HEADS, A_GROUP, 128))
        return jnp.pad(s, ((0, 0), (0, 8 - A_GROUP), (0, 0)))

    bounds = np.concatenate([[0], np.cumsum(SPLIT_SIZES)])

    def split_proj(pp):
        outs = []
        for lo, hi in zip(bounds[:-1], bounds[1:]):
            segs = []
            for j in range(N_DEV):
                a, b = max(lo, j * W_IN_SHARD), min(hi, (j + 1) * W_IN_SHARD)
                if a < b:
                    base = j * W_IN_ROWS - j * W_IN_SHARD
                    segs.append(pp[:, base + a:base + b])
            outs.append(segs[0] if len(segs) == 1 else jnp.concatenate(segs, axis=1))
        return outs

    def join_dproj(pieces):
        zero_cols = jnp.zeros((t, W_IN_ROWS - W_IN_SHARD), F32)
        segs = []
        for j in range(N_DEV):
            for p, lo, hi in zip(pieces, bounds[:-1], bounds[1:]):
                a, b = max(lo, j * W_IN_SHARD), min(hi, (j + 1) * W_IN_SHARD)
                if a < b:
                    segs.append(p[:, a - lo:b - lo])
            segs.append(zero_cols)
        return jnp.concatenate(segs, axis=1).astype(BF16)

    def lead(a):
        return a.reshape(a.shape[0], C_WIDTH // 128, 128)

    saved = []
    h = rms_fwd(x, full["norm_mix_pre"][0] + gather_token, "rms_first")
    for l in range(depth):
        gw_in = gather_finish(ag_started[l][0], x, "ag_wait_in%d" % l)
        sv = {"x": x, "h_in": h}
        proj_pad = mm_wn(h, gw_in, off_in, W_IN_ROWS, "mm_w_in")
        aq, ak, av, bq, bk, bv, bg, zf, zb, cx, cy = split_proj(proj_pad)
        sv.update(aq=aq, ak=ak, av=av, bq=bq, bk=bk, bv=bv, bg=bg, zf=zf, zb=zb, cx=cx, cy=cy)
        sink_b = sink_rows(full["attn_sink"][l])
        passing = None
        if l >= EARLY_PASS_FROM_LAYER:
            send_sems, recv_sems, blk_thru, land_thru, _ = ag_started[l][1]
            _, land = gather_wait(send_sems, recv_sems, blk_thru, land_thru, proj_pad, "ag_wait_rest%d" % l)
            passing = pass_start(land, "ag_pass_start%d" % l)
            sink_b = sink_rows(full["attn_sink"][l] + passing[3][0, 0])
        oa = attn_fwd(aq, ak, av, bias, sink_b, "attn_fwd")
        la_f, la_b = gla_gates_fwd(zf, zb, full["gla_w2_f"][l], full["gla_b2_f"][l], full["gla_w2_b"][l],
                                   full["gla_b2_b"][l], "gla_gates_fwd")
        bk_t = bk.T
        gla = {}
        for nm, la, bdir in (("f", la_f, False), ("b", la_b, True)):
            la_t = la.T
            u, dec = gla_outer(bk_t, la_t, bv, "gla_outer_k_" + nm, bdir, "khat")
            sp = scan_lead(dec, u, "gla_state_scan_" + nm, reverse=bdir, inclusive=False)
            o_dir = gla_inner_fwd(bq, bk, bv, la, sp, "gla_inner_fwd_" + nm, bdir)
            gla[nm] = dict(la=la, la_t=la_t, dec=dec, sp=sp, o=o_dir)
        ob = gla_out_fwd(gla["f"]["o"], gla["b"]["o"], bg, full["gla_norm"][l], "gla_out_fwd")
        sv["gla"] = gla
        xm2, xm1, xp1 = _shift(cx, -2), _shift(cx, -1), _shift(cx, 1)
        xc, a0, u0, a1, u1 = lru_gates_fwd(cx, xm2, xm1, xp1, full["conv_w"][l], full["conv_b"][l], full["lru_wa"][l],
                                           full["lru_ba"][l], full["lru_wx"][l], full["lru_bx"][l],
                                           full["lru_lambda"][l], "lru_gates_fwd")
        h0 = scan_lead(lead(a0), lead(u0), "lru_scan_fwd", reverse=False, inclusive=True).reshape(t, C_WIDTH)
        h1 = scan_lead(lead(a1), lead(u1), "lru_scan_rev", reverse=True, inclusive=True).reshape(t, C_WIDTH)
        oc = lru_out_fwd(h0, h1, cy, "lru_out_fwd")
        sv.update(xm2=xm2, xm1=xm1, xp1=xp1, xc=xc, a0=a0, a1=a1, h0=h0, h1=h1, oa=oa)
        cat = jnp.concatenate([oa, ob, oc], axis=1).astype(BF16)
        if passing is None:
            gw = gather_finish(ag_started[l][1], cat, "ag_wait_rest%d" % l)
        else:
            gw = pass_wait(passing[0], passing[1], passing[2], cat, "ag_pass_wait%d" % l)
        gws[l] = (gw_in, gw)
        mixed = mm_wk(cat, gw, off_out, r_out, "mm_w_out")
        x1, h2 = resid_rms(x, mixed, full["norm_mix_post"][l], full["norm_x_pre"][l], "resid_rms")
        sv.update(cat=cat, mixed=mixed, x1=x1, h2=h2)
        memn = rms_fwd(mem, full["norm_mem"][l], "rms_mem")
        q = mm_wk(h2, gw, off_xq, rx, "mm_xq")
        k = mm_wk(memn, gw, off_xk, rx, "mm_xkv")
        vv = mm_wk(memn, gw, off_xv, rx, "mm_xkv")
        ox = xattn_fwd(q, k, vv, "xattn_fwd")
        ca = mm_wk(ox, gw, off_xo, rx, "mm_xo")
        x2, h3 = resid_rms(x1, ca, full["norm_x_post"][l], full["norm_ff_pre"][l], "resid_rms")
        sv.update(memn=memn, q=q, k=k, v=vv, ox=ox, ca=ca, x2=x2, h3=h3)
        up, act = mm_wn(h3, gw, off_up, rf, "mm_w_up", with_relu2=True)
        ff = mm_wk(act, gw, off_down, rf, "mm_w_down", jb=max(1, min(N_DEV, 2048 // rf)))
        if l + 1 < depth:
            x, h = resid_rms(x2, ff, full["norm_ff_post"][l], full["norm_mix_pre"][l + 1], "resid_rms")
        else:
            x = resid_rms(x2, ff, full["norm_ff_post"][l], None, "resid_rms_last")
        sv.update(up=up, act=act, ff=ff)
        saved.append(sv)

    dx, loss_local = loss_and_grad(x, loss_target, "loss")
    loss = lax.psum(loss_local, AXES)

    grads = {n: [None] * depth for n in WEIGHTS if n != "rel_bias"}
    dbias_total = None
    big_grads = [None] * depth
    rs_started = [None] * depth
    rs_token = 0.0
    bf = lambda a: a.astype(BF16)
    for l in reversed(range(depth)):
        gw_in, gw = gws[l]
        sv = saved[l]
        dff, grads["norm_ff_post"][l] = rms_bwd(sv["ff"], full["norm_ff_post"][l] + rs_token, dx, "rms_bwd",
                                                matmul_operand=True)
        dup = mm_wn(dff, gw, off_down, rf, "mm_w_down_dx", relu_grad_of=sv["up"], out_dtype=BF16)
        gpack = mm_dw_into(sv["act"], dff, lax.empty((N_DEV, r_rest, d), BF16), off_down, rf, "mm_dw_down")
        gpack = mm_dw_into(dup, sv["h3"], gpack, off_up, rf, "mm_dw_up")
        dh3 = mm_wk(dup, gw, off_up, rf, "mm_w_up_dx", jb=max(1, min(N_DEV, 2048 // rf)))
        dx2, grads["norm_ff_pre"][l] = rms_bwd(sv["x2"], full["norm_ff_pre"][l], dh3, "rms_bwd_add", add=dx)
        dca, grads["norm_x_post"][l] = rms_bwd(sv["ca"], full["norm_x_post"][l], dx2, "rms_bwd", matmul_operand=True)
        dox = mm_wn(dca, gw, off_xo, rx, "mm_x_dx")
        gpack = mm_dw_into(sv["ox"], dca, gpack, off_xo, rx, "mm_dw_xo")
        dq, dk, dv = xattn_bwd(sv["q"], sv["k"], sv["v"], sv["ox"], dox, "xattn_bwd")
        gpack = mm_dw_into(sv["h2"], dq, gpack, off_xq, rx, "mm_dw_xq")
        gpack = mm_dw_into(sv["memn"], dk, gpack, off_xk, rx, "mm_dw_xk")
        gpack = mm_dw_into(sv["memn"], dv, gpack, off_xv, rx, "mm_dw_xv")
        dh2 = mm_wn(dq, gw, off_xq, rx, "mm_x_dx")
        dmem_k = mm_wn(dk, gw, off_xk, rx, "mm_x_dx_mem")
        dmem_v = mm_wn(dv, gw, off_xv, rx, "mm_x_dx_mem")
        _, grads["norm_mem"][l] = rms_bwd(mem, full["norm_mem"][l], dmem_k, "rms_bwd_mem", dy2=dmem_v)
        dx1, grads["norm_x_pre"][l] = rms_bwd(sv["x1"], full["norm_x_pre"][l], dh2, "rms_bwd_add", add=dx2)
        dmixed, grads["norm_mix_post"][l] = rms_bwd(sv["mixed"], full["norm_mix_post"][l], dx1, "rms_bwd",
                                                   matmul_operand=True)
        dcat = mm_wn(dmixed, gw, off_out, r_out, "mm_w_out_dx")
        gpack = mm_dw_into(sv["cat"], dmixed, gpack, off_out, r_out, "mm_dw_out")
        if l == 0:
            rs_started[l] = [reduce_scatter_begin(gpack, "rs_rest", str(l)), None]
            mix_token = rs_started[l][0][4][0, 0]
        else:
            sib = sibling_start(gpack, "rs_rest_sib_start%d" % l)
            mix_token = sib[4][0, 0]
        doa, dob, doc = dcat[:, :A_Q], dcat[:, A_Q:A_Q + B_V], dcat[:, A_Q + B_V:]
        daq, dak, dav, dbias, dsink = attn_bwd(sv["aq"], sv["ak"], sv["av"], bias, bias_t,
                                               sink_rows(full["attn_sink"][l] + mix_token), doa, sv["oa"], "attn_bwd")
        grads["attn_sink"][l] = dsink[:, :A_GROUP, 0].reshape(A_HEADS)
        dbias_total = dbias if dbias_total is None else add_n([dbias_total, dbias], F32, "add_dbias")
        gf, gb = sv["gla"]["f"], sv["gla"]["b"]
        do_gla, dbg, dgn = gla_out_bwd(gf["o"], gb["o"], sv["bg"], full["gla_norm"][l] + mix_token, dob,
                                       "gla_out_bwd")
        grads["gla_norm"][l] = dgn.reshape(B_V)
        bq_t = sv["bq"].T
        acc = None
        dlas = {}
        for nm, gd, bdir in (("f", gf, False), ("b", gb, True)):
            wq = gla_outer(bq_t, gd["la_t"], do_gla, "gla_outer_q_" + nm, bdir, "qtil")
            gs = scan_lead(gd["dec"], wq, "gla_adj_scan_" + nm, reverse=not bdir, inclusive=False)
            dbq, dbk, dbv, dlas[nm] = gla_inner_bwd(sv["bq"], sv["bk"], sv["bv"], gd["la"], do_gla, gd["sp"], gs,
                                                    gd["dec"], "gla_inner_bwd_" + nm, bdir, add=acc)
            acc = (dbq, dbk, dbv)
        dzf, dzb, dpre_f, dpre_b, db2f, db2b = gla_gates_bwd(
            sv["zf"], sv["zb"], full["gla_w2_f"][l], full["gla_b2_f"][l], full["gla_w2_b"][l], full["gla_b2_b"][l],
            dlas["f"], dlas["b"], "gla_gates_bwd")
        grads["gla_b2_f"][l] = db2f.reshape(B_QK)
        grads["gla_b2_b"][l] = db2b.reshape(B_QK)
        grads["gla_w2_f"][l] = mm_plain(sv["zf"].T, dpre_f, "mm_dw_gate", hi=True)
        grads["gla_w2_b"][l] = mm_plain(sv["zb"].T, dpre_b, "mm_dw_gate", hi=True)
        dh, dcy = lru_out_bwd(sv["h0"], sv["h1"], sv["cy"], doc, "lru_out_bwd")
        g0 = scan_lead(lead(_shift(sv["a0"], 1)), lead(dh), "lru_scan_rev", reverse=True,
                       inclusive=True).reshape(t, C_WIDTH)
        g1 = scan_lead(lead(_shift(sv["a1"], -1)), lead(dh), "lru_scan_fwd", reverse=False,
                       inclusive=True).reshape(t, C_WIDTH)
        dxc, dzr0, dzi0, dzr1, dzi1, dlam, dba, dbx = lru_gates_bwd(
            sv["xc"], g0, _shift(sv["h0"], -1), g1, _shift(sv["h1"], 1), full["lru_wa"][l], full["lru_ba"][l],
            full["lru_wx"][l], full["lru_bx"][l], full["lru_lambda"][l], "lru_gates_bwd")
        xc_t = bf(sv["xc"].T)
        grads["lru_wa"][l] = jnp.stack([blockdiag_dw(xc_t, dzr0, "lru_dw"), blockdiag_dw(xc_t, dzr1, "lru_dw")])
        grads["lru_wx"][l] = jnp.stack([blockdiag_dw(xc_t, dzi0, "lru_dw"), blockdiag_dw(xc_t, dzi1, "lru_dw")])
        grads["lru_lambda"][l], grads["lru_ba"][l], grads["lru_bx"][l] = dlam, dba, dbx
        dcx, dcw, dcb = conv_bwd(dxc, _shift(dxc, 2), _shift(dxc, 1), _shift(dxc, -1), sv["cx"], sv["xm2"],
                                 sv["xm1"], sv["xp1"], full["conv_w"][l], "conv_bwd")
        grads["conv_w"][l] = dcw
        grads["conv_b"][l] = dcb.reshape(C_WIDTH)
        dproj_pad = join_dproj([daq, dak, dav, dbq, dbk, dbv, dbg, dzf, dzb, dcx, dcy])
        g_in_t = mm_plain(dproj_pad, sv["h_in"], "mm_dw_in", ta=True, out_dtype=BF16)
        rs_in = reduce_scatter_begin(g_in_t.reshape(N_DEV, W_IN_ROWS, d), "rs_in", str(l))
        rs_token = rs_in[4][0, 0]
        dh1 = mm_wk(dproj_pad, gw_in, off_in, W_IN_ROWS, "mm_w_in_dx", jb=2)
        dx, grads["norm_mix_pre"][l] = rms_bwd(sv["x"], full["norm_mix_pre"][l] + rs_token, dh1, "rms_bwd_add",
                                               add=dx1)
        if l > 0:
            g5, theirs = sibling_wait(sib[0], sib[1], sib[2], sib[3], dx, "rs_rest_sib_wait%d" % l)
            chip_sum = add_own_half(g5, theirs, "rs_rest_add2")
            rs_started[l] = [chips_start(chip_sum, "rs_rest_start%d" % l), rs_in]
            rs_token = rs_token + rs_started[l][0][4][0, 0]
        else:
            rs_started[l][1] = rs_in

    grad_rel = mm_plain(dbias_total.reshape(A_HEADS, -1), onehot_t, "rel_bias_grad", tb=True, hi=True).T

    small_names = [n for n in WEIGHTS if n not in BIG]
    small_g = {"rel_bias": grad_rel}
    for n in small_names:
        if n != "rel_bias":
            small_g[n] = jnp.stack([g.reshape(full[n].shape[1:]) for g in grads[n]])
    shapes = [small_g[n].shape for n in small_names]
    small_started = gather_start(_pack([small_g[n] for n in small_names]), "ag_start_small_grads")
    for l in range(depth):
        big_grads[l] = (reduce_scatter_end(rs_started[l][1], small_started[4], "rs_in", str(l)),
                        reduce_scatter_end(rs_started[l][0], small_started[4], "rs_rest", str(l)))

    grad_out, delta, new_m, new_v = {}, {}, {}, {}

    def rows(l, off, r):
        return big_grads[l][1][off:off + r]

    big_g = {
        "w_in": jnp.stack([big_grads[l][0][:W_IN_SHARD].T for l in range(depth)]),
        "w_out": jnp.stack([rows(l, off_out, r_out) for l in range(depth)]),
        "xq": jnp.stack([rows(l, off_xq, rx) for l in range(depth)]),
        "xk": jnp.stack([rows(l, off_xk, rx) for l in range(depth)]),
        "xv": jnp.stack([rows(l, off_xv, rx) for l in range(depth)]),
        "xo": jnp.stack([rows(l, off_xo, rx) for l in range(depth)]),
        "w_up": jnp.stack([rows(l, off_up, rf).T for l in range(depth)]),
        "w_down": jnp.stack([rows(l, off_down, rf) for l in range(depth)]),
    }
    for n in BIG:
        grad_out[n] = big_g[n]
        delta[n], new_m[n], new_v[n] = adamw(big_g[n], w[n], m[n], v[n], "adamw_" + n)

    packed = gather_finish(small_started, delta["w_down"], "ag_wait_small_grads")
    summed = sum_lead(packed, tuple(range(N_DEV)), F32, "add8_small")
    small_g = dict(zip(small_names, _unpack(summed, shapes)))
    for n in SMALL_SHARDED:
        wdt = w[n].shape[-1]
        small_g[n] = lax.dynamic_slice_in_dim(small_g[n], my_idx * wdt, wdt, axis=small_g[n].ndim - 1)

    direct = ("lru_wa", "lru_wx")
    packed_names = [n for n in small_names if n not in direct]
    sshapes = [w[n].shape for n in packed_names]
    ds, ms, vs = adamw(_pack([small_g[n] for n in packed_names]), _pack([w[n] for n in packed_names]),
                       _pack([m[n] for n in packed_names]), _pack([v[n] for n in packed_names]), "adamw_small")
    for n, d_, m_, v_ in zip(packed_names, _unpack(ds, sshapes), _unpack(ms, sshapes), _unpack(vs, sshapes)):
        grad_out[n], delta[n], new_m[n], new_v[n] = small_g[n], d_, m_, v_
    for n in direct:
        grad_out[n] = small_g[n]
        delta[n], new_m[n], new_v[n] = adamw(small_g[n], w[n], m[n], v[n], "adamw_lru")

    return (loss, dx.reshape(1, t, d), *[grad_out[n] for n in WEIGHTS], *[delta[n] for n in WEIGHTS],
            *[new_m[n] for n in WEIGHTS], *[new_v[n] for n in WEIGHTS])


def kernel(x, mem, rel_bias, w_in, w_out, attn_sink, gla_w2_f, gla_b2_f, gla_w2_b, gla_b2_b, gla_norm, conv_w, conv_b, lru_wa, lru_ba, lru_wx, lru_bx, lru_lambda, xq, xk, xv, xo, w_up, w_down, norm_mix_pre, norm_mix_post, norm_mem, norm_x_pre, norm_x_post, norm_ff_pre, norm_ff_post, loss_target, m_rel_bias, m_w_in, m_w_out, m_attn_sink, m_gla_w2_f, m_gla_b2_f, m_gla_w2_b, m_gla_b2_b, m_gla_norm, m_conv_w, m_conv_b, m_lru_wa, m_lru_ba, m_lru_wx, m_lru_bx, m_lru_lambda, m_xq, m_xk, m_xv, m_xo, m_w_up, m_w_down, m_norm_mix_pre, m_norm_mix_post, m_norm_mem, m_norm_x_pre, m_norm_x_post, m_norm_ff_pre, m_norm_ff_post, v_rel_bias, v_w_in, v_w_out, v_attn_sink, v_gla_w2_f, v_gla_b2_f, v_gla_w2_b, v_gla_b2_b, v_gla_norm, v_conv_w, v_conv_b, v_lru_wa, v_lru_ba, v_lru_wx, v_lru_bx, v_lru_lambda, v_xq, v_xk, v_xv, v_xo, v_w_up, v_w_down, v_norm_mix_pre, v_norm_mix_post, v_norm_mem, v_norm_x_pre, v_norm_x_post, v_norm_ff_pre, v_norm_ff_post):
    given = dict(locals())
    w = {n: given[n] for n in WEIGHTS}
    m = {n: given["m_" + n] for n in WEIGHTS}
    v = {n: given["v_" + n] for n in WEIGHTS}
    return _step(x, mem, loss_target, w, m, v)
```
